```python
import math
import jax
import jax.numpy as jnp
from jax import lax
import numpy as np

D_MODEL = 1024
BATCH = 8
SEQ = 8192
DEPTH = 1

GRID_W = 64
D_MIX = 1024
EPS = 1e-6

HG_HEADS = 4
HG_DK = 128
HG_DV = 128
HG_KW = HG_HEADS * HG_DK
HG_VW = HG_HEADS * HG_DV
CHUNK = 64

ATT_HEADS = 8
ATT_KV_HEADS = 2
ATT_DH = 64
ATT_GROUP = ATT_HEADS // ATT_KV_HEADS
ATT_QW = ATT_HEADS * ATT_DH
ATT_KVW = ATT_KV_HEADS * ATT_DH
ROPE_THETA = 10000.0
Q_BLOCK = 128

D_IN = 2 * HG_KW + HG_KW + 2 * HG_VW + ATT_QW + 2 * ATT_KVW

D_FF = -(-8 * D_MODEL // (3 * 256)) * 256

kernel_name = "hybrid_hgrn2_axial_gqa_encoder"


def _rmsnorm(x, w):
    xf = x.astype(jnp.float32)
    y = xf * lax.rsqrt(jnp.mean(xf * xf, axis=-1, keepdims=True) + EPS)
    return (y * w.astype(jnp.float32)).astype(x.dtype)


def _gla_chunk_scan(q, k, v, log_f):
    B, L, H, DK = q.shape
    DV = v.shape[-1]
    n = L // CHUNK

    def to_chunks(a):
        return a.reshape(B, n, CHUNK, H, a.shape[-1]).transpose(1, 0, 3, 2, 4)

    qc, kc, vc, gc = to_chunks(q), to_chunks(k), to_chunks(v), to_chunks(log_f)
    b = jnp.cumsum(gc, axis=3)
    b_ref = b[:, :, :, CHUNK // 2 - 1:CHUNK // 2, :]
    b_last = b[:, :, :, -1:, :]
    q_in = qc * jnp.exp(b - b_ref)
    k_in = kc * jnp.exp(b_ref - b)
    scores = jnp.einsum('nbhtk,nbhsk->nbhts', q_in, k_in)
    causal_in_scan = jnp.tril(jnp.ones((CHUNK, CHUNK), dtype=bool))
    scores = jnp.where(causal_in_scan, scores, 0.0)
    o_intra = jnp.einsum('nbhts,nbhsv->nbhtv', scores, vc)
    contrib = jnp.einsum('nbhsk,nbhsv->nbhkv', kc * jnp.exp(b_last - b), vc).astype(jnp.float32)
    decay = jnp.exp(b_last[:, :, :, 0, :]).astype(jnp.float32)

    def step(S, inp):
        d, c = inp
        return d[..., None] * S + c, S

    S0 = jnp.zeros((B, H, DK, DV), jnp.float32)
    _, S_prev = lax.scan(step, S0, (decay, contrib))
    o_inter = jnp.einsum('nbhtk,nbhkv->nbhtv', qc * jnp.exp(b), S_prev)
    o = (o_intra + o_inter).transpose(1, 0, 3, 2, 4).reshape(B, L, H, DV)
    return o.astype(v.dtype)


def _hgrn2_group(u_q, u_ff, u_fb, u_i, u_g, lb_fwd, lb_bwd, norm_w):
    B, L, _ = u_q.shape
    q = jax.nn.silu(u_q).reshape(B, L, HG_HEADS, HG_DK)
    i = u_i.reshape(B, L, HG_HEADS, HG_DV)

    def gates(z, lb):
        zf = z.astype(jnp.float32)
        f = lb + (1.0 - lb) * jax.nn.sigmoid(zf)
        k = (1.0 - lb) * jax.nn.sigmoid(-zf)
        return (jnp.log(f).reshape(B, L, HG_HEADS, HG_DK),
                k.reshape(B, L, HG_HEADS, HG_DK).astype(z.dtype))

    logf_f, k_f = gates(u_ff, lb_fwd)
    logf_b, k_b = gates(u_fb, lb_bwd)
    o_fwd = _gla_chunk_scan(q, k_f, i, logf_f)
    flip = lambda a: jnp.flip(a, axis=1)
    o_bwd = flip(_gla_chunk_scan(flip(q), flip(k_b), flip(i), flip(logf_b)))
    o = _rmsnorm(o_fwd + o_bwd, norm_w)
    o = o * jax.nn.silu(u_g.reshape(B, L, HG_HEADS, HG_DV))
    return o.reshape(B, L, HG_VW)


def _axial_rope_tables(L):
    rows = L // GRID_W
    row = jnp.repeat(jnp.arange(rows), GRID_W).astype(jnp.float32)
    col = jnp.tile(jnp.arange(GRID_W), rows).astype(jnp.float32)
    axis_dim = ATT_DH // 2
    freqs = ROPE_THETA ** (-jnp.arange(0, axis_dim, 2, dtype=jnp.float32) / axis_dim)
    ang = jnp.concatenate([row[:, None] * freqs, col[:, None] * freqs], axis=-1)
    return jnp.cos(ang), jnp.sin(ang)


def _apply_rope(x, cos, sin):
    xf = x.astype(jnp.float32)
    x1, x2 = xf[..., 0::2], xf[..., 1::2]
    c, s = cos[None, :, None, :], sin[None, :, None, :]
    out = jnp.stack([x1 * c - x2 * s, x1 * s + x2 * c], axis=-1).reshape(x.shape)
    return out.astype(x.dtype)


def _block_attention(q, k, v):
    B, L, _, dh = q.shape
    nq = L // Q_BLOCK
    qb = q.reshape(B, nq, Q_BLOCK, ATT_KV_HEADS, ATT_GROUP, dh).transpose(1, 0, 3, 4, 2, 5)
    kt = k.transpose(0, 2, 1, 3)
    vt = v.transpose(0, 2, 1, 3)
    scale = dh ** -0.5

    def one_block(qblk):
        s = jnp.einsum('bhgqd,bhkd->bhgqk', qblk, kt).astype(jnp.float32) * scale
        p = jax.nn.softmax(s, axis=-1)
        return jnp.einsum('bhgqk,bhkd->bhgqd', p.astype(vt.dtype), vt)

    o = lax.map(one_block, qb)
    return o.transpose(1, 0, 4, 2, 3, 5).reshape(B, L, ATT_QW)


def _attention_group(u_q, u_k, u_v, q_norm_w, k_norm_w, out_norm_w):
    B, L, _ = u_q.shape
    q = _rmsnorm(u_q.reshape(B, L, ATT_HEADS, ATT_DH), q_norm_w)
    k = _rmsnorm(u_k.reshape(B, L, ATT_KV_HEADS, ATT_DH), k_norm_w)
    v = u_v.reshape(B, L, ATT_KV_HEADS, ATT_DH)
    cos, sin = _axial_rope_tables(L)
    q, k = _apply_rope(q, cos, sin), _apply_rope(k, cos, sin)
    o = _block_attention(q, k, v)
    return _rmsnorm(o, out_norm_w)


def _fwd_setup_inputs(seed: int = 0) -> dict:
    key = jax.random.key(seed)
    ks = jax.random.split(key, 16)
    f32 = jnp.float32
    gain = lambda k, shape: 1.0 + 0.02 * jax.random.normal(k, shape, f32)
    return {
        "x": jax.random.normal(ks[0], (BATCH, SEQ, D_MODEL), f32),
        "norm1_w": gain(ks[1], (DEPTH, D_MODEL)),
        "w_in": jax.random.normal(ks[2], (DEPTH, D_MODEL, D_IN), f32) * D_MODEL ** -0.5,
        "lb_logits": 0.5 * jax.random.normal(ks[3], (2, DEPTH + 1, HG_KW), f32),
        "hg_norm_w": gain(ks[4], (DEPTH, HG_DV)),
        "q_norm_w": gain(ks[5], (DEPTH, ATT_DH)),
        "k_norm_w": gain(ks[6], (DEPTH, ATT_DH)),
        "att_norm_w": gain(ks[7], (DEPTH, ATT_QW)),
        "w_out": jax.random.normal(ks[8], (DEPTH, D_MIX, D_MODEL), f32) * D_MIX ** -0.5,
        "norm2_w": gain(ks[9], (DEPTH, D_MODEL)),
        "w_gate_up": jax.random.normal(ks[10], (DEPTH, D_MODEL, 2 * D_FF), f32) * D_MODEL ** -0.5,
        "w_down": jax.random.normal(ks[11], (DEPTH, D_FF, D_MODEL), f32) * D_FF ** -0.5,
        "final_norm_w": gain(ks[12], (D_MODEL,)),
    }


def _fwd_reference(x, norm1_w, w_in, lb_logits, hg_norm_w, q_norm_w, k_norm_w, att_norm_w,
              w_out, norm2_w, w_gate_up, w_down, final_norm_w):
    lb_all = jnp.cumsum(jax.nn.softmax(lb_logits.astype(jnp.float32), axis=1), axis=1)
    splits = np.cumsum([HG_KW, HG_KW, HG_KW, HG_VW, HG_VW, ATT_QW, ATT_KVW]).tolist()
    for l in range(DEPTH):
        h = _rmsnorm(x, norm1_w[l])
        u = jnp.einsum('bld,de->ble', h, w_in[l])
        u_q, u_ff, u_fb, u_i, u_g, a_q, a_k, a_v = jnp.split(u, splits, axis=-1)
        o_hg = _hgrn2_group(u_q, u_ff, u_fb, u_i, u_g, lb_all[0, l], lb_all[1, l], hg_norm_w[l])
        o_att = _attention_group(a_q, a_k, a_v, q_norm_w[l], k_norm_w[l], att_norm_w[l])
        mix = jnp.concatenate([o_hg, o_att], axis=-1)
        x = x + jnp.einsum('ble,ed->bld', mix, w_out[l])
        h2 = _rmsnorm(x, norm2_w[l])
        gate, up = jnp.split(jnp.einsum('bld,df->blf', h2, w_gate_up[l]), 2, axis=-1)
        x = x + jnp.einsum('blf,fd->bld', jax.nn.silu(gate) * up, w_down[l])
    return _rmsnorm(x, final_norm_w)


import jax as _jax
import jax.numpy as _jnp

TWIN_FORMAT = 'train_step'
FWD_PARAMS = ['x', 'norm1_w', 'w_in', 'lb_logits', 'hg_norm_w', 'q_norm_w', 'k_norm_w', 'att_norm_w', 'w_out', 'norm2_w', 'w_gate_up', 'w_down', 'final_norm_w']
TWIN_WEIGHTS = ['norm1_w', 'w_in', 'lb_logits', 'hg_norm_w', 'q_norm_w', 'k_norm_w', 'att_norm_w', 'w_out', 'norm2_w', 'w_gate_up', 'w_down', 'final_norm_w']
TWIN_DIFF_INPUT = 'x'
TWIN_INPUTS = ['x', 'norm1_w', 'w_in', 'lb_logits', 'hg_norm_w', 'q_norm_w', 'k_norm_w', 'att_norm_w', 'w_out', 'norm2_w', 'w_gate_up', 'w_down', 'final_norm_w', 'loss_target', 'm_norm1_w', 'm_w_in', 'm_lb_logits', 'm_hg_norm_w', 'm_q_norm_w', 'm_k_norm_w', 'm_att_norm_w', 'm_w_out', 'm_norm2_w', 'm_w_gate_up', 'm_w_down', 'm_final_norm_w', 'v_norm1_w', 'v_w_in', 'v_lb_logits', 'v_hg_norm_w', 'v_q_norm_w', 'v_k_norm_w', 'v_att_norm_w', 'v_w_out', 'v_norm2_w', 'v_w_gate_up', 'v_w_down', 'v_final_norm_w']
TWIN_OUTPUTS = ['loss', 'grad_x', 'grad_norm1_w', 'grad_w_in', 'grad_lb_logits', 'grad_hg_norm_w', 'grad_q_norm_w', 'grad_k_norm_w', 'grad_att_norm_w', 'grad_w_out', 'grad_norm2_w', 'grad_w_gate_up', 'grad_w_down', 'grad_final_norm_w', 'delta_norm1_w', 'delta_w_in', 'delta_lb_logits', 'delta_hg_norm_w', 'delta_q_norm_w', 'delta_k_norm_w', 'delta_att_norm_w', 'delta_w_out', 'delta_norm2_w', 'delta_w_gate_up', 'delta_w_down', 'delta_final_norm_w', 'new_m_norm1_w', 'new_m_w_in', 'new_m_lb_logits', 'new_m_hg_norm_w', 'new_m_q_norm_w', 'new_m_k_norm_w', 'new_m_att_norm_w', 'new_m_w_out', 'new_m_norm2_w', 'new_m_w_gate_up', 'new_m_w_down', 'new_m_final_norm_w', 'new_v_norm1_w', 'new_v_w_in', 'new_v_lb_logits', 'new_v_hg_norm_w', 'new_v_q_norm_w', 'new_v_k_norm_w', 'new_v_att_norm_w', 'new_v_w_out', 'new_v_norm2_w', 'new_v_w_gate_up', 'new_v_w_down', 'new_v_final_norm_w']
TWIN_LEAF_KINDS = {'loss': 'loss', 'grad_x': 'grad_x', 'grad_norm1_w': 'grad_w', 'grad_w_in': 'grad_w', 'grad_lb_logits': 'grad_w', 'grad_hg_norm_w': 'grad_w', 'grad_q_norm_w': 'grad_w', 'grad_k_norm_w': 'grad_w', 'grad_att_norm_w': 'grad_w', 'grad_w_out': 'grad_w', 'grad_norm2_w': 'grad_w', 'grad_w_gate_up': 'grad_w', 'grad_w_down': 'grad_w', 'grad_final_norm_w': 'grad_w', 'delta_norm1_w': 'delta_w', 'delta_w_in': 'delta_w', 'delta_lb_logits': 'delta_w', 'delta_hg_norm_w': 'delta_w', 'delta_q_norm_w': 'delta_w', 'delta_k_norm_w': 'delta_w', 'delta_att_norm_w': 'delta_w', 'delta_w_out': 'delta_w', 'delta_norm2_w': 'delta_w', 'delta_w_gate_up': 'delta_w', 'delta_w_down': 'delta_w', 'delta_final_norm_w': 'delta_w', 'new_m_norm1_w': 'new_m', 'new_m_w_in': 'new_m', 'new_m_lb_logits': 'new_m', 'new_m_hg_norm_w': 'new_m', 'new_m_q_norm_w': 'new_m', 'new_m_k_norm_w': 'new_m', 'new_m_att_norm_w': 'new_m', 'new_m_w_out': 'new_m', 'new_m_norm2_w': 'new_m', 'new_m_w_gate_up': 'new_m', 'new_m_w_down': 'new_m', 'new_m_final_norm_w': 'new_m', 'new_v_norm1_w': 'new_v', 'new_v_w_in': 'new_v', 'new_v_lb_logits': 'new_v', 'new_v_hg_norm_w': 'new_v', 'new_v_q_norm_w': 'new_v', 'new_v_k_norm_w': 'new_v', 'new_v_att_norm_w': 'new_v', 'new_v_w_out': 'new_v', 'new_v_norm2_w': 'new_v', 'new_v_w_gate_up': 'new_v', 'new_v_w_down': 'new_v', 'new_v_final_norm_w': 'new_v'}


def _forward(args):
    return _fwd_reference(*[args[k] for k in FWD_PARAMS])


def _output_shape():
    def fwd():
        inp = _fwd_setup_inputs(0)
        return _fwd_reference(*[inp[k] for k in FWD_PARAMS])
    out = _jax.eval_shape(fwd)
    return out.shape, out.dtype

N_MICROBATCH = 1
ADAM_LR = 0.001
ADAM_B1 = 0.9
ADAM_B2 = 0.999
ADAM_EPS = 1e-08
ADAM_WD = 0.01
ADAM_STEP = 10
PER_EXAMPLE_BATCH_AXIS = {'x': 0, 'loss_target': 0}
SHARED_INPUTS = []
_WEIGHT_DTYPES = {'norm1_w': _jnp.float32, 'w_in': _jnp.float32, 'lb_logits': _jnp.float32, 'hg_norm_w': _jnp.float32, 'q_norm_w': _jnp.float32, 'k_norm_w': _jnp.float32, 'att_norm_w': _jnp.float32, 'w_out': _jnp.float32, 'norm2_w': _jnp.float32, 'w_gate_up': _jnp.float32, 'w_down': _jnp.float32, 'final_norm_w': _jnp.float32}
MOMENT_SCALE = {'norm1_w': 2.943459e-01, 'w_in': 1.614034e-01, 'lb_logits': 6.398348e-03, 'hg_norm_w': 2.482544e-01, 'q_norm_w': 6.148160e-01, 'k_norm_w': 5.924796e-01, 'att_norm_w': 2.439325e-01, 'w_out': 1.780079e-01, 'norm2_w': 1.520219e-01, 'w_gate_up': 6.435699e-02, 'w_down': 1.049891e-01, 'final_norm_w': 6.411296e+01}


def _to_microbatches(a, axis):
    t = _jnp.moveaxis(a, axis, 0)
    t = t.reshape((N_MICROBATCH, t.shape[0] // N_MICROBATCH) + t.shape[1:])
    return _jnp.moveaxis(t, 1, axis + 1)


def setup_inputs(seed: int = 0) -> dict:
    inp = _fwd_setup_inputs(seed)
    key = _jax.random.fold_in(_jax.random.key(seed), 7919)
    shape, _ = _output_shape()
    out = dict(inp)
    out["loss_target"] = _jax.random.normal(_jax.random.fold_in(key, 0), shape, _jnp.float32)
    for i, name in enumerate(TWIN_WEIGHTS):
        w = inp[name].astype(_jnp.float32)
        if MOMENT_SCALE is None:
            s = _jnp.sqrt(_jnp.mean(_jnp.square(w)) + 1e-30)
        else:
            s = MOMENT_SCALE[name]
        km, kv = _jax.random.split(_jax.random.fold_in(key, i + 1))
        out[name] = w
        out["m_" + name] = s * _jax.random.normal(km, w.shape, _jnp.float32)
        out["v_" + name] = (s * s) * _jax.random.uniform(kv, w.shape, _jnp.float32, 0.5, 1.5)
    if N_MICROBATCH > 1:
        for name, axis in PER_EXAMPLE_BATCH_AXIS.items():
            out[name] = _to_microbatches(out[name], axis)
    return {'x': out['x'], 'norm1_w': out['norm1_w'], 'w_in': out['w_in'], 'lb_logits': out['lb_logits'], 'hg_norm_w': out['hg_norm_w'], 'q_norm_w': out['q_norm_w'], 'k_norm_w': out['k_norm_w'], 'att_norm_w': out['att_norm_w'], 'w_out': out['w_out'], 'norm2_w': out['norm2_w'], 'w_gate_up': out['w_gate_up'], 'w_down': out['w_down'], 'final_norm_w': out['final_norm_w'], 'loss_target': out['loss_target'], 'm_norm1_w': out['m_norm1_w'], 'm_w_in': out['m_w_in'], 'm_lb_logits': out['m_lb_logits'], 'm_hg_norm_w': out['m_hg_norm_w'], 'm_q_norm_w': out['m_q_norm_w'], 'm_k_norm_w': out['m_k_norm_w'], 'm_att_norm_w': out['m_att_norm_w'], 'm_w_out': out['m_w_out'], 'm_norm2_w': out['m_norm2_w'], 'm_w_gate_up': out['m_w_gate_up'], 'm_w_down': out['m_w_down'], 'm_final_norm_w': out['m_final_norm_w'], 'v_norm1_w': out['v_norm1_w'], 'v_w_in': out['v_w_in'], 'v_lb_logits': out['v_lb_logits'], 'v_hg_norm_w': out['v_hg_norm_w'], 'v_q_norm_w': out['v_q_norm_w'], 'v_k_norm_w': out['v_k_norm_w'], 'v_att_norm_w': out['v_att_norm_w'], 'v_w_out': out['v_w_out'], 'v_norm2_w': out['v_norm2_w'], 'v_w_gate_up': out['v_w_gate_up'], 'v_w_down': out['v_w_down'], 'v_final_norm_w': out['v_final_norm_w']}


def _loss(weights, diff, rest, loss_target):
    with _jax.named_scope("forward"):
        args = {**rest, TWIN_DIFF_INPUT: diff, **{k: w.astype(_WEIGHT_DTYPES[k]) for k, w in weights.items()}}
        y = _forward(args)
    with _jax.named_scope("loss_head"):
        err = _jnp.square(y.astype(_jnp.float32) - loss_target)
        return 0.5 * _jnp.sum(_jnp.mean(err, axis=-1)) if err.ndim else 0.5 * err


def _adamw(w, g, m, v):
    m = ADAM_B1 * m + (1.0 - ADAM_B1) * g
    v = ADAM_B2 * v + (1.0 - ADAM_B2) * _jnp.square(g)
    m_hat = m / (1.0 - ADAM_B1 ** ADAM_STEP)
    v_hat = v / (1.0 - ADAM_B2 ** ADAM_STEP)
    delta = -ADAM_LR * (m_hat / (_jnp.sqrt(v_hat) + ADAM_EPS) + ADAM_WD * w)
    return delta, m, v


def reference(x, norm1_w, w_in, lb_logits, hg_norm_w, q_norm_w, k_norm_w, att_norm_w, w_out, norm2_w, w_gate_up, w_down, final_norm_w, loss_target, m_norm1_w, m_w_in, m_lb_logits, m_hg_norm_w, m_q_norm_w, m_k_norm_w, m_att_norm_w, m_w_out, m_norm2_w, m_w_gate_up, m_w_down, m_final_norm_w, v_norm1_w, v_w_in, v_lb_logits, v_hg_norm_w, v_q_norm_w, v_k_norm_w, v_att_norm_w, v_w_out, v_norm2_w, v_w_gate_up, v_w_down, v_final_norm_w):
    given = dict(x=x, norm1_w=norm1_w, w_in=w_in, lb_logits=lb_logits, hg_norm_w=hg_norm_w, q_norm_w=q_norm_w, k_norm_w=k_norm_w, att_norm_w=att_norm_w, w_out=w_out, norm2_w=norm2_w, w_gate_up=w_gate_up, w_down=w_down, final_norm_w=final_norm_w, loss_target=loss_target, m_norm1_w=m_norm1_w, m_w_in=m_w_in, m_lb_logits=m_lb_logits, m_hg_norm_w=m_hg_norm_w, m_q_norm_w=m_q_norm_w, m_k_norm_w=m_k_norm_w, m_att_norm_w=m_att_norm_w, m_w_out=m_w_out, m_norm2_w=m_norm2_w, m_w_gate_up=m_w_gate_up, m_w_down=m_w_down, m_final_norm_w=m_final_norm_w, v_norm1_w=v_norm1_w, v_w_in=v_w_in, v_lb_logits=v_lb_logits, v_hg_norm_w=v_hg_norm_w, v_q_norm_w=v_q_norm_w, v_k_norm_w=v_k_norm_w, v_att_norm_w=v_att_norm_w, v_w_out=v_w_out, v_norm2_w=v_norm2_w, v_w_gate_up=v_w_gate_up, v_w_down=v_w_down, v_final_norm_w=v_final_norm_w)
    weights = {n: given[n] for n in TWIN_WEIGHTS}
    shared = {n: given[n] for n in SHARED_INPUTS}
    per_example = {n: given[n] for n in ['x']}
    grad_fn = _jax.value_and_grad(_loss, argnums=(0, 1))

    def one_microbatch(ex, loss_target):
        ex = dict(ex)
        diff = ex.pop(TWIN_DIFF_INPUT)
        return grad_fn(weights, diff, {**shared, **ex}, loss_target)

    if N_MICROBATCH == 1:
        loss, (grad_w, grad_x) = one_microbatch(per_example, given["loss_target"])
    else:
        def body(carry, xs):
            loss_sum, grad_sum = carry
            l_k, (gw_k, gx_k) = one_microbatch(xs[0], xs[1])
            with _jax.named_scope("update"):
                return (loss_sum + l_k, _jax.tree.map(_jnp.add, grad_sum, gw_k)), gx_k

        init = (_jnp.zeros((), _jnp.float32), _jax.tree.map(_jnp.zeros_like, weights))
        (loss, grad_w), grad_x = _jax.lax.scan(body, init, (per_example, given["loss_target"]))
    with _jax.named_scope("update"):
        delta_w, new_m, new_v = {}, {}, {}
        for n in TWIN_WEIGHTS:
            delta_w[n], new_m[n], new_v[n] = _adamw(weights[n], grad_w[n], given["m_" + n], given["v_" + n])
    return (loss, grad_x, *[grad_w[n] for n in TWIN_WEIGHTS], *[delta_w[n] for n in TWIN_WEIGHTS],
            *[new_m[n] for n in TWIN_WEIGHTS], *[new_v[n] for n in TWIN_WEIGHTS])
```

```python
import functools
import math

import jax
import jax.numpy as jnp
import numpy as np
from jax import lax
from jax.experimental import pallas as pl
from jax.experimental.pallas import tpu as pltpu

F32 = jnp.float32
BF16 = jnp.bfloat16

N_DEV = 8
D_MODEL = 1024
EPS = 1e-6
HG_HEADS = 4
HG_D = 128
HG_W = HG_HEADS * HG_D
CHUNK = 64
ATT_HEADS = 8
ATT_KV = 2
ATT_G = ATT_HEADS // ATT_KV
ATT_DH = 64
ATT_QW = ATT_HEADS * ATT_DH
ATT_KW = ATT_KV * ATT_DH
GRID_W = 64
ROPE_THETA = 10000.0
D_IN = 5 * HG_W + ATT_QW + 2 * ATT_KW
D_FF = 2816
ADAM_LR, ADAM_B1, ADAM_B2, ADAM_EPS, ADAM_WD, ADAM_STEP = 0.001, 0.9, 0.999, 1e-08, 0.01, 10

LANES = 128
VMEM_LIMIT = 48 * 1024 * 1024
MESH = pl.DeviceIdType.MESH
ANY = pl.BlockSpec(memory_space=pl.ANY)


def _params(sem=None):
    return pltpu.CompilerParams(dimension_semantics=sem, vmem_limit_bytes=VMEM_LIMIT)


def _pick(n, cap):
    best = None
    for t in range(LANES, cap + 1, LANES):
        if n % t == 0:
            best = t
    assert best is not None, (n, cap)
    return best


def _sigmoid(x):
    return 1.0 / (1.0 + jnp.exp(-x))


def _dot(a, b):
    return jnp.dot(a.astype(BF16), b.astype(BF16), preferred_element_type=F32)


def _dot_nt(a, b):
    return lax.dot_general(a.astype(BF16), b.astype(BF16), (((1,), (1,)), ((), ())),
                           preferred_element_type=F32)


def _dot_tn(a, b):
    return lax.dot_general(a.astype(BF16), b.astype(BF16), (((0,), (0,)), ((), ())),
                           preferred_element_type=F32)


def _mm_nn(pairs, *, name, out_dtype=F32, residual=None, tm=512, tn_cap=512):
    M = pairs[0][0].shape[0]
    N = pairs[0][1].shape[1]
    tn = _pick(N, tn_cap)
    n_pairs = len(pairs)
    has_res = residual is not None

    def body(*refs):
        acc = None
        for i in range(n_pairs):
            d = jnp.dot(refs[2 * i][...], refs[2 * i + 1][...], preferred_element_type=F32)
            acc = d if acc is None else acc + d
        if has_res:
            acc = acc + refs[2 * n_pairs][...]
        refs[-1][...] = acc.astype(out_dtype)

    in_specs, args = [], []
    for a, b in pairs:
        k = a.shape[1]
        in_specs += [pl.BlockSpec((tm, k), lambda i, j: (i, 0)), pl.BlockSpec((k, tn), lambda i, j: (0, j))]
        args += [a, b]
    if has_res:
        in_specs.append(pl.BlockSpec((tm, tn), lambda i, j: (i, j)))
        args.append(residual)
    return pl.pallas_call(
        body, name=name, grid=(M // tm, N // tn), in_specs=in_specs,
        out_specs=pl.BlockSpec((tm, tn), lambda i, j: (i, j)),
        out_shape=jax.ShapeDtypeStruct((M, N), out_dtype),
        compiler_params=_params(("parallel", "arbitrary")),
    )(*args)


def _mm_tn(a, b, *, name, tma_cap=1024, tnb_cap=1024, tk=512):
    T, Ma = a.shape
    Nb = b.shape[1]
    tma, tnb = _pick(Ma, tma_cap), _pick(Nb, tnb_cap)
    n_k = T // tk

    def body(a_ref, b_ref, o_ref, acc_ref):
        k = pl.program_id(2)

        @pl.when(k == 0)
        def _():
            acc_ref[...] = jnp.zeros_like(acc_ref)

        acc_ref[...] += lax.dot_general(a_ref[...], b_ref[...], (((0,), (0,)), ((), ())),
                                        preferred_element_type=F32)

        @pl.when(k == n_k - 1)
        def _():
            o_ref[...] = acc_ref[...]

    return pl.pallas_call(
        body, name=name, grid=(Ma // tma, Nb // tnb, n_k),
        in_specs=[pl.BlockSpec((tk, tma), lambda i, j, k: (k, i)), pl.BlockSpec((tk, tnb), lambda i, j, k: (k, j))],
        out_specs=pl.BlockSpec((tma, tnb), lambda i, j, k: (i, j)),
        out_shape=jax.ShapeDtypeStruct((Ma, Nb), F32),
        scratch_shapes=[pltpu.VMEM((tma, tnb), F32)],
        compiler_params=_params(("parallel", "parallel", "arbitrary")),
    )(a, b)


def _rms_fwd(x, w, *, name, tm=512):
    T, Dm = x.shape

    def body(x_ref, w_ref, h_ref, r_ref):
        xv = x_ref[...]
        r = lax.rsqrt(jnp.mean(xv * xv, axis=-1, keepdims=True) + EPS)
        h_ref[...] = (xv * r * w_ref[...]).astype(BF16)
        r_ref[...] = r

    return pl.pallas_call(
        body, name=name, grid=(T // tm,),
        in_specs=[pl.BlockSpec((tm, Dm), lambda i: (i, 0)), pl.BlockSpec((1, Dm), lambda i: (0, 0))],
        out_specs=[pl.BlockSpec((tm, Dm), lambda i: (i, 0)), pl.BlockSpec((tm, 1), lambda i: (i, 0))],
        out_shape=[jax.ShapeDtypeStruct((T, Dm), BF16), jax.ShapeDtypeStruct((T, 1), F32)],
        compiler_params=_params(("parallel",)),
    )(x, w)


def _rms_bwd(dh, x, r, w, dres, *, name, emit_bf16, tm=512):
    T, Dm = x.shape

    def body(dh_ref, x_ref, r_ref, w_ref, dres_ref, *outs):
        dx_ref, dw_ref = outs[0], outs[-1]

        @pl.when(pl.program_id(0) == 0)
        def _():
            dw_ref[...] = jnp.zeros_like(dw_ref)

        rv = r_ref[...]
        xh = x_ref[...] * rv
        dhv = dh_ref[...]
        dxh = dhv * w_ref[...]
        t = jnp.mean(dxh * xh, axis=-1, keepdims=True)
        dx = dres_ref[...] + rv * (dxh - xh * t)
        dx_ref[...] = dx
        if emit_bf16:
            outs[1][...] = dx.astype(BF16)
        dw_ref[...] += jnp.sum(dhv * xh, axis=0, keepdims=True)

    row = pl.BlockSpec((tm, Dm), lambda i: (i, 0))
    vec = pl.BlockSpec((1, Dm), lambda i: (0, 0))
    out_specs = [row] + ([row] if emit_bf16 else []) + [vec]
    out_shape = ([jax.ShapeDtypeStruct((T, Dm), F32)] + ([jax.ShapeDtypeStruct((T, Dm), BF16)] if emit_bf16 else [])
                 + [jax.ShapeDtypeStruct((1, Dm), F32)])
    return pl.pallas_call(
        body, name=name, grid=(T // tm,),
        in_specs=[row, row, pl.BlockSpec((tm, 1), lambda i: (i, 0)), vec, row],
        out_specs=out_specs, out_shape=out_shape,
        compiler_params=_params(("arbitrary",)),
    )(dh, x, r, w, dres)


def _loss_head(x2, target, w, *, name, tm=512):
    T, Dm = x2.shape

    def body(x_ref, t_ref, w_ref, loss_ref, dx_ref, dxb_ref, dw_ref):
        @pl.when(pl.program_id(0) == 0)
        def _():
            loss_ref[...] = jnp.zeros_like(loss_ref)
            dw_ref[...] = jnp.zeros_like(dw_ref)

        xv = x_ref[...]
        r = lax.rsqrt(jnp.mean(xv * xv, axis=-1, keepdims=True) + EPS)
        xh = xv * r
        wv = w_ref[...]
        err = xh * wv - t_ref[...]
        row_loss = jnp.mean(err * err, axis=-1, keepdims=True)
        loss_ref[...] += 0.5 * jnp.sum(row_loss, axis=0, keepdims=True)
        dy = err * (1.0 / Dm)
        dxh = dy * wv
        t = jnp.mean(dxh * xh, axis=-1, keepdims=True)
        dx = r * (dxh - xh * t)
        dx_ref[...] = dx
        dxb_ref[...] = dx.astype(BF16)
        dw_ref[...] += jnp.sum(dy * xh, axis=0, keepdims=True)

    row = pl.BlockSpec((tm, Dm), lambda i: (i, 0))
    vec = pl.BlockSpec((1, Dm), lambda i: (0, 0))
    return pl.pallas_call(
        body, name=name, grid=(T // tm,),
        in_specs=[row, row, vec],
        out_specs=[pl.BlockSpec((1, 1), lambda i: (0, 0)), row, row, vec],
        out_shape=[jax.ShapeDtypeStruct((1, 1), F32), jax.ShapeDtypeStruct((T, Dm), F32),
                   jax.ShapeDtypeStruct((T, Dm), BF16), jax.ShapeDtypeStruct((1, Dm), F32)],
        compiler_params=_params(("arbitrary",)),
    )(x2, target, w)


GLA_TB = 512
GLA_NC = GLA_TB // CHUNK


def _cumsum_rows(x, row, reverse):
    n = x.shape[0]
    s = 1
    while s < n:
        if not reverse:
            x = x + jnp.where(row >= s, pltpu.roll(x, s, 0), 0.0)
        else:
            x = x + jnp.where(row < n - s, pltpu.roll(x, n - s, 0), 0.0)
        s *= 2
    return x


def _gla_gates(uq, z, lbv):
    q = uq * _sigmoid(uq)
    sg = _sigmoid(z)
    sgn = _sigmoid(-z)
    f = lbv + (1.0 - lbv) * sg
    k = (1.0 - lbv) * sgn
    return q, sg, sgn, f, k


def _gla_decays(f, row, reverse):
    b = _cumsum_rows(jnp.log(f), row, reverse)
    if not reverse:
        bref, blast = b[CHUNK // 2 - 1:CHUNK // 2, :], b[CHUNK - 1:CHUNK, :]
    else:
        bref, blast = b[CHUNK // 2:CHUNK // 2 + 1, :], b[0:1, :]
    return b, bref, blast


def _gla_fwd(U, lb, *, f_block, reverse, name):
    T = U.shape[0]
    nb = T // GLA_TB

    def body(uq_ref, uf_ref, ui_ref, lb_ref, o_ref, st_ref, s_ref):
        @pl.when(pl.program_id(0) == 0)
        def _():
            s_ref[...] = jnp.zeros_like(s_ref)

        row = lax.broadcasted_iota(jnp.int32, (CHUNK, HG_D), 0)
        ri = lax.broadcasted_iota(jnp.int32, (CHUNK, CHUNK), 0)
        ci = lax.broadcasted_iota(jnp.int32, (CHUNK, CHUNK), 1)
        mask = (ri <= ci) if reverse else (ri >= ci)

        def chunk(j, carry):
            c = (GLA_NC - 1 - j) if reverse else j
            rows = pl.ds(pl.multiple_of(c * CHUNK, CHUNK), CHUNK)
            for h in range(HG_HEADS):
                cols = pl.ds(h * HG_D, HG_D)
                v = ui_ref[rows, cols]
                q, _, _, f, k = _gla_gates(uq_ref[rows, cols], uf_ref[rows, cols], lb_ref[:, cols])
                b, bref, blast = _gla_decays(f, row, reverse)
                s = jnp.where(mask, _dot_nt(q * jnp.exp(b - bref), k * jnp.exp(bref - b)), 0.0)
                st = s_ref[h]
                st_ref[c, h] = st
                o_ref[rows, cols] = _dot(s, v) + _dot_nt(q * jnp.exp(b), st)
                s_ref[h] = st * jnp.exp(blast) + _dot_tn(v, k * jnp.exp(blast - b))
            return carry

        lax.fori_loop(0, GLA_NC, chunk, 0)

    blk = (lambda i: nb - 1 - i) if reverse else (lambda i: i)
    ucol = lambda cb: pl.BlockSpec((GLA_TB, HG_W), lambda i: (blk(i), cb))
    return pl.pallas_call(
        body, name=name, grid=(nb,),
        in_specs=[ucol(0), ucol(f_block), ucol(3), pl.BlockSpec((1, HG_W), lambda i: (0, 0))],
        out_specs=[pl.BlockSpec((GLA_TB, HG_W), lambda i: (blk(i), 0)),
                   pl.BlockSpec((GLA_NC, HG_HEADS, HG_D, HG_D), lambda i: (blk(i), 0, 0, 0))],
        out_shape=[jax.ShapeDtypeStruct((T, HG_W), F32),
                   jax.ShapeDtypeStruct((T // CHUNK, HG_HEADS, HG_D, HG_D), F32)],
        scratch_shapes=[pltpu.VMEM((HG_HEADS, HG_D, HG_D), F32)],
        compiler_params=_params(("arbitrary",)),
    )(U, U, U, lb)


def _gla_bwd(U, lb, do, states, *, f_block, reverse, name):
    T = U.shape[0]
    nb = T // GLA_TB

    def body(uq_ref, uf_ref, ui_ref, lb_ref, do_ref, st_ref, dq_ref, dz_ref, dv_ref, dlb_ref, ds_ref):
        @pl.when(pl.program_id(0) == 0)
        def _():
            ds_ref[...] = jnp.zeros_like(ds_ref)
            dlb_ref[...] = jnp.zeros_like(dlb_ref)

        row = lax.broadcasted_iota(jnp.int32, (CHUNK, HG_D), 0)
        ri = lax.broadcasted_iota(jnp.int32, (CHUNK, CHUNK), 0)
        ci = lax.broadcasted_iota(jnp.int32, (CHUNK, CHUNK), 1)
        mask = (ri <= ci) if reverse else (ri >= ci)

        def chunk(j, carry):
            c = j if reverse else (GLA_NC - 1 - j)
            rows = pl.ds(pl.multiple_of(c * CHUNK, CHUNK), CHUNK)
            for h in range(HG_HEADS):
                cols = pl.ds(h * HG_D, HG_D)
                v = ui_ref[rows, cols]
                lbv = lb_ref[:, cols]
                q, sg, sgn, f, k = _gla_gates(uq_ref[rows, cols], uf_ref[rows, cols], lbv)
                b, bref, blast = _gla_decays(f, row, reverse)
                eq, ek, eb, el, dec = (jnp.exp(b - bref), jnp.exp(bref - b), jnp.exp(b), jnp.exp(blast - b),
                                       jnp.exp(blast))
                qin, kin, qb, klast = q * eq, k * ek, q * eb, k * el
                dov = do_ref[rows, cols]
                st = st_ref[c, h]
                dst = ds_ref[h]
                p = jnp.where(mask, _dot_nt(qin, kin), 0.0)
                dp = jnp.where(mask, _dot_nt(dov, v), 0.0)
                dqin = _dot(dp, kin)
                dkin = _dot_tn(dp, qin)
                dv_ref[rows, cols] = _dot_tn(p, dov) + _dot_nt(klast, dst)
                dqb = _dot(dov, st)
                dklast = _dot(v, dst)
                ds_ref[h] = _dot_tn(dov, qb) + dst * dec
                db = dqin * qin - dkin * kin + dqb * qb - dklast * klast
                extra = (jnp.sum(dklast * klast, axis=0, keepdims=True)
                         + dec * jnp.sum(st * dst, axis=0, keepdims=True))
                dg = _cumsum_rows(db, row, not reverse) + extra
                dq_ref[rows, cols] = dqin * eq + dqb * eb
                dk = dkin * ek + dklast * el
                dfk = dg / f - dk
                dz_ref[rows, cols] = dfk * (1.0 - lbv) * sg * sgn
                dlb_ref[:, cols] += jnp.sum(dfk * sgn, axis=0, keepdims=True)
            return carry

        lax.fori_loop(0, GLA_NC, chunk, 0)

    blk = (lambda i: i) if reverse else (lambda i: nb - 1 - i)
    ucol = lambda cb: pl.BlockSpec((GLA_TB, HG_W), lambda i: (blk(i), cb))
    tok = pl.BlockSpec((GLA_TB, HG_W), lambda i: (blk(i), 0))
    vec = pl.BlockSpec((1, HG_W), lambda i: (0, 0))
    return pl.pallas_call(
        body, name=name, grid=(nb,),
        in_specs=[ucol(0), ucol(f_block), ucol(3), vec, tok,
                  pl.BlockSpec((GLA_NC, HG_HEADS, HG_D, HG_D), lambda i: (blk(i), 0, 0, 0))],
        out_specs=[tok, tok, tok, vec],
        out_shape=[jax.ShapeDtypeStruct((T, HG_W), F32)] * 3 + [jax.ShapeDtypeStruct((1, HG_W), F32)],
        scratch_shapes=[pltpu.VMEM((HG_HEADS, HG_D, HG_D), F32)],
        compiler_params=_params(("arbitrary",)),
    )(U, U, U, lb, do, states)


def _hg_post_fwd(o_f, o_b, U, w, *, name, tm=512):
    T = o_f.shape[0]

    def body(of_ref, ob_ref, ug_ref, w_ref, out_ref):
        wv = w_ref[...]
        for h in range(HG_HEADS):
            cols = pl.ds(h * HG_D, HG_D)
            o = of_ref[:, cols] + ob_ref[:, cols]
            r = lax.rsqrt(jnp.mean(o * o, axis=-1, keepdims=True) + EPS)
            ug = ug_ref[:, cols]
            out_ref[:, cols] = (o * r * wv * (ug * _sigmoid(ug))).astype(BF16)

    tok = pl.BlockSpec((tm, HG_W), lambda i: (i, 0))
    return pl.pallas_call(
        body, name=name, grid=(T // tm,),
        in_specs=[tok, tok, pl.BlockSpec((tm, HG_W), lambda i: (i, 4)), pl.BlockSpec((1, HG_D), lambda i: (0, 0))],
        out_specs=tok, out_shape=jax.ShapeDtypeStruct((T, HG_W), BF16),
        compiler_params=_params(("parallel",)),
    )(o_f, o_b, U, w)


def _hg_post_bwd(dmix, o_f, o_b, U, w, *, name, tm=512):
    T = o_f.shape[0]

    def body(dm_ref, of_ref, ob_ref, ug_ref, w_ref, do_ref, dug_ref, dw_ref):
        @pl.when(pl.program_id(0) == 0)
        def _():
            dw_ref[...] = jnp.zeros_like(dw_ref)

        wv = w_ref[...]
        for h in range(HG_HEADS):
            cols = pl.ds(h * HG_D, HG_D)
            o = of_ref[:, cols] + ob_ref[:, cols]
            r = lax.rsqrt(jnp.mean(o * o, axis=-1, keepdims=True) + EPS)
            xh = o * r
            ug = ug_ref[:, cols]
            sg = _sigmoid(ug)
            dm = dm_ref[:, cols]
            dn = dm * (ug * sg)
            dug_ref[:, cols] = dm * (xh * wv) * (sg * (1.0 + ug * (1.0 - sg)))
            dxh = dn * wv
            t = jnp.mean(dxh * xh, axis=-1, keepdims=True)
            do_ref[:, cols] = r * (dxh - xh * t)
            dw_ref[:, cols] += jnp.sum(dn * xh, axis=0, keepdims=True)

    tok = pl.BlockSpec((tm, HG_W), lambda i: (i, 0))
    vec = pl.BlockSpec((1, HG_W), lambda i: (0, 0))
    return pl.pallas_call(
        body, name=name, grid=(T // tm,),
        in_specs=[tok, tok, tok, pl.BlockSpec((tm, HG_W), lambda i: (i, 4)), pl.BlockSpec((1, HG_D), lambda i: (0, 0))],
        out_specs=[tok, tok, vec],
        out_shape=[jax.ShapeDtypeStruct((T, HG_W), F32)] * 2 + [jax.ShapeDtypeStruct((1, HG_W), F32)],
        compiler_params=_params(("arbitrary",)),
    )(dmix, o_f, o_b, U, w)


def _rope_tables(T):
    rows = T // GRID_W
    row = jnp.repeat(jnp.arange(rows), GRID_W).astype(F32)
    col = jnp.tile(jnp.arange(GRID_W), rows).astype(F32)
    axis_dim = ATT_DH // 2
    freqs = ROPE_THETA ** (-jnp.arange(0, axis_dim, 2, dtype=F32) / axis_dim)
    ang = jnp.concatenate([row[:, None] * freqs, col[:, None] * freqs], axis=-1)
    cos, sin = jnp.cos(ang), jnp.sin(ang)
    c = jnp.repeat(cos, 2, axis=-1)
    s = jnp.stack([-sin, sin], axis=-1).reshape(T, ATT_DH)
    return jnp.tile(c, (1, 2)), jnp.tile(s, (1, 2))


def _head_blockdiag(width):
    shift = ATT_DH.bit_length() - 1
    ri = jnp.right_shift(lax.broadcasted_iota(jnp.int32, (width, width), 0), shift)
    ci = jnp.right_shift(lax.broadcasted_iota(jnp.int32, (width, width), 1), shift)
    return jnp.where(ri == ci, 1.0, 0.0).astype(BF16)


def _head_sum(x, bd):
    hi = x.astype(BF16)
    lo = (x - hi.astype(F32)).astype(BF16)
    return jnp.dot(hi, bd, preferred_element_type=F32) + jnp.dot(lo, bd, preferred_element_type=F32)


def _pair_swap(x, even):
    n = x.shape[-1]
    return jnp.where(even, pltpu.roll(x, n - 1, 1), pltpu.roll(x, 1, 1))


def _att_prep_fwd(U, cos, sin, qw, kw, *, name, tm=512):
    T = U.shape[0]
    scale = ATT_DH ** -0.5

    def body(aq_ref, ak_ref, av_ref, c_ref, s_ref, qw_ref, kw_ref, q_ref, k_ref, v_ref):
        bd = _head_blockdiag(ATT_QW)
        c2, s2 = c_ref[...], s_ref[...]
        c8, s8 = jnp.tile(c2, (1, 4)), jnp.tile(s2, (1, 4))

        def norm_rope(x, w, c, s, bdm):
            r = lax.rsqrt(_head_sum(x * x, bdm) * (1.0 / ATT_DH) + EPS)
            y = x * r * w
            even = (lax.broadcasted_iota(jnp.int32, y.shape, 1) & 1) == 0
            return y * c + _pair_swap(y, even) * s

        q_ref[...] = (norm_rope(aq_ref[...], qw_ref[...], c8, s8, bd) * scale).astype(BF16)
        k_ref[...] = norm_rope(ak_ref[...], kw_ref[...], c2, s2, bd[:ATT_KW, :ATT_KW]).astype(BF16)
        v_ref[...] = av_ref[...].astype(BF16)

    kv_spec = pl.BlockSpec((tm, ATT_KW), lambda i: (i, 0))
    return pl.pallas_call(
        body, name=name, grid=(T // tm,),
        in_specs=[pl.BlockSpec((tm, ATT_QW), lambda i: (i, 5)),
                  pl.BlockSpec((tm, ATT_KW), lambda i: (i, 24)), pl.BlockSpec((tm, ATT_KW), lambda i: (i, 25)),
                  kv_spec, kv_spec,
                  pl.BlockSpec((1, ATT_QW), lambda i: (0, 0)), pl.BlockSpec((1, ATT_KW), lambda i: (0, 0))],
        out_specs=[pl.BlockSpec((tm, ATT_QW), lambda i: (i, 0)), kv_spec, kv_spec],
        out_shape=[jax.ShapeDtypeStruct((T, ATT_QW), BF16), jax.ShapeDtypeStruct((T, ATT_KW), BF16),
                   jax.ShapeDtypeStruct((T, ATT_KW), BF16)],
        compiler_params=_params(("parallel",)),
    )(U, U, U, cos, sin, qw, kw)


def _att_prep_bwd(U, dq, dk, cos, sin, qw, kw, *, name, tm=512):
    T = U.shape[0]
    scale = ATT_DH ** -0.5

    def body(aq_ref, ak_ref, dq_ref, dk_ref, c_ref, s_ref, qw_ref, kw_ref, daq_ref, dak_ref, dqw_ref, dkw_ref):
        @pl.when(pl.program_id(0) == 0)
        def _():
            dqw_ref[...] = jnp.zeros_like(dqw_ref)
            dkw_ref[...] = jnp.zeros_like(dkw_ref)

        bd = _head_blockdiag(ATT_QW)
        c2, s2 = c_ref[...], s_ref[...]
        c8, s8 = jnp.tile(c2, (1, 4)), jnp.tile(s2, (1, 4))

        def bwd(x, dy, w, c, s, bdm):
            even = (lax.broadcasted_iota(jnp.int32, x.shape, 1) & 1) == 0
            dn = dy * c - _pair_swap(dy, even) * s
            r = lax.rsqrt(_head_sum(x * x, bdm) * (1.0 / ATT_DH) + EPS)
            xh = x * r
            dxh = dn * w
            t = _head_sum(dxh * xh, bdm) * (1.0 / ATT_DH)
            return r * (dxh - xh * t), jnp.sum(dn * xh, axis=0, keepdims=True)

        da, dw = bwd(aq_ref[...], dq_ref[...] * scale, qw_ref[...], c8, s8, bd)
        daq_ref[...] = da
        dqw_ref[...] += dw
        da, dw = bwd(ak_ref[...], dk_ref[...], kw_ref[...], c2, s2, bd[:ATT_KW, :ATT_KW])
        dak_ref[...] = da
        dkw_ref[...] += dw

    q_spec = pl.BlockSpec((tm, ATT_QW), lambda i: (i, 0))
    kv_spec = pl.BlockSpec((tm, ATT_KW), lambda i: (i, 0))
    qv = pl.BlockSpec((1, ATT_QW), lambda i: (0, 0))
    kv = pl.BlockSpec((1, ATT_KW), lambda i: (0, 0))
    return pl.pallas_call(
        body, name=name, grid=(T // tm,),
        in_specs=[pl.BlockSpec((tm, ATT_QW), lambda i: (i, 5)), pl.BlockSpec((tm, ATT_KW), lambda i: (i, 24)),
                  q_spec, kv_spec, kv_spec, kv_spec, qv, kv],
        out_specs=[q_spec, kv_spec, qv, kv],
        out_shape=[jax.ShapeDtypeStruct((T, ATT_QW), F32), jax.ShapeDtypeStruct((T, ATT_KW), F32),
                   jax.ShapeDtypeStruct((1, ATT_QW), F32), jax.ShapeDtypeStruct((1, ATT_KW), F32)],
        compiler_params=_params(("arbitrary",)),
    )(U, U, dq, dk, cos, sin, qw, kw)


FA_TQ = 256
FA_TK = 512


def _flash_fwd(q4, k, v, *, name):
    T = k.shape[1]
    tq, tk = min(FA_TQ, T), min(FA_TK, T)
    n_k = T // tk
    R = ATT_G * tq

    def body(q_ref, k_ref, v_ref, o_ref, lse_ref, m_ref, l_ref, acc_ref):
        kk = pl.program_id(2)

        @pl.when(kk == 0)
        def _():
            m_ref[...] = jnp.full_like(m_ref, -jnp.inf)
            l_ref[...] = jnp.zeros_like(l_ref)
            acc_ref[...] = jnp.zeros_like(acc_ref)

        q = q_ref[0].reshape(R, ATT_DH)
        s = lax.dot_general(q, k_ref[0], (((1,), (1,)), ((), ())), preferred_element_type=F32)
        m_old = m_ref[...]
        m_new = jnp.maximum(m_old, jnp.max(s, axis=-1, keepdims=True))
        alpha = jnp.exp(m_old - m_new)
        p = jnp.exp(s - m_new)
        l_ref[...] = alpha * l_ref[...] + jnp.sum(p, axis=-1, keepdims=True)
        acc_ref[...] = alpha * acc_ref[...] + jnp.dot(p.astype(BF16), v_ref[0], preferred_element_type=F32)
        m_ref[...] = m_new

        @pl.when(kk == n_k - 1)
        def _():
            l = l_ref[...]
            o_ref[0] = (acc_ref[...] / l).reshape(ATT_G, tq, ATT_DH)
            lse_ref[0] = (m_ref[...] + jnp.log(l)).reshape(ATT_G, tq, 1)

    return pl.pallas_call(
        body, name=name, grid=(ATT_KV, T // tq, n_k),
        in_specs=[pl.BlockSpec((1, ATT_G, tq, ATT_DH), lambda h, i, j: (h, 0, i, 0)),
                  pl.BlockSpec((1, tk, ATT_DH), lambda h, i, j: (h, j, 0)),
                  pl.BlockSpec((1, tk, ATT_DH), lambda h, i, j: (h, j, 0))],
        out_specs=[pl.BlockSpec((1, ATT_G, tq, ATT_DH), lambda h, i, j: (h, 0, i, 0)),
                   pl.BlockSpec((1, ATT_G, tq, 1), lambda h, i, j: (h, 0, i, 0))],
        out_shape=[jax.ShapeDtypeStruct((ATT_KV, ATT_G, T, ATT_DH), F32),
                   jax.ShapeDtypeStruct((ATT_KV, ATT_G, T, 1), F32)],
        scratch_shapes=[pltpu.VMEM((R, 1), F32), pltpu.VMEM((R, 1), F32), pltpu.VMEM((R, ATT_DH), F32)],
        compiler_params=_params(("parallel", "parallel", "arbitrary")),
    )(q4, k, v)


def _flash_bwd_dq(q4, k, v, do4, o4, lse, *, name):
    T = k.shape[1]
    tq, tk = min(FA_TQ, T), min(FA_TK, T)
    n_k = T // tk
    R = ATT_G * tq

    def body(q_ref, k_ref, v_ref, do_ref, o_ref, lse_ref, dq_ref, delta_ref, acc_ref, dl_ref):
        kk = pl.program_id(2)

        @pl.when(kk == 0)
        def _():
            acc_ref[...] = jnp.zeros_like(acc_ref)
            dl = jnp.sum(do_ref[0].astype(F32) * o_ref[0], axis=-1, keepdims=True)
            delta_ref[0] = dl
            dl_ref[...] = dl.reshape(R, 1)

        q = q_ref[0].reshape(R, ATT_DH)
        do = do_ref[0].reshape(R, ATT_DH)
        kv = k_ref[0]
        s = lax.dot_general(q, kv, (((1,), (1,)), ((), ())), preferred_element_type=F32)
        p = jnp.exp(s - lse_ref[0].reshape(R, 1))
        dp = lax.dot_general(do, v_ref[0], (((1,), (1,)), ((), ())), preferred_element_type=F32)
        ds = p * (dp - dl_ref[...])
        acc_ref[...] += jnp.dot(ds.astype(BF16), kv, preferred_element_type=F32)

        @pl.when(kk == n_k - 1)
        def _():
            dq_ref[0] = acc_ref[...].reshape(ATT_G, tq, ATT_DH)

    qspec = pl.BlockSpec((1, ATT_G, tq, ATT_DH), lambda h, i, j: (h, 0, i, 0))
    kspec = pl.BlockSpec((1, tk, ATT_DH), lambda h, i, j: (h, j, 0))
    rspec = pl.BlockSpec((1, ATT_G, tq, 1), lambda h, i, j: (h, 0, i, 0))
    return pl.pallas_call(
        body, name=name, grid=(ATT_KV, T // tq, n_k),
        in_specs=[qspec, kspec, kspec, qspec, qspec, rspec],
        out_specs=[qspec, rspec],
        out_shape=[jax.ShapeDtypeStruct((ATT_KV, ATT_G, T, ATT_DH), F32),
                   jax.ShapeDtypeStruct((ATT_KV, ATT_G, T, 1), F32)],
        scratch_shapes=[pltpu.VMEM((R, ATT_DH), F32), pltpu.VMEM((R, 1), F32)],
        compiler_params=_params(("parallel", "parallel", "arbitrary")),
    )(q4, k, v, do4, o4, lse)


def _flash_bwd_dkv(q4, k, v, do4, lse, delta, *, name):
    T = k.shape[1]
    tq, tk = min(FA_TQ, T), min(FA_TK, T)
    n_q = T // tq
    R = ATT_G * tq

    def body(q_ref, k_ref, v_ref, do_ref, lse_ref, delta_ref, dk_ref, dv_ref, dk_acc, dv_acc):
        qq = pl.program_id(2)

        @pl.when(qq == 0)
        def _():
            dk_acc[...] = jnp.zeros_like(dk_acc)
            dv_acc[...] = jnp.zeros_like(dv_acc)

        q = q_ref[0].reshape(R, ATT_DH)
        do = do_ref[0].reshape(R, ATT_DH)
        s = lax.dot_general(q, k_ref[0], (((1,), (1,)), ((), ())), preferred_element_type=F32)
        p = jnp.exp(s - lse_ref[0].reshape(R, 1))
        dv_acc[...] += lax.dot_general(p.astype(BF16), do, (((0,), (0,)), ((), ())), preferred_element_type=F32)
        dp = lax.dot_general(do, v_ref[0], (((1,), (1,)), ((), ())), preferred_element_type=F32)
        ds = p * (dp - delta_ref[0].reshape(R, 1))
        dk_acc[...] += lax.dot_general(ds.astype(BF16), q, (((0,), (0,)), ((), ())), preferred_element_type=F32)

        @pl.when(qq == n_q - 1)
        def _():
            dk_ref[0] = dk_acc[...]
            dv_ref[0] = dv_acc[...]

    qspec = pl.BlockSpec((1, ATT_G, tq, ATT_DH), lambda h, j, i: (h, 0, i, 0))
    kspec = pl.BlockSpec((1, tk, ATT_DH), lambda h, j, i: (h, j, 0))
    rspec = pl.BlockSpec((1, ATT_G, tq, 1), lambda h, j, i: (h, 0, i, 0))
    return pl.pallas_call(
        body, name=name, grid=(ATT_KV, T // tk, n_q),
        in_specs=[qspec, kspec, kspec, qspec, rspec, rspec],
        out_specs=[kspec, kspec],
        out_shape=[jax.ShapeDtypeStruct((ATT_KV, T, ATT_DH), F32)] * 2,
        scratch_shapes=[pltpu.VMEM((tk, ATT_DH), F32), pltpu.VMEM((tk, ATT_DH), F32)],
        compiler_params=_params(("parallel", "parallel", "arbitrary")),
    )(q4, k, v, do4, lse, delta)


def _att_post_fwd(o, w, *, name, tm=512):
    T = o.shape[0]

    def body(o_ref, w_ref, out_ref):
        ov = o_ref[...]
        r = lax.rsqrt(jnp.mean(ov * ov, axis=-1, keepdims=True) + EPS)
        out_ref[...] = (ov * r * w_ref[...]).astype(BF16)

    tok = pl.BlockSpec((tm, ATT_QW), lambda i: (i, 0))
    return pl.pallas_call(
        body, name=name, grid=(T // tm,),
        in_specs=[tok, pl.BlockSpec((1, ATT_QW), lambda i: (0, 0))],
        out_specs=tok, out_shape=jax.ShapeDtypeStruct((T, ATT_QW), BF16),
        compiler_params=_params(("parallel",)),
    )(o, w)


def _att_post_bwd(dmix, o, w, *, name, tm=512):
    T = o.shape[0]

    def body(dm_ref, o_ref, w_ref, do_ref, dw_ref):
        @pl.when(pl.program_id(0) == 0)
        def _():
            dw_ref[...] = jnp.zeros_like(dw_ref)

        ov = o_ref[...]
        r = lax.rsqrt(jnp.mean(ov * ov, axis=-1, keepdims=True) + EPS)
        xh = ov * r
        dm = dm_ref[...]
        dxh = dm * w_ref[...]
        t = jnp.mean(dxh * xh, axis=-1, keepdims=True)
        do_ref[...] = (r * (dxh - xh * t)).astype(BF16)
        dw_ref[...] += jnp.sum(dm * xh, axis=0, keepdims=True)

    tok = pl.BlockSpec((tm, ATT_QW), lambda i: (i, 0))
    vec = pl.BlockSpec((1, ATT_QW), lambda i: (0, 0))
    return pl.pallas_call(
        body, name=name, grid=(T // tm,),
        in_specs=[pl.BlockSpec((tm, ATT_QW), lambda i: (i, 1)), tok, vec],
        out_specs=[tok, vec],
        out_shape=[jax.ShapeDtypeStruct((T, ATT_QW), BF16), jax.ShapeDtypeStruct((1, ATT_QW), F32)],
        compiler_params=_params(("arbitrary",)),
    )(dmix, o, w)


def _ffn_up(h2, wg, wu, *, name, tm=512):
    T = h2.shape[0]
    tn = _pick(D_FF, 1408)

    def body(h_ref, wg_ref, wu_ref, g_ref, u_ref, a_ref):
        hv = h_ref[...]
        g = jnp.dot(hv, wg_ref[...], preferred_element_type=F32)
        u = jnp.dot(hv, wu_ref[...], preferred_element_type=F32)
        g_ref[...] = g.astype(BF16)
        u_ref[...] = u.astype(BF16)
        a_ref[...] = (g * _sigmoid(g) * u).astype(BF16)

    wspec = pl.BlockSpec((D_MODEL, tn), lambda i, j: (0, j))
    ospec = pl.BlockSpec((tm, tn), lambda i, j: (i, j))
    return pl.pallas_call(
        body, name=name, grid=(T // tm, D_FF // tn),
        in_specs=[pl.BlockSpec((tm, D_MODEL), lambda i, j: (i, 0)), wspec, wspec],
        out_specs=[ospec] * 3, out_shape=[jax.ShapeDtypeStruct((T, D_FF), BF16)] * 3,
        compiler_params=_params(("parallel", "arbitrary")),
    )(h2, wg, wu)


def _ffn_act_bwd(dx2b, wd_t, gate, up, *, name, tm=512):
    T = dx2b.shape[0]
    tn = _pick(D_FF, 1408)

    def body(dx_ref, w_ref, g_ref, u_ref, dg_ref, du_ref):
        da = jnp.dot(dx_ref[...], w_ref[...], preferred_element_type=F32)
        g = g_ref[...].astype(F32)
        u = u_ref[...].astype(F32)
        sg = _sigmoid(g)
        dg_ref[...] = (da * u * (sg * (1.0 + g * (1.0 - sg)))).astype(BF16)
        du_ref[...] = (da * (g * sg)).astype(BF16)

    ospec = pl.BlockSpec((tm, tn), lambda i, j: (i, j))
    return pl.pallas_call(
        body, name=name, grid=(T // tm, D_FF // tn),
        in_specs=[pl.BlockSpec((tm, D_MODEL), lambda i, j: (i, 0)),
                  pl.BlockSpec((D_MODEL, tn), lambda i, j: (0, j)), ospec, ospec],
        out_specs=[ospec] * 2, out_shape=[jax.ShapeDtypeStruct((T, D_FF), BF16)] * 2,
        compiler_params=_params(("parallel", "arbitrary")),
    )(dx2b, wd_t, gate, up)


def _assemble_du(U, dq_f, dq_b, dz_f, dz_b, dv_f, dv_b, du_g, da_q, da_k, da_v, *, name, tm=256):
    T = U.shape[0]

    def body(uq_ref, dqf, dqb, dzf, dzb, dvf, dvb, dug, daq, dak, dav, out_ref):
        uq = uq_ref[...]
        sg = _sigmoid(uq)
        out_ref[:, 0:HG_W] = ((dqf[...] + dqb[...]) * (sg * (1.0 + uq * (1.0 - sg)))).astype(BF16)
        out_ref[:, HG_W:2 * HG_W] = dzf[...].astype(BF16)
        out_ref[:, 2 * HG_W:3 * HG_W] = dzb[...].astype(BF16)
        out_ref[:, 3 * HG_W:4 * HG_W] = (dvf[...] + dvb[...]).astype(BF16)
        out_ref[:, 4 * HG_W:5 * HG_W] = dug[...].astype(BF16)
        out_ref[:, 5 * HG_W:5 * HG_W + ATT_QW] = daq[...].astype(BF16)
        out_ref[:, 5 * HG_W + ATT_QW:5 * HG_W + ATT_QW + ATT_KW] = dak[...].astype(BF16)
        out_ref[:, 5 * HG_W + ATT_QW + ATT_KW:D_IN] = dav[...].astype(BF16)

    tok = pl.BlockSpec((tm, HG_W), lambda i: (i, 0))
    kv = pl.BlockSpec((tm, ATT_KW), lambda i: (i, 0))
    return pl.pallas_call(
        body, name=name, grid=(T // tm,),
        in_specs=[tok] * 9 + [kv, kv],
        out_specs=pl.BlockSpec((tm, D_IN), lambda i: (i, 0)),
        out_shape=jax.ShapeDtypeStruct((T, D_IN), BF16),
        compiler_params=_params(("parallel",)),
    )(U, dq_f, dq_b, dz_f, dz_b, dv_f, dv_b, du_g, da_q, da_k, da_v)


def _adam_math(w, g, m, v):
    m = ADAM_B1 * m + (1.0 - ADAM_B1) * g
    v = ADAM_B2 * v + (1.0 - ADAM_B2) * (g * g)
    m_hat = m / (1.0 - ADAM_B1 ** ADAM_STEP)
    v_hat = v / (1.0 - ADAM_B2 ** ADAM_STEP)
    delta = -ADAM_LR * (m_hat / (jnp.sqrt(v_hat) + ADAM_EPS) + ADAM_WD * w)
    return delta, m, v


def _adamw(parts, w, m, v, *, name, tr_cap=256):
    P, R, C = parts.shape
    tr = R
    for t in range(8, min(R, tr_cap) + 1, 8):
        if R % t == 0:
            tr = t

    def body(p_ref, w_ref, m_ref, v_ref, g_ref, d_ref, nm_ref, nv_ref):
        g = p_ref[0]
        for j in range(1, P):
            g = g + p_ref[j]
        d, nm, nv = _adam_math(w_ref[...], g, m_ref[...], v_ref[...])
        g_ref[...] = g
        d_ref[...] = d
        nm_ref[...] = nm
        nv_ref[...] = nv

    blk = pl.BlockSpec((tr, C), lambda i: (i, 0))
    return pl.pallas_call(
        body, name=name, grid=(R // tr,),
        in_specs=[pl.BlockSpec((P, tr, C), lambda i: (0, i, 0)), blk, blk, blk],
        out_specs=[blk] * 4, out_shape=[jax.ShapeDtypeStruct((R, C), F32)] * 4,
        compiler_params=_params(("parallel",)),
    )(parts, w, m, v)


def _all_gather(xs, *, name):
    n = len(xs)

    def body(*refs):
        ins, outs = refs[:n], refs[n:2 * n]
        send_sems, recv_sems, local_sems = refs[2 * n:]
        x, y, c = lax.axis_index("x"), lax.axis_index("y"), lax.axis_index("c")
        me, sibling = (x, y, c), (x, y, 1 - c)
        chips = [(1 - x, y), (x, 1 - y), (1 - x, 1 - y)]

        def slot(p):
            return 4 * p[0] + 2 * p[1] + p[2]

        def copy(a, k, block, to, src=None):
            dst = outs[a].at[slot(block)]
            return pltpu.make_async_remote_copy(
                src_ref=dst if src is None else src, dst_ref=dst,
                send_sem=send_sems.at[a * 7 + k], recv_sem=recv_sems.at[a * 7 + k],
                device_id=to, device_id_type=MESH)

        mine = [pltpu.make_async_copy(ins[a], outs[a].at[slot(me)], local_sems.at[a]) for a in range(n)]
        for cp in mine:
            cp.start()
        first = []
        for a in range(n):
            first.append(copy(a, 0, me, sibling, src=ins[a]))
            first += [copy(a, 1 + j, me, (*chip, c), src=ins[a]) for j, chip in enumerate(chips)]
        for cp in first:
            cp.start()
        passed = []
        for j, chip in enumerate(chips):
            for a in range(n):
                copy(a, 1 + j, (*chip, c), me).wait_recv()
                cp = copy(a, 4 + j, (*chip, c), sibling)
                cp.start()
                passed.append(cp)
        for a in range(n):
            copy(a, 0, sibling, me).wait_recv()
            for j, chip in enumerate(chips):
                copy(a, 4 + j, (*chip, 1 - c), me).wait_recv()
        for cp in first + passed:
            cp.wait_send()
        for cp in mine:
            cp.wait()

    return pl.pallas_call(
        body, name=name,
        in_specs=[ANY] * n, out_specs=[ANY] * n,
        out_shape=[jax.ShapeDtypeStruct((N_DEV,) + x.shape, x.dtype) for x in xs],
        scratch_shapes=[pltpu.SemaphoreType.DMA((7 * n,)), pltpu.SemaphoreType.DMA((7 * n,)),
                        pltpu.SemaphoreType.DMA((n,))],
        compiler_params=pltpu.CompilerParams(has_side_effects=True),
    )(*xs)


def _exchange(gs, *, name):
    n = len(gs)
    masks = [(mx, my, mc) for mx in (0, 1) for my in (0, 1) for mc in (0, 1)][1:]

    def body(*refs):
        ins, outs = refs[:n], refs[n:2 * n]
        send_sems, recv_sems, local_sems = refs[2 * n:]
        x, y, c = lax.axis_index("x"), lax.axis_index("y"), lax.axis_index("c")
        me = 4 * x + 2 * y + c

        def flip(v, bit):
            return 1 - v if bit else v

        mine = [pltpu.make_async_copy(ins[a].at[me], outs[a].at[me], local_sems.at[a]) for a in range(n)]
        for cp in mine:
            cp.start()
        copies = []
        for a in range(n):
            for k, (mx, my, mc) in enumerate(masks):
                peer = (flip(x, mx), flip(y, my), flip(c, mc))
                peer_slot = 4 * peer[0] + 2 * peer[1] + peer[2]
                copies.append((
                    pltpu.make_async_remote_copy(
                        src_ref=ins[a].at[peer_slot], dst_ref=outs[a].at[me],
                        send_sem=send_sems.at[a * 7 + k], recv_sem=recv_sems.at[a * 7 + k],
                        device_id=peer, device_id_type=MESH),
                    pltpu.make_async_remote_copy(
                        src_ref=ins[a].at[peer_slot], dst_ref=outs[a].at[peer_slot],
                        send_sem=send_sems.at[a * 7 + k], recv_sem=recv_sems.at[a * 7 + k],
                        device_id=peer, device_id_type=MESH)))
        for send, _ in copies:
            send.start()
        for send, recv in copies:
            recv.wait_recv()
            send.wait_send()
        for cp in mine:
            cp.wait()

    return pl.pallas_call(
        body, name=name,
        in_specs=[ANY] * n, out_specs=[ANY] * n,
        out_shape=[jax.ShapeDtypeStruct(g.shape, g.dtype) for g in gs],
        scratch_shapes=[pltpu.SemaphoreType.DMA((7 * n,)), pltpu.SemaphoreType.DMA((7 * n,)),
                        pltpu.SemaphoreType.DMA((n,))],
        compiler_params=pltpu.CompilerParams(has_side_effects=True),
    )(*gs)


PACK_ROWS = 8


def _pack_small(norm1, norm2, final, att, hg, qn, kn, lb=None):
    z = lambda n: jnp.zeros((n,), F32)
    rows = [norm1.reshape(-1), norm2.reshape(-1), final.reshape(-1),
            jnp.concatenate([att.reshape(-1), z(512)]),
            jnp.concatenate([hg.reshape(-1), qn.reshape(-1), kn.reshape(-1), z(1024 - 256)]),
            z(1024) if lb is None else lb.reshape(-1), z(1024), z(1024)]
    return jnp.stack(rows, axis=0)


def _unpack_small(p):
    return (p[0:1, :], p[1:2, :], p[2, :], p[3:4, 0:512], p[4:5, 0:128], p[4:5, 128:192], p[4:5, 192:256])


def _fold_heads(dhg, dqn, dkn, *, name):
    def body(hg_ref, q_ref, k_ref, ohg_ref, oq_ref, ok_ref):
        def fold128(v):
            acc = v[:, 0:LANES]
            for j in range(1, v.shape[1] // LANES):
                acc = acc + v[:, j * LANES:(j + 1) * LANES]
            return acc

        ohg_ref[...] = fold128(hg_ref[...])
        q = fold128(q_ref[...])
        oq_ref[...] = q + pltpu.roll(q, ATT_DH, 1)
        k = k_ref[...]
        ok_ref[...] = k + pltpu.roll(k, ATT_DH, 1)

    return pl.pallas_call(body, name=name, out_shape=[jax.ShapeDtypeStruct((1, LANES), F32)] * 3)(dhg, dqn, dkn)


def _lb_grad(dlb_sum, lb, *, name):
    def body(d_ref, lb_ref, o_ref):
        lbv = lb_ref[...]
        gl = d_ref[...] * lbv * (1.0 - lbv)
        o_ref[0:1, :] = gl[0:1, :]
        o_ref[1:2, :] = -gl[0:1, :]
        o_ref[2:3, :] = gl[1:2, :]
        o_ref[3:4, :] = -gl[1:2, :]

    return pl.pallas_call(body, name=name, out_shape=jax.ShapeDtypeStruct((4, HG_W), F32))(dlb_sum, lb)


def _lower_bounds(lb_logits_full, *, name):
    def body(l_ref, o_ref):
        for d in range(2):
            l0, l1 = l_ref[2 * d:2 * d + 1, :], l_ref[2 * d + 1:2 * d + 2, :]
            mx = jnp.maximum(l0, l1)
            e0, e1 = jnp.exp(l0 - mx), jnp.exp(l1 - mx)
            o_ref[d:d + 1, :] = e0 / (e0 + e1)

    return pl.pallas_call(body, name=name, out_shape=jax.ShapeDtypeStruct((2, HG_W), F32))(
        lb_logits_full.reshape(4, HG_W))


def _local_step(x, target, norm1_w, w_in, lb, hg_norm_w, q_norm_w, k_norm_w, att_norm_w, w_out, norm2_w,
                w_g, w_u, w_down, final_norm_w):
    T = x.shape[0]
    cos, sin = _rope_tables(T)
    qw8 = jnp.tile(q_norm_w, (1, ATT_HEADS))
    kw2 = jnp.tile(k_norm_w, (1, ATT_KV))

    h, r1 = _rms_fwd(x, norm1_w, name="norm1_fwd")
    U = _mm_nn([(h, w_in)], name="in_proj")
    o_f, st_f = _gla_fwd(U, lb[0:1], f_block=1, reverse=False, name="gla_fwd_f")
    o_b, st_b = _gla_fwd(U, lb[1:2], f_block=2, reverse=True, name="gla_fwd_b")
    mix_hg = _hg_post_fwd(o_f, o_b, U, hg_norm_w, name="hg_post_fwd")
    q, k, v = _att_prep_fwd(U, cos, sin, qw8, kw2, name="att_prep_fwd")
    q4 = q.reshape(T, ATT_KV, ATT_G, ATT_DH).transpose(1, 2, 0, 3)
    k3 = k.reshape(T, ATT_KV, ATT_DH).transpose(1, 0, 2)
    v3 = v.reshape(T, ATT_KV, ATT_DH).transpose(1, 0, 2)
    o4, lse = _flash_fwd(q4, k3, v3, name="flash_fwd")
    o_att = o4.transpose(2, 0, 1, 3).reshape(T, ATT_QW)
    mix_att = _att_post_fwd(o_att, att_norm_w, name="att_post_fwd")
    mix = jnp.concatenate([mix_hg, mix_att], axis=1)
    x1 = _mm_nn([(mix, w_out)], residual=x, name="out_proj")
    h2, r2 = _rms_fwd(x1, norm2_w, name="norm2_fwd")
    gate, up, act = _ffn_up(h2, w_g, w_u, name="ffn_up")
    x2 = _mm_nn([(act, w_down)], residual=x1, name="ffn_down")
    loss, dx2, dx2b, d_final = _loss_head(x2, target, final_norm_w.reshape(1, D_MODEL), name="loss_head")

    d_gate, d_up = _ffn_act_bwd(dx2b, w_down.T, gate, up, name="ffn_act_bwd")
    dw_down = _mm_tn(act, dx2b, tma_cap=1408, name="dw_down")
    dh2 = _mm_nn([(d_gate, w_g.T), (d_up, w_u.T)], name="ffn_up_bwd")
    dw_g = _mm_tn(h2, d_gate, tnb_cap=1408, name="dw_gate")
    dw_u = _mm_tn(h2, d_up, tnb_cap=1408, name="dw_up")
    dx1, dx1b, d_norm2 = _rms_bwd(dh2, x1, r2, norm2_w, dx2, emit_bf16=True, name="norm2_bwd")
    dmix = _mm_nn([(dx1b, w_out.T)], name="out_proj_bwd")
    dw_out = _mm_tn(mix, dx1b, name="dw_out")
    do_att, d_att = _att_post_bwd(dmix, o_att, att_norm_w, name="att_post_bwd")
    do4 = do_att.reshape(T, ATT_KV, ATT_G, ATT_DH).transpose(1, 2, 0, 3)
    dq4, delta = _flash_bwd_dq(q4, k3, v3, do4, o4, lse, name="flash_bwd_dq")
    dk3, dv3 = _flash_bwd_dkv(q4, k3, v3, do4, lse, delta, name="flash_bwd_dkv")
    dq = dq4.transpose(2, 0, 1, 3).reshape(T, ATT_QW)
    dk = dk3.transpose(1, 0, 2).reshape(T, ATT_KW)
    da_v = dv3.transpose(1, 0, 2).reshape(T, ATT_KW)
    da_q, da_k, d_qn, d_kn = _att_prep_bwd(U, dq, dk, cos, sin, qw8, kw2, name="att_prep_bwd")
    do_hg, du_g, d_hg = _hg_post_bwd(dmix, o_f, o_b, U, hg_norm_w, name="hg_post_bwd")
    dq_f, dz_f, dv_f, dlb_f = _gla_bwd(U, lb[0:1], do_hg, st_f, f_block=1, reverse=False, name="gla_bwd_f")
    dq_b, dz_b, dv_b, dlb_b = _gla_bwd(U, lb[1:2], do_hg, st_b, f_block=2, reverse=True, name="gla_bwd_b")
    dU = _assemble_du(U, dq_f, dq_b, dz_f, dz_b, dv_f, dv_b, du_g, da_q, da_k, da_v, name="assemble_du")
    dh = _mm_nn([(dU, w_in.T)], name="in_proj_bwd")
    dw_in = _mm_tn(h, dU, tnb_cap=1664, name="dw_in")
    grad_x, d_norm1 = _rms_bwd(dh, x, r1, norm1_w, dx1, emit_bf16=False, name="norm1_bwd")
    d_hg, d_qn, d_kn = _fold_heads(d_hg, d_qn, d_kn, name="fold_heads")

    big = dict(w_in=dw_in, w_out=dw_out, w_g=dw_g, w_u=dw_u, w_down=dw_down)
    small = dict(norm1=d_norm1, norm2=d_norm2, final=d_final, att=d_att, hg=d_hg,
                 qn=d_qn[:, :ATT_DH], kn=d_kn[:, :ATT_DH], lb=jnp.concatenate([dlb_f, dlb_b], axis=0))
    return loss, grad_x, big, small


def kernel(x, norm1_w, w_in, lb_logits, hg_norm_w, q_norm_w, k_norm_w, att_norm_w, w_out, norm2_w, w_gate_up, w_down, final_norm_w, loss_target, m_norm1_w, m_w_in, m_lb_logits, m_hg_norm_w, m_q_norm_w, m_k_norm_w, m_att_norm_w, m_w_out, m_norm2_w, m_w_gate_up, m_w_down, m_final_norm_w, v_norm1_w, v_w_in, v_lb_logits, v_hg_norm_w, v_q_norm_w, v_k_norm_w, v_att_norm_w, v_w_out, v_norm2_w, v_w_gate_up, v_w_down, v_final_norm_w):
    T = x.shape[1]
    me = 4 * lax.axis_index("x") + 2 * lax.axis_index("y") + lax.axis_index("c")
    c_in, r_out, c_gu, r_dn = w_in.shape[2], w_out.shape[1], w_gate_up.shape[2], w_down.shape[1]
    lb_cols = lb_logits.shape[2]

    g_in, g_out, g_gu, g_dn, g_lb = _all_gather(
        [w_in[0].astype(BF16), w_out[0].astype(BF16), w_gate_up[0].astype(BF16), w_down[0].astype(BF16),
         lb_logits.reshape(4, lb_cols)], name="gather_weights")
    w_in_f = g_in.transpose(1, 0, 2).reshape(D_MODEL, N_DEV * c_in)
    w_out_f = g_out.reshape(N_DEV * r_out, D_MODEL)
    half = N_DEV // 2
    w_g_f = g_gu[:half].transpose(1, 0, 2).reshape(D_MODEL, half * c_gu)
    w_u_f = g_gu[half:].transpose(1, 0, 2).reshape(D_MODEL, half * c_gu)
    w_dn_f = g_dn.reshape(N_DEV * r_dn, D_MODEL)
    lb_logits_f = g_lb.transpose(1, 0, 2).reshape(2, 2, N_DEV * lb_cols)
    lb = _lower_bounds(lb_logits_f, name="lower_bounds")

    loss, grad_x, big, small = _local_step(
        x[0], loss_target[0], norm1_w, w_in_f, lb, hg_norm_w, q_norm_w, k_norm_w, att_norm_w, w_out_f, norm2_w,
        w_g_f, w_u_f, w_dn_f, final_norm_w)

    s_in = big["w_in"].reshape(D_MODEL, N_DEV, c_in).transpose(1, 0, 2)
    s_out = big["w_out"].reshape(N_DEV, r_out, D_MODEL)
    s_gu = jnp.concatenate([big["w_g"].reshape(D_MODEL, half, c_gu).transpose(1, 0, 2),
                            big["w_u"].reshape(D_MODEL, half, c_gu).transpose(1, 0, 2)], axis=0)
    s_dn = big["w_down"].reshape(N_DEV, r_dn, D_MODEL)
    p_in, p_out, p_gu, p_dn = _exchange([s_in, s_out, s_gu, s_dn], name="exchange_grads")

    packed = _pack_small(small["norm1"], small["norm2"], small["final"], small["att"], small["hg"],
                         small["qn"], small["kn"], small["lb"])
    (all_small,) = _all_gather([packed], name="gather_small_grads")

    g_w_in, d_w_in, nm_w_in, nv_w_in = _adamw(p_in, w_in[0], m_w_in[0], v_w_in[0], name="adamw_w_in")
    g_w_out, d_w_out, nm_w_out, nv_w_out = _adamw(p_out, w_out[0], m_w_out[0], v_w_out[0], name="adamw_w_out")
    g_w_gu, d_w_gu, nm_w_gu, nv_w_gu = _adamw(p_gu, w_gate_up[0], m_w_gate_up[0], v_w_gate_up[0], name="adamw_w_gu")
    g_w_dn, d_w_dn, nm_w_dn, nv_w_dn = _adamw(p_dn, w_down[0], m_w_down[0], v_w_down[0], name="adamw_w_down")

    pk = lambda vecs: _pack_small(*vecs)
    w_pk = pk([norm1_w, norm2_w, final_norm_w, att_norm_w, hg_norm_w, q_norm_w, k_norm_w])
    m_pk = pk([m_norm1_w, m_norm2_w, m_final_norm_w, m_att_norm_w, m_hg_norm_w, m_q_norm_w, m_k_norm_w])
    v_pk = pk([v_norm1_w, v_norm2_w, v_final_norm_w, v_att_norm_w, v_hg_norm_w, v_q_norm_w, v_k_norm_w])
    g_pk, d_pk, nm_pk, nv_pk = _adamw(all_small, w_pk, m_pk, v_pk, name="adamw_small")

    dlb_sum = g_pk[5:6, :].reshape(2, HG_W)
    g_lb_full = _lb_grad(dlb_sum, lb, name="lb_grad")
    g_lb_mine = lax.dynamic_slice_in_dim(g_lb_full, me * lb_cols, lb_cols, axis=1)
    g_lb_s, d_lb, nm_lb, nv_lb = _adamw(g_lb_mine[None], lb_logits.reshape(4, lb_cols),
                                        m_lb_logits.reshape(4, lb_cols), v_lb_logits.reshape(4, lb_cols),
                                        name="adamw_lb")

    loss_total = lax.psum(loss[0, 0], ("x", "y", "c"))

    def outs(big4, lb_arr, pk_arr):
        n1, n2, fin, att, hg, qn, kn = _unpack_small(pk_arr)
        b_in, b_out, b_gu, b_dn = big4
        return [n1, b_in[None], lb_arr.reshape(2, 2, lb_cols), hg, qn, kn, att, b_out[None], n2, b_gu[None],
                b_dn[None], fin]

    return (loss_total, grad_x[None],
            *outs((g_w_in, g_w_out, g_w_gu, g_w_dn), g_lb_s, g_pk),
            *outs((d_w_in, d_w_out, d_w_gu, d_w_dn), d_lb, d_pk),
            *outs((nm_w_in, nm_w_out, nm_w_gu, nm_w_dn), nm_lb, nm_pk),
            *outs((nv_w_in, nv_w_out, nv_w_gu, nv_w_dn), nv_lb, nv_pk))
```

```python
import functools
import math

import jax
import jax.numpy as jnp
import numpy as np
from jax import lax
from jax.experimental import pallas as pl
from jax.experimental.pallas import tpu as pltpu

F32 = jnp.float32
BF16 = jnp.bfloat16

N_DEV = 8
D_MODEL = 1024
EPS = 1e-6
HG_HEADS = 4
HG_D = 128
HG_W = HG_HEADS * HG_D
CHUNK = 64
ATT_HEADS = 8
ATT_KV = 2
ATT_G = ATT_HEADS // ATT_KV
ATT_DH = 64
ATT_QW = ATT_HEADS * ATT_DH
ATT_KW = ATT_KV * ATT_DH
GRID_W = 64
ROPE_THETA = 10000.0
D_IN = 5 * HG_W + ATT_QW + 2 * ATT_KW
D_FF = 2816
ADAM_LR, ADAM_B1, ADAM_B2, ADAM_EPS, ADAM_WD, ADAM_STEP = 0.001, 0.9, 0.999, 1e-08, 0.01, 10

LANES = 128
VMEM_LIMIT = 48 * 1024 * 1024
MESH = pl.DeviceIdType.MESH
ANY = pl.BlockSpec(memory_space=pl.ANY)


def _params(sem=None):
    return pltpu.CompilerParams(dimension_semantics=sem, vmem_limit_bytes=VMEM_LIMIT)


def _pick(n, cap):
    best = None
    for t in range(LANES, cap + 1, LANES):
        if n % t == 0:
            best = t
    assert best is not None, (n, cap)
    return best


def _sigmoid(x):
    return 1.0 / (1.0 + jnp.exp(-x))


def _dot(a, b):
    return jnp.dot(a.astype(BF16), b.astype(BF16), preferred_element_type=F32)


def _dot_nt(a, b):
    return lax.dot_general(a.astype(BF16), b.astype(BF16), (((1,), (1,)), ((), ())),
                           preferred_element_type=F32)


def _dot_tn(a, b):
    return lax.dot_general(a.astype(BF16), b.astype(BF16), (((0,), (0,)), ((), ())),
                           preferred_element_type=F32)


def _mm_nn(pairs, *, name, out_dtype=F32, residual=None, tm=512, tn_cap=None):
    M = pairs[0][0].shape[0]
    N = pairs[0][1].shape[1]
    tn = N if tn_cap is None else _pick(N, tn_cap)
    n_pairs = len(pairs)
    has_res = residual is not None

    def body(*refs):
        acc = None
        for i in range(n_pairs):
            d = jnp.dot(refs[2 * i][...], refs[2 * i + 1][...], preferred_element_type=F32)
            acc = d if acc is None else acc + d
        if has_res:
            acc = acc + refs[2 * n_pairs][...]
        refs[-1][...] = acc.astype(out_dtype)

    in_specs, args = [], []
    for a, b in pairs:
        k = a.shape[1]
        in_specs += [pl.BlockSpec((tm, k), lambda i, j: (i, 0)), pl.BlockSpec((k, tn), lambda i, j: (0, j))]
        args += [a, b]
    if has_res:
        in_specs.append(pl.BlockSpec((tm, tn), lambda i, j: (i, j)))
        args.append(residual)
    return pl.pallas_call(
        body, name=name, grid=(M // tm, N // tn), in_specs=in_specs,
        out_specs=pl.BlockSpec((tm, tn), lambda i, j: (i, j)),
        out_shape=jax.ShapeDtypeStruct((M, N), out_dtype),
        compiler_params=_params(("parallel", "arbitrary")),
    )(*args)


def _mm_tn(a, b, *, name, tma_cap=1024, tnb_cap=1024, tk=512):
    T, Ma = a.shape
    Nb = b.shape[1]
    tma, tnb = _pick(Ma, tma_cap), _pick(Nb, tnb_cap)
    n_k = T // tk

    def body(a_ref, b_ref, o_ref, acc_ref):
        k = pl.program_id(2)

        @pl.when(k == 0)
        def _():
            acc_ref[...] = jnp.zeros_like(acc_ref)

        acc_ref[...] += lax.dot_general(a_ref[...], b_ref[...], (((0,), (0,)), ((), ())),
                                        preferred_element_type=F32)

        @pl.when(k == n_k - 1)
        def _():
            o_ref[...] = acc_ref[...]

    return pl.pallas_call(
        body, name=name, grid=(Ma // tma, Nb // tnb, n_k),
        in_specs=[pl.BlockSpec((tk, tma), lambda i, j, k: (k, i)), pl.BlockSpec((tk, tnb), lambda i, j, k: (k, j))],
        out_specs=pl.BlockSpec((tma, tnb), lambda i, j, k: (i, j)),
        out_shape=jax.ShapeDtypeStruct((Ma, Nb), F32),
        scratch_shapes=[pltpu.VMEM((tma, tnb), F32)],
        compiler_params=_params(("parallel", "parallel", "arbitrary")),
    )(a, b)


def _rms_fwd(x, w, *, name, tm=512):
    T, Dm = x.shape

    def body(x_ref, w_ref, h_ref, r_ref):
        xv = x_ref[...]
        r = lax.rsqrt(jnp.mean(xv * xv, axis=-1, keepdims=True) + EPS)
        h_ref[...] = (xv * r * w_ref[...]).astype(BF16)
        r_ref[...] = r

    return pl.pallas_call(
        body, name=name, grid=(T // tm,),
        in_specs=[pl.BlockSpec((tm, Dm), lambda i: (i, 0)), pl.BlockSpec((1, Dm), lambda i: (0, 0))],
        out_specs=[pl.BlockSpec((tm, Dm), lambda i: (i, 0)), pl.BlockSpec((tm, 1), lambda i: (i, 0))],
        out_shape=[jax.ShapeDtypeStruct((T, Dm), BF16), jax.ShapeDtypeStruct((T, 1), F32)],
        compiler_params=_params(("parallel",)),
    )(x, w)


def _rms_bwd(dh, x, r, w, dres, *, name, emit_bf16, tm=512):
    T, Dm = x.shape

    def body(dh_ref, x_ref, r_ref, w_ref, dres_ref, *outs):
        dx_ref, dw_ref = outs[0], outs[-1]

        @pl.when(pl.program_id(0) == 0)
        def _():
            dw_ref[...] = jnp.zeros_like(dw_ref)

        rv = r_ref[...]
        xh = x_ref[...] * rv
        dhv = dh_ref[...]
        dxh = dhv * w_ref[...]
        t = jnp.mean(dxh * xh, axis=-1, keepdims=True)
        dx = dres_ref[...] + rv * (dxh - xh * t)
        dx_ref[...] = dx
        if emit_bf16:
            outs[1][...] = dx.astype(BF16)
        dw_ref[...] += jnp.sum(dhv * xh, axis=0, keepdims=True)

    row = pl.BlockSpec((tm, Dm), lambda i: (i, 0))
    vec = pl.BlockSpec((1, Dm), lambda i: (0, 0))
    out_specs = [row] + ([row] if emit_bf16 else []) + [vec]
    out_shape = ([jax.ShapeDtypeStruct((T, Dm), F32)] + ([jax.ShapeDtypeStruct((T, Dm), BF16)] if emit_bf16 else [])
                 + [jax.ShapeDtypeStruct((1, Dm), F32)])
    return pl.pallas_call(
        body, name=name, grid=(T // tm,),
        in_specs=[row, row, pl.BlockSpec((tm, 1), lambda i: (i, 0)), vec, row],
        out_specs=out_specs, out_shape=out_shape,
        compiler_params=_params(("arbitrary",)),
    )(dh, x, r, w, dres)


def _loss_head(x2, target, w, *, name, tm=512):
    T, Dm = x2.shape

    def body(x_ref, t_ref, w_ref, loss_ref, dx_ref, dxb_ref, dw_ref):
        @pl.when(pl.program_id(0) == 0)
        def _():
            loss_ref[...] = jnp.zeros_like(loss_ref)
            dw_ref[...] = jnp.zeros_like(dw_ref)

        xv = x_ref[...]
        r = lax.rsqrt(jnp.mean(xv * xv, axis=-1, keepdims=True) + EPS)
        xh = xv * r
        wv = w_ref[...]
        err = xh * wv - t_ref[...]
        row_loss = jnp.mean(err * err, axis=-1, keepdims=True)
        loss_ref[...] += 0.5 * jnp.sum(row_loss, axis=0, keepdims=True)
        dy = err * (1.0 / Dm)
        dxh = dy * wv
        t = jnp.mean(dxh * xh, axis=-1, keepdims=True)
        dx = r * (dxh - xh * t)
        dx_ref[...] = dx
        dxb_ref[...] = dx.astype(BF16)
        dw_ref[...] += jnp.sum(dy * xh, axis=0, keepdims=True)

    row = pl.BlockSpec((tm, Dm), lambda i: (i, 0))
    vec = pl.BlockSpec((1, Dm), lambda i: (0, 0))
    return pl.pallas_call(
        body, name=name, grid=(T // tm,),
        in_specs=[row, row, vec],
        out_specs=[pl.BlockSpec((1, 1), lambda i: (0, 0)), row, row, vec],
        out_shape=[jax.ShapeDtypeStruct((1, 1), F32), jax.ShapeDtypeStruct((T, Dm), F32),
                   jax.ShapeDtypeStruct((T, Dm), BF16), jax.ShapeDtypeStruct((1, Dm), F32)],
        compiler_params=_params(("arbitrary",)),
    )(x2, target, w)


GLA_TB = 512
GLA_NC = GLA_TB // CHUNK


def _cumsum_rows(x, row, reverse):
    n = x.shape[0]
    s = 1
    while s < n:
        if not reverse:
            x = x + jnp.where(row >= s, pltpu.roll(x, s, 0), 0.0)
        else:
            x = x + jnp.where(row < n - s, pltpu.roll(x, n - s, 0), 0.0)
        s *= 2
    return x


def _gla_gates(uq, z, lbv):
    q = uq * _sigmoid(uq)
    sg = _sigmoid(z)
    sgn = _sigmoid(-z)
    f = lbv + (1.0 - lbv) * sg
    k = (1.0 - lbv) * sgn
    return q, sg, sgn, f, k


def _gla_decays(f, row, reverse):
    b = _cumsum_rows(jnp.log(f), row, reverse)
    if not reverse:
        bref, blast = b[CHUNK // 2 - 1:CHUNK // 2, :], b[CHUNK - 1:CHUNK, :]
    else:
        bref, blast = b[CHUNK // 2:CHUNK // 2 + 1, :], b[0:1, :]
    return b, bref, blast


def _gla_fwd(U, lb, *, f_block, reverse, name):
    T = U.shape[0]
    nb = T // GLA_TB

    def body(uq_ref, uf_ref, ui_ref, lb_ref, o_ref, st_ref, s_ref):
        @pl.when(pl.program_id(0) == 0)
        def _():
            s_ref[...] = jnp.zeros_like(s_ref)

        row = lax.broadcasted_iota(jnp.int32, (CHUNK, HG_D), 0)
        ri = lax.broadcasted_iota(jnp.int32, (CHUNK, CHUNK), 0)
        ci = lax.broadcasted_iota(jnp.int32, (CHUNK, CHUNK), 1)
        mask = (ri <= ci) if reverse else (ri >= ci)

        def chunk(j, carry):
            c = (GLA_NC - 1 - j) if reverse else j
            rows = pl.ds(pl.multiple_of(c * CHUNK, CHUNK), CHUNK)
            for h in range(HG_HEADS):
                cols = pl.ds(h * HG_D, HG_D)
                v = ui_ref[rows, cols]
                q, _, _, f, k = _gla_gates(uq_ref[rows, cols], uf_ref[rows, cols], lb_ref[:, cols])
                b, bref, blast = _gla_decays(f, row, reverse)
                s = jnp.where(mask, _dot_nt(q * jnp.exp(b - bref), k * jnp.exp(bref - b)), 0.0)
                st = s_ref[h]
                st_ref[c, h] = st
                o_ref[rows, cols] = _dot(s, v) + _dot_nt(q * jnp.exp(b), st)
                s_ref[h] = st * jnp.exp(blast) + _dot_tn(v, k * jnp.exp(blast - b))
            return carry

        lax.fori_loop(0, GLA_NC, chunk, 0)

    blk = (lambda i: nb - 1 - i) if reverse else (lambda i: i)
    ucol = lambda cb: pl.BlockSpec((GLA_TB, HG_W), lambda i: (blk(i), cb))
    return pl.pallas_call(
        body, name=name, grid=(nb,),
        in_specs=[ucol(0), ucol(f_block), ucol(3), pl.BlockSpec((1, HG_W), lambda i: (0, 0))],
        out_specs=[pl.BlockSpec((GLA_TB, HG_W), lambda i: (blk(i), 0)),
                   pl.BlockSpec((GLA_NC, HG_HEADS, HG_D, HG_D), lambda i: (blk(i), 0, 0, 0))],
        out_shape=[jax.ShapeDtypeStruct((T, HG_W), F32),
                   jax.ShapeDtypeStruct((T // CHUNK, HG_HEADS, HG_D, HG_D), F32)],
        scratch_shapes=[pltpu.VMEM((HG_HEADS, HG_D, HG_D), F32)],
        compiler_params=_params(("arbitrary",)),
    )(U, U, U, lb)


def _gla_bwd(U, lb, do, states, *, f_block, reverse, name):
    T = U.shape[0]
    nb = T // GLA_TB

    def body(uq_ref, uf_ref, ui_ref, lb_ref, do_ref, st_ref, dq_ref, dz_ref, dv_ref, dlb_ref, ds_ref):
        @pl.when(pl.program_id(0) == 0)
        def _():
            ds_ref[...] = jnp.zeros_like(ds_ref)
            dlb_ref[...] = jnp.zeros_like(dlb_ref)

        row = lax.broadcasted_iota(jnp.int32, (CHUNK, HG_D), 0)
        ri = lax.broadcasted_iota(jnp.int32, (CHUNK, CHUNK), 0)
        ci = lax.broadcasted_iota(jnp.int32, (CHUNK, CHUNK), 1)
        mask = (ri <= ci) if reverse else (ri >= ci)

        def chunk(j, carry):
            c = j if reverse else (GLA_NC - 1 - j)
            rows = pl.ds(pl.multiple_of(c * CHUNK, CHUNK), CHUNK)
            for h in range(HG_HEADS):
                cols = pl.ds(h * HG_D, HG_D)
                v = ui_ref[rows, cols]
                lbv = lb_ref[:, cols]
                q, sg, sgn, f, k = _gla_gates(uq_ref[rows, cols], uf_ref[rows, cols], lbv)
                b, bref, blast = _gla_decays(f, row, reverse)
                eq, ek, eb, el, dec = (jnp.exp(b - bref), jnp.exp(bref - b), jnp.exp(b), jnp.exp(blast - b),
                                       jnp.exp(blast))
                qin, kin, qb, klast = q * eq, k * ek, q * eb, k * el
                dov = do_ref[rows, cols]
                st = st_ref[c, h]
                dst = ds_ref[h]
                p = jnp.where(mask, _dot_nt(qin, kin), 0.0)
                dp = jnp.where(mask, _dot_nt(dov, v), 0.0)
                dqin = _dot(dp, kin)
                dkin = _dot_tn(dp, qin)
                dv_ref[rows, cols] = _dot_tn(p, dov) + _dot_nt(klast, dst)
                dqb = _dot(dov, st)
                dklast = _dot(v, dst)
                ds_ref[h] = _dot_tn(dov, qb) + dst * dec
                db = dqin * qin - dkin * kin + dqb * qb - dklast * klast
                extra = (jnp.sum(dklast * klast, axis=0, keepdims=True)
                         + dec * jnp.sum(st * dst, axis=0, keepdims=True))
                dg = _cumsum_rows(db, row, not reverse) + extra
                dq_ref[rows, cols] = dqin * eq + dqb * eb
                dk = dkin * ek + dklast * el
                dfk = dg / f - dk
                dz_ref[rows, cols] = dfk * (1.0 - lbv) * sg * sgn
                dlb_ref[:, cols] += jnp.sum(dfk * sgn, axis=0, keepdims=True)
            return carry

        lax.fori_loop(0, GLA_NC, chunk, 0)

    blk = (lambda i: i) if reverse else (lambda i: nb - 1 - i)
    ucol = lambda cb: pl.BlockSpec((GLA_TB, HG_W), lambda i: (blk(i), cb))
    tok = pl.BlockSpec((GLA_TB, HG_W), lambda i: (blk(i), 0))
    vec = pl.BlockSpec((1, HG_W), lambda i: (0, 0))
    return pl.pallas_call(
        body, name=name, grid=(nb,),
        in_specs=[ucol(0), ucol(f_block), ucol(3), vec, tok,
                  pl.BlockSpec((GLA_NC, HG_HEADS, HG_D, HG_D), lambda i: (blk(i), 0, 0, 0))],
        out_specs=[tok, tok, tok, vec],
        out_shape=[jax.ShapeDtypeStruct((T, HG_W), F32)] * 3 + [jax.ShapeDtypeStruct((1, HG_W), F32)],
        scratch_shapes=[pltpu.VMEM((HG_HEADS, HG_D, HG_D), F32)],
        compiler_params=_params(("arbitrary",)),
    )(U, U, U, lb, do, states)


def _hg_post_fwd(o_f, o_b, U, w, *, name, tm=512):
    T = o_f.shape[0]

    def body(of_ref, ob_ref, ug_ref, w_ref, out_ref):
        wv = w_ref[...]
        for h in range(HG_HEADS):
            cols = pl.ds(h * HG_D, HG_D)
            o = of_ref[:, cols] + ob_ref[:, cols]
            r = lax.rsqrt(jnp.mean(o * o, axis=-1, keepdims=True) + EPS)
            ug = ug_ref[:, cols]
            out_ref[:, cols] = (o * r * wv * (ug * _sigmoid(ug))).astype(BF16)

    tok = pl.BlockSpec((tm, HG_W), lambda i: (i, 0))
    return pl.pallas_call(
        body, name=name, grid=(T // tm,),
        in_specs=[tok, tok, pl.BlockSpec((tm, HG_W), lambda i: (i, 4)), pl.BlockSpec((1, HG_D), lambda i: (0, 0))],
        out_specs=tok, out_shape=jax.ShapeDtypeStruct((T, HG_W), BF16),
        compiler_params=_params(("parallel",)),
    )(o_f, o_b, U, w)


def _hg_post_bwd(dmix, o_f, o_b, U, w, *, name, tm=512):
    T = o_f.shape[0]

    def body(dm_ref, of_ref, ob_ref, ug_ref, w_ref, do_ref, dug_ref, dw_ref):
        @pl.when(pl.program_id(0) == 0)
        def _():
            dw_ref[...] = jnp.zeros_like(dw_ref)

        wv = w_ref[...]
        for h in range(HG_HEADS):
            cols = pl.ds(h * HG_D, HG_D)
            o = of_ref[:, cols] + ob_ref[:, cols]
            r = lax.rsqrt(jnp.mean(o * o, axis=-1, keepdims=True) + EPS)
            xh = o * r
            ug = ug_ref[:, cols]
            sg = _sigmoid(ug)
            dm = dm_ref[:, cols]
            dn = dm * (ug * sg)
            dug_ref[:, cols] = dm * (xh * wv) * (sg * (1.0 + ug * (1.0 - sg)))
            dxh = dn * wv
            t = jnp.mean(dxh * xh, axis=-1, keepdims=True)
            do_ref[:, cols] = r * (dxh - xh * t)
            dw_ref[:, cols] += jnp.sum(dn * xh, axis=0, keepdims=True)

    tok = pl.BlockSpec((tm, HG_W), lambda i: (i, 0))
    vec = pl.BlockSpec((1, HG_W), lambda i: (0, 0))
    return pl.pallas_call(
        body, name=name, grid=(T // tm,),
        in_specs=[tok, tok, tok, pl.BlockSpec((tm, HG_W), lambda i: (i, 4)), pl.BlockSpec((1, HG_D), lambda i: (0, 0))],
        out_specs=[tok, tok, vec],
        out_shape=[jax.ShapeDtypeStruct((T, HG_W), F32)] * 2 + [jax.ShapeDtypeStruct((1, HG_W), F32)],
        compiler_params=_params(("arbitrary",)),
    )(dmix, o_f, o_b, U, w)


def _rope_tables(T):
    rows = T // GRID_W
    row = jnp.repeat(jnp.arange(rows), GRID_W).astype(F32)
    col = jnp.tile(jnp.arange(GRID_W), rows).astype(F32)
    axis_dim = ATT_DH // 2
    freqs = ROPE_THETA ** (-jnp.arange(0, axis_dim, 2, dtype=F32) / axis_dim)
    ang = jnp.concatenate([row[:, None] * freqs, col[:, None] * freqs], axis=-1)
    cos, sin = jnp.cos(ang), jnp.sin(ang)
    c = jnp.repeat(cos, 2, axis=-1)
    s = jnp.stack([-sin, sin], axis=-1).reshape(T, ATT_DH)
    return jnp.tile(c, (1, 2)), jnp.tile(s, (1, 2))


def _head_blockdiag(width):
    shift = ATT_DH.bit_length() - 1
    ri = jnp.right_shift(lax.broadcasted_iota(jnp.int32, (width, width), 0), shift)
    ci = jnp.right_shift(lax.broadcasted_iota(jnp.int32, (width, width), 1), shift)
    return jnp.where(ri == ci, 1.0, 0.0).astype(BF16)


def _head_sum(x, bd):
    hi = x.astype(BF16)
    lo = (x - hi.astype(F32)).astype(BF16)
    return jnp.dot(hi, bd, preferred_element_type=F32) + jnp.dot(lo, bd, preferred_element_type=F32)


def _pair_swap(x, even):
    n = x.shape[-1]
    return jnp.where(even, pltpu.roll(x, n - 1, 1), pltpu.roll(x, 1, 1))


def _att_prep_fwd(U, cos, sin, qw, kw, *, name, tm=512):
    T = U.shape[0]
    scale = ATT_DH ** -0.5

    def body(aq_ref, ak_ref, av_ref, c_ref, s_ref, qw_ref, kw_ref, q_ref, k_ref, v_ref):
        bd = _head_blockdiag(ATT_QW)
        c2, s2 = c_ref[...], s_ref[...]
        c8, s8 = jnp.tile(c2, (1, 4)), jnp.tile(s2, (1, 4))

        def norm_rope(x, w, c, s, bdm):
            r = lax.rsqrt(_head_sum(x * x, bdm) * (1.0 / ATT_DH) + EPS)
            y = x * r * w
            even = (lax.broadcasted_iota(jnp.int32, y.shape, 1) & 1) == 0
            return y * c + _pair_swap(y, even) * s

        q_ref[...] = (norm_rope(aq_ref[...], qw_ref[...], c8, s8, bd) * scale).astype(BF16)
        k_ref[...] = norm_rope(ak_ref[...], kw_ref[...], c2, s2, bd[:ATT_KW, :ATT_KW]).astype(BF16)
        v_ref[...] = av_ref[...].astype(BF16)

    kv_spec = pl.BlockSpec((tm, ATT_KW), lambda i: (i, 0))
    return pl.pallas_call(
        body, name=name, grid=(T // tm,),
        in_specs=[pl.BlockSpec((tm, ATT_QW), lambda i: (i, 5)),
                  pl.BlockSpec((tm, ATT_KW), lambda i: (i, 24)), pl.BlockSpec((tm, ATT_KW), lambda i: (i, 25)),
                  kv_spec, kv_spec,
                  pl.BlockSpec((1, ATT_QW), lambda i: (0, 0)), pl.BlockSpec((1, ATT_KW), lambda i: (0, 0))],
        out_specs=[pl.BlockSpec((tm, ATT_QW), lambda i: (i, 0)), kv_spec, kv_spec],
        out_shape=[jax.ShapeDtypeStruct((T, ATT_QW), BF16), jax.ShapeDtypeStruct((T, ATT_KW), BF16),
                   jax.ShapeDtypeStruct((T, ATT_KW), BF16)],
        compiler_params=_params(("parallel",)),
    )(U, U, U, cos, sin, qw, kw)


def _att_prep_bwd(U, dq, dk, cos, sin, qw, kw, *, name, tm=512):
    T = U.shape[0]
    scale = ATT_DH ** -0.5

    def body(aq_ref, ak_ref, dq_ref, dk_ref, c_ref, s_ref, qw_ref, kw_ref, daq_ref, dak_ref, dqw_ref, dkw_ref):
        @pl.when(pl.program_id(0) == 0)
        def _():
            dqw_ref[...] = jnp.zeros_like(dqw_ref)
            dkw_ref[...] = jnp.zeros_like(dkw_ref)

        bd = _head_blockdiag(ATT_QW)
        c2, s2 = c_ref[...], s_ref[...]
        c8, s8 = jnp.tile(c2, (1, 4)), jnp.tile(s2, (1, 4))

        def bwd(x, dy, w, c, s, bdm):
            even = (lax.broadcasted_iota(jnp.int32, x.shape, 1) & 1) == 0
            dn = dy * c - _pair_swap(dy, even) * s
            r = lax.rsqrt(_head_sum(x * x, bdm) * (1.0 / ATT_DH) + EPS)
            xh = x * r
            dxh = dn * w
            t = _head_sum(dxh * xh, bdm) * (1.0 / ATT_DH)
            return r * (dxh - xh * t), jnp.sum(dn * xh, axis=0, keepdims=True)

        da, dw = bwd(aq_ref[...], dq_ref[...] * scale, qw_ref[...], c8, s8, bd)
        daq_ref[...] = da
        dqw_ref[...] += dw
        da, dw = bwd(ak_ref[...], dk_ref[...], kw_ref[...], c2, s2, bd[:ATT_KW, :ATT_KW])
        dak_ref[...] = da
        dkw_ref[...] += dw

    q_spec = pl.BlockSpec((tm, ATT_QW), lambda i: (i, 0))
    kv_spec = pl.BlockSpec((tm, ATT_KW), lambda i: (i, 0))
    qv = pl.BlockSpec((1, ATT_QW), lambda i: (0, 0))
    kv = pl.BlockSpec((1, ATT_KW), lambda i: (0, 0))
    return pl.pallas_call(
        body, name=name, grid=(T // tm,),
        in_specs=[pl.BlockSpec((tm, ATT_QW), lambda i: (i, 5)), pl.BlockSpec((tm, ATT_KW), lambda i: (i, 24)),
                  q_spec, kv_spec, kv_spec, kv_spec, qv, kv],
        out_specs=[q_spec, kv_spec, qv, kv],
        out_shape=[jax.ShapeDtypeStruct((T, ATT_QW), F32), jax.ShapeDtypeStruct((T, ATT_KW), F32),
                   jax.ShapeDtypeStruct((1, ATT_QW), F32), jax.ShapeDtypeStruct((1, ATT_KW), F32)],
        compiler_params=_params(("arbitrary",)),
    )(U, U, dq, dk, cos, sin, qw, kw)


FA_TQ = 256
FA_TK = 512


def _fa_tiles(T):
    tq, tk = min(FA_TQ, T), min(FA_TK, T)
    return tq, tk, T // tq, T // tk


def _to_fa_cols(a, T):
    tq, _, nq, _ = _fa_tiles(T)
    return a.reshape(nq, tq, ATT_KV, ATT_G, ATT_DH).transpose(2, 0, 4, 3, 1).reshape(ATT_KV, nq, ATT_DH, ATT_G * tq)


def _to_fa_rows(a, T):
    tq, _, nq, _ = _fa_tiles(T)
    return a.reshape(nq, tq, ATT_KV, ATT_G, ATT_DH).transpose(2, 0, 3, 1, 4).reshape(ATT_KV, nq, ATT_G * tq, ATT_DH)


def _from_fa_cols(a, T):
    tq, _, nq, _ = _fa_tiles(T)
    return a.reshape(ATT_KV, nq, ATT_DH, ATT_G, tq).transpose(1, 4, 0, 3, 2).reshape(T, ATT_QW)


def _kv_rows(a, T):
    _, tk, _, n_k = _fa_tiles(T)
    return a.reshape(n_k, tk, ATT_KV, ATT_DH).transpose(2, 0, 1, 3)


def _kv_cols(a, T):
    _, tk, _, n_k = _fa_tiles(T)
    return a.reshape(n_k, tk, ATT_KV, ATT_DH).transpose(2, 0, 3, 1)


def _flash_fwd(q_c, k_r, v_c, *, name):
    _, nq, _, R = q_c.shape
    _, n_k, tk, _ = k_r.shape

    def body(q_ref, k_ref, v_ref, o_ref, lse_ref, acc_ref):
        qv = q_ref[0, 0]
        acc_ref[...] = jnp.zeros_like(acc_ref)

        def step(j, carry):
            m, l = carry
            s = jnp.dot(k_ref[0, j], qv, preferred_element_type=F32)
            m_new = jnp.maximum(m, jnp.max(s, axis=0, keepdims=True))
            alpha = jnp.exp(m - m_new)
            p = jnp.exp(s - m_new)
            l = alpha * l + jnp.sum(p, axis=0, keepdims=True)
            acc_ref[...] = alpha * acc_ref[...] + jnp.dot(v_ref[0, j], p.astype(BF16), preferred_element_type=F32)
            return m_new, l

        m, l = lax.fori_loop(0, n_k, step, (jnp.full((1, R), -jnp.inf, F32), jnp.zeros((1, R), F32)))
        o_ref[0, 0] = acc_ref[...] / l
        lse_ref[0, 0] = m + jnp.log(l)

    qspec = pl.BlockSpec((1, 1, ATT_DH, R), lambda h, i: (h, i, 0, 0))
    return pl.pallas_call(
        body, name=name, grid=(ATT_KV, nq),
        in_specs=[qspec, pl.BlockSpec((1, n_k, tk, ATT_DH), lambda h, i: (h, 0, 0, 0)),
                  pl.BlockSpec((1, n_k, ATT_DH, tk), lambda h, i: (h, 0, 0, 0))],
        out_specs=[qspec, pl.BlockSpec((1, 1, 1, R), lambda h, i: (h, i, 0, 0))],
        out_shape=[jax.ShapeDtypeStruct((ATT_KV, nq, ATT_DH, R), F32), jax.ShapeDtypeStruct((ATT_KV, nq, 1, R), F32)],
        scratch_shapes=[pltpu.VMEM((ATT_DH, R), F32)],
        compiler_params=_params(("parallel", "parallel")),
    )(q_c, k_r, v_c)


def _flash_bwd(q_c, q_r, k_r, k_c, v_r, do_c, do_r, o_c, lse, *, name):
    _, nq, _, R = q_c.shape
    _, n_k, tk, _ = k_r.shape

    def body(qc_ref, qr_ref, kr_ref, kc_ref, vr_ref, doc_ref, dor_ref, oc_ref, lse_ref, dq_ref, dk_ref, dv_ref,
             acc_ref):
        @pl.when(pl.program_id(1) == 0)
        def _():
            dk_ref[...] = jnp.zeros_like(dk_ref)
            dv_ref[...] = jnp.zeros_like(dv_ref)

        qc, doc = qc_ref[0, 0], doc_ref[0, 0]
        qr, dor = qr_ref[0, 0], dor_ref[0, 0]
        delta = jnp.sum(doc.astype(F32) * oc_ref[0, 0], axis=0, keepdims=True)
        lsev = lse_ref[0, 0]
        acc_ref[...] = jnp.zeros_like(acc_ref)

        def step(j, carry):
            s = jnp.dot(kr_ref[0, j], qc, preferred_element_type=F32)
            p = jnp.exp(s - lsev)
            dp = jnp.dot(vr_ref[0, j], doc, preferred_element_type=F32)
            ds = (p * (dp - delta)).astype(BF16)
            acc_ref[...] += jnp.dot(kc_ref[0, j], ds, preferred_element_type=F32)
            dk_ref[0, j] += jnp.dot(ds, qr, preferred_element_type=F32)
            dv_ref[0, j] += jnp.dot(p.astype(BF16), dor, preferred_element_type=F32)
            return carry

        lax.fori_loop(0, n_k, step, 0)
        dq_ref[0, 0] = acc_ref[...]

    cspec = pl.BlockSpec((1, 1, ATT_DH, R), lambda h, i: (h, i, 0, 0))
    rspec = pl.BlockSpec((1, 1, R, ATT_DH), lambda h, i: (h, i, 0, 0))
    krspec = pl.BlockSpec((1, n_k, tk, ATT_DH), lambda h, i: (h, 0, 0, 0))
    kcspec = pl.BlockSpec((1, n_k, ATT_DH, tk), lambda h, i: (h, 0, 0, 0))
    return pl.pallas_call(
        body, name=name, grid=(ATT_KV, nq),
        in_specs=[cspec, rspec, krspec, kcspec, krspec, cspec, rspec, cspec,
                  pl.BlockSpec((1, 1, 1, R), lambda h, i: (h, i, 0, 0))],
        out_specs=[cspec, krspec, krspec],
        out_shape=[jax.ShapeDtypeStruct((ATT_KV, nq, ATT_DH, R), F32),
                   jax.ShapeDtypeStruct((ATT_KV, n_k, tk, ATT_DH), F32),
                   jax.ShapeDtypeStruct((ATT_KV, n_k, tk, ATT_DH), F32)],
        scratch_shapes=[pltpu.VMEM((ATT_DH, R), F32)],
        compiler_params=_params(("parallel", "arbitrary")),
    )(q_c, q_r, k_r, k_c, v_r, do_c, do_r, o_c, lse)


def _att_post_fwd(o, w, *, name, tm=512):
    T = o.shape[0]

    def body(o_ref, w_ref, out_ref):
        ov = o_ref[...]
        r = lax.rsqrt(jnp.mean(ov * ov, axis=-1, keepdims=True) + EPS)
        out_ref[...] = (ov * r * w_ref[...]).astype(BF16)

    tok = pl.BlockSpec((tm, ATT_QW), lambda i: (i, 0))
    return pl.pallas_call(
        body, name=name, grid=(T // tm,),
        in_specs=[tok, pl.BlockSpec((1, ATT_QW), lambda i: (0, 0))],
        out_specs=tok, out_shape=jax.ShapeDtypeStruct((T, ATT_QW), BF16),
        compiler_params=_params(("parallel",)),
    )(o, w)


def _att_post_bwd(dmix, o, w, *, name, tm=512):
    T = o.shape[0]

    def body(dm_ref, o_ref, w_ref, do_ref, dw_ref):
        @pl.when(pl.program_id(0) == 0)
        def _():
            dw_ref[...] = jnp.zeros_like(dw_ref)

        ov = o_ref[...]
        r = lax.rsqrt(jnp.mean(ov * ov, axis=-1, keepdims=True) + EPS)
        xh = ov * r
        dm = dm_ref[...]
        dxh = dm * w_ref[...]
        t = jnp.mean(dxh * xh, axis=-1, keepdims=True)
        do_ref[...] = (r * (dxh - xh * t)).astype(BF16)
        dw_ref[...] += jnp.sum(dm * xh, axis=0, keepdims=True)

    tok = pl.BlockSpec((tm, ATT_QW), lambda i: (i, 0))
    vec = pl.BlockSpec((1, ATT_QW), lambda i: (0, 0))
    return pl.pallas_call(
        body, name=name, grid=(T // tm,),
        in_specs=[pl.BlockSpec((tm, ATT_QW), lambda i: (i, 1)), tok, vec],
        out_specs=[tok, vec],
        out_shape=[jax.ShapeDtypeStruct((T, ATT_QW), BF16), jax.ShapeDtypeStruct((1, ATT_QW), F32)],
        compiler_params=_params(("arbitrary",)),
    )(dmix, o, w)


def _ffn_up(h2, wg, wu, *, name, tm=512):
    T = h2.shape[0]
    tn = _pick(D_FF, 1408)

    def body(h_ref, wg_ref, wu_ref, g_ref, u_ref, a_ref):
        hv = h_ref[...]
        g = jnp.dot(hv, wg_ref[...], preferred_element_type=F32)
        u = jnp.dot(hv, wu_ref[...], preferred_element_type=F32)
        g_ref[...] = g.astype(BF16)
        u_ref[...] = u.astype(BF16)
        a_ref[...] = (g * _sigmoid(g) * u).astype(BF16)

    wspec = pl.BlockSpec((D_MODEL, tn), lambda i, j: (0, j))
    ospec = pl.BlockSpec((tm, tn), lambda i, j: (i, j))
    return pl.pallas_call(
        body, name=name, grid=(T // tm, D_FF // tn),
        in_specs=[pl.BlockSpec((tm, D_MODEL), lambda i, j: (i, 0)), wspec, wspec],
        out_specs=[ospec] * 3, out_shape=[jax.ShapeDtypeStruct((T, D_FF), BF16)] * 3,
        compiler_params=_params(("parallel", "arbitrary")),
    )(h2, wg, wu)


def _ffn_act_bwd(dx2b, wd_t, gate, up, *, name, tm=512):
    T = dx2b.shape[0]
    tn = _pick(D_FF, 1408)

    def body(dx_ref, w_ref, g_ref, u_ref, dg_ref, du_ref):
        da = jnp.dot(dx_ref[...], w_ref[...], preferred_element_type=F32)
        g = g_ref[...].astype(F32)
        u = u_ref[...].astype(F32)
        sg = _sigmoid(g)
        dg_ref[...] = (da * u * (sg * (1.0 + g * (1.0 - sg)))).astype(BF16)
        du_ref[...] = (da * (g * sg)).astype(BF16)

    ospec = pl.BlockSpec((tm, tn), lambda i, j: (i, j))
    return pl.pallas_call(
        body, name=name, grid=(T // tm, D_FF // tn),
        in_specs=[pl.BlockSpec((tm, D_MODEL), lambda i, j: (i, 0)),
                  pl.BlockSpec((D_MODEL, tn), lambda i, j: (0, j)), ospec, ospec],
        out_specs=[ospec] * 2, out_shape=[jax.ShapeDtypeStruct((T, D_FF), BF16)] * 2,
        compiler_params=_params(("parallel", "arbitrary")),
    )(dx2b, wd_t, gate, up)


def _assemble_du(U, dq_f, dq_b, dz_f, dz_b, dv_f, dv_b, du_g, da_q, da_k, da_v, *, name, tm=256):
    T = U.shape[0]

    def body(uq_ref, dqf, dqb, dzf, dzb, dvf, dvb, dug, daq, dak, dav, out_ref):
        uq = uq_ref[...]
        sg = _sigmoid(uq)
        out_ref[:, 0:HG_W] = ((dqf[...] + dqb[...]) * (sg * (1.0 + uq * (1.0 - sg)))).astype(BF16)
        out_ref[:, HG_W:2 * HG_W] = dzf[...].astype(BF16)
        out_ref[:, 2 * HG_W:3 * HG_W] = dzb[...].astype(BF16)
        out_ref[:, 3 * HG_W:4 * HG_W] = (dvf[...] + dvb[...]).astype(BF16)
        out_ref[:, 4 * HG_W:5 * HG_W] = dug[...].astype(BF16)
        out_ref[:, 5 * HG_W:5 * HG_W + ATT_QW] = daq[...].astype(BF16)
        out_ref[:, 5 * HG_W + ATT_QW:5 * HG_W + ATT_QW + ATT_KW] = dak[...].astype(BF16)
        out_ref[:, 5 * HG_W + ATT_QW + ATT_KW:D_IN] = dav[...].astype(BF16)

    tok = pl.BlockSpec((tm, HG_W), lambda i: (i, 0))
    kv = pl.BlockSpec((tm, ATT_KW), lambda i: (i, 0))
    return pl.pallas_call(
        body, name=name, grid=(T // tm,),
        in_specs=[tok] * 9 + [kv, kv],
        out_specs=pl.BlockSpec((tm, D_IN), lambda i: (i, 0)),
        out_shape=jax.ShapeDtypeStruct((T, D_IN), BF16),
        compiler_params=_params(("parallel",)),
    )(U, dq_f, dq_b, dz_f, dz_b, dv_f, dv_b, du_g, da_q, da_k, da_v)


def _adam_math(w, g, m, v):
    m = ADAM_B1 * m + (1.0 - ADAM_B1) * g
    v = ADAM_B2 * v + (1.0 - ADAM_B2) * (g * g)
    m_hat = m / (1.0 - ADAM_B1 ** ADAM_STEP)
    v_hat = v / (1.0 - ADAM_B2 ** ADAM_STEP)
    delta = -ADAM_LR * (m_hat / (jnp.sqrt(v_hat) + ADAM_EPS) + ADAM_WD * w)
    return delta, m, v


def _adamw(parts, w, m, v, *, name, tr_cap=256):
    P, R, C = parts.shape
    tr = R
    for t in range(8, min(R, tr_cap) + 1, 8):
        if R % t == 0:
            tr = t

    def body(p_ref, w_ref, m_ref, v_ref, g_ref, d_ref, nm_ref, nv_ref):
        g = p_ref[0].astype(F32)
        for j in range(1, P):
            g = g + p_ref[j].astype(F32)
        d, nm, nv = _adam_math(w_ref[...], g, m_ref[...], v_ref[...])
        g_ref[...] = g
        d_ref[...] = d
        nm_ref[...] = nm
        nv_ref[...] = nv

    blk = pl.BlockSpec((tr, C), lambda i: (i, 0))
    return pl.pallas_call(
        body, name=name, grid=(R // tr,),
        in_specs=[pl.BlockSpec((P, tr, C), lambda i: (0, i, 0)), blk, blk, blk],
        out_specs=[blk] * 4, out_shape=[jax.ShapeDtypeStruct((R, C), F32)] * 4,
        compiler_params=_params(("parallel",)),
    )(parts, w, m, v)


def _all_gather(xs, *, name):
    n = len(xs)

    def body(*refs):
        ins, outs = refs[:n], refs[n:2 * n]
        send_sems, recv_sems, local_sems = refs[2 * n:]
        x, y, c = lax.axis_index("x"), lax.axis_index("y"), lax.axis_index("c")
        me, sibling = (x, y, c), (x, y, 1 - c)
        chips = [(1 - x, y), (x, 1 - y), (1 - x, 1 - y)]

        def slot(p):
            return 4 * p[0] + 2 * p[1] + p[2]

        def copy(a, k, block, to, src=None):
            dst = outs[a].at[slot(block)]
            return pltpu.make_async_remote_copy(
                src_ref=dst if src is None else src, dst_ref=dst,
                send_sem=send_sems.at[a * 7 + k], recv_sem=recv_sems.at[a * 7 + k],
                device_id=to, device_id_type=MESH)

        mine = [pltpu.make_async_copy(ins[a], outs[a].at[slot(me)], local_sems.at[a]) for a in range(n)]
        for cp in mine:
            cp.start()
        first = []
        for a in range(n):
            first.append(copy(a, 0, me, sibling, src=ins[a]))
            first += [copy(a, 1 + j, me, (*chip, c), src=ins[a]) for j, chip in enumerate(chips)]
        for cp in first:
            cp.start()
        passed = []
        for j, chip in enumerate(chips):
            for a in range(n):
                copy(a, 1 + j, (*chip, c), me).wait_recv()
                cp = copy(a, 4 + j, (*chip, c), sibling)
                cp.start()
                passed.append(cp)
        for a in range(n):
            copy(a, 0, sibling, me).wait_recv()
            for j, chip in enumerate(chips):
                copy(a, 4 + j, (*chip, 1 - c), me).wait_recv()
        for cp in first + passed:
            cp.wait_send()
        for cp in mine:
            cp.wait()

    return pl.pallas_call(
        body, name=name,
        in_specs=[ANY] * n, out_specs=[ANY] * n,
        out_shape=[jax.ShapeDtypeStruct((N_DEV,) + x.shape, x.dtype) for x in xs],
        scratch_shapes=[pltpu.SemaphoreType.DMA((7 * n,)), pltpu.SemaphoreType.DMA((7 * n,)),
                        pltpu.SemaphoreType.DMA((n,))],
        compiler_params=pltpu.CompilerParams(has_side_effects=True),
    )(*xs)


def _exchange(gs, *, masks, slot, name):
    n, n_peers = len(gs), len(masks)

    def body(*refs):
        ins, outs = refs[:n], refs[n:2 * n]
        send_sems, recv_sems, local_sems = refs[2 * n:]
        x, y, c = lax.axis_index("x"), lax.axis_index("y"), lax.axis_index("c")
        my_slot = slot((x, y, c))

        def flip(v, bit):
            return 1 - v if bit else v

        mine = [pltpu.make_async_copy(ins[a].at[my_slot], outs[a].at[my_slot], local_sems.at[a]) for a in range(n)]
        for cp in mine:
            cp.start()
        copies = []
        for a in range(n):
            for k, (mx, my, mc) in enumerate(masks):
                peer = (flip(x, mx), flip(y, my), flip(c, mc))
                peer_slot = slot(peer)
                sems = dict(send_sem=send_sems.at[a * n_peers + k], recv_sem=recv_sems.at[a * n_peers + k],
                            device_id=peer, device_id_type=MESH)
                copies.append((
                    pltpu.make_async_remote_copy(src_ref=ins[a].at[peer_slot], dst_ref=outs[a].at[my_slot], **sems),
                    pltpu.make_async_remote_copy(src_ref=ins[a].at[peer_slot], dst_ref=outs[a].at[peer_slot], **sems)))
        for send, _ in copies:
            send.start()
        for send, recv in copies:
            recv.wait_recv()
            send.wait_send()
        for cp in mine:
            cp.wait()

    return pl.pallas_call(
        body, name=name,
        in_specs=[ANY] * n, out_specs=[ANY] * n,
        out_shape=[jax.ShapeDtypeStruct(g.shape, g.dtype) for g in gs],
        scratch_shapes=[pltpu.SemaphoreType.DMA((n_peers * n,)), pltpu.SemaphoreType.DMA((n_peers * n,)),
                        pltpu.SemaphoreType.DMA((n,))],
        compiler_params=pltpu.CompilerParams(has_side_effects=True),
    )(*gs)


def _pair_sum(g, *, name, tr_cap=256):
    _, Q, R, C = g.shape
    tr = max(t for t in range(16, min(R, tr_cap) + 1, 16) if R % t == 0)

    def body(g_ref, o_ref):
        o_ref[0] = (g_ref[0, 0] + g_ref[1, 0]).astype(BF16)

    return pl.pallas_call(
        body, name=name, grid=(Q, R // tr),
        in_specs=[pl.BlockSpec((2, 1, tr, C), lambda q, i: (0, q, i, 0))],
        out_specs=pl.BlockSpec((1, tr, C), lambda q, i: (q, i, 0)),
        out_shape=jax.ShapeDtypeStruct((Q, R, C), BF16),
        compiler_params=_params(("parallel", "parallel")),
    )(g)


PACK_ROWS = 8


def _pack_small(norm1, norm2, final, att, hg, qn, kn, lb=None):
    z = lambda n: jnp.zeros((n,), F32)
    rows = [norm1.reshape(-1), norm2.reshape(-1), final.reshape(-1),
            jnp.concatenate([att.reshape(-1), z(512)]),
            jnp.concatenate([hg.reshape(-1), qn.reshape(-1), kn.reshape(-1), z(1024 - 256)]),
            z(1024) if lb is None else lb.reshape(-1), z(1024), z(1024)]
    return jnp.stack(rows, axis=0)


def _unpack_small(p):
    return (p[0:1, :], p[1:2, :], p[2, :], p[3:4, 0:512], p[4:5, 0:128], p[4:5, 128:192], p[4:5, 192:256])


def _fold_heads(dhg, dqn, dkn, *, name):
    def body(hg_ref, q_ref, k_ref, ohg_ref, oq_ref, ok_ref):
        def fold128(v):
            acc = v[:, 0:LANES]
            for j in range(1, v.shape[1] // LANES):
                acc = acc + v[:, j * LANES:(j + 1) * LANES]
            return acc

        ohg_ref[...] = fold128(hg_ref[...])
        q = fold128(q_ref[...])
        oq_ref[...] = q + pltpu.roll(q, ATT_DH, 1)
        k = k_ref[...]
        ok_ref[...] = k + pltpu.roll(k, ATT_DH, 1)

    return pl.pallas_call(body, name=name, out_shape=[jax.ShapeDtypeStruct((1, LANES), F32)] * 3)(dhg, dqn, dkn)


def _lb_grad(dlb_sum, lb, *, name):
    def body(d_ref, lb_ref, o_ref):
        lbv = lb_ref[...]
        gl = d_ref[...] * lbv * (1.0 - lbv)
        o_ref[0:1, :] = gl[0:1, :]
        o_ref[1:2, :] = -gl[0:1, :]
        o_ref[2:3, :] = gl[1:2, :]
        o_ref[3:4, :] = -gl[1:2, :]

    return pl.pallas_call(body, name=name, out_shape=jax.ShapeDtypeStruct((4, HG_W), F32))(dlb_sum, lb)


def _lower_bounds(lb_logits_full, *, name):
    def body(l_ref, o_ref):
        for d in range(2):
            l0, l1 = l_ref[2 * d:2 * d + 1, :], l_ref[2 * d + 1:2 * d + 2, :]
            mx = jnp.maximum(l0, l1)
            e0, e1 = jnp.exp(l0 - mx), jnp.exp(l1 - mx)
            o_ref[d:d + 1, :] = e0 / (e0 + e1)

    return pl.pallas_call(body, name=name, out_shape=jax.ShapeDtypeStruct((2, HG_W), F32))(
        lb_logits_full.reshape(4, HG_W))


def _local_step(x, target, norm1_w, w_in, lb, hg_norm_w, q_norm_w, k_norm_w, att_norm_w, w_out, norm2_w,
                w_g, w_u, w_down, final_norm_w):
    T = x.shape[0]
    cos, sin = _rope_tables(T)
    qw8 = jnp.tile(q_norm_w, (1, ATT_HEADS))
    kw2 = jnp.tile(k_norm_w, (1, ATT_KV))

    h, r1 = _rms_fwd(x, norm1_w, name="norm1_fwd")
    U = _mm_nn([(h, w_in)], name="in_proj")
    o_f, st_f = _gla_fwd(U, lb[0:1], f_block=1, reverse=False, name="gla_fwd_f")
    o_b, st_b = _gla_fwd(U, lb[1:2], f_block=2, reverse=True, name="gla_fwd_b")
    mix_hg = _hg_post_fwd(o_f, o_b, U, hg_norm_w, name="hg_post_fwd")
    q, k, v = _att_prep_fwd(U, cos, sin, qw8, kw2, name="att_prep_fwd")
    q_c, k_r = _to_fa_cols(q, T), _kv_rows(k, T)
    o_c, lse = _flash_fwd(q_c, k_r, _kv_cols(v, T), name="flash_fwd")
    o_att = _from_fa_cols(o_c, T)
    mix_att = _att_post_fwd(o_att, att_norm_w, name="att_post_fwd")
    mix = jnp.concatenate([mix_hg, mix_att], axis=1)
    x1 = _mm_nn([(mix, w_out)], residual=x, name="out_proj")
    h2, r2 = _rms_fwd(x1, norm2_w, name="norm2_fwd")
    gate, up, act = _ffn_up(h2, w_g, w_u, name="ffn_up")
    x2 = _mm_nn([(act, w_down)], residual=x1, name="ffn_down")
    loss, dx2, dx2b, d_final = _loss_head(x2, target, final_norm_w.reshape(1, D_MODEL), name="loss_head")

    d_gate, d_up = _ffn_act_bwd(dx2b, w_down.T, gate, up, name="ffn_act_bwd")
    dw_down = _mm_tn(act, dx2b, tma_cap=1408, name="dw_down")
    dh2 = _mm_nn([(d_gate, w_g.T), (d_up, w_u.T)], tm=256, name="ffn_up_bwd")
    dw_g = _mm_tn(h2, d_gate, tnb_cap=1408, name="dw_gate")
    dw_u = _mm_tn(h2, d_up, tnb_cap=1408, name="dw_up")
    dx1, dx1b, d_norm2 = _rms_bwd(dh2, x1, r2, norm2_w, dx2, emit_bf16=True, name="norm2_bwd")
    dmix = _mm_nn([(dx1b, w_out.T)], name="out_proj_bwd")
    dw_out = _mm_tn(mix, dx1b, name="dw_out")
    do_att, d_att = _att_post_bwd(dmix, o_att, att_norm_w, name="att_post_bwd")
    dq_c, dk_r, dv_r = _flash_bwd(q_c, _to_fa_rows(q, T), k_r, _kv_cols(k, T), _kv_rows(v, T),
                                  _to_fa_cols(do_att, T), _to_fa_rows(do_att, T), o_c, lse, name="flash_bwd")
    dq = _from_fa_cols(dq_c, T)
    dk = dk_r.transpose(1, 2, 0, 3).reshape(T, ATT_KW)
    da_v = dv_r.transpose(1, 2, 0, 3).reshape(T, ATT_KW)
    da_q, da_k, d_qn, d_kn = _att_prep_bwd(U, dq, dk, cos, sin, qw8, kw2, name="att_prep_bwd")
    do_hg, du_g, d_hg = _hg_post_bwd(dmix, o_f, o_b, U, hg_norm_w, name="hg_post_bwd")
    dq_f, dz_f, dv_f, dlb_f = _gla_bwd(U, lb[0:1], do_hg, st_f, f_block=1, reverse=False, name="gla_bwd_f")
    dq_b, dz_b, dv_b, dlb_b = _gla_bwd(U, lb[1:2], do_hg, st_b, f_block=2, reverse=True, name="gla_bwd_b")
    dU = _assemble_du(U, dq_f, dq_b, dz_f, dz_b, dv_f, dv_b, du_g, da_q, da_k, da_v, name="assemble_du")
    dh = _mm_nn([(dU, w_in.T)], name="in_proj_bwd")
    dw_in = _mm_tn(h, dU, tnb_cap=1664, name="dw_in")
    grad_x, d_norm1 = _rms_bwd(dh, x, r1, norm1_w, dx1, emit_bf16=False, name="norm1_bwd")
    d_hg, d_qn, d_kn = _fold_heads(d_hg, d_qn, d_kn, name="fold_heads")

    big = dict(w_in=dw_in, w_out=dw_out, w_g=dw_g, w_u=dw_u, w_down=dw_down)
    small = dict(norm1=d_norm1, norm2=d_norm2, final=d_final, att=d_att, hg=d_hg,
                 qn=d_qn[:, :ATT_DH], kn=d_kn[:, :ATT_DH], lb=jnp.concatenate([dlb_f, dlb_b], axis=0))
    return loss, grad_x, big, small


def kernel(x, norm1_w, w_in, lb_logits, hg_norm_w, q_norm_w, k_norm_w, att_norm_w, w_out, norm2_w, w_gate_up, w_down, final_norm_w, loss_target, m_norm1_w, m_w_in, m_lb_logits, m_hg_norm_w, m_q_norm_w, m_k_norm_w, m_att_norm_w, m_w_out, m_norm2_w, m_w_gate_up, m_w_down, m_final_norm_w, v_norm1_w, v_w_in, v_lb_logits, v_hg_norm_w, v_q_norm_w, v_k_norm_w, v_att_norm_w, v_w_out, v_norm2_w, v_w_gate_up, v_w_down, v_final_norm_w):
    T = x.shape[1]
    me = 4 * lax.axis_index("x") + 2 * lax.axis_index("y") + lax.axis_index("c")
    c_in, r_out, c_gu, r_dn = w_in.shape[2], w_out.shape[1], w_gate_up.shape[2], w_down.shape[1]
    lb_cols = lb_logits.shape[2]

    g_in, g_out, g_gu, g_dn, g_lb = _all_gather(
        [w_in[0].astype(BF16), w_out[0].astype(BF16), w_gate_up[0].astype(BF16), w_down[0].astype(BF16),
         lb_logits.reshape(4, lb_cols)], name="gather_weights")
    w_in_f = g_in.transpose(1, 0, 2).reshape(D_MODEL, N_DEV * c_in)
    w_out_f = g_out.reshape(N_DEV * r_out, D_MODEL)
    half = N_DEV // 2
    w_g_f = g_gu[:half].transpose(1, 0, 2).reshape(D_MODEL, half * c_gu)
    w_u_f = g_gu[half:].transpose(1, 0, 2).reshape(D_MODEL, half * c_gu)
    w_dn_f = g_dn.reshape(N_DEV * r_dn, D_MODEL)
    lb_logits_f = g_lb.transpose(1, 0, 2).reshape(2, 2, N_DEV * lb_cols)
    lb = _lower_bounds(lb_logits_f, name="lower_bounds")

    loss, grad_x, big, small = _local_step(
        x[0], loss_target[0], norm1_w, w_in_f, lb, hg_norm_w, q_norm_w, k_norm_w, att_norm_w, w_out_f, norm2_w,
        w_g_f, w_u_f, w_dn_f, final_norm_w)

    chips = N_DEV // 2
    by_owner_cols = lambda g, n_q, w: g.reshape(D_MODEL, n_q, 2, w).transpose(2, 1, 0, 3)
    by_owner_rows = lambda g, r: g.reshape(chips, 2, r, D_MODEL).transpose(1, 0, 2, 3)
    s_in = by_owner_cols(big["w_in"], chips, c_in)
    s_out = by_owner_rows(big["w_out"], r_out)
    s_gu = jnp.concatenate([by_owner_cols(big["w_g"], chips // 2, c_gu), by_owner_cols(big["w_u"], chips // 2, c_gu)],
                           axis=1)
    s_dn = by_owner_rows(big["w_down"], r_dn)
    pairs = _exchange([s_in, s_out, s_gu, s_dn], masks=[(0, 0, 1)], slot=lambda p: p[2], name="exchange_cores")
    chip_sums = [_pair_sum(g, name="pair_sum_" + nm) for g, nm in zip(pairs, ("w_in", "w_out", "w_gu", "w_down"))]
    p_in, p_out, p_gu, p_dn = _exchange(chip_sums, masks=[(1, 0, 0), (0, 1, 0), (1, 1, 0)],
                                        slot=lambda p: 2 * p[0] + p[1], name="exchange_chips")

    packed = _pack_small(small["norm1"], small["norm2"], small["final"], small["att"], small["hg"],
                         small["qn"], small["kn"], small["lb"])
    (all_small,) = _all_gather([packed], name="gather_small_grads")

    g_w_in, d_w_in, nm_w_in, nv_w_in = _adamw(p_in, w_in[0], m_w_in[0], v_w_in[0], name="adamw_w_in")
    g_w_out, d_w_out, nm_w_out, nv_w_out = _adamw(p_out, w_out[0], m_w_out[0], v_w_out[0], name="adamw_w_out")
    g_w_gu, d_w_gu, nm_w_gu, nv_w_gu = _adamw(p_gu, w_gate_up[0], m_w_gate_up[0], v_w_gate_up[0], name="adamw_w_gu")
    g_w_dn, d_w_dn, nm_w_dn, nv_w_dn = _adamw(p_dn, w_down[0], m_w_down[0], v_w_down[0], name="adamw_w_down")

    pk = lambda vecs: _pack_small(*vecs)
    w_pk = pk([norm1_w, norm2_w, final_norm_w, att_norm_w, hg_norm_w, q_norm_w, k_norm_w])
    m_pk = pk([m_norm1_w, m_norm2_w, m_final_norm_w, m_att_norm_w, m_hg_norm_w, m_q_norm_w, m_k_norm_w])
    v_pk = pk([v_norm1_w, v_norm2_w, v_final_norm_w, v_att_norm_w, v_hg_norm_w, v_q_norm_w, v_k_norm_w])
    g_pk, d_pk, nm_pk, nv_pk = _adamw(all_small, w_pk, m_pk, v_pk, name="adamw_small")

    dlb_sum = g_pk[5:6, :].reshape(2, HG_W)
    g_lb_full = _lb_grad(dlb_sum, lb, name="lb_grad")
    g_lb_mine = lax.dynamic_slice_in_dim(g_lb_full, me * lb_cols, lb_cols, axis=1)
    g_lb_s, d_lb, nm_lb, nv_lb = _adamw(g_lb_mine[None], lb_logits.reshape(4, lb_cols),
                                        m_lb_logits.reshape(4, lb_cols), v_lb_logits.reshape(4, lb_cols),
                                        name="adamw_lb")

    loss_total = lax.psum(loss[0, 0], ("x", "y", "c"))

    def outs(big4, lb_arr, pk_arr):
        n1, n2, fin, att, hg, qn, kn = _unpack_small(pk_arr)
        b_in, b_out, b_gu, b_dn = big4
        return [n1, b_in[None], lb_arr.reshape(2, 2, lb_cols), hg, qn, kn, att, b_out[None], n2, b_gu[None],
                b_dn[None], fin]

    return (loss_total, grad_x[None],
            *outs((g_w_in, g_w_out, g_w_gu, g_w_dn), g_lb_s, g_pk),
            *outs((d_w_in, d_w_out, d_w_gu, d_w_dn), d_lb, d_pk),
            *outs((nm_w_in, nm_w_out, nm_w_gu, nm_w_dn), nm_lb, nm_pk),
            *outs((nv_w_in, nv_w_out, nv_w_gu, nv_w_dn), nv_lb, nv_pk))
```

```python
import functools
import math

import jax
import jax.numpy as jnp
import numpy as np
from jax import lax
from jax.experimental import pallas as pl
from jax.experimental.pallas import tpu as pltpu

F32 = jnp.float32
BF16 = jnp.bfloat16

N_DEV = 8
D_MODEL = 1024
EPS = 1e-6
HG_HEADS = 4
HG_D = 128
HG_W = HG_HEADS * HG_D
CHUNK = 64
ATT_HEADS = 8
ATT_KV = 2
ATT_G = ATT_HEADS // ATT_KV
ATT_DH = 64
ATT_QW = ATT_HEADS * ATT_DH
ATT_KW = ATT_KV * ATT_DH
GRID_W = 64
ROPE_THETA = 10000.0
D_IN = 5 * HG_W + ATT_QW + 2 * ATT_KW
D_FF = 2816
ADAM_LR, ADAM_B1, ADAM_B2, ADAM_EPS, ADAM_WD, ADAM_STEP = 0.001, 0.9, 0.999, 1e-08, 0.01, 10

LANES = 128
VMEM_LIMIT = 48 * 1024 * 1024
MESH = pl.DeviceIdType.MESH
ANY = pl.BlockSpec(memory_space=pl.ANY)


def _params(sem=None):
    return pltpu.CompilerParams(dimension_semantics=sem, vmem_limit_bytes=VMEM_LIMIT)


def _pick(n, cap):
    best = None
    for t in range(LANES, cap + 1, LANES):
        if n % t == 0:
            best = t
    assert best is not None, (n, cap)
    return best


def _sigmoid(x):
    return 1.0 / (1.0 + jnp.exp(-x))


def _dot(a, b):
    return jnp.dot(a.astype(BF16), b.astype(BF16), preferred_element_type=F32)


def _dot_nt(a, b):
    return lax.dot_general(a.astype(BF16), b.astype(BF16), (((1,), (1,)), ((), ())),
                           preferred_element_type=F32)


def _dot_tn(a, b):
    return lax.dot_general(a.astype(BF16), b.astype(BF16), (((0,), (0,)), ((), ())),
                           preferred_element_type=F32)


def _mm_nn(pairs, *, name, out_dtype=F32, residual=None, tm=512, tn_cap=None):
    M = pairs[0][0].shape[0]
    N = pairs[0][1].shape[1]
    tn = N if tn_cap is None else _pick(N, tn_cap)
    n_pairs = len(pairs)
    has_res = residual is not None

    def body(*refs):
        acc = None
        for i in range(n_pairs):
            d = jnp.dot(refs[2 * i][...], refs[2 * i + 1][...], preferred_element_type=F32)
            acc = d if acc is None else acc + d
        if has_res:
            acc = acc + refs[2 * n_pairs][...]
        refs[-1][...] = acc.astype(out_dtype)

    in_specs, args = [], []
    for a, b in pairs:
        k = a.shape[1]
        in_specs += [pl.BlockSpec((tm, k), lambda i, j: (i, 0)), pl.BlockSpec((k, tn), lambda i, j: (0, j))]
        args += [a, b]
    if has_res:
        in_specs.append(pl.BlockSpec((tm, tn), lambda i, j: (i, j)))
        args.append(residual)
    return pl.pallas_call(
        body, name=name, grid=(M // tm, N // tn), in_specs=in_specs,
        out_specs=pl.BlockSpec((tm, tn), lambda i, j: (i, j)),
        out_shape=jax.ShapeDtypeStruct((M, N), out_dtype),
        compiler_params=_params(("parallel", "arbitrary")),
    )(*args)


def _mm_tn(a, b, *, name, tma_cap=1024, tnb_cap=1024, tk=512):
    T, Ma = a.shape
    Nb = b.shape[1]
    tma, tnb = _pick(Ma, tma_cap), _pick(Nb, tnb_cap)
    n_k = T // tk

    def body(a_ref, b_ref, o_ref, acc_ref):
        k = pl.program_id(2)

        @pl.when(k == 0)
        def _():
            acc_ref[...] = jnp.zeros_like(acc_ref)

        acc_ref[...] += lax.dot_general(a_ref[...], b_ref[...], (((0,), (0,)), ((), ())),
                                        preferred_element_type=F32)

        @pl.when(k == n_k - 1)
        def _():
            o_ref[...] = acc_ref[...]

    return pl.pallas_call(
        body, name=name, grid=(Ma // tma, Nb // tnb, n_k),
        in_specs=[pl.BlockSpec((tk, tma), lambda i, j, k: (k, i)), pl.BlockSpec((tk, tnb), lambda i, j, k: (k, j))],
        out_specs=pl.BlockSpec((tma, tnb), lambda i, j, k: (i, j)),
        out_shape=jax.ShapeDtypeStruct((Ma, Nb), F32),
        scratch_shapes=[pltpu.VMEM((tma, tnb), F32)],
        compiler_params=_params(("parallel", "parallel", "arbitrary")),
    )(a, b)


def _rms_fwd(x, w, *, name, tm=512):
    T, Dm = x.shape

    def body(x_ref, w_ref, h_ref, r_ref):
        xv = x_ref[...]
        r = lax.rsqrt(jnp.mean(xv * xv, axis=-1, keepdims=True) + EPS)
        h_ref[...] = (xv * r * w_ref[...]).astype(BF16)
        r_ref[...] = r

    return pl.pallas_call(
        body, name=name, grid=(T // tm,),
        in_specs=[pl.BlockSpec((tm, Dm), lambda i: (i, 0)), pl.BlockSpec((1, Dm), lambda i: (0, 0))],
        out_specs=[pl.BlockSpec((tm, Dm), lambda i: (i, 0)), pl.BlockSpec((tm, 1), lambda i: (i, 0))],
        out_shape=[jax.ShapeDtypeStruct((T, Dm), BF16), jax.ShapeDtypeStruct((T, 1), F32)],
        compiler_params=_params(("parallel",)),
    )(x, w)


def _rms_bwd(dh, x, r, w, dres, *, name, emit_bf16, tm=512):
    T, Dm = x.shape

    def body(dh_ref, x_ref, r_ref, w_ref, dres_ref, *outs):
        dx_ref, dw_ref = outs[0], outs[-1]

        @pl.when(pl.program_id(0) == 0)
        def _():
            dw_ref[...] = jnp.zeros_like(dw_ref)

        rv = r_ref[...]
        xh = x_ref[...] * rv
        dhv = dh_ref[...]
        dxh = dhv * w_ref[...]
        t = jnp.mean(dxh * xh, axis=-1, keepdims=True)
        dx = dres_ref[...] + rv * (dxh - xh * t)
        dx_ref[...] = dx
        if emit_bf16:
            outs[1][...] = dx.astype(BF16)
        dw_ref[...] += jnp.sum(dhv * xh, axis=0, keepdims=True)

    row = pl.BlockSpec((tm, Dm), lambda i: (i, 0))
    vec = pl.BlockSpec((1, Dm), lambda i: (0, 0))
    out_specs = [row] + ([row] if emit_bf16 else []) + [vec]
    out_shape = ([jax.ShapeDtypeStruct((T, Dm), F32)] + ([jax.ShapeDtypeStruct((T, Dm), BF16)] if emit_bf16 else [])
                 + [jax.ShapeDtypeStruct((1, Dm), F32)])
    return pl.pallas_call(
        body, name=name, grid=(T // tm,),
        in_specs=[row, row, pl.BlockSpec((tm, 1), lambda i: (i, 0)), vec, row],
        out_specs=out_specs, out_shape=out_shape,
        compiler_params=_params(("arbitrary",)),
    )(dh, x, r, w, dres)


def _loss_head(x2, target, w, *, name, tm=512):
    T, Dm = x2.shape

    def body(x_ref, t_ref, w_ref, loss_ref, dx_ref, dxb_ref, dw_ref):
        @pl.when(pl.program_id(0) == 0)
        def _():
            loss_ref[...] = jnp.zeros_like(loss_ref)
            dw_ref[...] = jnp.zeros_like(dw_ref)

        xv = x_ref[...]
        r = lax.rsqrt(jnp.mean(xv * xv, axis=-1, keepdims=True) + EPS)
        xh = xv * r
        wv = w_ref[...]
        err = xh * wv - t_ref[...]
        row_loss = jnp.mean(err * err, axis=-1, keepdims=True)
        loss_ref[...] += 0.5 * jnp.sum(row_loss, axis=0, keepdims=True)
        dy = err * (1.0 / Dm)
        dxh = dy * wv
        t = jnp.mean(dxh * xh, axis=-1, keepdims=True)
        dx = r * (dxh - xh * t)
        dx_ref[...] = dx
        dxb_ref[...] = dx.astype(BF16)
        dw_ref[...] += jnp.sum(dy * xh, axis=0, keepdims=True)

    row = pl.BlockSpec((tm, Dm), lambda i: (i, 0))
    vec = pl.BlockSpec((1, Dm), lambda i: (0, 0))
    return pl.pallas_call(
        body, name=name, grid=(T // tm,),
        in_specs=[row, row, vec],
        out_specs=[pl.BlockSpec((1, 1), lambda i: (0, 0)), row, row, vec],
        out_shape=[jax.ShapeDtypeStruct((1, 1), F32), jax.ShapeDtypeStruct((T, Dm), F32),
                   jax.ShapeDtypeStruct((T, Dm), BF16), jax.ShapeDtypeStruct((1, Dm), F32)],
        compiler_params=_params(("arbitrary",)),
    )(x2, target, w)


GLA_TB = 512
GLA_NC = GLA_TB // CHUNK


def _cumsum_rows(x, row, reverse):
    n = x.shape[0]
    s = 1
    while s < n:
        if not reverse:
            x = x + jnp.where(row >= s, pltpu.roll(x, s, 0), 0.0)
        else:
            x = x + jnp.where(row < n - s, pltpu.roll(x, n - s, 0), 0.0)
        s *= 2
    return x


def _gla_gates(uq, z, lbv):
    q = uq * _sigmoid(uq)
    sg = _sigmoid(z)
    sgn = _sigmoid(-z)
    f = lbv + (1.0 - lbv) * sg
    k = (1.0 - lbv) * sgn
    return q, sg, sgn, f, k


def _gla_decays(f, row, reverse):
    b = _cumsum_rows(jnp.log(f), row, reverse)
    if not reverse:
        bref, blast = b[CHUNK // 2 - 1:CHUNK // 2, :], b[CHUNK - 1:CHUNK, :]
    else:
        bref, blast = b[CHUNK // 2:CHUNK // 2 + 1, :], b[0:1, :]
    return b, bref, blast


def _gla_fwd(U, lb, *, f_block, reverse, name):
    T = U.shape[0]
    nb = T // GLA_TB

    def body(uq_ref, uf_ref, ui_ref, lb_ref, o_ref, st_ref, s_ref):
        @pl.when(pl.program_id(0) == 0)
        def _():
            s_ref[...] = jnp.zeros_like(s_ref)

        row = lax.broadcasted_iota(jnp.int32, (CHUNK, HG_D), 0)
        ri = lax.broadcasted_iota(jnp.int32, (CHUNK, CHUNK), 0)
        ci = lax.broadcasted_iota(jnp.int32, (CHUNK, CHUNK), 1)
        mask = (ri <= ci) if reverse else (ri >= ci)

        def chunk(j, carry):
            c = (GLA_NC - 1 - j) if reverse else j
            rows = pl.ds(pl.multiple_of(c * CHUNK, CHUNK), CHUNK)
            for h in range(HG_HEADS):
                cols = pl.ds(h * HG_D, HG_D)
                v = ui_ref[rows, cols]
                q, _, _, f, k = _gla_gates(uq_ref[rows, cols], uf_ref[rows, cols], lb_ref[:, cols])
                b, bref, blast = _gla_decays(f, row, reverse)
                s = jnp.where(mask, _dot_nt(q * jnp.exp(b - bref), k * jnp.exp(bref - b)), 0.0)
                st = s_ref[h]
                st_ref[c, h] = st
                o_ref[rows, cols] = _dot(s, v) + _dot_nt(q * jnp.exp(b), st)
                s_ref[h] = st * jnp.exp(blast) + _dot_tn(v, k * jnp.exp(blast - b))
            return carry

        lax.fori_loop(0, GLA_NC, chunk, 0)

    blk = (lambda i: nb - 1 - i) if reverse else (lambda i: i)
    ucol = lambda cb: pl.BlockSpec((GLA_TB, HG_W), lambda i: (blk(i), cb))
    return pl.pallas_call(
        body, name=name, grid=(nb,),
        in_specs=[ucol(0), ucol(f_block), ucol(3), pl.BlockSpec((1, HG_W), lambda i: (0, 0))],
        out_specs=[pl.BlockSpec((GLA_TB, HG_W), lambda i: (blk(i), 0)),
                   pl.BlockSpec((GLA_NC, HG_HEADS, HG_D, HG_D), lambda i: (blk(i), 0, 0, 0))],
        out_shape=[jax.ShapeDtypeStruct((T, HG_W), F32),
                   jax.ShapeDtypeStruct((T // CHUNK, HG_HEADS, HG_D, HG_D), F32)],
        scratch_shapes=[pltpu.VMEM((HG_HEADS, HG_D, HG_D), F32)],
        compiler_params=_params(("arbitrary",)),
    )(U, U, U, lb)


def _gla_bwd(U, lb, do, states, *, f_block, reverse, name):
    T = U.shape[0]
    nb = T // GLA_TB

    def body(uq_ref, uf_ref, ui_ref, lb_ref, do_ref, st_ref, dq_ref, dz_ref, dv_ref, dlb_ref, ds_ref):
        @pl.when(pl.program_id(0) == 0)
        def _():
            ds_ref[...] = jnp.zeros_like(ds_ref)
            dlb_ref[...] = jnp.zeros_like(dlb_ref)

        row = lax.broadcasted_iota(jnp.int32, (CHUNK, HG_D), 0)
        ri = lax.broadcasted_iota(jnp.int32, (CHUNK, CHUNK), 0)
        ci = lax.broadcasted_iota(jnp.int32, (CHUNK, CHUNK), 1)
        mask = (ri <= ci) if reverse else (ri >= ci)

        def chunk(j, carry):
            c = j if reverse else (GLA_NC - 1 - j)
            rows = pl.ds(pl.multiple_of(c * CHUNK, CHUNK), CHUNK)
            for h in range(HG_HEADS):
                cols = pl.ds(h * HG_D, HG_D)
                v = ui_ref[rows, cols]
                lbv = lb_ref[:, cols]
                q, sg, sgn, f, k = _gla_gates(uq_ref[rows, cols], uf_ref[rows, cols], lbv)
                b, bref, blast = _gla_decays(f, row, reverse)
                eq, ek, eb, el, dec = (jnp.exp(b - bref), jnp.exp(bref - b), jnp.exp(b), jnp.exp(blast - b),
                                       jnp.exp(blast))
                qin, kin, qb, klast = q * eq, k * ek, q * eb, k * el
                dov = do_ref[rows, cols]
                st = st_ref[c, h]
                dst = ds_ref[h]
                p = jnp.where(mask, _dot_nt(qin, kin), 0.0)
                dp = jnp.where(mask, _dot_nt(dov, v), 0.0)
                dqin = _dot(dp, kin)
                dkin = _dot_tn(dp, qin)
                dv_ref[rows, cols] = _dot_tn(p, dov) + _dot_nt(klast, dst)
                dqb = _dot(dov, st)
                dklast = _dot(v, dst)
                ds_ref[h] = _dot_tn(dov, qb) + dst * dec
                db = dqin * qin - dkin * kin + dqb * qb - dklast * klast
                extra = (jnp.sum(dklast * klast, axis=0, keepdims=True)
                         + dec * jnp.sum(st * dst, axis=0, keepdims=True))
                dg = _cumsum_rows(db, row, not reverse) + extra
                dq_ref[rows, cols] = dqin * eq + dqb * eb
                dk = dkin * ek + dklast * el
                dfk = dg / f - dk
                dz_ref[rows, cols] = dfk * (1.0 - lbv) * sg * sgn
                dlb_ref[:, cols] += jnp.sum(dfk * sgn, axis=0, keepdims=True)
            return carry

        lax.fori_loop(0, GLA_NC, chunk, 0)

    blk = (lambda i: i) if reverse else (lambda i: nb - 1 - i)
    ucol = lambda cb: pl.BlockSpec((GLA_TB, HG_W), lambda i: (blk(i), cb))
    tok = pl.BlockSpec((GLA_TB, HG_W), lambda i: (blk(i), 0))
    vec = pl.BlockSpec((1, HG_W), lambda i: (0, 0))
    return pl.pallas_call(
        body, name=name, grid=(nb,),
        in_specs=[ucol(0), ucol(f_block), ucol(3), vec, tok,
                  pl.BlockSpec((GLA_NC, HG_HEADS, HG_D, HG_D), lambda i: (blk(i), 0, 0, 0))],
        out_specs=[tok, tok, tok, vec],
        out_shape=[jax.ShapeDtypeStruct((T, HG_W), F32)] * 3 + [jax.ShapeDtypeStruct((1, HG_W), F32)],
        scratch_shapes=[pltpu.VMEM((HG_HEADS, HG_D, HG_D), F32)],
        compiler_params=_params(("arbitrary",)),
    )(U, U, U, lb, do, states)


def _hg_post_fwd(o_f, o_b, U, w, *, name, tm=512):
    T = o_f.shape[0]

    def body(of_ref, ob_ref, ug_ref, w_ref, out_ref):
        wv = w_ref[...]
        for h in range(HG_HEADS):
            cols = pl.ds(h * HG_D, HG_D)
            o = of_ref[:, cols] + ob_ref[:, cols]
            r = lax.rsqrt(jnp.mean(o * o, axis=-1, keepdims=True) + EPS)
            ug = ug_ref[:, cols]
            out_ref[:, cols] = (o * r * wv * (ug * _sigmoid(ug))).astype(BF16)

    tok = pl.BlockSpec((tm, HG_W), lambda i: (i, 0))
    return pl.pallas_call(
        body, name=name, grid=(T // tm,),
        in_specs=[tok, tok, pl.BlockSpec((tm, HG_W), lambda i: (i, 4)), pl.BlockSpec((1, HG_D), lambda i: (0, 0))],
        out_specs=tok, out_shape=jax.ShapeDtypeStruct((T, HG_W), BF16),
        compiler_params=_params(("parallel",)),
    )(o_f, o_b, U, w)


def _hg_post_bwd(dmix, o_f, o_b, U, w, *, name, tm=512):
    T = o_f.shape[0]

    def body(dm_ref, of_ref, ob_ref, ug_ref, w_ref, do_ref, dug_ref, dw_ref):
        @pl.when(pl.program_id(0) == 0)
        def _():
            dw_ref[...] = jnp.zeros_like(dw_ref)

        wv = w_ref[...]
        for h in range(HG_HEADS):
            cols = pl.ds(h * HG_D, HG_D)
            o = of_ref[:, cols] + ob_ref[:, cols]
            r = lax.rsqrt(jnp.mean(o * o, axis=-1, keepdims=True) + EPS)
            xh = o * r
            ug = ug_ref[:, cols]
            sg = _sigmoid(ug)
            dm = dm_ref[:, cols]
            dn = dm * (ug * sg)
            dug_ref[:, cols] = dm * (xh * wv) * (sg * (1.0 + ug * (1.0 - sg)))
            dxh = dn * wv
            t = jnp.mean(dxh * xh, axis=-1, keepdims=True)
            do_ref[:, cols] = r * (dxh - xh * t)
            dw_ref[:, cols] += jnp.sum(dn * xh, axis=0, keepdims=True)

    tok = pl.BlockSpec((tm, HG_W), lambda i: (i, 0))
    vec = pl.BlockSpec((1, HG_W), lambda i: (0, 0))
    return pl.pallas_call(
        body, name=name, grid=(T // tm,),
        in_specs=[tok, tok, tok, pl.BlockSpec((tm, HG_W), lambda i: (i, 4)), pl.BlockSpec((1, HG_D), lambda i: (0, 0))],
        out_specs=[tok, tok, vec],
        out_shape=[jax.ShapeDtypeStruct((T, HG_W), F32)] * 2 + [jax.ShapeDtypeStruct((1, HG_W), F32)],
        compiler_params=_params(("arbitrary",)),
    )(dmix, o_f, o_b, U, w)


def _rope_tables(T):
    rows = T // GRID_W
    row = jnp.repeat(jnp.arange(rows), GRID_W).astype(F32)
    col = jnp.tile(jnp.arange(GRID_W), rows).astype(F32)
    axis_dim = ATT_DH // 2
    freqs = ROPE_THETA ** (-jnp.arange(0, axis_dim, 2, dtype=F32) / axis_dim)
    ang = jnp.concatenate([row[:, None] * freqs, col[:, None] * freqs], axis=-1)
    cos, sin = jnp.cos(ang), jnp.sin(ang)
    c = jnp.repeat(cos, 2, axis=-1)
    s = jnp.stack([-sin, sin], axis=-1).reshape(T, ATT_DH)
    return jnp.tile(c, (1, 2)), jnp.tile(s, (1, 2))


def _head_blockdiag(width):
    shift = ATT_DH.bit_length() - 1
    ri = jnp.right_shift(lax.broadcasted_iota(jnp.int32, (width, width), 0), shift)
    ci = jnp.right_shift(lax.broadcasted_iota(jnp.int32, (width, width), 1), shift)
    return jnp.where(ri == ci, 1.0, 0.0).astype(BF16)


def _head_sum(x, bd):
    hi = x.astype(BF16)
    lo = (x - hi.astype(F32)).astype(BF16)
    return jnp.dot(hi, bd, preferred_element_type=F32) + jnp.dot(lo, bd, preferred_element_type=F32)


def _pair_swap(x, even):
    n = x.shape[-1]
    return jnp.where(even, pltpu.roll(x, n - 1, 1), pltpu.roll(x, 1, 1))


def _att_prep_fwd(U, cos, sin, qw, kw, *, name, tm=512):
    T = U.shape[0]
    scale = ATT_DH ** -0.5

    def body(aq_ref, ak_ref, av_ref, c_ref, s_ref, qw_ref, kw_ref, q_ref, k_ref, v_ref):
        bd = _head_blockdiag(ATT_QW)
        c2, s2 = c_ref[...], s_ref[...]
        c8, s8 = jnp.tile(c2, (1, 4)), jnp.tile(s2, (1, 4))

        def norm_rope(x, w, c, s, bdm):
            r = lax.rsqrt(_head_sum(x * x, bdm) * (1.0 / ATT_DH) + EPS)
            y = x * r * w
            even = (lax.broadcasted_iota(jnp.int32, y.shape, 1) & 1) == 0
            return y * c + _pair_swap(y, even) * s

        q_ref[...] = (norm_rope(aq_ref[...], qw_ref[...], c8, s8, bd) * scale).astype(BF16)
        k_ref[...] = norm_rope(ak_ref[...], kw_ref[...], c2, s2, bd[:ATT_KW, :ATT_KW]).astype(BF16)
        v_ref[...] = av_ref[...].astype(BF16)

    kv_spec = pl.BlockSpec((tm, ATT_KW), lambda i: (i, 0))
    return pl.pallas_call(
        body, name=name, grid=(T // tm,),
        in_specs=[pl.BlockSpec((tm, ATT_QW), lambda i: (i, 5)),
                  pl.BlockSpec((tm, ATT_KW), lambda i: (i, 24)), pl.BlockSpec((tm, ATT_KW), lambda i: (i, 25)),
                  kv_spec, kv_spec,
                  pl.BlockSpec((1, ATT_QW), lambda i: (0, 0)), pl.BlockSpec((1, ATT_KW), lambda i: (0, 0))],
        out_specs=[pl.BlockSpec((tm, ATT_QW), lambda i: (i, 0)), kv_spec, kv_spec],
        out_shape=[jax.ShapeDtypeStruct((T, ATT_QW), BF16), jax.ShapeDtypeStruct((T, ATT_KW), BF16),
                   jax.ShapeDtypeStruct((T, ATT_KW), BF16)],
        compiler_params=_params(("parallel",)),
    )(U, U, U, cos, sin, qw, kw)


def _att_prep_bwd(U, dq, dk, cos, sin, qw, kw, *, name, tm=512):
    T = U.shape[0]
    scale = ATT_DH ** -0.5

    def body(aq_ref, ak_ref, dq_ref, dk_ref, c_ref, s_ref, qw_ref, kw_ref, daq_ref, dak_ref, dqw_ref, dkw_ref):
        @pl.when(pl.program_id(0) == 0)
        def _():
            dqw_ref[...] = jnp.zeros_like(dqw_ref)
            dkw_ref[...] = jnp.zeros_like(dkw_ref)

        bd = _head_blockdiag(ATT_QW)
        c2, s2 = c_ref[...], s_ref[...]
        c8, s8 = jnp.tile(c2, (1, 4)), jnp.tile(s2, (1, 4))

        def bwd(x, dy, w, c, s, bdm):
            even = (lax.broadcasted_iota(jnp.int32, x.shape, 1) & 1) == 0
            dn = dy * c - _pair_swap(dy, even) * s
            r = lax.rsqrt(_head_sum(x * x, bdm) * (1.0 / ATT_DH) + EPS)
            xh = x * r
            dxh = dn * w
            t = _head_sum(dxh * xh, bdm) * (1.0 / ATT_DH)
            return r * (dxh - xh * t), jnp.sum(dn * xh, axis=0, keepdims=True)

        da, dw = bwd(aq_ref[...], dq_ref[...] * scale, qw_ref[...], c8, s8, bd)
        daq_ref[...] = da
        dqw_ref[...] += dw
        da, dw = bwd(ak_ref[...], dk_ref[...], kw_ref[...], c2, s2, bd[:ATT_KW, :ATT_KW])
        dak_ref[...] = da
        dkw_ref[...] += dw

    q_spec = pl.BlockSpec((tm, ATT_QW), lambda i: (i, 0))
    kv_spec = pl.BlockSpec((tm, ATT_KW), lambda i: (i, 0))
    qv = pl.BlockSpec((1, ATT_QW), lambda i: (0, 0))
    kv = pl.BlockSpec((1, ATT_KW), lambda i: (0, 0))
    return pl.pallas_call(
        body, name=name, grid=(T // tm,),
        in_specs=[pl.BlockSpec((tm, ATT_QW), lambda i: (i, 5)), pl.BlockSpec((tm, ATT_KW), lambda i: (i, 24)),
                  q_spec, kv_spec, kv_spec, kv_spec, qv, kv],
        out_specs=[q_spec, kv_spec, qv, kv],
        out_shape=[jax.ShapeDtypeStruct((T, ATT_QW), F32), jax.ShapeDtypeStruct((T, ATT_KW), F32),
                   jax.ShapeDtypeStruct((1, ATT_QW), F32), jax.ShapeDtypeStruct((1, ATT_KW), F32)],
        compiler_params=_params(("arbitrary",)),
    )(U, U, dq, dk, cos, sin, qw, kw)


FA_TQ = 256
FA_TK = 512


def _fa_tiles(T):
    tq, tk = min(FA_TQ, T), min(FA_TK, T)
    return tq, tk, T // tq, T // tk


def _to_fa_cols(a, T):
    tq, _, nq, _ = _fa_tiles(T)
    return a.reshape(nq, tq, ATT_KV, ATT_G, ATT_DH).transpose(2, 0, 4, 3, 1).reshape(ATT_KV, nq, ATT_DH, ATT_G * tq)


def _to_fa_rows(a, T):
    tq, _, nq, _ = _fa_tiles(T)
    return a.reshape(nq, tq, ATT_KV, ATT_G, ATT_DH).transpose(2, 0, 3, 1, 4).reshape(ATT_KV, nq, ATT_G * tq, ATT_DH)


def _from_fa_cols(a, T):
    tq, _, nq, _ = _fa_tiles(T)
    return a.reshape(ATT_KV, nq, ATT_DH, ATT_G, tq).transpose(1, 4, 0, 3, 2).reshape(T, ATT_QW)


def _kv_rows(a, T):
    _, tk, _, n_k = _fa_tiles(T)
    return a.reshape(n_k, tk, ATT_KV, ATT_DH).transpose(2, 0, 1, 3)


def _kv_cols(a, T):
    _, tk, _, n_k = _fa_tiles(T)
    return a.reshape(n_k, tk, ATT_KV, ATT_DH).transpose(2, 0, 3, 1)


def _flash_fwd(q_c, k_r, v_c, *, name):
    _, nq, _, R = q_c.shape
    _, n_k, tk, _ = k_r.shape

    def body(q_ref, k_ref, v_ref, o_ref, lse_ref, acc_ref):
        qv = q_ref[0, 0]
        acc_ref[...] = jnp.zeros_like(acc_ref)

        def step(j, carry):
            m, l = carry
            s = jnp.dot(k_ref[0, j], qv, preferred_element_type=F32)
            m_new = jnp.maximum(m, jnp.max(s, axis=0, keepdims=True))
            alpha = jnp.exp(m - m_new)
            p = jnp.exp(s - m_new)
            l = alpha * l + jnp.sum(p, axis=0, keepdims=True)
            acc_ref[...] = alpha * acc_ref[...] + jnp.dot(v_ref[0, j], p.astype(BF16), preferred_element_type=F32)
            return m_new, l

        m, l = lax.fori_loop(0, n_k, step, (jnp.full((1, R), -jnp.inf, F32), jnp.zeros((1, R), F32)))
        o_ref[0, 0] = acc_ref[...] / l
        lse_ref[0, 0] = m + jnp.log(l)

    qspec = pl.BlockSpec((1, 1, ATT_DH, R), lambda h, i: (h, i, 0, 0))
    return pl.pallas_call(
        body, name=name, grid=(ATT_KV, nq),
        in_specs=[qspec, pl.BlockSpec((1, n_k, tk, ATT_DH), lambda h, i: (h, 0, 0, 0)),
                  pl.BlockSpec((1, n_k, ATT_DH, tk), lambda h, i: (h, 0, 0, 0))],
        out_specs=[qspec, pl.BlockSpec((1, 1, 1, R), lambda h, i: (h, i, 0, 0))],
        out_shape=[jax.ShapeDtypeStruct((ATT_KV, nq, ATT_DH, R), F32), jax.ShapeDtypeStruct((ATT_KV, nq, 1, R), F32)],
        scratch_shapes=[pltpu.VMEM((ATT_DH, R), F32)],
        compiler_params=_params(("parallel", "parallel")),
    )(q_c, k_r, v_c)


def _flash_bwd(q_c, q_r, k_r, k_c, v_r, do_c, do_r, o_c, lse, *, name):
    _, nq, _, R = q_c.shape
    _, n_k, tk, _ = k_r.shape

    def body(qc_ref, qr_ref, kr_ref, kc_ref, vr_ref, doc_ref, dor_ref, oc_ref, lse_ref, dq_ref, dk_ref, dv_ref,
             acc_ref):
        @pl.when(pl.program_id(1) == 0)
        def _():
            dk_ref[...] = jnp.zeros_like(dk_ref)
            dv_ref[...] = jnp.zeros_like(dv_ref)

        qc, doc = qc_ref[0, 0], doc_ref[0, 0]
        qr, dor = qr_ref[0, 0], dor_ref[0, 0]
        delta = jnp.sum(doc.astype(F32) * oc_ref[0, 0], axis=0, keepdims=True)
        lsev = lse_ref[0, 0]
        acc_ref[...] = jnp.zeros_like(acc_ref)

        def step(j, carry):
            s = jnp.dot(kr_ref[0, j], qc, preferred_element_type=F32)
            p = jnp.exp(s - lsev)
            dp = jnp.dot(vr_ref[0, j], doc, preferred_element_type=F32)
            ds = (p * (dp - delta)).astype(BF16)
            acc_ref[...] += jnp.dot(kc_ref[0, j], ds, preferred_element_type=F32)
            dk_ref[0, j] += jnp.dot(ds, qr, preferred_element_type=F32)
            dv_ref[0, j] += jnp.dot(p.astype(BF16), dor, preferred_element_type=F32)
            return carry

        lax.fori_loop(0, n_k, step, 0)
        dq_ref[0, 0] = acc_ref[...]

    cspec = pl.BlockSpec((1, 1, ATT_DH, R), lambda h, i: (h, i, 0, 0))
    rspec = pl.BlockSpec((1, 1, R, ATT_DH), lambda h, i: (h, i, 0, 0))
    krspec = pl.BlockSpec((1, n_k, tk, ATT_DH), lambda h, i: (h, 0, 0, 0))
    kcspec = pl.BlockSpec((1, n_k, ATT_DH, tk), lambda h, i: (h, 0, 0, 0))
    return pl.pallas_call(
        body, name=name, grid=(ATT_KV, nq),
        in_specs=[cspec, rspec, krspec, kcspec, krspec, cspec, rspec, cspec,
                  pl.BlockSpec((1, 1, 1, R), lambda h, i: (h, i, 0, 0))],
        out_specs=[cspec, krspec, krspec],
        out_shape=[jax.ShapeDtypeStruct((ATT_KV, nq, ATT_DH, R), F32),
                   jax.ShapeDtypeStruct((ATT_KV, n_k, tk, ATT_DH), F32),
                   jax.ShapeDtypeStruct((ATT_KV, n_k, tk, ATT_DH), F32)],
        scratch_shapes=[pltpu.VMEM((ATT_DH, R), F32)],
        compiler_params=_params(("parallel", "arbitrary")),
    )(q_c, q_r, k_r, k_c, v_r, do_c, do_r, o_c, lse)


def _att_post_fwd(o, w, *, name, tm=512):
    T = o.shape[0]

    def body(o_ref, w_ref, out_ref):
        ov = o_ref[...]
        r = lax.rsqrt(jnp.mean(ov * ov, axis=-1, keepdims=True) + EPS)
        out_ref[...] = (ov * r * w_ref[...]).astype(BF16)

    tok = pl.BlockSpec((tm, ATT_QW), lambda i: (i, 0))
    return pl.pallas_call(
        body, name=name, grid=(T // tm,),
        in_specs=[tok, pl.BlockSpec((1, ATT_QW), lambda i: (0, 0))],
        out_specs=tok, out_shape=jax.ShapeDtypeStruct((T, ATT_QW), BF16),
        compiler_params=_params(("parallel",)),
    )(o, w)


def _att_post_bwd(dmix, o, w, *, name, tm=512):
    T = o.shape[0]

    def body(dm_ref, o_ref, w_ref, do_ref, dw_ref):
        @pl.when(pl.program_id(0) == 0)
        def _():
            dw_ref[...] = jnp.zeros_like(dw_ref)

        ov = o_ref[...]
        r = lax.rsqrt(jnp.mean(ov * ov, axis=-1, keepdims=True) + EPS)
        xh = ov * r
        dm = dm_ref[...]
        dxh = dm * w_ref[...]
        t = jnp.mean(dxh * xh, axis=-1, keepdims=True)
        do_ref[...] = (r * (dxh - xh * t)).astype(BF16)
        dw_ref[...] += jnp.sum(dm * xh, axis=0, keepdims=True)

    tok = pl.BlockSpec((tm, ATT_QW), lambda i: (i, 0))
    vec = pl.BlockSpec((1, ATT_QW), lambda i: (0, 0))
    return pl.pallas_call(
        body, name=name, grid=(T // tm,),
        in_specs=[pl.BlockSpec((tm, ATT_QW), lambda i: (i, 1)), tok, vec],
        out_specs=[tok, vec],
        out_shape=[jax.ShapeDtypeStruct((T, ATT_QW), BF16), jax.ShapeDtypeStruct((1, ATT_QW), F32)],
        compiler_params=_params(("arbitrary",)),
    )(dmix, o, w)


def _ffn_up(h2, wg, wu, *, name, tm=512):
    T = h2.shape[0]
    tn = _pick(D_FF, 1408)

    def body(h_ref, wg_ref, wu_ref, g_ref, u_ref, a_ref):
        hv = h_ref[...]
        g = jnp.dot(hv, wg_ref[...], preferred_element_type=F32)
        u = jnp.dot(hv, wu_ref[...], preferred_element_type=F32)
        g_ref[...] = g.astype(BF16)
        u_ref[...] = u.astype(BF16)
        a_ref[...] = (g * _sigmoid(g) * u).astype(BF16)

    wspec = pl.BlockSpec((D_MODEL, tn), lambda i, j: (0, j))
    ospec = pl.BlockSpec((tm, tn), lambda i, j: (i, j))
    return pl.pallas_call(
        body, name=name, grid=(T // tm, D_FF // tn),
        in_specs=[pl.BlockSpec((tm, D_MODEL), lambda i, j: (i, 0)), wspec, wspec],
        out_specs=[ospec] * 3, out_shape=[jax.ShapeDtypeStruct((T, D_FF), BF16)] * 3,
        compiler_params=_params(("parallel", "arbitrary")),
    )(h2, wg, wu)


def _ffn_act_bwd(dx2b, wd_t, gate, up, *, name, tm=512):
    T = dx2b.shape[0]
    tn = _pick(D_FF, 1408)

    def body(dx_ref, w_ref, g_ref, u_ref, dg_ref, du_ref):
        da = jnp.dot(dx_ref[...], w_ref[...], preferred_element_type=F32)
        g = g_ref[...].astype(F32)
        u = u_ref[...].astype(F32)
        sg = _sigmoid(g)
        dg_ref[...] = (da * u * (sg * (1.0 + g * (1.0 - sg)))).astype(BF16)
        du_ref[...] = (da * (g * sg)).astype(BF16)

    ospec = pl.BlockSpec((tm, tn), lambda i, j: (i, j))
    return pl.pallas_call(
        body, name=name, grid=(T // tm, D_FF // tn),
        in_specs=[pl.BlockSpec((tm, D_MODEL), lambda i, j: (i, 0)),
                  pl.BlockSpec((D_MODEL, tn), lambda i, j: (0, j)), ospec, ospec],
        out_specs=[ospec] * 2, out_shape=[jax.ShapeDtypeStruct((T, D_FF), BF16)] * 2,
        compiler_params=_params(("parallel", "arbitrary")),
    )(dx2b, wd_t, gate, up)


def _assemble_du(U, dq_f, dq_b, dz_f, dz_b, dv_f, dv_b, du_g, da_q, da_k, da_v, *, name, tm=256):
    T = U.shape[0]

    def body(uq_ref, dqf, dqb, dzf, dzb, dvf, dvb, dug, daq, dak, dav, out_ref):
        uq = uq_ref[...]
        sg = _sigmoid(uq)
        out_ref[:, 0:HG_W] = ((dqf[...] + dqb[...]) * (sg * (1.0 + uq * (1.0 - sg)))).astype(BF16)
        out_ref[:, HG_W:2 * HG_W] = dzf[...].astype(BF16)
        out_ref[:, 2 * HG_W:3 * HG_W] = dzb[...].astype(BF16)
        out_ref[:, 3 * HG_W:4 * HG_W] = (dvf[...] + dvb[...]).astype(BF16)
        out_ref[:, 4 * HG_W:5 * HG_W] = dug[...].astype(BF16)
        out_ref[:, 5 * HG_W:5 * HG_W + ATT_QW] = daq[...].astype(BF16)
        out_ref[:, 5 * HG_W + ATT_QW:5 * HG_W + ATT_QW + ATT_KW] = dak[...].astype(BF16)
        out_ref[:, 5 * HG_W + ATT_QW + ATT_KW:D_IN] = dav[...].astype(BF16)

    tok = pl.BlockSpec((tm, HG_W), lambda i: (i, 0))
    kv = pl.BlockSpec((tm, ATT_KW), lambda i: (i, 0))
    return pl.pallas_call(
        body, name=name, grid=(T // tm,),
        in_specs=[tok] * 9 + [kv, kv],
        out_specs=pl.BlockSpec((tm, D_IN), lambda i: (i, 0)),
        out_shape=jax.ShapeDtypeStruct((T, D_IN), BF16),
        compiler_params=_params(("parallel",)),
    )(U, dq_f, dq_b, dz_f, dz_b, dv_f, dv_b, du_g, da_q, da_k, da_v)


def _adam_math(w, g, m, v):
    m = ADAM_B1 * m + (1.0 - ADAM_B1) * g
    v = ADAM_B2 * v + (1.0 - ADAM_B2) * (g * g)
    m_hat = m / (1.0 - ADAM_B1 ** ADAM_STEP)
    v_hat = v / (1.0 - ADAM_B2 ** ADAM_STEP)
    delta = -ADAM_LR * (m_hat / (jnp.sqrt(v_hat) + ADAM_EPS) + ADAM_WD * w)
    return delta, m, v


def _adamw(parts, w, m, v, *, name, tr_cap=256):
    P, R, C = parts.shape
    tr = R
    for t in range(8, min(R, tr_cap) + 1, 8):
        if R % t == 0:
            tr = t

    def body(p_ref, w_ref, m_ref, v_ref, g_ref, d_ref, nm_ref, nv_ref):
        g = p_ref[0].astype(F32)
        for j in range(1, P):
            g = g + p_ref[j].astype(F32)
        d, nm, nv = _adam_math(w_ref[...], g, m_ref[...], v_ref[...])
        g_ref[...] = g
        d_ref[...] = d
        nm_ref[...] = nm
        nv_ref[...] = nv

    blk = pl.BlockSpec((tr, C), lambda i: (i, 0))
    return pl.pallas_call(
        body, name=name, grid=(R // tr,),
        in_specs=[pl.BlockSpec((P, tr, C), lambda i: (0, i, 0)), blk, blk, blk],
        out_specs=[blk] * 4, out_shape=[jax.ShapeDtypeStruct((R, C), F32)] * 4,
        compiler_params=_params(("parallel",)),
    )(parts, w, m, v)


def _all_gather(xs, *, name):
    n = len(xs)

    def body(*refs):
        ins, outs = refs[:n], refs[n:2 * n]
        send_sems, recv_sems, local_sems = refs[2 * n:]
        x, y, c = lax.axis_index("x"), lax.axis_index("y"), lax.axis_index("c")
        me, sibling = (x, y, c), (x, y, 1 - c)
        chips = [(1 - x, y), (x, 1 - y), (1 - x, 1 - y)]

        def slot(p):
            return 4 * p[0] + 2 * p[1] + p[2]

        def copy(a, k, block, to, src=None):
            dst = outs[a].at[slot(block)]
            return pltpu.make_async_remote_copy(
                src_ref=dst if src is None else src, dst_ref=dst,
                send_sem=send_sems.at[a * 7 + k], recv_sem=recv_sems.at[a * 7 + k],
                device_id=to, device_id_type=MESH)

        mine = [pltpu.make_async_copy(ins[a], outs[a].at[slot(me)], local_sems.at[a]) for a in range(n)]
        for cp in mine:
            cp.start()
        first = []
        for a in range(n):
            first.append(copy(a, 0, me, sibling, src=ins[a]))
            first += [copy(a, 1 + j, me, (*chip, c), src=ins[a]) for j, chip in enumerate(chips)]
        for cp in first:
            cp.start()
        passed = []
        for j, chip in enumerate(chips):
            for a in range(n):
                copy(a, 1 + j, (*chip, c), me).wait_recv()
                cp = copy(a, 4 + j, (*chip, c), sibling)
                cp.start()
                passed.append(cp)
        for a in range(n):
            copy(a, 0, sibling, me).wait_recv()
            for j, chip in enumerate(chips):
                copy(a, 4 + j, (*chip, 1 - c), me).wait_recv()
        for cp in first + passed:
            cp.wait_send()
        for cp in mine:
            cp.wait()

    return pl.pallas_call(
        body, name=name,
        in_specs=[ANY] * n, out_specs=[ANY] * n,
        out_shape=[jax.ShapeDtypeStruct((N_DEV,) + x.shape, x.dtype) for x in xs],
        scratch_shapes=[pltpu.SemaphoreType.DMA((7 * n,)), pltpu.SemaphoreType.DMA((7 * n,)),
                        pltpu.SemaphoreType.DMA((n,))],
        compiler_params=pltpu.CompilerParams(has_side_effects=True),
    )(*xs)


def _exchange(gs, *, masks, slot, name):
    n, n_peers = len(gs), len(masks)

    def body(*refs):
        ins, outs = refs[:n], refs[n:2 * n]
        send_sems, recv_sems, local_sems = refs[2 * n:]
        x, y, c = lax.axis_index("x"), lax.axis_index("y"), lax.axis_index("c")
        my_slot = slot((x, y, c))

        def flip(v, bit):
            return 1 - v if bit else v

        mine = [pltpu.make_async_copy(ins[a].at[my_slot], outs[a].at[my_slot], local_sems.at[a]) for a in range(n)]
        for cp in mine:
            cp.start()
        copies = []
        for a in range(n):
            for k, (mx, my, mc) in enumerate(masks):
                peer = (flip(x, mx), flip(y, my), flip(c, mc))
                peer_slot = slot(peer)
                sems = dict(send_sem=send_sems.at[a * n_peers + k], recv_sem=recv_sems.at[a * n_peers + k],
                            device_id=peer, device_id_type=MESH)
                copies.append((
                    pltpu.make_async_remote_copy(src_ref=ins[a].at[peer_slot], dst_ref=outs[a].at[my_slot], **sems),
                    pltpu.make_async_remote_copy(src_ref=ins[a].at[peer_slot], dst_ref=outs[a].at[peer_slot], **sems)))
        for send, _ in copies:
            send.start()
        for send, recv in copies:
            recv.wait_recv()
            send.wait_send()
        for cp in mine:
            cp.wait()

    return pl.pallas_call(
        body, name=name,
        in_specs=[ANY] * n, out_specs=[ANY] * n,
        out_shape=[jax.ShapeDtypeStruct(g.shape, g.dtype) for g in gs],
        scratch_shapes=[pltpu.SemaphoreType.DMA((n_peers * n,)), pltpu.SemaphoreType.DMA((n_peers * n,)),
                        pltpu.SemaphoreType.DMA((n,))],
        compiler_params=pltpu.CompilerParams(has_side_effects=True),
    )(*gs)


SWAP_ROW_CHUNKS = 4


def _core_swap(gs, *, name):
    n = len(gs)

    def body(*refs):
        ins, outs = refs[:n], refs[n:2 * n]
        send_sems, recv_sems = refs[2 * n:]
        x, y, c = lax.axis_index("x"), lax.axis_index("y"), lax.axis_index("c")
        sibling = (x, y, 1 - c)
        started = []
        for a in range(n):
            _, Q, R, _ = ins[a].shape
            rows = R // SWAP_ROW_CHUNKS
            for q in range(Q):
                for j in range(SWAP_ROW_CHUNKS):
                    cp = pltpu.make_async_remote_copy(
                        src_ref=ins[a].at[1 - c, q, pl.ds(j * rows, rows)], dst_ref=outs[a].at[q, pl.ds(j * rows, rows)],
                        send_sem=send_sems.at[a], recv_sem=recv_sems.at[a], device_id=sibling, device_id_type=MESH)
                    cp.start()
                    started.append(cp)
        for a in range(n):
            pltpu.make_async_remote_copy(
                src_ref=ins[a].at[1 - c], dst_ref=outs[a], send_sem=send_sems.at[a], recv_sem=recv_sems.at[a],
                device_id=sibling, device_id_type=MESH).wait()

    return pl.pallas_call(
        body, name=name,
        in_specs=[ANY] * n, out_specs=[ANY] * n,
        out_shape=[jax.ShapeDtypeStruct(g.shape[1:], g.dtype) for g in gs],
        scratch_shapes=[pltpu.SemaphoreType.DMA((n,)), pltpu.SemaphoreType.DMA((n,))],
        compiler_params=pltpu.CompilerParams(has_side_effects=True),
    )(*gs)


def _pair_sum(g, other, core, *, name, tr_cap=256):
    _, Q, R, C = g.shape
    tr = max(t for t in range(16, min(R, tr_cap) + 1, 16) if R % t == 0)

    def body(core_ref, g_ref, o_ref, out_ref):
        out_ref[0] = (g_ref[0, 0] + o_ref[0]).astype(BF16)

    return pl.pallas_call(
        body, name=name,
        grid_spec=pltpu.PrefetchScalarGridSpec(
            num_scalar_prefetch=1, grid=(Q, R // tr),
            in_specs=[pl.BlockSpec((1, 1, tr, C), lambda q, i, core_ref: (core_ref[0], q, i, 0)),
                      pl.BlockSpec((1, tr, C), lambda q, i, core_ref: (q, i, 0))],
            out_specs=pl.BlockSpec((1, tr, C), lambda q, i, core_ref: (q, i, 0))),
        out_shape=jax.ShapeDtypeStruct((Q, R, C), BF16),
        compiler_params=_params(("parallel", "parallel")),
    )(core, g, other)


PACK_ROWS = 8


def _pack_small(norm1, norm2, final, att, hg, qn, kn, lb=None):
    z = lambda n: jnp.zeros((n,), F32)
    rows = [norm1.reshape(-1), norm2.reshape(-1), final.reshape(-1),
            jnp.concatenate([att.reshape(-1), z(512)]),
            jnp.concatenate([hg.reshape(-1), qn.reshape(-1), kn.reshape(-1), z(1024 - 256)]),
            z(1024) if lb is None else lb.reshape(-1), z(1024), z(1024)]
    return jnp.stack(rows, axis=0)


def _unpack_small(p):
    return (p[0:1, :], p[1:2, :], p[2, :], p[3:4, 0:512], p[4:5, 0:128], p[4:5, 128:192], p[4:5, 192:256])


def _fold_heads(dhg, dqn, dkn, *, name):
    def body(hg_ref, q_ref, k_ref, ohg_ref, oq_ref, ok_ref):
        def fold128(v):
            acc = v[:, 0:LANES]
            for j in range(1, v.shape[1] // LANES):
                acc = acc + v[:, j * LANES:(j + 1) * LANES]
            return acc

        ohg_ref[...] = fold128(hg_ref[...])
        q = fold128(q_ref[...])
        oq_ref[...] = q + pltpu.roll(q, ATT_DH, 1)
        k = k_ref[...]
        ok_ref[...] = k + pltpu.roll(k, ATT_DH, 1)

    return pl.pallas_call(body, name=name, out_shape=[jax.ShapeDtypeStruct((1, LANES), F32)] * 3)(dhg, dqn, dkn)


def _lb_grad(dlb_sum, lb, *, name):
    def body(d_ref, lb_ref, o_ref):
        lbv = lb_ref[...]
        gl = d_ref[...] * lbv * (1.0 - lbv)
        o_ref[0:1, :] = gl[0:1, :]
        o_ref[1:2, :] = -gl[0:1, :]
        o_ref[2:3, :] = gl[1:2, :]
        o_ref[3:4, :] = -gl[1:2, :]

    return pl.pallas_call(body, name=name, out_shape=jax.ShapeDtypeStruct((4, HG_W), F32))(dlb_sum, lb)


def _lower_bounds(lb_logits_full, *, name):
    def body(l_ref, o_ref):
        for d in range(2):
            l0, l1 = l_ref[2 * d:2 * d + 1, :], l_ref[2 * d + 1:2 * d + 2, :]
            mx = jnp.maximum(l0, l1)
            e0, e1 = jnp.exp(l0 - mx), jnp.exp(l1 - mx)
            o_ref[d:d + 1, :] = e0 / (e0 + e1)

    return pl.pallas_call(body, name=name, out_shape=jax.ShapeDtypeStruct((2, HG_W), F32))(
        lb_logits_full.reshape(4, HG_W))


def _local_step(x, target, norm1_w, w_in, lb, hg_norm_w, q_norm_w, k_norm_w, att_norm_w, w_out, norm2_w,
                w_g, w_u, w_down, final_norm_w):
    T = x.shape[0]
    cos, sin = _rope_tables(T)
    qw8 = jnp.tile(q_norm_w, (1, ATT_HEADS))
    kw2 = jnp.tile(k_norm_w, (1, ATT_KV))

    h, r1 = _rms_fwd(x, norm1_w, name="norm1_fwd")
    U = _mm_nn([(h, w_in)], name="in_proj")
    o_f, st_f = _gla_fwd(U, lb[0:1], f_block=1, reverse=False, name="gla_fwd_f")
    o_b, st_b = _gla_fwd(U, lb[1:2], f_block=2, reverse=True, name="gla_fwd_b")
    mix_hg = _hg_post_fwd(o_f, o_b, U, hg_norm_w, name="hg_post_fwd")
    q, k, v = _att_prep_fwd(U, cos, sin, qw8, kw2, name="att_prep_fwd")
    q_c, k_r = _to_fa_cols(q, T), _kv_rows(k, T)
    o_c, lse = _flash_fwd(q_c, k_r, _kv_cols(v, T), name="flash_fwd")
    o_att = _from_fa_cols(o_c, T)
    mix_att = _att_post_fwd(o_att, att_norm_w, name="att_post_fwd")
    mix = jnp.concatenate([mix_hg, mix_att], axis=1)
    x1 = _mm_nn([(mix, w_out)], residual=x, name="out_proj")
    h2, r2 = _rms_fwd(x1, norm2_w, name="norm2_fwd")
    gate, up, act = _ffn_up(h2, w_g, w_u, name="ffn_up")
    x2 = _mm_nn([(act, w_down)], residual=x1, name="ffn_down")
    loss, dx2, dx2b, d_final = _loss_head(x2, target, final_norm_w.reshape(1, D_MODEL), name="loss_head")

    d_gate, d_up = _ffn_act_bwd(dx2b, w_down.T, gate, up, name="ffn_act_bwd")
    dw_down = _mm_tn(act, dx2b, tma_cap=1408, name="dw_down")
    dh2 = _mm_nn([(d_gate, w_g.T), (d_up, w_u.T)], tm=256, name="ffn_up_bwd")
    dw_g = _mm_tn(h2, d_gate, tnb_cap=1408, name="dw_gate")
    dw_u = _mm_tn(h2, d_up, tnb_cap=1408, name="dw_up")
    dx1, dx1b, d_norm2 = _rms_bwd(dh2, x1, r2, norm2_w, dx2, emit_bf16=True, name="norm2_bwd")
    dmix = _mm_nn([(dx1b, w_out.T)], name="out_proj_bwd")
    dw_out = _mm_tn(mix, dx1b, name="dw_out")
    do_att, d_att = _att_post_bwd(dmix, o_att, att_norm_w, name="att_post_bwd")
    dq_c, dk_r, dv_r = _flash_bwd(q_c, _to_fa_rows(q, T), k_r, _kv_cols(k, T), _kv_rows(v, T),
                                  _to_fa_cols(do_att, T), _to_fa_rows(do_att, T), o_c, lse, name="flash_bwd")
    dq = _from_fa_cols(dq_c, T)
    dk = dk_r.transpose(1, 2, 0, 3).reshape(T, ATT_KW)
    da_v = dv_r.transpose(1, 2, 0, 3).reshape(T, ATT_KW)
    da_q, da_k, d_qn, d_kn = _att_prep_bwd(U, dq, dk, cos, sin, qw8, kw2, name="att_prep_bwd")
    do_hg, du_g, d_hg = _hg_post_bwd(dmix, o_f, o_b, U, hg_norm_w, name="hg_post_bwd")
    dq_f, dz_f, dv_f, dlb_f = _gla_bwd(U, lb[0:1], do_hg, st_f, f_block=1, reverse=False, name="gla_bwd_f")
    dq_b, dz_b, dv_b, dlb_b = _gla_bwd(U, lb[1:2], do_hg, st_b, f_block=2, reverse=True, name="gla_bwd_b")
    dU = _assemble_du(U, dq_f, dq_b, dz_f, dz_b, dv_f, dv_b, du_g, da_q, da_k, da_v, name="assemble_du")
    dh = _mm_nn([(dU, w_in.T)], name="in_proj_bwd")
    dw_in = _mm_tn(h, dU, tnb_cap=1664, name="dw_in")
    grad_x, d_norm1 = _rms_bwd(dh, x, r1, norm1_w, dx1, emit_bf16=False, name="norm1_bwd")
    d_hg, d_qn, d_kn = _fold_heads(d_hg, d_qn, d_kn, name="fold_heads")

    big = dict(w_in=dw_in, w_out=dw_out, w_g=dw_g, w_u=dw_u, w_down=dw_down)
    small = dict(norm1=d_norm1, norm2=d_norm2, final=d_final, att=d_att, hg=d_hg,
                 qn=d_qn[:, :ATT_DH], kn=d_kn[:, :ATT_DH], lb=jnp.concatenate([dlb_f, dlb_b], axis=0))
    return loss, grad_x, big, small


def kernel(x, norm1_w, w_in, lb_logits, hg_norm_w, q_norm_w, k_norm_w, att_norm_w, w_out, norm2_w, w_gate_up, w_down, final_norm_w, loss_target, m_norm1_w, m_w_in, m_lb_logits, m_hg_norm_w, m_q_norm_w, m_k_norm_w, m_att_norm_w, m_w_out, m_norm2_w, m_w_gate_up, m_w_down, m_final_norm_w, v_norm1_w, v_w_in, v_lb_logits, v_hg_norm_w, v_q_norm_w, v_k_norm_w, v_att_norm_w, v_w_out, v_norm2_w, v_w_gate_up, v_w_down, v_final_norm_w):
    T = x.shape[1]
    me = 4 * lax.axis_index("x") + 2 * lax.axis_index("y") + lax.axis_index("c")
    c_in, r_out, c_gu, r_dn = w_in.shape[2], w_out.shape[1], w_gate_up.shape[2], w_down.shape[1]
    lb_cols = lb_logits.shape[2]

    g_in, g_out, g_gu, g_dn, g_lb = _all_gather(
        [w_in[0].astype(BF16), w_out[0].astype(BF16), w_gate_up[0].astype(BF16), w_down[0].astype(BF16),
         lb_logits.reshape(4, lb_cols)], name="gather_weights")
    w_in_f = g_in.transpose(1, 0, 2).reshape(D_MODEL, N_DEV * c_in)
    w_out_f = g_out.reshape(N_DEV * r_out, D_MODEL)
    half = N_DEV // 2
    w_g_f = g_gu[:half].transpose(1, 0, 2).reshape(D_MODEL, half * c_gu)
    w_u_f = g_gu[half:].transpose(1, 0, 2).reshape(D_MODEL, half * c_gu)
    w_dn_f = g_dn.reshape(N_DEV * r_dn, D_MODEL)
    lb_logits_f = g_lb.transpose(1, 0, 2).reshape(2, 2, N_DEV * lb_cols)
    lb = _lower_bounds(lb_logits_f, name="lower_bounds")

    loss, grad_x, big, small = _local_step(
        x[0], loss_target[0], norm1_w, w_in_f, lb, hg_norm_w, q_norm_w, k_norm_w, att_norm_w, w_out_f, norm2_w,
        w_g_f, w_u_f, w_dn_f, final_norm_w)

    chips = N_DEV // 2
    by_owner_cols = lambda g, n_q, w: g.reshape(D_MODEL, n_q, 2, w).transpose(2, 1, 0, 3)
    by_owner_rows = lambda g, r: g.reshape(chips, 2, r, D_MODEL).transpose(1, 0, 2, 3)
    s_in = by_owner_cols(big["w_in"], chips, c_in)
    s_out = by_owner_rows(big["w_out"], r_out)
    s_gu = jnp.concatenate([by_owner_cols(big["w_g"], chips // 2, c_gu), by_owner_cols(big["w_u"], chips // 2, c_gu)],
                           axis=1)
    s_dn = by_owner_rows(big["w_down"], r_dn)
    mine = [s_in, s_out, s_gu, s_dn]
    theirs = _core_swap(mine, name="exchange_cores")
    core = lax.axis_index("c").astype(jnp.int32).reshape(1)
    chip_sums = [_pair_sum(g, o, core, name="pair_sum_" + nm)
                 for g, o, nm in zip(mine, theirs, ("w_in", "w_out", "w_gu", "w_down"))]
    p_in, p_out, p_gu, p_dn = _exchange(chip_sums, masks=[(1, 0, 0), (0, 1, 0), (1, 1, 0)],
                                        slot=lambda p: 2 * p[0] + p[1], name="exchange_chips")

    packed = _pack_small(small["norm1"], small["norm2"], small["final"], small["att"], small["hg"],
                         small["qn"], small["kn"], small["lb"])
    (all_small,) = _all_gather([packed], name="gather_small_grads")

    g_w_in, d_w_in, nm_w_in, nv_w_in = _adamw(p_in, w_in[0], m_w_in[0], v_w_in[0], name="adamw_w_in")
    g_w_out, d_w_out, nm_w_out, nv_w_out = _adamw(p_out, w_out[0], m_w_out[0], v_w_out[0], name="adamw_w_out")
    g_w_gu, d_w_gu, nm_w_gu, nv_w_gu = _adamw(p_gu, w_gate_up[0], m_w_gate_up[0], v_w_gate_up[0], name="adamw_w_gu")
    g_w_dn, d_w_dn, nm_w_dn, nv_w_dn = _adamw(p_dn, w_down[0], m_w_down[0], v_w_down[0], name="adamw_w_down")

    pk = lambda vecs: _pack_small(*vecs)
    w_pk = pk([norm1_w, norm2_w, final_norm_w, att_norm_w, hg_norm_w, q_norm_w, k_norm_w])
    m_pk = pk([m_norm1_w, m_norm2_w, m_final_norm_w, m_att_norm_w, m_hg_norm_w, m_q_norm_w, m_k_norm_w])
    v_pk = pk([v_norm1_w, v_norm2_w, v_final_norm_w, v_att_norm_w, v_hg_norm_w, v_q_norm_w, v_k_norm_w])
    g_pk, d_pk, nm_pk, nv_pk = _adamw(all_small, w_pk, m_pk, v_pk, name="adamw_small")

    dlb_sum = g_pk[5:6, :].reshape(2, HG_W)
    g_lb_full = _lb_grad(dlb_sum, lb, name="lb_grad")
    g_lb_mine = lax.dynamic_slice_in_dim(g_lb_full, me * lb_cols, lb_cols, axis=1)
    g_lb_s, d_lb, nm_lb, nv_lb = _adamw(g_lb_mine[None], lb_logits.reshape(4, lb_cols),
                                        m_lb_logits.reshape(4, lb_cols), v_lb_logits.reshape(4, lb_cols),
                                        name="adamw_lb")

    loss_total = lax.psum(loss[0, 0], ("x", "y", "c"))

    def outs(big4, lb_arr, pk_arr):
        n1, n2, fin, att, hg, qn, kn = _unpack_small(pk_arr)
        b_in, b_out, b_gu, b_dn = big4
        return [n1, b_in[None], lb_arr.reshape(2, 2, lb_cols), hg, qn, kn, att, b_out[None], n2, b_gu[None],
                b_dn[None], fin]

    return (loss_total, grad_x[None],
            *outs((g_w_in, g_w_out, g_w_gu, g_w_dn), g_lb_s, g_pk),
            *outs((d_w_in, d_w_out, d_w_gu, d_w_dn), d_lb, d_pk),
            *outs((nm_w_in, nm_w_out, nm_w_gu, nm_w_dn), nm_lb, nm_pk),
            *outs((nv_w_in, nv_w_out, nv_w_gu, nv_w_dn), nv_lb, nv_pk))
```

```python
import functools
import math

import jax
import jax.numpy as jnp
import numpy as np
from jax import lax
from jax.experimental import pallas as pl
from jax.experimental.pallas import tpu as pltpu

F32 = jnp.float32
BF16 = jnp.bfloat16

N_DEV = 8
D_MODEL = 1024
EPS = 1e-6
HG_HEADS = 4
HG_D = 128
HG_W = HG_HEADS * HG_D
CHUNK = 64
ATT_HEADS = 8
ATT_KV = 2
ATT_G = ATT_HEADS // ATT_KV
ATT_DH = 64
ATT_QW = ATT_HEADS * ATT_DH
ATT_KW = ATT_KV * ATT_DH
GRID_W = 64
ROPE_THETA = 10000.0
D_IN = 5 * HG_W + ATT_QW + 2 * ATT_KW
D_FF = 2816
ADAM_LR, ADAM_B1, ADAM_B2, ADAM_EPS, ADAM_WD, ADAM_STEP = 0.001, 0.9, 0.999, 1e-08, 0.01, 10

LANES = 128
VMEM_LIMIT = 48 * 1024 * 1024
MESH = pl.DeviceIdType.MESH
ANY = pl.BlockSpec(memory_space=pl.ANY)


def _params(sem=None):
    return pltpu.CompilerParams(dimension_semantics=sem, vmem_limit_bytes=VMEM_LIMIT)


def _pick(n, cap):
    best = None
    for t in range(LANES, cap + 1, LANES):
        if n % t == 0:
            best = t
    assert best is not None, (n, cap)
    return best


def _sigmoid(x):
    return 1.0 / (1.0 + jnp.exp(-x))


def _dot(a, b):
    return jnp.dot(a.astype(BF16), b.astype(BF16), preferred_element_type=F32)


def _dot_nt(a, b):
    return lax.dot_general(a.astype(BF16), b.astype(BF16), (((1,), (1,)), ((), ())),
                           preferred_element_type=F32)


def _dot_tn(a, b):
    return lax.dot_general(a.astype(BF16), b.astype(BF16), (((0,), (0,)), ((), ())),
                           preferred_element_type=F32)


def _mm_nn(pairs, *, name, out_dtype=F32, residual=None, tm=512, tn_cap=None, trans_b=False):
    M = pairs[0][0].shape[0]
    N = pairs[0][1].shape[0 if trans_b else 1]
    tn = N if tn_cap is None else _pick(N, tn_cap)
    n_pairs = len(pairs)
    has_res = residual is not None
    dims = (((1,), (1,)), ((), ())) if trans_b else (((1,), (0,)), ((), ()))

    def body(*refs):
        acc = None
        for i in range(n_pairs):
            d = lax.dot_general(refs[2 * i][...], refs[2 * i + 1][...], dims, preferred_element_type=F32)
            acc = d if acc is None else acc + d
        if has_res:
            acc = acc + refs[2 * n_pairs][...]
        refs[-1][...] = acc.astype(out_dtype)

    in_specs, args = [], []
    for a, b in pairs:
        k = a.shape[1]
        b_spec = pl.BlockSpec((tn, k), lambda i, j: (j, 0)) if trans_b else pl.BlockSpec((k, tn), lambda i, j: (0, j))
        in_specs += [pl.BlockSpec((tm, k), lambda i, j: (i, 0)), b_spec]
        args += [a, b]
    if has_res:
        in_specs.append(pl.BlockSpec((tm, tn), lambda i, j: (i, j)))
        args.append(residual)
    return pl.pallas_call(
        body, name=name, grid=(M // tm, N // tn), in_specs=in_specs,
        out_specs=pl.BlockSpec((tm, tn), lambda i, j: (i, j)),
        out_shape=jax.ShapeDtypeStruct((M, N), out_dtype),
        compiler_params=_params(("parallel", "arbitrary")),
    )(*args)


def _mm_tn(a, b, *, name, tma_cap=1024, tnb_cap=1024, tk=512):
    T, Ma = a.shape
    Nb = b.shape[1]
    tma, tnb = _pick(Ma, tma_cap), _pick(Nb, tnb_cap)
    n_k = T // tk

    def body(a_ref, b_ref, o_ref, acc_ref):
        k = pl.program_id(2)

        @pl.when(k == 0)
        def _():
            acc_ref[...] = jnp.zeros_like(acc_ref)

        acc_ref[...] += lax.dot_general(a_ref[...], b_ref[...], (((0,), (0,)), ((), ())),
                                        preferred_element_type=F32)

        @pl.when(k == n_k - 1)
        def _():
            o_ref[...] = acc_ref[...]

    return pl.pallas_call(
        body, name=name, grid=(Ma // tma, Nb // tnb, n_k),
        in_specs=[pl.BlockSpec((tk, tma), lambda i, j, k: (k, i)), pl.BlockSpec((tk, tnb), lambda i, j, k: (k, j))],
        out_specs=pl.BlockSpec((tma, tnb), lambda i, j, k: (i, j)),
        out_shape=jax.ShapeDtypeStruct((Ma, Nb), F32),
        scratch_shapes=[pltpu.VMEM((tma, tnb), F32)],
        compiler_params=_params(("parallel", "parallel", "arbitrary")),
    )(a, b)


def _rms_fwd(x, w, *, name, tm=512):
    T, Dm = x.shape

    def body(x_ref, w_ref, h_ref, r_ref):
        xv = x_ref[...]
        r = lax.rsqrt(jnp.mean(xv * xv, axis=-1, keepdims=True) + EPS)
        h_ref[...] = (xv * r * w_ref[...]).astype(BF16)
        r_ref[...] = r

    return pl.pallas_call(
        body, name=name, grid=(T // tm,),
        in_specs=[pl.BlockSpec((tm, Dm), lambda i: (i, 0)), pl.BlockSpec((1, Dm), lambda i: (0, 0))],
        out_specs=[pl.BlockSpec((tm, Dm), lambda i: (i, 0)), pl.BlockSpec((tm, 1), lambda i: (i, 0))],
        out_shape=[jax.ShapeDtypeStruct((T, Dm), BF16), jax.ShapeDtypeStruct((T, 1), F32)],
        compiler_params=_params(("parallel",)),
    )(x, w)


def _rms_bwd(dh, x, r, w, dres, *, name, emit_bf16, tm=512):
    T, Dm = x.shape

    def body(dh_ref, x_ref, r_ref, w_ref, dres_ref, *outs):
        dx_ref, dw_ref = outs[0], outs[-1]

        @pl.when(pl.program_id(0) == 0)
        def _():
            dw_ref[...] = jnp.zeros_like(dw_ref)

        rv = r_ref[...]
        xh = x_ref[...] * rv
        dhv = dh_ref[...]
        dxh = dhv * w_ref[...]
        t = jnp.mean(dxh * xh, axis=-1, keepdims=True)
        dx = dres_ref[...] + rv * (dxh - xh * t)
        dx_ref[...] = dx
        if emit_bf16:
            outs[1][...] = dx.astype(BF16)
        dw_ref[...] += jnp.sum(dhv * xh, axis=0, keepdims=True)

    row = pl.BlockSpec((tm, Dm), lambda i: (i, 0))
    vec = pl.BlockSpec((1, Dm), lambda i: (0, 0))
    out_specs = [row] + ([row] if emit_bf16 else []) + [vec]
    out_shape = ([jax.ShapeDtypeStruct((T, Dm), F32)] + ([jax.ShapeDtypeStruct((T, Dm), BF16)] if emit_bf16 else [])
                 + [jax.ShapeDtypeStruct((1, Dm), F32)])
    return pl.pallas_call(
        body, name=name, grid=(T // tm,),
        in_specs=[row, row, pl.BlockSpec((tm, 1), lambda i: (i, 0)), vec, row],
        out_specs=out_specs, out_shape=out_shape,
        compiler_params=_params(("arbitrary",)),
    )(dh, x, r, w, dres)


def _loss_head(x2, target, w, *, name, tm=512):
    T, Dm = x2.shape

    def body(x_ref, t_ref, w_ref, loss_ref, dx_ref, dxb_ref, dw_ref):
        @pl.when(pl.program_id(0) == 0)
        def _():
            loss_ref[...] = jnp.zeros_like(loss_ref)
            dw_ref[...] = jnp.zeros_like(dw_ref)

        xv = x_ref[...]
        r = lax.rsqrt(jnp.mean(xv * xv, axis=-1, keepdims=True) + EPS)
        xh = xv * r
        wv = w_ref[...]
        err = xh * wv - t_ref[...]
        row_loss = jnp.mean(err * err, axis=-1, keepdims=True)
        loss_ref[...] += 0.5 * jnp.sum(row_loss, axis=0, keepdims=True)
        dy = err * (1.0 / Dm)
        dxh = dy * wv
        t = jnp.mean(dxh * xh, axis=-1, keepdims=True)
        dx = r * (dxh - xh * t)
        dx_ref[...] = dx
        dxb_ref[...] = dx.astype(BF16)
        dw_ref[...] += jnp.sum(dy * xh, axis=0, keepdims=True)

    row = pl.BlockSpec((tm, Dm), lambda i: (i, 0))
    vec = pl.BlockSpec((1, Dm), lambda i: (0, 0))
    return pl.pallas_call(
        body, name=name, grid=(T // tm,),
        in_specs=[row, row, vec],
        out_specs=[pl.BlockSpec((1, 1), lambda i: (0, 0)), row, row, vec],
        out_shape=[jax.ShapeDtypeStruct((1, 1), F32), jax.ShapeDtypeStruct((T, Dm), F32),
                   jax.ShapeDtypeStruct((T, Dm), BF16), jax.ShapeDtypeStruct((1, Dm), F32)],
        compiler_params=_params(("arbitrary",)),
    )(x2, target, w)


GLA_TB = 512
GLA_NC = GLA_TB // CHUNK


def _cumsum_rows(x, row, reverse):
    n = x.shape[0]
    s = 1
    while s < n:
        if not reverse:
            x = x + jnp.where(row >= s, pltpu.roll(x, s, 0), 0.0)
        else:
            x = x + jnp.where(row < n - s, pltpu.roll(x, n - s, 0), 0.0)
        s *= 2
    return x


def _gla_gates(uq, z, lbv):
    q = uq * _sigmoid(uq)
    sg = _sigmoid(z)
    sgn = _sigmoid(-z)
    f = lbv + (1.0 - lbv) * sg
    k = (1.0 - lbv) * sgn
    return q, sg, sgn, f, k


def _gla_decays(f, row, reverse):
    b = _cumsum_rows(jnp.log(f), row, reverse)
    if not reverse:
        bref, blast = b[CHUNK // 2 - 1:CHUNK // 2, :], b[CHUNK - 1:CHUNK, :]
    else:
        bref, blast = b[CHUNK // 2:CHUNK // 2 + 1, :], b[0:1, :]
    return b, bref, blast


def _gla_fwd(U, lb, *, f_block, reverse, name):
    T = U.shape[0]
    nb = T // GLA_TB

    def body(uq_ref, uf_ref, ui_ref, lb_ref, o_ref, st_ref, s_ref):
        @pl.when(pl.program_id(0) == 0)
        def _():
            s_ref[...] = jnp.zeros_like(s_ref)

        row = lax.broadcasted_iota(jnp.int32, (CHUNK, HG_D), 0)
        ri = lax.broadcasted_iota(jnp.int32, (CHUNK, CHUNK), 0)
        ci = lax.broadcasted_iota(jnp.int32, (CHUNK, CHUNK), 1)
        mask = (ri <= ci) if reverse else (ri >= ci)

        def chunk(j, carry):
            c = (GLA_NC - 1 - j) if reverse else j
            rows = pl.ds(pl.multiple_of(c * CHUNK, CHUNK), CHUNK)
            for h in range(HG_HEADS):
                cols = pl.ds(h * HG_D, HG_D)
                v = ui_ref[rows, cols]
                q, _, _, f, k = _gla_gates(uq_ref[rows, cols], uf_ref[rows, cols], lb_ref[:, cols])
                b, bref, blast = _gla_decays(f, row, reverse)
                s = jnp.where(mask, _dot_nt(q * jnp.exp(b - bref), k * jnp.exp(bref - b)), 0.0)
                st = s_ref[h]
                st_ref[c, h] = st
                o_ref[rows, cols] = _dot(s, v) + _dot_nt(q * jnp.exp(b), st)
                s_ref[h] = st * jnp.exp(blast) + _dot_tn(v, k * jnp.exp(blast - b))
            return carry

        lax.fori_loop(0, GLA_NC, chunk, 0)

    blk = (lambda i: nb - 1 - i) if reverse else (lambda i: i)
    ucol = lambda cb: pl.BlockSpec((GLA_TB, HG_W), lambda i: (blk(i), cb))
    return pl.pallas_call(
        body, name=name, grid=(nb,),
        in_specs=[ucol(0), ucol(f_block), ucol(3), pl.BlockSpec((1, HG_W), lambda i: (0, 0))],
        out_specs=[pl.BlockSpec((GLA_TB, HG_W), lambda i: (blk(i), 0)),
                   pl.BlockSpec((GLA_NC, HG_HEADS, HG_D, HG_D), lambda i: (blk(i), 0, 0, 0))],
        out_shape=[jax.ShapeDtypeStruct((T, HG_W), F32),
                   jax.ShapeDtypeStruct((T // CHUNK, HG_HEADS, HG_D, HG_D), F32)],
        scratch_shapes=[pltpu.VMEM((HG_HEADS, HG_D, HG_D), F32)],
        compiler_params=_params(("arbitrary",)),
    )(U, U, U, lb)


def _gla_bwd(U, lb, do, states, *, f_block, reverse, name):
    T = U.shape[0]
    nb = T // GLA_TB

    def body(uq_ref, uf_ref, ui_ref, lb_ref, do_ref, st_ref, dq_ref, dz_ref, dv_ref, dlb_ref, ds_ref):
        @pl.when(pl.program_id(0) == 0)
        def _():
            ds_ref[...] = jnp.zeros_like(ds_ref)
            dlb_ref[...] = jnp.zeros_like(dlb_ref)

        row = lax.broadcasted_iota(jnp.int32, (CHUNK, HG_D), 0)
        ri = lax.broadcasted_iota(jnp.int32, (CHUNK, CHUNK), 0)
        ci = lax.broadcasted_iota(jnp.int32, (CHUNK, CHUNK), 1)
        mask = (ri <= ci) if reverse else (ri >= ci)

        def chunk(j, carry):
            c = j if reverse else (GLA_NC - 1 - j)
            rows = pl.ds(pl.multiple_of(c * CHUNK, CHUNK), CHUNK)
            for h in range(HG_HEADS):
                cols = pl.ds(h * HG_D, HG_D)
                v = ui_ref[rows, cols]
                lbv = lb_ref[:, cols]
                q, sg, sgn, f, k = _gla_gates(uq_ref[rows, cols], uf_ref[rows, cols], lbv)
                b, bref, blast = _gla_decays(f, row, reverse)
                eq, ek, eb, el, dec = (jnp.exp(b - bref), jnp.exp(bref - b), jnp.exp(b), jnp.exp(blast - b),
                                       jnp.exp(blast))
                qin, kin, qb, klast = q * eq, k * ek, q * eb, k * el
                dov = do_ref[rows, cols]
                st = st_ref[c, h]
                dst = ds_ref[h]
                p = jnp.where(mask, _dot_nt(qin, kin), 0.0)
                dp = jnp.where(mask, _dot_nt(dov, v), 0.0)
                dqin = _dot(dp, kin)
                dkin = _dot_tn(dp, qin)
                dv_ref[rows, cols] = _dot_tn(p, dov) + _dot_nt(klast, dst)
                dqb = _dot(dov, st)
                dklast = _dot(v, dst)
                ds_ref[h] = _dot_tn(dov, qb) + dst * dec
                db = dqin * qin - dkin * kin + dqb * qb - dklast * klast
                extra = (jnp.sum(dklast * klast, axis=0, keepdims=True)
                         + dec * jnp.sum(st * dst, axis=0, keepdims=True))
                dg = _cumsum_rows(db, row, not reverse) + extra
                dq_ref[rows, cols] = dqin * eq + dqb * eb
                dk = dkin * ek + dklast * el
                dfk = dg / f - dk
                dz_ref[rows, cols] = dfk * (1.0 - lbv) * sg * sgn
                dlb_ref[:, cols] += jnp.sum(dfk * sgn, axis=0, keepdims=True)
            return carry

        lax.fori_loop(0, GLA_NC, chunk, 0)

    blk = (lambda i: i) if reverse else (lambda i: nb - 1 - i)
    ucol = lambda cb: pl.BlockSpec((GLA_TB, HG_W), lambda i: (blk(i), cb))
    tok = pl.BlockSpec((GLA_TB, HG_W), lambda i: (blk(i), 0))
    vec = pl.BlockSpec((1, HG_W), lambda i: (0, 0))
    return pl.pallas_call(
        body, name=name, grid=(nb,),
        in_specs=[ucol(0), ucol(f_block), ucol(3), vec, tok,
                  pl.BlockSpec((GLA_NC, HG_HEADS, HG_D, HG_D), lambda i: (blk(i), 0, 0, 0))],
        out_specs=[tok, tok, tok, vec],
        out_shape=[jax.ShapeDtypeStruct((T, HG_W), F32)] * 3 + [jax.ShapeDtypeStruct((1, HG_W), F32)],
        scratch_shapes=[pltpu.VMEM((HG_HEADS, HG_D, HG_D), F32)],
        compiler_params=_params(("arbitrary",)),
    )(U, U, U, lb, do, states)


def _hg_post_fwd(o_f, o_b, U, w, *, name, tm=512):
    T = o_f.shape[0]

    def body(of_ref, ob_ref, ug_ref, w_ref, out_ref):
        wv = w_ref[...]
        for h in range(HG_HEADS):
            cols = pl.ds(h * HG_D, HG_D)
            o = of_ref[:, cols] + ob_ref[:, cols]
            r = lax.rsqrt(jnp.mean(o * o, axis=-1, keepdims=True) + EPS)
            ug = ug_ref[:, cols]
            out_ref[:, cols] = (o * r * wv * (ug * _sigmoid(ug))).astype(BF16)

    tok = pl.BlockSpec((tm, HG_W), lambda i: (i, 0))
    return pl.pallas_call(
        body, name=name, grid=(T // tm,),
        in_specs=[tok, tok, pl.BlockSpec((tm, HG_W), lambda i: (i, 4)), pl.BlockSpec((1, HG_D), lambda i: (0, 0))],
        out_specs=tok, out_shape=jax.ShapeDtypeStruct((T, HG_W), BF16),
        compiler_params=_params(("parallel",)),
    )(o_f, o_b, U, w)


def _hg_post_bwd(dmix, o_f, o_b, U, w, *, name, tm=512):
    T = o_f.shape[0]

    def body(dm_ref, of_ref, ob_ref, ug_ref, w_ref, do_ref, dug_ref, dw_ref):
        @pl.when(pl.program_id(0) == 0)
        def _():
            dw_ref[...] = jnp.zeros_like(dw_ref)

        wv = w_ref[...]
        for h in range(HG_HEADS):
            cols = pl.ds(h * HG_D, HG_D)
            o = of_ref[:, cols] + ob_ref[:, cols]
            r = lax.rsqrt(jnp.mean(o * o, axis=-1, keepdims=True) + EPS)
            xh = o * r
            ug = ug_ref[:, cols]
            sg = _sigmoid(ug)
            dm = dm_ref[:, cols]
            dn = dm * (ug * sg)
            dug_ref[:, cols] = dm * (xh * wv) * (sg * (1.0 + ug * (1.0 - sg)))
            dxh = dn * wv
            t = jnp.mean(dxh * xh, axis=-1, keepdims=True)
            do_ref[:, cols] = r * (dxh - xh * t)
            dw_ref[:, cols] += jnp.sum(dn * xh, axis=0, keepdims=True)

    tok = pl.BlockSpec((tm, HG_W), lambda i: (i, 0))
    vec = pl.BlockSpec((1, HG_W), lambda i: (0, 0))
    return pl.pallas_call(
        body, name=name, grid=(T // tm,),
        in_specs=[tok, tok, tok, pl.BlockSpec((tm, HG_W), lambda i: (i, 4)), pl.BlockSpec((1, HG_D), lambda i: (0, 0))],
        out_specs=[tok, tok, vec],
        out_shape=[jax.ShapeDtypeStruct((T, HG_W), F32)] * 2 + [jax.ShapeDtypeStruct((1, HG_W), F32)],
        compiler_params=_params(("arbitrary",)),
    )(dmix, o_f, o_b, U, w)


def _rope_tables(T):
    rows = T // GRID_W
    row = jnp.repeat(jnp.arange(rows), GRID_W).astype(F32)
    col = jnp.tile(jnp.arange(GRID_W), rows).astype(F32)
    axis_dim = ATT_DH // 2
    freqs = ROPE_THETA ** (-jnp.arange(0, axis_dim, 2, dtype=F32) / axis_dim)
    ang = jnp.concatenate([row[:, None] * freqs, col[:, None] * freqs], axis=-1)
    cos, sin = jnp.cos(ang), jnp.sin(ang)
    c = jnp.repeat(cos, 2, axis=-1)
    s = jnp.stack([-sin, sin], axis=-1).reshape(T, ATT_DH)
    return jnp.tile(c, (1, 2)), jnp.tile(s, (1, 2))


def _head_blockdiag(width):
    shift = ATT_DH.bit_length() - 1
    ri = jnp.right_shift(lax.broadcasted_iota(jnp.int32, (width, width), 0), shift)
    ci = jnp.right_shift(lax.broadcasted_iota(jnp.int32, (width, width), 1), shift)
    return jnp.where(ri == ci, 1.0, 0.0).astype(BF16)


def _head_sum(x, bd):
    hi = x.astype(BF16)
    lo = (x - hi.astype(F32)).astype(BF16)
    return jnp.dot(hi, bd, preferred_element_type=F32) + jnp.dot(lo, bd, preferred_element_type=F32)


def _pair_swap(x, even):
    n = x.shape[-1]
    return jnp.where(even, pltpu.roll(x, n - 1, 1), pltpu.roll(x, 1, 1))


def _att_prep_fwd(U, cos, sin, qw, kw, *, name, tm=512):
    T = U.shape[0]
    scale = ATT_DH ** -0.5

    def body(aq_ref, ak_ref, av_ref, c_ref, s_ref, qw_ref, kw_ref, q_ref, k_ref, v_ref):
        bd = _head_blockdiag(ATT_QW)
        c2, s2 = c_ref[...], s_ref[...]
        c8, s8 = jnp.tile(c2, (1, 4)), jnp.tile(s2, (1, 4))

        def norm_rope(x, w, c, s, bdm):
            r = lax.rsqrt(_head_sum(x * x, bdm) * (1.0 / ATT_DH) + EPS)
            y = x * r * w
            even = (lax.broadcasted_iota(jnp.int32, y.shape, 1) & 1) == 0
            return y * c + _pair_swap(y, even) * s

        q_ref[...] = (norm_rope(aq_ref[...], qw_ref[...], c8, s8, bd) * scale).astype(BF16)
        k_ref[...] = norm_rope(ak_ref[...], kw_ref[...], c2, s2, bd[:ATT_KW, :ATT_KW]).astype(BF16)
        v_ref[...] = av_ref[...].astype(BF16)

    kv_spec = pl.BlockSpec((tm, ATT_KW), lambda i: (i, 0))
    return pl.pallas_call(
        body, name=name, grid=(T // tm,),
        in_specs=[pl.BlockSpec((tm, ATT_QW), lambda i: (i, 5)),
                  pl.BlockSpec((tm, ATT_KW), lambda i: (i, 24)), pl.BlockSpec((tm, ATT_KW), lambda i: (i, 25)),
                  kv_spec, kv_spec,
                  pl.BlockSpec((1, ATT_QW), lambda i: (0, 0)), pl.BlockSpec((1, ATT_KW), lambda i: (0, 0))],
        out_specs=[pl.BlockSpec((tm, ATT_QW), lambda i: (i, 0)), kv_spec, kv_spec],
        out_shape=[jax.ShapeDtypeStruct((T, ATT_QW), BF16), jax.ShapeDtypeStruct((T, ATT_KW), BF16),
                   jax.ShapeDtypeStruct((T, ATT_KW), BF16)],
        compiler_params=_params(("parallel",)),
    )(U, U, U, cos, sin, qw, kw)


def _att_prep_bwd(U, dq, dk, cos, sin, qw, kw, *, name, tm=512):
    T = U.shape[0]
    scale = ATT_DH ** -0.5

    def body(aq_ref, ak_ref, dq_ref, dk_ref, c_ref, s_ref, qw_ref, kw_ref, daq_ref, dak_ref, dqw_ref, dkw_ref):
        @pl.when(pl.program_id(0) == 0)
        def _():
            dqw_ref[...] = jnp.zeros_like(dqw_ref)
            dkw_ref[...] = jnp.zeros_like(dkw_ref)

        bd = _head_blockdiag(ATT_QW)
        c2, s2 = c_ref[...], s_ref[...]
        c8, s8 = jnp.tile(c2, (1, 4)), jnp.tile(s2, (1, 4))

        def bwd(x, dy, w, c, s, bdm):
            even = (lax.broadcasted_iota(jnp.int32, x.shape, 1) & 1) == 0
            dn = dy * c - _pair_swap(dy, even) * s
            r = lax.rsqrt(_head_sum(x * x, bdm) * (1.0 / ATT_DH) + EPS)
            xh = x * r
            dxh = dn * w
            t = _head_sum(dxh * xh, bdm) * (1.0 / ATT_DH)
            return r * (dxh - xh * t), jnp.sum(dn * xh, axis=0, keepdims=True)

        da, dw = bwd(aq_ref[...], dq_ref[...] * scale, qw_ref[...], c8, s8, bd)
        daq_ref[...] = da
        dqw_ref[...] += dw
        da, dw = bwd(ak_ref[...], dk_ref[...], kw_ref[...], c2, s2, bd[:ATT_KW, :ATT_KW])
        dak_ref[...] = da
        dkw_ref[...] += dw

    q_spec = pl.BlockSpec((tm, ATT_QW), lambda i: (i, 0))
    kv_spec = pl.BlockSpec((tm, ATT_KW), lambda i: (i, 0))
    qv = pl.BlockSpec((1, ATT_QW), lambda i: (0, 0))
    kv = pl.BlockSpec((1, ATT_KW), lambda i: (0, 0))
    return pl.pallas_call(
        body, name=name, grid=(T // tm,),
        in_specs=[pl.BlockSpec((tm, ATT_QW), lambda i: (i, 5)), pl.BlockSpec((tm, ATT_KW), lambda i: (i, 24)),
                  q_spec, kv_spec, kv_spec, kv_spec, qv, kv],
        out_specs=[q_spec, kv_spec, qv, kv],
        out_shape=[jax.ShapeDtypeStruct((T, ATT_QW), F32), jax.ShapeDtypeStruct((T, ATT_KW), F32),
                   jax.ShapeDtypeStruct((1, ATT_QW), F32), jax.ShapeDtypeStruct((1, ATT_KW), F32)],
        compiler_params=_params(("arbitrary",)),
    )(U, U, dq, dk, cos, sin, qw, kw)


FA_TQ = 256
FA_TK = 256
FA_SW = 128


def _fa_tiles(T):
    tq, tk = min(FA_TQ, T), min(FA_TK, T)
    return tq, tk, T // tq, T // tk


def _to_fa_cols(a, T):
    tq, _, nq, _ = _fa_tiles(T)
    return a.reshape(nq, tq, ATT_KV, ATT_G, ATT_DH).transpose(2, 0, 4, 3, 1).reshape(ATT_KV, nq, ATT_DH, ATT_G * tq)


def _to_fa_rows(a, T):
    tq, _, nq, _ = _fa_tiles(T)
    return a.reshape(nq, tq, ATT_KV, ATT_G, ATT_DH).transpose(2, 0, 3, 1, 4).reshape(ATT_KV, nq, ATT_G * tq, ATT_DH)


def _from_fa_cols(a, T):
    tq, _, nq, _ = _fa_tiles(T)
    return a.reshape(ATT_KV, nq, ATT_DH, ATT_G, tq).transpose(1, 4, 0, 3, 2).reshape(T, ATT_QW)


def _kv_rows(a, T):
    _, tk, _, n_k = _fa_tiles(T)
    return a.reshape(n_k, tk, ATT_KV, ATT_DH).transpose(2, 0, 1, 3)


def _kv_cols(a, T):
    _, tk, _, n_k = _fa_tiles(T)
    return a.reshape(n_k, tk, ATT_KV, ATT_DH).transpose(2, 0, 3, 1)


def _flash_fwd(q_c, k_r, v_c, *, name):
    _, nq, _, R = q_c.shape
    _, n_k, tk, _ = k_r.shape

    def body(q_ref, k_ref, v_ref, o_ref, lse_ref):
        for st in range(R // FA_SW):
            lanes = pl.ds(st * FA_SW, FA_SW)
            qv = q_ref[0, 0, :, lanes]

            def step(j, carry):
                m, l, acc = carry
                s = jnp.dot(k_ref[0, j], qv, preferred_element_type=F32)
                m_new = jnp.maximum(m, jnp.max(s, axis=0, keepdims=True))
                alpha = jnp.exp(m - m_new)
                p = jnp.exp(s - m_new)
                l = alpha * l + jnp.sum(p, axis=0, keepdims=True)
                acc = alpha * acc + jnp.dot(v_ref[0, j], p.astype(BF16), preferred_element_type=F32)
                return m_new, l, acc

            m, l, acc = lax.fori_loop(0, n_k, step, (jnp.full((1, FA_SW), -jnp.inf, F32), jnp.zeros((1, FA_SW), F32),
                                                     jnp.zeros((ATT_DH, FA_SW), F32)))
            o_ref[0, 0, :, lanes] = acc / l
            lse_ref[0, 0, :, lanes] = m + jnp.log(l)

    qspec = pl.BlockSpec((1, 1, ATT_DH, R), lambda h, i: (h, i, 0, 0))
    return pl.pallas_call(
        body, name=name, grid=(ATT_KV, nq),
        in_specs=[qspec, pl.BlockSpec((1, n_k, tk, ATT_DH), lambda h, i: (h, 0, 0, 0)),
                  pl.BlockSpec((1, n_k, ATT_DH, tk), lambda h, i: (h, 0, 0, 0))],
        out_specs=[qspec, pl.BlockSpec((1, 1, 1, R), lambda h, i: (h, i, 0, 0))],
        out_shape=[jax.ShapeDtypeStruct((ATT_KV, nq, ATT_DH, R), F32), jax.ShapeDtypeStruct((ATT_KV, nq, 1, R), F32)],
        compiler_params=_params(("parallel", "parallel")),
    )(q_c, k_r, v_c)


def _flash_bwd(q_c, q_r, k_r, k_c, v_r, do_c, do_r, o_c, lse, *, name):
    _, nq, _, R = q_c.shape
    _, n_k, tk, _ = k_r.shape

    def body(qc_ref, qr_ref, kr_ref, kc_ref, vr_ref, doc_ref, dor_ref, oc_ref, lse_ref, dq_ref, dk_ref, dv_ref,
             acc_ref):
        @pl.when(pl.program_id(1) == 0)
        def _():
            dk_ref[...] = jnp.zeros_like(dk_ref)
            dv_ref[...] = jnp.zeros_like(dv_ref)

        qc, doc = qc_ref[0, 0], doc_ref[0, 0]
        qr, dor = qr_ref[0, 0], dor_ref[0, 0]
        delta = jnp.sum(doc.astype(F32) * oc_ref[0, 0], axis=0, keepdims=True)
        lsev = lse_ref[0, 0]
        acc_ref[...] = jnp.zeros_like(acc_ref)

        def step(j, carry):
            s = jnp.dot(kr_ref[0, j], qc, preferred_element_type=F32)
            p = jnp.exp(s - lsev)
            dp = jnp.dot(vr_ref[0, j], doc, preferred_element_type=F32)
            ds = (p * (dp - delta)).astype(BF16)
            acc_ref[...] += jnp.dot(kc_ref[0, j], ds, preferred_element_type=F32)
            dk_ref[0, j] += jnp.dot(ds, qr, preferred_element_type=F32)
            dv_ref[0, j] += jnp.dot(p.astype(BF16), dor, preferred_element_type=F32)
            return carry

        lax.fori_loop(0, n_k, step, 0)
        dq_ref[0, 0] = acc_ref[...]

    cspec = pl.BlockSpec((1, 1, ATT_DH, R), lambda h, i: (h, i, 0, 0))
    rspec = pl.BlockSpec((1, 1, R, ATT_DH), lambda h, i: (h, i, 0, 0))
    krspec = pl.BlockSpec((1, n_k, tk, ATT_DH), lambda h, i: (h, 0, 0, 0))
    kcspec = pl.BlockSpec((1, n_k, ATT_DH, tk), lambda h, i: (h, 0, 0, 0))
    return pl.pallas_call(
        body, name=name, grid=(ATT_KV, nq),
        in_specs=[cspec, rspec, krspec, kcspec, krspec, cspec, rspec, cspec,
                  pl.BlockSpec((1, 1, 1, R), lambda h, i: (h, i, 0, 0))],
        out_specs=[cspec, krspec, krspec],
        out_shape=[jax.ShapeDtypeStruct((ATT_KV, nq, ATT_DH, R), F32),
                   jax.ShapeDtypeStruct((ATT_KV, n_k, tk, ATT_DH), F32),
                   jax.ShapeDtypeStruct((ATT_KV, n_k, tk, ATT_DH), F32)],
        scratch_shapes=[pltpu.VMEM((ATT_DH, R), F32)],
        compiler_params=_params(("parallel", "arbitrary")),
    )(q_c, q_r, k_r, k_c, v_r, do_c, do_r, o_c, lse)


def _att_post_fwd(o, w, *, name, tm=512):
    T = o.shape[0]

    def body(o_ref, w_ref, out_ref):
        ov = o_ref[...]
        r = lax.rsqrt(jnp.mean(ov * ov, axis=-1, keepdims=True) + EPS)
        out_ref[...] = (ov * r * w_ref[...]).astype(BF16)

    tok = pl.BlockSpec((tm, ATT_QW), lambda i: (i, 0))
    return pl.pallas_call(
        body, name=name, grid=(T // tm,),
        in_specs=[tok, pl.BlockSpec((1, ATT_QW), lambda i: (0, 0))],
        out_specs=tok, out_shape=jax.ShapeDtypeStruct((T, ATT_QW), BF16),
        compiler_params=_params(("parallel",)),
    )(o, w)


def _att_post_bwd(dmix, o, w, *, name, tm=512):
    T = o.shape[0]

    def body(dm_ref, o_ref, w_ref, do_ref, dw_ref):
        @pl.when(pl.program_id(0) == 0)
        def _():
            dw_ref[...] = jnp.zeros_like(dw_ref)

        ov = o_ref[...]
        r = lax.rsqrt(jnp.mean(ov * ov, axis=-1, keepdims=True) + EPS)
        xh = ov * r
        dm = dm_ref[...]
        dxh = dm * w_ref[...]
        t = jnp.mean(dxh * xh, axis=-1, keepdims=True)
        do_ref[...] = (r * (dxh - xh * t)).astype(BF16)
        dw_ref[...] += jnp.sum(dm * xh, axis=0, keepdims=True)

    tok = pl.BlockSpec((tm, ATT_QW), lambda i: (i, 0))
    vec = pl.BlockSpec((1, ATT_QW), lambda i: (0, 0))
    return pl.pallas_call(
        body, name=name, grid=(T // tm,),
        in_specs=[pl.BlockSpec((tm, ATT_QW), lambda i: (i, 1)), tok, vec],
        out_specs=[tok, vec],
        out_shape=[jax.ShapeDtypeStruct((T, ATT_QW), BF16), jax.ShapeDtypeStruct((1, ATT_QW), F32)],
        compiler_params=_params(("arbitrary",)),
    )(dmix, o, w)


FA_HP = ATT_KV * ATT_DH
FA_TK_FWD = 512
FA_TK_BWD = 512


def _cols_from_tokens(x, kv):
    w = ATT_G * ATT_DH
    xt = x[:, kv * w:(kv + 1) * w].T
    return jnp.concatenate([xt[g * ATT_DH:(g + 1) * ATT_DH, :] for g in range(ATT_G)], axis=1)


def _tokens_from_cols(c):
    tq = c.shape[1] // ATT_G
    return jnp.concatenate([c[:, g * tq:(g + 1) * tq] for g in range(ATT_G)], axis=0).T


def _store_padded_cols(ref, x):
    for kv in range(ATT_KV):
        cols = _cols_from_tokens(x, kv).astype(BF16)
        ref[kv, 0, kv * ATT_DH:(kv + 1) * ATT_DH, :] = cols
        ref[kv, 0, (1 - kv) * ATT_DH:(2 - kv) * ATT_DH, :] = jnp.zeros_like(cols)


def _att_prep_fwd2(U, cos, sin, qw, kw, *, name):
    T = U.shape[0]
    tm = min(FA_TQ, T)
    R = ATT_G * tm
    scale = ATT_DH ** -0.5

    def body(aq_ref, ak_ref, av_ref, c_ref, s_ref, qw_ref, kw_ref, q_ref, k_ref, v_ref):
        bd = _head_blockdiag(ATT_QW)
        c2, s2 = c_ref[...], s_ref[...]
        c8, s8 = jnp.tile(c2, (1, 4)), jnp.tile(s2, (1, 4))

        def norm_rope(x, w, c, s, bdm):
            r = lax.rsqrt(_head_sum(x * x, bdm) * (1.0 / ATT_DH) + EPS)
            y = x * r * w
            even = (lax.broadcasted_iota(jnp.int32, y.shape, 1) & 1) == 0
            return y * c + _pair_swap(y, even) * s

        _store_padded_cols(q_ref, norm_rope(aq_ref[...], qw_ref[...], c8, s8, bd) * scale)
        k_ref[...] = norm_rope(ak_ref[...], kw_ref[...], c2, s2, bd[:ATT_KW, :ATT_KW]).astype(BF16)
        v_ref[...] = av_ref[...].astype(BF16)

    kv_spec = pl.BlockSpec((tm, ATT_KW), lambda i: (i, 0))
    return pl.pallas_call(
        body, name=name, grid=(T // tm,),
        in_specs=[pl.BlockSpec((tm, ATT_QW), lambda i: (i, 5)),
                  pl.BlockSpec((tm, ATT_KW), lambda i: (i, 24)), pl.BlockSpec((tm, ATT_KW), lambda i: (i, 25)),
                  kv_spec, kv_spec,
                  pl.BlockSpec((1, ATT_QW), lambda i: (0, 0)), pl.BlockSpec((1, ATT_KW), lambda i: (0, 0))],
        out_specs=[pl.BlockSpec((ATT_KV, 1, FA_HP, R), lambda i: (0, i, 0, 0)), kv_spec, kv_spec],
        out_shape=[jax.ShapeDtypeStruct((ATT_KV, T // tm, FA_HP, R), BF16),
                   jax.ShapeDtypeStruct((T, ATT_KW), BF16), jax.ShapeDtypeStruct((T, ATT_KW), BF16)],
        compiler_params=_params(("parallel",)),
    )(U, U, U, cos, sin, qw, kw)


def _att_prep_bwd2(U, dq_c, dk, cos, sin, qw, kw, *, name):
    T = U.shape[0]
    tm = min(FA_TQ, T)
    R = ATT_G * tm
    scale = ATT_DH ** -0.5

    def body(aq_ref, ak_ref, dq_ref, dk_ref, c_ref, s_ref, qw_ref, kw_ref, daq_ref, dak_ref, dqw_ref, dkw_ref):
        @pl.when(pl.program_id(0) == 0)
        def _():
            dqw_ref[...] = jnp.zeros_like(dqw_ref)
            dkw_ref[...] = jnp.zeros_like(dkw_ref)

        bd = _head_blockdiag(ATT_QW)
        c2, s2 = c_ref[...], s_ref[...]
        c8, s8 = jnp.tile(c2, (1, 4)), jnp.tile(s2, (1, 4))

        def bwd(x, dy, w, c, s, bdm):
            even = (lax.broadcasted_iota(jnp.int32, x.shape, 1) & 1) == 0
            dn = dy * c - _pair_swap(dy, even) * s
            r = lax.rsqrt(_head_sum(x * x, bdm) * (1.0 / ATT_DH) + EPS)
            xh = x * r
            dxh = dn * w
            t = _head_sum(dxh * xh, bdm) * (1.0 / ATT_DH)
            return r * (dxh - xh * t), jnp.sum(dn * xh, axis=0, keepdims=True)

        dq = jnp.concatenate([_tokens_from_cols(dq_ref[kv, 0]) for kv in range(ATT_KV)], axis=1)
        da, dw = bwd(aq_ref[...], dq * scale, qw_ref[...], c8, s8, bd)
        daq_ref[...] = da
        dqw_ref[...] += dw
        da, dw = bwd(ak_ref[...], dk_ref[...], kw_ref[...], c2, s2, bd[:ATT_KW, :ATT_KW])
        dak_ref[...] = da
        dkw_ref[...] += dw

    q_spec = pl.BlockSpec((tm, ATT_QW), lambda i: (i, 0))
    kv_spec = pl.BlockSpec((tm, ATT_KW), lambda i: (i, 0))
    qv = pl.BlockSpec((1, ATT_QW), lambda i: (0, 0))
    kv = pl.BlockSpec((1, ATT_KW), lambda i: (0, 0))
    return pl.pallas_call(
        body, name=name, grid=(T // tm,),
        in_specs=[pl.BlockSpec((tm, ATT_QW), lambda i: (i, 5)), pl.BlockSpec((tm, ATT_KW), lambda i: (i, 24)),
                  pl.BlockSpec((ATT_KV, 1, ATT_DH, R), lambda i: (0, i, 0, 0)), kv_spec, kv_spec, kv_spec, qv, kv],
        out_specs=[q_spec, kv_spec, qv, kv],
        out_shape=[jax.ShapeDtypeStruct((T, ATT_QW), F32), jax.ShapeDtypeStruct((T, ATT_KW), F32),
                   jax.ShapeDtypeStruct((1, ATT_QW), F32), jax.ShapeDtypeStruct((1, ATT_KW), F32)],
        compiler_params=_params(("arbitrary",)),
    )(U, U, dq_c, dk, cos, sin, qw, kw)


def _pick_head(x, kv):
    return jnp.where(kv == 0, x[0:ATT_DH, :], x[ATT_DH:FA_HP, :])


def _flash_fwd2(q_c, k, v, *, name):
    _, nq, _, R = q_c.shape
    T = k.shape[0]
    tk = min(FA_TK_FWD, T)
    n_k = T // tk

    def body(q_ref, k_ref, v_ref, o_ref, lse_ref, acc_ref):
        kv = pl.program_id(0)
        qv = q_ref[0, 0]
        acc_ref[...] = jnp.zeros_like(acc_ref)

        def step(j, carry):
            m, l = carry
            s = jnp.dot(k_ref[j], qv, preferred_element_type=F32)
            m_new = jnp.maximum(m, jnp.max(s, axis=0, keepdims=True))
            alpha = jnp.exp(m - m_new)
            p = jnp.exp(s - m_new)
            l = alpha * l + jnp.sum(p, axis=0, keepdims=True)
            pv = lax.dot_general(v_ref[j], p.astype(BF16), (((0,), (0,)), ((), ())), preferred_element_type=F32)
            acc_ref[...] = alpha * acc_ref[...] + _pick_head(pv, kv)
            return m_new, l

        m, l = lax.fori_loop(0, n_k, step, (jnp.full((1, R), -jnp.inf, F32), jnp.zeros((1, R), F32)))
        o_ref[0, 0] = acc_ref[...] / l
        lse_ref[0, 0] = m + jnp.log(l)

    kspec = pl.BlockSpec((n_k, tk, FA_HP), lambda h, i: (0, 0, 0))
    return pl.pallas_call(
        body, name=name, grid=(ATT_KV, nq),
        in_specs=[pl.BlockSpec((1, 1, FA_HP, R), lambda h, i: (h, i, 0, 0)), kspec, kspec],
        out_specs=[pl.BlockSpec((1, 1, ATT_DH, R), lambda h, i: (h, i, 0, 0)),
                   pl.BlockSpec((1, 1, 1, R), lambda h, i: (h, i, 0, 0))],
        out_shape=[jax.ShapeDtypeStruct((ATT_KV, nq, ATT_DH, R), F32), jax.ShapeDtypeStruct((ATT_KV, nq, 1, R), F32)],
        scratch_shapes=[pltpu.VMEM((ATT_DH, R), F32)],
        compiler_params=_params(("parallel", "parallel")),
    )(q_c, k.reshape(n_k, tk, FA_HP), v.reshape(n_k, tk, FA_HP))


def _flash_bwd2(q_c, k, v, do_c, lse, delta, *, name):
    _, nq, _, R = q_c.shape
    T = k.shape[0]
    tk = min(FA_TK_BWD, T)
    n_k = T // tk

    def body(qc_ref, k_ref, v_ref, doc_ref, lse_ref, delta_ref, dq_ref, dk_ref, dv_ref, acc_ref):
        kv = pl.program_id(0)

        @pl.when((kv == 0) & (pl.program_id(1) == 0))
        def _():
            dk_ref[...] = jnp.zeros_like(dk_ref)
            dv_ref[...] = jnp.zeros_like(dv_ref)

        qc, doc = qc_ref[0, 0], doc_ref[0, 0]
        lsev, delta = lse_ref[0, 0], delta_ref[0, 0]
        acc_ref[...] = jnp.zeros_like(acc_ref)

        def step(j, carry):
            kb = k_ref[j]
            s = jnp.dot(kb, qc, preferred_element_type=F32)
            p = jnp.exp(s - lsev)
            dp = jnp.dot(v_ref[j], doc, preferred_element_type=F32)
            ds = (p * (dp - delta)).astype(BF16)
            acc_ref[...] += lax.dot_general(kb, ds, (((0,), (0,)), ((), ())), preferred_element_type=F32)
            dk_ref[j] += lax.dot_general(ds, qc, (((1,), (1,)), ((), ())), preferred_element_type=F32)
            dv_ref[j] += lax.dot_general(p.astype(BF16), doc, (((1,), (1,)), ((), ())), preferred_element_type=F32)
            return carry

        lax.fori_loop(0, n_k, step, 0)
        dq_ref[0, 0] = _pick_head(acc_ref[...], kv)

    cspec = pl.BlockSpec((1, 1, FA_HP, R), lambda h, i: (h, i, 0, 0))
    vspec = pl.BlockSpec((1, 1, 1, R), lambda h, i: (h, i, 0, 0))
    kspec = pl.BlockSpec((n_k, tk, FA_HP), lambda h, i: (0, 0, 0))
    dq_c, dk, dv = pl.pallas_call(
        body, name=name, grid=(ATT_KV, nq),
        in_specs=[cspec, kspec, kspec, cspec, vspec, vspec],
        out_specs=[pl.BlockSpec((1, 1, ATT_DH, R), lambda h, i: (h, i, 0, 0)), kspec, kspec],
        out_shape=[jax.ShapeDtypeStruct((ATT_KV, nq, ATT_DH, R), F32),
                   jax.ShapeDtypeStruct((n_k, tk, FA_HP), F32), jax.ShapeDtypeStruct((n_k, tk, FA_HP), F32)],
        scratch_shapes=[pltpu.VMEM((FA_HP, R), F32)],
        compiler_params=_params(("arbitrary", "arbitrary")),
    )(q_c, k.reshape(n_k, tk, FA_HP), v.reshape(n_k, tk, FA_HP), do_c, lse, delta)
    return dq_c, dk.reshape(T, FA_HP), dv.reshape(T, FA_HP)


def _att_post_fwd2(o_c, w, *, name):
    _, nq, _, R = o_c.shape
    tm = R // ATT_G
    T = nq * tm

    def body(oc_ref, w_ref, o_ref, out_ref):
        ov = jnp.concatenate([_tokens_from_cols(oc_ref[kv, 0]) for kv in range(ATT_KV)], axis=1)
        r = lax.rsqrt(jnp.mean(ov * ov, axis=-1, keepdims=True) + EPS)
        o_ref[...] = ov
        out_ref[...] = (ov * r * w_ref[...]).astype(BF16)

    tok = pl.BlockSpec((tm, ATT_QW), lambda i: (i, 0))
    return pl.pallas_call(
        body, name=name, grid=(nq,),
        in_specs=[pl.BlockSpec((ATT_KV, 1, ATT_DH, R), lambda i: (0, i, 0, 0)), pl.BlockSpec((1, ATT_QW), lambda i: (0, 0))],
        out_specs=[tok, tok],
        out_shape=[jax.ShapeDtypeStruct((T, ATT_QW), F32), jax.ShapeDtypeStruct((T, ATT_QW), BF16)],
        compiler_params=_params(("parallel",)),
    )(o_c, w)


def _att_post_bwd2(dmix, o, w, *, name):
    T = o.shape[0]
    tm = min(FA_TQ, T)
    R = ATT_G * tm

    def body(dm_ref, o_ref, w_ref, do_ref, delta_ref, dw_ref):
        @pl.when(pl.program_id(0) == 0)
        def _():
            dw_ref[...] = jnp.zeros_like(dw_ref)

        ov = o_ref[...]
        r = lax.rsqrt(jnp.mean(ov * ov, axis=-1, keepdims=True) + EPS)
        xh = ov * r
        dm = dm_ref[...]
        dxh = dm * w_ref[...]
        t = jnp.mean(dxh * xh, axis=-1, keepdims=True)
        do = r * (dxh - xh * t)
        _store_padded_cols(do_ref, do)
        dob = do.astype(BF16).astype(F32)
        for kv in range(ATT_KV):
            delta_ref[kv, 0] = jnp.sum(_cols_from_tokens(dob * ov, kv), axis=0, keepdims=True)
        dw_ref[...] += jnp.sum(dm * xh, axis=0, keepdims=True)

    tok = pl.BlockSpec((tm, ATT_QW), lambda i: (i, 0))
    vec = pl.BlockSpec((1, ATT_QW), lambda i: (0, 0))
    return pl.pallas_call(
        body, name=name, grid=(T // tm,),
        in_specs=[pl.BlockSpec((tm, ATT_QW), lambda i: (i, 1)), tok, vec],
        out_specs=[pl.BlockSpec((ATT_KV, 1, FA_HP, R), lambda i: (0, i, 0, 0)),
                   pl.BlockSpec((ATT_KV, 1, 1, R), lambda i: (0, i, 0, 0)), vec],
        out_shape=[jax.ShapeDtypeStruct((ATT_KV, T // tm, FA_HP, R), BF16),
                   jax.ShapeDtypeStruct((ATT_KV, T // tm, 1, R), F32), jax.ShapeDtypeStruct((1, ATT_QW), F32)],
        compiler_params=_params(("arbitrary",)),
    )(dmix, o, w)


def _ffn_up(h2, wg, wu, *, name, tm=512):
    T = h2.shape[0]
    tn = _pick(D_FF, 1408)

    def body(h_ref, wg_ref, wu_ref, g_ref, u_ref, a_ref):
        hv = h_ref[...]
        g = jnp.dot(hv, wg_ref[...], preferred_element_type=F32)
        u = jnp.dot(hv, wu_ref[...], preferred_element_type=F32)
        g_ref[...] = g.astype(BF16)
        u_ref[...] = u.astype(BF16)
        a_ref[...] = (g * _sigmoid(g) * u).astype(BF16)

    wspec = pl.BlockSpec((D_MODEL, tn), lambda i, j: (0, j))
    ospec = pl.BlockSpec((tm, tn), lambda i, j: (i, j))
    return pl.pallas_call(
        body, name=name, grid=(T // tm, D_FF // tn),
        in_specs=[pl.BlockSpec((tm, D_MODEL), lambda i, j: (i, 0)), wspec, wspec],
        out_specs=[ospec] * 3, out_shape=[jax.ShapeDtypeStruct((T, D_FF), BF16)] * 3,
        compiler_params=_params(("parallel", "arbitrary")),
    )(h2, wg, wu)


def _ffn_act_bwd(dx2b, w_down, gate, up, *, name, tm=512):
    T = dx2b.shape[0]
    tn = _pick(D_FF, 1408)

    def body(dx_ref, w_ref, g_ref, u_ref, dg_ref, du_ref):
        da = lax.dot_general(dx_ref[...], w_ref[...], (((1,), (1,)), ((), ())), preferred_element_type=F32)
        g = g_ref[...].astype(F32)
        u = u_ref[...].astype(F32)
        sg = _sigmoid(g)
        dg_ref[...] = (da * u * (sg * (1.0 + g * (1.0 - sg)))).astype(BF16)
        du_ref[...] = (da * (g * sg)).astype(BF16)

    ospec = pl.BlockSpec((tm, tn), lambda i, j: (i, j))
    return pl.pallas_call(
        body, name=name, grid=(T // tm, D_FF // tn),
        in_specs=[pl.BlockSpec((tm, D_MODEL), lambda i, j: (i, 0)),
                  pl.BlockSpec((tn, D_MODEL), lambda i, j: (j, 0)), ospec, ospec],
        out_specs=[ospec] * 2, out_shape=[jax.ShapeDtypeStruct((T, D_FF), BF16)] * 2,
        compiler_params=_params(("parallel", "arbitrary")),
    )(dx2b, w_down, gate, up)


def _assemble_du(U, dq_f, dq_b, dz_f, dz_b, dv_f, dv_b, du_g, da_q, da_k, da_v, *, name, tm=256):
    T = U.shape[0]

    def body(uq_ref, dqf, dqb, dzf, dzb, dvf, dvb, dug, daq, dak, dav, out_ref):
        uq = uq_ref[...]
        sg = _sigmoid(uq)
        out_ref[:, 0:HG_W] = ((dqf[...] + dqb[...]) * (sg * (1.0 + uq * (1.0 - sg)))).astype(BF16)
        out_ref[:, HG_W:2 * HG_W] = dzf[...].astype(BF16)
        out_ref[:, 2 * HG_W:3 * HG_W] = dzb[...].astype(BF16)
        out_ref[:, 3 * HG_W:4 * HG_W] = (dvf[...] + dvb[...]).astype(BF16)
        out_ref[:, 4 * HG_W:5 * HG_W] = dug[...].astype(BF16)
        out_ref[:, 5 * HG_W:5 * HG_W + ATT_QW] = daq[...].astype(BF16)
        out_ref[:, 5 * HG_W + ATT_QW:5 * HG_W + ATT_QW + ATT_KW] = dak[...].astype(BF16)
        out_ref[:, 5 * HG_W + ATT_QW + ATT_KW:D_IN] = dav[...].astype(BF16)

    tok = pl.BlockSpec((tm, HG_W), lambda i: (i, 0))
    kv = pl.BlockSpec((tm, ATT_KW), lambda i: (i, 0))
    return pl.pallas_call(
        body, name=name, grid=(T // tm,),
        in_specs=[tok] * 9 + [kv, kv],
        out_specs=pl.BlockSpec((tm, D_IN), lambda i: (i, 0)),
        out_shape=jax.ShapeDtypeStruct((T, D_IN), BF16),
        compiler_params=_params(("parallel",)),
    )(U, dq_f, dq_b, dz_f, dz_b, dv_f, dv_b, du_g, da_q, da_k, da_v)


def _adam_math(w, g, m, v):
    m = ADAM_B1 * m + (1.0 - ADAM_B1) * g
    v = ADAM_B2 * v + (1.0 - ADAM_B2) * (g * g)
    m_hat = m / (1.0 - ADAM_B1 ** ADAM_STEP)
    v_hat = v / (1.0 - ADAM_B2 ** ADAM_STEP)
    delta = -ADAM_LR * (m_hat / (jnp.sqrt(v_hat) + ADAM_EPS) + ADAM_WD * w)
    return delta, m, v


def _adamw(parts, w, m, v, *, name, tr_cap=256):
    P, R, C = parts.shape
    tr = R
    for t in range(8, min(R, tr_cap) + 1, 8):
        if R % t == 0:
            tr = t

    def body(p_ref, w_ref, m_ref, v_ref, g_ref, d_ref, nm_ref, nv_ref):
        g = p_ref[0].astype(F32)
        for j in range(1, P):
            g = g + p_ref[j].astype(F32)
        d, nm, nv = _adam_math(w_ref[...], g, m_ref[...], v_ref[...])
        g_ref[...] = g
        d_ref[...] = d
        nm_ref[...] = nm
        nv_ref[...] = nv

    blk = pl.BlockSpec((tr, C), lambda i: (i, 0))
    return pl.pallas_call(
        body, name=name, grid=(R // tr,),
        in_specs=[pl.BlockSpec((P, tr, C), lambda i: (0, i, 0)), blk, blk, blk],
        out_specs=[blk] * 4, out_shape=[jax.ShapeDtypeStruct((R, C), F32)] * 4,
        compiler_params=_params(("parallel",)),
    )(parts, w, m, v)


def _all_gather(xs, *, name):
    n = len(xs)

    def body(*refs):
        ins, outs = refs[:n], refs[n:2 * n]
        send_sems, recv_sems, local_sems = refs[2 * n:]
        x, y, c = lax.axis_index("x"), lax.axis_index("y"), lax.axis_index("c")
        me, sibling = (x, y, c), (x, y, 1 - c)
        chips = [(1 - x, y), (x, 1 - y), (1 - x, 1 - y)]

        def slot(p):
            return 4 * p[0] + 2 * p[1] + p[2]

        def copy(a, k, block, to, src=None):
            dst = outs[a].at[slot(block)]
            return pltpu.make_async_remote_copy(
                src_ref=dst if src is None else src, dst_ref=dst,
                send_sem=send_sems.at[a * 7 + k], recv_sem=recv_sems.at[a * 7 + k],
                device_id=to, device_id_type=MESH)

        mine = [pltpu.make_async_copy(ins[a], outs[a].at[slot(me)], local_sems.at[a]) for a in range(n)]
        for cp in mine:
            cp.start()
        first = []
        for a in range(n):
            first.append(copy(a, 0, me, sibling, src=ins[a]))
            first += [copy(a, 1 + j, me, (*chip, c), src=ins[a]) for j, chip in enumerate(chips)]
        for cp in first:
            cp.start()
        passed = []
        for j, chip in enumerate(chips):
            for a in range(n):
                copy(a, 1 + j, (*chip, c), me).wait_recv()
                cp = copy(a, 4 + j, (*chip, c), sibling)
                cp.start()
                passed.append(cp)
        for a in range(n):
            copy(a, 0, sibling, me).wait_recv()
            for j, chip in enumerate(chips):
                copy(a, 4 + j, (*chip, 1 - c), me).wait_recv()
        for cp in first + passed:
            cp.wait_send()
        for cp in mine:
            cp.wait()

    return pl.pallas_call(
        body, name=name,
        in_specs=[ANY] * n, out_specs=[ANY] * n,
        out_shape=[jax.ShapeDtypeStruct((N_DEV,) + x.shape, x.dtype) for x in xs],
        scratch_shapes=[pltpu.SemaphoreType.DMA((7 * n,)), pltpu.SemaphoreType.DMA((7 * n,)),
                        pltpu.SemaphoreType.DMA((n,))],
        compiler_params=pltpu.CompilerParams(has_side_effects=True),
    )(*xs)


def _exchange(gs, *, masks, slot, name):
    n, n_peers = len(gs), len(masks)

    def body(*refs):
        ins, outs = refs[:n], refs[n:2 * n]
        send_sems, recv_sems, local_sems = refs[2 * n:]
        x, y, c = lax.axis_index("x"), lax.axis_index("y"), lax.axis_index("c")
        my_slot = slot((x, y, c))

        def flip(v, bit):
            return 1 - v if bit else v

        mine = [pltpu.make_async_copy(ins[a].at[my_slot], outs[a].at[my_slot], local_sems.at[a]) for a in range(n)]
        for cp in mine:
            cp.start()
        copies = []
        for a in range(n):
            for k, (mx, my, mc) in enumerate(masks):
                peer = (flip(x, mx), flip(y, my), flip(c, mc))
                peer_slot = slot(peer)
                sems = dict(send_sem=send_sems.at[a * n_peers + k], recv_sem=recv_sems.at[a * n_peers + k],
                            device_id=peer, device_id_type=MESH)
                copies.append((
                    pltpu.make_async_remote_copy(src_ref=ins[a].at[peer_slot], dst_ref=outs[a].at[my_slot], **sems),
                    pltpu.make_async_remote_copy(src_ref=ins[a].at[peer_slot], dst_ref=outs[a].at[peer_slot], **sems)))
        for send, _ in copies:
            send.start()
        for send, recv in copies:
            recv.wait_recv()
            send.wait_send()
        for cp in mine:
            cp.wait()

    return pl.pallas_call(
        body, name=name,
        in_specs=[ANY] * n, out_specs=[ANY] * n,
        out_shape=[jax.ShapeDtypeStruct(g.shape, g.dtype) for g in gs],
        scratch_shapes=[pltpu.SemaphoreType.DMA((n_peers * n,)), pltpu.SemaphoreType.DMA((n_peers * n,)),
                        pltpu.SemaphoreType.DMA((n,))],
        compiler_params=pltpu.CompilerParams(has_side_effects=True),
    )(*gs)


SWAP_ROW_CHUNKS = 4


def _core_swap(gs, *, name):
    n = len(gs)

    def body(*refs):
        ins, outs = refs[:n], refs[n:2 * n]
        send_sems, recv_sems = refs[2 * n:]
        x, y, c = lax.axis_index("x"), lax.axis_index("y"), lax.axis_index("c")
        sibling = (x, y, 1 - c)
        started = []
        for a in range(n):
            _, Q, R, _ = ins[a].shape
            rows = R // SWAP_ROW_CHUNKS
            for q in range(Q):
                for j in range(SWAP_ROW_CHUNKS):
                    cp = pltpu.make_async_remote_copy(
                        src_ref=ins[a].at[1 - c, q, pl.ds(j * rows, rows)], dst_ref=outs[a].at[q, pl.ds(j * rows, rows)],
                        send_sem=send_sems.at[a], recv_sem=recv_sems.at[a], device_id=sibling, device_id_type=MESH)
                    cp.start()
                    started.append(cp)
        for a in range(n):
            pltpu.make_async_remote_copy(
                src_ref=ins[a].at[1 - c], dst_ref=outs[a], send_sem=send_sems.at[a], recv_sem=recv_sems.at[a],
                device_id=sibling, device_id_type=MESH).wait()

    return pl.pallas_call(
        body, name=name,
        in_specs=[ANY] * n, out_specs=[ANY] * n,
        out_shape=[jax.ShapeDtypeStruct(g.shape[1:], g.dtype) for g in gs],
        scratch_shapes=[pltpu.SemaphoreType.DMA((n,)), pltpu.SemaphoreType.DMA((n,))],
        compiler_params=pltpu.CompilerParams(has_side_effects=True),
    )(*gs)


def _pair_sum(g, other, core, *, name, tr_cap=256):
    _, Q, R, C = g.shape
    tr = max(t for t in range(16, min(R, tr_cap) + 1, 16) if R % t == 0)

    def body(core_ref, g_ref, o_ref, out_ref):
        out_ref[0] = (g_ref[0, 0] + o_ref[0]).astype(BF16)

    return pl.pallas_call(
        body, name=name,
        grid_spec=pltpu.PrefetchScalarGridSpec(
            num_scalar_prefetch=1, grid=(Q, R // tr),
            in_specs=[pl.BlockSpec((1, 1, tr, C), lambda q, i, core_ref: (core_ref[0], q, i, 0)),
                      pl.BlockSpec((1, tr, C), lambda q, i, core_ref: (q, i, 0))],
            out_specs=pl.BlockSpec((1, tr, C), lambda q, i, core_ref: (q, i, 0))),
        out_shape=jax.ShapeDtypeStruct((Q, R, C), BF16),
        compiler_params=_params(("parallel", "parallel")),
    )(core, g, other)


PACK_ROWS = 8


def _pack_small(norm1, norm2, final, att, hg, qn, kn, lb=None, loss=None):
    z = lambda n: jnp.zeros((n,), F32)
    rows = [norm1.reshape(-1), norm2.reshape(-1), final.reshape(-1),
            jnp.concatenate([att.reshape(-1), z(512)]),
            jnp.concatenate([hg.reshape(-1), qn.reshape(-1), kn.reshape(-1), z(1024 - 256)]),
            z(1024) if lb is None else lb.reshape(-1),
            z(1024) if loss is None else jnp.concatenate([loss.reshape(-1), z(1023)]), z(1024)]
    return jnp.stack(rows, axis=0)


def _unpack_small(p):
    return (p[0:1, :], p[1:2, :], p[2, :], p[3:4, 0:512], p[4:5, 0:128], p[4:5, 128:192], p[4:5, 192:256])


def _fold_heads(dhg, dqn, dkn, *, name):
    def body(hg_ref, q_ref, k_ref, ohg_ref, oq_ref, ok_ref):
        def fold128(v):
            acc = v[:, 0:LANES]
            for j in range(1, v.shape[1] // LANES):
                acc = acc + v[:, j * LANES:(j + 1) * LANES]
            return acc

        ohg_ref[...] = fold128(hg_ref[...])
        q = fold128(q_ref[...])
        oq_ref[...] = q + pltpu.roll(q, ATT_DH, 1)
        k = k_ref[...]
        ok_ref[...] = k + pltpu.roll(k, ATT_DH, 1)

    return pl.pallas_call(body, name=name, out_shape=[jax.ShapeDtypeStruct((1, LANES), F32)] * 3)(dhg, dqn, dkn)


def _lb_grad(dlb_sum, lb, *, name):
    def body(d_ref, lb_ref, o_ref):
        lbv = lb_ref[...]
        gl = d_ref[...] * lbv * (1.0 - lbv)
        o_ref[0:1, :] = gl[0:1, :]
        o_ref[1:2, :] = -gl[0:1, :]
        o_ref[2:3, :] = gl[1:2, :]
        o_ref[3:4, :] = -gl[1:2, :]

    return pl.pallas_call(body, name=name, out_shape=jax.ShapeDtypeStruct((4, HG_W), F32))(dlb_sum, lb)


def _lower_bounds(lb_logits_full, *, name):
    def body(l_ref, o_ref):
        for d in range(2):
            l0, l1 = l_ref[2 * d:2 * d + 1, :], l_ref[2 * d + 1:2 * d + 2, :]
            mx = jnp.maximum(l0, l1)
            e0, e1 = jnp.exp(l0 - mx), jnp.exp(l1 - mx)
            o_ref[d:d + 1, :] = e0 / (e0 + e1)

    return pl.pallas_call(body, name=name, out_shape=jax.ShapeDtypeStruct((2, HG_W), F32))(
        lb_logits_full.reshape(4, HG_W))


def _local_step(x, target, norm1_w, w_in, lb, hg_norm_w, q_norm_w, k_norm_w, att_norm_w, w_out, norm2_w,
                w_g, w_u, w_down, final_norm_w):
    T = x.shape[0]
    cos, sin = _rope_tables(T)
    qw8 = jnp.tile(q_norm_w, (1, ATT_HEADS))
    kw2 = jnp.tile(k_norm_w, (1, ATT_KV))

    h, r1 = _rms_fwd(x, norm1_w, name="norm1_fwd")
    U = _mm_nn([(h, w_in)], name="in_proj")
    o_f, st_f = _gla_fwd(U, lb[0:1], f_block=1, reverse=False, name="gla_fwd_f")
    o_b, st_b = _gla_fwd(U, lb[1:2], f_block=2, reverse=True, name="gla_fwd_b")
    mix_hg = _hg_post_fwd(o_f, o_b, U, hg_norm_w, name="hg_post_fwd")
    q_c, k, v = _att_prep_fwd2(U, cos, sin, qw8, kw2, name="att_prep_fwd")
    o_c, lse = _flash_fwd2(q_c, k, v, name="flash_fwd")
    o_att, mix_att = _att_post_fwd2(o_c, att_norm_w, name="att_post_fwd")
    x1 = _mm_nn([(mix_hg, w_out[:HG_W]), (mix_att, w_out[HG_W:])], residual=x, name="out_proj")
    h2, r2 = _rms_fwd(x1, norm2_w, name="norm2_fwd")
    gate, up, act = _ffn_up(h2, w_g, w_u, name="ffn_up")
    x2 = _mm_nn([(act, w_down)], residual=x1, name="ffn_down")
    loss, dx2, dx2b, d_final = _loss_head(x2, target, final_norm_w.reshape(1, D_MODEL), name="loss_head")

    d_gate, d_up = _ffn_act_bwd(dx2b, w_down, gate, up, name="ffn_act_bwd")
    dw_down = _mm_tn(act, dx2b, tma_cap=1408, name="dw_down")
    dh2 = _mm_nn([(d_gate, w_g), (d_up, w_u)], trans_b=True, tm=256, name="ffn_up_bwd")
    dw_g = _mm_tn(h2, d_gate, tnb_cap=1408, name="dw_gate")
    dw_u = _mm_tn(h2, d_up, tnb_cap=1408, name="dw_up")
    dx1, dx1b, d_norm2 = _rms_bwd(dh2, x1, r2, norm2_w, dx2, emit_bf16=True, name="norm2_bwd")
    dmix = _mm_nn([(dx1b, w_out)], trans_b=True, name="out_proj_bwd")
    dw_out = jnp.concatenate([_mm_tn(mix_hg, dx1b, name="dw_out_hg"), _mm_tn(mix_att, dx1b, name="dw_out_att")], axis=0)
    do_c, delta, d_att = _att_post_bwd2(dmix, o_att, att_norm_w, name="att_post_bwd")
    dq_c, dk, da_v = _flash_bwd2(q_c, k, v, do_c, lse, delta, name="flash_bwd")
    da_q, da_k, d_qn, d_kn = _att_prep_bwd2(U, dq_c, dk, cos, sin, qw8, kw2, name="att_prep_bwd")
    do_hg, du_g, d_hg = _hg_post_bwd(dmix, o_f, o_b, U, hg_norm_w, name="hg_post_bwd")
    dq_f, dz_f, dv_f, dlb_f = _gla_bwd(U, lb[0:1], do_hg, st_f, f_block=1, reverse=False, name="gla_bwd_f")
    dq_b, dz_b, dv_b, dlb_b = _gla_bwd(U, lb[1:2], do_hg, st_b, f_block=2, reverse=True, name="gla_bwd_b")
    dU = _assemble_du(U, dq_f, dq_b, dz_f, dz_b, dv_f, dv_b, du_g, da_q, da_k, da_v, name="assemble_du")
    dh = _mm_nn([(dU, w_in)], trans_b=True, name="in_proj_bwd")
    dw_in = _mm_tn(h, dU, tnb_cap=1664, name="dw_in")
    grad_x, d_norm1 = _rms_bwd(dh, x, r1, norm1_w, dx1, emit_bf16=False, name="norm1_bwd")
    d_hg, d_qn, d_kn = _fold_heads(d_hg, d_qn, d_kn, name="fold_heads")

    big = dict(w_in=dw_in, w_out=dw_out, w_g=dw_g, w_u=dw_u, w_down=dw_down)
    small = dict(norm1=d_norm1, norm2=d_norm2, final=d_final, att=d_att, hg=d_hg,
                 qn=d_qn[:, :ATT_DH], kn=d_kn[:, :ATT_DH], lb=jnp.concatenate([dlb_f, dlb_b], axis=0))
    return loss, grad_x, big, small


def kernel(x, norm1_w, w_in, lb_logits, hg_norm_w, q_norm_w, k_norm_w, att_norm_w, w_out, norm2_w, w_gate_up, w_down, final_norm_w, loss_target, m_norm1_w, m_w_in, m_lb_logits, m_hg_norm_w, m_q_norm_w, m_k_norm_w, m_att_norm_w, m_w_out, m_norm2_w, m_w_gate_up, m_w_down, m_final_norm_w, v_norm1_w, v_w_in, v_lb_logits, v_hg_norm_w, v_q_norm_w, v_k_norm_w, v_att_norm_w, v_w_out, v_norm2_w, v_w_gate_up, v_w_down, v_final_norm_w):
    T = x.shape[1]
    me = 4 * lax.axis_index("x") + 2 * lax.axis_index("y") + lax.axis_index("c")
    c_in, r_out, c_gu, r_dn = w_in.shape[2], w_out.shape[1], w_gate_up.shape[2], w_down.shape[1]
    lb_cols = lb_logits.shape[2]

    g_in, g_out, g_gu, g_dn, g_lb = _all_gather(
        [w_in[0].astype(BF16), w_out[0].astype(BF16), w_gate_up[0].astype(BF16), w_down[0].astype(BF16),
         lb_logits.reshape(4, lb_cols)], name="gather_weights")
    w_in_f = g_in.transpose(1, 0, 2).reshape(D_MODEL, N_DEV * c_in)
    w_out_f = g_out.reshape(N_DEV * r_out, D_MODEL)
    half = N_DEV // 2
    w_g_f = g_gu[:half].transpose(1, 0, 2).reshape(D_MODEL, half * c_gu)
    w_u_f = g_gu[half:].transpose(1, 0, 2).reshape(D_MODEL, half * c_gu)
    w_dn_f = g_dn.reshape(N_DEV * r_dn, D_MODEL)
    lb_logits_f = g_lb.transpose(1, 0, 2).reshape(2, 2, N_DEV * lb_cols)
    lb = _lower_bounds(lb_logits_f, name="lower_bounds")

    loss, grad_x, big, small = _local_step(
        x[0], loss_target[0], norm1_w, w_in_f, lb, hg_norm_w, q_norm_w, k_norm_w, att_norm_w, w_out_f, norm2_w,
        w_g_f, w_u_f, w_dn_f, final_norm_w)

    chips = N_DEV // 2
    by_owner_cols = lambda g, n_q, w: g.reshape(D_MODEL, n_q, 2, w).transpose(2, 1, 0, 3)
    by_owner_rows = lambda g, r: g.reshape(chips, 2, r, D_MODEL).transpose(1, 0, 2, 3)
    s_in = by_owner_cols(big["w_in"], chips, c_in)
    s_out = by_owner_rows(big["w_out"], r_out)
    s_gu = jnp.concatenate([by_owner_cols(big["w_g"], chips // 2, c_gu), by_owner_cols(big["w_u"], chips // 2, c_gu)],
                           axis=1)
    s_dn = by_owner_rows(big["w_down"], r_dn)
    mine = [s_in, s_out, s_gu, s_dn]
    theirs = _core_swap(mine, name="exchange_cores")
    core = lax.axis_index("c").astype(jnp.int32).reshape(1)
    chip_sums = [_pair_sum(g, o, core, name="pair_sum_" + nm)
                 for g, o, nm in zip(mine, theirs, ("w_in", "w_out", "w_gu", "w_down"))]
    p_in, p_out, p_gu, p_dn = _exchange(chip_sums, masks=[(1, 0, 0), (0, 1, 0), (1, 1, 0)],
                                        slot=lambda p: 2 * p[0] + p[1], name="exchange_chips")

    packed = _pack_small(small["norm1"], small["norm2"], small["final"], small["att"], small["hg"],
                         small["qn"], small["kn"], small["lb"], loss)
    (all_small,) = _all_gather([packed], name="gather_small_grads")

    g_w_in, d_w_in, nm_w_in, nv_w_in = _adamw(p_in, w_in[0], m_w_in[0], v_w_in[0], name="adamw_w_in")
    g_w_out, d_w_out, nm_w_out, nv_w_out = _adamw(p_out, w_out[0], m_w_out[0], v_w_out[0], name="adamw_w_out")
    g_w_gu, d_w_gu, nm_w_gu, nv_w_gu = _adamw(p_gu, w_gate_up[0], m_w_gate_up[0], v_w_gate_up[0], name="adamw_w_gu")
    g_w_dn, d_w_dn, nm_w_dn, nv_w_dn = _adamw(p_dn, w_down[0], m_w_down[0], v_w_down[0], name="adamw_w_down")

    pk = lambda vecs: _pack_small(*vecs)
    w_pk = pk([norm1_w, norm2_w, final_norm_w, att_norm_w, hg_norm_w, q_norm_w, k_norm_w])
    m_pk = pk([m_norm1_w, m_norm2_w, m_final_norm_w, m_att_norm_w, m_hg_norm_w, m_q_norm_w, m_k_norm_w])
    v_pk = pk([v_norm1_w, v_norm2_w, v_final_norm_w, v_att_norm_w, v_hg_norm_w, v_q_norm_w, v_k_norm_w])
    g_pk, d_pk, nm_pk, nv_pk = _adamw(all_small, w_pk, m_pk, v_pk, name="adamw_small")

    dlb_sum = g_pk[5:6, :].reshape(2, HG_W)
    g_lb_full = _lb_grad(dlb_sum, lb, name="lb_grad")
    g_lb_mine = lax.dynamic_slice_in_dim(g_lb_full, me * lb_cols, lb_cols, axis=1)
    g_lb_s, d_lb, nm_lb, nv_lb = _adamw(g_lb_mine[None], lb_logits.reshape(4, lb_cols),
                                        m_lb_logits.reshape(4, lb_cols), v_lb_logits.reshape(4, lb_cols),
                                        name="adamw_lb")

    loss_total = g_pk[6, 0]

    def outs(big4, lb_arr, pk_arr):
        n1, n2, fin, att, hg, qn, kn = _unpack_small(pk_arr)
        b_in, b_out, b_gu, b_dn = big4
        return [n1, b_in[None], lb_arr.reshape(2, 2, lb_cols), hg, qn, kn, att, b_out[None], n2, b_gu[None],
                b_dn[None], fin]

    return (loss_total, grad_x[None],
            *outs((g_w_in, g_w_out, g_w_gu, g_w_dn), g_lb_s, g_pk),
            *outs((d_w_in, d_w_out, d_w_gu, d_w_dn), d_lb, d_pk),
            *outs((nm_w_in, nm_w_out, nm_w_gu, nm_w_dn), nm_lb, nm_pk),
            *outs((nv_w_in, nv_w_out, nv_w_gu, nv_w_dn), nv_lb, nv_pk))
```

```python
import functools
import math

import jax
import jax.numpy as jnp
import numpy as np
from jax import lax
from jax.experimental import pallas as pl
from jax.experimental.pallas import tpu as pltpu

F32 = jnp.float32
BF16 = jnp.bfloat16

N_DEV = 8
D_MODEL = 1024
EPS = 1e-6
HG_HEADS = 4
HG_D = 128
HG_W = HG_HEADS * HG_D
CHUNK = 64
ATT_HEADS = 8
ATT_KV = 2
ATT_G = ATT_HEADS // ATT_KV
ATT_DH = 64
ATT_QW = ATT_HEADS * ATT_DH
ATT_KW = ATT_KV * ATT_DH
GRID_W = 64
ROPE_THETA = 10000.0
D_IN = 5 * HG_W + ATT_QW + 2 * ATT_KW
D_FF = 2816
ADAM_LR, ADAM_B1, ADAM_B2, ADAM_EPS, ADAM_WD, ADAM_STEP = 0.001, 0.9, 0.999, 1e-08, 0.01, 10

LANES = 128
VMEM_LIMIT = 48 * 1024 * 1024
MESH = pl.DeviceIdType.MESH
ANY = pl.BlockSpec(memory_space=pl.ANY)


def _params(sem=None):
    return pltpu.CompilerParams(dimension_semantics=sem, vmem_limit_bytes=VMEM_LIMIT)


def _pick(n, cap):
    best = None
    for t in range(LANES, cap + 1, LANES):
        if n % t == 0:
            best = t
    assert best is not None, (n, cap)
    return best


def _sigmoid(x):
    return 1.0 / (1.0 + jnp.exp(-x))


def _dot(a, b):
    return jnp.dot(a.astype(BF16), b.astype(BF16), preferred_element_type=F32)


def _dot_nt(a, b):
    return lax.dot_general(a.astype(BF16), b.astype(BF16), (((1,), (1,)), ((), ())),
                           preferred_element_type=F32)


def _dot_tn(a, b):
    return lax.dot_general(a.astype(BF16), b.astype(BF16), (((0,), (0,)), ((), ())),
                           preferred_element_type=F32)


def _mm_nn(pairs, *, name, out_dtype=F32, residual=None, tm=512, tn_cap=None, trans_b=False):
    M = pairs[0][0].shape[0]
    N = pairs[0][1].shape[0 if trans_b else 1]
    tn = N if tn_cap is None else _pick(N, tn_cap)
    n_pairs = len(pairs)
    has_res = residual is not None
    dims = (((1,), (1,)), ((), ())) if trans_b else (((1,), (0,)), ((), ()))

    def body(*refs):
        acc = None
        for i in range(n_pairs):
            d = lax.dot_general(refs[2 * i][...], refs[2 * i + 1][...], dims, preferred_element_type=F32)
            acc = d if acc is None else acc + d
        if has_res:
            acc = acc + refs[2 * n_pairs][...]
        refs[-1][...] = acc.astype(out_dtype)

    in_specs, args = [], []
    for a, b in pairs:
        k = a.shape[1]
        b_spec = pl.BlockSpec((tn, k), lambda i, j: (j, 0)) if trans_b else pl.BlockSpec((k, tn), lambda i, j: (0, j))
        in_specs += [pl.BlockSpec((tm, k), lambda i, j: (i, 0)), b_spec]
        args += [a, b]
    if has_res:
        in_specs.append(pl.BlockSpec((tm, tn), lambda i, j: (i, j)))
        args.append(residual)
    return pl.pallas_call(
        body, name=name, grid=(M // tm, N // tn), in_specs=in_specs,
        out_specs=pl.BlockSpec((tm, tn), lambda i, j: (i, j)),
        out_shape=jax.ShapeDtypeStruct((M, N), out_dtype),
        compiler_params=_params(("parallel", "arbitrary")),
    )(*args)


def _mm_tn(a, b, *, name, tma_cap=1024, tnb_cap=1024, tk=512):
    T, Ma = a.shape
    Nb = b.shape[1]
    tma, tnb = _pick(Ma, tma_cap), _pick(Nb, tnb_cap)
    n_k = T // tk

    def body(a_ref, b_ref, o_ref, acc_ref):
        k = pl.program_id(2)

        @pl.when(k == 0)
        def _():
            acc_ref[...] = jnp.zeros_like(acc_ref)

        acc_ref[...] += lax.dot_general(a_ref[...], b_ref[...], (((0,), (0,)), ((), ())),
                                        preferred_element_type=F32)

        @pl.when(k == n_k - 1)
        def _():
            o_ref[...] = acc_ref[...]

    return pl.pallas_call(
        body, name=name, grid=(Ma // tma, Nb // tnb, n_k),
        in_specs=[pl.BlockSpec((tk, tma), lambda i, j, k: (k, i)), pl.BlockSpec((tk, tnb), lambda i, j, k: (k, j))],
        out_specs=pl.BlockSpec((tma, tnb), lambda i, j, k: (i, j)),
        out_shape=jax.ShapeDtypeStruct((Ma, Nb), F32),
        scratch_shapes=[pltpu.VMEM((tma, tnb), F32)],
        compiler_params=_params(("parallel", "parallel", "arbitrary")),
    )(a, b)


def _rms_fwd(x, w, *, name, tm=512):
    T, Dm = x.shape

    def body(x_ref, w_ref, h_ref, r_ref):
        xv = x_ref[...]
        r = lax.rsqrt(jnp.mean(xv * xv, axis=-1, keepdims=True) + EPS)
        h_ref[...] = (xv * r * w_ref[...]).astype(BF16)
        r_ref[...] = r

    return pl.pallas_call(
        body, name=name, grid=(T // tm,),
        in_specs=[pl.BlockSpec((tm, Dm), lambda i: (i, 0)), pl.BlockSpec((1, Dm), lambda i: (0, 0))],
        out_specs=[pl.BlockSpec((tm, Dm), lambda i: (i, 0)), pl.BlockSpec((tm, 1), lambda i: (i, 0))],
        out_shape=[jax.ShapeDtypeStruct((T, Dm), BF16), jax.ShapeDtypeStruct((T, 1), F32)],
        compiler_params=_params(("parallel",)),
    )(x, w)


def _rms_bwd(dh, x, r, w, dres, *, name, emit_bf16, tm=512):
    T, Dm = x.shape

    def body(dh_ref, x_ref, r_ref, w_ref, dres_ref, *outs):
        dx_ref, dw_ref = outs[0], outs[-1]

        @pl.when(pl.program_id(0) == 0)
        def _():
            dw_ref[...] = jnp.zeros_like(dw_ref)

        rv = r_ref[...]
        xh = x_ref[...] * rv
        dhv = dh_ref[...]
        dxh = dhv * w_ref[...]
        t = jnp.mean(dxh * xh, axis=-1, keepdims=True)
        dx = dres_ref[...] + rv * (dxh - xh * t)
        dx_ref[...] = dx
        if emit_bf16:
            outs[1][...] = dx.astype(BF16)
        dw_ref[...] += jnp.sum(dhv * xh, axis=0, keepdims=True)

    row = pl.BlockSpec((tm, Dm), lambda i: (i, 0))
    vec = pl.BlockSpec((1, Dm), lambda i: (0, 0))
    out_specs = [row] + ([row] if emit_bf16 else []) + [vec]
    out_shape = ([jax.ShapeDtypeStruct((T, Dm), F32)] + ([jax.ShapeDtypeStruct((T, Dm), BF16)] if emit_bf16 else [])
                 + [jax.ShapeDtypeStruct((1, Dm), F32)])
    return pl.pallas_call(
        body, name=name, grid=(T // tm,),
        in_specs=[row, row, pl.BlockSpec((tm, 1), lambda i: (i, 0)), vec, row],
        out_specs=out_specs, out_shape=out_shape,
        compiler_params=_params(("arbitrary",)),
    )(dh, x, r, w, dres)


def _loss_head(x2, target, w, *, name, tm=512):
    T, Dm = x2.shape

    def body(x_ref, t_ref, w_ref, loss_ref, dx_ref, dxb_ref, dw_ref):
        @pl.when(pl.program_id(0) == 0)
        def _():
            loss_ref[...] = jnp.zeros_like(loss_ref)
            dw_ref[...] = jnp.zeros_like(dw_ref)

        xv = x_ref[...]
        r = lax.rsqrt(jnp.mean(xv * xv, axis=-1, keepdims=True) + EPS)
        xh = xv * r
        wv = w_ref[...]
        err = xh * wv - t_ref[...]
        row_loss = jnp.mean(err * err, axis=-1, keepdims=True)
        loss_ref[...] += 0.5 * jnp.sum(row_loss, axis=0, keepdims=True)
        dy = err * (1.0 / Dm)
        dxh = dy * wv
        t = jnp.mean(dxh * xh, axis=-1, keepdims=True)
        dx = r * (dxh - xh * t)
        dx_ref[...] = dx
        dxb_ref[...] = dx.astype(BF16)
        dw_ref[...] += jnp.sum(dy * xh, axis=0, keepdims=True)

    row = pl.BlockSpec((tm, Dm), lambda i: (i, 0))
    vec = pl.BlockSpec((1, Dm), lambda i: (0, 0))
    return pl.pallas_call(
        body, name=name, grid=(T // tm,),
        in_specs=[row, row, vec],
        out_specs=[pl.BlockSpec((1, 1), lambda i: (0, 0)), row, row, vec],
        out_shape=[jax.ShapeDtypeStruct((1, 1), F32), jax.ShapeDtypeStruct((T, Dm), F32),
                   jax.ShapeDtypeStruct((T, Dm), BF16), jax.ShapeDtypeStruct((1, Dm), F32)],
        compiler_params=_params(("arbitrary",)),
    )(x2, target, w)


GLA_TB = 512
GLA_NC = GLA_TB // CHUNK


def _cumsum_rows(x, row, reverse):
    n = x.shape[0]
    s = 1
    while s < n:
        if not reverse:
            x = x + jnp.where(row >= s, pltpu.roll(x, s, 0), 0.0)
        else:
            x = x + jnp.where(row < n - s, pltpu.roll(x, n - s, 0), 0.0)
        s *= 2
    return x


def _gla_gates(uq, z, lbv):
    q = uq * _sigmoid(uq)
    sg = _sigmoid(z)
    sgn = _sigmoid(-z)
    f = lbv + (1.0 - lbv) * sg
    k = (1.0 - lbv) * sgn
    return q, sg, sgn, f, k


def _gla_decays(f, row, reverse):
    b = _cumsum_rows(jnp.log(f), row, reverse)
    if not reverse:
        bref, blast = b[CHUNK // 2 - 1:CHUNK // 2, :], b[CHUNK - 1:CHUNK, :]
    else:
        bref, blast = b[CHUNK // 2:CHUNK // 2 + 1, :], b[0:1, :]
    return b, bref, blast


def _gla_fwd(U, lb, *, f_block, reverse, name):
    T = U.shape[0]
    nb = T // GLA_TB

    def body(uq_ref, uf_ref, ui_ref, lb_ref, o_ref, st_ref, s_ref):
        @pl.when(pl.program_id(0) == 0)
        def _():
            s_ref[...] = jnp.zeros_like(s_ref)

        row = lax.broadcasted_iota(jnp.int32, (CHUNK, HG_D), 0)
        ri = lax.broadcasted_iota(jnp.int32, (CHUNK, CHUNK), 0)
        ci = lax.broadcasted_iota(jnp.int32, (CHUNK, CHUNK), 1)
        mask = (ri <= ci) if reverse else (ri >= ci)

        def chunk(j, carry):
            c = (GLA_NC - 1 - j) if reverse else j
            rows = pl.ds(pl.multiple_of(c * CHUNK, CHUNK), CHUNK)
            for h in range(HG_HEADS):
                cols = pl.ds(h * HG_D, HG_D)
                v = ui_ref[rows, cols]
                q, _, _, f, k = _gla_gates(uq_ref[rows, cols], uf_ref[rows, cols], lb_ref[:, cols])
                b, bref, blast = _gla_decays(f, row, reverse)
                s = jnp.where(mask, _dot_nt(q * jnp.exp(b - bref), k * jnp.exp(bref - b)), 0.0)
                st = s_ref[h]
                st_ref[c, h] = st
                o_ref[rows, cols] = _dot(s, v) + _dot_nt(q * jnp.exp(b), st)
                s_ref[h] = st * jnp.exp(blast) + _dot_tn(v, k * jnp.exp(blast - b))
            return carry

        lax.fori_loop(0, GLA_NC, chunk, 0)

    blk = (lambda i: nb - 1 - i) if reverse else (lambda i: i)
    ucol = lambda cb: pl.BlockSpec((GLA_TB, HG_W), lambda i: (blk(i), cb))
    return pl.pallas_call(
        body, name=name, grid=(nb,),
        in_specs=[ucol(0), ucol(f_block), ucol(3), pl.BlockSpec((1, HG_W), lambda i: (0, 0))],
        out_specs=[pl.BlockSpec((GLA_TB, HG_W), lambda i: (blk(i), 0)),
                   pl.BlockSpec((GLA_NC, HG_HEADS, HG_D, HG_D), lambda i: (blk(i), 0, 0, 0))],
        out_shape=[jax.ShapeDtypeStruct((T, HG_W), F32),
                   jax.ShapeDtypeStruct((T // CHUNK, HG_HEADS, HG_D, HG_D), F32)],
        scratch_shapes=[pltpu.VMEM((HG_HEADS, HG_D, HG_D), F32)],
        compiler_params=_params(("arbitrary",)),
    )(U, U, U, lb)


def _gla_bwd(U, lb, do, states, *, f_block, reverse, name):
    T = U.shape[0]
    nb = T // GLA_TB

    def body(uq_ref, uf_ref, ui_ref, lb_ref, do_ref, st_ref, dq_ref, dz_ref, dv_ref, dlb_ref, ds_ref):
        @pl.when(pl.program_id(0) == 0)
        def _():
            ds_ref[...] = jnp.zeros_like(ds_ref)
            dlb_ref[...] = jnp.zeros_like(dlb_ref)

        row = lax.broadcasted_iota(jnp.int32, (CHUNK, HG_D), 0)
        ri = lax.broadcasted_iota(jnp.int32, (CHUNK, CHUNK), 0)
        ci = lax.broadcasted_iota(jnp.int32, (CHUNK, CHUNK), 1)
        mask = (ri <= ci) if reverse else (ri >= ci)

        def chunk(j, carry):
            c = j if reverse else (GLA_NC - 1 - j)
            rows = pl.ds(pl.multiple_of(c * CHUNK, CHUNK), CHUNK)
            for h in range(HG_HEADS):
                cols = pl.ds(h * HG_D, HG_D)
                v = ui_ref[rows, cols]
                lbv = lb_ref[:, cols]
                q, sg, sgn, f, k = _gla_gates(uq_ref[rows, cols], uf_ref[rows, cols], lbv)
                b, bref, blast = _gla_decays(f, row, reverse)
                eq, ek, eb, el, dec = (jnp.exp(b - bref), jnp.exp(bref - b), jnp.exp(b), jnp.exp(blast - b),
                                       jnp.exp(blast))
                qin, kin, qb, klast = q * eq, k * ek, q * eb, k * el
                dov = do_ref[rows, cols]
                st = st_ref[c, h]
                dst = ds_ref[h]
                p = jnp.where(mask, _dot_nt(qin, kin), 0.0)
                dp = jnp.where(mask, _dot_nt(dov, v), 0.0)
                dqin = _dot(dp, kin)
                dkin = _dot_tn(dp, qin)
                dv_ref[rows, cols] = _dot_tn(p, dov) + _dot_nt(klast, dst)
                dqb = _dot(dov, st)
                dklast = _dot(v, dst)
                ds_ref[h] = _dot_tn(dov, qb) + dst * dec
                db = dqin * qin - dkin * kin + dqb * qb - dklast * klast
                extra = (jnp.sum(dklast * klast, axis=0, keepdims=True)
                         + dec * jnp.sum(st * dst, axis=0, keepdims=True))
                dg = _cumsum_rows(db, row, not reverse) + extra
                dq_ref[rows, cols] = dqin * eq + dqb * eb
                dk = dkin * ek + dklast * el
                dfk = dg / f - dk
                dz_ref[rows, cols] = dfk * (1.0 - lbv) * sg * sgn
                dlb_ref[:, cols] += jnp.sum(dfk * sgn, axis=0, keepdims=True)
            return carry

        lax.fori_loop(0, GLA_NC, chunk, 0)

    blk = (lambda i: i) if reverse else (lambda i: nb - 1 - i)
    ucol = lambda cb: pl.BlockSpec((GLA_TB, HG_W), lambda i: (blk(i), cb))
    tok = pl.BlockSpec((GLA_TB, HG_W), lambda i: (blk(i), 0))
    vec = pl.BlockSpec((1, HG_W), lambda i: (0, 0))
    return pl.pallas_call(
        body, name=name, grid=(nb,),
        in_specs=[ucol(0), ucol(f_block), ucol(3), vec, tok,
                  pl.BlockSpec((GLA_NC, HG_HEADS, HG_D, HG_D), lambda i: (blk(i), 0, 0, 0))],
        out_specs=[tok, tok, tok, vec],
        out_shape=[jax.ShapeDtypeStruct((T, HG_W), F32)] * 3 + [jax.ShapeDtypeStruct((1, HG_W), F32)],
        scratch_shapes=[pltpu.VMEM((HG_HEADS, HG_D, HG_D), F32)],
        compiler_params=_params(("arbitrary",)),
    )(U, U, U, lb, do, states)


def _hg_post_fwd(o_f, o_b, U, w, *, name, tm=512):
    T = o_f.shape[0]

    def body(of_ref, ob_ref, ug_ref, w_ref, out_ref):
        wv = w_ref[...]
        for h in range(HG_HEADS):
            cols = pl.ds(h * HG_D, HG_D)
            o = of_ref[:, cols] + ob_ref[:, cols]
            r = lax.rsqrt(jnp.mean(o * o, axis=-1, keepdims=True) + EPS)
            ug = ug_ref[:, cols]
            out_ref[:, cols] = (o * r * wv * (ug * _sigmoid(ug))).astype(BF16)

    tok = pl.BlockSpec((tm, HG_W), lambda i: (i, 0))
    return pl.pallas_call(
        body, name=name, grid=(T // tm,),
        in_specs=[tok, tok, pl.BlockSpec((tm, HG_W), lambda i: (i, 4)), pl.BlockSpec((1, HG_D), lambda i: (0, 0))],
        out_specs=tok, out_shape=jax.ShapeDtypeStruct((T, HG_W), BF16),
        compiler_params=_params(("parallel",)),
    )(o_f, o_b, U, w)


def _hg_post_bwd(dmix, o_f, o_b, U, w, *, name, tm=512):
    T = o_f.shape[0]

    def body(dm_ref, of_ref, ob_ref, ug_ref, w_ref, do_ref, dug_ref, dw_ref):
        @pl.when(pl.program_id(0) == 0)
        def _():
            dw_ref[...] = jnp.zeros_like(dw_ref)

        wv = w_ref[...]
        for h in range(HG_HEADS):
            cols = pl.ds(h * HG_D, HG_D)
            o = of_ref[:, cols] + ob_ref[:, cols]
            r = lax.rsqrt(jnp.mean(o * o, axis=-1, keepdims=True) + EPS)
            xh = o * r
            ug = ug_ref[:, cols]
            sg = _sigmoid(ug)
            dm = dm_ref[:, cols]
            dn = dm * (ug * sg)
            dug_ref[:, cols] = dm * (xh * wv) * (sg * (1.0 + ug * (1.0 - sg)))
            dxh = dn * wv
            t = jnp.mean(dxh * xh, axis=-1, keepdims=True)
            do_ref[:, cols] = r * (dxh - xh * t)
            dw_ref[:, cols] += jnp.sum(dn * xh, axis=0, keepdims=True)

    tok = pl.BlockSpec((tm, HG_W), lambda i: (i, 0))
    vec = pl.BlockSpec((1, HG_W), lambda i: (0, 0))
    return pl.pallas_call(
        body, name=name, grid=(T // tm,),
        in_specs=[tok, tok, tok, pl.BlockSpec((tm, HG_W), lambda i: (i, 4)), pl.BlockSpec((1, HG_D), lambda i: (0, 0))],
        out_specs=[tok, tok, vec],
        out_shape=[jax.ShapeDtypeStruct((T, HG_W), F32)] * 2 + [jax.ShapeDtypeStruct((1, HG_W), F32)],
        compiler_params=_params(("arbitrary",)),
    )(dmix, o_f, o_b, U, w)


def _rope_tables(T):
    rows = T // GRID_W
    row = jnp.repeat(jnp.arange(rows), GRID_W).astype(F32)
    col = jnp.tile(jnp.arange(GRID_W), rows).astype(F32)
    axis_dim = ATT_DH // 2
    freqs = ROPE_THETA ** (-jnp.arange(0, axis_dim, 2, dtype=F32) / axis_dim)
    ang = jnp.concatenate([row[:, None] * freqs, col[:, None] * freqs], axis=-1)
    cos, sin = jnp.cos(ang), jnp.sin(ang)
    c = jnp.repeat(cos, 2, axis=-1)
    s = jnp.stack([-sin, sin], axis=-1).reshape(T, ATT_DH)
    return jnp.tile(c, (1, 2)), jnp.tile(s, (1, 2))


def _head_blockdiag(width):
    shift = ATT_DH.bit_length() - 1
    ri = jnp.right_shift(lax.broadcasted_iota(jnp.int32, (width, width), 0), shift)
    ci = jnp.right_shift(lax.broadcasted_iota(jnp.int32, (width, width), 1), shift)
    return jnp.where(ri == ci, 1.0, 0.0).astype(BF16)


def _head_sum(x, bd):
    hi = x.astype(BF16)
    lo = (x - hi.astype(F32)).astype(BF16)
    return jnp.dot(hi, bd, preferred_element_type=F32) + jnp.dot(lo, bd, preferred_element_type=F32)


def _pair_swap(x, even):
    n = x.shape[-1]
    return jnp.where(even, pltpu.roll(x, n - 1, 1), pltpu.roll(x, 1, 1))


def _att_prep_fwd(U, cos, sin, qw, kw, *, name, tm=512):
    T = U.shape[0]
    scale = ATT_DH ** -0.5

    def body(aq_ref, ak_ref, av_ref, c_ref, s_ref, qw_ref, kw_ref, q_ref, k_ref, v_ref):
        bd = _head_blockdiag(ATT_QW)
        c2, s2 = c_ref[...], s_ref[...]
        c8, s8 = jnp.tile(c2, (1, 4)), jnp.tile(s2, (1, 4))

        def norm_rope(x, w, c, s, bdm):
            r = lax.rsqrt(_head_sum(x * x, bdm) * (1.0 / ATT_DH) + EPS)
            y = x * r * w
            even = (lax.broadcasted_iota(jnp.int32, y.shape, 1) & 1) == 0
            return y * c + _pair_swap(y, even) * s

        q_ref[...] = (norm_rope(aq_ref[...], qw_ref[...], c8, s8, bd) * scale).astype(BF16)
        k_ref[...] = norm_rope(ak_ref[...], kw_ref[...], c2, s2, bd[:ATT_KW, :ATT_KW]).astype(BF16)
        v_ref[...] = av_ref[...].astype(BF16)

    kv_spec = pl.BlockSpec((tm, ATT_KW), lambda i: (i, 0))
    return pl.pallas_call(
        body, name=name, grid=(T // tm,),
        in_specs=[pl.BlockSpec((tm, ATT_QW), lambda i: (i, 5)),
                  pl.BlockSpec((tm, ATT_KW), lambda i: (i, 24)), pl.BlockSpec((tm, ATT_KW), lambda i: (i, 25)),
                  kv_spec, kv_spec,
                  pl.BlockSpec((1, ATT_QW), lambda i: (0, 0)), pl.BlockSpec((1, ATT_KW), lambda i: (0, 0))],
        out_specs=[pl.BlockSpec((tm, ATT_QW), lambda i: (i, 0)), kv_spec, kv_spec],
        out_shape=[jax.ShapeDtypeStruct((T, ATT_QW), BF16), jax.ShapeDtypeStruct((T, ATT_KW), BF16),
                   jax.ShapeDtypeStruct((T, ATT_KW), BF16)],
        compiler_params=_params(("parallel",)),
    )(U, U, U, cos, sin, qw, kw)


def _att_prep_bwd(U, dq, dk, cos, sin, qw, kw, *, name, tm=512):
    T = U.shape[0]
    scale = ATT_DH ** -0.5

    def body(aq_ref, ak_ref, dq_ref, dk_ref, c_ref, s_ref, qw_ref, kw_ref, daq_ref, dak_ref, dqw_ref, dkw_ref):
        @pl.when(pl.program_id(0) == 0)
        def _():
            dqw_ref[...] = jnp.zeros_like(dqw_ref)
            dkw_ref[...] = jnp.zeros_like(dkw_ref)

        bd = _head_blockdiag(ATT_QW)
        c2, s2 = c_ref[...], s_ref[...]
        c8, s8 = jnp.tile(c2, (1, 4)), jnp.tile(s2, (1, 4))

        def bwd(x, dy, w, c, s, bdm):
            even = (lax.broadcasted_iota(jnp.int32, x.shape, 1) & 1) == 0
            dn = dy * c - _pair_swap(dy, even) * s
            r = lax.rsqrt(_head_sum(x * x, bdm) * (1.0 / ATT_DH) + EPS)
            xh = x * r
            dxh = dn * w
            t = _head_sum(dxh * xh, bdm) * (1.0 / ATT_DH)
            return r * (dxh - xh * t), jnp.sum(dn * xh, axis=0, keepdims=True)

        da, dw = bwd(aq_ref[...], dq_ref[...] * scale, qw_ref[...], c8, s8, bd)
        daq_ref[...] = da
        dqw_ref[...] += dw
        da, dw = bwd(ak_ref[...], dk_ref[...], kw_ref[...], c2, s2, bd[:ATT_KW, :ATT_KW])
        dak_ref[...] = da
        dkw_ref[...] += dw

    q_spec = pl.BlockSpec((tm, ATT_QW), lambda i: (i, 0))
    kv_spec = pl.BlockSpec((tm, ATT_KW), lambda i: (i, 0))
    qv = pl.BlockSpec((1, ATT_QW), lambda i: (0, 0))
    kv = pl.BlockSpec((1, ATT_KW), lambda i: (0, 0))
    return pl.pallas_call(
        body, name=name, grid=(T // tm,),
        in_specs=[pl.BlockSpec((tm, ATT_QW), lambda i: (i, 5)), pl.BlockSpec((tm, ATT_KW), lambda i: (i, 24)),
                  q_spec, kv_spec, kv_spec, kv_spec, qv, kv],
        out_specs=[q_spec, kv_spec, qv, kv],
        out_shape=[jax.ShapeDtypeStruct((T, ATT_QW), F32), jax.ShapeDtypeStruct((T, ATT_KW), F32),
                   jax.ShapeDtypeStruct((1, ATT_QW), F32), jax.ShapeDtypeStruct((1, ATT_KW), F32)],
        compiler_params=_params(("arbitrary",)),
    )(U, U, dq, dk, cos, sin, qw, kw)


FA_TQ = 256
FA_TK = 256
FA_SW = 128


def _fa_tiles(T):
    tq, tk = min(FA_TQ, T), min(FA_TK, T)
    return tq, tk, T // tq, T // tk


def _to_fa_cols(a, T):
    tq, _, nq, _ = _fa_tiles(T)
    return a.reshape(nq, tq, ATT_KV, ATT_G, ATT_DH).transpose(2, 0, 4, 3, 1).reshape(ATT_KV, nq, ATT_DH, ATT_G * tq)


def _to_fa_rows(a, T):
    tq, _, nq, _ = _fa_tiles(T)
    return a.reshape(nq, tq, ATT_KV, ATT_G, ATT_DH).transpose(2, 0, 3, 1, 4).reshape(ATT_KV, nq, ATT_G * tq, ATT_DH)


def _from_fa_cols(a, T):
    tq, _, nq, _ = _fa_tiles(T)
    return a.reshape(ATT_KV, nq, ATT_DH, ATT_G, tq).transpose(1, 4, 0, 3, 2).reshape(T, ATT_QW)


def _kv_rows(a, T):
    _, tk, _, n_k = _fa_tiles(T)
    return a.reshape(n_k, tk, ATT_KV, ATT_DH).transpose(2, 0, 1, 3)


def _kv_cols(a, T):
    _, tk, _, n_k = _fa_tiles(T)
    return a.reshape(n_k, tk, ATT_KV, ATT_DH).transpose(2, 0, 3, 1)


def _flash_fwd(q_c, k_r, v_c, *, name):
    _, nq, _, R = q_c.shape
    _, n_k, tk, _ = k_r.shape

    def body(q_ref, k_ref, v_ref, o_ref, lse_ref):
        for st in range(R // FA_SW):
            lanes = pl.ds(st * FA_SW, FA_SW)
            qv = q_ref[0, 0, :, lanes]

            def step(j, carry):
                m, l, acc = carry
                s = jnp.dot(k_ref[0, j], qv, preferred_element_type=F32)
                m_new = jnp.maximum(m, jnp.max(s, axis=0, keepdims=True))
                alpha = jnp.exp(m - m_new)
                p = jnp.exp(s - m_new)
                l = alpha * l + jnp.sum(p, axis=0, keepdims=True)
                acc = alpha * acc + jnp.dot(v_ref[0, j], p.astype(BF16), preferred_element_type=F32)
                return m_new, l, acc

            m, l, acc = lax.fori_loop(0, n_k, step, (jnp.full((1, FA_SW), -jnp.inf, F32), jnp.zeros((1, FA_SW), F32),
                                                     jnp.zeros((ATT_DH, FA_SW), F32)))
            o_ref[0, 0, :, lanes] = acc / l
            lse_ref[0, 0, :, lanes] = m + jnp.log(l)

    qspec = pl.BlockSpec((1, 1, ATT_DH, R), lambda h, i: (h, i, 0, 0))
    return pl.pallas_call(
        body, name=name, grid=(ATT_KV, nq),
        in_specs=[qspec, pl.BlockSpec((1, n_k, tk, ATT_DH), lambda h, i: (h, 0, 0, 0)),
                  pl.BlockSpec((1, n_k, ATT_DH, tk), lambda h, i: (h, 0, 0, 0))],
        out_specs=[qspec, pl.BlockSpec((1, 1, 1, R), lambda h, i: (h, i, 0, 0))],
        out_shape=[jax.ShapeDtypeStruct((ATT_KV, nq, ATT_DH, R), F32), jax.ShapeDtypeStruct((ATT_KV, nq, 1, R), F32)],
        compiler_params=_params(("parallel", "parallel")),
    )(q_c, k_r, v_c)


def _flash_bwd(q_c, q_r, k_r, k_c, v_r, do_c, do_r, o_c, lse, *, name):
    _, nq, _, R = q_c.shape
    _, n_k, tk, _ = k_r.shape

    def body(qc_ref, qr_ref, kr_ref, kc_ref, vr_ref, doc_ref, dor_ref, oc_ref, lse_ref, dq_ref, dk_ref, dv_ref,
             acc_ref):
        @pl.when(pl.program_id(1) == 0)
        def _():
            dk_ref[...] = jnp.zeros_like(dk_ref)
            dv_ref[...] = jnp.zeros_like(dv_ref)

        qc, doc = qc_ref[0, 0], doc_ref[0, 0]
        qr, dor = qr_ref[0, 0], dor_ref[0, 0]
        delta = jnp.sum(doc.astype(F32) * oc_ref[0, 0], axis=0, keepdims=True)
        lsev = lse_ref[0, 0]
        acc_ref[...] = jnp.zeros_like(acc_ref)

        def step(j, carry):
            s = jnp.dot(kr_ref[0, j], qc, preferred_element_type=F32)
            p = jnp.exp(s - lsev)
            dp = jnp.dot(vr_ref[0, j], doc, preferred_element_type=F32)
            ds = (p * (dp - delta)).astype(BF16)
            acc_ref[...] += jnp.dot(kc_ref[0, j], ds, preferred_element_type=F32)
            dk_ref[0, j] += jnp.dot(ds, qr, preferred_element_type=F32)
            dv_ref[0, j] += jnp.dot(p.astype(BF16), dor, preferred_element_type=F32)
            return carry

        lax.fori_loop(0, n_k, step, 0)
        dq_ref[0, 0] = acc_ref[...]

    cspec = pl.BlockSpec((1, 1, ATT_DH, R), lambda h, i: (h, i, 0, 0))
    rspec = pl.BlockSpec((1, 1, R, ATT_DH), lambda h, i: (h, i, 0, 0))
    krspec = pl.BlockSpec((1, n_k, tk, ATT_DH), lambda h, i: (h, 0, 0, 0))
    kcspec = pl.BlockSpec((1, n_k, ATT_DH, tk), lambda h, i: (h, 0, 0, 0))
    return pl.pallas_call(
        body, name=name, grid=(ATT_KV, nq),
        in_specs=[cspec, rspec, krspec, kcspec, krspec, cspec, rspec, cspec,
                  pl.BlockSpec((1, 1, 1, R), lambda h, i: (h, i, 0, 0))],
        out_specs=[cspec, krspec, krspec],
        out_shape=[jax.ShapeDtypeStruct((ATT_KV, nq, ATT_DH, R), F32),
                   jax.ShapeDtypeStruct((ATT_KV, n_k, tk, ATT_DH), F32),
                   jax.ShapeDtypeStruct((ATT_KV, n_k, tk, ATT_DH), F32)],
        scratch_shapes=[pltpu.VMEM((ATT_DH, R), F32)],
        compiler_params=_params(("parallel", "arbitrary")),
    )(q_c, q_r, k_r, k_c, v_r, do_c, do_r, o_c, lse)


def _att_post_fwd(o, w, *, name, tm=512):
    T = o.shape[0]

    def body(o_ref, w_ref, out_ref):
        ov = o_ref[...]
        r = lax.rsqrt(jnp.mean(ov * ov, axis=-1, keepdims=True) + EPS)
        out_ref[...] = (ov * r * w_ref[...]).astype(BF16)

    tok = pl.BlockSpec((tm, ATT_QW), lambda i: (i, 0))
    return pl.pallas_call(
        body, name=name, grid=(T // tm,),
        in_specs=[tok, pl.BlockSpec((1, ATT_QW), lambda i: (0, 0))],
        out_specs=tok, out_shape=jax.ShapeDtypeStruct((T, ATT_QW), BF16),
        compiler_params=_params(("parallel",)),
    )(o, w)


def _att_post_bwd(dmix, o, w, *, name, tm=512):
    T = o.shape[0]

    def body(dm_ref, o_ref, w_ref, do_ref, dw_ref):
        @pl.when(pl.program_id(0) == 0)
        def _():
            dw_ref[...] = jnp.zeros_like(dw_ref)

        ov = o_ref[...]
        r = lax.rsqrt(jnp.mean(ov * ov, axis=-1, keepdims=True) + EPS)
        xh = ov * r
        dm = dm_ref[...]
        dxh = dm * w_ref[...]
        t = jnp.mean(dxh * xh, axis=-1, keepdims=True)
        do_ref[...] = (r * (dxh - xh * t)).astype(BF16)
        dw_ref[...] += jnp.sum(dm * xh, axis=0, keepdims=True)

    tok = pl.BlockSpec((tm, ATT_QW), lambda i: (i, 0))
    vec = pl.BlockSpec((1, ATT_QW), lambda i: (0, 0))
    return pl.pallas_call(
        body, name=name, grid=(T // tm,),
        in_specs=[pl.BlockSpec((tm, ATT_QW), lambda i: (i, 1)), tok, vec],
        out_specs=[tok, vec],
        out_shape=[jax.ShapeDtypeStruct((T, ATT_QW), BF16), jax.ShapeDtypeStruct((1, ATT_QW), F32)],
        compiler_params=_params(("arbitrary",)),
    )(dmix, o, w)


FA_HP = ATT_KV * ATT_DH
FA_TK_FWD = 512
FA_TK_BWD = 512


def _cols_from_tokens(x, kv):
    w = ATT_G * ATT_DH
    xt = x[:, kv * w:(kv + 1) * w].T
    return jnp.concatenate([xt[g * ATT_DH:(g + 1) * ATT_DH, :] for g in range(ATT_G)], axis=1)


def _tokens_from_cols(c):
    tq = c.shape[1] // ATT_G
    return jnp.concatenate([c[:, g * tq:(g + 1) * tq] for g in range(ATT_G)], axis=0).T


def _store_padded_cols(ref, x, norm_ref=None):
    for kv in range(ATT_KV):
        cols = _cols_from_tokens(x, kv).astype(BF16)
        ref[kv, 0, kv * ATT_DH:(kv + 1) * ATT_DH, :] = cols
        ref[kv, 0, (1 - kv) * ATT_DH:(2 - kv) * ATT_DH, :] = jnp.zeros_like(cols)
        if norm_ref is not None:
            cf = cols.astype(F32)
            norm_ref[kv, 0] = jnp.sqrt(jnp.sum(cf * cf, axis=0, keepdims=True))


def _att_prep_fwd2(U, cos, sin, qw, kw, *, name):
    T = U.shape[0]
    tm = min(FA_TQ, T)
    R = ATT_G * tm
    scale = ATT_DH ** -0.5

    def body(aq_ref, ak_ref, av_ref, c_ref, s_ref, qw_ref, kw_ref, q_ref, k_ref, v_ref, qn_ref, kmax_ref):
        @pl.when(pl.program_id(0) == 0)
        def _():
            kmax_ref[...] = jnp.zeros_like(kmax_ref)

        bd = _head_blockdiag(ATT_QW)
        c2, s2 = c_ref[...], s_ref[...]
        c8, s8 = jnp.tile(c2, (1, 4)), jnp.tile(s2, (1, 4))

        def norm_rope(x, w, c, s, bdm):
            r = lax.rsqrt(_head_sum(x * x, bdm) * (1.0 / ATT_DH) + EPS)
            y = x * r * w
            even = (lax.broadcasted_iota(jnp.int32, y.shape, 1) & 1) == 0
            return y * c + _pair_swap(y, even) * s

        _store_padded_cols(q_ref, norm_rope(aq_ref[...], qw_ref[...], c8, s8, bd) * scale, qn_ref)
        kb = norm_rope(ak_ref[...], kw_ref[...], c2, s2, bd[:ATT_KW, :ATT_KW]).astype(BF16)
        k_ref[...] = kb
        kf = kb.astype(F32)
        ksq = _head_sum(kf * kf, bd[:ATT_KW, :ATT_KW])
        kmax_ref[...] = jnp.maximum(kmax_ref[...], jnp.max(ksq, axis=0, keepdims=True))
        v_ref[...] = av_ref[...].astype(BF16)

    kv_spec = pl.BlockSpec((tm, ATT_KW), lambda i: (i, 0))
    return pl.pallas_call(
        body, name=name, grid=(T // tm,),
        in_specs=[pl.BlockSpec((tm, ATT_QW), lambda i: (i, 5)),
                  pl.BlockSpec((tm, ATT_KW), lambda i: (i, 24)), pl.BlockSpec((tm, ATT_KW), lambda i: (i, 25)),
                  kv_spec, kv_spec,
                  pl.BlockSpec((1, ATT_QW), lambda i: (0, 0)), pl.BlockSpec((1, ATT_KW), lambda i: (0, 0))],
        out_specs=[pl.BlockSpec((ATT_KV, 1, FA_HP, R), lambda i: (0, i, 0, 0)), kv_spec, kv_spec,
                   pl.BlockSpec((ATT_KV, 1, 1, R), lambda i: (0, i, 0, 0)), pl.BlockSpec((1, ATT_KW), lambda i: (0, 0))],
        out_shape=[jax.ShapeDtypeStruct((ATT_KV, T // tm, FA_HP, R), BF16),
                   jax.ShapeDtypeStruct((T, ATT_KW), BF16), jax.ShapeDtypeStruct((T, ATT_KW), BF16),
                   jax.ShapeDtypeStruct((ATT_KV, T // tm, 1, R), F32), jax.ShapeDtypeStruct((1, ATT_KW), F32)],
        compiler_params=_params(("arbitrary",)),
    )(U, U, U, cos, sin, qw, kw)


def _att_prep_bwd2(U, dq_c, dk, cos, sin, qw, kw, *, name):
    T = U.shape[0]
    tm = min(FA_TQ, T)
    R = ATT_G * tm
    scale = ATT_DH ** -0.5

    def body(aq_ref, ak_ref, dq_ref, dk_ref, c_ref, s_ref, qw_ref, kw_ref, daq_ref, dak_ref, dqw_ref, dkw_ref):
        @pl.when(pl.program_id(0) == 0)
        def _():
            dqw_ref[...] = jnp.zeros_like(dqw_ref)
            dkw_ref[...] = jnp.zeros_like(dkw_ref)

        bd = _head_blockdiag(ATT_QW)
        c2, s2 = c_ref[...], s_ref[...]
        c8, s8 = jnp.tile(c2, (1, 4)), jnp.tile(s2, (1, 4))

        def bwd(x, dy, w, c, s, bdm):
            even = (lax.broadcasted_iota(jnp.int32, x.shape, 1) & 1) == 0
            dn = dy * c - _pair_swap(dy, even) * s
            r = lax.rsqrt(_head_sum(x * x, bdm) * (1.0 / ATT_DH) + EPS)
            xh = x * r
            dxh = dn * w
            t = _head_sum(dxh * xh, bdm) * (1.0 / ATT_DH)
            return r * (dxh - xh * t), jnp.sum(dn * xh, axis=0, keepdims=True)

        dq = jnp.concatenate([_tokens_from_cols(dq_ref[kv, 0]) for kv in range(ATT_KV)], axis=1)
        da, dw = bwd(aq_ref[...], dq * scale, qw_ref[...], c8, s8, bd)
        daq_ref[...] = da
        dqw_ref[...] += dw
        da, dw = bwd(ak_ref[...], dk_ref[...], kw_ref[...], c2, s2, bd[:ATT_KW, :ATT_KW])
        dak_ref[...] = da
        dkw_ref[...] += dw

    q_spec = pl.BlockSpec((tm, ATT_QW), lambda i: (i, 0))
    kv_spec = pl.BlockSpec((tm, ATT_KW), lambda i: (i, 0))
    qv = pl.BlockSpec((1, ATT_QW), lambda i: (0, 0))
    kv = pl.BlockSpec((1, ATT_KW), lambda i: (0, 0))
    return pl.pallas_call(
        body, name=name, grid=(T // tm,),
        in_specs=[pl.BlockSpec((tm, ATT_QW), lambda i: (i, 5)), pl.BlockSpec((tm, ATT_KW), lambda i: (i, 24)),
                  pl.BlockSpec((ATT_KV, 1, ATT_DH, R), lambda i: (0, i, 0, 0)), kv_spec, kv_spec, kv_spec, qv, kv],
        out_specs=[q_spec, kv_spec, qv, kv],
        out_shape=[jax.ShapeDtypeStruct((T, ATT_QW), F32), jax.ShapeDtypeStruct((T, ATT_KW), F32),
                   jax.ShapeDtypeStruct((1, ATT_QW), F32), jax.ShapeDtypeStruct((1, ATT_KW), F32)],
        compiler_params=_params(("arbitrary",)),
    )(U, U, dq_c, dk, cos, sin, qw, kw)


def _pick_head(x, kv):
    return jnp.where(kv == 0, x[0:ATT_DH, :], x[ATT_DH:FA_HP, :])


def _flash_fwd2(q_c, k, v, *, name):
    _, nq, _, R = q_c.shape
    T = k.shape[0]
    tk = min(FA_TK_FWD, T)
    n_k = T // tk

    def body(q_ref, k_ref, v_ref, o_ref, lse_ref, acc_ref):
        kv = pl.program_id(0)
        qv = q_ref[0, 0]
        acc_ref[...] = jnp.zeros_like(acc_ref)

        def step(j, carry):
            m, l = carry
            s = jnp.dot(k_ref[j], qv, preferred_element_type=F32)
            m_new = jnp.maximum(m, jnp.max(s, axis=0, keepdims=True))
            alpha = jnp.exp(m - m_new)
            p = jnp.exp(s - m_new)
            l = alpha * l + jnp.sum(p, axis=0, keepdims=True)
            pv = lax.dot_general(v_ref[j], p.astype(BF16), (((0,), (0,)), ((), ())), preferred_element_type=F32)
            acc_ref[...] = alpha * acc_ref[...] + _pick_head(pv, kv)
            return m_new, l

        m, l = lax.fori_loop(0, n_k, step, (jnp.full((1, R), -jnp.inf, F32), jnp.zeros((1, R), F32)))
        o_ref[0, 0] = acc_ref[...] / l
        lse_ref[0, 0] = m + jnp.log(l)

    kspec = pl.BlockSpec((n_k, tk, FA_HP), lambda h, i: (0, 0, 0))
    return pl.pallas_call(
        body, name=name, grid=(ATT_KV, nq),
        in_specs=[pl.BlockSpec((1, 1, FA_HP, R), lambda h, i: (h, i, 0, 0)), kspec, kspec],
        out_specs=[pl.BlockSpec((1, 1, ATT_DH, R), lambda h, i: (h, i, 0, 0)),
                   pl.BlockSpec((1, 1, 1, R), lambda h, i: (h, i, 0, 0))],
        out_shape=[jax.ShapeDtypeStruct((ATT_KV, nq, ATT_DH, R), F32), jax.ShapeDtypeStruct((ATT_KV, nq, 1, R), F32)],
        scratch_shapes=[pltpu.VMEM((ATT_DH, R), F32)],
        compiler_params=_params(("parallel", "parallel")),
    )(q_c, k.reshape(n_k, tk, FA_HP), v.reshape(n_k, tk, FA_HP))


FA_BOUND_MAX = 40.0
FA_TK_FAST = 512


def _flash_fwd_bounded(q_c, k, v, m_c, *, name):
    _, nq, _, R = q_c.shape
    T = k.shape[0]
    tk = min(FA_TK_FAST, T)
    n_k = T // tk

    def body(q_ref, k_ref, v_ref, m_ref, o_ref, lse_ref, acc_ref):
        kv = pl.program_id(0)
        qv = q_ref[0, 0]
        m = m_ref[0, 0]
        acc_ref[...] = jnp.zeros_like(acc_ref)

        def step(j, l8):
            s = jnp.dot(k_ref[j], qv, preferred_element_type=F32)
            p = jnp.exp(s - m)
            l8 = l8 + jnp.sum(p.reshape(tk // 8, 8, R), axis=0)
            acc_ref[...] += lax.dot_general(v_ref[j], p.astype(BF16), (((0,), (0,)), ((), ())),
                                            preferred_element_type=F32)
            return l8

        l8 = lax.fori_loop(0, n_k, step, jnp.zeros((8, R), F32))
        l = jnp.sum(l8, axis=0, keepdims=True)
        o_ref[0, 0] = _pick_head(acc_ref[...], kv) / l
        lse_ref[0, 0] = m + jnp.log(l)

    kspec = pl.BlockSpec((n_k, tk, FA_HP), lambda h, i: (0, 0, 0))
    vspec = pl.BlockSpec((1, 1, 1, R), lambda h, i: (h, i, 0, 0))
    return pl.pallas_call(
        body, name=name, grid=(ATT_KV, nq),
        in_specs=[pl.BlockSpec((1, 1, FA_HP, R), lambda h, i: (h, i, 0, 0)), kspec, kspec, vspec],
        out_specs=[pl.BlockSpec((1, 1, ATT_DH, R), lambda h, i: (h, i, 0, 0)), vspec],
        out_shape=[jax.ShapeDtypeStruct((ATT_KV, nq, ATT_DH, R), F32), jax.ShapeDtypeStruct((ATT_KV, nq, 1, R), F32)],
        scratch_shapes=[pltpu.VMEM((FA_HP, R), F32)],
        compiler_params=_params(("parallel", "parallel")),
    )(q_c, k.reshape(n_k, tk, FA_HP), v.reshape(n_k, tk, FA_HP), m_c)


def _flash_bwd2(q_c, k, v, do_c, lse, delta, *, name):
    _, nq, _, R = q_c.shape
    T = k.shape[0]
    tk = min(FA_TK_BWD, T)
    n_k = T // tk

    def body(qc_ref, k_ref, v_ref, doc_ref, lse_ref, delta_ref, dq_ref, dk_ref, dv_ref, acc_ref):
        kv = pl.program_id(0)

        @pl.when((kv == 0) & (pl.program_id(1) == 0))
        def _():
            dk_ref[...] = jnp.zeros_like(dk_ref)
            dv_ref[...] = jnp.zeros_like(dv_ref)

        qc, doc = qc_ref[0, 0], doc_ref[0, 0]
        lsev, delta = lse_ref[0, 0], delta_ref[0, 0]
        acc_ref[...] = jnp.zeros_like(acc_ref)

        def step(j, carry):
            kb = k_ref[j]
            s = jnp.dot(kb, qc, preferred_element_type=F32)
            p = jnp.exp(s - lsev)
            dp = jnp.dot(v_ref[j], doc, preferred_element_type=F32)
            ds = (p * (dp - delta)).astype(BF16)
            acc_ref[...] += lax.dot_general(kb, ds, (((0,), (0,)), ((), ())), preferred_element_type=F32)
            dk_ref[j] += lax.dot_general(ds, qc, (((1,), (1,)), ((), ())), preferred_element_type=F32)
            dv_ref[j] += lax.dot_general(p.astype(BF16), doc, (((1,), (1,)), ((), ())), preferred_element_type=F32)
            return carry

        lax.fori_loop(0, n_k, step, 0)
        dq_ref[0, 0] = _pick_head(acc_ref[...], kv)

    cspec = pl.BlockSpec((1, 1, FA_HP, R), lambda h, i: (h, i, 0, 0))
    vspec = pl.BlockSpec((1, 1, 1, R), lambda h, i: (h, i, 0, 0))
    kspec = pl.BlockSpec((n_k, tk, FA_HP), lambda h, i: (0, 0, 0))
    dq_c, dk, dv = pl.pallas_call(
        body, name=name, grid=(ATT_KV, nq),
        in_specs=[cspec, kspec, kspec, cspec, vspec, vspec],
        out_specs=[pl.BlockSpec((1, 1, ATT_DH, R), lambda h, i: (h, i, 0, 0)), kspec, kspec],
        out_shape=[jax.ShapeDtypeStruct((ATT_KV, nq, ATT_DH, R), F32),
                   jax.ShapeDtypeStruct((n_k, tk, FA_HP), F32), jax.ShapeDtypeStruct((n_k, tk, FA_HP), F32)],
        scratch_shapes=[pltpu.VMEM((FA_HP, R), F32)],
        compiler_params=_params(("arbitrary", "arbitrary")),
    )(q_c, k.reshape(n_k, tk, FA_HP), v.reshape(n_k, tk, FA_HP), do_c, lse, delta)
    return dq_c, dk.reshape(T, FA_HP), dv.reshape(T, FA_HP)


def _att_post_fwd2(o_c, w, *, name):
    _, nq, _, R = o_c.shape
    tm = R // ATT_G
    T = nq * tm

    def body(oc_ref, w_ref, o_ref, out_ref):
        ov = jnp.concatenate([_tokens_from_cols(oc_ref[kv, 0]) for kv in range(ATT_KV)], axis=1)
        r = lax.rsqrt(jnp.mean(ov * ov, axis=-1, keepdims=True) + EPS)
        o_ref[...] = ov
        out_ref[...] = (ov * r * w_ref[...]).astype(BF16)

    tok = pl.BlockSpec((tm, ATT_QW), lambda i: (i, 0))
    return pl.pallas_call(
        body, name=name, grid=(nq,),
        in_specs=[pl.BlockSpec((ATT_KV, 1, ATT_DH, R), lambda i: (0, i, 0, 0)), pl.BlockSpec((1, ATT_QW), lambda i: (0, 0))],
        out_specs=[tok, tok],
        out_shape=[jax.ShapeDtypeStruct((T, ATT_QW), F32), jax.ShapeDtypeStruct((T, ATT_QW), BF16)],
        compiler_params=_params(("parallel",)),
    )(o_c, w)


def _att_post_bwd2(dmix, o, w, *, name):
    T = o.shape[0]
    tm = min(FA_TQ, T)
    R = ATT_G * tm

    def body(dm_ref, o_ref, w_ref, do_ref, delta_ref, dw_ref):
        @pl.when(pl.program_id(0) == 0)
        def _():
            dw_ref[...] = jnp.zeros_like(dw_ref)

        ov = o_ref[...]
        r = lax.rsqrt(jnp.mean(ov * ov, axis=-1, keepdims=True) + EPS)
        xh = ov * r
        dm = dm_ref[...]
        dxh = dm * w_ref[...]
        t = jnp.mean(dxh * xh, axis=-1, keepdims=True)
        do = r * (dxh - xh * t)
        _store_padded_cols(do_ref, do)
        dob = do.astype(BF16).astype(F32)
        for kv in range(ATT_KV):
            delta_ref[kv, 0] = jnp.sum(_cols_from_tokens(dob * ov, kv), axis=0, keepdims=True)
        dw_ref[...] += jnp.sum(dm * xh, axis=0, keepdims=True)

    tok = pl.BlockSpec((tm, ATT_QW), lambda i: (i, 0))
    vec = pl.BlockSpec((1, ATT_QW), lambda i: (0, 0))
    return pl.pallas_call(
        body, name=name, grid=(T // tm,),
        in_specs=[pl.BlockSpec((tm, ATT_QW), lambda i: (i, 1)), tok, vec],
        out_specs=[pl.BlockSpec((ATT_KV, 1, FA_HP, R), lambda i: (0, i, 0, 0)),
                   pl.BlockSpec((ATT_KV, 1, 1, R), lambda i: (0, i, 0, 0)), vec],
        out_shape=[jax.ShapeDtypeStruct((ATT_KV, T // tm, FA_HP, R), BF16),
                   jax.ShapeDtypeStruct((ATT_KV, T // tm, 1, R), F32), jax.ShapeDtypeStruct((1, ATT_QW), F32)],
        compiler_params=_params(("arbitrary",)),
    )(dmix, o, w)


def _ffn_up(h2, wg, wu, *, name, tm=512):
    T = h2.shape[0]
    tn = _pick(D_FF, 1408)

    def body(h_ref, wg_ref, wu_ref, g_ref, u_ref, a_ref):
        hv = h_ref[...]
        g = jnp.dot(hv, wg_ref[...], preferred_element_type=F32)
        u = jnp.dot(hv, wu_ref[...], preferred_element_type=F32)
        g_ref[...] = g.astype(BF16)
        u_ref[...] = u.astype(BF16)
        a_ref[...] = (g * _sigmoid(g) * u).astype(BF16)

    wspec = pl.BlockSpec((D_MODEL, tn), lambda i, j: (0, j))
    ospec = pl.BlockSpec((tm, tn), lambda i, j: (i, j))
    return pl.pallas_call(
        body, name=name, grid=(T // tm, D_FF // tn),
        in_specs=[pl.BlockSpec((tm, D_MODEL), lambda i, j: (i, 0)), wspec, wspec],
        out_specs=[ospec] * 3, out_shape=[jax.ShapeDtypeStruct((T, D_FF), BF16)] * 3,
        compiler_params=_params(("parallel", "arbitrary")),
    )(h2, wg, wu)


def _ffn_act_bwd(dx2b, w_down, gate, up, *, name, tm=512):
    T = dx2b.shape[0]
    tn = _pick(D_FF, 1408)

    def body(dx_ref, w_ref, g_ref, u_ref, dg_ref, du_ref):
        da = lax.dot_general(dx_ref[...], w_ref[...], (((1,), (1,)), ((), ())), preferred_element_type=F32)
        g = g_ref[...].astype(F32)
        u = u_ref[...].astype(F32)
        sg = _sigmoid(g)
        dg_ref[...] = (da * u * (sg * (1.0 + g * (1.0 - sg)))).astype(BF16)
        du_ref[...] = (da * (g * sg)).astype(BF16)

    ospec = pl.BlockSpec((tm, tn), lambda i, j: (i, j))
    return pl.pallas_call(
        body, name=name, grid=(T // tm, D_FF // tn),
        in_specs=[pl.BlockSpec((tm, D_MODEL), lambda i, j: (i, 0)),
                  pl.BlockSpec((tn, D_MODEL), lambda i, j: (j, 0)), ospec, ospec],
        out_specs=[ospec] * 2, out_shape=[jax.ShapeDtypeStruct((T, D_FF), BF16)] * 2,
        compiler_params=_params(("parallel", "arbitrary")),
    )(dx2b, w_down, gate, up)


def _assemble_du(U, dq_f, dq_b, dz_f, dz_b, dv_f, dv_b, du_g, da_q, da_k, da_v, *, name, tm=256):
    T = U.shape[0]

    def body(uq_ref, dqf, dqb, dzf, dzb, dvf, dvb, dug, daq, dak, dav, out_ref):
        uq = uq_ref[...]
        sg = _sigmoid(uq)
        out_ref[:, 0:HG_W] = ((dqf[...] + dqb[...]) * (sg * (1.0 + uq * (1.0 - sg)))).astype(BF16)
        out_ref[:, HG_W:2 * HG_W] = dzf[...].astype(BF16)
        out_ref[:, 2 * HG_W:3 * HG_W] = dzb[...].astype(BF16)
        out_ref[:, 3 * HG_W:4 * HG_W] = (dvf[...] + dvb[...]).astype(BF16)
        out_ref[:, 4 * HG_W:5 * HG_W] = dug[...].astype(BF16)
        out_ref[:, 5 * HG_W:5 * HG_W + ATT_QW] = daq[...].astype(BF16)
        out_ref[:, 5 * HG_W + ATT_QW:5 * HG_W + ATT_QW + ATT_KW] = dak[...].astype(BF16)
        out_ref[:, 5 * HG_W + ATT_QW + ATT_KW:D_IN] = dav[...].astype(BF16)

    tok = pl.BlockSpec((tm, HG_W), lambda i: (i, 0))
    kv = pl.BlockSpec((tm, ATT_KW), lambda i: (i, 0))
    return pl.pallas_call(
        body, name=name, grid=(T // tm,),
        in_specs=[tok] * 9 + [kv, kv],
        out_specs=pl.BlockSpec((tm, D_IN), lambda i: (i, 0)),
        out_shape=jax.ShapeDtypeStruct((T, D_IN), BF16),
        compiler_params=_params(("parallel",)),
    )(U, dq_f, dq_b, dz_f, dz_b, dv_f, dv_b, du_g, da_q, da_k, da_v)


def _adam_math(w, g, m, v):
    m = ADAM_B1 * m + (1.0 - ADAM_B1) * g
    v = ADAM_B2 * v + (1.0 - ADAM_B2) * (g * g)
    m_hat = m / (1.0 - ADAM_B1 ** ADAM_STEP)
    v_hat = v / (1.0 - ADAM_B2 ** ADAM_STEP)
    delta = -ADAM_LR * (m_hat / (jnp.sqrt(v_hat) + ADAM_EPS) + ADAM_WD * w)
    return delta, m, v


def _adamw(parts, w, m, v, *, name, tr_cap=256):
    P, R, C = parts.shape
    tr = R
    for t in range(8, min(R, tr_cap) + 1, 8):
        if R % t == 0:
            tr = t

    def body(p_ref, w_ref, m_ref, v_ref, g_ref, d_ref, nm_ref, nv_ref):
        g = p_ref[0].astype(F32)
        for j in range(1, P):
            g = g + p_ref[j].astype(F32)
        d, nm, nv = _adam_math(w_ref[...], g, m_ref[...], v_ref[...])
        g_ref[...] = g
        d_ref[...] = d
        nm_ref[...] = nm
        nv_ref[...] = nv

    blk = pl.BlockSpec((tr, C), lambda i: (i, 0))
    return pl.pallas_call(
        body, name=name, grid=(R // tr,),
        in_specs=[pl.BlockSpec((P, tr, C), lambda i: (0, i, 0)), blk, blk, blk],
        out_specs=[blk] * 4, out_shape=[jax.ShapeDtypeStruct((R, C), F32)] * 4,
        compiler_params=_params(("parallel",)),
    )(parts, w, m, v)


def _all_gather(xs, *, name):
    n = len(xs)

    def body(*refs):
        ins, outs = refs[:n], refs[n:2 * n]
        send_sems, recv_sems, local_sems = refs[2 * n:]
        x, y, c = lax.axis_index("x"), lax.axis_index("y"), lax.axis_index("c")
        me, sibling = (x, y, c), (x, y, 1 - c)
        chips = [(1 - x, y), (x, 1 - y), (1 - x, 1 - y)]

        def slot(p):
            return 4 * p[0] + 2 * p[1] + p[2]

        def copy(a, k, block, to, src=None):
            dst = outs[a].at[slot(block)]
            return pltpu.make_async_remote_copy(
                src_ref=dst if src is None else src, dst_ref=dst,
                send_sem=send_sems.at[a * 7 + k], recv_sem=recv_sems.at[a * 7 + k],
                device_id=to, device_id_type=MESH)

        mine = [pltpu.make_async_copy(ins[a], outs[a].at[slot(me)], local_sems.at[a]) for a in range(n)]
        for cp in mine:
            cp.start()
        first = []
        for a in range(n):
            first.append(copy(a, 0, me, sibling, src=ins[a]))
            first += [copy(a, 1 + j, me, (*chip, c), src=ins[a]) for j, chip in enumerate(chips)]
        for cp in first:
            cp.start()
        passed = []
        for j, chip in enumerate(chips):
            for a in range(n):
                copy(a, 1 + j, (*chip, c), me).wait_recv()
                cp = copy(a, 4 + j, (*chip, c), sibling)
                cp.start()
                passed.append(cp)
        for a in range(n):
            copy(a, 0, sibling, me).wait_recv()
            for j, chip in enumerate(chips):
                copy(a, 4 + j, (*chip, 1 - c), me).wait_recv()
        for cp in first + passed:
            cp.wait_send()
        for cp in mine:
            cp.wait()

    return pl.pallas_call(
        body, name=name,
        in_specs=[ANY] * n, out_specs=[ANY] * n,
        out_shape=[jax.ShapeDtypeStruct((N_DEV,) + x.shape, x.dtype) for x in xs],
        scratch_shapes=[pltpu.SemaphoreType.DMA((7 * n,)), pltpu.SemaphoreType.DMA((7 * n,)),
                        pltpu.SemaphoreType.DMA((n,))],
        compiler_params=pltpu.CompilerParams(has_side_effects=True),
    )(*xs)


def _exchange(gs, *, masks, slot, name):
    n, n_peers = len(gs), len(masks)

    def body(*refs):
        ins, outs = refs[:n], refs[n:2 * n]
        send_sems, recv_sems, local_sems = refs[2 * n:]
        x, y, c = lax.axis_index("x"), lax.axis_index("y"), lax.axis_index("c")
        my_slot = slot((x, y, c))

        def flip(v, bit):
            return 1 - v if bit else v

        mine = [pltpu.make_async_copy(ins[a].at[my_slot], outs[a].at[my_slot], local_sems.at[a]) for a in range(n)]
        for cp in mine:
            cp.start()
        copies = []
        for a in range(n):
            for k, (mx, my, mc) in enumerate(masks):
                peer = (flip(x, mx), flip(y, my), flip(c, mc))
                peer_slot = slot(peer)
                sems = dict(send_sem=send_sems.at[a * n_peers + k], recv_sem=recv_sems.at[a * n_peers + k],
                            device_id=peer, device_id_type=MESH)
                copies.append((
                    pltpu.make_async_remote_copy(src_ref=ins[a].at[peer_slot], dst_ref=outs[a].at[my_slot], **sems),
                    pltpu.make_async_remote_copy(src_ref=ins[a].at[peer_slot], dst_ref=outs[a].at[peer_slot], **sems)))
        for send, _ in copies:
            send.start()
        for send, recv in copies:
            recv.wait_recv()
            send.wait_send()
        for cp in mine:
            cp.wait()

    return pl.pallas_call(
        body, name=name,
        in_specs=[ANY] * n, out_specs=[ANY] * n,
        out_shape=[jax.ShapeDtypeStruct(g.shape, g.dtype) for g in gs],
        scratch_shapes=[pltpu.SemaphoreType.DMA((n_peers * n,)), pltpu.SemaphoreType.DMA((n_peers * n,)),
                        pltpu.SemaphoreType.DMA((n,))],
        compiler_params=pltpu.CompilerParams(has_side_effects=True),
    )(*gs)


SWAP_ROW_CHUNKS = 4


def _core_swap(gs, *, name):
    n = len(gs)

    def body(*refs):
        ins, outs = refs[:n], refs[n:2 * n]
        send_sems, recv_sems = refs[2 * n:]
        x, y, c = lax.axis_index("x"), lax.axis_index("y"), lax.axis_index("c")
        sibling = (x, y, 1 - c)
        started = []
        for a in range(n):
            _, Q, R, _ = ins[a].shape
            rows = R // SWAP_ROW_CHUNKS
            for q in range(Q):
                for j in range(SWAP_ROW_CHUNKS):
                    cp = pltpu.make_async_remote_copy(
                        src_ref=ins[a].at[1 - c, q, pl.ds(j * rows, rows)], dst_ref=outs[a].at[q, pl.ds(j * rows, rows)],
                        send_sem=send_sems.at[a], recv_sem=recv_sems.at[a], device_id=sibling, device_id_type=MESH)
                    cp.start()
                    started.append(cp)
        for a in range(n):
            pltpu.make_async_remote_copy(
                src_ref=ins[a].at[1 - c], dst_ref=outs[a], send_sem=send_sems.at[a], recv_sem=recv_sems.at[a],
                device_id=sibling, device_id_type=MESH).wait()

    return pl.pallas_call(
        body, name=name,
        in_specs=[ANY] * n, out_specs=[ANY] * n,
        out_shape=[jax.ShapeDtypeStruct(g.shape[1:], g.dtype) for g in gs],
        scratch_shapes=[pltpu.SemaphoreType.DMA((n,)), pltpu.SemaphoreType.DMA((n,))],
        compiler_params=pltpu.CompilerParams(has_side_effects=True),
    )(*gs)


def _pair_sum(g, other, core, *, name, tr_cap=256):
    _, Q, R, C = g.shape
    tr = max(t for t in range(16, min(R, tr_cap) + 1, 16) if R % t == 0)

    def body(core_ref, g_ref, o_ref, out_ref):
        out_ref[0] = (g_ref[0, 0] + o_ref[0]).astype(BF16)

    return pl.pallas_call(
        body, name=name,
        grid_spec=pltpu.PrefetchScalarGridSpec(
            num_scalar_prefetch=1, grid=(Q, R // tr),
            in_specs=[pl.BlockSpec((1, 1, tr, C), lambda q, i, core_ref: (core_ref[0], q, i, 0)),
                      pl.BlockSpec((1, tr, C), lambda q, i, core_ref: (q, i, 0))],
            out_specs=pl.BlockSpec((1, tr, C), lambda q, i, core_ref: (q, i, 0))),
        out_shape=jax.ShapeDtypeStruct((Q, R, C), BF16),
        compiler_params=_params(("parallel", "parallel")),
    )(core, g, other)


PACK_ROWS = 8


def _pack_small(norm1, norm2, final, att, hg, qn, kn, lb=None, loss=None):
    z = lambda n: jnp.zeros((n,), F32)
    rows = [norm1.reshape(-1), norm2.reshape(-1), final.reshape(-1),
            jnp.concatenate([att.reshape(-1), z(512)]),
            jnp.concatenate([hg.reshape(-1), qn.reshape(-1), kn.reshape(-1), z(1024 - 256)]),
            z(1024) if lb is None else lb.reshape(-1),
            z(1024) if loss is None else jnp.concatenate([loss.reshape(-1), z(1023)]), z(1024)]
    return jnp.stack(rows, axis=0)


def _unpack_small(p):
    return (p[0:1, :], p[1:2, :], p[2, :], p[3:4, 0:512], p[4:5, 0:128], p[4:5, 128:192], p[4:5, 192:256])


def _fold_heads(dhg, dqn, dkn, *, name):
    def body(hg_ref, q_ref, k_ref, ohg_ref, oq_ref, ok_ref):
        def fold128(v):
            acc = v[:, 0:LANES]
            for j in range(1, v.shape[1] // LANES):
                acc = acc + v[:, j * LANES:(j + 1) * LANES]
            return acc

        ohg_ref[...] = fold128(hg_ref[...])
        q = fold128(q_ref[...])
        oq_ref[...] = q + pltpu.roll(q, ATT_DH, 1)
        k = k_ref[...]
        ok_ref[...] = k + pltpu.roll(k, ATT_DH, 1)

    return pl.pallas_call(body, name=name, out_shape=[jax.ShapeDtypeStruct((1, LANES), F32)] * 3)(dhg, dqn, dkn)


def _lb_grad(dlb_sum, lb, *, name):
    def body(d_ref, lb_ref, o_ref):
        lbv = lb_ref[...]
        gl = d_ref[...] * lbv * (1.0 - lbv)
        o_ref[0:1, :] = gl[0:1, :]
        o_ref[1:2, :] = -gl[0:1, :]
        o_ref[2:3, :] = gl[1:2, :]
        o_ref[3:4, :] = -gl[1:2, :]

    return pl.pallas_call(body, name=name, out_shape=jax.ShapeDtypeStruct((4, HG_W), F32))(dlb_sum, lb)


def _lower_bounds(lb_logits_full, *, name):
    def body(l_ref, o_ref):
        for d in range(2):
            l0, l1 = l_ref[2 * d:2 * d + 1, :], l_ref[2 * d + 1:2 * d + 2, :]
            mx = jnp.maximum(l0, l1)
            e0, e1 = jnp.exp(l0 - mx), jnp.exp(l1 - mx)
            o_ref[d:d + 1, :] = e0 / (e0 + e1)

    return pl.pallas_call(body, name=name, out_shape=jax.ShapeDtypeStruct((2, HG_W), F32))(
        lb_logits_full.reshape(4, HG_W))


def _local_step(x, target, norm1_w, w_in, lb, hg_norm_w, q_norm_w, k_norm_w, att_norm_w, w_out, norm2_w,
                w_g, w_u, w_down, final_norm_w):
    T = x.shape[0]
    cos, sin = _rope_tables(T)
    qw8 = jnp.tile(q_norm_w, (1, ATT_HEADS))
    kw2 = jnp.tile(k_norm_w, (1, ATT_KV))

    h, r1 = _rms_fwd(x, norm1_w, name="norm1_fwd")
    U = _mm_nn([(h, w_in)], name="in_proj")
    o_f, st_f = _gla_fwd(U, lb[0:1], f_block=1, reverse=False, name="gla_fwd_f")
    o_b, st_b = _gla_fwd(U, lb[1:2], f_block=2, reverse=True, name="gla_fwd_b")
    mix_hg = _hg_post_fwd(o_f, o_b, U, hg_norm_w, name="hg_post_fwd")
    q_c, k, v, qn_c, kmax2 = _att_prep_fwd2(U, cos, sin, qw8, kw2, name="att_prep_fwd")
    kmax = jnp.sqrt(jnp.max(kmax2.reshape(ATT_KV, ATT_DH), axis=1))
    m_c = qn_c * (kmax * 1.001).reshape(ATT_KV, 1, 1, 1)
    o_c, lse = lax.cond(jnp.max(m_c) <= FA_BOUND_MAX,
                        lambda: _flash_fwd_bounded(q_c, k, v, m_c, name="flash_fwd_bounded"),
                        lambda: _flash_fwd2(q_c, k, v, name="flash_fwd"))
    o_att, mix_att = _att_post_fwd2(o_c, att_norm_w, name="att_post_fwd")
    x1 = _mm_nn([(mix_hg, w_out[:HG_W]), (mix_att, w_out[HG_W:])], residual=x, name="out_proj")
    h2, r2 = _rms_fwd(x1, norm2_w, name="norm2_fwd")
    gate, up, act = _ffn_up(h2, w_g, w_u, name="ffn_up")
    x2 = _mm_nn([(act, w_down)], residual=x1, name="ffn_down")
    loss, dx2, dx2b, d_final = _loss_head(x2, target, final_norm_w.reshape(1, D_MODEL), name="loss_head")

    d_gate, d_up = _ffn_act_bwd(dx2b, w_down, gate, up, name="ffn_act_bwd")
    dw_down = _mm_tn(act, dx2b, tma_cap=1408, name="dw_down")
    dh2 = _mm_nn([(d_gate, w_g), (d_up, w_u)], trans_b=True, tm=256, name="ffn_up_bwd")
    dw_g = _mm_tn(h2, d_gate, tnb_cap=1408, name="dw_gate")
    dw_u = _mm_tn(h2, d_up, tnb_cap=1408, name="dw_up")
    dx1, dx1b, d_norm2 = _rms_bwd(dh2, x1, r2, norm2_w, dx2, emit_bf16=True, name="norm2_bwd")
    dmix = _mm_nn([(dx1b, w_out)], trans_b=True, name="out_proj_bwd")
    dw_out = jnp.concatenate([_mm_tn(mix_hg, dx1b, name="dw_out_hg"), _mm_tn(mix_att, dx1b, name="dw_out_att")], axis=0)
    do_c, delta, d_att = _att_post_bwd2(dmix, o_att, att_norm_w, name="att_post_bwd")
    dq_c, dk, da_v = _flash_bwd2(q_c, k, v, do_c, lse, delta, name="flash_bwd")
    da_q, da_k, d_qn, d_kn = _att_prep_bwd2(U, dq_c, dk, cos, sin, qw8, kw2, name="att_prep_bwd")
    do_hg, du_g, d_hg = _hg_post_bwd(dmix, o_f, o_b, U, hg_norm_w, name="hg_post_bwd")
    dq_f, dz_f, dv_f, dlb_f = _gla_bwd(U, lb[0:1], do_hg, st_f, f_block=1, reverse=False, name="gla_bwd_f")
    dq_b, dz_b, dv_b, dlb_b = _gla_bwd(U, lb[1:2], do_hg, st_b, f_block=2, reverse=True, name="gla_bwd_b")
    dU = _assemble_du(U, dq_f, dq_b, dz_f, dz_b, dv_f, dv_b, du_g, da_q, da_k, da_v, name="assemble_du")
    dh = _mm_nn([(dU, w_in)], trans_b=True, name="in_proj_bwd")
    dw_in = _mm_tn(h, dU, tnb_cap=1664, name="dw_in")
    grad_x, d_norm1 = _rms_bwd(dh, x, r1, norm1_w, dx1, emit_bf16=False, name="norm1_bwd")
    d_hg, d_qn, d_kn = _fold_heads(d_hg, d_qn, d_kn, name="fold_heads")

    big = dict(w_in=dw_in, w_out=dw_out, w_g=dw_g, w_u=dw_u, w_down=dw_down)
    small = dict(norm1=d_norm1, norm2=d_norm2, final=d_final, att=d_att, hg=d_hg,
                 qn=d_qn[:, :ATT_DH], kn=d_kn[:, :ATT_DH], lb=jnp.concatenate([dlb_f, dlb_b], axis=0))
    return loss, grad_x, big, small


def kernel(x, norm1_w, w_in, lb_logits, hg_norm_w, q_norm_w, k_norm_w, att_norm_w, w_out, norm2_w, w_gate_up, w_down, final_norm_w, loss_target, m_norm1_w, m_w_in, m_lb_logits, m_hg_norm_w, m_q_norm_w, m_k_norm_w, m_att_norm_w, m_w_out, m_norm2_w, m_w_gate_up, m_w_down, m_final_norm_w, v_norm1_w, v_w_in, v_lb_logits, v_hg_norm_w, v_q_norm_w, v_k_norm_w, v_att_norm_w, v_w_out, v_norm2_w, v_w_gate_up, v_w_down, v_final_norm_w):
    T = x.shape[1]
    me = 4 * lax.axis_index("x") + 2 * lax.axis_index("y") + lax.axis_index("c")
    c_in, r_out, c_gu, r_dn = w_in.shape[2], w_out.shape[1], w_gate_up.shape[2], w_down.shape[1]
    lb_cols = lb_logits.shape[2]

    g_in, g_out, g_gu, g_dn, g_lb = _all_gather(
        [w_in[0].astype(BF16), w_out[0].astype(BF16), w_gate_up[0].astype(BF16), w_down[0].astype(BF16),
         lb_logits.reshape(4, lb_cols)], name="gather_weights")
    w_in_f = g_in.transpose(1, 0, 2).reshape(D_MODEL, N_DEV * c_in)
    w_out_f = g_out.reshape(N_DEV * r_out, D_MODEL)
    half = N_DEV // 2
    w_g_f = g_gu[:half].transpose(1, 0, 2).reshape(D_MODEL, half * c_gu)
    w_u_f = g_gu[half:].transpose(1, 0, 2).reshape(D_MODEL, half * c_gu)
    w_dn_f = g_dn.reshape(N_DEV * r_dn, D_MODEL)
    lb_logits_f = g_lb.transpose(1, 0, 2).reshape(2, 2, N_DEV * lb_cols)
    lb = _lower_bounds(lb_logits_f, name="lower_bounds")

    loss, grad_x, big, small = _local_step(
        x[0], loss_target[0], norm1_w, w_in_f, lb, hg_norm_w, q_norm_w, k_norm_w, att_norm_w, w_out_f, norm2_w,
        w_g_f, w_u_f, w_dn_f, final_norm_w)

    chips = N_DEV // 2
    by_owner_cols = lambda g, n_q, w: g.reshape(D_MODEL, n_q, 2, w).transpose(2, 1, 0, 3)
    by_owner_rows = lambda g, r: g.reshape(chips, 2, r, D_MODEL).transpose(1, 0, 2, 3)
    s_in = by_owner_cols(big["w_in"], chips, c_in)
    s_out = by_owner_rows(big["w_out"], r_out)
    s_gu = jnp.concatenate([by_owner_cols(big["w_g"], chips // 2, c_gu), by_owner_cols(big["w_u"], chips // 2, c_gu)],
                           axis=1)
    s_dn = by_owner_rows(big["w_down"], r_dn)
    mine = [s_in, s_out, s_gu, s_dn]
    theirs = _core_swap(mine, name="exchange_cores")
    core = lax.axis_index("c").astype(jnp.int32).reshape(1)
    chip_sums = [_pair_sum(g, o, core, name="pair_sum_" + nm)
                 for g, o, nm in zip(mine, theirs, ("w_in", "w_out", "w_gu", "w_down"))]
    p_in, p_out, p_gu, p_dn = _exchange(chip_sums, masks=[(1, 0, 0), (0, 1, 0), (1, 1, 0)],
                                        slot=lambda p: 2 * p[0] + p[1], name="exchange_chips")

    packed = _pack_small(small["norm1"], small["norm2"], small["final"], small["att"], small["hg"],
                         small["qn"], small["kn"], small["lb"], loss)
    (all_small,) = _all_gather([packed], name="gather_small_grads")

    g_w_in, d_w_in, nm_w_in, nv_w_in = _adamw(p_in, w_in[0], m_w_in[0], v_w_in[0], name="adamw_w_in")
    g_w_out, d_w_out, nm_w_out, nv_w_out = _adamw(p_out, w_out[0], m_w_out[0], v_w_out[0], name="adamw_w_out")
    g_w_gu, d_w_gu, nm_w_gu, nv_w_gu = _adamw(p_gu, w_gate_up[0], m_w_gate_up[0], v_w_gate_up[0], name="adamw_w_gu")
    g_w_dn, d_w_dn, nm_w_dn, nv_w_dn = _adamw(p_dn, w_down[0], m_w_down[0], v_w_down[0], name="adamw_w_down")

    pk = lambda vecs: _pack_small(*vecs)
    w_pk = pk([norm1_w, norm2_w, final_norm_w, att_norm_w, hg_norm_w, q_norm_w, k_norm_w])
    m_pk = pk([m_norm1_w, m_norm2_w, m_final_norm_w, m_att_norm_w, m_hg_norm_w, m_q_norm_w, m_k_norm_w])
    v_pk = pk([v_norm1_w, v_norm2_w, v_final_norm_w, v_att_norm_w, v_hg_norm_w, v_q_norm_w, v_k_norm_w])
    g_pk, d_pk, nm_pk, nv_pk = _adamw(all_small, w_pk, m_pk, v_pk, name="adamw_small")

    dlb_sum = g_pk[5:6, :].reshape(2, HG_W)
    g_lb_full = _lb_grad(dlb_sum, lb, name="lb_grad")
    g_lb_mine = lax.dynamic_slice_in_dim(g_lb_full, me * lb_cols, lb_cols, axis=1)
    g_lb_s, d_lb, nm_lb, nv_lb = _adamw(g_lb_mine[None], lb_logits.reshape(4, lb_cols),
                                        m_lb_logits.reshape(4, lb_cols), v_lb_logits.reshape(4, lb_cols),
                                        name="adamw_lb")

    loss_total = g_pk[6, 0]

    def outs(big4, lb_arr, pk_arr):
        n1, n2, fin, att, hg, qn, kn = _unpack_small(pk_arr)
        b_in, b_out, b_gu, b_dn = big4
        return [n1, b_in[None], lb_arr.reshape(2, 2, lb_cols), hg, qn, kn, att, b_out[None], n2, b_gu[None],
                b_dn[None], fin]

    return (loss_total, grad_x[None],
            *outs((g_w_in, g_w_out, g_w_gu, g_w_dn), g_lb_s, g_pk),
            *outs((d_w_in, d_w_out, d_w_gu, d_w_dn), d_lb, d_pk),
            *outs((nm_w_in, nm_w_out, nm_w_gu, nm_w_dn), nm_lb, nm_pk),
            *outs((nv_w_in, nv_w_out, nv_w_gu, nv_w_dn), nv_lb, nv_pk))
```

```python
import functools
import math

import jax
import jax.numpy as jnp
import numpy as np
from jax import lax
from jax.experimental import pallas as pl
from jax.experimental.pallas import tpu as pltpu

F32 = jnp.float32
BF16 = jnp.bfloat16

N_DEV = 8
D_MODEL = 1024
EPS = 1e-6
HG_HEADS = 4
HG_D = 128
HG_W = HG_HEADS * HG_D
CHUNK = 64
ATT_HEADS = 8
ATT_KV = 2
ATT_G = ATT_HEADS // ATT_KV
ATT_DH = 64
ATT_QW = ATT_HEADS * ATT_DH
ATT_KW = ATT_KV * ATT_DH
GRID_W = 64
ROPE_THETA = 10000.0
D_IN = 5 * HG_W + ATT_QW + 2 * ATT_KW
D_FF = 2816
ADAM_LR, ADAM_B1, ADAM_B2, ADAM_EPS, ADAM_WD, ADAM_STEP = 0.001, 0.9, 0.999, 1e-08, 0.01, 10

LANES = 128
VMEM_LIMIT = 48 * 1024 * 1024
MESH = pl.DeviceIdType.MESH
ANY = pl.BlockSpec(memory_space=pl.ANY)


def _params(sem=None):
    return pltpu.CompilerParams(dimension_semantics=sem, vmem_limit_bytes=VMEM_LIMIT)


def _pick(n, cap):
    best = None
    for t in range(LANES, cap + 1, LANES):
        if n % t == 0:
            best = t
    assert best is not None, (n, cap)
    return best


def _sigmoid(x):
    return 1.0 / (1.0 + jnp.exp(-x))


def _dot(a, b):
    return jnp.dot(a.astype(BF16), b.astype(BF16), preferred_element_type=F32)


def _dot_nt(a, b):
    return lax.dot_general(a.astype(BF16), b.astype(BF16), (((1,), (1,)), ((), ())),
                           preferred_element_type=F32)


def _dot_tn(a, b):
    return lax.dot_general(a.astype(BF16), b.astype(BF16), (((0,), (0,)), ((), ())),
                           preferred_element_type=F32)


def _mm_nn(pairs, *, name, out_dtype=F32, residual=None, tm=512, tn_cap=None, trans_b=False):
    M = pairs[0][0].shape[0]
    N = pairs[0][1].shape[0 if trans_b else 1]
    tn = N if tn_cap is None else _pick(N, tn_cap)
    n_pairs = len(pairs)
    has_res = residual is not None
    dims = (((1,), (1,)), ((), ())) if trans_b else (((1,), (0,)), ((), ()))

    def body(*refs):
        acc = None
        for i in range(n_pairs):
            d = lax.dot_general(refs[2 * i][...], refs[2 * i + 1][...], dims, preferred_element_type=F32)
            acc = d if acc is None else acc + d
        if has_res:
            acc = acc + refs[2 * n_pairs][...]
        refs[-1][...] = acc.astype(out_dtype)

    in_specs, args = [], []
    for a, b in pairs:
        k = a.shape[1]
        b_spec = pl.BlockSpec((tn, k), lambda i, j: (j, 0)) if trans_b else pl.BlockSpec((k, tn), lambda i, j: (0, j))
        in_specs += [pl.BlockSpec((tm, k), lambda i, j: (i, 0)), b_spec]
        args += [a, b]
    if has_res:
        in_specs.append(pl.BlockSpec((tm, tn), lambda i, j: (i, j)))
        args.append(residual)
    return pl.pallas_call(
        body, name=name, grid=(M // tm, N // tn), in_specs=in_specs,
        out_specs=pl.BlockSpec((tm, tn), lambda i, j: (i, j)),
        out_shape=jax.ShapeDtypeStruct((M, N), out_dtype),
        compiler_params=_params(("parallel", "arbitrary")),
    )(*args)


def _mm_tn(a, b, *, name, tma_cap=1024, tnb_cap=1024, tk=1024):
    T, Ma = a.shape
    Nb = b.shape[1]
    tma, tnb = _pick(Ma, tma_cap), _pick(Nb, tnb_cap)
    tk = min(tk, T)
    n_k = T // tk

    def body(a_ref, b_ref, o_ref, acc_ref):
        k = pl.program_id(2)

        @pl.when(k == 0)
        def _():
            acc_ref[...] = jnp.zeros_like(acc_ref)

        acc_ref[...] += lax.dot_general(a_ref[...], b_ref[...], (((0,), (0,)), ((), ())),
                                        preferred_element_type=F32)

        @pl.when(k == n_k - 1)
        def _():
            o_ref[...] = acc_ref[...]

    return pl.pallas_call(
        body, name=name, grid=(Ma // tma, Nb // tnb, n_k),
        in_specs=[pl.BlockSpec((tk, tma), lambda i, j, k: (k, i)), pl.BlockSpec((tk, tnb), lambda i, j, k: (k, j))],
        out_specs=pl.BlockSpec((tma, tnb), lambda i, j, k: (i, j)),
        out_shape=jax.ShapeDtypeStruct((Ma, Nb), F32),
        scratch_shapes=[pltpu.VMEM((tma, tnb), F32)],
        compiler_params=_params(("parallel", "parallel", "arbitrary")),
    )(a, b)


def _rms_fwd(x, w, *, name, tm=512):
    T, Dm = x.shape

    def body(x_ref, w_ref, h_ref, r_ref):
        xv = x_ref[...]
        r = lax.rsqrt(jnp.mean(xv * xv, axis=-1, keepdims=True) + EPS)
        h_ref[...] = (xv * r * w_ref[...]).astype(BF16)
        r_ref[...] = r

    return pl.pallas_call(
        body, name=name, grid=(T // tm,),
        in_specs=[pl.BlockSpec((tm, Dm), lambda i: (i, 0)), pl.BlockSpec((1, Dm), lambda i: (0, 0))],
        out_specs=[pl.BlockSpec((tm, Dm), lambda i: (i, 0)), pl.BlockSpec((tm, 1), lambda i: (i, 0))],
        out_shape=[jax.ShapeDtypeStruct((T, Dm), BF16), jax.ShapeDtypeStruct((T, 1), F32)],
        compiler_params=_params(("parallel",)),
    )(x, w)


def _rms_bwd(dh, x, r, w, dres, *, name, emit_bf16, tm=512):
    T, Dm = x.shape

    def body(dh_ref, x_ref, r_ref, w_ref, dres_ref, *outs):
        dx_ref, dw_ref = outs[0], outs[-1]

        @pl.when(pl.program_id(0) == 0)
        def _():
            dw_ref[...] = jnp.zeros_like(dw_ref)

        rv = r_ref[...]
        xh = x_ref[...] * rv
        dhv = dh_ref[...]
        dxh = dhv * w_ref[...]
        t = jnp.mean(dxh * xh, axis=-1, keepdims=True)
        dx = dres_ref[...] + rv * (dxh - xh * t)
        dx_ref[...] = dx
        if emit_bf16:
            outs[1][...] = dx.astype(BF16)
        dw_ref[...] += jnp.sum(dhv * xh, axis=0, keepdims=True)

    row = pl.BlockSpec((tm, Dm), lambda i: (i, 0))
    vec = pl.BlockSpec((1, Dm), lambda i: (0, 0))
    out_specs = [row] + ([row] if emit_bf16 else []) + [vec]
    out_shape = ([jax.ShapeDtypeStruct((T, Dm), F32)] + ([jax.ShapeDtypeStruct((T, Dm), BF16)] if emit_bf16 else [])
                 + [jax.ShapeDtypeStruct((1, Dm), F32)])
    return pl.pallas_call(
        body, name=name, grid=(T // tm,),
        in_specs=[row, row, pl.BlockSpec((tm, 1), lambda i: (i, 0)), vec, row],
        out_specs=out_specs, out_shape=out_shape,
        compiler_params=_params(("arbitrary",)),
    )(dh, x, r, w, dres)


def _loss_head(x2, target, w, *, name, tm=512):
    T, Dm = x2.shape

    def body(x_ref, t_ref, w_ref, loss_ref, dx_ref, dxb_ref, dw_ref):
        @pl.when(pl.program_id(0) == 0)
        def _():
            loss_ref[...] = jnp.zeros_like(loss_ref)
            dw_ref[...] = jnp.zeros_like(dw_ref)

        xv = x_ref[...]
        r = lax.rsqrt(jnp.mean(xv * xv, axis=-1, keepdims=True) + EPS)
        xh = xv * r
        wv = w_ref[...]
        err = xh * wv - t_ref[...]
        row_loss = jnp.mean(err * err, axis=-1, keepdims=True)
        loss_ref[...] += 0.5 * jnp.sum(row_loss, axis=0, keepdims=True)
        dy = err * (1.0 / Dm)
        dxh = dy * wv
        t = jnp.mean(dxh * xh, axis=-1, keepdims=True)
        dx = r * (dxh - xh * t)
        dx_ref[...] = dx
        dxb_ref[...] = dx.astype(BF16)
        dw_ref[...] += jnp.sum(dy * xh, axis=0, keepdims=True)

    row = pl.BlockSpec((tm, Dm), lambda i: (i, 0))
    vec = pl.BlockSpec((1, Dm), lambda i: (0, 0))
    return pl.pallas_call(
        body, name=name, grid=(T // tm,),
        in_specs=[row, row, vec],
        out_specs=[pl.BlockSpec((1, 1), lambda i: (0, 0)), row, row, vec],
        out_shape=[jax.ShapeDtypeStruct((1, 1), F32), jax.ShapeDtypeStruct((T, Dm), F32),
                   jax.ShapeDtypeStruct((T, Dm), BF16), jax.ShapeDtypeStruct((1, Dm), F32)],
        compiler_params=_params(("arbitrary",)),
    )(x2, target, w)


GLA_TB = 512
GLA_NC = GLA_TB // CHUNK


def _cumsum_rows(x, row, reverse):
    n = x.shape[0]
    s = 1
    while s < n:
        if not reverse:
            x = x + jnp.where(row >= s, pltpu.roll(x, s, 0), 0.0)
        else:
            x = x + jnp.where(row < n - s, pltpu.roll(x, n - s, 0), 0.0)
        s *= 2
    return x


def _gla_gates(uq, z, lbv):
    q = uq * _sigmoid(uq)
    sg = _sigmoid(z)
    sgn = _sigmoid(-z)
    f = lbv + (1.0 - lbv) * sg
    k = (1.0 - lbv) * sgn
    return q, sg, sgn, f, k


def _gla_decays(f, row, reverse):
    b = _cumsum_rows(jnp.log(f), row, reverse)
    if not reverse:
        bref, blast = b[CHUNK // 2 - 1:CHUNK // 2, :], b[CHUNK - 1:CHUNK, :]
    else:
        bref, blast = b[CHUNK // 2:CHUNK // 2 + 1, :], b[0:1, :]
    return b, bref, blast


def _gla_fwd(U, lb, *, f_block, reverse, name):
    T = U.shape[0]
    nb = T // GLA_TB

    def body(uq_ref, uf_ref, ui_ref, lb_ref, o_ref, st_ref, s_ref):
        @pl.when(pl.program_id(0) == 0)
        def _():
            s_ref[...] = jnp.zeros_like(s_ref)

        row = lax.broadcasted_iota(jnp.int32, (CHUNK, HG_D), 0)
        ri = lax.broadcasted_iota(jnp.int32, (CHUNK, CHUNK), 0)
        ci = lax.broadcasted_iota(jnp.int32, (CHUNK, CHUNK), 1)
        mask = (ri <= ci) if reverse else (ri >= ci)

        def chunk(j, carry):
            c = (GLA_NC - 1 - j) if reverse else j
            rows = pl.ds(pl.multiple_of(c * CHUNK, CHUNK), CHUNK)
            for h in range(HG_HEADS):
                cols = pl.ds(h * HG_D, HG_D)
                v = ui_ref[rows, cols]
                q, _, _, f, k = _gla_gates(uq_ref[rows, cols], uf_ref[rows, cols], lb_ref[:, cols])
                b, bref, blast = _gla_decays(f, row, reverse)
                s = jnp.where(mask, _dot_nt(q * jnp.exp(b - bref), k * jnp.exp(bref - b)), 0.0)
                st = s_ref[h]
                st_ref[c, h] = st
                o_ref[rows, cols] = _dot(s, v) + _dot_nt(q * jnp.exp(b), st)
                s_ref[h] = st * jnp.exp(blast) + _dot_tn(v, k * jnp.exp(blast - b))
            return carry

        lax.fori_loop(0, GLA_NC, chunk, 0)

    blk = (lambda i: nb - 1 - i) if reverse else (lambda i: i)
    ucol = lambda cb: pl.BlockSpec((GLA_TB, HG_W), lambda i: (blk(i), cb))
    return pl.pallas_call(
        body, name=name, grid=(nb,),
        in_specs=[ucol(0), ucol(f_block), ucol(3), pl.BlockSpec((1, HG_W), lambda i: (0, 0))],
        out_specs=[pl.BlockSpec((GLA_TB, HG_W), lambda i: (blk(i), 0)),
                   pl.BlockSpec((GLA_NC, HG_HEADS, HG_D, HG_D), lambda i: (blk(i), 0, 0, 0))],
        out_shape=[jax.ShapeDtypeStruct((T, HG_W), F32),
                   jax.ShapeDtypeStruct((T // CHUNK, HG_HEADS, HG_D, HG_D), F32)],
        scratch_shapes=[pltpu.VMEM((HG_HEADS, HG_D, HG_D), F32)],
        compiler_params=_params(("arbitrary",)),
    )(U, U, U, lb)


def _gla_bwd(U, lb, do, states, *, f_block, reverse, name, prev=None):
    T = U.shape[0]
    nb = T // GLA_TB
    final = prev is not None

    def body(uq_ref, uf_ref, ui_ref, lb_ref, do_ref, st_ref, *rest):
        if final:
            dqp_ref, dzp_ref, dvp_ref, dug_ref, out_ref, dlb_ref, ds_ref = rest
        else:
            dq_ref, dz_ref, dv_ref, dlb_ref, ds_ref = rest

        @pl.when(pl.program_id(0) == 0)
        def _():
            ds_ref[...] = jnp.zeros_like(ds_ref)
            dlb_ref[...] = jnp.zeros_like(dlb_ref)

        row = lax.broadcasted_iota(jnp.int32, (CHUNK, HG_D), 0)
        ri = lax.broadcasted_iota(jnp.int32, (CHUNK, CHUNK), 0)
        ci = lax.broadcasted_iota(jnp.int32, (CHUNK, CHUNK), 1)
        mask = (ri <= ci) if reverse else (ri >= ci)

        def chunk(j, carry):
            c = j if reverse else (GLA_NC - 1 - j)
            rows = pl.ds(pl.multiple_of(c * CHUNK, CHUNK), CHUNK)
            for h in range(HG_HEADS):
                cols = pl.ds(h * HG_D, HG_D)
                v = ui_ref[rows, cols]
                lbv = lb_ref[:, cols]
                uq = uq_ref[rows, cols]
                q, sg, sgn, f, k = _gla_gates(uq, uf_ref[rows, cols], lbv)
                b, bref, blast = _gla_decays(f, row, reverse)
                eq, ek, eb, el, dec = (jnp.exp(b - bref), jnp.exp(bref - b), jnp.exp(b), jnp.exp(blast - b),
                                       jnp.exp(blast))
                qin, kin, qb, klast = q * eq, k * ek, q * eb, k * el
                dov = do_ref[rows, cols]
                st = st_ref[c, h]
                dst = ds_ref[h]
                p = jnp.where(mask, _dot_nt(qin, kin), 0.0)
                dp = jnp.where(mask, _dot_nt(dov, v), 0.0)
                dqin = _dot(dp, kin)
                dkin = _dot_tn(dp, qin)
                dv = _dot_tn(p, dov) + _dot_nt(klast, dst)
                dqb = _dot(dov, st)
                dklast = _dot(v, dst)
                ds_ref[h] = _dot_tn(dov, qb) + dst * dec
                db = dqin * qin - dkin * kin + dqb * qb - dklast * klast
                extra = (jnp.sum(dklast * klast, axis=0, keepdims=True)
                         + dec * jnp.sum(st * dst, axis=0, keepdims=True))
                dg = _cumsum_rows(db, row, not reverse) + extra
                dq = dqin * eq + dqb * eb
                dk = dkin * ek + dklast * el
                dfk = dg / f - dk
                dz = (dfk * (1.0 - lbv) * sg * sgn).astype(BF16)
                dlb_ref[:, cols] += jnp.sum(dfk * sgn, axis=0, keepdims=True)
                if final:
                    sq = _sigmoid(uq)
                    col = lambda blk: pl.ds(blk * HG_W + h * HG_D, HG_D)
                    out_ref[rows, col(0)] = ((dq + dqp_ref[rows, cols]) * (sq * (1.0 + uq * (1.0 - sq)))).astype(BF16)
                    out_ref[rows, col(1)] = dzp_ref[rows, cols]
                    out_ref[rows, col(2)] = dz
                    out_ref[rows, col(3)] = (dv + dvp_ref[rows, cols]).astype(BF16)
                    out_ref[rows, col(4)] = dug_ref[rows, cols]
                else:
                    dq_ref[rows, cols] = dq
                    dz_ref[rows, cols] = dz
                    dv_ref[rows, cols] = dv
            return carry

        lax.fori_loop(0, GLA_NC, chunk, 0)

    blk = (lambda i: i) if reverse else (lambda i: nb - 1 - i)
    ucol = lambda cb: pl.BlockSpec((GLA_TB, HG_W), lambda i: (blk(i), cb))
    tok = pl.BlockSpec((GLA_TB, HG_W), lambda i: (blk(i), 0))
    vec = pl.BlockSpec((1, HG_W), lambda i: (0, 0))
    in_specs = [ucol(0), ucol(f_block), ucol(3), vec, tok,
                pl.BlockSpec((GLA_NC, HG_HEADS, HG_D, HG_D), lambda i: (blk(i), 0, 0, 0))]
    vec_shape = jax.ShapeDtypeStruct((1, HG_W), F32)
    if final:
        in_specs += [tok] * 4
        out_specs = [pl.BlockSpec((GLA_TB, 5 * HG_W), lambda i: (blk(i), 0)), vec]
        out_shape = [jax.ShapeDtypeStruct((T, 5 * HG_W), BF16), vec_shape]
    else:
        out_specs = [tok, tok, tok, vec]
        out_shape = [jax.ShapeDtypeStruct((T, HG_W), F32), jax.ShapeDtypeStruct((T, HG_W), BF16),
                     jax.ShapeDtypeStruct((T, HG_W), F32), vec_shape]
    return pl.pallas_call(
        body, name=name, grid=(nb,), in_specs=in_specs, out_specs=out_specs, out_shape=out_shape,
        scratch_shapes=[pltpu.VMEM((HG_HEADS, HG_D, HG_D), F32)],
        compiler_params=_params(("arbitrary",)),
    )(U, U, U, lb, do, states, *(prev if final else ()))


def _hg_post_fwd(o_f, o_b, U, w, *, name, tm=512):
    T = o_f.shape[0]

    def body(of_ref, ob_ref, ug_ref, w_ref, out_ref):
        wv = w_ref[...]
        for h in range(HG_HEADS):
            cols = pl.ds(h * HG_D, HG_D)
            o = of_ref[:, cols] + ob_ref[:, cols]
            r = lax.rsqrt(jnp.mean(o * o, axis=-1, keepdims=True) + EPS)
            ug = ug_ref[:, cols]
            out_ref[:, cols] = (o * r * wv * (ug * _sigmoid(ug))).astype(BF16)

    tok = pl.BlockSpec((tm, HG_W), lambda i: (i, 0))
    return pl.pallas_call(
        body, name=name, grid=(T // tm,),
        in_specs=[tok, tok, pl.BlockSpec((tm, HG_W), lambda i: (i, 4)), pl.BlockSpec((1, HG_D), lambda i: (0, 0))],
        out_specs=tok, out_shape=jax.ShapeDtypeStruct((T, HG_W), BF16),
        compiler_params=_params(("parallel",)),
    )(o_f, o_b, U, w)


def _hg_post_bwd(dmix, o_f, o_b, U, w, *, name, tm=512):
    T = o_f.shape[0]

    def body(dm_ref, of_ref, ob_ref, ug_ref, w_ref, do_ref, dug_ref, dw_ref):
        @pl.when(pl.program_id(0) == 0)
        def _():
            dw_ref[...] = jnp.zeros_like(dw_ref)

        wv = w_ref[...]
        for h in range(HG_HEADS):
            cols = pl.ds(h * HG_D, HG_D)
            o = of_ref[:, cols] + ob_ref[:, cols]
            r = lax.rsqrt(jnp.mean(o * o, axis=-1, keepdims=True) + EPS)
            xh = o * r
            ug = ug_ref[:, cols]
            sg = _sigmoid(ug)
            dm = dm_ref[:, cols]
            dn = dm * (ug * sg)
            dug_ref[:, cols] = (dm * (xh * wv) * (sg * (1.0 + ug * (1.0 - sg)))).astype(BF16)
            dxh = dn * wv
            t = jnp.mean(dxh * xh, axis=-1, keepdims=True)
            do_ref[:, cols] = r * (dxh - xh * t)
            dw_ref[:, cols] += jnp.sum(dn * xh, axis=0, keepdims=True)

    tok = pl.BlockSpec((tm, HG_W), lambda i: (i, 0))
    vec = pl.BlockSpec((1, HG_W), lambda i: (0, 0))
    return pl.pallas_call(
        body, name=name, grid=(T // tm,),
        in_specs=[tok, tok, tok, pl.BlockSpec((tm, HG_W), lambda i: (i, 4)), pl.BlockSpec((1, HG_D), lambda i: (0, 0))],
        out_specs=[tok, tok, vec],
        out_shape=[jax.ShapeDtypeStruct((T, HG_W), F32), jax.ShapeDtypeStruct((T, HG_W), BF16),
                   jax.ShapeDtypeStruct((1, HG_W), F32)],
        compiler_params=_params(("arbitrary",)),
    )(dmix, o_f, o_b, U, w)


def _rope_tables(T):
    rows = T // GRID_W
    row = np.repeat(np.arange(rows), GRID_W).astype(np.float32)
    col = np.tile(np.arange(GRID_W), rows).astype(np.float32)
    axis_dim = ATT_DH // 2
    freqs = (np.float32(ROPE_THETA) ** (-np.arange(0, axis_dim, 2, dtype=np.float32) / np.float32(axis_dim))
             ).astype(np.float32)
    ang = np.concatenate([row[:, None] * freqs, col[:, None] * freqs], axis=-1).astype(np.float32)
    cos, sin = np.cos(ang), np.sin(ang)
    c = np.repeat(cos, 2, axis=-1)
    s = np.stack([-sin, sin], axis=-1).reshape(T, ATT_DH)
    return jnp.asarray(np.tile(c, (1, 2)), F32), jnp.asarray(np.tile(s, (1, 2)), F32)


def _head_blockdiag(width):
    shift = ATT_DH.bit_length() - 1
    ri = jnp.right_shift(lax.broadcasted_iota(jnp.int32, (width, width), 0), shift)
    ci = jnp.right_shift(lax.broadcasted_iota(jnp.int32, (width, width), 1), shift)
    return jnp.where(ri == ci, 1.0, 0.0).astype(BF16)


def _head_sum(x, bd):
    hi = x.astype(BF16)
    lo = (x - hi.astype(F32)).astype(BF16)
    return jnp.dot(hi, bd, preferred_element_type=F32) + jnp.dot(lo, bd, preferred_element_type=F32)


def _pair_swap(x, even):
    n = x.shape[-1]
    return jnp.where(even, pltpu.roll(x, n - 1, 1), pltpu.roll(x, 1, 1))


def _att_prep_fwd(U, cos, sin, qw, kw, *, name, tm=512):
    T = U.shape[0]
    scale = ATT_DH ** -0.5

    def body(aq_ref, ak_ref, av_ref, c_ref, s_ref, qw_ref, kw_ref, q_ref, k_ref, v_ref):
        bd = _head_blockdiag(ATT_QW)
        c2, s2 = c_ref[...], s_ref[...]
        c8, s8 = jnp.tile(c2, (1, 4)), jnp.tile(s2, (1, 4))

        def norm_rope(x, w, c, s, bdm):
            r = lax.rsqrt(_head_sum(x * x, bdm) * (1.0 / ATT_DH) + EPS)
            y = x * r * w
            even = (lax.broadcasted_iota(jnp.int32, y.shape, 1) & 1) == 0
            return y * c + _pair_swap(y, even) * s

        q_ref[...] = (norm_rope(aq_ref[...], qw_ref[...], c8, s8, bd) * scale).astype(BF16)
        k_ref[...] = norm_rope(ak_ref[...], kw_ref[...], c2, s2, bd[:ATT_KW, :ATT_KW]).astype(BF16)
        v_ref[...] = av_ref[...].astype(BF16)

    kv_spec = pl.BlockSpec((tm, ATT_KW), lambda i: (i, 0))
    return pl.pallas_call(
        body, name=name, grid=(T // tm,),
        in_specs=[pl.BlockSpec((tm, ATT_QW), lambda i: (i, 5)),
                  pl.BlockSpec((tm, ATT_KW), lambda i: (i, 24)), pl.BlockSpec((tm, ATT_KW), lambda i: (i, 25)),
                  kv_spec, kv_spec,
                  pl.BlockSpec((1, ATT_QW), lambda i: (0, 0)), pl.BlockSpec((1, ATT_KW), lambda i: (0, 0))],
        out_specs=[pl.BlockSpec((tm, ATT_QW), lambda i: (i, 0)), kv_spec, kv_spec],
        out_shape=[jax.ShapeDtypeStruct((T, ATT_QW), BF16), jax.ShapeDtypeStruct((T, ATT_KW), BF16),
                   jax.ShapeDtypeStruct((T, ATT_KW), BF16)],
        compiler_params=_params(("parallel",)),
    )(U, U, U, cos, sin, qw, kw)


def _att_prep_bwd(U, dq, dk, cos, sin, qw, kw, *, name, tm=512):
    T = U.shape[0]
    scale = ATT_DH ** -0.5

    def body(aq_ref, ak_ref, dq_ref, dk_ref, c_ref, s_ref, qw_ref, kw_ref, daq_ref, dak_ref, dqw_ref, dkw_ref):
        @pl.when(pl.program_id(0) == 0)
        def _():
            dqw_ref[...] = jnp.zeros_like(dqw_ref)
            dkw_ref[...] = jnp.zeros_like(dkw_ref)

        bd = _head_blockdiag(ATT_QW)
        c2, s2 = c_ref[...], s_ref[...]
        c8, s8 = jnp.tile(c2, (1, 4)), jnp.tile(s2, (1, 4))

        def bwd(x, dy, w, c, s, bdm):
            even = (lax.broadcasted_iota(jnp.int32, x.shape, 1) & 1) == 0
            dn = dy * c - _pair_swap(dy, even) * s
            r = lax.rsqrt(_head_sum(x * x, bdm) * (1.0 / ATT_DH) + EPS)
            xh = x * r
            dxh = dn * w
            t = _head_sum(dxh * xh, bdm) * (1.0 / ATT_DH)
            return r * (dxh - xh * t), jnp.sum(dn * xh, axis=0, keepdims=True)

        da, dw = bwd(aq_ref[...], dq_ref[...] * scale, qw_ref[...], c8, s8, bd)
        daq_ref[...] = da
        dqw_ref[...] += dw
        da, dw = bwd(ak_ref[...], dk_ref[...], kw_ref[...], c2, s2, bd[:ATT_KW, :ATT_KW])
        dak_ref[...] = da
        dkw_ref[...] += dw

    q_spec = pl.BlockSpec((tm, ATT_QW), lambda i: (i, 0))
    kv_spec = pl.BlockSpec((tm, ATT_KW), lambda i: (i, 0))
    qv = pl.BlockSpec((1, ATT_QW), lambda i: (0, 0))
    kv = pl.BlockSpec((1, ATT_KW), lambda i: (0, 0))
    return pl.pallas_call(
        body, name=name, grid=(T // tm,),
        in_specs=[pl.BlockSpec((tm, ATT_QW), lambda i: (i, 5)), pl.BlockSpec((tm, ATT_KW), lambda i: (i, 24)),
                  q_spec, kv_spec, kv_spec, kv_spec, qv, kv],
        out_specs=[q_spec, kv_spec, qv, kv],
        out_shape=[jax.ShapeDtypeStruct((T, ATT_QW), F32), jax.ShapeDtypeStruct((T, ATT_KW), F32),
                   jax.ShapeDtypeStruct((1, ATT_QW), F32), jax.ShapeDtypeStruct((1, ATT_KW), F32)],
        compiler_params=_params(("arbitrary",)),
    )(U, U, dq, dk, cos, sin, qw, kw)


FA_TQ = 256
FA_TK = 256
FA_SW = 128


def _fa_tiles(T):
    tq, tk = min(FA_TQ, T), min(FA_TK, T)
    return tq, tk, T // tq, T // tk


def _to_fa_cols(a, T):
    tq, _, nq, _ = _fa_tiles(T)
    return a.reshape(nq, tq, ATT_KV, ATT_G, ATT_DH).transpose(2, 0, 4, 3, 1).reshape(ATT_KV, nq, ATT_DH, ATT_G * tq)


def _to_fa_rows(a, T):
    tq, _, nq, _ = _fa_tiles(T)
    return a.reshape(nq, tq, ATT_KV, ATT_G, ATT_DH).transpose(2, 0, 3, 1, 4).reshape(ATT_KV, nq, ATT_G * tq, ATT_DH)


def _from_fa_cols(a, T):
    tq, _, nq, _ = _fa_tiles(T)
    return a.reshape(ATT_KV, nq, ATT_DH, ATT_G, tq).transpose(1, 4, 0, 3, 2).reshape(T, ATT_QW)


def _kv_rows(a, T):
    _, tk, _, n_k = _fa_tiles(T)
    return a.reshape(n_k, tk, ATT_KV, ATT_DH).transpose(2, 0, 1, 3)


def _kv_cols(a, T):
    _, tk, _, n_k = _fa_tiles(T)
    return a.reshape(n_k, tk, ATT_KV, ATT_DH).transpose(2, 0, 3, 1)


def _flash_fwd(q_c, k_r, v_c, *, name):
    _, nq, _, R = q_c.shape
    _, n_k, tk, _ = k_r.shape

    def body(q_ref, k_ref, v_ref, o_ref, lse_ref):
        for st in range(R // FA_SW):
            lanes = pl.ds(st * FA_SW, FA_SW)
            qv = q_ref[0, 0, :, lanes]

            def step(j, carry):
                m, l, acc = carry
                s = jnp.dot(k_ref[0, j], qv, preferred_element_type=F32)
                m_new = jnp.maximum(m, jnp.max(s, axis=0, keepdims=True))
                alpha = jnp.exp(m - m_new)
                p = jnp.exp(s - m_new)
                l = alpha * l + jnp.sum(p, axis=0, keepdims=True)
                acc = alpha * acc + jnp.dot(v_ref[0, j], p.astype(BF16), preferred_element_type=F32)
                return m_new, l, acc

            m, l, acc = lax.fori_loop(0, n_k, step, (jnp.full((1, FA_SW), -jnp.inf, F32), jnp.zeros((1, FA_SW), F32),
                                                     jnp.zeros((ATT_DH, FA_SW), F32)))
            o_ref[0, 0, :, lanes] = acc / l
            lse_ref[0, 0, :, lanes] = m + jnp.log(l)

    qspec = pl.BlockSpec((1, 1, ATT_DH, R), lambda h, i: (h, i, 0, 0))
    return pl.pallas_call(
        body, name=name, grid=(ATT_KV, nq),
        in_specs=[qspec, pl.BlockSpec((1, n_k, tk, ATT_DH), lambda h, i: (h, 0, 0, 0)),
                  pl.BlockSpec((1, n_k, ATT_DH, tk), lambda h, i: (h, 0, 0, 0))],
        out_specs=[qspec, pl.BlockSpec((1, 1, 1, R), lambda h, i: (h, i, 0, 0))],
        out_shape=[jax.ShapeDtypeStruct((ATT_KV, nq, ATT_DH, R), F32), jax.ShapeDtypeStruct((ATT_KV, nq, 1, R), F32)],
        compiler_params=_params(("parallel", "parallel")),
    )(q_c, k_r, v_c)


def _flash_bwd(q_c, q_r, k_r, k_c, v_r, do_c, do_r, o_c, lse, *, name):
    _, nq, _, R = q_c.shape
    _, n_k, tk, _ = k_r.shape

    def body(qc_ref, qr_ref, kr_ref, kc_ref, vr_ref, doc_ref, dor_ref, oc_ref, lse_ref, dq_ref, dk_ref, dv_ref,
             acc_ref):
        @pl.when(pl.program_id(1) == 0)
        def _():
            dk_ref[...] = jnp.zeros_like(dk_ref)
            dv_ref[...] = jnp.zeros_like(dv_ref)

        qc, doc = qc_ref[0, 0], doc_ref[0, 0]
        qr, dor = qr_ref[0, 0], dor_ref[0, 0]
        delta = jnp.sum(doc.astype(F32) * oc_ref[0, 0], axis=0, keepdims=True)
        lsev = lse_ref[0, 0]
        acc_ref[...] = jnp.zeros_like(acc_ref)

        def step(j, carry):
            s = jnp.dot(kr_ref[0, j], qc, preferred_element_type=F32)
            p = jnp.exp(s - lsev)
            dp = jnp.dot(vr_ref[0, j], doc, preferred_element_type=F32)
            ds = (p * (dp - delta)).astype(BF16)
            acc_ref[...] += jnp.dot(kc_ref[0, j], ds, preferred_element_type=F32)
            dk_ref[0, j] += jnp.dot(ds, qr, preferred_element_type=F32)
            dv_ref[0, j] += jnp.dot(p.astype(BF16), dor, preferred_element_type=F32)
            return carry

        lax.fori_loop(0, n_k, step, 0)
        dq_ref[0, 0] = acc_ref[...]

    cspec = pl.BlockSpec((1, 1, ATT_DH, R), lambda h, i: (h, i, 0, 0))
    rspec = pl.BlockSpec((1, 1, R, ATT_DH), lambda h, i: (h, i, 0, 0))
    krspec = pl.BlockSpec((1, n_k, tk, ATT_DH), lambda h, i: (h, 0, 0, 0))
    kcspec = pl.BlockSpec((1, n_k, ATT_DH, tk), lambda h, i: (h, 0, 0, 0))
    return pl.pallas_call(
        body, name=name, grid=(ATT_KV, nq),
        in_specs=[cspec, rspec, krspec, kcspec, krspec, cspec, rspec, cspec,
                  pl.BlockSpec((1, 1, 1, R), lambda h, i: (h, i, 0, 0))],
        out_specs=[cspec, krspec, krspec],
        out_shape=[jax.ShapeDtypeStruct((ATT_KV, nq, ATT_DH, R), F32),
                   jax.ShapeDtypeStruct((ATT_KV, n_k, tk, ATT_DH), F32),
                   jax.ShapeDtypeStruct((ATT_KV, n_k, tk, ATT_DH), F32)],
        scratch_shapes=[pltpu.VMEM((ATT_DH, R), F32)],
        compiler_params=_params(("parallel", "arbitrary")),
    )(q_c, q_r, k_r, k_c, v_r, do_c, do_r, o_c, lse)


def _att_post_fwd(o, w, *, name, tm=512):
    T = o.shape[0]

    def body(o_ref, w_ref, out_ref):
        ov = o_ref[...]
        r = lax.rsqrt(jnp.mean(ov * ov, axis=-1, keepdims=True) + EPS)
        out_ref[...] = (ov * r * w_ref[...]).astype(BF16)

    tok = pl.BlockSpec((tm, ATT_QW), lambda i: (i, 0))
    return pl.pallas_call(
        body, name=name, grid=(T // tm,),
        in_specs=[tok, pl.BlockSpec((1, ATT_QW), lambda i: (0, 0))],
        out_specs=tok, out_shape=jax.ShapeDtypeStruct((T, ATT_QW), BF16),
        compiler_params=_params(("parallel",)),
    )(o, w)


def _att_post_bwd(dmix, o, w, *, name, tm=512):
    T = o.shape[0]

    def body(dm_ref, o_ref, w_ref, do_ref, dw_ref):
        @pl.when(pl.program_id(0) == 0)
        def _():
            dw_ref[...] = jnp.zeros_like(dw_ref)

        ov = o_ref[...]
        r = lax.rsqrt(jnp.mean(ov * ov, axis=-1, keepdims=True) + EPS)
        xh = ov * r
        dm = dm_ref[...]
        dxh = dm * w_ref[...]
        t = jnp.mean(dxh * xh, axis=-1, keepdims=True)
        do_ref[...] = (r * (dxh - xh * t)).astype(BF16)
        dw_ref[...] += jnp.sum(dm * xh, axis=0, keepdims=True)

    tok = pl.BlockSpec((tm, ATT_QW), lambda i: (i, 0))
    vec = pl.BlockSpec((1, ATT_QW), lambda i: (0, 0))
    return pl.pallas_call(
        body, name=name, grid=(T // tm,),
        in_specs=[pl.BlockSpec((tm, ATT_QW), lambda i: (i, 1)), tok, vec],
        out_specs=[tok, vec],
        out_shape=[jax.ShapeDtypeStruct((T, ATT_QW), BF16), jax.ShapeDtypeStruct((1, ATT_QW), F32)],
        compiler_params=_params(("arbitrary",)),
    )(dmix, o, w)


FA_HP = ATT_KV * ATT_DH
FA_TK_FWD = 512
FA_TK_BWD = 512


def _cols_from_tokens(x, kv):
    w = ATT_G * ATT_DH
    xt = x[:, kv * w:(kv + 1) * w].T
    return jnp.concatenate([xt[g * ATT_DH:(g + 1) * ATT_DH, :] for g in range(ATT_G)], axis=1)


def _tokens_from_cols(c):
    tq = c.shape[1] // ATT_G
    return jnp.concatenate([c[:, g * tq:(g + 1) * tq] for g in range(ATT_G)], axis=0).T


def _store_padded_cols(ref, x, norm_ref=None):
    for kv in range(ATT_KV):
        cols = _cols_from_tokens(x, kv).astype(BF16)
        ref[kv, 0, kv * ATT_DH:(kv + 1) * ATT_DH, :] = cols
        ref[kv, 0, (1 - kv) * ATT_DH:(2 - kv) * ATT_DH, :] = jnp.zeros_like(cols)
        if norm_ref is not None:
            cf = cols.astype(F32)
            norm_ref[kv, 0] = jnp.sqrt(jnp.sum(cf * cf, axis=0, keepdims=True))


def _att_prep_fwd2(U, cos, sin, qw, kw, *, name):
    T = U.shape[0]
    tm = min(FA_TQ, T)
    R = ATT_G * tm
    scale = ATT_DH ** -0.5

    def body(aq_ref, ak_ref, av_ref, c_ref, s_ref, qw_ref, kw_ref, q_ref, k_ref, v_ref, qn_ref, kmax_ref):
        @pl.when(pl.program_id(0) == 0)
        def _():
            kmax_ref[...] = jnp.zeros_like(kmax_ref)

        bd = _head_blockdiag(ATT_QW)
        c2, s2 = c_ref[...], s_ref[...]
        c8, s8 = jnp.tile(c2, (1, 4)), jnp.tile(s2, (1, 4))

        def norm_rope(x, w, c, s, bdm):
            r = lax.rsqrt(_head_sum(x * x, bdm) * (1.0 / ATT_DH) + EPS)
            y = x * r * w
            even = (lax.broadcasted_iota(jnp.int32, y.shape, 1) & 1) == 0
            return y * c + _pair_swap(y, even) * s

        _store_padded_cols(q_ref, norm_rope(aq_ref[...], qw_ref[...], c8, s8, bd) * scale, qn_ref)
        kb = norm_rope(ak_ref[...], kw_ref[...], c2, s2, bd[:ATT_KW, :ATT_KW]).astype(BF16)
        k_ref[...] = kb
        kf = kb.astype(F32)
        ksq = _head_sum(kf * kf, bd[:ATT_KW, :ATT_KW])
        kmax_ref[...] = jnp.maximum(kmax_ref[...], jnp.max(ksq, axis=0, keepdims=True))
        v_ref[...] = av_ref[...].astype(BF16)

    kv_spec = pl.BlockSpec((tm, ATT_KW), lambda i: (i, 0))
    return pl.pallas_call(
        body, name=name, grid=(T // tm,),
        in_specs=[pl.BlockSpec((tm, ATT_QW), lambda i: (i, 5)),
                  pl.BlockSpec((tm, ATT_KW), lambda i: (i, 24)), pl.BlockSpec((tm, ATT_KW), lambda i: (i, 25)),
                  kv_spec, kv_spec,
                  pl.BlockSpec((1, ATT_QW), lambda i: (0, 0)), pl.BlockSpec((1, ATT_KW), lambda i: (0, 0))],
        out_specs=[pl.BlockSpec((ATT_KV, 1, FA_HP, R), lambda i: (0, i, 0, 0)), kv_spec, kv_spec,
                   pl.BlockSpec((ATT_KV, 1, 1, R), lambda i: (0, i, 0, 0)), pl.BlockSpec((1, ATT_KW), lambda i: (0, 0))],
        out_shape=[jax.ShapeDtypeStruct((ATT_KV, T // tm, FA_HP, R), BF16),
                   jax.ShapeDtypeStruct((T, ATT_KW), BF16), jax.ShapeDtypeStruct((T, ATT_KW), BF16),
                   jax.ShapeDtypeStruct((ATT_KV, T // tm, 1, R), F32), jax.ShapeDtypeStruct((1, ATT_KW), F32)],
        compiler_params=_params(("arbitrary",)),
    )(U, U, U, cos, sin, qw, kw)


def _att_prep_bwd2(U, dq_c, dk, dv, cos, sin, qw, kw, *, name):
    T = U.shape[0]
    tm = min(FA_TQ, T)
    R = ATT_G * tm
    scale = ATT_DH ** -0.5

    def body(aq_ref, ak_ref, dq_ref, dk_ref, dv_ref, c_ref, s_ref, qw_ref, kw_ref, out_ref, dqw_ref, dkw_ref):
        @pl.when(pl.program_id(0) == 0)
        def _():
            dqw_ref[...] = jnp.zeros_like(dqw_ref)
            dkw_ref[...] = jnp.zeros_like(dkw_ref)

        bd = _head_blockdiag(ATT_QW)
        c2, s2 = c_ref[...], s_ref[...]
        c8, s8 = jnp.tile(c2, (1, 4)), jnp.tile(s2, (1, 4))

        def bwd(x, dy, w, c, s, bdm):
            even = (lax.broadcasted_iota(jnp.int32, x.shape, 1) & 1) == 0
            dn = dy * c - _pair_swap(dy, even) * s
            r = lax.rsqrt(_head_sum(x * x, bdm) * (1.0 / ATT_DH) + EPS)
            xh = x * r
            dxh = dn * w
            t = _head_sum(dxh * xh, bdm) * (1.0 / ATT_DH)
            return r * (dxh - xh * t), jnp.sum(dn * xh, axis=0, keepdims=True)

        dq = jnp.concatenate([_tokens_from_cols(dq_ref[kv, 0]) for kv in range(ATT_KV)], axis=1)
        da, dw = bwd(aq_ref[...], dq * scale, qw_ref[...], c8, s8, bd)
        out_ref[:, 0:ATT_QW] = da.astype(BF16)
        dqw_ref[...] += dw
        da, dw = bwd(ak_ref[...], dk_ref[...], kw_ref[...], c2, s2, bd[:ATT_KW, :ATT_KW])
        out_ref[:, ATT_QW:ATT_QW + ATT_KW] = da.astype(BF16)
        dkw_ref[...] += dw
        out_ref[:, ATT_QW + ATT_KW:ATT_QW + 2 * ATT_KW] = dv_ref[...].astype(BF16)

    kv_spec = pl.BlockSpec((tm, ATT_KW), lambda i: (i, 0))
    qv = pl.BlockSpec((1, ATT_QW), lambda i: (0, 0))
    kv = pl.BlockSpec((1, ATT_KW), lambda i: (0, 0))
    w_att = ATT_QW + 2 * ATT_KW
    return pl.pallas_call(
        body, name=name, grid=(T // tm,),
        in_specs=[pl.BlockSpec((tm, ATT_QW), lambda i: (i, 5)), pl.BlockSpec((tm, ATT_KW), lambda i: (i, 24)),
                  pl.BlockSpec((ATT_KV, 1, ATT_DH, R), lambda i: (0, i, 0, 0)), kv_spec, kv_spec, kv_spec, kv_spec, qv, kv],
        out_specs=[pl.BlockSpec((tm, w_att), lambda i: (i, 0)), qv, kv],
        out_shape=[jax.ShapeDtypeStruct((T, w_att), BF16),
                   jax.ShapeDtypeStruct((1, ATT_QW), F32), jax.ShapeDtypeStruct((1, ATT_KW), F32)],
        compiler_params=_params(("arbitrary",)),
    )(U, U, dq_c, dk, dv, cos, sin, qw, kw)


def _pick_head(x, kv):
    return jnp.where(kv == 0, x[0:ATT_DH, :], x[ATT_DH:FA_HP, :])


def _flash_fwd2(q_c, k, v, *, name):
    _, nq, _, R = q_c.shape
    T = k.shape[0]
    tk = min(FA_TK_FWD, T)
    n_k = T // tk

    def body(q_ref, k_ref, v_ref, o_ref, lse_ref, acc_ref):
        kv = pl.program_id(0)
        qv = q_ref[0, 0]
        acc_ref[...] = jnp.zeros_like(acc_ref)

        def step(j, carry):
            m, l = carry
            s = jnp.dot(k_ref[j], qv, preferred_element_type=F32)
            m_new = jnp.maximum(m, jnp.max(s, axis=0, keepdims=True))
            alpha = jnp.exp(m - m_new)
            p = jnp.exp(s - m_new)
            l = alpha * l + jnp.sum(p, axis=0, keepdims=True)
            pv = lax.dot_general(v_ref[j], p.astype(BF16), (((0,), (0,)), ((), ())), preferred_element_type=F32)
            acc_ref[...] = alpha * acc_ref[...] + _pick_head(pv, kv)
            return m_new, l

        m, l = lax.fori_loop(0, n_k, step, (jnp.full((1, R), -jnp.inf, F32), jnp.zeros((1, R), F32)))
        o_ref[0, 0] = acc_ref[...] / l
        lse_ref[0, 0] = m + jnp.log(l)

    kspec = pl.BlockSpec((n_k, tk, FA_HP), lambda h, i: (0, 0, 0))
    return pl.pallas_call(
        body, name=name, grid=(ATT_KV, nq),
        in_specs=[pl.BlockSpec((1, 1, FA_HP, R), lambda h, i: (h, i, 0, 0)), kspec, kspec],
        out_specs=[pl.BlockSpec((1, 1, ATT_DH, R), lambda h, i: (h, i, 0, 0)),
                   pl.BlockSpec((1, 1, 1, R), lambda h, i: (h, i, 0, 0))],
        out_shape=[jax.ShapeDtypeStruct((ATT_KV, nq, ATT_DH, R), F32), jax.ShapeDtypeStruct((ATT_KV, nq, 1, R), F32)],
        scratch_shapes=[pltpu.VMEM((ATT_DH, R), F32)],
        compiler_params=_params(("parallel", "parallel")),
    )(q_c, k.reshape(n_k, tk, FA_HP), v.reshape(n_k, tk, FA_HP))


FA_BOUND_MAX = 40.0
FA_TK_FAST = 512


def _flash_fwd_bounded(q_c, k, v, m_c, *, name):
    _, nq, _, R = q_c.shape
    T = k.shape[0]
    tk = min(FA_TK_FAST, T)
    n_k = T // tk

    def body(q_ref, k_ref, v_ref, m_ref, o_ref, lse_ref, acc_ref):
        kv = pl.program_id(0)
        qv = q_ref[0, 0]
        m = m_ref[0, 0]
        acc_ref[...] = jnp.zeros_like(acc_ref)

        def step(j, l8):
            s = jnp.dot(k_ref[j], qv, preferred_element_type=F32)
            p = jnp.exp(s - m)
            l8 = l8 + jnp.sum(p.reshape(tk // 8, 8, R), axis=0)
            acc_ref[...] += lax.dot_general(v_ref[j], p.astype(BF16), (((0,), (0,)), ((), ())),
                                            preferred_element_type=F32)
            return l8

        l8 = lax.fori_loop(0, n_k, step, jnp.zeros((8, R), F32))
        l = jnp.sum(l8, axis=0, keepdims=True)
        o_ref[0, 0] = _pick_head(acc_ref[...], kv) / l
        lse_ref[0, 0] = m + jnp.log(l)

    kspec = pl.BlockSpec((n_k, tk, FA_HP), lambda h, i: (0, 0, 0))
    vspec = pl.BlockSpec((1, 1, 1, R), lambda h, i: (h, i, 0, 0))
    return pl.pallas_call(
        body, name=name, grid=(ATT_KV, nq),
        in_specs=[pl.BlockSpec((1, 1, FA_HP, R), lambda h, i: (h, i, 0, 0)), kspec, kspec, vspec],
        out_specs=[pl.BlockSpec((1, 1, ATT_DH, R), lambda h, i: (h, i, 0, 0)), vspec],
        out_shape=[jax.ShapeDtypeStruct((ATT_KV, nq, ATT_DH, R), F32), jax.ShapeDtypeStruct((ATT_KV, nq, 1, R), F32)],
        scratch_shapes=[pltpu.VMEM((FA_HP, R), F32)],
        compiler_params=_params(("parallel", "parallel")),
    )(q_c, k.reshape(n_k, tk, FA_HP), v.reshape(n_k, tk, FA_HP), m_c)


def _flash_bwd2(q_c, k, v, do_c, lse, delta, *, name):
    _, nq, _, R = q_c.shape
    T = k.shape[0]
    tk = min(FA_TK_BWD, T)
    n_k = T // tk

    def body(qc_ref, k_ref, v_ref, doc_ref, lse_ref, delta_ref, dq_ref, dk_ref, dv_ref, acc_ref):
        kv = pl.program_id(0)

        @pl.when((kv == 0) & (pl.program_id(1) == 0))
        def _():
            dk_ref[...] = jnp.zeros_like(dk_ref)
            dv_ref[...] = jnp.zeros_like(dv_ref)

        qc, doc = qc_ref[0, 0], doc_ref[0, 0]
        lsev, delta = lse_ref[0, 0], delta_ref[0, 0]
        acc_ref[...] = jnp.zeros_like(acc_ref)

        def step(j, carry):
            kb = k_ref[j]
            s = jnp.dot(kb, qc, preferred_element_type=F32)
            p = jnp.exp(s - lsev)
            dp = jnp.dot(v_ref[j], doc, preferred_element_type=F32)
            ds = (p * (dp - delta)).astype(BF16)
            acc_ref[...] += lax.dot_general(kb, ds, (((0,), (0,)), ((), ())), preferred_element_type=F32)
            dk_ref[j] += lax.dot_general(ds, qc, (((1,), (1,)), ((), ())), preferred_element_type=F32)
            dv_ref[j] += lax.dot_general(p.astype(BF16), doc, (((1,), (1,)), ((), ())), preferred_element_type=F32)
            return carry

        lax.fori_loop(0, n_k, step, 0)
        dq_ref[0, 0] = _pick_head(acc_ref[...], kv)

    cspec = pl.BlockSpec((1, 1, FA_HP, R), lambda h, i: (h, i, 0, 0))
    vspec = pl.BlockSpec((1, 1, 1, R), lambda h, i: (h, i, 0, 0))
    kspec = pl.BlockSpec((n_k, tk, FA_HP), lambda h, i: (0, 0, 0))
    dq_c, dk, dv = pl.pallas_call(
        body, name=name, grid=(ATT_KV, nq),
        in_specs=[cspec, kspec, kspec, cspec, vspec, vspec],
        out_specs=[pl.BlockSpec((1, 1, ATT_DH, R), lambda h, i: (h, i, 0, 0)), kspec, kspec],
        out_shape=[jax.ShapeDtypeStruct((ATT_KV, nq, ATT_DH, R), F32),
                   jax.ShapeDtypeStruct((n_k, tk, FA_HP), F32), jax.ShapeDtypeStruct((n_k, tk, FA_HP), F32)],
        scratch_shapes=[pltpu.VMEM((FA_HP, R), F32)],
        compiler_params=_params(("arbitrary", "arbitrary")),
    )(q_c, k.reshape(n_k, tk, FA_HP), v.reshape(n_k, tk, FA_HP), do_c, lse, delta)
    return dq_c, dk.reshape(T, FA_HP), dv.reshape(T, FA_HP)


def _att_post_fwd2(o_c, w, *, name):
    _, nq, _, R = o_c.shape
    tm = R // ATT_G
    T = nq * tm

    def body(oc_ref, w_ref, o_ref, out_ref):
        ov = jnp.concatenate([_tokens_from_cols(oc_ref[kv, 0]) for kv in range(ATT_KV)], axis=1)
        r = lax.rsqrt(jnp.mean(ov * ov, axis=-1, keepdims=True) + EPS)
        o_ref[...] = ov
        out_ref[...] = (ov * r * w_ref[...]).astype(BF16)

    tok = pl.BlockSpec((tm, ATT_QW), lambda i: (i, 0))
    return pl.pallas_call(
        body, name=name, grid=(nq,),
        in_specs=[pl.BlockSpec((ATT_KV, 1, ATT_DH, R), lambda i: (0, i, 0, 0)), pl.BlockSpec((1, ATT_QW), lambda i: (0, 0))],
        out_specs=[tok, tok],
        out_shape=[jax.ShapeDtypeStruct((T, ATT_QW), F32), jax.ShapeDtypeStruct((T, ATT_QW), BF16)],
        compiler_params=_params(("parallel",)),
    )(o_c, w)


def _att_post_bwd2(dmix, o, w, *, name):
    T = o.shape[0]
    tm = min(FA_TQ, T)
    R = ATT_G * tm

    def body(dm_ref, o_ref, w_ref, do_ref, delta_ref, dw_ref):
        @pl.when(pl.program_id(0) == 0)
        def _():
            dw_ref[...] = jnp.zeros_like(dw_ref)

        ov = o_ref[...]
        r = lax.rsqrt(jnp.mean(ov * ov, axis=-1, keepdims=True) + EPS)
        xh = ov * r
        dm = dm_ref[...]
        dxh = dm * w_ref[...]
        t = jnp.mean(dxh * xh, axis=-1, keepdims=True)
        do = r * (dxh - xh * t)
        _store_padded_cols(do_ref, do)
        dob = do.astype(BF16).astype(F32)
        for kv in range(ATT_KV):
            delta_ref[kv, 0] = jnp.sum(_cols_from_tokens(dob * ov, kv), axis=0, keepdims=True)
        dw_ref[...] += jnp.sum(dm * xh, axis=0, keepdims=True)

    tok = pl.BlockSpec((tm, ATT_QW), lambda i: (i, 0))
    vec = pl.BlockSpec((1, ATT_QW), lambda i: (0, 0))
    return pl.pallas_call(
        body, name=name, grid=(T // tm,),
        in_specs=[pl.BlockSpec((tm, ATT_QW), lambda i: (i, 1)), tok, vec],
        out_specs=[pl.BlockSpec((ATT_KV, 1, FA_HP, R), lambda i: (0, i, 0, 0)),
                   pl.BlockSpec((ATT_KV, 1, 1, R), lambda i: (0, i, 0, 0)), vec],
        out_shape=[jax.ShapeDtypeStruct((ATT_KV, T // tm, FA_HP, R), BF16),
                   jax.ShapeDtypeStruct((ATT_KV, T // tm, 1, R), F32), jax.ShapeDtypeStruct((1, ATT_QW), F32)],
        compiler_params=_params(("arbitrary",)),
    )(dmix, o, w)


def _ffn_up(h2, wg, wu, *, name, tm=512):
    T = h2.shape[0]
    tn = _pick(D_FF, 1408)

    def body(h_ref, wg_ref, wu_ref, g_ref, u_ref, a_ref):
        hv = h_ref[...]
        g = jnp.dot(hv, wg_ref[...], preferred_element_type=F32)
        u = jnp.dot(hv, wu_ref[...], preferred_element_type=F32)
        g_ref[...] = g.astype(BF16)
        u_ref[...] = u.astype(BF16)
        a_ref[...] = (g * _sigmoid(g) * u).astype(BF16)

    wspec = pl.BlockSpec((D_MODEL, tn), lambda i, j: (0, j))
    ospec = pl.BlockSpec((tm, tn), lambda i, j: (i, j))
    return pl.pallas_call(
        body, name=name, grid=(T // tm, D_FF // tn),
        in_specs=[pl.BlockSpec((tm, D_MODEL), lambda i, j: (i, 0)), wspec, wspec],
        out_specs=[ospec] * 3, out_shape=[jax.ShapeDtypeStruct((T, D_FF), BF16)] * 3,
        compiler_params=_params(("parallel", "arbitrary")),
    )(h2, wg, wu)


def _ffn_act_bwd(dx2b, w_down, gate, up, *, name, tm=512):
    T = dx2b.shape[0]
    tn = _pick(D_FF, 1408)

    def body(dx_ref, w_ref, g_ref, u_ref, dg_ref, du_ref):
        da = lax.dot_general(dx_ref[...], w_ref[...], (((1,), (1,)), ((), ())), preferred_element_type=F32)
        g = g_ref[...].astype(F32)
        u = u_ref[...].astype(F32)
        sg = _sigmoid(g)
        dg_ref[...] = (da * u * (sg * (1.0 + g * (1.0 - sg)))).astype(BF16)
        du_ref[...] = (da * (g * sg)).astype(BF16)

    ospec = pl.BlockSpec((tm, tn), lambda i, j: (i, j))
    return pl.pallas_call(
        body, name=name, grid=(T // tm, D_FF // tn),
        in_specs=[pl.BlockSpec((tm, D_MODEL), lambda i, j: (i, 0)),
                  pl.BlockSpec((tn, D_MODEL), lambda i, j: (j, 0)), ospec, ospec],
        out_specs=[ospec] * 2, out_shape=[jax.ShapeDtypeStruct((T, D_FF), BF16)] * 2,
        compiler_params=_params(("parallel", "arbitrary")),
    )(dx2b, w_down, gate, up)


def _assemble_du(U, dq_f, dq_b, dz_f, dz_b, dv_f, dv_b, du_g, da_q, da_k, da_v, *, name, tm=256):
    T = U.shape[0]

    def body(uq_ref, dqf, dqb, dzf, dzb, dvf, dvb, dug, daq, dak, dav, out_ref):
        uq = uq_ref[...]
        sg = _sigmoid(uq)
        out_ref[:, 0:HG_W] = ((dqf[...] + dqb[...]) * (sg * (1.0 + uq * (1.0 - sg)))).astype(BF16)
        out_ref[:, HG_W:2 * HG_W] = dzf[...].astype(BF16)
        out_ref[:, 2 * HG_W:3 * HG_W] = dzb[...].astype(BF16)
        out_ref[:, 3 * HG_W:4 * HG_W] = (dvf[...] + dvb[...]).astype(BF16)
        out_ref[:, 4 * HG_W:5 * HG_W] = dug[...].astype(BF16)
        out_ref[:, 5 * HG_W:5 * HG_W + ATT_QW] = daq[...].astype(BF16)
        out_ref[:, 5 * HG_W + ATT_QW:5 * HG_W + ATT_QW + ATT_KW] = dak[...].astype(BF16)
        out_ref[:, 5 * HG_W + ATT_QW + ATT_KW:D_IN] = dav[...].astype(BF16)

    tok = pl.BlockSpec((tm, HG_W), lambda i: (i, 0))
    kv = pl.BlockSpec((tm, ATT_KW), lambda i: (i, 0))
    return pl.pallas_call(
        body, name=name, grid=(T // tm,),
        in_specs=[tok] * 9 + [kv, kv],
        out_specs=pl.BlockSpec((tm, D_IN), lambda i: (i, 0)),
        out_shape=jax.ShapeDtypeStruct((T, D_IN), BF16),
        compiler_params=_params(("parallel",)),
    )(U, dq_f, dq_b, dz_f, dz_b, dv_f, dv_b, du_g, da_q, da_k, da_v)


def _adam_math(w, g, m, v):
    m = ADAM_B1 * m + (1.0 - ADAM_B1) * g
    v = ADAM_B2 * v + (1.0 - ADAM_B2) * (g * g)
    m_hat = m / (1.0 - ADAM_B1 ** ADAM_STEP)
    v_hat = v / (1.0 - ADAM_B2 ** ADAM_STEP)
    delta = -ADAM_LR * (m_hat / (jnp.sqrt(v_hat) + ADAM_EPS) + ADAM_WD * w)
    return delta, m, v


def _adamw(parts, w, m, v, *, name, tr_cap=256):
    P, R, C = parts.shape
    tr = R
    for t in range(8, min(R, tr_cap) + 1, 8):
        if R % t == 0:
            tr = t

    def body(p_ref, w_ref, m_ref, v_ref, g_ref, d_ref, nm_ref, nv_ref):
        g = p_ref[0].astype(F32)
        for j in range(1, P):
            g = g + p_ref[j].astype(F32)
        d, nm, nv = _adam_math(w_ref[...], g, m_ref[...], v_ref[...])
        g_ref[...] = g
        d_ref[...] = d
        nm_ref[...] = nm
        nv_ref[...] = nv

    blk = pl.BlockSpec((tr, C), lambda i: (i, 0))
    return pl.pallas_call(
        body, name=name, grid=(R // tr,),
        in_specs=[pl.BlockSpec((P, tr, C), lambda i: (0, i, 0)), blk, blk, blk],
        out_specs=[blk] * 4, out_shape=[jax.ShapeDtypeStruct((R, C), F32)] * 4,
        compiler_params=_params(("parallel",)),
    )(parts, w, m, v)


def _all_gather(xs, *, name):
    n = len(xs)

    def body(*refs):
        ins, outs = refs[:n], refs[n:2 * n]
        send_sems, recv_sems, local_sems = refs[2 * n:]
        x, y, c = lax.axis_index("x"), lax.axis_index("y"), lax.axis_index("c")
        me, sibling = (x, y, c), (x, y, 1 - c)
        chips = [(1 - x, y), (x, 1 - y), (1 - x, 1 - y)]

        def slot(p):
            return 4 * p[0] + 2 * p[1] + p[2]

        def copy(a, k, block, to, src=None):
            dst = outs[a].at[slot(block)]
            return pltpu.make_async_remote_copy(
                src_ref=dst if src is None else src, dst_ref=dst,
                send_sem=send_sems.at[a * 7 + k], recv_sem=recv_sems.at[a * 7 + k],
                device_id=to, device_id_type=MESH)

        mine = [pltpu.make_async_copy(ins[a], outs[a].at[slot(me)], local_sems.at[a]) for a in range(n)]
        for cp in mine:
            cp.start()
        first = []
        for a in range(n):
            first.append(copy(a, 0, me, sibling, src=ins[a]))
            first += [copy(a, 1 + j, me, (*chip, c), src=ins[a]) for j, chip in enumerate(chips)]
        for cp in first:
            cp.start()
        passed = []
        for j, chip in enumerate(chips):
            for a in range(n):
                copy(a, 1 + j, (*chip, c), me).wait_recv()
                cp = copy(a, 4 + j, (*chip, c), sibling)
                cp.start()
                passed.append(cp)
        for a in range(n):
            copy(a, 0, sibling, me).wait_recv()
            for j, chip in enumerate(chips):
                copy(a, 4 + j, (*chip, 1 - c), me).wait_recv()
        for cp in first + passed:
            cp.wait_send()
        for cp in mine:
            cp.wait()

    return pl.pallas_call(
        body, name=name,
        in_specs=[ANY] * n, out_specs=[ANY] * n,
        out_shape=[jax.ShapeDtypeStruct((N_DEV,) + x.shape, x.dtype) for x in xs],
        scratch_shapes=[pltpu.SemaphoreType.DMA((7 * n,)), pltpu.SemaphoreType.DMA((7 * n,)),
                        pltpu.SemaphoreType.DMA((n,))],
        compiler_params=pltpu.CompilerParams(has_side_effects=True),
    )(*xs)


ALL_MASKS = [(mx, my, mc) for mx in (0, 1) for my in (0, 1) for mc in (0, 1)][1:]


def _exchange(gs, *, masks, slot, name, bcast=None):
    n, n_peers = len(gs), len(masks)
    has_bcast = bcast is not None

    def body(*refs):
        n_in = n + has_bcast
        ins, outs = refs[:n], refs[n_in:n_in + n]
        send_sems, recv_sems, local_sems = refs[2 * n_in:2 * n_in + 3]
        x, y, c = lax.axis_index("x"), lax.axis_index("y"), lax.axis_index("c")
        my_slot = slot((x, y, c))

        def flip(v, bit):
            return 1 - v if bit else v

        mine = [pltpu.make_async_copy(ins[a].at[my_slot], outs[a].at[my_slot], local_sems.at[a]) for a in range(n)]
        copies = []
        if has_bcast:
            b_in, b_out = refs[n], refs[2 * n_in - 1]
            b_send, b_recv = refs[2 * n_in + 3:]
            me = 4 * x + 2 * y + c
            mine.append(pltpu.make_async_copy(b_in, b_out.at[me], local_sems.at[n]))
            for k, (mx, my, mc) in enumerate(ALL_MASKS):
                peer = (flip(x, mx), flip(y, my), flip(c, mc))
                peer_id = 4 * peer[0] + 2 * peer[1] + peer[2]
                sems = dict(send_sem=b_send.at[k], recv_sem=b_recv.at[k], device_id=peer, device_id_type=MESH)
                copies.append((pltpu.make_async_remote_copy(src_ref=b_in, dst_ref=b_out.at[me], **sems),
                               pltpu.make_async_remote_copy(src_ref=b_in, dst_ref=b_out.at[peer_id], **sems)))
        for cp in mine:
            cp.start()
        for a in range(n):
            for k, (mx, my, mc) in enumerate(masks):
                peer = (flip(x, mx), flip(y, my), flip(c, mc))
                peer_slot = slot(peer)
                sems = dict(send_sem=send_sems.at[a * n_peers + k], recv_sem=recv_sems.at[a * n_peers + k],
                            device_id=peer, device_id_type=MESH)
                copies.append((
                    pltpu.make_async_remote_copy(src_ref=ins[a].at[peer_slot], dst_ref=outs[a].at[my_slot], **sems),
                    pltpu.make_async_remote_copy(src_ref=ins[a].at[peer_slot], dst_ref=outs[a].at[peer_slot], **sems)))
        for send, _ in copies:
            send.start()
        for send, recv in copies:
            recv.wait_recv()
            send.wait_send()
        for cp in mine:
            cp.wait()

    n_io = n + has_bcast
    out_shape = [jax.ShapeDtypeStruct(g.shape, g.dtype) for g in gs]
    scratch = [pltpu.SemaphoreType.DMA((n_peers * n,)), pltpu.SemaphoreType.DMA((n_peers * n,)),
               pltpu.SemaphoreType.DMA((n_io,))]
    if has_bcast:
        out_shape.append(jax.ShapeDtypeStruct((N_DEV,) + bcast.shape, bcast.dtype))
        scratch += [pltpu.SemaphoreType.DMA((len(ALL_MASKS),)), pltpu.SemaphoreType.DMA((len(ALL_MASKS),))]
    return pl.pallas_call(
        body, name=name,
        in_specs=[ANY] * n_io, out_specs=[ANY] * n_io, out_shape=out_shape, scratch_shapes=scratch,
        compiler_params=pltpu.CompilerParams(has_side_effects=True),
    )(*gs, *([bcast] if has_bcast else []))


SWAP_ROW_CHUNKS = 4


def _core_swap(gs, *, name):
    n = len(gs)

    def body(*refs):
        ins, outs = refs[:n], refs[n:2 * n]
        send_sems, recv_sems = refs[2 * n:]
        x, y, c = lax.axis_index("x"), lax.axis_index("y"), lax.axis_index("c")
        sibling = (x, y, 1 - c)
        started = []
        for a in range(n):
            _, Q, R, _ = ins[a].shape
            rows = R // SWAP_ROW_CHUNKS
            for q in range(Q):
                for j in range(SWAP_ROW_CHUNKS):
                    cp = pltpu.make_async_remote_copy(
                        src_ref=ins[a].at[1 - c, q, pl.ds(j * rows, rows)], dst_ref=outs[a].at[q, pl.ds(j * rows, rows)],
                        send_sem=send_sems.at[a], recv_sem=recv_sems.at[a], device_id=sibling, device_id_type=MESH)
                    cp.start()
                    started.append(cp)
        for a in range(n):
            pltpu.make_async_remote_copy(
                src_ref=ins[a].at[1 - c], dst_ref=outs[a], send_sem=send_sems.at[a], recv_sem=recv_sems.at[a],
                device_id=sibling, device_id_type=MESH).wait()

    return pl.pallas_call(
        body, name=name,
        in_specs=[ANY] * n, out_specs=[ANY] * n,
        out_shape=[jax.ShapeDtypeStruct(g.shape[1:], g.dtype) for g in gs],
        scratch_shapes=[pltpu.SemaphoreType.DMA((n,)), pltpu.SemaphoreType.DMA((n,))],
        compiler_params=pltpu.CompilerParams(has_side_effects=True),
    )(*gs)


def _pair_sum(g, other, core, *, name, tr_cap=256):
    _, Q, R, C = g.shape
    tr = max(t for t in range(16, min(R, tr_cap) + 1, 16) if R % t == 0)

    def body(core_ref, g_ref, o_ref, out_ref):
        out_ref[0] = (g_ref[0, 0] + o_ref[0]).astype(BF16)

    return pl.pallas_call(
        body, name=name,
        grid_spec=pltpu.PrefetchScalarGridSpec(
            num_scalar_prefetch=1, grid=(Q, R // tr),
            in_specs=[pl.BlockSpec((1, 1, tr, C), lambda q, i, core_ref: (core_ref[0], q, i, 0)),
                      pl.BlockSpec((1, tr, C), lambda q, i, core_ref: (q, i, 0))],
            out_specs=pl.BlockSpec((1, tr, C), lambda q, i, core_ref: (q, i, 0))),
        out_shape=jax.ShapeDtypeStruct((Q, R, C), BF16),
        compiler_params=_params(("parallel", "parallel")),
    )(core, g, other)


PACK_ROWS = 8


def _pack_small(norm1, norm2, final, att, hg, qn, kn, lb=None, loss=None):
    z = lambda n: jnp.zeros((n,), F32)
    rows = [norm1.reshape(-1), norm2.reshape(-1), final.reshape(-1),
            jnp.concatenate([att.reshape(-1), z(512)]),
            jnp.concatenate([hg.reshape(-1), qn.reshape(-1), kn.reshape(-1), z(1024 - 256)]),
            z(1024) if lb is None else lb.reshape(-1),
            z(1024) if loss is None else jnp.concatenate([loss.reshape(-1), z(1023)]), z(1024)]
    return jnp.stack(rows, axis=0)


def _unpack_small(p):
    return (p[0:1, :], p[1:2, :], p[2, :], p[3:4, 0:512], p[4:5, 0:128], p[4:5, 128:192], p[4:5, 192:256])


def _fold_heads(dhg, dqn, dkn, *, name):
    def body(hg_ref, q_ref, k_ref, ohg_ref, oq_ref, ok_ref):
        def fold128(v):
            acc = v[:, 0:LANES]
            for j in range(1, v.shape[1] // LANES):
                acc = acc + v[:, j * LANES:(j + 1) * LANES]
            return acc

        ohg_ref[...] = fold128(hg_ref[...])
        q = fold128(q_ref[...])
        oq_ref[...] = q + pltpu.roll(q, ATT_DH, 1)
        k = k_ref[...]
        ok_ref[...] = k + pltpu.roll(k, ATT_DH, 1)

    return pl.pallas_call(body, name=name, out_shape=[jax.ShapeDtypeStruct((1, LANES), F32)] * 3)(dhg, dqn, dkn)


def _lb_grad(dlb_sum, lb, *, name):
    def body(d_ref, lb_ref, o_ref):
        lbv = lb_ref[...]
        gl = d_ref[...] * lbv * (1.0 - lbv)
        o_ref[0:1, :] = gl[0:1, :]
        o_ref[1:2, :] = -gl[0:1, :]
        o_ref[2:3, :] = gl[1:2, :]
        o_ref[3:4, :] = -gl[1:2, :]

    return pl.pallas_call(body, name=name, out_shape=jax.ShapeDtypeStruct((4, HG_W), F32))(dlb_sum, lb)


def _lower_bounds(lb_logits_full, *, name):
    def body(l_ref, o_ref):
        for d in range(2):
            l0, l1 = l_ref[2 * d:2 * d + 1, :], l_ref[2 * d + 1:2 * d + 2, :]
            mx = jnp.maximum(l0, l1)
            e0, e1 = jnp.exp(l0 - mx), jnp.exp(l1 - mx)
            o_ref[d:d + 1, :] = e0 / (e0 + e1)

    return pl.pallas_call(body, name=name, out_shape=jax.ShapeDtypeStruct((2, HG_W), F32))(
        lb_logits_full.reshape(4, HG_W))


def _local_step(x, target, norm1_w, w_in, lb, hg_norm_w, q_norm_w, k_norm_w, att_norm_w, w_out, norm2_w,
                w_g, w_u, w_down, final_norm_w):
    T = x.shape[0]
    cos, sin = _rope_tables(T)
    qw8 = jnp.tile(q_norm_w, (1, ATT_HEADS))
    kw2 = jnp.tile(k_norm_w, (1, ATT_KV))

    h, r1 = _rms_fwd(x, norm1_w, name="norm1_fwd")
    U = _mm_nn([(h, w_in)], name="in_proj")
    o_f, st_f = _gla_fwd(U, lb[0:1], f_block=1, reverse=False, name="gla_fwd_f")
    o_b, st_b = _gla_fwd(U, lb[1:2], f_block=2, reverse=True, name="gla_fwd_b")
    mix_hg = _hg_post_fwd(o_f, o_b, U, hg_norm_w, name="hg_post_fwd")
    q_c, k, v, qn_c, kmax2 = _att_prep_fwd2(U, cos, sin, qw8, kw2, name="att_prep_fwd")
    kmax = jnp.sqrt(jnp.max(kmax2.reshape(ATT_KV, ATT_DH), axis=1))
    m_c = qn_c * (kmax * 1.001).reshape(ATT_KV, 1, 1, 1)
    o_c, lse = lax.cond(jnp.max(m_c) <= FA_BOUND_MAX,
                        lambda: _flash_fwd_bounded(q_c, k, v, m_c, name="flash_fwd_bounded"),
                        lambda: _flash_fwd2(q_c, k, v, name="flash_fwd"))
    o_att, mix_att = _att_post_fwd2(o_c, att_norm_w, name="att_post_fwd")
    x1 = _mm_nn([(mix_hg, w_out[:HG_W]), (mix_att, w_out[HG_W:])], residual=x, name="out_proj")
    h2, r2 = _rms_fwd(x1, norm2_w, name="norm2_fwd")
    gate, up, act = _ffn_up(h2, w_g, w_u, name="ffn_up")
    x2 = _mm_nn([(act, w_down)], residual=x1, name="ffn_down")
    loss, dx2, dx2b, d_final = _loss_head(x2, target, final_norm_w.reshape(1, D_MODEL), name="loss_head")

    d_gate, d_up = _ffn_act_bwd(dx2b, w_down, gate, up, name="ffn_act_bwd")
    dw_down = _mm_tn(act, dx2b, tma_cap=1408, name="dw_down")
    dh2 = _mm_nn([(d_gate, w_g), (d_up, w_u)], trans_b=True, tm=256, name="ffn_up_bwd")
    dw_g = _mm_tn(h2, d_gate, tnb_cap=1408, name="dw_gate")
    dw_u = _mm_tn(h2, d_up, tnb_cap=1408, name="dw_up")
    dx1, dx1b, d_norm2 = _rms_bwd(dh2, x1, r2, norm2_w, dx2, emit_bf16=True, name="norm2_bwd")
    dmix = _mm_nn([(dx1b, w_out)], trans_b=True, name="out_proj_bwd")
    dw_out = jnp.concatenate([_mm_tn(mix_hg, dx1b, name="dw_out_hg"), _mm_tn(mix_att, dx1b, name="dw_out_att")], axis=0)
    do_c, delta, d_att = _att_post_bwd2(dmix, o_att, att_norm_w, name="att_post_bwd")
    dq_c, dk, dv = _flash_bwd2(q_c, k, v, do_c, lse, delta, name="flash_bwd")
    dU_att, d_qn, d_kn = _att_prep_bwd2(U, dq_c, dk, dv, cos, sin, qw8, kw2, name="att_prep_bwd")
    do_hg, du_g, d_hg = _hg_post_bwd(dmix, o_f, o_b, U, hg_norm_w, name="hg_post_bwd")
    dq_f, dz_f, dv_f, dlb_f = _gla_bwd(U, lb[0:1], do_hg, st_f, f_block=1, reverse=False, name="gla_bwd_f")
    dU_hg, dlb_b = _gla_bwd(U, lb[1:2], do_hg, st_b, f_block=2, reverse=True, prev=(dq_f, dz_f, dv_f, du_g),
                            name="gla_bwd_b")
    w_hg = 5 * HG_W
    dh = _mm_nn([(dU_hg, w_in[:, :w_hg]), (dU_att, w_in[:, w_hg:])], trans_b=True, name="in_proj_bwd")
    dw_in = jnp.concatenate([_mm_tn(h, dU_hg, tnb_cap=1280, name="dw_in_hg"), _mm_tn(h, dU_att, name="dw_in_att")],
                            axis=1)
    grad_x, d_norm1 = _rms_bwd(dh, x, r1, norm1_w, dx1, emit_bf16=False, name="norm1_bwd")
    d_hg, d_qn, d_kn = _fold_heads(d_hg, d_qn, d_kn, name="fold_heads")

    big = dict(w_in=dw_in, w_out=dw_out, w_g=dw_g, w_u=dw_u, w_down=dw_down)
    small = dict(norm1=d_norm1, norm2=d_norm2, final=d_final, att=d_att, hg=d_hg,
                 qn=d_qn[:, :ATT_DH], kn=d_kn[:, :ATT_DH], lb=jnp.concatenate([dlb_f, dlb_b], axis=0))
    return loss, grad_x, big, small


def kernel(x, norm1_w, w_in, lb_logits, hg_norm_w, q_norm_w, k_norm_w, att_norm_w, w_out, norm2_w, w_gate_up, w_down, final_norm_w, loss_target, m_norm1_w, m_w_in, m_lb_logits, m_hg_norm_w, m_q_norm_w, m_k_norm_w, m_att_norm_w, m_w_out, m_norm2_w, m_w_gate_up, m_w_down, m_final_norm_w, v_norm1_w, v_w_in, v_lb_logits, v_hg_norm_w, v_q_norm_w, v_k_norm_w, v_att_norm_w, v_w_out, v_norm2_w, v_w_gate_up, v_w_down, v_final_norm_w):
    T = x.shape[1]
    me = 4 * lax.axis_index("x") + 2 * lax.axis_index("y") + lax.axis_index("c")
    c_in, r_out, c_gu, r_dn = w_in.shape[2], w_out.shape[1], w_gate_up.shape[2], w_down.shape[1]
    lb_cols = lb_logits.shape[2]

    g_in, g_out, g_gu, g_dn, g_lb = _all_gather(
        [w_in[0].astype(BF16), w_out[0].astype(BF16), w_gate_up[0].astype(BF16), w_down[0].astype(BF16),
         lb_logits.reshape(4, lb_cols)], name="gather_weights")
    w_in_f = g_in.transpose(1, 0, 2).reshape(D_MODEL, N_DEV * c_in)
    w_out_f = g_out.reshape(N_DEV * r_out, D_MODEL)
    half = N_DEV // 2
    w_g_f = g_gu[:half].transpose(1, 0, 2).reshape(D_MODEL, half * c_gu)
    w_u_f = g_gu[half:].transpose(1, 0, 2).reshape(D_MODEL, half * c_gu)
    w_dn_f = g_dn.reshape(N_DEV * r_dn, D_MODEL)
    lb_logits_f = g_lb.transpose(1, 0, 2).reshape(2, 2, N_DEV * lb_cols)
    lb = _lower_bounds(lb_logits_f, name="lower_bounds")

    loss, grad_x, big, small = _local_step(
        x[0], loss_target[0], norm1_w, w_in_f, lb, hg_norm_w, q_norm_w, k_norm_w, att_norm_w, w_out_f, norm2_w,
        w_g_f, w_u_f, w_dn_f, final_norm_w)

    chips = N_DEV // 2
    by_owner_cols = lambda g, n_q, w: g.reshape(D_MODEL, n_q, 2, w).transpose(2, 1, 0, 3)
    by_owner_rows = lambda g, r: g.reshape(chips, 2, r, D_MODEL).transpose(1, 0, 2, 3)
    s_in = by_owner_cols(big["w_in"], chips, c_in)
    s_out = by_owner_rows(big["w_out"], r_out)
    s_gu = jnp.concatenate([by_owner_cols(big["w_g"], chips // 2, c_gu), by_owner_cols(big["w_u"], chips // 2, c_gu)],
                           axis=1)
    s_dn = by_owner_rows(big["w_down"], r_dn)
    mine = [s_in, s_out, s_gu, s_dn]
    theirs = _core_swap(mine, name="exchange_cores")
    core = lax.axis_index("c").astype(jnp.int32).reshape(1)
    chip_sums = [_pair_sum(g, o, core, name="pair_sum_" + nm)
                 for g, o, nm in zip(mine, theirs, ("w_in", "w_out", "w_gu", "w_down"))]

    packed = _pack_small(small["norm1"], small["norm2"], small["final"], small["att"], small["hg"],
                         small["qn"], small["kn"], small["lb"], loss)
    p_in, p_out, p_gu, p_dn, all_small = _exchange(chip_sums, masks=[(1, 0, 0), (0, 1, 0), (1, 1, 0)],
                                                   slot=lambda p: 2 * p[0] + p[1], bcast=packed,
                                                   name="exchange_chips")

    g_w_in, d_w_in, nm_w_in, nv_w_in = _adamw(p_in, w_in[0], m_w_in[0], v_w_in[0], name="adamw_w_in")
    g_w_out, d_w_out, nm_w_out, nv_w_out = _adamw(p_out, w_out[0], m_w_out[0], v_w_out[0], name="adamw_w_out")
    g_w_gu, d_w_gu, nm_w_gu, nv_w_gu = _adamw(p_gu, w_gate_up[0], m_w_gate_up[0], v_w_gate_up[0], name="adamw_w_gu")
    g_w_dn, d_w_dn, nm_w_dn, nv_w_dn = _adamw(p_dn, w_down[0], m_w_down[0], v_w_down[0], name="adamw_w_down")

    pk = lambda vecs: _pack_small(*vecs)
    w_pk = pk([norm1_w, norm2_w, final_norm_w, att_norm_w, hg_norm_w, q_norm_w, k_norm_w])
    m_pk = pk([m_norm1_w, m_norm2_w, m_final_norm_w, m_att_norm_w, m_hg_norm_w, m_q_norm_w, m_k_norm_w])
    v_pk = pk([v_norm1_w, v_norm2_w, v_final_norm_w, v_att_norm_w, v_hg_norm_w, v_q_norm_w, v_k_norm_w])
    g_pk, d_pk, nm_pk, nv_pk = _adamw(all_small, w_pk, m_pk, v_pk, name="adamw_small")

    dlb_sum = g_pk[5:6, :].reshape(2, HG_W)
    g_lb_full = _lb_grad(dlb_sum, lb, name="lb_grad")
    g_lb_mine = lax.dynamic_slice_in_dim(g_lb_full, me * lb_cols, lb_cols, axis=1)
    g_lb_s, d_lb, nm_lb, nv_lb = _adamw(g_lb_mine[None], lb_logits.reshape(4, lb_cols),
                                        m_lb_logits.reshape(4, lb_cols), v_lb_logits.reshape(4, lb_cols),
                                        name="adamw_lb")

    loss_total = g_pk[6, 0]

    def outs(big4, lb_arr, pk_arr):
        n1, n2, fin, att, hg, qn, kn = _unpack_small(pk_arr)
        b_in, b_out, b_gu, b_dn = big4
        return [n1, b_in[None], lb_arr.reshape(2, 2, lb_cols), hg, qn, kn, att, b_out[None], n2, b_gu[None],
                b_dn[None], fin]

    return (loss_total, grad_x[None],
            *outs((g_w_in, g_w_out, g_w_gu, g_w_dn), g_lb_s, g_pk),
            *outs((d_w_in, d_w_out, d_w_gu, d_w_dn), d_lb, d_pk),
            *outs((nm_w_in, nm_w_out, nm_w_gu, nm_w_dn), nm_lb, nm_pk),
            *outs((nv_w_in, nv_w_out, nv_w_gu, nv_w_dn), nv_lb, nv_pk))
```

```python
import functools
import math

import jax
import jax.numpy as jnp
import numpy as np
from jax import lax
from jax.experimental import pallas as pl
from jax.experimental.pallas import tpu as pltpu

F32 = jnp.float32
BF16 = jnp.bfloat16

N_DEV = 8
D_MODEL = 1024
EPS = 1e-6
HG_HEADS = 4
HG_D = 128
HG_W = HG_HEADS * HG_D
CHUNK = 64
ATT_HEADS = 8
ATT_KV = 2
ATT_G = ATT_HEADS // ATT_KV
ATT_DH = 64
ATT_QW = ATT_HEADS * ATT_DH
ATT_KW = ATT_KV * ATT_DH
GRID_W = 64
ROPE_THETA = 10000.0
D_IN = 5 * HG_W + ATT_QW + 2 * ATT_KW
D_FF = 2816
ADAM_LR, ADAM_B1, ADAM_B2, ADAM_EPS, ADAM_WD, ADAM_STEP = 0.001, 0.9, 0.999, 1e-08, 0.01, 10

LANES = 128
VMEM_LIMIT = 48 * 1024 * 1024
MESH = pl.DeviceIdType.MESH
ANY = pl.BlockSpec(memory_space=pl.ANY)


def _params(sem=None):
    return pltpu.CompilerParams(dimension_semantics=sem, vmem_limit_bytes=VMEM_LIMIT)


def _pick(n, cap):
    best = None
    for t in range(LANES, cap + 1, LANES):
        if n % t == 0:
            best = t
    assert best is not None, (n, cap)
    return best


def _sigmoid(x):
    return 1.0 / (1.0 + jnp.exp(-x))


def _dot(a, b):
    return jnp.dot(a.astype(BF16), b.astype(BF16), preferred_element_type=F32)


def _dot_nt(a, b):
    return lax.dot_general(a.astype(BF16), b.astype(BF16), (((1,), (1,)), ((), ())),
                           preferred_element_type=F32)


def _dot_tn(a, b):
    return lax.dot_general(a.astype(BF16), b.astype(BF16), (((0,), (0,)), ((), ())),
                           preferred_element_type=F32)


def _mm_nn(pairs, *, name, out_dtype=F32, residual=None, tm=512, tn_cap=None, trans_b=False):
    M = pairs[0][0].shape[0]
    N = pairs[0][1].shape[0 if trans_b else 1]
    tn = N if tn_cap is None else _pick(N, tn_cap)
    n_pairs = len(pairs)
    has_res = residual is not None
    dims = (((1,), (1,)), ((), ())) if trans_b else (((1,), (0,)), ((), ()))

    def body(*refs):
        acc = None
        for i in range(n_pairs):
            d = lax.dot_general(refs[2 * i][...], refs[2 * i + 1][...], dims, preferred_element_type=F32)
            acc = d if acc is None else acc + d
        if has_res:
            acc = acc + refs[2 * n_pairs][...]
        refs[-1][...] = acc.astype(out_dtype)

    in_specs, args = [], []
    for a, b in pairs:
        k = a.shape[1]
        b_spec = pl.BlockSpec((tn, k), lambda i, j: (j, 0)) if trans_b else pl.BlockSpec((k, tn), lambda i, j: (0, j))
        in_specs += [pl.BlockSpec((tm, k), lambda i, j: (i, 0)), b_spec]
        args += [a, b]
    if has_res:
        in_specs.append(pl.BlockSpec((tm, tn), lambda i, j: (i, j)))
        args.append(residual)
    return pl.pallas_call(
        body, name=name, grid=(M // tm, N // tn), in_specs=in_specs,
        out_specs=pl.BlockSpec((tm, tn), lambda i, j: (i, j)),
        out_shape=jax.ShapeDtypeStruct((M, N), out_dtype),
        compiler_params=_params(("parallel", "arbitrary")),
    )(*args)


def _mm_tn(a, b, *, name, tma_cap=1024, tnb_cap=1024, tk=1024):
    T, Ma = a.shape
    Nb = b.shape[1]
    tma, tnb = _pick(Ma, tma_cap), _pick(Nb, tnb_cap)
    tk = min(tk, T)
    n_k = T // tk

    def body(a_ref, b_ref, o_ref, acc_ref):
        k = pl.program_id(2)

        @pl.when(k == 0)
        def _():
            acc_ref[...] = jnp.zeros_like(acc_ref)

        acc_ref[...] += lax.dot_general(a_ref[...], b_ref[...], (((0,), (0,)), ((), ())),
                                        preferred_element_type=F32)

        @pl.when(k == n_k - 1)
        def _():
            o_ref[...] = acc_ref[...]

    return pl.pallas_call(
        body, name=name, grid=(Ma // tma, Nb // tnb, n_k),
        in_specs=[pl.BlockSpec((tk, tma), lambda i, j, k: (k, i)), pl.BlockSpec((tk, tnb), lambda i, j, k: (k, j))],
        out_specs=pl.BlockSpec((tma, tnb), lambda i, j, k: (i, j)),
        out_shape=jax.ShapeDtypeStruct((Ma, Nb), F32),
        scratch_shapes=[pltpu.VMEM((tma, tnb), F32)],
        compiler_params=_params(("parallel", "parallel", "arbitrary")),
    )(a, b)


def _rms_fwd(x, w, *, name, tm=512):
    T, Dm = x.shape

    def body(x_ref, w_ref, h_ref, r_ref):
        xv = x_ref[...]
        r = lax.rsqrt(jnp.mean(xv * xv, axis=-1, keepdims=True) + EPS)
        h_ref[...] = (xv * r * w_ref[...]).astype(BF16)
        r_ref[...] = r

    return pl.pallas_call(
        body, name=name, grid=(T // tm,),
        in_specs=[pl.BlockSpec((tm, Dm), lambda i: (i, 0)), pl.BlockSpec((1, Dm), lambda i: (0, 0))],
        out_specs=[pl.BlockSpec((tm, Dm), lambda i: (i, 0)), pl.BlockSpec((tm, 1), lambda i: (i, 0))],
        out_shape=[jax.ShapeDtypeStruct((T, Dm), BF16), jax.ShapeDtypeStruct((T, 1), F32)],
        compiler_params=_params(("parallel",)),
    )(x, w)


def _rms_bwd(dh, x, r, w, dres, *, name, emit_bf16, tm=512):
    T, Dm = x.shape

    def body(dh_ref, x_ref, r_ref, w_ref, dres_ref, *outs):
        dx_ref, dw_ref = outs[0], outs[-1]

        @pl.when(pl.program_id(0) == 0)
        def _():
            dw_ref[...] = jnp.zeros_like(dw_ref)

        rv = r_ref[...]
        xh = x_ref[...] * rv
        dhv = dh_ref[...]
        dxh = dhv * w_ref[...]
        t = jnp.mean(dxh * xh, axis=-1, keepdims=True)
        dx = dres_ref[...] + rv * (dxh - xh * t)
        dx_ref[...] = dx
        if emit_bf16:
            outs[1][...] = dx.astype(BF16)
        dw_ref[...] += jnp.sum(dhv * xh, axis=0, keepdims=True)

    row = pl.BlockSpec((tm, Dm), lambda i: (i, 0))
    vec = pl.BlockSpec((1, Dm), lambda i: (0, 0))
    out_specs = [row] + ([row] if emit_bf16 else []) + [vec]
    out_shape = ([jax.ShapeDtypeStruct((T, Dm), F32)] + ([jax.ShapeDtypeStruct((T, Dm), BF16)] if emit_bf16 else [])
                 + [jax.ShapeDtypeStruct((1, Dm), F32)])
    return pl.pallas_call(
        body, name=name, grid=(T // tm,),
        in_specs=[row, row, pl.BlockSpec((tm, 1), lambda i: (i, 0)), vec, row],
        out_specs=out_specs, out_shape=out_shape,
        compiler_params=_params(("arbitrary",)),
    )(dh, x, r, w, dres)


def _loss_head(x2, target, w, *, name, tm=512):
    T, Dm = x2.shape

    def body(x_ref, t_ref, w_ref, loss_ref, dx_ref, dxb_ref, dw_ref):
        @pl.when(pl.program_id(0) == 0)
        def _():
            loss_ref[...] = jnp.zeros_like(loss_ref)
            dw_ref[...] = jnp.zeros_like(dw_ref)

        xv = x_ref[...]
        r = lax.rsqrt(jnp.mean(xv * xv, axis=-1, keepdims=True) + EPS)
        xh = xv * r
        wv = w_ref[...]
        err = xh * wv - t_ref[...]
        row_loss = jnp.mean(err * err, axis=-1, keepdims=True)
        loss_ref[...] += 0.5 * jnp.sum(row_loss, axis=0, keepdims=True)
        dy = err * (1.0 / Dm)
        dxh = dy * wv
        t = jnp.mean(dxh * xh, axis=-1, keepdims=True)
        dx = r * (dxh - xh * t)
        dx_ref[...] = dx
        dxb_ref[...] = dx.astype(BF16)
        dw_ref[...] += jnp.sum(dy * xh, axis=0, keepdims=True)

    row = pl.BlockSpec((tm, Dm), lambda i: (i, 0))
    vec = pl.BlockSpec((1, Dm), lambda i: (0, 0))
    return pl.pallas_call(
        body, name=name, grid=(T // tm,),
        in_specs=[row, row, vec],
        out_specs=[pl.BlockSpec((1, 1), lambda i: (0, 0)), row, row, vec],
        out_shape=[jax.ShapeDtypeStruct((1, 1), F32), jax.ShapeDtypeStruct((T, Dm), F32),
                   jax.ShapeDtypeStruct((T, Dm), BF16), jax.ShapeDtypeStruct((1, Dm), F32)],
        compiler_params=_params(("arbitrary",)),
    )(x2, target, w)


GLA_TB = 512
GLA_NC = GLA_TB // CHUNK
GLA_UNROLL = 4


def _cumsum_rows(x, row, reverse):
    n = x.shape[0]
    s = 1
    while s < n:
        if not reverse:
            x = x + jnp.where(row >= s, pltpu.roll(x, s, 0), 0.0)
        else:
            x = x + jnp.where(row < n - s, pltpu.roll(x, n - s, 0), 0.0)
        s *= 2
    return x


def _gla_gates(uq, z, lbv):
    q = uq * _sigmoid(uq)
    sg = _sigmoid(z)
    sgn = _sigmoid(-z)
    f = lbv + (1.0 - lbv) * sg
    k = (1.0 - lbv) * sgn
    return q, sg, sgn, f, k


def _gla_decays(f, row, reverse):
    b = _cumsum_rows(jnp.log(f), row, reverse)
    if not reverse:
        bref, blast = b[CHUNK // 2 - 1:CHUNK // 2, :], b[CHUNK - 1:CHUNK, :]
    else:
        bref, blast = b[CHUNK // 2:CHUNK // 2 + 1, :], b[0:1, :]
    return b, bref, blast


def _gla_fwd(U, lb, *, f_block, reverse, name):
    T = U.shape[0]
    nb = T // GLA_TB

    def body(uq_ref, uf_ref, ui_ref, lb_ref, o_ref, st_ref, s_ref):
        @pl.when(pl.program_id(0) == 0)
        def _():
            s_ref[...] = jnp.zeros_like(s_ref)

        row = lax.broadcasted_iota(jnp.int32, (CHUNK, HG_D), 0)
        ri = lax.broadcasted_iota(jnp.int32, (CHUNK, CHUNK), 0)
        ci = lax.broadcasted_iota(jnp.int32, (CHUNK, CHUNK), 1)
        mask = (ri <= ci) if reverse else (ri >= ci)

        def chunk(j, carry):
            c = (GLA_NC - 1 - j) if reverse else j
            rows = pl.ds(pl.multiple_of(c * CHUNK, CHUNK), CHUNK)
            for h in range(HG_HEADS):
                cols = pl.ds(h * HG_D, HG_D)
                v = ui_ref[rows, cols]
                q, _, _, f, k = _gla_gates(uq_ref[rows, cols], uf_ref[rows, cols], lb_ref[:, cols])
                b, bref, blast = _gla_decays(f, row, reverse)
                s = jnp.where(mask, _dot_nt(q * jnp.exp(b - bref), k * jnp.exp(bref - b)), 0.0)
                st = s_ref[h]
                st_ref[c, h] = st
                o_ref[rows, cols] = _dot(s, v) + _dot_nt(q * jnp.exp(b), st)
                s_ref[h] = st * jnp.exp(blast) + _dot_tn(v, k * jnp.exp(blast - b))
            return carry

        lax.fori_loop(0, GLA_NC, chunk, 0, unroll=GLA_NC)

    blk = (lambda i: nb - 1 - i) if reverse else (lambda i: i)
    ucol = lambda cb: pl.BlockSpec((GLA_TB, HG_W), lambda i: (blk(i), cb))
    return pl.pallas_call(
        body, name=name, grid=(nb,),
        in_specs=[ucol(0), ucol(f_block), ucol(3), pl.BlockSpec((1, HG_W), lambda i: (0, 0))],
        out_specs=[pl.BlockSpec((GLA_TB, HG_W), lambda i: (blk(i), 0)),
                   pl.BlockSpec((GLA_NC, HG_HEADS, HG_D, HG_D), lambda i: (blk(i), 0, 0, 0))],
        out_shape=[jax.ShapeDtypeStruct((T, HG_W), F32),
                   jax.ShapeDtypeStruct((T // CHUNK, HG_HEADS, HG_D, HG_D), F32)],
        scratch_shapes=[pltpu.VMEM((HG_HEADS, HG_D, HG_D), F32)],
        compiler_params=_params(("arbitrary",)),
    )(U, U, U, lb)


def _gla_bwd(U, lb, do, states, *, f_block, reverse, name, prev=None):
    T = U.shape[0]
    nb = T // GLA_TB
    final = prev is not None

    def body(uq_ref, uf_ref, ui_ref, lb_ref, do_ref, st_ref, *rest):
        if final:
            dqp_ref, dzp_ref, dvp_ref, dug_ref, out_ref, dlb_ref, ds_ref = rest
        else:
            dq_ref, dz_ref, dv_ref, dlb_ref, ds_ref = rest

        @pl.when(pl.program_id(0) == 0)
        def _():
            ds_ref[...] = jnp.zeros_like(ds_ref)
            dlb_ref[...] = jnp.zeros_like(dlb_ref)

        row = lax.broadcasted_iota(jnp.int32, (CHUNK, HG_D), 0)
        ri = lax.broadcasted_iota(jnp.int32, (CHUNK, CHUNK), 0)
        ci = lax.broadcasted_iota(jnp.int32, (CHUNK, CHUNK), 1)
        mask = (ri <= ci) if reverse else (ri >= ci)

        def chunk(j, carry):
            c = j if reverse else (GLA_NC - 1 - j)
            rows = pl.ds(pl.multiple_of(c * CHUNK, CHUNK), CHUNK)
            for h in range(HG_HEADS):
                cols = pl.ds(h * HG_D, HG_D)
                v = ui_ref[rows, cols]
                lbv = lb_ref[:, cols]
                uq = uq_ref[rows, cols]
                q, sg, sgn, f, k = _gla_gates(uq, uf_ref[rows, cols], lbv)
                b, bref, blast = _gla_decays(f, row, reverse)
                eq, ek, eb, el, dec = (jnp.exp(b - bref), jnp.exp(bref - b), jnp.exp(b), jnp.exp(blast - b),
                                       jnp.exp(blast))
                qin, kin, qb, klast = q * eq, k * ek, q * eb, k * el
                dov = do_ref[rows, cols]
                st = st_ref[c, h]
                dst = ds_ref[h]
                p = jnp.where(mask, _dot_nt(qin, kin), 0.0)
                dp = jnp.where(mask, _dot_nt(dov, v), 0.0)
                dqin = _dot(dp, kin)
                dkin = _dot_tn(dp, qin)
                dv = _dot_tn(p, dov) + _dot_nt(klast, dst)
                dqb = _dot(dov, st)
                dklast = _dot(v, dst)
                ds_ref[h] = _dot_tn(dov, qb) + dst * dec
                db = dqin * qin - dkin * kin + dqb * qb - dklast * klast
                extra = (jnp.sum(dklast * klast, axis=0, keepdims=True)
                         + dec * jnp.sum(st * dst, axis=0, keepdims=True))
                dg = _cumsum_rows(db, row, not reverse) + extra
                dq = dqin * eq + dqb * eb
                dk = dkin * ek + dklast * el
                dfk = dg / f - dk
                dz = (dfk * (1.0 - lbv) * sg * sgn).astype(BF16)
                dlb_ref[:, cols] += jnp.sum(dfk * sgn, axis=0, keepdims=True)
                if final:
                    sq = _sigmoid(uq)
                    col = lambda blk: pl.ds(blk * HG_W + h * HG_D, HG_D)
                    out_ref[rows, col(0)] = ((dq + dqp_ref[rows, cols]) * (sq * (1.0 + uq * (1.0 - sq)))).astype(BF16)
                    out_ref[rows, col(1)] = dzp_ref[rows, cols]
                    out_ref[rows, col(2)] = dz
                    out_ref[rows, col(3)] = (dv + dvp_ref[rows, cols]).astype(BF16)
                    out_ref[rows, col(4)] = dug_ref[rows, cols]
                else:
                    dq_ref[rows, cols] = dq
                    dz_ref[rows, cols] = dz
                    dv_ref[rows, cols] = dv
            return carry

        lax.fori_loop(0, GLA_NC, chunk, 0, unroll=GLA_UNROLL)

    blk = (lambda i: i) if reverse else (lambda i: nb - 1 - i)
    ucol = lambda cb: pl.BlockSpec((GLA_TB, HG_W), lambda i: (blk(i), cb))
    tok = pl.BlockSpec((GLA_TB, HG_W), lambda i: (blk(i), 0))
    vec = pl.BlockSpec((1, HG_W), lambda i: (0, 0))
    in_specs = [ucol(0), ucol(f_block), ucol(3), vec, tok,
                pl.BlockSpec((GLA_NC, HG_HEADS, HG_D, HG_D), lambda i: (blk(i), 0, 0, 0))]
    vec_shape = jax.ShapeDtypeStruct((1, HG_W), F32)
    if final:
        in_specs += [tok] * 4
        out_specs = [pl.BlockSpec((GLA_TB, 5 * HG_W), lambda i: (blk(i), 0)), vec]
        out_shape = [jax.ShapeDtypeStruct((T, 5 * HG_W), BF16), vec_shape]
    else:
        out_specs = [tok, tok, tok, vec]
        out_shape = [jax.ShapeDtypeStruct((T, HG_W), F32), jax.ShapeDtypeStruct((T, HG_W), BF16),
                     jax.ShapeDtypeStruct((T, HG_W), F32), vec_shape]
    return pl.pallas_call(
        body, name=name, grid=(nb,), in_specs=in_specs, out_specs=out_specs, out_shape=out_shape,
        scratch_shapes=[pltpu.VMEM((HG_HEADS, HG_D, HG_D), F32)],
        compiler_params=_params(("arbitrary",)),
    )(U, U, U, lb, do, states, *(prev if final else ()))


def _hg_post_fwd(o_f, o_b, U, w, *, name, tm=512):
    T = o_f.shape[0]

    def body(of_ref, ob_ref, ug_ref, w_ref, out_ref):
        wv = w_ref[...]
        for h in range(HG_HEADS):
            cols = pl.ds(h * HG_D, HG_D)
            o = of_ref[:, cols] + ob_ref[:, cols]
            r = lax.rsqrt(jnp.mean(o * o, axis=-1, keepdims=True) + EPS)
            ug = ug_ref[:, cols]
            out_ref[:, cols] = (o * r * wv * (ug * _sigmoid(ug))).astype(BF16)

    tok = pl.BlockSpec((tm, HG_W), lambda i: (i, 0))
    return pl.pallas_call(
        body, name=name, grid=(T // tm,),
        in_specs=[tok, tok, pl.BlockSpec((tm, HG_W), lambda i: (i, 4)), pl.BlockSpec((1, HG_D), lambda i: (0, 0))],
        out_specs=tok, out_shape=jax.ShapeDtypeStruct((T, HG_W), BF16),
        compiler_params=_params(("parallel",)),
    )(o_f, o_b, U, w)


def _hg_post_bwd(dmix, o_f, o_b, U, w, *, name, tm=512):
    T = o_f.shape[0]

    def body(dm_ref, of_ref, ob_ref, ug_ref, w_ref, do_ref, dug_ref, dw_ref):
        @pl.when(pl.program_id(0) == 0)
        def _():
            dw_ref[...] = jnp.zeros_like(dw_ref)

        wv = w_ref[...]
        for h in range(HG_HEADS):
            cols = pl.ds(h * HG_D, HG_D)
            o = of_ref[:, cols] + ob_ref[:, cols]
            r = lax.rsqrt(jnp.mean(o * o, axis=-1, keepdims=True) + EPS)
            xh = o * r
            ug = ug_ref[:, cols]
            sg = _sigmoid(ug)
            dm = dm_ref[:, cols]
            dn = dm * (ug * sg)
            dug_ref[:, cols] = (dm * (xh * wv) * (sg * (1.0 + ug * (1.0 - sg)))).astype(BF16)
            dxh = dn * wv
            t = jnp.mean(dxh * xh, axis=-1, keepdims=True)
            do_ref[:, cols] = r * (dxh - xh * t)
            dw_ref[:, cols] += jnp.sum(dn * xh, axis=0, keepdims=True)

    tok = pl.BlockSpec((tm, HG_W), lambda i: (i, 0))
    vec = pl.BlockSpec((1, HG_W), lambda i: (0, 0))
    return pl.pallas_call(
        body, name=name, grid=(T // tm,),
        in_specs=[tok, tok, tok, pl.BlockSpec((tm, HG_W), lambda i: (i, 4)), pl.BlockSpec((1, HG_D), lambda i: (0, 0))],
        out_specs=[tok, tok, vec],
        out_shape=[jax.ShapeDtypeStruct((T, HG_W), F32), jax.ShapeDtypeStruct((T, HG_W), BF16),
                   jax.ShapeDtypeStruct((1, HG_W), F32)],
        compiler_params=_params(("arbitrary",)),
    )(dmix, o_f, o_b, U, w)


def _rope_tables(T):
    rows = T // GRID_W
    row = np.repeat(np.arange(rows), GRID_W).astype(np.float32)
    col = np.tile(np.arange(GRID_W), rows).astype(np.float32)
    axis_dim = ATT_DH // 2
    freqs = (np.float32(ROPE_THETA) ** (-np.arange(0, axis_dim, 2, dtype=np.float32) / np.float32(axis_dim))
             ).astype(np.float32)
    ang = np.concatenate([row[:, None] * freqs, col[:, None] * freqs], axis=-1).astype(np.float32)
    cos, sin = np.cos(ang), np.sin(ang)
    c = np.repeat(cos, 2, axis=-1)
    s = np.stack([-sin, sin], axis=-1).reshape(T, ATT_DH)
    return jnp.asarray(np.tile(c, (1, 2)), F32), jnp.asarray(np.tile(s, (1, 2)), F32)


def _head_blockdiag(width):
    shift = ATT_DH.bit_length() - 1
    ri = jnp.right_shift(lax.broadcasted_iota(jnp.int32, (width, width), 0), shift)
    ci = jnp.right_shift(lax.broadcasted_iota(jnp.int32, (width, width), 1), shift)
    return jnp.where(ri == ci, 1.0, 0.0).astype(BF16)


def _head_sum(x, bd):
    hi = x.astype(BF16)
    lo = (x - hi.astype(F32)).astype(BF16)
    return jnp.dot(hi, bd, preferred_element_type=F32) + jnp.dot(lo, bd, preferred_element_type=F32)


def _pair_swap(x, even):
    n = x.shape[-1]
    return jnp.where(even, pltpu.roll(x, n - 1, 1), pltpu.roll(x, 1, 1))


def _att_prep_fwd(U, cos, sin, qw, kw, *, name, tm=512):
    T = U.shape[0]
    scale = ATT_DH ** -0.5

    def body(aq_ref, ak_ref, av_ref, c_ref, s_ref, qw_ref, kw_ref, q_ref, k_ref, v_ref):
        bd = _head_blockdiag(ATT_QW)
        c2, s2 = c_ref[...], s_ref[...]
        c8, s8 = jnp.tile(c2, (1, 4)), jnp.tile(s2, (1, 4))

        def norm_rope(x, w, c, s, bdm):
            r = lax.rsqrt(_head_sum(x * x, bdm) * (1.0 / ATT_DH) + EPS)
            y = x * r * w
            even = (lax.broadcasted_iota(jnp.int32, y.shape, 1) & 1) == 0
            return y * c + _pair_swap(y, even) * s

        q_ref[...] = (norm_rope(aq_ref[...], qw_ref[...], c8, s8, bd) * scale).astype(BF16)
        k_ref[...] = norm_rope(ak_ref[...], kw_ref[...], c2, s2, bd[:ATT_KW, :ATT_KW]).astype(BF16)
        v_ref[...] = av_ref[...].astype(BF16)

    kv_spec = pl.BlockSpec((tm, ATT_KW), lambda i: (i, 0))
    return pl.pallas_call(
        body, name=name, grid=(T // tm,),
        in_specs=[pl.BlockSpec((tm, ATT_QW), lambda i: (i, 5)),
                  pl.BlockSpec((tm, ATT_KW), lambda i: (i, 24)), pl.BlockSpec((tm, ATT_KW), lambda i: (i, 25)),
                  kv_spec, kv_spec,
                  pl.BlockSpec((1, ATT_QW), lambda i: (0, 0)), pl.BlockSpec((1, ATT_KW), lambda i: (0, 0))],
        out_specs=[pl.BlockSpec((tm, ATT_QW), lambda i: (i, 0)), kv_spec, kv_spec],
        out_shape=[jax.ShapeDtypeStruct((T, ATT_QW), BF16), jax.ShapeDtypeStruct((T, ATT_KW), BF16),
                   jax.ShapeDtypeStruct((T, ATT_KW), BF16)],
        compiler_params=_params(("parallel",)),
    )(U, U, U, cos, sin, qw, kw)


def _att_prep_bwd(U, dq, dk, cos, sin, qw, kw, *, name, tm=512):
    T = U.shape[0]
    scale = ATT_DH ** -0.5

    def body(aq_ref, ak_ref, dq_ref, dk_ref, c_ref, s_ref, qw_ref, kw_ref, daq_ref, dak_ref, dqw_ref, dkw_ref):
        @pl.when(pl.program_id(0) == 0)
        def _():
            dqw_ref[...] = jnp.zeros_like(dqw_ref)
            dkw_ref[...] = jnp.zeros_like(dkw_ref)

        bd = _head_blockdiag(ATT_QW)
        c2, s2 = c_ref[...], s_ref[...]
        c8, s8 = jnp.tile(c2, (1, 4)), jnp.tile(s2, (1, 4))

        def bwd(x, dy, w, c, s, bdm):
            even = (lax.broadcasted_iota(jnp.int32, x.shape, 1) & 1) == 0
            dn = dy * c - _pair_swap(dy, even) * s
            r = lax.rsqrt(_head_sum(x * x, bdm) * (1.0 / ATT_DH) + EPS)
            xh = x * r
            dxh = dn * w
            t = _head_sum(dxh * xh, bdm) * (1.0 / ATT_DH)
            return r * (dxh - xh * t), jnp.sum(dn * xh, axis=0, keepdims=True)

        da, dw = bwd(aq_ref[...], dq_ref[...] * scale, qw_ref[...], c8, s8, bd)
        daq_ref[...] = da
        dqw_ref[...] += dw
        da, dw = bwd(ak_ref[...], dk_ref[...], kw_ref[...], c2, s2, bd[:ATT_KW, :ATT_KW])
        dak_ref[...] = da
        dkw_ref[...] += dw

    q_spec = pl.BlockSpec((tm, ATT_QW), lambda i: (i, 0))
    kv_spec = pl.BlockSpec((tm, ATT_KW), lambda i: (i, 0))
    qv = pl.BlockSpec((1, ATT_QW), lambda i: (0, 0))
    kv = pl.BlockSpec((1, ATT_KW), lambda i: (0, 0))
    return pl.pallas_call(
        body, name=name, grid=(T // tm,),
        in_specs=[pl.BlockSpec((tm, ATT_QW), lambda i: (i, 5)), pl.BlockSpec((tm, ATT_KW), lambda i: (i, 24)),
                  q_spec, kv_spec, kv_spec, kv_spec, qv, kv],
        out_specs=[q_spec, kv_spec, qv, kv],
        out_shape=[jax.ShapeDtypeStruct((T, ATT_QW), F32), jax.ShapeDtypeStruct((T, ATT_KW), F32),
                   jax.ShapeDtypeStruct((1, ATT_QW), F32), jax.ShapeDtypeStruct((1, ATT_KW), F32)],
        compiler_params=_params(("arbitrary",)),
    )(U, U, dq, dk, cos, sin, qw, kw)


FA_TQ = 256
FA_TK = 256
FA_SW = 128


def _fa_tiles(T):
    tq, tk = min(FA_TQ, T), min(FA_TK, T)
    return tq, tk, T // tq, T // tk


def _to_fa_cols(a, T):
    tq, _, nq, _ = _fa_tiles(T)
    return a.reshape(nq, tq, ATT_KV, ATT_G, ATT_DH).transpose(2, 0, 4, 3, 1).reshape(ATT_KV, nq, ATT_DH, ATT_G * tq)


def _to_fa_rows(a, T):
    tq, _, nq, _ = _fa_tiles(T)
    return a.reshape(nq, tq, ATT_KV, ATT_G, ATT_DH).transpose(2, 0, 3, 1, 4).reshape(ATT_KV, nq, ATT_G * tq, ATT_DH)


def _from_fa_cols(a, T):
    tq, _, nq, _ = _fa_tiles(T)
    return a.reshape(ATT_KV, nq, ATT_DH, ATT_G, tq).transpose(1, 4, 0, 3, 2).reshape(T, ATT_QW)


def _kv_rows(a, T):
    _, tk, _, n_k = _fa_tiles(T)
    return a.reshape(n_k, tk, ATT_KV, ATT_DH).transpose(2, 0, 1, 3)


def _kv_cols(a, T):
    _, tk, _, n_k = _fa_tiles(T)
    return a.reshape(n_k, tk, ATT_KV, ATT_DH).transpose(2, 0, 3, 1)


def _flash_fwd(q_c, k_r, v_c, *, name):
    _, nq, _, R = q_c.shape
    _, n_k, tk, _ = k_r.shape

    def body(q_ref, k_ref, v_ref, o_ref, lse_ref):
        for st in range(R // FA_SW):
            lanes = pl.ds(st * FA_SW, FA_SW)
            qv = q_ref[0, 0, :, lanes]

            def step(j, carry):
                m, l, acc = carry
                s = jnp.dot(k_ref[0, j], qv, preferred_element_type=F32)
                m_new = jnp.maximum(m, jnp.max(s, axis=0, keepdims=True))
                alpha = jnp.exp(m - m_new)
                p = jnp.exp(s - m_new)
                l = alpha * l + jnp.sum(p, axis=0, keepdims=True)
                acc = alpha * acc + jnp.dot(v_ref[0, j], p.astype(BF16), preferred_element_type=F32)
                return m_new, l, acc

            m, l, acc = lax.fori_loop(0, n_k, step, (jnp.full((1, FA_SW), -jnp.inf, F32), jnp.zeros((1, FA_SW), F32),
                                                     jnp.zeros((ATT_DH, FA_SW), F32)))
            o_ref[0, 0, :, lanes] = acc / l
            lse_ref[0, 0, :, lanes] = m + jnp.log(l)

    qspec = pl.BlockSpec((1, 1, ATT_DH, R), lambda h, i: (h, i, 0, 0))
    return pl.pallas_call(
        body, name=name, grid=(ATT_KV, nq),
        in_specs=[qspec, pl.BlockSpec((1, n_k, tk, ATT_DH), lambda h, i: (h, 0, 0, 0)),
                  pl.BlockSpec((1, n_k, ATT_DH, tk), lambda h, i: (h, 0, 0, 0))],
        out_specs=[qspec, pl.BlockSpec((1, 1, 1, R), lambda h, i: (h, i, 0, 0))],
        out_shape=[jax.ShapeDtypeStruct((ATT_KV, nq, ATT_DH, R), F32), jax.ShapeDtypeStruct((ATT_KV, nq, 1, R), F32)],
        compiler_params=_params(("parallel", "parallel")),
    )(q_c, k_r, v_c)


def _flash_bwd(q_c, q_r, k_r, k_c, v_r, do_c, do_r, o_c, lse, *, name):
    _, nq, _, R = q_c.shape
    _, n_k, tk, _ = k_r.shape

    def body(qc_ref, qr_ref, kr_ref, kc_ref, vr_ref, doc_ref, dor_ref, oc_ref, lse_ref, dq_ref, dk_ref, dv_ref,
             acc_ref):
        @pl.when(pl.program_id(1) == 0)
        def _():
            dk_ref[...] = jnp.zeros_like(dk_ref)
            dv_ref[...] = jnp.zeros_like(dv_ref)

        qc, doc = qc_ref[0, 0], doc_ref[0, 0]
        qr, dor = qr_ref[0, 0], dor_ref[0, 0]
        delta = jnp.sum(doc.astype(F32) * oc_ref[0, 0], axis=0, keepdims=True)
        lsev = lse_ref[0, 0]
        acc_ref[...] = jnp.zeros_like(acc_ref)

        def step(j, carry):
            s = jnp.dot(kr_ref[0, j], qc, preferred_element_type=F32)
            p = jnp.exp(s - lsev)
            dp = jnp.dot(vr_ref[0, j], doc, preferred_element_type=F32)
            ds = (p * (dp - delta)).astype(BF16)
            acc_ref[...] += jnp.dot(kc_ref[0, j], ds, preferred_element_type=F32)
            dk_ref[0, j] += jnp.dot(ds, qr, preferred_element_type=F32)
            dv_ref[0, j] += jnp.dot(p.astype(BF16), dor, preferred_element_type=F32)
            return carry

        lax.fori_loop(0, n_k, step, 0)
        dq_ref[0, 0] = acc_ref[...]

    cspec = pl.BlockSpec((1, 1, ATT_DH, R), lambda h, i: (h, i, 0, 0))
    rspec = pl.BlockSpec((1, 1, R, ATT_DH), lambda h, i: (h, i, 0, 0))
    krspec = pl.BlockSpec((1, n_k, tk, ATT_DH), lambda h, i: (h, 0, 0, 0))
    kcspec = pl.BlockSpec((1, n_k, ATT_DH, tk), lambda h, i: (h, 0, 0, 0))
    return pl.pallas_call(
        body, name=name, grid=(ATT_KV, nq),
        in_specs=[cspec, rspec, krspec, kcspec, krspec, cspec, rspec, cspec,
                  pl.BlockSpec((1, 1, 1, R), lambda h, i: (h, i, 0, 0))],
        out_specs=[cspec, krspec, krspec],
        out_shape=[jax.ShapeDtypeStruct((ATT_KV, nq, ATT_DH, R), F32),
                   jax.ShapeDtypeStruct((ATT_KV, n_k, tk, ATT_DH), F32),
                   jax.ShapeDtypeStruct((ATT_KV, n_k, tk, ATT_DH), F32)],
        scratch_shapes=[pltpu.VMEM((ATT_DH, R), F32)],
        compiler_params=_params(("parallel", "arbitrary")),
    )(q_c, q_r, k_r, k_c, v_r, do_c, do_r, o_c, lse)


def _att_post_fwd(o, w, *, name, tm=512):
    T = o.shape[0]

    def body(o_ref, w_ref, out_ref):
        ov = o_ref[...]
        r = lax.rsqrt(jnp.mean(ov * ov, axis=-1, keepdims=True) + EPS)
        out_ref[...] = (ov * r * w_ref[...]).astype(BF16)

    tok = pl.BlockSpec((tm, ATT_QW), lambda i: (i, 0))
    return pl.pallas_call(
        body, name=name, grid=(T // tm,),
        in_specs=[tok, pl.BlockSpec((1, ATT_QW), lambda i: (0, 0))],
        out_specs=tok, out_shape=jax.ShapeDtypeStruct((T, ATT_QW), BF16),
        compiler_params=_params(("parallel",)),
    )(o, w)


def _att_post_bwd(dmix, o, w, *, name, tm=512):
    T = o.shape[0]

    def body(dm_ref, o_ref, w_ref, do_ref, dw_ref):
        @pl.when(pl.program_id(0) == 0)
        def _():
            dw_ref[...] = jnp.zeros_like(dw_ref)

        ov = o_ref[...]
        r = lax.rsqrt(jnp.mean(ov * ov, axis=-1, keepdims=True) + EPS)
        xh = ov * r
        dm = dm_ref[...]
        dxh = dm * w_ref[...]
        t = jnp.mean(dxh * xh, axis=-1, keepdims=True)
        do_ref[...] = (r * (dxh - xh * t)).astype(BF16)
        dw_ref[...] += jnp.sum(dm * xh, axis=0, keepdims=True)

    tok = pl.BlockSpec((tm, ATT_QW), lambda i: (i, 0))
    vec = pl.BlockSpec((1, ATT_QW), lambda i: (0, 0))
    return pl.pallas_call(
        body, name=name, grid=(T // tm,),
        in_specs=[pl.BlockSpec((tm, ATT_QW), lambda i: (i, 1)), tok, vec],
        out_specs=[tok, vec],
        out_shape=[jax.ShapeDtypeStruct((T, ATT_QW), BF16), jax.ShapeDtypeStruct((1, ATT_QW), F32)],
        compiler_params=_params(("arbitrary",)),
    )(dmix, o, w)


FA_HP = ATT_KV * ATT_DH
FA_TK_FWD = 512
FA_TK_BWD = 512


def _cols_from_tokens(x, kv):
    w = ATT_G * ATT_DH
    xt = x[:, kv * w:(kv + 1) * w].T
    return jnp.concatenate([xt[g * ATT_DH:(g + 1) * ATT_DH, :] for g in range(ATT_G)], axis=1)


def _tokens_from_cols(c):
    tq = c.shape[1] // ATT_G
    return jnp.concatenate([c[:, g * tq:(g + 1) * tq] for g in range(ATT_G)], axis=0).T


def _store_padded_cols(ref, x, norm_ref=None):
    for kv in range(ATT_KV):
        cols = _cols_from_tokens(x, kv).astype(BF16)
        ref[kv, 0, kv * ATT_DH:(kv + 1) * ATT_DH, :] = cols
        ref[kv, 0, (1 - kv) * ATT_DH:(2 - kv) * ATT_DH, :] = jnp.zeros_like(cols)
        if norm_ref is not None:
            cf = cols.astype(F32)
            norm_ref[kv, 0] = jnp.sqrt(jnp.sum(cf * cf, axis=0, keepdims=True))


def _att_prep_fwd2(U, cos, sin, qw, kw, *, name):
    T = U.shape[0]
    tm = min(FA_TQ, T)
    R = ATT_G * tm
    scale = ATT_DH ** -0.5

    def body(aq_ref, ak_ref, av_ref, c_ref, s_ref, qw_ref, kw_ref, q_ref, k_ref, v_ref, qn_ref, kmax_ref):
        @pl.when(pl.program_id(0) == 0)
        def _():
            kmax_ref[...] = jnp.zeros_like(kmax_ref)

        bd = _head_blockdiag(ATT_QW)
        c2, s2 = c_ref[...], s_ref[...]
        c8, s8 = jnp.tile(c2, (1, 4)), jnp.tile(s2, (1, 4))

        def norm_rope(x, w, c, s, bdm):
            r = lax.rsqrt(_head_sum(x * x, bdm) * (1.0 / ATT_DH) + EPS)
            y = x * r * w
            even = (lax.broadcasted_iota(jnp.int32, y.shape, 1) & 1) == 0
            return y * c + _pair_swap(y, even) * s

        _store_padded_cols(q_ref, norm_rope(aq_ref[...], qw_ref[...], c8, s8, bd) * scale, qn_ref)
        kb = norm_rope(ak_ref[...], kw_ref[...], c2, s2, bd[:ATT_KW, :ATT_KW]).astype(BF16)
        k_ref[...] = kb
        kf = kb.astype(F32)
        ksq = _head_sum(kf * kf, bd[:ATT_KW, :ATT_KW])
        kmax_ref[...] = jnp.maximum(kmax_ref[...], jnp.max(ksq, axis=0, keepdims=True))
        v_ref[...] = av_ref[...].astype(BF16)

    kv_spec = pl.BlockSpec((tm, ATT_KW), lambda i: (i, 0))
    return pl.pallas_call(
        body, name=name, grid=(T // tm,),
        in_specs=[pl.BlockSpec((tm, ATT_QW), lambda i: (i, 5)),
                  pl.BlockSpec((tm, ATT_KW), lambda i: (i, 24)), pl.BlockSpec((tm, ATT_KW), lambda i: (i, 25)),
                  kv_spec, kv_spec,
                  pl.BlockSpec((1, ATT_QW), lambda i: (0, 0)), pl.BlockSpec((1, ATT_KW), lambda i: (0, 0))],
        out_specs=[pl.BlockSpec((ATT_KV, 1, FA_HP, R), lambda i: (0, i, 0, 0)), kv_spec, kv_spec,
                   pl.BlockSpec((ATT_KV, 1, 1, R), lambda i: (0, i, 0, 0)), pl.BlockSpec((1, ATT_KW), lambda i: (0, 0))],
        out_shape=[jax.ShapeDtypeStruct((ATT_KV, T // tm, FA_HP, R), BF16),
                   jax.ShapeDtypeStruct((T, ATT_KW), BF16), jax.ShapeDtypeStruct((T, ATT_KW), BF16),
                   jax.ShapeDtypeStruct((ATT_KV, T // tm, 1, R), F32), jax.ShapeDtypeStruct((1, ATT_KW), F32)],
        compiler_params=_params(("arbitrary",)),
    )(U, U, U, cos, sin, qw, kw)


def _att_prep_bwd2(U, dq_c, dk, dv, cos, sin, qw, kw, *, name):
    T = U.shape[0]
    tm = min(FA_TQ, T)
    R = ATT_G * tm
    scale = ATT_DH ** -0.5

    def body(aq_ref, ak_ref, dq_ref, dk_ref, dv_ref, c_ref, s_ref, qw_ref, kw_ref, out_ref, dqw_ref, dkw_ref):
        @pl.when(pl.program_id(0) == 0)
        def _():
            dqw_ref[...] = jnp.zeros_like(dqw_ref)
            dkw_ref[...] = jnp.zeros_like(dkw_ref)

        bd = _head_blockdiag(ATT_QW)
        c2, s2 = c_ref[...], s_ref[...]
        c8, s8 = jnp.tile(c2, (1, 4)), jnp.tile(s2, (1, 4))

        def bwd(x, dy, w, c, s, bdm):
            even = (lax.broadcasted_iota(jnp.int32, x.shape, 1) & 1) == 0
            dn = dy * c - _pair_swap(dy, even) * s
            r = lax.rsqrt(_head_sum(x * x, bdm) * (1.0 / ATT_DH) + EPS)
            xh = x * r
            dxh = dn * w
            t = _head_sum(dxh * xh, bdm) * (1.0 / ATT_DH)
            return r * (dxh - xh * t), jnp.sum(dn * xh, axis=0, keepdims=True)

        dq = jnp.concatenate([_tokens_from_cols(dq_ref[kv, 0]) for kv in range(ATT_KV)], axis=1)
        da, dw = bwd(aq_ref[...], dq * scale, qw_ref[...], c8, s8, bd)
        out_ref[:, 0:ATT_QW] = da.astype(BF16)
        dqw_ref[...] += dw
        da, dw = bwd(ak_ref[...], dk_ref[...], kw_ref[...], c2, s2, bd[:ATT_KW, :ATT_KW])
        out_ref[:, ATT_QW:ATT_QW + ATT_KW] = da.astype(BF16)
        dkw_ref[...] += dw
        out_ref[:, ATT_QW + ATT_KW:ATT_QW + 2 * ATT_KW] = dv_ref[...].astype(BF16)

    kv_spec = pl.BlockSpec((tm, ATT_KW), lambda i: (i, 0))
    qv = pl.BlockSpec((1, ATT_QW), lambda i: (0, 0))
    kv = pl.BlockSpec((1, ATT_KW), lambda i: (0, 0))
    w_att = ATT_QW + 2 * ATT_KW
    return pl.pallas_call(
        body, name=name, grid=(T // tm,),
        in_specs=[pl.BlockSpec((tm, ATT_QW), lambda i: (i, 5)), pl.BlockSpec((tm, ATT_KW), lambda i: (i, 24)),
                  pl.BlockSpec((ATT_KV, 1, ATT_DH, R), lambda i: (0, i, 0, 0)), kv_spec, kv_spec, kv_spec, kv_spec, qv, kv],
        out_specs=[pl.BlockSpec((tm, w_att), lambda i: (i, 0)), qv, kv],
        out_shape=[jax.ShapeDtypeStruct((T, w_att), BF16),
                   jax.ShapeDtypeStruct((1, ATT_QW), F32), jax.ShapeDtypeStruct((1, ATT_KW), F32)],
        compiler_params=_params(("arbitrary",)),
    )(U, U, dq_c, dk, dv, cos, sin, qw, kw)


def _pick_head(x, kv):
    return jnp.where(kv == 0, x[0:ATT_DH, :], x[ATT_DH:FA_HP, :])


def _flash_fwd2(q_c, k, v, *, name):
    _, nq, _, R = q_c.shape
    T = k.shape[0]
    tk = min(FA_TK_FWD, T)
    n_k = T // tk

    def body(q_ref, k_ref, v_ref, o_ref, lse_ref, acc_ref):
        kv = pl.program_id(0)
        qv = q_ref[0, 0]
        acc_ref[...] = jnp.zeros_like(acc_ref)

        def step(j, carry):
            m, l = carry
            s = jnp.dot(k_ref[j], qv, preferred_element_type=F32)
            m_new = jnp.maximum(m, jnp.max(s, axis=0, keepdims=True))
            alpha = jnp.exp(m - m_new)
            p = jnp.exp(s - m_new)
            l = alpha * l + jnp.sum(p, axis=0, keepdims=True)
            pv = lax.dot_general(v_ref[j], p.astype(BF16), (((0,), (0,)), ((), ())), preferred_element_type=F32)
            acc_ref[...] = alpha * acc_ref[...] + _pick_head(pv, kv)
            return m_new, l

        m, l = lax.fori_loop(0, n_k, step, (jnp.full((1, R), -jnp.inf, F32), jnp.zeros((1, R), F32)))
        o_ref[0, 0] = acc_ref[...] / l
        lse_ref[0, 0] = m + jnp.log(l)

    kspec = pl.BlockSpec((n_k, tk, FA_HP), lambda h, i: (0, 0, 0))
    return pl.pallas_call(
        body, name=name, grid=(ATT_KV, nq),
        in_specs=[pl.BlockSpec((1, 1, FA_HP, R), lambda h, i: (h, i, 0, 0)), kspec, kspec],
        out_specs=[pl.BlockSpec((1, 1, ATT_DH, R), lambda h, i: (h, i, 0, 0)),
                   pl.BlockSpec((1, 1, 1, R), lambda h, i: (h, i, 0, 0))],
        out_shape=[jax.ShapeDtypeStruct((ATT_KV, nq, ATT_DH, R), F32), jax.ShapeDtypeStruct((ATT_KV, nq, 1, R), F32)],
        scratch_shapes=[pltpu.VMEM((ATT_DH, R), F32)],
        compiler_params=_params(("parallel", "parallel")),
    )(q_c, k.reshape(n_k, tk, FA_HP), v.reshape(n_k, tk, FA_HP))


FA_BOUND_MAX = 40.0
FA_TK_FAST = 512


def _flash_fwd_bounded(q_c, k, v, m_c, *, name):
    _, nq, _, R = q_c.shape
    T = k.shape[0]
    tk = min(FA_TK_FAST, T)
    n_k = T // tk

    def body(q_ref, k_ref, v_ref, m_ref, o_ref, lse_ref, acc_ref):
        kv = pl.program_id(0)
        qv = q_ref[0, 0]
        m = m_ref[0, 0]
        acc_ref[...] = jnp.zeros_like(acc_ref)

        def step(j, l8):
            s = jnp.dot(k_ref[j], qv, preferred_element_type=F32)
            p = jnp.exp(s - m)
            l8 = l8 + jnp.sum(p.reshape(tk // 8, 8, R), axis=0)
            acc_ref[...] += lax.dot_general(v_ref[j], p.astype(BF16), (((0,), (0,)), ((), ())),
                                            preferred_element_type=F32)
            return l8

        l8 = lax.fori_loop(0, n_k, step, jnp.zeros((8, R), F32))
        l = jnp.sum(l8, axis=0, keepdims=True)
        o_ref[0, 0] = _pick_head(acc_ref[...], kv) / l
        lse_ref[0, 0] = m + jnp.log(l)

    kspec = pl.BlockSpec((n_k, tk, FA_HP), lambda h, i: (0, 0, 0))
    vspec = pl.BlockSpec((1, 1, 1, R), lambda h, i: (h, i, 0, 0))
    return pl.pallas_call(
        body, name=name, grid=(ATT_KV, nq),
        in_specs=[pl.BlockSpec((1, 1, FA_HP, R), lambda h, i: (h, i, 0, 0)), kspec, kspec, vspec],
        out_specs=[pl.BlockSpec((1, 1, ATT_DH, R), lambda h, i: (h, i, 0, 0)), vspec],
        out_shape=[jax.ShapeDtypeStruct((ATT_KV, nq, ATT_DH, R), F32), jax.ShapeDtypeStruct((ATT_KV, nq, 1, R), F32)],
        scratch_shapes=[pltpu.VMEM((FA_HP, R), F32)],
        compiler_params=_params(("parallel", "parallel")),
    )(q_c, k.reshape(n_k, tk, FA_HP), v.reshape(n_k, tk, FA_HP), m_c)


def _flash_bwd2(q_c, k, v, do_c, lse, delta, *, name):
    _, nq, _, R = q_c.shape
    T = k.shape[0]
    tk = min(FA_TK_BWD, T)
    n_k = T // tk

    def body(qc_ref, k_ref, v_ref, doc_ref, lse_ref, delta_ref, dq_ref, dk_ref, dv_ref, acc_ref):
        kv = pl.program_id(0)

        @pl.when((kv == 0) & (pl.program_id(1) == 0))
        def _():
            dk_ref[...] = jnp.zeros_like(dk_ref)
            dv_ref[...] = jnp.zeros_like(dv_ref)

        qc, doc = qc_ref[0, 0], doc_ref[0, 0]
        lsev, delta = lse_ref[0, 0], delta_ref[0, 0]
        acc_ref[...] = jnp.zeros_like(acc_ref)

        def step(j, carry):
            kb = k_ref[j]
            s = jnp.dot(kb, qc, preferred_element_type=F32)
            p = jnp.exp(s - lsev)
            dp = jnp.dot(v_ref[j], doc, preferred_element_type=F32)
            ds = (p * (dp - delta)).astype(BF16)
            acc_ref[...] += lax.dot_general(kb, ds, (((0,), (0,)), ((), ())), preferred_element_type=F32)
            dk_ref[j] += lax.dot_general(ds, qc, (((1,), (1,)), ((), ())), preferred_element_type=F32)
            dv_ref[j] += lax.dot_general(p.astype(BF16), doc, (((1,), (1,)), ((), ())), preferred_element_type=F32)
            return carry

        lax.fori_loop(0, n_k, step, 0)
        dq_ref[0, 0] = _pick_head(acc_ref[...], kv)

    cspec = pl.BlockSpec((1, 1, FA_HP, R), lambda h, i: (h, i, 0, 0))
    vspec = pl.BlockSpec((1, 1, 1, R), lambda h, i: (h, i, 0, 0))
    kspec = pl.BlockSpec((n_k, tk, FA_HP), lambda h, i: (0, 0, 0))
    dq_c, dk, dv = pl.pallas_call(
        body, name=name, grid=(ATT_KV, nq),
        in_specs=[cspec, kspec, kspec, cspec, vspec, vspec],
        out_specs=[pl.BlockSpec((1, 1, ATT_DH, R), lambda h, i: (h, i, 0, 0)), kspec, kspec],
        out_shape=[jax.ShapeDtypeStruct((ATT_KV, nq, ATT_DH, R), F32),
                   jax.ShapeDtypeStruct((n_k, tk, FA_HP), F32), jax.ShapeDtypeStruct((n_k, tk, FA_HP), F32)],
        scratch_shapes=[pltpu.VMEM((FA_HP, R), F32)],
        compiler_params=_params(("arbitrary", "arbitrary")),
    )(q_c, k.reshape(n_k, tk, FA_HP), v.reshape(n_k, tk, FA_HP), do_c, lse, delta)
    return dq_c, dk.reshape(T, FA_HP), dv.reshape(T, FA_HP)


def _att_post_fwd2(o_c, w, *, name):
    _, nq, _, R = o_c.shape
    tm = R // ATT_G
    T = nq * tm

    def body(oc_ref, w_ref, o_ref, out_ref):
        ov = jnp.concatenate([_tokens_from_cols(oc_ref[kv, 0]) for kv in range(ATT_KV)], axis=1)
        r = lax.rsqrt(jnp.mean(ov * ov, axis=-1, keepdims=True) + EPS)
        o_ref[...] = ov
        out_ref[...] = (ov * r * w_ref[...]).astype(BF16)

    tok = pl.BlockSpec((tm, ATT_QW), lambda i: (i, 0))
    return pl.pallas_call(
        body, name=name, grid=(nq,),
        in_specs=[pl.BlockSpec((ATT_KV, 1, ATT_DH, R), lambda i: (0, i, 0, 0)), pl.BlockSpec((1, ATT_QW), lambda i: (0, 0))],
        out_specs=[tok, tok],
        out_shape=[jax.ShapeDtypeStruct((T, ATT_QW), F32), jax.ShapeDtypeStruct((T, ATT_QW), BF16)],
        compiler_params=_params(("parallel",)),
    )(o_c, w)


def _att_post_bwd2(dmix, o, w, *, name):
    T = o.shape[0]
    tm = min(FA_TQ, T)
    R = ATT_G * tm

    def body(dm_ref, o_ref, w_ref, do_ref, delta_ref, dw_ref):
        @pl.when(pl.program_id(0) == 0)
        def _():
            dw_ref[...] = jnp.zeros_like(dw_ref)

        ov = o_ref[...]
        r = lax.rsqrt(jnp.mean(ov * ov, axis=-1, keepdims=True) + EPS)
        xh = ov * r
        dm = dm_ref[...]
        dxh = dm * w_ref[...]
        t = jnp.mean(dxh * xh, axis=-1, keepdims=True)
        do = r * (dxh - xh * t)
        _store_padded_cols(do_ref, do)
        dob = do.astype(BF16).astype(F32)
        for kv in range(ATT_KV):
            delta_ref[kv, 0] = jnp.sum(_cols_from_tokens(dob * ov, kv), axis=0, keepdims=True)
        dw_ref[...] += jnp.sum(dm * xh, axis=0, keepdims=True)

    tok = pl.BlockSpec((tm, ATT_QW), lambda i: (i, 0))
    vec = pl.BlockSpec((1, ATT_QW), lambda i: (0, 0))
    return pl.pallas_call(
        body, name=name, grid=(T // tm,),
        in_specs=[pl.BlockSpec((tm, ATT_QW), lambda i: (i, 1)), tok, vec],
        out_specs=[pl.BlockSpec((ATT_KV, 1, FA_HP, R), lambda i: (0, i, 0, 0)),
                   pl.BlockSpec((ATT_KV, 1, 1, R), lambda i: (0, i, 0, 0)), vec],
        out_shape=[jax.ShapeDtypeStruct((ATT_KV, T // tm, FA_HP, R), BF16),
                   jax.ShapeDtypeStruct((ATT_KV, T // tm, 1, R), F32), jax.ShapeDtypeStruct((1, ATT_QW), F32)],
        compiler_params=_params(("arbitrary",)),
    )(dmix, o, w)


def _ffn_up(h2, wg, wu, *, name, tm=512):
    T = h2.shape[0]
    tn = _pick(D_FF, 1408)

    def body(h_ref, wg_ref, wu_ref, g_ref, u_ref, a_ref):
        hv = h_ref[...]
        g = jnp.dot(hv, wg_ref[...], preferred_element_type=F32)
        u = jnp.dot(hv, wu_ref[...], preferred_element_type=F32)
        g_ref[...] = g.astype(BF16)
        u_ref[...] = u.astype(BF16)
        a_ref[...] = (g * _sigmoid(g) * u).astype(BF16)

    wspec = pl.BlockSpec((D_MODEL, tn), lambda i, j: (0, j))
    ospec = pl.BlockSpec((tm, tn), lambda i, j: (i, j))
    return pl.pallas_call(
        body, name=name, grid=(T // tm, D_FF // tn),
        in_specs=[pl.BlockSpec((tm, D_MODEL), lambda i, j: (i, 0)), wspec, wspec],
        out_specs=[ospec] * 3, out_shape=[jax.ShapeDtypeStruct((T, D_FF), BF16)] * 3,
        compiler_params=_params(("parallel", "arbitrary")),
    )(h2, wg, wu)


def _ffn_act_bwd(dx2b, w_down, gate, up, *, name, tm=512):
    T = dx2b.shape[0]
    tn = _pick(D_FF, 1408)

    def body(dx_ref, w_ref, g_ref, u_ref, dg_ref, du_ref):
        da = lax.dot_general(dx_ref[...], w_ref[...], (((1,), (1,)), ((), ())), preferred_element_type=F32)
        g = g_ref[...].astype(F32)
        u = u_ref[...].astype(F32)
        sg = _sigmoid(g)
        dg_ref[...] = (da * u * (sg * (1.0 + g * (1.0 - sg)))).astype(BF16)
        du_ref[...] = (da * (g * sg)).astype(BF16)

    ospec = pl.BlockSpec((tm, tn), lambda i, j: (i, j))
    return pl.pallas_call(
        body, name=name, grid=(T // tm, D_FF // tn),
        in_specs=[pl.BlockSpec((tm, D_MODEL), lambda i, j: (i, 0)),
                  pl.BlockSpec((tn, D_MODEL), lambda i, j: (j, 0)), ospec, ospec],
        out_specs=[ospec] * 2, out_shape=[jax.ShapeDtypeStruct((T, D_FF), BF16)] * 2,
        compiler_params=_params(("parallel", "arbitrary")),
    )(dx2b, w_down, gate, up)


def _assemble_du(U, dq_f, dq_b, dz_f, dz_b, dv_f, dv_b, du_g, da_q, da_k, da_v, *, name, tm=256):
    T = U.shape[0]

    def body(uq_ref, dqf, dqb, dzf, dzb, dvf, dvb, dug, daq, dak, dav, out_ref):
        uq = uq_ref[...]
        sg = _sigmoid(uq)
        out_ref[:, 0:HG_W] = ((dqf[...] + dqb[...]) * (sg * (1.0 + uq * (1.0 - sg)))).astype(BF16)
        out_ref[:, HG_W:2 * HG_W] = dzf[...].astype(BF16)
        out_ref[:, 2 * HG_W:3 * HG_W] = dzb[...].astype(BF16)
        out_ref[:, 3 * HG_W:4 * HG_W] = (dvf[...] + dvb[...]).astype(BF16)
        out_ref[:, 4 * HG_W:5 * HG_W] = dug[...].astype(BF16)
        out_ref[:, 5 * HG_W:5 * HG_W + ATT_QW] = daq[...].astype(BF16)
        out_ref[:, 5 * HG_W + ATT_QW:5 * HG_W + ATT_QW + ATT_KW] = dak[...].astype(BF16)
        out_ref[:, 5 * HG_W + ATT_QW + ATT_KW:D_IN] = dav[...].astype(BF16)

    tok = pl.BlockSpec((tm, HG_W), lambda i: (i, 0))
    kv = pl.BlockSpec((tm, ATT_KW), lambda i: (i, 0))
    return pl.pallas_call(
        body, name=name, grid=(T // tm,),
        in_specs=[tok] * 9 + [kv, kv],
        out_specs=pl.BlockSpec((tm, D_IN), lambda i: (i, 0)),
        out_shape=jax.ShapeDtypeStruct((T, D_IN), BF16),
        compiler_params=_params(("parallel",)),
    )(U, dq_f, dq_b, dz_f, dz_b, dv_f, dv_b, du_g, da_q, da_k, da_v)


def _adam_math(w, g, m, v):
    m = ADAM_B1 * m + (1.0 - ADAM_B1) * g
    v = ADAM_B2 * v + (1.0 - ADAM_B2) * (g * g)
    m_hat = m / (1.0 - ADAM_B1 ** ADAM_STEP)
    v_hat = v / (1.0 - ADAM_B2 ** ADAM_STEP)
    delta = -ADAM_LR * (m_hat / (jnp.sqrt(v_hat) + ADAM_EPS) + ADAM_WD * w)
    return delta, m, v


def _adamw(parts, w, m, v, *, name, tr_cap=256):
    P, R, C = parts.shape
    tr = R
    for t in range(8, min(R, tr_cap) + 1, 8):
        if R % t == 0:
            tr = t

    def body(p_ref, w_ref, m_ref, v_ref, g_ref, d_ref, nm_ref, nv_ref):
        g = p_ref[0].astype(F32)
        for j in range(1, P):
            g = g + p_ref[j].astype(F32)
        d, nm, nv = _adam_math(w_ref[...], g, m_ref[...], v_ref[...])
        g_ref[...] = g
        d_ref[...] = d
        nm_ref[...] = nm
        nv_ref[...] = nv

    blk = pl.BlockSpec((tr, C), lambda i: (i, 0))
    return pl.pallas_call(
        body, name=name, grid=(R // tr,),
        in_specs=[pl.BlockSpec((P, tr, C), lambda i: (0, i, 0)), blk, blk, blk],
        out_specs=[blk] * 4, out_shape=[jax.ShapeDtypeStruct((R, C), F32)] * 4,
        compiler_params=_params(("parallel",)),
    )(parts, w, m, v)


def _all_gather(xs, *, name):
    n = len(xs)

    def body(*refs):
        ins, outs = refs[:n], refs[n:2 * n]
        send_sems, recv_sems, local_sems = refs[2 * n:]
        x, y, c = lax.axis_index("x"), lax.axis_index("y"), lax.axis_index("c")
        me, sibling = (x, y, c), (x, y, 1 - c)
        chips = [(1 - x, y), (x, 1 - y), (1 - x, 1 - y)]

        def slot(p):
            return 4 * p[0] + 2 * p[1] + p[2]

        def copy(a, k, block, to, src=None):
            dst = outs[a].at[slot(block)]
            return pltpu.make_async_remote_copy(
                src_ref=dst if src is None else src, dst_ref=dst,
                send_sem=send_sems.at[a * 7 + k], recv_sem=recv_sems.at[a * 7 + k],
                device_id=to, device_id_type=MESH)

        mine = [pltpu.make_async_copy(ins[a], outs[a].at[slot(me)], local_sems.at[a]) for a in range(n)]
        for cp in mine:
            cp.start()
        first = []
        for a in range(n):
            first.append(copy(a, 0, me, sibling, src=ins[a]))
            first += [copy(a, 1 + j, me, (*chip, c), src=ins[a]) for j, chip in enumerate(chips)]
        for cp in first:
            cp.start()
        passed = []
        for j, chip in enumerate(chips):
            for a in range(n):
                copy(a, 1 + j, (*chip, c), me).wait_recv()
                cp = copy(a, 4 + j, (*chip, c), sibling)
                cp.start()
                passed.append(cp)
        for a in range(n):
            copy(a, 0, sibling, me).wait_recv()
            for j, chip in enumerate(chips):
                copy(a, 4 + j, (*chip, 1 - c), me).wait_recv()
        for cp in first + passed:
            cp.wait_send()
        for cp in mine:
            cp.wait()

    return pl.pallas_call(
        body, name=name,
        in_specs=[ANY] * n, out_specs=[ANY] * n,
        out_shape=[jax.ShapeDtypeStruct((N_DEV,) + x.shape, x.dtype) for x in xs],
        scratch_shapes=[pltpu.SemaphoreType.DMA((7 * n,)), pltpu.SemaphoreType.DMA((7 * n,)),
                        pltpu.SemaphoreType.DMA((n,))],
        compiler_params=pltpu.CompilerParams(has_side_effects=True),
    )(*xs)


ALL_MASKS = [(mx, my, mc) for mx in (0, 1) for my in (0, 1) for mc in (0, 1)][1:]


def _exchange(gs, *, masks, slot, name, bcast=None):
    n, n_peers = len(gs), len(masks)
    has_bcast = bcast is not None

    def body(*refs):
        n_in = n + has_bcast
        ins, outs = refs[:n], refs[n_in:n_in + n]
        send_sems, recv_sems, local_sems = refs[2 * n_in:2 * n_in + 3]
        x, y, c = lax.axis_index("x"), lax.axis_index("y"), lax.axis_index("c")
        my_slot = slot((x, y, c))

        def flip(v, bit):
            return 1 - v if bit else v

        mine = [pltpu.make_async_copy(ins[a].at[my_slot], outs[a].at[my_slot], local_sems.at[a]) for a in range(n)]
        copies = []
        if has_bcast:
            b_in, b_out = refs[n], refs[2 * n_in - 1]
            b_send, b_recv = refs[2 * n_in + 3:]
            me = 4 * x + 2 * y + c
            mine.append(pltpu.make_async_copy(b_in, b_out.at[me], local_sems.at[n]))
            for k, (mx, my, mc) in enumerate(ALL_MASKS):
                peer = (flip(x, mx), flip(y, my), flip(c, mc))
                peer_id = 4 * peer[0] + 2 * peer[1] + peer[2]
                sems = dict(send_sem=b_send.at[k], recv_sem=b_recv.at[k], device_id=peer, device_id_type=MESH)
                copies.append((pltpu.make_async_remote_copy(src_ref=b_in, dst_ref=b_out.at[me], **sems),
                               pltpu.make_async_remote_copy(src_ref=b_in, dst_ref=b_out.at[peer_id], **sems)))
        for cp in mine:
            cp.start()
        for a in range(n):
            for k, (mx, my, mc) in enumerate(masks):
                peer = (flip(x, mx), flip(y, my), flip(c, mc))
                peer_slot = slot(peer)
                sems = dict(send_sem=send_sems.at[a * n_peers + k], recv_sem=recv_sems.at[a * n_peers + k],
                            device_id=peer, device_id_type=MESH)
                copies.append((
                    pltpu.make_async_remote_copy(src_ref=ins[a].at[peer_slot], dst_ref=outs[a].at[my_slot], **sems),
                    pltpu.make_async_remote_copy(src_ref=ins[a].at[peer_slot], dst_ref=outs[a].at[peer_slot], **sems)))
        for send, _ in copies:
            send.start()
        for send, recv in copies:
            recv.wait_recv()
            send.wait_send()
        for cp in mine:
            cp.wait()

    n_io = n + has_bcast
    out_shape = [jax.ShapeDtypeStruct(g.shape, g.dtype) for g in gs]
    scratch = [pltpu.SemaphoreType.DMA((n_peers * n,)), pltpu.SemaphoreType.DMA((n_peers * n,)),
               pltpu.SemaphoreType.DMA((n_io,))]
    if has_bcast:
        out_shape.append(jax.ShapeDtypeStruct((N_DEV,) + bcast.shape, bcast.dtype))
        scratch += [pltpu.SemaphoreType.DMA((len(ALL_MASKS),)), pltpu.SemaphoreType.DMA((len(ALL_MASKS),))]
    return pl.pallas_call(
        body, name=name,
        in_specs=[ANY] * n_io, out_specs=[ANY] * n_io, out_shape=out_shape, scratch_shapes=scratch,
        compiler_params=pltpu.CompilerParams(has_side_effects=True),
    )(*gs, *([bcast] if has_bcast else []))


SWAP_ROW_CHUNKS = 4


def _core_swap(gs, *, name):
    n = len(gs)

    def body(*refs):
        ins, outs = refs[:n], refs[n:2 * n]
        send_sems, recv_sems = refs[2 * n:]
        x, y, c = lax.axis_index("x"), lax.axis_index("y"), lax.axis_index("c")
        sibling = (x, y, 1 - c)
        started = []
        for a in range(n):
            _, Q, R, _ = ins[a].shape
            rows = R // SWAP_ROW_CHUNKS
            for q in range(Q):
                for j in range(SWAP_ROW_CHUNKS):
                    cp = pltpu.make_async_remote_copy(
                        src_ref=ins[a].at[1 - c, q, pl.ds(j * rows, rows)], dst_ref=outs[a].at[q, pl.ds(j * rows, rows)],
                        send_sem=send_sems.at[a], recv_sem=recv_sems.at[a], device_id=sibling, device_id_type=MESH)
                    cp.start()
                    started.append(cp)
        for a in range(n):
            pltpu.make_async_remote_copy(
                src_ref=ins[a].at[1 - c], dst_ref=outs[a], send_sem=send_sems.at[a], recv_sem=recv_sems.at[a],
                device_id=sibling, device_id_type=MESH).wait()

    return pl.pallas_call(
        body, name=name,
        in_specs=[ANY] * n, out_specs=[ANY] * n,
        out_shape=[jax.ShapeDtypeStruct(g.shape[1:], g.dtype) for g in gs],
        scratch_shapes=[pltpu.SemaphoreType.DMA((n,)), pltpu.SemaphoreType.DMA((n,))],
        compiler_params=pltpu.CompilerParams(has_side_effects=True),
    )(*gs)


def _pair_sum(g, other, core, *, name, tr_cap=256):
    _, Q, R, C = g.shape
    tr = max(t for t in range(16, min(R, tr_cap) + 1, 16) if R % t == 0)

    def body(core_ref, g_ref, o_ref, out_ref):
        out_ref[0] = (g_ref[0, 0] + o_ref[0]).astype(BF16)

    return pl.pallas_call(
        body, name=name,
        grid_spec=pltpu.PrefetchScalarGridSpec(
            num_scalar_prefetch=1, grid=(Q, R // tr),
            in_specs=[pl.BlockSpec((1, 1, tr, C), lambda q, i, core_ref: (core_ref[0], q, i, 0)),
                      pl.BlockSpec((1, tr, C), lambda q, i, core_ref: (q, i, 0))],
            out_specs=pl.BlockSpec((1, tr, C), lambda q, i, core_ref: (q, i, 0))),
        out_shape=jax.ShapeDtypeStruct((Q, R, C), BF16),
        compiler_params=_params(("parallel", "parallel")),
    )(core, g, other)


PACK_ROWS = 8


def _pack_small(norm1, norm2, final, att, hg, qn, kn, lb=None, loss=None):
    z = lambda n: jnp.zeros((n,), F32)
    rows = [norm1.reshape(-1), norm2.reshape(-1), final.reshape(-1),
            jnp.concatenate([att.reshape(-1), z(512)]),
            jnp.concatenate([hg.reshape(-1), qn.reshape(-1), kn.reshape(-1), z(1024 - 256)]),
            z(1024) if lb is None else lb.reshape(-1),
            z(1024) if loss is None else jnp.concatenate([loss.reshape(-1), z(1023)]), z(1024)]
    return jnp.stack(rows, axis=0)


def _unpack_small(p):
    return (p[0:1, :], p[1:2, :], p[2, :], p[3:4, 0:512], p[4:5, 0:128], p[4:5, 128:192], p[4:5, 192:256])


def _fold_heads(dhg, dqn, dkn, *, name):
    def body(hg_ref, q_ref, k_ref, ohg_ref, oq_ref, ok_ref):
        def fold128(v):
            acc = v[:, 0:LANES]
            for j in range(1, v.shape[1] // LANES):
                acc = acc + v[:, j * LANES:(j + 1) * LANES]
            return acc

        ohg_ref[...] = fold128(hg_ref[...])
        q = fold128(q_ref[...])
        oq_ref[...] = q + pltpu.roll(q, ATT_DH, 1)
        k = k_ref[...]
        ok_ref[...] = k + pltpu.roll(k, ATT_DH, 1)

    return pl.pallas_call(body, name=name, out_shape=[jax.ShapeDtypeStruct((1, LANES), F32)] * 3)(dhg, dqn, dkn)


def _lb_grad(dlb_sum, lb, *, name):
    def body(d_ref, lb_ref, o_ref):
        lbv = lb_ref[...]
        gl = d_ref[...] * lbv * (1.0 - lbv)
        o_ref[0:1, :] = gl[0:1, :]
        o_ref[1:2, :] = -gl[0:1, :]
        o_ref[2:3, :] = gl[1:2, :]
        o_ref[3:4, :] = -gl[1:2, :]

    return pl.pallas_call(body, name=name, out_shape=jax.ShapeDtypeStruct((4, HG_W), F32))(dlb_sum, lb)


def _lower_bounds(lb_logits_full, *, name):
    def body(l_ref, o_ref):
        for d in range(2):
            l0, l1 = l_ref[2 * d:2 * d + 1, :], l_ref[2 * d + 1:2 * d + 2, :]
            mx = jnp.maximum(l0, l1)
            e0, e1 = jnp.exp(l0 - mx), jnp.exp(l1 - mx)
            o_ref[d:d + 1, :] = e0 / (e0 + e1)

    return pl.pallas_call(body, name=name, out_shape=jax.ShapeDtypeStruct((2, HG_W), F32))(
        lb_logits_full.reshape(4, HG_W))


def _local_step(x, target, norm1_w, w_in, lb, hg_norm_w, q_norm_w, k_norm_w, att_norm_w, w_out, norm2_w,
                w_g, w_u, w_down, final_norm_w):
    T = x.shape[0]
    cos, sin = _rope_tables(T)
    qw8 = jnp.tile(q_norm_w, (1, ATT_HEADS))
    kw2 = jnp.tile(k_norm_w, (1, ATT_KV))

    h, r1 = _rms_fwd(x, norm1_w, name="norm1_fwd")
    U = _mm_nn([(h, w_in)], name="in_proj")
    o_f, st_f = _gla_fwd(U, lb[0:1], f_block=1, reverse=False, name="gla_fwd_f")
    o_b, st_b = _gla_fwd(U, lb[1:2], f_block=2, reverse=True, name="gla_fwd_b")
    mix_hg = _hg_post_fwd(o_f, o_b, U, hg_norm_w, name="hg_post_fwd")
    q_c, k, v, qn_c, kmax2 = _att_prep_fwd2(U, cos, sin, qw8, kw2, name="att_prep_fwd")
    kmax = jnp.sqrt(jnp.max(kmax2.reshape(ATT_KV, ATT_DH), axis=1))
    m_c = qn_c * (kmax * 1.001).reshape(ATT_KV, 1, 1, 1)
    o_c, lse = lax.cond(jnp.max(m_c) <= FA_BOUND_MAX,
                        lambda: _flash_fwd_bounded(q_c, k, v, m_c, name="flash_fwd_bounded"),
                        lambda: _flash_fwd2(q_c, k, v, name="flash_fwd"))
    o_att, mix_att = _att_post_fwd2(o_c, att_norm_w, name="att_post_fwd")
    x1 = _mm_nn([(mix_hg, w_out[:HG_W]), (mix_att, w_out[HG_W:])], residual=x, name="out_proj")
    h2, r2 = _rms_fwd(x1, norm2_w, name="norm2_fwd")
    gate, up, act = _ffn_up(h2, w_g, w_u, name="ffn_up")
    x2 = _mm_nn([(act, w_down)], residual=x1, name="ffn_down")
    loss, dx2, dx2b, d_final = _loss_head(x2, target, final_norm_w.reshape(1, D_MODEL), name="loss_head")

    d_gate, d_up = _ffn_act_bwd(dx2b, w_down, gate, up, name="ffn_act_bwd")
    dw_down = _mm_tn(act, dx2b, tma_cap=1408, name="dw_down")
    dh2 = _mm_nn([(d_gate, w_g), (d_up, w_u)], trans_b=True, tm=256, name="ffn_up_bwd")
    dw_g = _mm_tn(h2, d_gate, tnb_cap=1408, name="dw_gate")
    dw_u = _mm_tn(h2, d_up, tnb_cap=1408, name="dw_up")
    dx1, dx1b, d_norm2 = _rms_bwd(dh2, x1, r2, norm2_w, dx2, emit_bf16=True, name="norm2_bwd")
    dmix = _mm_nn([(dx1b, w_out)], trans_b=True, name="out_proj_bwd")
    dw_out = jnp.concatenate([_mm_tn(mix_hg, dx1b, name="dw_out_hg"), _mm_tn(mix_att, dx1b, name="dw_out_att")], axis=0)
    do_c, delta, d_att = _att_post_bwd2(dmix, o_att, att_norm_w, name="att_post_bwd")
    dq_c, dk, dv = _flash_bwd2(q_c, k, v, do_c, lse, delta, name="flash_bwd")
    dU_att, d_qn, d_kn = _att_prep_bwd2(U, dq_c, dk, dv, cos, sin, qw8, kw2, name="att_prep_bwd")
    do_hg, du_g, d_hg = _hg_post_bwd(dmix, o_f, o_b, U, hg_norm_w, name="hg_post_bwd")
    dq_f, dz_f, dv_f, dlb_f = _gla_bwd(U, lb[0:1], do_hg, st_f, f_block=1, reverse=False, name="gla_bwd_f")
    dU_hg, dlb_b = _gla_bwd(U, lb[1:2], do_hg, st_b, f_block=2, reverse=True, prev=(dq_f, dz_f, dv_f, du_g),
                            name="gla_bwd_b")
    w_hg = 5 * HG_W
    dh = _mm_nn([(dU_hg, w_in[:, :w_hg]), (dU_att, w_in[:, w_hg:])], trans_b=True, name="in_proj_bwd")
    dw_in = jnp.concatenate([_mm_tn(h, dU_hg, tnb_cap=1280, name="dw_in_hg"), _mm_tn(h, dU_att, name="dw_in_att")],
                            axis=1)
    grad_x, d_norm1 = _rms_bwd(dh, x, r1, norm1_w, dx1, emit_bf16=False, name="norm1_bwd")
    d_hg, d_qn, d_kn = _fold_heads(d_hg, d_qn, d_kn, name="fold_heads")

    big = dict(w_in=dw_in, w_out=dw_out, w_g=dw_g, w_u=dw_u, w_down=dw_down)
    small = dict(norm1=d_norm1, norm2=d_norm2, final=d_final, att=d_att, hg=d_hg,
                 qn=d_qn[:, :ATT_DH], kn=d_kn[:, :ATT_DH], lb=jnp.concatenate([dlb_f, dlb_b], axis=0))
    return loss, grad_x, big, small


def kernel(x, norm1_w, w_in, lb_logits, hg_norm_w, q_norm_w, k_norm_w, att_norm_w, w_out, norm2_w, w_gate_up, w_down, final_norm_w, loss_target, m_norm1_w, m_w_in, m_lb_logits, m_hg_norm_w, m_q_norm_w, m_k_norm_w, m_att_norm_w, m_w_out, m_norm2_w, m_w_gate_up, m_w_down, m_final_norm_w, v_norm1_w, v_w_in, v_lb_logits, v_hg_norm_w, v_q_norm_w, v_k_norm_w, v_att_norm_w, v_w_out, v_norm2_w, v_w_gate_up, v_w_down, v_final_norm_w):
    T = x.shape[1]
    me = 4 * lax.axis_index("x") + 2 * lax.axis_index("y") + lax.axis_index("c")
    c_in, r_out, c_gu, r_dn = w_in.shape[2], w_out.shape[1], w_gate_up.shape[2], w_down.shape[1]
    lb_cols = lb_logits.shape[2]

    g_in, g_out, g_gu, g_dn, g_lb = _all_gather(
        [w_in[0].astype(BF16), w_out[0].astype(BF16), w_gate_up[0].astype(BF16), w_down[0].astype(BF16),
         lb_logits.reshape(4, lb_cols)], name="gather_weights")
    w_in_f = g_in.transpose(1, 0, 2).reshape(D_MODEL, N_DEV * c_in)
    w_out_f = g_out.reshape(N_DEV * r_out, D_MODEL)
    half = N_DEV // 2
    w_g_f = g_gu[:half].transpose(1, 0, 2).reshape(D_MODEL, half * c_gu)
    w_u_f = g_gu[half:].transpose(1, 0, 2).reshape(D_MODEL, half * c_gu)
    w_dn_f = g_dn.reshape(N_DEV * r_dn, D_MODEL)
    lb_logits_f = g_lb.transpose(1, 0, 2).reshape(2, 2, N_DEV * lb_cols)
    lb = _lower_bounds(lb_logits_f, name="lower_bounds")

    loss, grad_x, big, small = _local_step(
        x[0], loss_target[0], norm1_w, w_in_f, lb, hg_norm_w, q_norm_w, k_norm_w, att_norm_w, w_out_f, norm2_w,
        w_g_f, w_u_f, w_dn_f, final_norm_w)

    chips = N_DEV // 2
    by_owner_cols = lambda g, n_q, w: g.reshape(D_MODEL, n_q, 2, w).transpose(2, 1, 0, 3)
    by_owner_rows = lambda g, r: g.reshape(chips, 2, r, D_MODEL).transpose(1, 0, 2, 3)
    s_in = by_owner_cols(big["w_in"], chips, c_in)
    s_out = by_owner_rows(big["w_out"], r_out)
    s_gu = jnp.concatenate([by_owner_cols(big["w_g"], chips // 2, c_gu), by_owner_cols(big["w_u"], chips // 2, c_gu)],
                           axis=1)
    s_dn = by_owner_rows(big["w_down"], r_dn)
    mine = [s_in, s_out, s_gu, s_dn]
    theirs = _core_swap(mine, name="exchange_cores")
    core = lax.axis_index("c").astype(jnp.int32).reshape(1)
    chip_sums = [_pair_sum(g, o, core, name="pair_sum_" + nm)
                 for g, o, nm in zip(mine, theirs, ("w_in", "w_out", "w_gu", "w_down"))]

    packed = _pack_small(small["norm1"], small["norm2"], small["final"], small["att"], small["hg"],
                         small["qn"], small["kn"], small["lb"], loss)
    p_in, p_out, p_gu, p_dn, all_small = _exchange(chip_sums, masks=[(1, 0, 0), (0, 1, 0), (1, 1, 0)],
                                                   slot=lambda p: 2 * p[0] + p[1], bcast=packed,
                                                   name="exchange_chips")

    g_w_in, d_w_in, nm_w_in, nv_w_in = _adamw(p_in, w_in[0], m_w_in[0], v_w_in[0], name="adamw_w_in")
    g_w_out, d_w_out, nm_w_out, nv_w_out = _adamw(p_out, w_out[0], m_w_out[0], v_w_out[0], name="adamw_w_out")
    g_w_gu, d_w_gu, nm_w_gu, nv_w_gu = _adamw(p_gu, w_gate_up[0], m_w_gate_up[0], v_w_gate_up[0], name="adamw_w_gu")
    g_w_dn, d_w_dn, nm_w_dn, nv_w_dn = _adamw(p_dn, w_down[0], m_w_down[0], v_w_down[0], name="adamw_w_down")

    pk = lambda vecs: _pack_small(*vecs)
    w_pk = pk([norm1_w, norm2_w, final_norm_w, att_norm_w, hg_norm_w, q_norm_w, k_norm_w])
    m_pk = pk([m_norm1_w, m_norm2_w, m_final_norm_w, m_att_norm_w, m_hg_norm_w, m_q_norm_w, m_k_norm_w])
    v_pk = pk([v_norm1_w, v_norm2_w, v_final_norm_w, v_att_norm_w, v_hg_norm_w, v_q_norm_w, v_k_norm_w])
    g_pk, d_pk, nm_pk, nv_pk = _adamw(all_small, w_pk, m_pk, v_pk, name="adamw_small")

    dlb_sum = g_pk[5:6, :].reshape(2, HG_W)
    g_lb_full = _lb_grad(dlb_sum, lb, name="lb_grad")
    g_lb_mine = lax.dynamic_slice_in_dim(g_lb_full, me * lb_cols, lb_cols, axis=1)
    g_lb_s, d_lb, nm_lb, nv_lb = _adamw(g_lb_mine[None], lb_logits.reshape(4, lb_cols),
                                        m_lb_logits.reshape(4, lb_cols), v_lb_logits.reshape(4, lb_cols),
                                        name="adamw_lb")

    loss_total = g_pk[6, 0]

    def outs(big4, lb_arr, pk_arr):
        n1, n2, fin, att, hg, qn, kn = _unpack_small(pk_arr)
        b_in, b_out, b_gu, b_dn = big4
        return [n1, b_in[None], lb_arr.reshape(2, 2, lb_cols), hg, qn, kn, att, b_out[None], n2, b_gu[None],
                b_dn[None], fin]

    return (loss_total, grad_x[None],
            *outs((g_w_in, g_w_out, g_w_gu, g_w_dn), g_lb_s, g_pk),
            *outs((d_w_in, d_w_out, d_w_gu, d_w_dn), d_lb, d_pk),
            *outs((nm_w_in, nm_w_out, nm_w_gu, nm_w_dn), nm_lb, nm_pk),
            *outs((nv_w_in, nv_w_out, nv_w_gu, nv_w_dn), nv_lb, nv_pk))
```

```python
import functools
import math

import jax
import jax.numpy as jnp
import numpy as np
from jax import lax
from jax.experimental import pallas as pl
from jax.experimental.pallas import tpu as pltpu

F32 = jnp.float32
BF16 = jnp.bfloat16

N_DEV = 8
D_MODEL = 1024
EPS = 1e-6
HG_HEADS = 4
HG_D = 128
HG_W = HG_HEADS * HG_D
CHUNK = 64
ATT_HEADS = 8
ATT_KV = 2
ATT_G = ATT_HEADS // ATT_KV
ATT_DH = 64
ATT_QW = ATT_HEADS * ATT_DH
ATT_KW = ATT_KV * ATT_DH
GRID_W = 64
ROPE_THETA = 10000.0
D_IN = 5 * HG_W + ATT_QW + 2 * ATT_KW
D_FF = 2816
ADAM_LR, ADAM_B1, ADAM_B2, ADAM_EPS, ADAM_WD, ADAM_STEP = 0.001, 0.9, 0.999, 1e-08, 0.01, 10

LANES = 128
VMEM_LIMIT = 48 * 1024 * 1024
MESH = pl.DeviceIdType.MESH
ANY = pl.BlockSpec(memory_space=pl.ANY)


def _params(sem=None):
    return pltpu.CompilerParams(dimension_semantics=sem, vmem_limit_bytes=VMEM_LIMIT)


def _pick(n, cap):
    best = None
    for t in range(LANES, cap + 1, LANES):
        if n % t == 0:
            best = t
    assert best is not None, (n, cap)
    return best


def _sigmoid(x):
    return 1.0 / (1.0 + jnp.exp(-x))


def _dot(a, b):
    return jnp.dot(a.astype(BF16), b.astype(BF16), preferred_element_type=F32)


def _dot_nt(a, b):
    return lax.dot_general(a.astype(BF16), b.astype(BF16), (((1,), (1,)), ((), ())),
                           preferred_element_type=F32)


def _dot_tn(a, b):
    return lax.dot_general(a.astype(BF16), b.astype(BF16), (((0,), (0,)), ((), ())),
                           preferred_element_type=F32)


def _mm_nn(pairs, *, name, out_dtype=F32, residual=None, tm=512, tn_cap=None, trans_b=False):
    M = pairs[0][0].shape[0]
    N = pairs[0][1].shape[0 if trans_b else 1]
    tn = N if tn_cap is None else _pick(N, tn_cap)
    n_pairs = len(pairs)
    has_res = residual is not None
    dims = (((1,), (1,)), ((), ())) if trans_b else (((1,), (0,)), ((), ()))

    def body(*refs):
        acc = None
        for i in range(n_pairs):
            d = lax.dot_general(refs[2 * i][...], refs[2 * i + 1][...], dims, preferred_element_type=F32)
            acc = d if acc is None else acc + d
        if has_res:
            acc = acc + refs[2 * n_pairs][...]
        refs[-1][...] = acc.astype(out_dtype)

    in_specs, args = [], []
    for a, b in pairs:
        k = a.shape[1]
        b_spec = pl.BlockSpec((tn, k), lambda i, j: (j, 0)) if trans_b else pl.BlockSpec((k, tn), lambda i, j: (0, j))
        in_specs += [pl.BlockSpec((tm, k), lambda i, j: (i, 0)), b_spec]
        args += [a, b]
    if has_res:
        in_specs.append(pl.BlockSpec((tm, tn), lambda i, j: (i, j)))
        args.append(residual)
    return pl.pallas_call(
        body, name=name, grid=(M // tm, N // tn), in_specs=in_specs,
        out_specs=pl.BlockSpec((tm, tn), lambda i, j: (i, j)),
        out_shape=jax.ShapeDtypeStruct((M, N), out_dtype),
        compiler_params=_params(("parallel", "arbitrary")),
    )(*args)


def _mm_tn(a, b, *, name, tma_cap=1024, tnb_cap=1024, tk=1024):
    T, Ma = a.shape
    Nb = b.shape[1]
    tma, tnb = _pick(Ma, tma_cap), _pick(Nb, tnb_cap)
    tk = min(tk, T)
    n_k = T // tk

    def body(a_ref, b_ref, o_ref, acc_ref):
        k = pl.program_id(2)

        @pl.when(k == 0)
        def _():
            acc_ref[...] = jnp.zeros_like(acc_ref)

        acc_ref[...] += lax.dot_general(a_ref[...], b_ref[...], (((0,), (0,)), ((), ())),
                                        preferred_element_type=F32)

        @pl.when(k == n_k - 1)
        def _():
            o_ref[...] = acc_ref[...]

    return pl.pallas_call(
        body, name=name, grid=(Ma // tma, Nb // tnb, n_k),
        in_specs=[pl.BlockSpec((tk, tma), lambda i, j, k: (k, i)), pl.BlockSpec((tk, tnb), lambda i, j, k: (k, j))],
        out_specs=pl.BlockSpec((tma, tnb), lambda i, j, k: (i, j)),
        out_shape=jax.ShapeDtypeStruct((Ma, Nb), F32),
        scratch_shapes=[pltpu.VMEM((tma, tnb), F32)],
        compiler_params=_params(("parallel", "parallel", "arbitrary")),
    )(a, b)


def _rms_fwd(x, w, *, name, tm=512):
    T, Dm = x.shape

    def body(x_ref, w_ref, h_ref, r_ref):
        xv = x_ref[...]
        r = lax.rsqrt(jnp.mean(xv * xv, axis=-1, keepdims=True) + EPS)
        h_ref[...] = (xv * r * w_ref[...]).astype(BF16)
        r_ref[...] = r

    return pl.pallas_call(
        body, name=name, grid=(T // tm,),
        in_specs=[pl.BlockSpec((tm, Dm), lambda i: (i, 0)), pl.BlockSpec((1, Dm), lambda i: (0, 0))],
        out_specs=[pl.BlockSpec((tm, Dm), lambda i: (i, 0)), pl.BlockSpec((tm, 1), lambda i: (i, 0))],
        out_shape=[jax.ShapeDtypeStruct((T, Dm), BF16), jax.ShapeDtypeStruct((T, 1), F32)],
        compiler_params=_params(("parallel",)),
    )(x, w)


def _rms_bwd(dh, x, r, w, dres, *, name, emit_bf16, tm=512):
    T, Dm = x.shape

    def body(dh_ref, x_ref, r_ref, w_ref, dres_ref, *outs):
        dx_ref, dw_ref = outs[0], outs[-1]

        @pl.when(pl.program_id(0) == 0)
        def _():
            dw_ref[...] = jnp.zeros_like(dw_ref)

        rv = r_ref[...]
        xh = x_ref[...] * rv
        dhv = dh_ref[...]
        dxh = dhv * w_ref[...]
        t = jnp.mean(dxh * xh, axis=-1, keepdims=True)
        dx = dres_ref[...] + rv * (dxh - xh * t)
        dx_ref[...] = dx
        if emit_bf16:
            outs[1][...] = dx.astype(BF16)
        dw_ref[...] += jnp.sum(dhv * xh, axis=0, keepdims=True)

    row = pl.BlockSpec((tm, Dm), lambda i: (i, 0))
    vec = pl.BlockSpec((1, Dm), lambda i: (0, 0))
    out_specs = [row] + ([row] if emit_bf16 else []) + [vec]
    out_shape = ([jax.ShapeDtypeStruct((T, Dm), F32)] + ([jax.ShapeDtypeStruct((T, Dm), BF16)] if emit_bf16 else [])
                 + [jax.ShapeDtypeStruct((1, Dm), F32)])
    return pl.pallas_call(
        body, name=name, grid=(T // tm,),
        in_specs=[row, row, pl.BlockSpec((tm, 1), lambda i: (i, 0)), vec, row],
        out_specs=out_specs, out_shape=out_shape,
        compiler_params=_params(("arbitrary",)),
    )(dh, x, r, w, dres)


def _loss_head(x2, target, w, *, name, tm=512):
    T, Dm = x2.shape

    def body(x_ref, t_ref, w_ref, loss_ref, dx_ref, dxb_ref, dw_ref):
        @pl.when(pl.program_id(0) == 0)
        def _():
            loss_ref[...] = jnp.zeros_like(loss_ref)
            dw_ref[...] = jnp.zeros_like(dw_ref)

        xv = x_ref[...]
        r = lax.rsqrt(jnp.mean(xv * xv, axis=-1, keepdims=True) + EPS)
        xh = xv * r
        wv = w_ref[...]
        err = xh * wv - t_ref[...]
        row_loss = jnp.mean(err * err, axis=-1, keepdims=True)
        loss_ref[...] += 0.5 * jnp.sum(row_loss, axis=0, keepdims=True)
        dy = err * (1.0 / Dm)
        dxh = dy * wv
        t = jnp.mean(dxh * xh, axis=-1, keepdims=True)
        dx = r * (dxh - xh * t)
        dx_ref[...] = dx
        dxb_ref[...] = dx.astype(BF16)
        dw_ref[...] += jnp.sum(dy * xh, axis=0, keepdims=True)

    row = pl.BlockSpec((tm, Dm), lambda i: (i, 0))
    vec = pl.BlockSpec((1, Dm), lambda i: (0, 0))
    return pl.pallas_call(
        body, name=name, grid=(T // tm,),
        in_specs=[row, row, vec],
        out_specs=[pl.BlockSpec((1, 1), lambda i: (0, 0)), row, row, vec],
        out_shape=[jax.ShapeDtypeStruct((1, 1), F32), jax.ShapeDtypeStruct((T, Dm), F32),
                   jax.ShapeDtypeStruct((T, Dm), BF16), jax.ShapeDtypeStruct((1, Dm), F32)],
        compiler_params=_params(("arbitrary",)),
    )(x2, target, w)


GLA_TB = 512
GLA_NC = GLA_TB // CHUNK
GLA_UNROLL = 4


def _cumsum_rows(x, row, reverse):
    n = x.shape[0]
    s = 1
    while s < n:
        if not reverse:
            x = x + jnp.where(row >= s, pltpu.roll(x, s, 0), 0.0)
        else:
            x = x + jnp.where(row < n - s, pltpu.roll(x, n - s, 0), 0.0)
        s *= 2
    return x


def _gla_gates(uq, z, lbv):
    q = uq * _sigmoid(uq)
    sg = _sigmoid(z)
    sgn = _sigmoid(-z)
    f = lbv + (1.0 - lbv) * sg
    k = (1.0 - lbv) * sgn
    return q, sg, sgn, f, k


def _gla_decays(f, row, reverse):
    b = _cumsum_rows(jnp.log(f), row, reverse)
    if not reverse:
        bref, blast = b[CHUNK // 2 - 1:CHUNK // 2, :], b[CHUNK - 1:CHUNK, :]
    else:
        bref, blast = b[CHUNK // 2:CHUNK // 2 + 1, :], b[0:1, :]
    return b, bref, blast


def _gla_fwd(U, lb, *, f_block, reverse, name):
    T = U.shape[0]
    nb = T // GLA_TB

    def body(uq_ref, uf_ref, ui_ref, lb_ref, o_ref, st_ref, s_ref):
        @pl.when(pl.program_id(0) == 0)
        def _():
            s_ref[...] = jnp.zeros_like(s_ref)

        row = lax.broadcasted_iota(jnp.int32, (CHUNK, HG_D), 0)
        ri = lax.broadcasted_iota(jnp.int32, (CHUNK, CHUNK), 0)
        ci = lax.broadcasted_iota(jnp.int32, (CHUNK, CHUNK), 1)
        mask = (ri <= ci) if reverse else (ri >= ci)

        def chunk(j, carry):
            c = (GLA_NC - 1 - j) if reverse else j
            rows = pl.ds(pl.multiple_of(c * CHUNK, CHUNK), CHUNK)
            for h in range(HG_HEADS):
                cols = pl.ds(h * HG_D, HG_D)
                v = ui_ref[rows, cols]
                q, _, _, f, k = _gla_gates(uq_ref[rows, cols], uf_ref[rows, cols], lb_ref[:, cols])
                b, bref, blast = _gla_decays(f, row, reverse)
                s = jnp.where(mask, _dot_nt(q * jnp.exp(b - bref), k * jnp.exp(bref - b)), 0.0)
                st = s_ref[h]
                st_ref[c, h] = st
                o_ref[rows, cols] = _dot(s, v) + _dot_nt(q * jnp.exp(b), st)
                s_ref[h] = st * jnp.exp(blast) + _dot_tn(v, k * jnp.exp(blast - b))
            return carry

        lax.fori_loop(0, GLA_NC, chunk, 0, unroll=GLA_NC)

    blk = (lambda i: nb - 1 - i) if reverse else (lambda i: i)
    ucol = lambda cb: pl.BlockSpec((GLA_TB, HG_W), lambda i: (blk(i), cb))
    return pl.pallas_call(
        body, name=name, grid=(nb,),
        in_specs=[ucol(0), ucol(f_block), ucol(3), pl.BlockSpec((1, HG_W), lambda i: (0, 0))],
        out_specs=[pl.BlockSpec((GLA_TB, HG_W), lambda i: (blk(i), 0)),
                   pl.BlockSpec((GLA_NC, HG_HEADS, HG_D, HG_D), lambda i: (blk(i), 0, 0, 0))],
        out_shape=[jax.ShapeDtypeStruct((T, HG_W), F32),
                   jax.ShapeDtypeStruct((T // CHUNK, HG_HEADS, HG_D, HG_D), F32)],
        scratch_shapes=[pltpu.VMEM((HG_HEADS, HG_D, HG_D), F32)],
        compiler_params=_params(("arbitrary",)),
    )(U, U, U, lb)


def _gla_bwd(U, lb, do, states, *, f_block, reverse, name, prev=None):
    T = U.shape[0]
    nb = T // GLA_TB
    final = prev is not None

    def body(uq_ref, uf_ref, ui_ref, lb_ref, do_ref, st_ref, *rest):
        if final:
            dqp_ref, dzp_ref, dvp_ref, dug_ref, out_ref, dlb_ref, ds_ref = rest
        else:
            dq_ref, dz_ref, dv_ref, dlb_ref, ds_ref = rest

        @pl.when(pl.program_id(0) == 0)
        def _():
            ds_ref[...] = jnp.zeros_like(ds_ref)
            dlb_ref[...] = jnp.zeros_like(dlb_ref)

        row = lax.broadcasted_iota(jnp.int32, (CHUNK, HG_D), 0)
        ri = lax.broadcasted_iota(jnp.int32, (CHUNK, CHUNK), 0)
        ci = lax.broadcasted_iota(jnp.int32, (CHUNK, CHUNK), 1)
        mask = (ri <= ci) if reverse else (ri >= ci)

        def chunk(j, carry):
            c = j if reverse else (GLA_NC - 1 - j)
            rows = pl.ds(pl.multiple_of(c * CHUNK, CHUNK), CHUNK)
            for h in range(HG_HEADS):
                cols = pl.ds(h * HG_D, HG_D)
                v = ui_ref[rows, cols]
                lbv = lb_ref[:, cols]
                uq = uq_ref[rows, cols]
                q, sg, sgn, f, k = _gla_gates(uq, uf_ref[rows, cols], lbv)
                b, bref, blast = _gla_decays(f, row, reverse)
                eq, ek, eb, el, dec = (jnp.exp(b - bref), jnp.exp(bref - b), jnp.exp(b), jnp.exp(blast - b),
                                       jnp.exp(blast))
                qin, kin, qb, klast = q * eq, k * ek, q * eb, k * el
                dov = do_ref[rows, cols]
                st = st_ref[c, h]
                dst = ds_ref[h]
                p = jnp.where(mask, _dot_nt(qin, kin), 0.0)
                dp = jnp.where(mask, _dot_nt(dov, v), 0.0)
                dqin = _dot(dp, kin)
                dkin = _dot_tn(dp, qin)
                dv = _dot_tn(p, dov) + _dot_nt(klast, dst)
                dqb = _dot(dov, st)
                dklast = _dot(v, dst)
                ds_ref[h] = _dot_tn(dov, qb) + dst * dec
                db = dqin * qin - dkin * kin + dqb * qb - dklast * klast
                extra = (jnp.sum(dklast * klast, axis=0, keepdims=True)
                         + dec * jnp.sum(st * dst, axis=0, keepdims=True))
                dg = _cumsum_rows(db, row, not reverse) + extra
                dq = dqin * eq + dqb * eb
                dk = dkin * ek + dklast * el
                dfk = dg / f - dk
                dz = (dfk * (1.0 - lbv) * sg * sgn).astype(BF16)
                dlb_ref[:, cols] += jnp.sum(dfk * sgn, axis=0, keepdims=True)
                if final:
                    sq = _sigmoid(uq)
                    col = lambda blk: pl.ds(blk * HG_W + h * HG_D, HG_D)
                    out_ref[rows, col(0)] = ((dq + dqp_ref[rows, cols]) * (sq * (1.0 + uq * (1.0 - sq)))).astype(BF16)
                    out_ref[rows, col(1)] = dzp_ref[rows, cols]
                    out_ref[rows, col(2)] = dz
                    out_ref[rows, col(3)] = (dv + dvp_ref[rows, cols]).astype(BF16)
                    out_ref[rows, col(4)] = dug_ref[rows, cols]
                else:
                    dq_ref[rows, cols] = dq
                    dz_ref[rows, cols] = dz
                    dv_ref[rows, cols] = dv
            return carry

        lax.fori_loop(0, GLA_NC, chunk, 0, unroll=GLA_UNROLL)

    blk = (lambda i: i) if reverse else (lambda i: nb - 1 - i)
    ucol = lambda cb: pl.BlockSpec((GLA_TB, HG_W), lambda i: (blk(i), cb))
    tok = pl.BlockSpec((GLA_TB, HG_W), lambda i: (blk(i), 0))
    vec = pl.BlockSpec((1, HG_W), lambda i: (0, 0))
    in_specs = [ucol(0), ucol(f_block), ucol(3), vec, tok,
                pl.BlockSpec((GLA_NC, HG_HEADS, HG_D, HG_D), lambda i: (blk(i), 0, 0, 0))]
    vec_shape = jax.ShapeDtypeStruct((1, HG_W), F32)
    if final:
        in_specs += [tok] * 4
        out_specs = [pl.BlockSpec((GLA_TB, 5 * HG_W), lambda i: (blk(i), 0)), vec]
        out_shape = [jax.ShapeDtypeStruct((T, 5 * HG_W), BF16), vec_shape]
    else:
        out_specs = [tok, tok, tok, vec]
        out_shape = [jax.ShapeDtypeStruct((T, HG_W), F32), jax.ShapeDtypeStruct((T, HG_W), BF16),
                     jax.ShapeDtypeStruct((T, HG_W), F32), vec_shape]
    return pl.pallas_call(
        body, name=name, grid=(nb,), in_specs=in_specs, out_specs=out_specs, out_shape=out_shape,
        scratch_shapes=[pltpu.VMEM((HG_HEADS, HG_D, HG_D), F32)],
        compiler_params=_params(("arbitrary",)),
    )(U, U, U, lb, do, states, *(prev if final else ()))


def _hg_post_fwd(o_f, o_b, U, w, *, name, tm=512):
    T = o_f.shape[0]

    def body(of_ref, ob_ref, ug_ref, w_ref, out_ref):
        wv = w_ref[...]
        for h in range(HG_HEADS):
            cols = pl.ds(h * HG_D, HG_D)
            o = of_ref[:, cols] + ob_ref[:, cols]
            r = lax.rsqrt(jnp.mean(o * o, axis=-1, keepdims=True) + EPS)
            ug = ug_ref[:, cols]
            out_ref[:, cols] = (o * r * wv * (ug * _sigmoid(ug))).astype(BF16)

    tok = pl.BlockSpec((tm, HG_W), lambda i: (i, 0))
    return pl.pallas_call(
        body, name=name, grid=(T // tm,),
        in_specs=[tok, tok, pl.BlockSpec((tm, HG_W), lambda i: (i, 4)), pl.BlockSpec((1, HG_D), lambda i: (0, 0))],
        out_specs=tok, out_shape=jax.ShapeDtypeStruct((T, HG_W), BF16),
        compiler_params=_params(("parallel",)),
    )(o_f, o_b, U, w)


def _hg_post_bwd(dmix, o_f, o_b, U, w, *, name, tm=512):
    T = o_f.shape[0]

    def body(dm_ref, of_ref, ob_ref, ug_ref, w_ref, do_ref, dug_ref, dw_ref):
        @pl.when(pl.program_id(0) == 0)
        def _():
            dw_ref[...] = jnp.zeros_like(dw_ref)

        wv = w_ref[...]
        for h in range(HG_HEADS):
            cols = pl.ds(h * HG_D, HG_D)
            o = of_ref[:, cols] + ob_ref[:, cols]
            r = lax.rsqrt(jnp.mean(o * o, axis=-1, keepdims=True) + EPS)
            xh = o * r
            ug = ug_ref[:, cols]
            sg = _sigmoid(ug)
            dm = dm_ref[:, cols]
            dn = dm * (ug * sg)
            dug_ref[:, cols] = (dm * (xh * wv) * (sg * (1.0 + ug * (1.0 - sg)))).astype(BF16)
            dxh = dn * wv
            t = jnp.mean(dxh * xh, axis=-1, keepdims=True)
            do_ref[:, cols] = r * (dxh - xh * t)
            dw_ref[:, cols] += jnp.sum(dn * xh, axis=0, keepdims=True)

    tok = pl.BlockSpec((tm, HG_W), lambda i: (i, 0))
    vec = pl.BlockSpec((1, HG_W), lambda i: (0, 0))
    return pl.pallas_call(
        body, name=name, grid=(T // tm,),
        in_specs=[tok, tok, tok, pl.BlockSpec((tm, HG_W), lambda i: (i, 4)), pl.BlockSpec((1, HG_D), lambda i: (0, 0))],
        out_specs=[tok, tok, vec],
        out_shape=[jax.ShapeDtypeStruct((T, HG_W), F32), jax.ShapeDtypeStruct((T, HG_W), BF16),
                   jax.ShapeDtypeStruct((1, HG_W), F32)],
        compiler_params=_params(("arbitrary",)),
    )(dmix, o_f, o_b, U, w)


def _rope_tables(T):
    rows = T // GRID_W
    row = np.repeat(np.arange(rows), GRID_W).astype(np.float32)
    col = np.tile(np.arange(GRID_W), rows).astype(np.float32)
    axis_dim = ATT_DH // 2
    freqs = (np.float32(ROPE_THETA) ** (-np.arange(0, axis_dim, 2, dtype=np.float32) / np.float32(axis_dim))
             ).astype(np.float32)
    ang = np.concatenate([row[:, None] * freqs, col[:, None] * freqs], axis=-1).astype(np.float32)
    cos, sin = np.cos(ang), np.sin(ang)
    c = np.repeat(cos, 2, axis=-1)
    s = np.stack([-sin, sin], axis=-1).reshape(T, ATT_DH)
    return jnp.asarray(np.tile(c, (1, 2)), F32), jnp.asarray(np.tile(s, (1, 2)), F32)


def _head_blockdiag(width):
    shift = ATT_DH.bit_length() - 1
    ri = jnp.right_shift(lax.broadcasted_iota(jnp.int32, (width, width), 0), shift)
    ci = jnp.right_shift(lax.broadcasted_iota(jnp.int32, (width, width), 1), shift)
    return jnp.where(ri == ci, 1.0, 0.0).astype(BF16)


def _head_sum(x, bd):
    hi = x.astype(BF16)
    lo = (x - hi.astype(F32)).astype(BF16)
    return jnp.dot(hi, bd, preferred_element_type=F32) + jnp.dot(lo, bd, preferred_element_type=F32)


def _pair_swap(x, even):
    n = x.shape[-1]
    return jnp.where(even, pltpu.roll(x, n - 1, 1), pltpu.roll(x, 1, 1))


def _att_prep_fwd(U, cos, sin, qw, kw, *, name, tm=512):
    T = U.shape[0]
    scale = ATT_DH ** -0.5

    def body(aq_ref, ak_ref, av_ref, c_ref, s_ref, qw_ref, kw_ref, q_ref, k_ref, v_ref):
        bd = _head_blockdiag(ATT_QW)
        c2, s2 = c_ref[...], s_ref[...]
        c8, s8 = jnp.tile(c2, (1, 4)), jnp.tile(s2, (1, 4))

        def norm_rope(x, w, c, s, bdm):
            r = lax.rsqrt(_head_sum(x * x, bdm) * (1.0 / ATT_DH) + EPS)
            y = x * r * w
            even = (lax.broadcasted_iota(jnp.int32, y.shape, 1) & 1) == 0
            return y * c + _pair_swap(y, even) * s

        q_ref[...] = (norm_rope(aq_ref[...], qw_ref[...], c8, s8, bd) * scale).astype(BF16)
        k_ref[...] = norm_rope(ak_ref[...], kw_ref[...], c2, s2, bd[:ATT_KW, :ATT_KW]).astype(BF16)
        v_ref[...] = av_ref[...].astype(BF16)

    kv_spec = pl.BlockSpec((tm, ATT_KW), lambda i: (i, 0))
    return pl.pallas_call(
        body, name=name, grid=(T // tm,),
        in_specs=[pl.BlockSpec((tm, ATT_QW), lambda i: (i, 5)),
                  pl.BlockSpec((tm, ATT_KW), lambda i: (i, 24)), pl.BlockSpec((tm, ATT_KW), lambda i: (i, 25)),
                  kv_spec, kv_spec,
                  pl.BlockSpec((1, ATT_QW), lambda i: (0, 0)), pl.BlockSpec((1, ATT_KW), lambda i: (0, 0))],
        out_specs=[pl.BlockSpec((tm, ATT_QW), lambda i: (i, 0)), kv_spec, kv_spec],
        out_shape=[jax.ShapeDtypeStruct((T, ATT_QW), BF16), jax.ShapeDtypeStruct((T, ATT_KW), BF16),
                   jax.ShapeDtypeStruct((T, ATT_KW), BF16)],
        compiler_params=_params(("parallel",)),
    )(U, U, U, cos, sin, qw, kw)


def _att_prep_bwd(U, dq, dk, cos, sin, qw, kw, *, name, tm=512):
    T = U.shape[0]
    scale = ATT_DH ** -0.5

    def body(aq_ref, ak_ref, dq_ref, dk_ref, c_ref, s_ref, qw_ref, kw_ref, daq_ref, dak_ref, dqw_ref, dkw_ref):
        @pl.when(pl.program_id(0) == 0)
        def _():
            dqw_ref[...] = jnp.zeros_like(dqw_ref)
            dkw_ref[...] = jnp.zeros_like(dkw_ref)

        bd = _head_blockdiag(ATT_QW)
        c2, s2 = c_ref[...], s_ref[...]
        c8, s8 = jnp.tile(c2, (1, 4)), jnp.tile(s2, (1, 4))

        def bwd(x, dy, w, c, s, bdm):
            even = (lax.broadcasted_iota(jnp.int32, x.shape, 1) & 1) == 0
            dn = dy * c - _pair_swap(dy, even) * s
            r = lax.rsqrt(_head_sum(x * x, bdm) * (1.0 / ATT_DH) + EPS)
            xh = x * r
            dxh = dn * w
            t = _head_sum(dxh * xh, bdm) * (1.0 / ATT_DH)
            return r * (dxh - xh * t), jnp.sum(dn * xh, axis=0, keepdims=True)

        da, dw = bwd(aq_ref[...], dq_ref[...] * scale, qw_ref[...], c8, s8, bd)
        daq_ref[...] = da
        dqw_ref[...] += dw
        da, dw = bwd(ak_ref[...], dk_ref[...], kw_ref[...], c2, s2, bd[:ATT_KW, :ATT_KW])
        dak_ref[...] = da
        dkw_ref[...] += dw

    q_spec = pl.BlockSpec((tm, ATT_QW), lambda i: (i, 0))
    kv_spec = pl.BlockSpec((tm, ATT_KW), lambda i: (i, 0))
    qv = pl.BlockSpec((1, ATT_QW), lambda i: (0, 0))
    kv = pl.BlockSpec((1, ATT_KW), lambda i: (0, 0))
    return pl.pallas_call(
        body, name=name, grid=(T // tm,),
        in_specs=[pl.BlockSpec((tm, ATT_QW), lambda i: (i, 5)), pl.BlockSpec((tm, ATT_KW), lambda i: (i, 24)),
                  q_spec, kv_spec, kv_spec, kv_spec, qv, kv],
        out_specs=[q_spec, kv_spec, qv, kv],
        out_shape=[jax.ShapeDtypeStruct((T, ATT_QW), F32), jax.ShapeDtypeStruct((T, ATT_KW), F32),
                   jax.ShapeDtypeStruct((1, ATT_QW), F32), jax.ShapeDtypeStruct((1, ATT_KW), F32)],
        compiler_params=_params(("arbitrary",)),
    )(U, U, dq, dk, cos, sin, qw, kw)


FA_TQ = 256
FA_TK = 256
FA_SW = 128


def _fa_tiles(T):
    tq, tk = min(FA_TQ, T), min(FA_TK, T)
    return tq, tk, T // tq, T // tk


def _to_fa_cols(a, T):
    tq, _, nq, _ = _fa_tiles(T)
    return a.reshape(nq, tq, ATT_KV, ATT_G, ATT_DH).transpose(2, 0, 4, 3, 1).reshape(ATT_KV, nq, ATT_DH, ATT_G * tq)


def _to_fa_rows(a, T):
    tq, _, nq, _ = _fa_tiles(T)
    return a.reshape(nq, tq, ATT_KV, ATT_G, ATT_DH).transpose(2, 0, 3, 1, 4).reshape(ATT_KV, nq, ATT_G * tq, ATT_DH)


def _from_fa_cols(a, T):
    tq, _, nq, _ = _fa_tiles(T)
    return a.reshape(ATT_KV, nq, ATT_DH, ATT_G, tq).transpose(1, 4, 0, 3, 2).reshape(T, ATT_QW)


def _kv_rows(a, T):
    _, tk, _, n_k = _fa_tiles(T)
    return a.reshape(n_k, tk, ATT_KV, ATT_DH).transpose(2, 0, 1, 3)


def _kv_cols(a, T):
    _, tk, _, n_k = _fa_tiles(T)
    return a.reshape(n_k, tk, ATT_KV, ATT_DH).transpose(2, 0, 3, 1)


def _flash_fwd(q_c, k_r, v_c, *, name):
    _, nq, _, R = q_c.shape
    _, n_k, tk, _ = k_r.shape

    def body(q_ref, k_ref, v_ref, o_ref, lse_ref):
        for st in range(R // FA_SW):
            lanes = pl.ds(st * FA_SW, FA_SW)
            qv = q_ref[0, 0, :, lanes]

            def step(j, carry):
                m, l, acc = carry
                s = jnp.dot(k_ref[0, j], qv, preferred_element_type=F32)
                m_new = jnp.maximum(m, jnp.max(s, axis=0, keepdims=True))
                alpha = jnp.exp(m - m_new)
                p = jnp.exp(s - m_new)
                l = alpha * l + jnp.sum(p, axis=0, keepdims=True)
                acc = alpha * acc + jnp.dot(v_ref[0, j], p.astype(BF16), preferred_element_type=F32)
                return m_new, l, acc

            m, l, acc = lax.fori_loop(0, n_k, step, (jnp.full((1, FA_SW), -jnp.inf, F32), jnp.zeros((1, FA_SW), F32),
                                                     jnp.zeros((ATT_DH, FA_SW), F32)))
            o_ref[0, 0, :, lanes] = acc / l
            lse_ref[0, 0, :, lanes] = m + jnp.log(l)

    qspec = pl.BlockSpec((1, 1, ATT_DH, R), lambda h, i: (h, i, 0, 0))
    return pl.pallas_call(
        body, name=name, grid=(ATT_KV, nq),
        in_specs=[qspec, pl.BlockSpec((1, n_k, tk, ATT_DH), lambda h, i: (h, 0, 0, 0)),
                  pl.BlockSpec((1, n_k, ATT_DH, tk), lambda h, i: (h, 0, 0, 0))],
        out_specs=[qspec, pl.BlockSpec((1, 1, 1, R), lambda h, i: (h, i, 0, 0))],
        out_shape=[jax.ShapeDtypeStruct((ATT_KV, nq, ATT_DH, R), F32), jax.ShapeDtypeStruct((ATT_KV, nq, 1, R), F32)],
        compiler_params=_params(("parallel", "parallel")),
    )(q_c, k_r, v_c)


def _flash_bwd(q_c, q_r, k_r, k_c, v_r, do_c, do_r, o_c, lse, *, name):
    _, nq, _, R = q_c.shape
    _, n_k, tk, _ = k_r.shape

    def body(qc_ref, qr_ref, kr_ref, kc_ref, vr_ref, doc_ref, dor_ref, oc_ref, lse_ref, dq_ref, dk_ref, dv_ref,
             acc_ref):
        @pl.when(pl.program_id(1) == 0)
        def _():
            dk_ref[...] = jnp.zeros_like(dk_ref)
            dv_ref[...] = jnp.zeros_like(dv_ref)

        qc, doc = qc_ref[0, 0], doc_ref[0, 0]
        qr, dor = qr_ref[0, 0], dor_ref[0, 0]
        delta = jnp.sum(doc.astype(F32) * oc_ref[0, 0], axis=0, keepdims=True)
        lsev = lse_ref[0, 0]
        acc_ref[...] = jnp.zeros_like(acc_ref)

        def step(j, carry):
            s = jnp.dot(kr_ref[0, j], qc, preferred_element_type=F32)
            p = jnp.exp(s - lsev)
            dp = jnp.dot(vr_ref[0, j], doc, preferred_element_type=F32)
            ds = (p * (dp - delta)).astype(BF16)
            acc_ref[...] += jnp.dot(kc_ref[0, j], ds, preferred_element_type=F32)
            dk_ref[0, j] += jnp.dot(ds, qr, preferred_element_type=F32)
            dv_ref[0, j] += jnp.dot(p.astype(BF16), dor, preferred_element_type=F32)
            return carry

        lax.fori_loop(0, n_k, step, 0)
        dq_ref[0, 0] = acc_ref[...]

    cspec = pl.BlockSpec((1, 1, ATT_DH, R), lambda h, i: (h, i, 0, 0))
    rspec = pl.BlockSpec((1, 1, R, ATT_DH), lambda h, i: (h, i, 0, 0))
    krspec = pl.BlockSpec((1, n_k, tk, ATT_DH), lambda h, i: (h, 0, 0, 0))
    kcspec = pl.BlockSpec((1, n_k, ATT_DH, tk), lambda h, i: (h, 0, 0, 0))
    return pl.pallas_call(
        body, name=name, grid=(ATT_KV, nq),
        in_specs=[cspec, rspec, krspec, kcspec, krspec, cspec, rspec, cspec,
                  pl.BlockSpec((1, 1, 1, R), lambda h, i: (h, i, 0, 0))],
        out_specs=[cspec, krspec, krspec],
        out_shape=[jax.ShapeDtypeStruct((ATT_KV, nq, ATT_DH, R), F32),
                   jax.ShapeDtypeStruct((ATT_KV, n_k, tk, ATT_DH), F32),
                   jax.ShapeDtypeStruct((ATT_KV, n_k, tk, ATT_DH), F32)],
        scratch_shapes=[pltpu.VMEM((ATT_DH, R), F32)],
        compiler_params=_params(("parallel", "arbitrary")),
    )(q_c, q_r, k_r, k_c, v_r, do_c, do_r, o_c, lse)


def _att_post_fwd(o, w, *, name, tm=512):
    T = o.shape[0]

    def body(o_ref, w_ref, out_ref):
        ov = o_ref[...]
        r = lax.rsqrt(jnp.mean(ov * ov, axis=-1, keepdims=True) + EPS)
        out_ref[...] = (ov * r * w_ref[...]).astype(BF16)

    tok = pl.BlockSpec((tm, ATT_QW), lambda i: (i, 0))
    return pl.pallas_call(
        body, name=name, grid=(T // tm,),
        in_specs=[tok, pl.BlockSpec((1, ATT_QW), lambda i: (0, 0))],
        out_specs=tok, out_shape=jax.ShapeDtypeStruct((T, ATT_QW), BF16),
        compiler_params=_params(("parallel",)),
    )(o, w)


def _att_post_bwd(dmix, o, w, *, name, tm=512):
    T = o.shape[0]

    def body(dm_ref, o_ref, w_ref, do_ref, dw_ref):
        @pl.when(pl.program_id(0) == 0)
        def _():
            dw_ref[...] = jnp.zeros_like(dw_ref)

        ov = o_ref[...]
        r = lax.rsqrt(jnp.mean(ov * ov, axis=-1, keepdims=True) + EPS)
        xh = ov * r
        dm = dm_ref[...]
        dxh = dm * w_ref[...]
        t = jnp.mean(dxh * xh, axis=-1, keepdims=True)
        do_ref[...] = (r * (dxh - xh * t)).astype(BF16)
        dw_ref[...] += jnp.sum(dm * xh, axis=0, keepdims=True)

    tok = pl.BlockSpec((tm, ATT_QW), lambda i: (i, 0))
    vec = pl.BlockSpec((1, ATT_QW), lambda i: (0, 0))
    return pl.pallas_call(
        body, name=name, grid=(T // tm,),
        in_specs=[pl.BlockSpec((tm, ATT_QW), lambda i: (i, 1)), tok, vec],
        out_specs=[tok, vec],
        out_shape=[jax.ShapeDtypeStruct((T, ATT_QW), BF16), jax.ShapeDtypeStruct((1, ATT_QW), F32)],
        compiler_params=_params(("arbitrary",)),
    )(dmix, o, w)


FA_HP = ATT_KV * ATT_DH
FA_TK_FWD = 512
FA_TK_BWD = 512


def _cols_from_tokens(x, kv):
    w = ATT_G * ATT_DH
    xt = x[:, kv * w:(kv + 1) * w].T
    return jnp.concatenate([xt[g * ATT_DH:(g + 1) * ATT_DH, :] for g in range(ATT_G)], axis=1)


def _tokens_from_cols(c):
    tq = c.shape[1] // ATT_G
    return jnp.concatenate([c[:, g * tq:(g + 1) * tq] for g in range(ATT_G)], axis=0).T


def _store_padded_cols(ref, x, norm_ref=None):
    for kv in range(ATT_KV):
        cols = _cols_from_tokens(x, kv).astype(BF16)
        ref[kv, 0, kv * ATT_DH:(kv + 1) * ATT_DH, :] = cols
        ref[kv, 0, (1 - kv) * ATT_DH:(2 - kv) * ATT_DH, :] = jnp.zeros_like(cols)
        if norm_ref is not None:
            cf = cols.astype(F32)
            norm_ref[kv, 0] = jnp.sqrt(jnp.sum(cf * cf, axis=0, keepdims=True))


def _att_prep_fwd2(U, cos, sin, qw, kw, *, name):
    T = U.shape[0]
    tm = min(FA_TQ, T)
    R = ATT_G * tm
    scale = ATT_DH ** -0.5

    def body(aq_ref, ak_ref, av_ref, c_ref, s_ref, qw_ref, kw_ref, q_ref, k_ref, v_ref, qn_ref, kmax_ref):
        @pl.when(pl.program_id(0) == 0)
        def _():
            kmax_ref[...] = jnp.zeros_like(kmax_ref)

        bd = _head_blockdiag(ATT_QW)
        c2, s2 = c_ref[...], s_ref[...]
        c8, s8 = jnp.tile(c2, (1, 4)), jnp.tile(s2, (1, 4))

        def norm_rope(x, w, c, s, bdm):
            r = lax.rsqrt(_head_sum(x * x, bdm) * (1.0 / ATT_DH) + EPS)
            y = x * r * w
            even = (lax.broadcasted_iota(jnp.int32, y.shape, 1) & 1) == 0
            return y * c + _pair_swap(y, even) * s

        _store_padded_cols(q_ref, norm_rope(aq_ref[...], qw_ref[...], c8, s8, bd) * scale, qn_ref)
        kb = norm_rope(ak_ref[...], kw_ref[...], c2, s2, bd[:ATT_KW, :ATT_KW]).astype(BF16)
        k_ref[...] = kb
        kf = kb.astype(F32)
        ksq = _head_sum(kf * kf, bd[:ATT_KW, :ATT_KW])
        kmax_ref[...] = jnp.maximum(kmax_ref[...], jnp.max(ksq, axis=0, keepdims=True))
        v_ref[...] = av_ref[...].astype(BF16)

    kv_spec = pl.BlockSpec((tm, ATT_KW), lambda i: (i, 0))
    return pl.pallas_call(
        body, name=name, grid=(T // tm,),
        in_specs=[pl.BlockSpec((tm, ATT_QW), lambda i: (i, 5)),
                  pl.BlockSpec((tm, ATT_KW), lambda i: (i, 24)), pl.BlockSpec((tm, ATT_KW), lambda i: (i, 25)),
                  kv_spec, kv_spec,
                  pl.BlockSpec((1, ATT_QW), lambda i: (0, 0)), pl.BlockSpec((1, ATT_KW), lambda i: (0, 0))],
        out_specs=[pl.BlockSpec((ATT_KV, 1, FA_HP, R), lambda i: (0, i, 0, 0)), kv_spec, kv_spec,
                   pl.BlockSpec((ATT_KV, 1, 1, R), lambda i: (0, i, 0, 0)), pl.BlockSpec((1, ATT_KW), lambda i: (0, 0))],
        out_shape=[jax.ShapeDtypeStruct((ATT_KV, T // tm, FA_HP, R), BF16),
                   jax.ShapeDtypeStruct((T, ATT_KW), BF16), jax.ShapeDtypeStruct((T, ATT_KW), BF16),
                   jax.ShapeDtypeStruct((ATT_KV, T // tm, 1, R), F32), jax.ShapeDtypeStruct((1, ATT_KW), F32)],
        compiler_params=_params(("arbitrary",)),
    )(U, U, U, cos, sin, qw, kw)


def _att_prep_bwd2(U, dq_c, dk, dv, cos, sin, qw, kw, *, name):
    T = U.shape[0]
    tm = min(FA_TQ, T)
    R = ATT_G * tm
    scale = ATT_DH ** -0.5

    def body(aq_ref, ak_ref, dq_ref, dk_ref, dv_ref, c_ref, s_ref, qw_ref, kw_ref, out_ref, dqw_ref, dkw_ref):
        @pl.when(pl.program_id(0) == 0)
        def _():
            dqw_ref[...] = jnp.zeros_like(dqw_ref)
            dkw_ref[...] = jnp.zeros_like(dkw_ref)

        bd = _head_blockdiag(ATT_QW)
        c2, s2 = c_ref[...], s_ref[...]
        c8, s8 = jnp.tile(c2, (1, 4)), jnp.tile(s2, (1, 4))

        def bwd(x, dy, w, c, s, bdm):
            even = (lax.broadcasted_iota(jnp.int32, x.shape, 1) & 1) == 0
            dn = dy * c - _pair_swap(dy, even) * s
            r = lax.rsqrt(_head_sum(x * x, bdm) * (1.0 / ATT_DH) + EPS)
            xh = x * r
            dxh = dn * w
            t = _head_sum(dxh * xh, bdm) * (1.0 / ATT_DH)
            return r * (dxh - xh * t), jnp.sum(dn * xh, axis=0, keepdims=True)

        dq = jnp.concatenate([_tokens_from_cols(dq_ref[kv, 0]) for kv in range(ATT_KV)], axis=1)
        da, dw = bwd(aq_ref[...], dq * scale, qw_ref[...], c8, s8, bd)
        out_ref[:, 0:ATT_QW] = da.astype(BF16)
        dqw_ref[...] += dw
        da, dw = bwd(ak_ref[...], dk_ref[...], kw_ref[...], c2, s2, bd[:ATT_KW, :ATT_KW])
        out_ref[:, ATT_QW:ATT_QW + ATT_KW] = da.astype(BF16)
        dkw_ref[...] += dw
        out_ref[:, ATT_QW + ATT_KW:ATT_QW + 2 * ATT_KW] = dv_ref[...].astype(BF16)

    kv_spec = pl.BlockSpec((tm, ATT_KW), lambda i: (i, 0))
    qv = pl.BlockSpec((1, ATT_QW), lambda i: (0, 0))
    kv = pl.BlockSpec((1, ATT_KW), lambda i: (0, 0))
    w_att = ATT_QW + 2 * ATT_KW
    return pl.pallas_call(
        body, name=name, grid=(T // tm,),
        in_specs=[pl.BlockSpec((tm, ATT_QW), lambda i: (i, 5)), pl.BlockSpec((tm, ATT_KW), lambda i: (i, 24)),
                  pl.BlockSpec((ATT_KV, 1, ATT_DH, R), lambda i: (0, i, 0, 0)), kv_spec, kv_spec, kv_spec, kv_spec, qv, kv],
        out_specs=[pl.BlockSpec((tm, w_att), lambda i: (i, 0)), qv, kv],
        out_shape=[jax.ShapeDtypeStruct((T, w_att), BF16),
                   jax.ShapeDtypeStruct((1, ATT_QW), F32), jax.ShapeDtypeStruct((1, ATT_KW), F32)],
        compiler_params=_params(("arbitrary",)),
    )(U, U, dq_c, dk, dv, cos, sin, qw, kw)


def _pick_head(x, kv):
    return jnp.where(kv == 0, x[0:ATT_DH, :], x[ATT_DH:FA_HP, :])


def _flash_fwd2(q_c, k, v, *, name):
    _, nq, _, R = q_c.shape
    T = k.shape[0]
    tk = min(FA_TK_FWD, T)
    n_k = T // tk

    def body(q_ref, k_ref, v_ref, o_ref, lse_ref, acc_ref):
        kv = pl.program_id(0)
        qv = q_ref[0, 0]
        acc_ref[...] = jnp.zeros_like(acc_ref)

        def step(j, carry):
            m, l = carry
            s = jnp.dot(k_ref[j], qv, preferred_element_type=F32)
            m_new = jnp.maximum(m, jnp.max(s, axis=0, keepdims=True))
            alpha = jnp.exp(m - m_new)
            p = jnp.exp(s - m_new)
            l = alpha * l + jnp.sum(p, axis=0, keepdims=True)
            pv = lax.dot_general(v_ref[j], p.astype(BF16), (((0,), (0,)), ((), ())), preferred_element_type=F32)
            acc_ref[...] = alpha * acc_ref[...] + _pick_head(pv, kv)
            return m_new, l

        m, l = lax.fori_loop(0, n_k, step, (jnp.full((1, R), -jnp.inf, F32), jnp.zeros((1, R), F32)))
        o_ref[0, 0] = acc_ref[...] / l
        lse_ref[0, 0] = m + jnp.log(l)

    kspec = pl.BlockSpec((n_k, tk, FA_HP), lambda h, i: (0, 0, 0))
    return pl.pallas_call(
        body, name=name, grid=(ATT_KV, nq),
        in_specs=[pl.BlockSpec((1, 1, FA_HP, R), lambda h, i: (h, i, 0, 0)), kspec, kspec],
        out_specs=[pl.BlockSpec((1, 1, ATT_DH, R), lambda h, i: (h, i, 0, 0)),
                   pl.BlockSpec((1, 1, 1, R), lambda h, i: (h, i, 0, 0))],
        out_shape=[jax.ShapeDtypeStruct((ATT_KV, nq, ATT_DH, R), F32), jax.ShapeDtypeStruct((ATT_KV, nq, 1, R), F32)],
        scratch_shapes=[pltpu.VMEM((ATT_DH, R), F32)],
        compiler_params=_params(("parallel", "parallel")),
    )(q_c, k.reshape(n_k, tk, FA_HP), v.reshape(n_k, tk, FA_HP))


FA_BOUND_MAX = 40.0
FA_TK_FAST = 512


def _flash_fwd_bounded(q_c, k, v, m_c, *, name):
    _, nq, _, R = q_c.shape
    T = k.shape[0]
    tk = min(FA_TK_FAST, T)
    n_k = T // tk

    def body(q_ref, k_ref, v_ref, m_ref, o_ref, lse_ref, acc_ref):
        kv = pl.program_id(0)
        qv = q_ref[0, 0]
        m = m_ref[0, 0]
        acc_ref[...] = jnp.zeros_like(acc_ref)

        def step(j, l8):
            s = jnp.dot(k_ref[j], qv, preferred_element_type=F32)
            p = jnp.exp(s - m)
            l8 = l8 + jnp.sum(p.reshape(tk // 8, 8, R), axis=0)
            acc_ref[...] += lax.dot_general(v_ref[j], p.astype(BF16), (((0,), (0,)), ((), ())),
                                            preferred_element_type=F32)
            return l8

        l8 = lax.fori_loop(0, n_k, step, jnp.zeros((8, R), F32))
        l = jnp.sum(l8, axis=0, keepdims=True)
        o_ref[0, 0] = _pick_head(acc_ref[...], kv) / l
        lse_ref[0, 0] = m + jnp.log(l)

    kspec = pl.BlockSpec((n_k, tk, FA_HP), lambda h, i: (0, 0, 0))
    vspec = pl.BlockSpec((1, 1, 1, R), lambda h, i: (h, i, 0, 0))
    return pl.pallas_call(
        body, name=name, grid=(ATT_KV, nq),
        in_specs=[pl.BlockSpec((1, 1, FA_HP, R), lambda h, i: (h, i, 0, 0)), kspec, kspec, vspec],
        out_specs=[pl.BlockSpec((1, 1, ATT_DH, R), lambda h, i: (h, i, 0, 0)), vspec],
        out_shape=[jax.ShapeDtypeStruct((ATT_KV, nq, ATT_DH, R), F32), jax.ShapeDtypeStruct((ATT_KV, nq, 1, R), F32)],
        scratch_shapes=[pltpu.VMEM((FA_HP, R), F32)],
        compiler_params=_params(("parallel", "parallel")),
    )(q_c, k.reshape(n_k, tk, FA_HP), v.reshape(n_k, tk, FA_HP), m_c)


CHIP_MASKS = [(1, 0, 0), (0, 1, 0), (1, 1, 0)]


def _chip_slot(p):
    return 2 * p[0] + p[1]


def _flash_bwd2(q_c, k, v, do_c, lse, delta, *, name, ride=None):
    _, nq, _, R = q_c.shape
    T = k.shape[0]
    tk = min(FA_TK_BWD, T)
    n_k = T // tk
    n_ride = 0 if ride is None else len(ride)

    def body(qc_ref, k_ref, v_ref, doc_ref, lse_ref, delta_ref, *rest):
        ride_in, rest = rest[:n_ride], rest[n_ride:]
        dq_ref, dk_ref, dv_ref = rest[:3]
        ride_out, rest = rest[3:3 + n_ride], rest[3 + n_ride:]
        acc_ref = rest[0]
        kv = pl.program_id(0)
        first = (kv == 0) & (pl.program_id(1) == 0)

        if n_ride:
            mine, copies = _exchange_copies(ride_in, ride_out, *rest[1:], masks=CHIP_MASKS, slot=_chip_slot)

            @pl.when(first)
            def _():
                for cp in mine:
                    cp.start()
                for send, _ in copies:
                    send.start()

        @pl.when(first)
        def _():
            dk_ref[...] = jnp.zeros_like(dk_ref)
            dv_ref[...] = jnp.zeros_like(dv_ref)

        qc, doc = qc_ref[0, 0], doc_ref[0, 0]
        lsev, delta = lse_ref[0, 0], delta_ref[0, 0]
        acc_ref[...] = jnp.zeros_like(acc_ref)

        def step(j, carry):
            kb = k_ref[j]
            s = jnp.dot(kb, qc, preferred_element_type=F32)
            p = jnp.exp(s - lsev)
            dp = jnp.dot(v_ref[j], doc, preferred_element_type=F32)
            ds = (p * (dp - delta)).astype(BF16)
            acc_ref[...] += lax.dot_general(kb, ds, (((0,), (0,)), ((), ())), preferred_element_type=F32)
            dk_ref[j] += lax.dot_general(ds, qc, (((1,), (1,)), ((), ())), preferred_element_type=F32)
            dv_ref[j] += lax.dot_general(p.astype(BF16), doc, (((1,), (1,)), ((), ())), preferred_element_type=F32)
            return carry

        lax.fori_loop(0, n_k, step, 0)
        dq_ref[0, 0] = _pick_head(acc_ref[...], kv)

        if n_ride:
            @pl.when((kv == ATT_KV - 1) & (pl.program_id(1) == nq - 1))
            def _():
                for send, recv in copies:
                    recv.wait_recv()
                    send.wait_send()
                for cp in mine:
                    cp.wait()

    cspec = pl.BlockSpec((1, 1, FA_HP, R), lambda h, i: (h, i, 0, 0))
    vspec = pl.BlockSpec((1, 1, 1, R), lambda h, i: (h, i, 0, 0))
    kspec = pl.BlockSpec((n_k, tk, FA_HP), lambda h, i: (0, 0, 0))
    ride = [] if ride is None else list(ride)
    scratch = [pltpu.VMEM((FA_HP, R), F32)]
    if n_ride:
        n_sem = len(CHIP_MASKS) * n_ride
        scratch += [pltpu.SemaphoreType.DMA((n_sem,)), pltpu.SemaphoreType.DMA((n_sem,)),
                    pltpu.SemaphoreType.DMA((n_ride,))]
    dq_c, dk, dv, *rode = pl.pallas_call(
        body, name=name, grid=(ATT_KV, nq),
        in_specs=[cspec, kspec, kspec, cspec, vspec, vspec] + [ANY] * n_ride,
        out_specs=[pl.BlockSpec((1, 1, ATT_DH, R), lambda h, i: (h, i, 0, 0)), kspec, kspec] + [ANY] * n_ride,
        out_shape=[jax.ShapeDtypeStruct((ATT_KV, nq, ATT_DH, R), F32),
                   jax.ShapeDtypeStruct((n_k, tk, FA_HP), F32), jax.ShapeDtypeStruct((n_k, tk, FA_HP), F32)]
                  + [jax.ShapeDtypeStruct(g.shape, g.dtype) for g in ride],
        scratch_shapes=scratch,
        compiler_params=pltpu.CompilerParams(dimension_semantics=("arbitrary", "arbitrary"),
                                             vmem_limit_bytes=VMEM_LIMIT, has_side_effects=bool(n_ride)),
    )(q_c, k.reshape(n_k, tk, FA_HP), v.reshape(n_k, tk, FA_HP), do_c, lse, delta, *ride)
    return (dq_c, dk.reshape(T, FA_HP), dv.reshape(T, FA_HP), *rode)


def _att_post_fwd2(o_c, w, *, name):
    _, nq, _, R = o_c.shape
    tm = R // ATT_G
    T = nq * tm

    def body(oc_ref, w_ref, o_ref, out_ref):
        ov = jnp.concatenate([_tokens_from_cols(oc_ref[kv, 0]) for kv in range(ATT_KV)], axis=1)
        r = lax.rsqrt(jnp.mean(ov * ov, axis=-1, keepdims=True) + EPS)
        o_ref[...] = ov
        out_ref[...] = (ov * r * w_ref[...]).astype(BF16)

    tok = pl.BlockSpec((tm, ATT_QW), lambda i: (i, 0))
    return pl.pallas_call(
        body, name=name, grid=(nq,),
        in_specs=[pl.BlockSpec((ATT_KV, 1, ATT_DH, R), lambda i: (0, i, 0, 0)), pl.BlockSpec((1, ATT_QW), lambda i: (0, 0))],
        out_specs=[tok, tok],
        out_shape=[jax.ShapeDtypeStruct((T, ATT_QW), F32), jax.ShapeDtypeStruct((T, ATT_QW), BF16)],
        compiler_params=_params(("parallel",)),
    )(o_c, w)


def _att_post_bwd2(dmix, o, w, *, name):
    T = o.shape[0]
    tm = min(FA_TQ, T)
    R = ATT_G * tm

    def body(dm_ref, o_ref, w_ref, do_ref, delta_ref, dw_ref):
        @pl.when(pl.program_id(0) == 0)
        def _():
            dw_ref[...] = jnp.zeros_like(dw_ref)

        ov = o_ref[...]
        r = lax.rsqrt(jnp.mean(ov * ov, axis=-1, keepdims=True) + EPS)
        xh = ov * r
        dm = dm_ref[...]
        dxh = dm * w_ref[...]
        t = jnp.mean(dxh * xh, axis=-1, keepdims=True)
        do = r * (dxh - xh * t)
        _store_padded_cols(do_ref, do)
        dob = do.astype(BF16).astype(F32)
        for kv in range(ATT_KV):
            delta_ref[kv, 0] = jnp.sum(_cols_from_tokens(dob * ov, kv), axis=0, keepdims=True)
        dw_ref[...] += jnp.sum(dm * xh, axis=0, keepdims=True)

    tok = pl.BlockSpec((tm, ATT_QW), lambda i: (i, 0))
    vec = pl.BlockSpec((1, ATT_QW), lambda i: (0, 0))
    return pl.pallas_call(
        body, name=name, grid=(T // tm,),
        in_specs=[pl.BlockSpec((tm, ATT_QW), lambda i: (i, 1)), tok, vec],
        out_specs=[pl.BlockSpec((ATT_KV, 1, FA_HP, R), lambda i: (0, i, 0, 0)),
                   pl.BlockSpec((ATT_KV, 1, 1, R), lambda i: (0, i, 0, 0)), vec],
        out_shape=[jax.ShapeDtypeStruct((ATT_KV, T // tm, FA_HP, R), BF16),
                   jax.ShapeDtypeStruct((ATT_KV, T // tm, 1, R), F32), jax.ShapeDtypeStruct((1, ATT_QW), F32)],
        compiler_params=_params(("arbitrary",)),
    )(dmix, o, w)


def _ffn_up(h2, wg, wu, *, name, tm=512):
    T = h2.shape[0]
    tn = _pick(D_FF, 1408)

    def body(h_ref, wg_ref, wu_ref, g_ref, u_ref, a_ref):
        hv = h_ref[...]
        g = jnp.dot(hv, wg_ref[...], preferred_element_type=F32)
        u = jnp.dot(hv, wu_ref[...], preferred_element_type=F32)
        g_ref[...] = g.astype(BF16)
        u_ref[...] = u.astype(BF16)
        a_ref[...] = (g * _sigmoid(g) * u).astype(BF16)

    wspec = pl.BlockSpec((D_MODEL, tn), lambda i, j: (0, j))
    ospec = pl.BlockSpec((tm, tn), lambda i, j: (i, j))
    return pl.pallas_call(
        body, name=name, grid=(T // tm, D_FF // tn),
        in_specs=[pl.BlockSpec((tm, D_MODEL), lambda i, j: (i, 0)), wspec, wspec],
        out_specs=[ospec] * 3, out_shape=[jax.ShapeDtypeStruct((T, D_FF), BF16)] * 3,
        compiler_params=_params(("parallel", "arbitrary")),
    )(h2, wg, wu)


def _ffn_act_bwd(dx2b, w_down, gate, up, *, name, tm=512):
    T = dx2b.shape[0]
    tn = _pick(D_FF, 1408)

    def body(dx_ref, w_ref, g_ref, u_ref, dg_ref, du_ref):
        da = lax.dot_general(dx_ref[...], w_ref[...], (((1,), (1,)), ((), ())), preferred_element_type=F32)
        g = g_ref[...].astype(F32)
        u = u_ref[...].astype(F32)
        sg = _sigmoid(g)
        dg_ref[...] = (da * u * (sg * (1.0 + g * (1.0 - sg)))).astype(BF16)
        du_ref[...] = (da * (g * sg)).astype(BF16)

    ospec = pl.BlockSpec((tm, tn), lambda i, j: (i, j))
    return pl.pallas_call(
        body, name=name, grid=(T // tm, D_FF // tn),
        in_specs=[pl.BlockSpec((tm, D_MODEL), lambda i, j: (i, 0)),
                  pl.BlockSpec((tn, D_MODEL), lambda i, j: (j, 0)), ospec, ospec],
        out_specs=[ospec] * 2, out_shape=[jax.ShapeDtypeStruct((T, D_FF), BF16)] * 2,
        compiler_params=_params(("parallel", "arbitrary")),
    )(dx2b, w_down, gate, up)


def _assemble_du(U, dq_f, dq_b, dz_f, dz_b, dv_f, dv_b, du_g, da_q, da_k, da_v, *, name, tm=256):
    T = U.shape[0]

    def body(uq_ref, dqf, dqb, dzf, dzb, dvf, dvb, dug, daq, dak, dav, out_ref):
        uq = uq_ref[...]
        sg = _sigmoid(uq)
        out_ref[:, 0:HG_W] = ((dqf[...] + dqb[...]) * (sg * (1.0 + uq * (1.0 - sg)))).astype(BF16)
        out_ref[:, HG_W:2 * HG_W] = dzf[...].astype(BF16)
        out_ref[:, 2 * HG_W:3 * HG_W] = dzb[...].astype(BF16)
        out_ref[:, 3 * HG_W:4 * HG_W] = (dvf[...] + dvb[...]).astype(BF16)
        out_ref[:, 4 * HG_W:5 * HG_W] = dug[...].astype(BF16)
        out_ref[:, 5 * HG_W:5 * HG_W + ATT_QW] = daq[...].astype(BF16)
        out_ref[:, 5 * HG_W + ATT_QW:5 * HG_W + ATT_QW + ATT_KW] = dak[...].astype(BF16)
        out_ref[:, 5 * HG_W + ATT_QW + ATT_KW:D_IN] = dav[...].astype(BF16)

    tok = pl.BlockSpec((tm, HG_W), lambda i: (i, 0))
    kv = pl.BlockSpec((tm, ATT_KW), lambda i: (i, 0))
    return pl.pallas_call(
        body, name=name, grid=(T // tm,),
        in_specs=[tok] * 9 + [kv, kv],
        out_specs=pl.BlockSpec((tm, D_IN), lambda i: (i, 0)),
        out_shape=jax.ShapeDtypeStruct((T, D_IN), BF16),
        compiler_params=_params(("parallel",)),
    )(U, dq_f, dq_b, dz_f, dz_b, dv_f, dv_b, du_g, da_q, da_k, da_v)


def _adam_math(w, g, m, v):
    m = ADAM_B1 * m + (1.0 - ADAM_B1) * g
    v = ADAM_B2 * v + (1.0 - ADAM_B2) * (g * g)
    m_hat = m / (1.0 - ADAM_B1 ** ADAM_STEP)
    v_hat = v / (1.0 - ADAM_B2 ** ADAM_STEP)
    delta = -ADAM_LR * (m_hat / (jnp.sqrt(v_hat) + ADAM_EPS) + ADAM_WD * w)
    return delta, m, v


def _adamw(parts, w, m, v, *, name, tr_cap=256):
    P, R, C = parts.shape
    tr = R
    for t in range(8, min(R, tr_cap) + 1, 8):
        if R % t == 0:
            tr = t

    def body(p_ref, w_ref, m_ref, v_ref, g_ref, d_ref, nm_ref, nv_ref):
        g = p_ref[0].astype(F32)
        for j in range(1, P):
            g = g + p_ref[j].astype(F32)
        d, nm, nv = _adam_math(w_ref[...], g, m_ref[...], v_ref[...])
        g_ref[...] = g
        d_ref[...] = d
        nm_ref[...] = nm
        nv_ref[...] = nv

    blk = pl.BlockSpec((tr, C), lambda i: (i, 0))
    return pl.pallas_call(
        body, name=name, grid=(R // tr,),
        in_specs=[pl.BlockSpec((P, tr, C), lambda i: (0, i, 0)), blk, blk, blk],
        out_specs=[blk] * 4, out_shape=[jax.ShapeDtypeStruct((R, C), F32)] * 4,
        compiler_params=_params(("parallel",)),
    )(parts, w, m, v)


def _all_gather(xs, *, name):
    n = len(xs)

    def body(*refs):
        ins, outs = refs[:n], refs[n:2 * n]
        send_sems, recv_sems, local_sems = refs[2 * n:]
        x, y, c = lax.axis_index("x"), lax.axis_index("y"), lax.axis_index("c")
        me, sibling = (x, y, c), (x, y, 1 - c)
        chips = [(1 - x, y), (x, 1 - y), (1 - x, 1 - y)]

        def slot(p):
            return 4 * p[0] + 2 * p[1] + p[2]

        def copy(a, k, block, to, src=None):
            dst = outs[a].at[slot(block)]
            return pltpu.make_async_remote_copy(
                src_ref=dst if src is None else src, dst_ref=dst,
                send_sem=send_sems.at[a * 7 + k], recv_sem=recv_sems.at[a * 7 + k],
                device_id=to, device_id_type=MESH)

        mine = [pltpu.make_async_copy(ins[a], outs[a].at[slot(me)], local_sems.at[a]) for a in range(n)]
        for cp in mine:
            cp.start()
        first = []
        for a in range(n):
            first.append(copy(a, 0, me, sibling, src=ins[a]))
            first += [copy(a, 1 + j, me, (*chip, c), src=ins[a]) for j, chip in enumerate(chips)]
        for cp in first:
            cp.start()
        passed = []
        for j, chip in enumerate(chips):
            for a in range(n):
                copy(a, 1 + j, (*chip, c), me).wait_recv()
                cp = copy(a, 4 + j, (*chip, c), sibling)
                cp.start()
                passed.append(cp)
        for a in range(n):
            copy(a, 0, sibling, me).wait_recv()
            for j, chip in enumerate(chips):
                copy(a, 4 + j, (*chip, 1 - c), me).wait_recv()
        for cp in first + passed:
            cp.wait_send()
        for cp in mine:
            cp.wait()

    return pl.pallas_call(
        body, name=name,
        in_specs=[ANY] * n, out_specs=[ANY] * n,
        out_shape=[jax.ShapeDtypeStruct((N_DEV,) + x.shape, x.dtype) for x in xs],
        scratch_shapes=[pltpu.SemaphoreType.DMA((7 * n,)), pltpu.SemaphoreType.DMA((7 * n,)),
                        pltpu.SemaphoreType.DMA((n,))],
        compiler_params=pltpu.CompilerParams(has_side_effects=True),
    )(*xs)


ALL_MASKS = [(mx, my, mc) for mx in (0, 1) for my in (0, 1) for mc in (0, 1)][1:]


def _flip(v, bit):
    return 1 - v if bit else v


def _exchange_copies(ins, outs, send_sems, recv_sems, local_sems, *, masks, slot):
    n, n_peers = len(ins), len(masks)
    x, y, c = lax.axis_index("x"), lax.axis_index("y"), lax.axis_index("c")
    my_slot = slot((x, y, c))
    mine = [pltpu.make_async_copy(ins[a].at[my_slot], outs[a].at[my_slot], local_sems.at[a]) for a in range(n)]
    copies = []
    for a in range(n):
        for k, (mx, my, mc) in enumerate(masks):
            peer = (_flip(x, mx), _flip(y, my), _flip(c, mc))
            peer_slot = slot(peer)
            sems = dict(send_sem=send_sems.at[a * n_peers + k], recv_sem=recv_sems.at[a * n_peers + k],
                        device_id=peer, device_id_type=MESH)
            copies.append((
                pltpu.make_async_remote_copy(src_ref=ins[a].at[peer_slot], dst_ref=outs[a].at[my_slot], **sems),
                pltpu.make_async_remote_copy(src_ref=ins[a].at[peer_slot], dst_ref=outs[a].at[peer_slot], **sems)))
    return mine, copies


def _exchange(gs, *, masks, slot, name, bcast=None):
    n, n_peers = len(gs), len(masks)
    has_bcast = bcast is not None

    def body(*refs):
        n_in = n + has_bcast
        ins, outs = refs[:n], refs[n_in:n_in + n]
        send_sems, recv_sems, local_sems = refs[2 * n_in:2 * n_in + 3]
        x, y, c = lax.axis_index("x"), lax.axis_index("y"), lax.axis_index("c")
        mine, copies = _exchange_copies(ins, outs, send_sems, recv_sems, local_sems, masks=masks, slot=slot)
        if has_bcast:
            b_in, b_out = refs[n], refs[2 * n_in - 1]
            b_send, b_recv = refs[2 * n_in + 3:]
            me = 4 * x + 2 * y + c
            mine.append(pltpu.make_async_copy(b_in, b_out.at[me], local_sems.at[n]))
            for k, (mx, my, mc) in enumerate(ALL_MASKS):
                peer = (_flip(x, mx), _flip(y, my), _flip(c, mc))
                peer_id = 4 * peer[0] + 2 * peer[1] + peer[2]
                sems = dict(send_sem=b_send.at[k], recv_sem=b_recv.at[k], device_id=peer, device_id_type=MESH)
                copies.append((pltpu.make_async_remote_copy(src_ref=b_in, dst_ref=b_out.at[me], **sems),
                               pltpu.make_async_remote_copy(src_ref=b_in, dst_ref=b_out.at[peer_id], **sems)))
        for cp in mine:
            cp.start()
        for send, _ in copies:
            send.start()
        for send, recv in copies:
            recv.wait_recv()
            send.wait_send()
        for cp in mine:
            cp.wait()

    n_io = n + has_bcast
    out_shape = [jax.ShapeDtypeStruct(g.shape, g.dtype) for g in gs]
    scratch = [pltpu.SemaphoreType.DMA((n_peers * n,)), pltpu.SemaphoreType.DMA((n_peers * n,)),
               pltpu.SemaphoreType.DMA((n_io,))]
    if has_bcast:
        out_shape.append(jax.ShapeDtypeStruct((N_DEV,) + bcast.shape, bcast.dtype))
        scratch += [pltpu.SemaphoreType.DMA((len(ALL_MASKS),)), pltpu.SemaphoreType.DMA((len(ALL_MASKS),))]
    return pl.pallas_call(
        body, name=name,
        in_specs=[ANY] * n_io, out_specs=[ANY] * n_io, out_shape=out_shape, scratch_shapes=scratch,
        compiler_params=pltpu.CompilerParams(has_side_effects=True),
    )(*gs, *([bcast] if has_bcast else []))


SWAP_ROW_CHUNKS = 4


def _core_swap(gs, *, name):
    n = len(gs)

    def body(*refs):
        ins, outs = refs[:n], refs[n:2 * n]
        send_sems, recv_sems = refs[2 * n:]
        x, y, c = lax.axis_index("x"), lax.axis_index("y"), lax.axis_index("c")
        sibling = (x, y, 1 - c)
        started = []
        for a in range(n):
            _, Q, R, _ = ins[a].shape
            rows = R // SWAP_ROW_CHUNKS
            for q in range(Q):
                for j in range(SWAP_ROW_CHUNKS):
                    cp = pltpu.make_async_remote_copy(
                        src_ref=ins[a].at[1 - c, q, pl.ds(j * rows, rows)], dst_ref=outs[a].at[q, pl.ds(j * rows, rows)],
                        send_sem=send_sems.at[a], recv_sem=recv_sems.at[a], device_id=sibling, device_id_type=MESH)
                    cp.start()
                    started.append(cp)
        for a in range(n):
            pltpu.make_async_remote_copy(
                src_ref=ins[a].at[1 - c], dst_ref=outs[a], send_sem=send_sems.at[a], recv_sem=recv_sems.at[a],
                device_id=sibling, device_id_type=MESH).wait()

    return pl.pallas_call(
        body, name=name,
        in_specs=[ANY] * n, out_specs=[ANY] * n,
        out_shape=[jax.ShapeDtypeStruct(g.shape[1:], g.dtype) for g in gs],
        scratch_shapes=[pltpu.SemaphoreType.DMA((n,)), pltpu.SemaphoreType.DMA((n,))],
        compiler_params=pltpu.CompilerParams(has_side_effects=True),
    )(*gs)


def _pair_sum(g, other, core, *, name, tr_cap=256):
    _, Q, R, C = g.shape
    tr = max(t for t in range(16, min(R, tr_cap) + 1, 16) if R % t == 0)

    def body(core_ref, g_ref, o_ref, out_ref):
        out_ref[0] = (g_ref[0, 0] + o_ref[0]).astype(BF16)

    return pl.pallas_call(
        body, name=name,
        grid_spec=pltpu.PrefetchScalarGridSpec(
            num_scalar_prefetch=1, grid=(Q, R // tr),
            in_specs=[pl.BlockSpec((1, 1, tr, C), lambda q, i, core_ref: (core_ref[0], q, i, 0)),
                      pl.BlockSpec((1, tr, C), lambda q, i, core_ref: (q, i, 0))],
            out_specs=pl.BlockSpec((1, tr, C), lambda q, i, core_ref: (q, i, 0))),
        out_shape=jax.ShapeDtypeStruct((Q, R, C), BF16),
        compiler_params=_params(("parallel", "parallel")),
    )(core, g, other)


PACK_ROWS = 8


def _pack_small(norm1, norm2, final, att, hg, qn, kn, lb=None, loss=None):
    z = lambda n: jnp.zeros((n,), F32)
    rows = [norm1.reshape(-1), norm2.reshape(-1), final.reshape(-1),
            jnp.concatenate([att.reshape(-1), z(512)]),
            jnp.concatenate([hg.reshape(-1), qn.reshape(-1), kn.reshape(-1), z(1024 - 256)]),
            z(1024) if lb is None else lb.reshape(-1),
            z(1024) if loss is None else jnp.concatenate([loss.reshape(-1), z(1023)]), z(1024)]
    return jnp.stack(rows, axis=0)


def _unpack_small(p):
    return (p[0:1, :], p[1:2, :], p[2, :], p[3:4, 0:512], p[4:5, 0:128], p[4:5, 128:192], p[4:5, 192:256])


def _fold_heads(dhg, dqn, dkn, *, name):
    def body(hg_ref, q_ref, k_ref, ohg_ref, oq_ref, ok_ref):
        def fold128(v):
            acc = v[:, 0:LANES]
            for j in range(1, v.shape[1] // LANES):
                acc = acc + v[:, j * LANES:(j + 1) * LANES]
            return acc

        ohg_ref[...] = fold128(hg_ref[...])
        q = fold128(q_ref[...])
        oq_ref[...] = q + pltpu.roll(q, ATT_DH, 1)
        k = k_ref[...]
        ok_ref[...] = k + pltpu.roll(k, ATT_DH, 1)

    return pl.pallas_call(body, name=name, out_shape=[jax.ShapeDtypeStruct((1, LANES), F32)] * 3)(dhg, dqn, dkn)


def _lb_grad(dlb_sum, lb, *, name):
    def body(d_ref, lb_ref, o_ref):
        lbv = lb_ref[...]
        gl = d_ref[...] * lbv * (1.0 - lbv)
        o_ref[0:1, :] = gl[0:1, :]
        o_ref[1:2, :] = -gl[0:1, :]
        o_ref[2:3, :] = gl[1:2, :]
        o_ref[3:4, :] = -gl[1:2, :]

    return pl.pallas_call(body, name=name, out_shape=jax.ShapeDtypeStruct((4, HG_W), F32))(dlb_sum, lb)


def _lower_bounds(lb_logits_full, *, name):
    def body(l_ref, o_ref):
        for d in range(2):
            l0, l1 = l_ref[2 * d:2 * d + 1, :], l_ref[2 * d + 1:2 * d + 2, :]
            mx = jnp.maximum(l0, l1)
            e0, e1 = jnp.exp(l0 - mx), jnp.exp(l1 - mx)
            o_ref[d:d + 1, :] = e0 / (e0 + e1)

    return pl.pallas_call(body, name=name, out_shape=jax.ShapeDtypeStruct((2, HG_W), F32))(
        lb_logits_full.reshape(4, HG_W))


def _local_step(x, target, norm1_w, w_in, lb, hg_norm_w, q_norm_w, k_norm_w, att_norm_w, w_out, norm2_w,
                w_g, w_u, w_down, final_norm_w, reduce_early=None):
    T = x.shape[0]
    cos, sin = _rope_tables(T)
    qw8 = jnp.tile(q_norm_w, (1, ATT_HEADS))
    kw2 = jnp.tile(k_norm_w, (1, ATT_KV))

    h, r1 = _rms_fwd(x, norm1_w, name="norm1_fwd")
    U = _mm_nn([(h, w_in)], name="in_proj")
    o_f, st_f = _gla_fwd(U, lb[0:1], f_block=1, reverse=False, name="gla_fwd_f")
    o_b, st_b = _gla_fwd(U, lb[1:2], f_block=2, reverse=True, name="gla_fwd_b")
    mix_hg = _hg_post_fwd(o_f, o_b, U, hg_norm_w, name="hg_post_fwd")
    q_c, k, v, qn_c, kmax2 = _att_prep_fwd2(U, cos, sin, qw8, kw2, name="att_prep_fwd")
    kmax = jnp.sqrt(jnp.max(kmax2.reshape(ATT_KV, ATT_DH), axis=1))
    m_c = qn_c * (kmax * 1.001).reshape(ATT_KV, 1, 1, 1)
    o_c, lse = lax.cond(jnp.max(m_c) <= FA_BOUND_MAX,
                        lambda: _flash_fwd_bounded(q_c, k, v, m_c, name="flash_fwd_bounded"),
                        lambda: _flash_fwd2(q_c, k, v, name="flash_fwd"))
    o_att, mix_att = _att_post_fwd2(o_c, att_norm_w, name="att_post_fwd")
    x1 = _mm_nn([(mix_hg, w_out[:HG_W]), (mix_att, w_out[HG_W:])], residual=x, name="out_proj")
    h2, r2 = _rms_fwd(x1, norm2_w, name="norm2_fwd")
    gate, up, act = _ffn_up(h2, w_g, w_u, name="ffn_up")
    x2 = _mm_nn([(act, w_down)], residual=x1, name="ffn_down")
    loss, dx2, dx2b, d_final = _loss_head(x2, target, final_norm_w.reshape(1, D_MODEL), name="loss_head")

    d_gate, d_up = _ffn_act_bwd(dx2b, w_down, gate, up, name="ffn_act_bwd")
    dw_down = _mm_tn(act, dx2b, tma_cap=1408, name="dw_down")
    dh2 = _mm_nn([(d_gate, w_g), (d_up, w_u)], trans_b=True, tm=256, name="ffn_up_bwd")
    dw_g = _mm_tn(h2, d_gate, tnb_cap=1408, name="dw_gate")
    dw_u = _mm_tn(h2, d_up, tnb_cap=1408, name="dw_up")
    dx1, dx1b, d_norm2 = _rms_bwd(dh2, x1, r2, norm2_w, dx2, emit_bf16=True, name="norm2_bwd")
    dmix = _mm_nn([(dx1b, w_out)], trans_b=True, name="out_proj_bwd")
    dw_out = jnp.concatenate([_mm_tn(mix_hg, dx1b, name="dw_out_hg"), _mm_tn(mix_att, dx1b, name="dw_out_att")], axis=0)
    do_c, delta, d_att = _att_post_bwd2(dmix, o_att, att_norm_w, name="att_post_bwd")
    ride = None if reduce_early is None else reduce_early(dw_out, dw_g, dw_u, dw_down)
    dq_c, dk, dv, *rode = _flash_bwd2(q_c, k, v, do_c, lse, delta, ride=ride, name="flash_bwd")
    dU_att, d_qn, d_kn = _att_prep_bwd2(U, dq_c, dk, dv, cos, sin, qw8, kw2, name="att_prep_bwd")
    do_hg, du_g, d_hg = _hg_post_bwd(dmix, o_f, o_b, U, hg_norm_w, name="hg_post_bwd")
    dq_f, dz_f, dv_f, dlb_f = _gla_bwd(U, lb[0:1], do_hg, st_f, f_block=1, reverse=False, name="gla_bwd_f")
    dU_hg, dlb_b = _gla_bwd(U, lb[1:2], do_hg, st_b, f_block=2, reverse=True, prev=(dq_f, dz_f, dv_f, du_g),
                            name="gla_bwd_b")
    w_hg = 5 * HG_W
    dh = _mm_nn([(dU_hg, w_in[:, :w_hg]), (dU_att, w_in[:, w_hg:])], trans_b=True, name="in_proj_bwd")
    dw_in = jnp.concatenate([_mm_tn(h, dU_hg, tnb_cap=1280, name="dw_in_hg"), _mm_tn(h, dU_att, name="dw_in_att")],
                            axis=1)
    grad_x, d_norm1 = _rms_bwd(dh, x, r1, norm1_w, dx1, emit_bf16=False, name="norm1_bwd")
    d_hg, d_qn, d_kn = _fold_heads(d_hg, d_qn, d_kn, name="fold_heads")

    big = dict(w_in=dw_in, w_out=dw_out, w_g=dw_g, w_u=dw_u, w_down=dw_down)
    small = dict(norm1=d_norm1, norm2=d_norm2, final=d_final, att=d_att, hg=d_hg,
                 qn=d_qn[:, :ATT_DH], kn=d_kn[:, :ATT_DH], lb=jnp.concatenate([dlb_f, dlb_b], axis=0))
    return loss, grad_x, big, small, rode


def kernel(x, norm1_w, w_in, lb_logits, hg_norm_w, q_norm_w, k_norm_w, att_norm_w, w_out, norm2_w, w_gate_up, w_down, final_norm_w, loss_target, m_norm1_w, m_w_in, m_lb_logits, m_hg_norm_w, m_q_norm_w, m_k_norm_w, m_att_norm_w, m_w_out, m_norm2_w, m_w_gate_up, m_w_down, m_final_norm_w, v_norm1_w, v_w_in, v_lb_logits, v_hg_norm_w, v_q_norm_w, v_k_norm_w, v_att_norm_w, v_w_out, v_norm2_w, v_w_gate_up, v_w_down, v_final_norm_w):
    T = x.shape[1]
    me = 4 * lax.axis_index("x") + 2 * lax.axis_index("y") + lax.axis_index("c")
    c_in, r_out, c_gu, r_dn = w_in.shape[2], w_out.shape[1], w_gate_up.shape[2], w_down.shape[1]
    lb_cols = lb_logits.shape[2]

    g_in, g_out, g_gu, g_dn, g_lb = _all_gather(
        [w_in[0].astype(BF16), w_out[0].astype(BF16), w_gate_up[0].astype(BF16), w_down[0].astype(BF16),
         lb_logits.reshape(4, lb_cols)], name="gather_weights")
    w_in_f = g_in.transpose(1, 0, 2).reshape(D_MODEL, N_DEV * c_in)
    w_out_f = g_out.reshape(N_DEV * r_out, D_MODEL)
    half = N_DEV // 2
    w_g_f = g_gu[:half].transpose(1, 0, 2).reshape(D_MODEL, half * c_gu)
    w_u_f = g_gu[half:].transpose(1, 0, 2).reshape(D_MODEL, half * c_gu)
    w_dn_f = g_dn.reshape(N_DEV * r_dn, D_MODEL)
    lb_logits_f = g_lb.transpose(1, 0, 2).reshape(2, 2, N_DEV * lb_cols)
    lb = _lower_bounds(lb_logits_f, name="lower_bounds")

    chips = N_DEV // 2
    core = lax.axis_index("c").astype(jnp.int32).reshape(1)
    by_owner_cols = lambda g, n_q, w: g.reshape(D_MODEL, n_q, 2, w).transpose(2, 1, 0, 3)
    by_owner_rows = lambda g, r: g.reshape(chips, 2, r, D_MODEL).transpose(1, 0, 2, 3)

    def chip_sums(mine, names, call):
        theirs = _core_swap(mine, name=call)
        return [_pair_sum(g, o, core, name="pair_sum_" + nm) for g, o, nm in zip(mine, theirs, names)]

    def reduce_early(dw_out, dw_g, dw_u, dw_down):
        s_gu = jnp.concatenate([by_owner_cols(dw_g, chips // 2, c_gu), by_owner_cols(dw_u, chips // 2, c_gu)], axis=1)
        return chip_sums([by_owner_rows(dw_out, r_out), s_gu, by_owner_rows(dw_down, r_dn)],
                         ("w_out", "w_gu", "w_down"), "exchange_cores_early")

    loss, grad_x, big, small, (p_out, p_gu, p_dn) = _local_step(
        x[0], loss_target[0], norm1_w, w_in_f, lb, hg_norm_w, q_norm_w, k_norm_w, att_norm_w, w_out_f, norm2_w,
        w_g_f, w_u_f, w_dn_f, final_norm_w, reduce_early=reduce_early)

    packed = _pack_small(small["norm1"], small["norm2"], small["final"], small["att"], small["hg"],
                         small["qn"], small["kn"], small["lb"], loss)
    p_in, all_small = _exchange(chip_sums([by_owner_cols(big["w_in"], chips, c_in)], ("w_in",), "exchange_cores"),
                                masks=CHIP_MASKS, slot=_chip_slot, bcast=packed, name="exchange_chips")

    g_w_in, d_w_in, nm_w_in, nv_w_in = _adamw(p_in, w_in[0], m_w_in[0], v_w_in[0], name="adamw_w_in")
    g_w_out, d_w_out, nm_w_out, nv_w_out = _adamw(p_out, w_out[0], m_w_out[0], v_w_out[0], name="adamw_w_out")
    g_w_gu, d_w_gu, nm_w_gu, nv_w_gu = _adamw(p_gu, w_gate_up[0], m_w_gate_up[0], v_w_gate_up[0], name="adamw_w_gu")
    g_w_dn, d_w_dn, nm_w_dn, nv_w_dn = _adamw(p_dn, w_down[0], m_w_down[0], v_w_down[0], name="adamw_w_down")

    pk = lambda vecs: _pack_small(*vecs)
    w_pk = pk([norm1_w, norm2_w, final_norm_w, att_norm_w, hg_norm_w, q_norm_w, k_norm_w])
    m_pk = pk([m_norm1_w, m_norm2_w, m_final_norm_w, m_att_norm_w, m_hg_norm_w, m_q_norm_w, m_k_norm_w])
    v_pk = pk([v_norm1_w, v_norm2_w, v_final_norm_w, v_att_norm_w, v_hg_norm_w, v_q_norm_w, v_k_norm_w])
    g_pk, d_pk, nm_pk, nv_pk = _adamw(all_small, w_pk, m_pk, v_pk, name="adamw_small")

    dlb_sum = g_pk[5:6, :].reshape(2, HG_W)
    g_lb_full = _lb_grad(dlb_sum, lb, name="lb_grad")
    g_lb_mine = lax.dynamic_slice_in_dim(g_lb_full, me * lb_cols, lb_cols, axis=1)
    g_lb_s, d_lb, nm_lb, nv_lb = _adamw(g_lb_mine[None], lb_logits.reshape(4, lb_cols),
                                        m_lb_logits.reshape(4, lb_cols), v_lb_logits.reshape(4, lb_cols),
                                        name="adamw_lb")

    loss_total = g_pk[6, 0]

    def outs(big4, lb_arr, pk_arr):
        n1, n2, fin, att, hg, qn, kn = _unpack_small(pk_arr)
        b_in, b_out, b_gu, b_dn = big4
        return [n1, b_in[None], lb_arr.reshape(2, 2, lb_cols), hg, qn, kn, att, b_out[None], n2, b_gu[None],
                b_dn[None], fin]

    return (loss_total, grad_x[None],
            *outs((g_w_in, g_w_out, g_w_gu, g_w_dn), g_lb_s, g_pk),
            *outs((d_w_in, d_w_out, d_w_gu, d_w_dn), d_lb, d_pk),
            *outs((nm_w_in, nm_w_out, nm_w_gu, nm_w_dn), nm_lb, nm_pk),
            *outs((nv_w_in, nv_w_out, nv_w_gu, nv_w_dn), nv_lb, nv_pk))
```

```python
import functools
import math

import jax
import jax.numpy as jnp
import numpy as np
from jax import lax
from jax.experimental import pallas as pl
from jax.experimental.pallas import tpu as pltpu

F32 = jnp.float32
BF16 = jnp.bfloat16

N_DEV = 8
D_MODEL = 1024
EPS = 1e-6
HG_HEADS = 4
HG_D = 128
HG_W = HG_HEADS * HG_D
CHUNK = 64
ATT_HEADS = 8
ATT_KV = 2
ATT_G = ATT_HEADS // ATT_KV
ATT_DH = 64
ATT_QW = ATT_HEADS * ATT_DH
ATT_KW = ATT_KV * ATT_DH
GRID_W = 64
ROPE_THETA = 10000.0
D_IN = 5 * HG_W + ATT_QW + 2 * ATT_KW
D_FF = 2816
ADAM_LR, ADAM_B1, ADAM_B2, ADAM_EPS, ADAM_WD, ADAM_STEP = 0.001, 0.9, 0.999, 1e-08, 0.01, 10

LANES = 128
VMEM_LIMIT = 48 * 1024 * 1024
MESH = pl.DeviceIdType.MESH
ANY = pl.BlockSpec(memory_space=pl.ANY)


def _params(sem=None):
    return pltpu.CompilerParams(dimension_semantics=sem, vmem_limit_bytes=VMEM_LIMIT)


def _pick(n, cap):
    best = None
    for t in range(LANES, cap + 1, LANES):
        if n % t == 0:
            best = t
    assert best is not None, (n, cap)
    return best


def _sigmoid(x):
    return 1.0 / (1.0 + jnp.exp(-x))


def _dot(a, b):
    return jnp.dot(a.astype(BF16), b.astype(BF16), preferred_element_type=F32)


def _dot_nt(a, b):
    return lax.dot_general(a.astype(BF16), b.astype(BF16), (((1,), (1,)), ((), ())),
                           preferred_element_type=F32)


def _dot_tn(a, b):
    return lax.dot_general(a.astype(BF16), b.astype(BF16), (((0,), (0,)), ((), ())),
                           preferred_element_type=F32)


def _mm_nn(pairs, *, name, out_dtype=F32, residual=None, tm=512, tn_cap=None, trans_b=False):
    M = pairs[0][0].shape[0]
    N = pairs[0][1].shape[0 if trans_b else 1]
    tn = N if tn_cap is None else _pick(N, tn_cap)
    n_pairs = len(pairs)
    has_res = residual is not None
    dims = (((1,), (1,)), ((), ())) if trans_b else (((1,), (0,)), ((), ()))

    def body(*refs):
        acc = None
        for i in range(n_pairs):
            d = lax.dot_general(refs[2 * i][...], refs[2 * i + 1][...], dims, preferred_element_type=F32)
            acc = d if acc is None else acc + d
        if has_res:
            acc = acc + refs[2 * n_pairs][...]
        refs[-1][...] = acc.astype(out_dtype)

    in_specs, args = [], []
    for a, b in pairs:
        k = a.shape[1]
        b_spec = pl.BlockSpec((tn, k), lambda i, j: (j, 0)) if trans_b else pl.BlockSpec((k, tn), lambda i, j: (0, j))
        in_specs += [pl.BlockSpec((tm, k), lambda i, j: (i, 0)), b_spec]
        args += [a, b]
    if has_res:
        in_specs.append(pl.BlockSpec((tm, tn), lambda i, j: (i, j)))
        args.append(residual)
    return pl.pallas_call(
        body, name=name, grid=(M // tm, N // tn), in_specs=in_specs,
        out_specs=pl.BlockSpec((tm, tn), lambda i, j: (i, j)),
        out_shape=jax.ShapeDtypeStruct((M, N), out_dtype),
        compiler_params=_params(("parallel", "arbitrary")),
    )(*args)


def _mm_tn(a, b, *, name, tma_cap=1024, tnb_cap=1024, tk=1024):
    T, Ma = a.shape
    Nb = b.shape[1]
    tma, tnb = _pick(Ma, tma_cap), _pick(Nb, tnb_cap)
    tk = min(tk, T)
    n_k = T // tk

    def body(a_ref, b_ref, o_ref, acc_ref):
        k = pl.program_id(2)

        @pl.when(k == 0)
        def _():
            acc_ref[...] = jnp.zeros_like(acc_ref)

        acc_ref[...] += lax.dot_general(a_ref[...], b_ref[...], (((0,), (0,)), ((), ())),
                                        preferred_element_type=F32)

        @pl.when(k == n_k - 1)
        def _():
            o_ref[...] = acc_ref[...]

    return pl.pallas_call(
        body, name=name, grid=(Ma // tma, Nb // tnb, n_k),
        in_specs=[pl.BlockSpec((tk, tma), lambda i, j, k: (k, i)), pl.BlockSpec((tk, tnb), lambda i, j, k: (k, j))],
        out_specs=pl.BlockSpec((tma, tnb), lambda i, j, k: (i, j)),
        out_shape=jax.ShapeDtypeStruct((Ma, Nb), F32),
        scratch_shapes=[pltpu.VMEM((tma, tnb), F32)],
        compiler_params=_params(("parallel", "parallel", "arbitrary")),
    )(a, b)


def _rms_fwd(x, w, *, name, tm=512):
    T, Dm = x.shape

    def body(x_ref, w_ref, h_ref, r_ref):
        xv = x_ref[...]
        r = lax.rsqrt(jnp.mean(xv * xv, axis=-1, keepdims=True) + EPS)
        h_ref[...] = (xv * r * w_ref[...]).astype(BF16)
        r_ref[...] = r

    return pl.pallas_call(
        body, name=name, grid=(T // tm,),
        in_specs=[pl.BlockSpec((tm, Dm), lambda i: (i, 0)), pl.BlockSpec((1, Dm), lambda i: (0, 0))],
        out_specs=[pl.BlockSpec((tm, Dm), lambda i: (i, 0)), pl.BlockSpec((tm, 1), lambda i: (i, 0))],
        out_shape=[jax.ShapeDtypeStruct((T, Dm), BF16), jax.ShapeDtypeStruct((T, 1), F32)],
        compiler_params=_params(("parallel",)),
    )(x, w)


def _rms_bwd(dh, x, r, w, dres, *, name, emit_bf16, tm=512):
    T, Dm = x.shape

    def body(dh_ref, x_ref, r_ref, w_ref, dres_ref, *outs):
        dx_ref, dw_ref = outs[0], outs[-1]

        @pl.when(pl.program_id(0) == 0)
        def _():
            dw_ref[...] = jnp.zeros_like(dw_ref)

        rv = r_ref[...]
        xh = x_ref[...] * rv
        dhv = dh_ref[...]
        dxh = dhv * w_ref[...]
        t = jnp.mean(dxh * xh, axis=-1, keepdims=True)
        dx = dres_ref[...] + rv * (dxh - xh * t)
        dx_ref[...] = dx
        if emit_bf16:
            outs[1][...] = dx.astype(BF16)
        dw_ref[...] += jnp.sum(dhv * xh, axis=0, keepdims=True)

    row = pl.BlockSpec((tm, Dm), lambda i: (i, 0))
    vec = pl.BlockSpec((1, Dm), lambda i: (0, 0))
    out_specs = [row] + ([row] if emit_bf16 else []) + [vec]
    out_shape = ([jax.ShapeDtypeStruct((T, Dm), F32)] + ([jax.ShapeDtypeStruct((T, Dm), BF16)] if emit_bf16 else [])
                 + [jax.ShapeDtypeStruct((1, Dm), F32)])
    return pl.pallas_call(
        body, name=name, grid=(T // tm,),
        in_specs=[row, row, pl.BlockSpec((tm, 1), lambda i: (i, 0)), vec, row],
        out_specs=out_specs, out_shape=out_shape,
        compiler_params=_params(("arbitrary",)),
    )(dh, x, r, w, dres)


def _loss_head(x2, target, w, *, name, tm=512):
    T, Dm = x2.shape

    def body(x_ref, t_ref, w_ref, loss_ref, dx_ref, dxb_ref, dw_ref):
        @pl.when(pl.program_id(0) == 0)
        def _():
            loss_ref[...] = jnp.zeros_like(loss_ref)
            dw_ref[...] = jnp.zeros_like(dw_ref)

        xv = x_ref[...]
        r = lax.rsqrt(jnp.mean(xv * xv, axis=-1, keepdims=True) + EPS)
        xh = xv * r
        wv = w_ref[...]
        err = xh * wv - t_ref[...]
        row_loss = jnp.mean(err * err, axis=-1, keepdims=True)
        loss_ref[...] += 0.5 * jnp.sum(row_loss, axis=0, keepdims=True)
        dy = err * (1.0 / Dm)
        dxh = dy * wv
        t = jnp.mean(dxh * xh, axis=-1, keepdims=True)
        dx = r * (dxh - xh * t)
        dx_ref[...] = dx
        dxb_ref[...] = dx.astype(BF16)
        dw_ref[...] += jnp.sum(dy * xh, axis=0, keepdims=True)

    row = pl.BlockSpec((tm, Dm), lambda i: (i, 0))
    vec = pl.BlockSpec((1, Dm), lambda i: (0, 0))
    return pl.pallas_call(
        body, name=name, grid=(T // tm,),
        in_specs=[row, row, vec],
        out_specs=[pl.BlockSpec((1, 1), lambda i: (0, 0)), row, row, vec],
        out_shape=[jax.ShapeDtypeStruct((1, 1), F32), jax.ShapeDtypeStruct((T, Dm), F32),
                   jax.ShapeDtypeStruct((T, Dm), BF16), jax.ShapeDtypeStruct((1, Dm), F32)],
        compiler_params=_params(("arbitrary",)),
    )(x2, target, w)


GLA_TB = 512
GLA_NC = GLA_TB // CHUNK
GLA_UNROLL = 4


def _cumsum_rows(x, row, reverse):
    n = x.shape[0]
    s = 1
    while s < n:
        if not reverse:
            x = x + jnp.where(row >= s, pltpu.roll(x, s, 0), 0.0)
        else:
            x = x + jnp.where(row < n - s, pltpu.roll(x, n - s, 0), 0.0)
        s *= 2
    return x


def _gla_gates(uq, z, lbv):
    q = uq * _sigmoid(uq)
    sg = _sigmoid(z)
    sgn = _sigmoid(-z)
    f = lbv + (1.0 - lbv) * sg
    k = (1.0 - lbv) * sgn
    return q, sg, sgn, f, k


def _gla_decays(f, row, reverse):
    b = _cumsum_rows(jnp.log(f), row, reverse)
    if not reverse:
        bref, blast = b[CHUNK // 2 - 1:CHUNK // 2, :], b[CHUNK - 1:CHUNK, :]
    else:
        bref, blast = b[CHUNK // 2:CHUNK // 2 + 1, :], b[0:1, :]
    return b, bref, blast


def _gla_fwd(U, lb, *, f_block, reverse, name):
    T = U.shape[0]
    nb = T // GLA_TB

    def body(uq_ref, uf_ref, ui_ref, lb_ref, o_ref, st_ref, s_ref):
        @pl.when(pl.program_id(0) == 0)
        def _():
            s_ref[...] = jnp.zeros_like(s_ref)

        row = lax.broadcasted_iota(jnp.int32, (CHUNK, HG_D), 0)
        ri = lax.broadcasted_iota(jnp.int32, (CHUNK, CHUNK), 0)
        ci = lax.broadcasted_iota(jnp.int32, (CHUNK, CHUNK), 1)
        mask = (ri <= ci) if reverse else (ri >= ci)

        def chunk(j, carry):
            c = (GLA_NC - 1 - j) if reverse else j
            rows = pl.ds(pl.multiple_of(c * CHUNK, CHUNK), CHUNK)
            for h in range(HG_HEADS):
                cols = pl.ds(h * HG_D, HG_D)
                v = ui_ref[rows, cols]
                q, _, _, f, k = _gla_gates(uq_ref[rows, cols], uf_ref[rows, cols], lb_ref[:, cols])
                b, bref, blast = _gla_decays(f, row, reverse)
                s = jnp.where(mask, _dot_nt(q * jnp.exp(b - bref), k * jnp.exp(bref - b)), 0.0)
                st = s_ref[h]
                st_ref[c, h] = st
                o_ref[rows, cols] = _dot(s, v) + _dot_nt(q * jnp.exp(b), st)
                s_ref[h] = st * jnp.exp(blast) + _dot_tn(v, k * jnp.exp(blast - b))
            return carry

        lax.fori_loop(0, GLA_NC, chunk, 0, unroll=GLA_NC)

    blk = (lambda i: nb - 1 - i) if reverse else (lambda i: i)
    ucol = lambda cb: pl.BlockSpec((GLA_TB, HG_W), lambda i: (blk(i), cb))
    return pl.pallas_call(
        body, name=name, grid=(nb,),
        in_specs=[ucol(0), ucol(f_block), ucol(3), pl.BlockSpec((1, HG_W), lambda i: (0, 0))],
        out_specs=[pl.BlockSpec((GLA_TB, HG_W), lambda i: (blk(i), 0)),
                   pl.BlockSpec((GLA_NC, HG_HEADS, HG_D, HG_D), lambda i: (blk(i), 0, 0, 0))],
        out_shape=[jax.ShapeDtypeStruct((T, HG_W), F32),
                   jax.ShapeDtypeStruct((T // CHUNK, HG_HEADS, HG_D, HG_D), F32)],
        scratch_shapes=[pltpu.VMEM((HG_HEADS, HG_D, HG_D), F32)],
        compiler_params=_params(("arbitrary",)),
    )(U, U, U, lb)


def _gla_bwd(U, lb, do, states, *, f_block, reverse, name, prev=None):
    T = U.shape[0]
    nb = T // GLA_TB
    final = prev is not None

    def body(uq_ref, uf_ref, ui_ref, lb_ref, do_ref, st_ref, *rest):
        if final:
            dqp_ref, dzp_ref, dvp_ref, dug_ref, out_ref, dlb_ref, ds_ref = rest
        else:
            dq_ref, dz_ref, dv_ref, dlb_ref, ds_ref = rest

        @pl.when(pl.program_id(0) == 0)
        def _():
            ds_ref[...] = jnp.zeros_like(ds_ref)
            dlb_ref[...] = jnp.zeros_like(dlb_ref)

        row = lax.broadcasted_iota(jnp.int32, (CHUNK, HG_D), 0)
        ri = lax.broadcasted_iota(jnp.int32, (CHUNK, CHUNK), 0)
        ci = lax.broadcasted_iota(jnp.int32, (CHUNK, CHUNK), 1)
        mask = (ri <= ci) if reverse else (ri >= ci)

        def chunk(j, carry):
            c = j if reverse else (GLA_NC - 1 - j)
            rows = pl.ds(pl.multiple_of(c * CHUNK, CHUNK), CHUNK)
            for h in range(HG_HEADS):
                cols = pl.ds(h * HG_D, HG_D)
                v = ui_ref[rows, cols]
                lbv = lb_ref[:, cols]
                uq = uq_ref[rows, cols]
                q, sg, sgn, f, k = _gla_gates(uq, uf_ref[rows, cols], lbv)
                b, bref, blast = _gla_decays(f, row, reverse)
                eq, ek, eb, el, dec = (jnp.exp(b - bref), jnp.exp(bref - b), jnp.exp(b), jnp.exp(blast - b),
                                       jnp.exp(blast))
                qin, kin, qb, klast = q * eq, k * ek, q * eb, k * el
                dov = do_ref[rows, cols]
                st = st_ref[c, h]
                dst = ds_ref[h]
                p = jnp.where(mask, _dot_nt(qin, kin), 0.0)
                dp = jnp.where(mask, _dot_nt(dov, v), 0.0)
                dqin = _dot(dp, kin)
                dkin = _dot_tn(dp, qin)
                dv = _dot_tn(p, dov) + _dot_nt(klast, dst)
                dqb = _dot(dov, st)
                dklast = _dot(v, dst)
                ds_ref[h] = _dot_tn(dov, qb) + dst * dec
                db = dqin * qin - dkin * kin + dqb * qb - dklast * klast
                extra = (jnp.sum(dklast * klast, axis=0, keepdims=True)
                         + dec * jnp.sum(st * dst, axis=0, keepdims=True))
                dg = _cumsum_rows(db, row, not reverse) + extra
                dq = dqin * eq + dqb * eb
                dk = dkin * ek + dklast * el
                dfk = dg / f - dk
                dz = (dfk * (1.0 - lbv) * sg * sgn).astype(BF16)
                dlb_ref[:, cols] += jnp.sum(dfk * sgn, axis=0, keepdims=True)
                if final:
                    sq = _sigmoid(uq)
                    col = lambda blk: pl.ds(blk * HG_W + h * HG_D, HG_D)
                    out_ref[rows, col(0)] = ((dq + dqp_ref[rows, cols]) * (sq * (1.0 + uq * (1.0 - sq)))).astype(BF16)
                    out_ref[rows, col(1)] = dzp_ref[rows, cols]
                    out_ref[rows, col(2)] = dz
                    out_ref[rows, col(3)] = (dv + dvp_ref[rows, cols]).astype(BF16)
                    out_ref[rows, col(4)] = dug_ref[rows, cols]
                else:
                    dq_ref[rows, cols] = dq
                    dz_ref[rows, cols] = dz
                    dv_ref[rows, cols] = dv
            return carry

        lax.fori_loop(0, GLA_NC, chunk, 0, unroll=GLA_UNROLL)

    blk = (lambda i: i) if reverse else (lambda i: nb - 1 - i)
    ucol = lambda cb: pl.BlockSpec((GLA_TB, HG_W), lambda i: (blk(i), cb))
    tok = pl.BlockSpec((GLA_TB, HG_W), lambda i: (blk(i), 0))
    vec = pl.BlockSpec((1, HG_W), lambda i: (0, 0))
    in_specs = [ucol(0), ucol(f_block), ucol(3), vec, tok,
                pl.BlockSpec((GLA_NC, HG_HEADS, HG_D, HG_D), lambda i: (blk(i), 0, 0, 0))]
    vec_shape = jax.ShapeDtypeStruct((1, HG_W), F32)
    if final:
        in_specs += [tok] * 4
        out_specs = [pl.BlockSpec((GLA_TB, 5 * HG_W), lambda i: (blk(i), 0)), vec]
        out_shape = [jax.ShapeDtypeStruct((T, 5 * HG_W), BF16), vec_shape]
    else:
        out_specs = [tok, tok, tok, vec]
        out_shape = [jax.ShapeDtypeStruct((T, HG_W), F32), jax.ShapeDtypeStruct((T, HG_W), BF16),
                     jax.ShapeDtypeStruct((T, HG_W), F32), vec_shape]
    return pl.pallas_call(
        body, name=name, grid=(nb,), in_specs=in_specs, out_specs=out_specs, out_shape=out_shape,
        scratch_shapes=[pltpu.VMEM((HG_HEADS, HG_D, HG_D), F32)],
        compiler_params=_params(("arbitrary",)),
    )(U, U, U, lb, do, states, *(prev if final else ()))


def _hg_post_fwd(o_f, o_b, U, w, *, name, tm=512):
    T = o_f.shape[0]

    def body(of_ref, ob_ref, ug_ref, w_ref, out_ref):
        wv = w_ref[...]
        for h in range(HG_HEADS):
            cols = pl.ds(h * HG_D, HG_D)
            o = of_ref[:, cols] + ob_ref[:, cols]
            r = lax.rsqrt(jnp.mean(o * o, axis=-1, keepdims=True) + EPS)
            ug = ug_ref[:, cols]
            out_ref[:, cols] = (o * r * wv * (ug * _sigmoid(ug))).astype(BF16)

    tok = pl.BlockSpec((tm, HG_W), lambda i: (i, 0))
    return pl.pallas_call(
        body, name=name, grid=(T // tm,),
        in_specs=[tok, tok, pl.BlockSpec((tm, HG_W), lambda i: (i, 4)), pl.BlockSpec((1, HG_D), lambda i: (0, 0))],
        out_specs=tok, out_shape=jax.ShapeDtypeStruct((T, HG_W), BF16),
        compiler_params=_params(("parallel",)),
    )(o_f, o_b, U, w)


def _hg_post_bwd(dmix, o_f, o_b, U, w, *, name, tm=512):
    T = o_f.shape[0]

    def body(dm_ref, of_ref, ob_ref, ug_ref, w_ref, do_ref, dug_ref, dw_ref):
        @pl.when(pl.program_id(0) == 0)
        def _():
            dw_ref[...] = jnp.zeros_like(dw_ref)

        wv = w_ref[...]
        for h in range(HG_HEADS):
            cols = pl.ds(h * HG_D, HG_D)
            o = of_ref[:, cols] + ob_ref[:, cols]
            r = lax.rsqrt(jnp.mean(o * o, axis=-1, keepdims=True) + EPS)
            xh = o * r
            ug = ug_ref[:, cols]
            sg = _sigmoid(ug)
            dm = dm_ref[:, cols]
            dn = dm * (ug * sg)
            dug_ref[:, cols] = (dm * (xh * wv) * (sg * (1.0 + ug * (1.0 - sg)))).astype(BF16)
            dxh = dn * wv
            t = jnp.mean(dxh * xh, axis=-1, keepdims=True)
            do_ref[:, cols] = r * (dxh - xh * t)
            dw_ref[:, cols] += jnp.sum(dn * xh, axis=0, keepdims=True)

    tok = pl.BlockSpec((tm, HG_W), lambda i: (i, 0))
    vec = pl.BlockSpec((1, HG_W), lambda i: (0, 0))
    return pl.pallas_call(
        body, name=name, grid=(T // tm,),
        in_specs=[tok, tok, tok, pl.BlockSpec((tm, HG_W), lambda i: (i, 4)), pl.BlockSpec((1, HG_D), lambda i: (0, 0))],
        out_specs=[tok, tok, vec],
        out_shape=[jax.ShapeDtypeStruct((T, HG_W), F32), jax.ShapeDtypeStruct((T, HG_W), BF16),
                   jax.ShapeDtypeStruct((1, HG_W), F32)],
        compiler_params=_params(("arbitrary",)),
    )(dmix, o_f, o_b, U, w)


def _rope_tables(T):
    rows = T // GRID_W
    row = np.repeat(np.arange(rows), GRID_W).astype(np.float32)
    col = np.tile(np.arange(GRID_W), rows).astype(np.float32)
    axis_dim = ATT_DH // 2
    freqs = (np.float32(ROPE_THETA) ** (-np.arange(0, axis_dim, 2, dtype=np.float32) / np.float32(axis_dim))
             ).astype(np.float32)
    ang = np.concatenate([row[:, None] * freqs, col[:, None] * freqs], axis=-1).astype(np.float32)
    cos, sin = np.cos(ang), np.sin(ang)
    c = np.repeat(cos, 2, axis=-1)
    s = np.stack([-sin, sin], axis=-1).reshape(T, ATT_DH)
    return jnp.asarray(np.tile(c, (1, 2)), F32), jnp.asarray(np.tile(s, (1, 2)), F32)


def _head_blockdiag(width):
    shift = ATT_DH.bit_length() - 1
    ri = jnp.right_shift(lax.broadcasted_iota(jnp.int32, (width, width), 0), shift)
    ci = jnp.right_shift(lax.broadcasted_iota(jnp.int32, (width, width), 1), shift)
    return jnp.where(ri == ci, 1.0, 0.0).astype(BF16)


def _head_sum(x, bd):
    hi = x.astype(BF16)
    lo = (x - hi.astype(F32)).astype(BF16)
    return jnp.dot(hi, bd, preferred_element_type=F32) + jnp.dot(lo, bd, preferred_element_type=F32)


def _pair_swap(x, even):
    n = x.shape[-1]
    return jnp.where(even, pltpu.roll(x, n - 1, 1), pltpu.roll(x, 1, 1))


def _att_prep_fwd(U, cos, sin, qw, kw, *, name, tm=512):
    T = U.shape[0]
    scale = ATT_DH ** -0.5

    def body(aq_ref, ak_ref, av_ref, c_ref, s_ref, qw_ref, kw_ref, q_ref, k_ref, v_ref):
        bd = _head_blockdiag(ATT_QW)
        c2, s2 = c_ref[...], s_ref[...]
        c8, s8 = jnp.tile(c2, (1, 4)), jnp.tile(s2, (1, 4))

        def norm_rope(x, w, c, s, bdm):
            r = lax.rsqrt(_head_sum(x * x, bdm) * (1.0 / ATT_DH) + EPS)
            y = x * r * w
            even = (lax.broadcasted_iota(jnp.int32, y.shape, 1) & 1) == 0
            return y * c + _pair_swap(y, even) * s

        q_ref[...] = (norm_rope(aq_ref[...], qw_ref[...], c8, s8, bd) * scale).astype(BF16)
        k_ref[...] = norm_rope(ak_ref[...], kw_ref[...], c2, s2, bd[:ATT_KW, :ATT_KW]).astype(BF16)
        v_ref[...] = av_ref[...].astype(BF16)

    kv_spec = pl.BlockSpec((tm, ATT_KW), lambda i: (i, 0))
    return pl.pallas_call(
        body, name=name, grid=(T // tm,),
        in_specs=[pl.BlockSpec((tm, ATT_QW), lambda i: (i, 5)),
                  pl.BlockSpec((tm, ATT_KW), lambda i: (i, 24)), pl.BlockSpec((tm, ATT_KW), lambda i: (i, 25)),
                  kv_spec, kv_spec,
                  pl.BlockSpec((1, ATT_QW), lambda i: (0, 0)), pl.BlockSpec((1, ATT_KW), lambda i: (0, 0))],
        out_specs=[pl.BlockSpec((tm, ATT_QW), lambda i: (i, 0)), kv_spec, kv_spec],
        out_shape=[jax.ShapeDtypeStruct((T, ATT_QW), BF16), jax.ShapeDtypeStruct((T, ATT_KW), BF16),
                   jax.ShapeDtypeStruct((T, ATT_KW), BF16)],
        compiler_params=_params(("parallel",)),
    )(U, U, U, cos, sin, qw, kw)


def _att_prep_bwd(U, dq, dk, cos, sin, qw, kw, *, name, tm=512):
    T = U.shape[0]
    scale = ATT_DH ** -0.5

    def body(aq_ref, ak_ref, dq_ref, dk_ref, c_ref, s_ref, qw_ref, kw_ref, daq_ref, dak_ref, dqw_ref, dkw_ref):
        @pl.when(pl.program_id(0) == 0)
        def _():
            dqw_ref[...] = jnp.zeros_like(dqw_ref)
            dkw_ref[...] = jnp.zeros_like(dkw_ref)

        bd = _head_blockdiag(ATT_QW)
        c2, s2 = c_ref[...], s_ref[...]
        c8, s8 = jnp.tile(c2, (1, 4)), jnp.tile(s2, (1, 4))

        def bwd(x, dy, w, c, s, bdm):
            even = (lax.broadcasted_iota(jnp.int32, x.shape, 1) & 1) == 0
            dn = dy * c - _pair_swap(dy, even) * s
            r = lax.rsqrt(_head_sum(x * x, bdm) * (1.0 / ATT_DH) + EPS)
            xh = x * r
            dxh = dn * w
            t = _head_sum(dxh * xh, bdm) * (1.0 / ATT_DH)
            return r * (dxh - xh * t), jnp.sum(dn * xh, axis=0, keepdims=True)

        da, dw = bwd(aq_ref[...], dq_ref[...] * scale, qw_ref[...], c8, s8, bd)
        daq_ref[...] = da
        dqw_ref[...] += dw
        da, dw = bwd(ak_ref[...], dk_ref[...], kw_ref[...], c2, s2, bd[:ATT_KW, :ATT_KW])
        dak_ref[...] = da
        dkw_ref[...] += dw

    q_spec = pl.BlockSpec((tm, ATT_QW), lambda i: (i, 0))
    kv_spec = pl.BlockSpec((tm, ATT_KW), lambda i: (i, 0))
    qv = pl.BlockSpec((1, ATT_QW), lambda i: (0, 0))
    kv = pl.BlockSpec((1, ATT_KW), lambda i: (0, 0))
    return pl.pallas_call(
        body, name=name, grid=(T // tm,),
        in_specs=[pl.BlockSpec((tm, ATT_QW), lambda i: (i, 5)), pl.BlockSpec((tm, ATT_KW), lambda i: (i, 24)),
                  q_spec, kv_spec, kv_spec, kv_spec, qv, kv],
        out_specs=[q_spec, kv_spec, qv, kv],
        out_shape=[jax.ShapeDtypeStruct((T, ATT_QW), F32), jax.ShapeDtypeStruct((T, ATT_KW), F32),
                   jax.ShapeDtypeStruct((1, ATT_QW), F32), jax.ShapeDtypeStruct((1, ATT_KW), F32)],
        compiler_params=_params(("arbitrary",)),
    )(U, U, dq, dk, cos, sin, qw, kw)


FA_TQ = 256
FA_TK = 256
FA_SW = 128


def _fa_tiles(T):
    tq, tk = min(FA_TQ, T), min(FA_TK, T)
    return tq, tk, T // tq, T // tk


def _to_fa_cols(a, T):
    tq, _, nq, _ = _fa_tiles(T)
    return a.reshape(nq, tq, ATT_KV, ATT_G, ATT_DH).transpose(2, 0, 4, 3, 1).reshape(ATT_KV, nq, ATT_DH, ATT_G * tq)


def _to_fa_rows(a, T):
    tq, _, nq, _ = _fa_tiles(T)
    return a.reshape(nq, tq, ATT_KV, ATT_G, ATT_DH).transpose(2, 0, 3, 1, 4).reshape(ATT_KV, nq, ATT_G * tq, ATT_DH)


def _from_fa_cols(a, T):
    tq, _, nq, _ = _fa_tiles(T)
    return a.reshape(ATT_KV, nq, ATT_DH, ATT_G, tq).transpose(1, 4, 0, 3, 2).reshape(T, ATT_QW)


def _kv_rows(a, T):
    _, tk, _, n_k = _fa_tiles(T)
    return a.reshape(n_k, tk, ATT_KV, ATT_DH).transpose(2, 0, 1, 3)


def _kv_cols(a, T):
    _, tk, _, n_k = _fa_tiles(T)
    return a.reshape(n_k, tk, ATT_KV, ATT_DH).transpose(2, 0, 3, 1)


def _flash_fwd(q_c, k_r, v_c, *, name):
    _, nq, _, R = q_c.shape
    _, n_k, tk, _ = k_r.shape

    def body(q_ref, k_ref, v_ref, o_ref, lse_ref):
        for st in range(R // FA_SW):
            lanes = pl.ds(st * FA_SW, FA_SW)
            qv = q_ref[0, 0, :, lanes]

            def step(j, carry):
                m, l, acc = carry
                s = jnp.dot(k_ref[0, j], qv, preferred_element_type=F32)
                m_new = jnp.maximum(m, jnp.max(s, axis=0, keepdims=True))
                alpha = jnp.exp(m - m_new)
                p = jnp.exp(s - m_new)
                l = alpha * l + jnp.sum(p, axis=0, keepdims=True)
                acc = alpha * acc + jnp.dot(v_ref[0, j], p.astype(BF16), preferred_element_type=F32)
                return m_new, l, acc

            m, l, acc = lax.fori_loop(0, n_k, step, (jnp.full((1, FA_SW), -jnp.inf, F32), jnp.zeros((1, FA_SW), F32),
                                                     jnp.zeros((ATT_DH, FA_SW), F32)))
            o_ref[0, 0, :, lanes] = acc / l
            lse_ref[0, 0, :, lanes] = m + jnp.log(l)

    qspec = pl.BlockSpec((1, 1, ATT_DH, R), lambda h, i: (h, i, 0, 0))
    return pl.pallas_call(
        body, name=name, grid=(ATT_KV, nq),
        in_specs=[qspec, pl.BlockSpec((1, n_k, tk, ATT_DH), lambda h, i: (h, 0, 0, 0)),
                  pl.BlockSpec((1, n_k, ATT_DH, tk), lambda h, i: (h, 0, 0, 0))],
        out_specs=[qspec, pl.BlockSpec((1, 1, 1, R), lambda h, i: (h, i, 0, 0))],
        out_shape=[jax.ShapeDtypeStruct((ATT_KV, nq, ATT_DH, R), F32), jax.ShapeDtypeStruct((ATT_KV, nq, 1, R), F32)],
        compiler_params=_params(("parallel", "parallel")),
    )(q_c, k_r, v_c)


def _flash_bwd(q_c, q_r, k_r, k_c, v_r, do_c, do_r, o_c, lse, *, name):
    _, nq, _, R = q_c.shape
    _, n_k, tk, _ = k_r.shape

    def body(qc_ref, qr_ref, kr_ref, kc_ref, vr_ref, doc_ref, dor_ref, oc_ref, lse_ref, dq_ref, dk_ref, dv_ref,
             acc_ref):
        @pl.when(pl.program_id(1) == 0)
        def _():
            dk_ref[...] = jnp.zeros_like(dk_ref)
            dv_ref[...] = jnp.zeros_like(dv_ref)

        qc, doc = qc_ref[0, 0], doc_ref[0, 0]
        qr, dor = qr_ref[0, 0], dor_ref[0, 0]
        delta = jnp.sum(doc.astype(F32) * oc_ref[0, 0], axis=0, keepdims=True)
        lsev = lse_ref[0, 0]
        acc_ref[...] = jnp.zeros_like(acc_ref)

        def step(j, carry):
            s = jnp.dot(kr_ref[0, j], qc, preferred_element_type=F32)
            p = jnp.exp(s - lsev)
            dp = jnp.dot(vr_ref[0, j], doc, preferred_element_type=F32)
            ds = (p * (dp - delta)).astype(BF16)
            acc_ref[...] += jnp.dot(kc_ref[0, j], ds, preferred_element_type=F32)
            dk_ref[0, j] += jnp.dot(ds, qr, preferred_element_type=F32)
            dv_ref[0, j] += jnp.dot(p.astype(BF16), dor, preferred_element_type=F32)
            return carry

        lax.fori_loop(0, n_k, step, 0)
        dq_ref[0, 0] = acc_ref[...]

    cspec = pl.BlockSpec((1, 1, ATT_DH, R), lambda h, i: (h, i, 0, 0))
    rspec = pl.BlockSpec((1, 1, R, ATT_DH), lambda h, i: (h, i, 0, 0))
    krspec = pl.BlockSpec((1, n_k, tk, ATT_DH), lambda h, i: (h, 0, 0, 0))
    kcspec = pl.BlockSpec((1, n_k, ATT_DH, tk), lambda h, i: (h, 0, 0, 0))
    return pl.pallas_call(
        body, name=name, grid=(ATT_KV, nq),
        in_specs=[cspec, rspec, krspec, kcspec, krspec, cspec, rspec, cspec,
                  pl.BlockSpec((1, 1, 1, R), lambda h, i: (h, i, 0, 0))],
        out_specs=[cspec, krspec, krspec],
        out_shape=[jax.ShapeDtypeStruct((ATT_KV, nq, ATT_DH, R), F32),
                   jax.ShapeDtypeStruct((ATT_KV, n_k, tk, ATT_DH), F32),
                   jax.ShapeDtypeStruct((ATT_KV, n_k, tk, ATT_DH), F32)],
        scratch_shapes=[pltpu.VMEM((ATT_DH, R), F32)],
        compiler_params=_params(("parallel", "arbitrary")),
    )(q_c, q_r, k_r, k_c, v_r, do_c, do_r, o_c, lse)


def _att_post_fwd(o, w, *, name, tm=512):
    T = o.shape[0]

    def body(o_ref, w_ref, out_ref):
        ov = o_ref[...]
        r = lax.rsqrt(jnp.mean(ov * ov, axis=-1, keepdims=True) + EPS)
        out_ref[...] = (ov * r * w_ref[...]).astype(BF16)

    tok = pl.BlockSpec((tm, ATT_QW), lambda i: (i, 0))
    return pl.pallas_call(
        body, name=name, grid=(T // tm,),
        in_specs=[tok, pl.BlockSpec((1, ATT_QW), lambda i: (0, 0))],
        out_specs=tok, out_shape=jax.ShapeDtypeStruct((T, ATT_QW), BF16),
        compiler_params=_params(("parallel",)),
    )(o, w)


def _att_post_bwd(dmix, o, w, *, name, tm=512):
    T = o.shape[0]

    def body(dm_ref, o_ref, w_ref, do_ref, dw_ref):
        @pl.when(pl.program_id(0) == 0)
        def _():
            dw_ref[...] = jnp.zeros_like(dw_ref)

        ov = o_ref[...]
        r = lax.rsqrt(jnp.mean(ov * ov, axis=-1, keepdims=True) + EPS)
        xh = ov * r
        dm = dm_ref[...]
        dxh = dm * w_ref[...]
        t = jnp.mean(dxh * xh, axis=-1, keepdims=True)
        do_ref[...] = (r * (dxh - xh * t)).astype(BF16)
        dw_ref[...] += jnp.sum(dm * xh, axis=0, keepdims=True)

    tok = pl.BlockSpec((tm, ATT_QW), lambda i: (i, 0))
    vec = pl.BlockSpec((1, ATT_QW), lambda i: (0, 0))
    return pl.pallas_call(
        body, name=name, grid=(T // tm,),
        in_specs=[pl.BlockSpec((tm, ATT_QW), lambda i: (i, 1)), tok, vec],
        out_specs=[tok, vec],
        out_shape=[jax.ShapeDtypeStruct((T, ATT_QW), BF16), jax.ShapeDtypeStruct((1, ATT_QW), F32)],
        compiler_params=_params(("arbitrary",)),
    )(dmix, o, w)


FA_HP = ATT_KV * ATT_DH
FA_TK_FWD = 512
FA_TK_BWD = 512


def _cols_from_tokens(x, kv):
    w = ATT_G * ATT_DH
    xt = x[:, kv * w:(kv + 1) * w].T
    return jnp.concatenate([xt[g * ATT_DH:(g + 1) * ATT_DH, :] for g in range(ATT_G)], axis=1)


def _tokens_from_cols(c):
    tq = c.shape[1] // ATT_G
    return jnp.concatenate([c[:, g * tq:(g + 1) * tq] for g in range(ATT_G)], axis=0).T


def _store_padded_cols(ref, x, norm_ref=None):
    for kv in range(ATT_KV):
        cols = _cols_from_tokens(x, kv).astype(BF16)
        ref[kv, 0] = cols
        if norm_ref is not None:
            cf = cols.astype(F32)
            norm_ref[kv, 0] = jnp.sqrt(jnp.sum(cf * cf, axis=0, keepdims=True))


def _att_prep_fwd2(U, cos, sin, qw, kw, *, name):
    T = U.shape[0]
    tm = min(FA_TQ, T)
    R = ATT_G * tm
    scale = ATT_DH ** -0.5

    def head_rows(ref, x):
        xt = x.astype(F32).T
        for kv in range(ATT_KV):
            ref[kv, 0] = xt[kv * ATT_DH:(kv + 1) * ATT_DH, :].astype(BF16)

    def body(aq_ref, ak_ref, av_ref, c_ref, s_ref, qw_ref, kw_ref, q_ref, qn_ref, kmax_ref, kc_ref, vc_ref):
        @pl.when(pl.program_id(0) == 0)
        def _():
            kmax_ref[...] = jnp.zeros_like(kmax_ref)

        bd = _head_blockdiag(ATT_QW)
        c2, s2 = c_ref[...], s_ref[...]
        c8, s8 = jnp.tile(c2, (1, 4)), jnp.tile(s2, (1, 4))

        def norm_rope(x, w, c, s, bdm):
            r = lax.rsqrt(_head_sum(x * x, bdm) * (1.0 / ATT_DH) + EPS)
            y = x * r * w
            even = (lax.broadcasted_iota(jnp.int32, y.shape, 1) & 1) == 0
            return y * c + _pair_swap(y, even) * s

        _store_padded_cols(q_ref, norm_rope(aq_ref[...], qw_ref[...], c8, s8, bd) * scale, qn_ref)
        kb = norm_rope(ak_ref[...], kw_ref[...], c2, s2, bd[:ATT_KW, :ATT_KW]).astype(BF16)
        kf = kb.astype(F32)
        ksq = _head_sum(kf * kf, bd[:ATT_KW, :ATT_KW])
        kmax_ref[...] = jnp.maximum(kmax_ref[...], jnp.max(ksq, axis=0, keepdims=True))
        head_rows(kc_ref, kb)
        head_rows(vc_ref, av_ref[...].astype(BF16))

    kv_spec = pl.BlockSpec((tm, ATT_KW), lambda i: (i, 0))
    tk = min(FA_TK_FWD, T)
    per = tk // tm
    c_spec = pl.BlockSpec((ATT_KV, 1, ATT_DH, tm), lambda i: (0, i // per, 0, i % per))
    c_shape = jax.ShapeDtypeStruct((ATT_KV, T // tk, ATT_DH, tk), BF16)
    return pl.pallas_call(
        body, name=name, grid=(T // tm,),
        in_specs=[pl.BlockSpec((tm, ATT_QW), lambda i: (i, 5)),
                  pl.BlockSpec((tm, ATT_KW), lambda i: (i, 24)), pl.BlockSpec((tm, ATT_KW), lambda i: (i, 25)),
                  kv_spec, kv_spec,
                  pl.BlockSpec((1, ATT_QW), lambda i: (0, 0)), pl.BlockSpec((1, ATT_KW), lambda i: (0, 0))],
        out_specs=[pl.BlockSpec((ATT_KV, 1, ATT_DH, R), lambda i: (0, i, 0, 0)),
                   pl.BlockSpec((ATT_KV, 1, 1, R), lambda i: (0, i, 0, 0)), pl.BlockSpec((1, ATT_KW), lambda i: (0, 0)),
                   c_spec, c_spec],
        out_shape=[jax.ShapeDtypeStruct((ATT_KV, T // tm, ATT_DH, R), BF16),
                   jax.ShapeDtypeStruct((ATT_KV, T // tm, 1, R), F32), jax.ShapeDtypeStruct((1, ATT_KW), F32),
                   c_shape, c_shape],
        compiler_params=_params(("arbitrary",)),
    )(U, U, U, cos, sin, qw, kw)


def _att_prep_bwd2(U, dq_c, dk_c, dv_c, cos, sin, qw, kw, *, name):
    T = U.shape[0]
    tm = min(FA_TQ, T)
    R = ATT_G * tm
    scale = ATT_DH ** -0.5

    def body(aq_ref, ak_ref, dq_ref, dk_ref, dv_ref, c_ref, s_ref, qw_ref, kw_ref, out_ref, dqw_ref, dkw_ref):
        @pl.when(pl.program_id(0) == 0)
        def _():
            dqw_ref[...] = jnp.zeros_like(dqw_ref)
            dkw_ref[...] = jnp.zeros_like(dkw_ref)

        bd = _head_blockdiag(ATT_QW)
        c2, s2 = c_ref[...], s_ref[...]
        c8, s8 = jnp.tile(c2, (1, 4)), jnp.tile(s2, (1, 4))

        def bwd(x, dy, w, c, s, bdm):
            even = (lax.broadcasted_iota(jnp.int32, x.shape, 1) & 1) == 0
            dn = dy * c - _pair_swap(dy, even) * s
            r = lax.rsqrt(_head_sum(x * x, bdm) * (1.0 / ATT_DH) + EPS)
            xh = x * r
            dxh = dn * w
            t = _head_sum(dxh * xh, bdm) * (1.0 / ATT_DH)
            return r * (dxh - xh * t), jnp.sum(dn * xh, axis=0, keepdims=True)

        dq = jnp.concatenate([_tokens_from_cols(dq_ref[kv, 0]) for kv in range(ATT_KV)], axis=1)
        da, dw = bwd(aq_ref[...], dq * scale, qw_ref[...], c8, s8, bd)
        out_ref[:, 0:ATT_QW] = da.astype(BF16)
        dqw_ref[...] += dw
        tokens = lambda ref: jnp.concatenate([ref[kv, 0] for kv in range(ATT_KV)], axis=0).T
        da, dw = bwd(ak_ref[...], tokens(dk_ref), kw_ref[...], c2, s2, bd[:ATT_KW, :ATT_KW])
        out_ref[:, ATT_QW:ATT_QW + ATT_KW] = da.astype(BF16)
        dkw_ref[...] += dw
        out_ref[:, ATT_QW + ATT_KW:ATT_QW + 2 * ATT_KW] = tokens(dv_ref).astype(BF16)

    kv_spec = pl.BlockSpec((tm, ATT_KW), lambda i: (i, 0))
    qv = pl.BlockSpec((1, ATT_QW), lambda i: (0, 0))
    kv = pl.BlockSpec((1, ATT_KW), lambda i: (0, 0))
    w_att = ATT_QW + 2 * ATT_KW
    per = dk_c.shape[3] // tm
    c_spec = pl.BlockSpec((ATT_KV, 1, ATT_DH, tm), lambda i: (0, i // per, 0, i % per))
    return pl.pallas_call(
        body, name=name, grid=(T // tm,),
        in_specs=[pl.BlockSpec((tm, ATT_QW), lambda i: (i, 5)), pl.BlockSpec((tm, ATT_KW), lambda i: (i, 24)),
                  pl.BlockSpec((ATT_KV, 1, ATT_DH, R), lambda i: (0, i, 0, 0)), c_spec, c_spec, kv_spec, kv_spec, qv, kv],
        out_specs=[pl.BlockSpec((tm, w_att), lambda i: (i, 0)), qv, kv],
        out_shape=[jax.ShapeDtypeStruct((T, w_att), BF16),
                   jax.ShapeDtypeStruct((1, ATT_QW), F32), jax.ShapeDtypeStruct((1, ATT_KW), F32)],
        compiler_params=_params(("arbitrary",)),
    )(U, U, dq_c, dk_c, dv_c, cos, sin, qw, kw)


def _pick_head(x, kv):
    return jnp.where(kv == 0, x[0:ATT_DH, :], x[ATT_DH:FA_HP, :])


def _scores(k_ref, j, qv):
    return lax.dot_general(k_ref[0, j], qv, (((0,), (0,)), ((), ())), preferred_element_type=F32)


def _flash_fwd2(q_c, k_c, v_c, *, name):
    _, nq, _, R = q_c.shape
    _, n_k, _, tk = v_c.shape

    def body(q_ref, k_ref, v_ref, o_ref, lse_ref, acc_ref):
        qv = q_ref[0, 0]
        acc_ref[...] = jnp.zeros_like(acc_ref)

        def step(j, carry):
            m, l = carry
            s = _scores(k_ref, j, qv)
            m_new = jnp.maximum(m, jnp.max(s, axis=0, keepdims=True))
            alpha = jnp.exp(m - m_new)
            p = jnp.exp(s - m_new)
            l = alpha * l + jnp.sum(p, axis=0, keepdims=True)
            acc_ref[...] = alpha * acc_ref[...] + jnp.dot(v_ref[0, j], p.astype(BF16), preferred_element_type=F32)
            return m_new, l

        m, l = lax.fori_loop(0, n_k, step, (jnp.full((1, R), -jnp.inf, F32), jnp.zeros((1, R), F32)))
        o_ref[0, 0] = acc_ref[...] / l
        lse_ref[0, 0] = m + jnp.log(l)

    cspec = pl.BlockSpec((1, 1, ATT_DH, R), lambda h, i: (h, i, 0, 0))
    kspec = pl.BlockSpec((1, n_k, ATT_DH, tk), lambda h, i: (h, 0, 0, 0))
    return pl.pallas_call(
        body, name=name, grid=(ATT_KV, nq),
        in_specs=[cspec, kspec, kspec],
        out_specs=[cspec, pl.BlockSpec((1, 1, 1, R), lambda h, i: (h, i, 0, 0))],
        out_shape=[jax.ShapeDtypeStruct((ATT_KV, nq, ATT_DH, R), F32), jax.ShapeDtypeStruct((ATT_KV, nq, 1, R), F32)],
        scratch_shapes=[pltpu.VMEM((ATT_DH, R), F32)],
        compiler_params=_params(("parallel", "parallel")),
    )(q_c, k_c, v_c)


FA_BOUND_MAX = 40.0
FA_TK_FAST = 512


def _flash_fwd_bounded(q_c, k_c, v_c, m_c, *, name):
    _, nq, _, R = q_c.shape
    _, n_k, _, tk = v_c.shape

    def body(q_ref, k_ref, v_ref, m_ref, o_ref, lse_ref, acc_ref):
        qv = q_ref[0, 0]
        m = m_ref[0, 0]
        acc_ref[...] = jnp.zeros_like(acc_ref)

        per = math.gcd(n_k, 4)

        def step(jj, l8):
            pv = None
            for u in range(per):
                j = per * jj + u
                p = jnp.exp(_scores(k_ref, j, qv) - m)
                l8 = l8 + jnp.sum(p.reshape(tk // 8, 8, R), axis=0)
                d = jnp.dot(v_ref[0, j], p.astype(BF16), preferred_element_type=F32)
                pv = d if pv is None else pv + d
            acc_ref[...] += pv
            return l8

        l8 = lax.fori_loop(0, n_k // per, step, jnp.zeros((8, R), F32))
        l = jnp.sum(l8, axis=0, keepdims=True)
        o_ref[0, 0] = acc_ref[...] / l
        lse_ref[0, 0] = m + jnp.log(l)

    cspec = pl.BlockSpec((1, 1, ATT_DH, R), lambda h, i: (h, i, 0, 0))
    kspec = pl.BlockSpec((1, n_k, ATT_DH, tk), lambda h, i: (h, 0, 0, 0))
    vspec = pl.BlockSpec((1, 1, 1, R), lambda h, i: (h, i, 0, 0))
    return pl.pallas_call(
        body, name=name, grid=(ATT_KV, nq),
        in_specs=[cspec, kspec, kspec, vspec],
        out_specs=[cspec, vspec],
        out_shape=[jax.ShapeDtypeStruct((ATT_KV, nq, ATT_DH, R), F32), jax.ShapeDtypeStruct((ATT_KV, nq, 1, R), F32)],
        scratch_shapes=[pltpu.VMEM((ATT_DH, R), F32)],
        compiler_params=_params(("parallel", "parallel")),
    )(q_c, k_c, v_c, m_c)


CHIP_MASKS = [(1, 0, 0), (0, 1, 0), (1, 1, 0)]


def _chip_slot(p):
    return 2 * p[0] + p[1]


def _flash_bwd2(q_c, k_c, v_c, do_c, lse, delta, *, name, ride=None):
    _, nq, _, R = q_c.shape
    _, n_k, _, tk = k_c.shape
    n_ride = 0 if ride is None else len(ride)

    def body(qc_ref, kc_ref, vc_ref, doc_ref, lse_ref, delta_ref, *rest):
        ride_in, rest = rest[:n_ride], rest[n_ride:]
        dq_ref, dk_ref, dv_ref = rest[:3]
        ride_out, rest = rest[3:3 + n_ride], rest[3 + n_ride:]
        acc_ref = rest[0]
        kv = pl.program_id(0)
        first = (kv == 0) & (pl.program_id(1) == 0)

        if n_ride:
            mine, copies = _exchange_copies(ride_in, ride_out, *rest[1:], masks=CHIP_MASKS, slot=_chip_slot)

            @pl.when(first)
            def _():
                for cp in mine:
                    cp.start()
                for send, _ in copies:
                    send.start()

        @pl.when(pl.program_id(1) == 0)
        def _():
            dk_ref[...] = jnp.zeros_like(dk_ref)
            dv_ref[...] = jnp.zeros_like(dv_ref)

        qc, doc = qc_ref[0, 0], doc_ref[0, 0]
        lsev, delta = lse_ref[0, 0], delta_ref[0, 0]
        acc_ref[...] = jnp.zeros_like(acc_ref)
        nt = (((1,), (1,)), ((), ()))

        def step(j, carry):
            p = jnp.exp(_scores(kc_ref, j, qc) - lsev)
            dp = _scores(vc_ref, j, doc)
            ds = (p * (dp - delta)).astype(BF16)
            acc_ref[...] += jnp.dot(kc_ref[0, j], ds, preferred_element_type=F32)
            dk_ref[0, j] += lax.dot_general(qc, ds, nt, preferred_element_type=F32)
            dv_ref[0, j] += lax.dot_general(doc, p.astype(BF16), nt, preferred_element_type=F32)
            return carry

        lax.fori_loop(0, n_k, step, 0, unroll=2)
        dq_ref[0, 0] = acc_ref[...]

        if n_ride:
            @pl.when((kv == ATT_KV - 1) & (pl.program_id(1) == nq - 1))
            def _():
                for send, recv in copies:
                    recv.wait_recv()
                    send.wait_send()
                for cp in mine:
                    cp.wait()

    cspec = pl.BlockSpec((1, 1, ATT_DH, R), lambda h, i: (h, i, 0, 0))
    vspec = pl.BlockSpec((1, 1, 1, R), lambda h, i: (h, i, 0, 0))
    kspec = pl.BlockSpec((1, n_k, ATT_DH, tk), lambda h, i: (h, 0, 0, 0))
    ride = [] if ride is None else list(ride)
    scratch = [pltpu.VMEM((ATT_DH, R), F32)]
    if n_ride:
        n_sem = len(CHIP_MASKS) * n_ride
        scratch += [pltpu.SemaphoreType.DMA((n_sem,)), pltpu.SemaphoreType.DMA((n_sem,)),
                    pltpu.SemaphoreType.DMA((n_ride,))]
    k_shape = jax.ShapeDtypeStruct(k_c.shape, F32)
    return pl.pallas_call(
        body, name=name, grid=(ATT_KV, nq),
        in_specs=[cspec, kspec, kspec, cspec, vspec, vspec] + [ANY] * n_ride,
        out_specs=[cspec, kspec, kspec] + [ANY] * n_ride,
        out_shape=[jax.ShapeDtypeStruct((ATT_KV, nq, ATT_DH, R), F32), k_shape, k_shape]
                  + [jax.ShapeDtypeStruct(g.shape, g.dtype) for g in ride],
        scratch_shapes=scratch,
        compiler_params=pltpu.CompilerParams(dimension_semantics=("arbitrary", "arbitrary"),
                                             vmem_limit_bytes=VMEM_LIMIT, has_side_effects=bool(n_ride)),
    )(q_c, k_c, v_c, do_c, lse, delta, *ride)


def _att_post_fwd2(o_c, w, *, name):
    _, nq, _, R = o_c.shape
    tm = R // ATT_G
    T = nq * tm

    def body(oc_ref, w_ref, o_ref, out_ref):
        ov = jnp.concatenate([_tokens_from_cols(oc_ref[kv, 0]) for kv in range(ATT_KV)], axis=1)
        r = lax.rsqrt(jnp.mean(ov * ov, axis=-1, keepdims=True) + EPS)
        o_ref[...] = ov
        out_ref[...] = (ov * r * w_ref[...]).astype(BF16)

    tok = pl.BlockSpec((tm, ATT_QW), lambda i: (i, 0))
    return pl.pallas_call(
        body, name=name, grid=(nq,),
        in_specs=[pl.BlockSpec((ATT_KV, 1, ATT_DH, R), lambda i: (0, i, 0, 0)), pl.BlockSpec((1, ATT_QW), lambda i: (0, 0))],
        out_specs=[tok, tok],
        out_shape=[jax.ShapeDtypeStruct((T, ATT_QW), F32), jax.ShapeDtypeStruct((T, ATT_QW), BF16)],
        compiler_params=_params(("parallel",)),
    )(o_c, w)


def _att_post_bwd2(dmix, o, w, *, name):
    T = o.shape[0]
    tm = min(FA_TQ, T)
    R = ATT_G * tm

    def body(dm_ref, o_ref, w_ref, do_ref, delta_ref, dw_ref):
        @pl.when(pl.program_id(0) == 0)
        def _():
            dw_ref[...] = jnp.zeros_like(dw_ref)

        ov = o_ref[...]
        r = lax.rsqrt(jnp.mean(ov * ov, axis=-1, keepdims=True) + EPS)
        xh = ov * r
        dm = dm_ref[...]
        dxh = dm * w_ref[...]
        t = jnp.mean(dxh * xh, axis=-1, keepdims=True)
        do = r * (dxh - xh * t)
        _store_padded_cols(do_ref, do)
        dob = do.astype(BF16).astype(F32)
        for kv in range(ATT_KV):
            delta_ref[kv, 0] = jnp.sum(_cols_from_tokens(dob * ov, kv), axis=0, keepdims=True)
        dw_ref[...] += jnp.sum(dm * xh, axis=0, keepdims=True)

    tok = pl.BlockSpec((tm, ATT_QW), lambda i: (i, 0))
    vec = pl.BlockSpec((1, ATT_QW), lambda i: (0, 0))
    return pl.pallas_call(
        body, name=name, grid=(T // tm,),
        in_specs=[pl.BlockSpec((tm, ATT_QW), lambda i: (i, 1)), tok, vec],
        out_specs=[pl.BlockSpec((ATT_KV, 1, ATT_DH, R), lambda i: (0, i, 0, 0)),
                   pl.BlockSpec((ATT_KV, 1, 1, R), lambda i: (0, i, 0, 0)), vec],
        out_shape=[jax.ShapeDtypeStruct((ATT_KV, T // tm, ATT_DH, R), BF16),
                   jax.ShapeDtypeStruct((ATT_KV, T // tm, 1, R), F32), jax.ShapeDtypeStruct((1, ATT_QW), F32)],
        compiler_params=_params(("arbitrary",)),
    )(dmix, o, w)


def _ffn_up(h2, wg, wu, *, name, tm=512):
    T = h2.shape[0]
    tn = _pick(D_FF, 1408)

    def body(h_ref, wg_ref, wu_ref, g_ref, u_ref, a_ref):
        hv = h_ref[...]
        g = jnp.dot(hv, wg_ref[...], preferred_element_type=F32)
        u = jnp.dot(hv, wu_ref[...], preferred_element_type=F32)
        g_ref[...] = g.astype(BF16)
        u_ref[...] = u.astype(BF16)
        a_ref[...] = (g * _sigmoid(g) * u).astype(BF16)

    wspec = pl.BlockSpec((D_MODEL, tn), lambda i, j: (0, j))
    ospec = pl.BlockSpec((tm, tn), lambda i, j: (i, j))
    return pl.pallas_call(
        body, name=name, grid=(T // tm, D_FF // tn),
        in_specs=[pl.BlockSpec((tm, D_MODEL), lambda i, j: (i, 0)), wspec, wspec],
        out_specs=[ospec] * 3, out_shape=[jax.ShapeDtypeStruct((T, D_FF), BF16)] * 3,
        compiler_params=_params(("parallel", "arbitrary")),
    )(h2, wg, wu)


def _ffn_act_bwd(dx2b, w_down, gate, up, *, name, tm=512):
    T = dx2b.shape[0]
    tn = _pick(D_FF, 1408)

    def body(dx_ref, w_ref, g_ref, u_ref, dg_ref, du_ref):
        da = lax.dot_general(dx_ref[...], w_ref[...], (((1,), (1,)), ((), ())), preferred_element_type=F32)
        g = g_ref[...].astype(F32)
        u = u_ref[...].astype(F32)
        sg = _sigmoid(g)
        dg_ref[...] = (da * u * (sg * (1.0 + g * (1.0 - sg)))).astype(BF16)
        du_ref[...] = (da * (g * sg)).astype(BF16)

    ospec = pl.BlockSpec((tm, tn), lambda i, j: (i, j))
    return pl.pallas_call(
        body, name=name, grid=(T // tm, D_FF // tn),
        in_specs=[pl.BlockSpec((tm, D_MODEL), lambda i, j: (i, 0)),
                  pl.BlockSpec((tn, D_MODEL), lambda i, j: (j, 0)), ospec, ospec],
        out_specs=[ospec] * 2, out_shape=[jax.ShapeDtypeStruct((T, D_FF), BF16)] * 2,
        compiler_params=_params(("parallel", "arbitrary")),
    )(dx2b, w_down, gate, up)


def _assemble_du(U, dq_f, dq_b, dz_f, dz_b, dv_f, dv_b, du_g, da_q, da_k, da_v, *, name, tm=256):
    T = U.shape[0]

    def body(uq_ref, dqf, dqb, dzf, dzb, dvf, dvb, dug, daq, dak, dav, out_ref):
        uq = uq_ref[...]
        sg = _sigmoid(uq)
        out_ref[:, 0:HG_W] = ((dqf[...] + dqb[...]) * (sg * (1.0 + uq * (1.0 - sg)))).astype(BF16)
        out_ref[:, HG_W:2 * HG_W] = dzf[...].astype(BF16)
        out_ref[:, 2 * HG_W:3 * HG_W] = dzb[...].astype(BF16)
        out_ref[:, 3 * HG_W:4 * HG_W] = (dvf[...] + dvb[...]).astype(BF16)
        out_ref[:, 4 * HG_W:5 * HG_W] = dug[...].astype(BF16)
        out_ref[:, 5 * HG_W:5 * HG_W + ATT_QW] = daq[...].astype(BF16)
        out_ref[:, 5 * HG_W + ATT_QW:5 * HG_W + ATT_QW + ATT_KW] = dak[...].astype(BF16)
        out_ref[:, 5 * HG_W + ATT_QW + ATT_KW:D_IN] = dav[...].astype(BF16)

    tok = pl.BlockSpec((tm, HG_W), lambda i: (i, 0))
    kv = pl.BlockSpec((tm, ATT_KW), lambda i: (i, 0))
    return pl.pallas_call(
        body, name=name, grid=(T // tm,),
        in_specs=[tok] * 9 + [kv, kv],
        out_specs=pl.BlockSpec((tm, D_IN), lambda i: (i, 0)),
        out_shape=jax.ShapeDtypeStruct((T, D_IN), BF16),
        compiler_params=_params(("parallel",)),
    )(U, dq_f, dq_b, dz_f, dz_b, dv_f, dv_b, du_g, da_q, da_k, da_v)


def _adam_math(w, g, m, v):
    m = ADAM_B1 * m + (1.0 - ADAM_B1) * g
    v = ADAM_B2 * v + (1.0 - ADAM_B2) * (g * g)
    m_hat = m / (1.0 - ADAM_B1 ** ADAM_STEP)
    v_hat = v / (1.0 - ADAM_B2 ** ADAM_STEP)
    delta = -ADAM_LR * (m_hat / (jnp.sqrt(v_hat) + ADAM_EPS) + ADAM_WD * w)
    return delta, m, v


def _adamw(parts, w, m, v, *, name, tr_cap=256):
    P, R, C = parts.shape
    tr = R
    for t in range(8, min(R, tr_cap) + 1, 8):
        if R % t == 0:
            tr = t

    def body(p_ref, w_ref, m_ref, v_ref, g_ref, d_ref, nm_ref, nv_ref):
        g = p_ref[0].astype(F32)
        for j in range(1, P):
            g = g + p_ref[j].astype(F32)
        d, nm, nv = _adam_math(w_ref[...], g, m_ref[...], v_ref[...])
        g_ref[...] = g
        d_ref[...] = d
        nm_ref[...] = nm
        nv_ref[...] = nv

    blk = pl.BlockSpec((tr, C), lambda i: (i, 0))
    return pl.pallas_call(
        body, name=name, grid=(R // tr,),
        in_specs=[pl.BlockSpec((P, tr, C), lambda i: (0, i, 0)), blk, blk, blk],
        out_specs=[blk] * 4, out_shape=[jax.ShapeDtypeStruct((R, C), F32)] * 4,
        compiler_params=_params(("parallel",)),
    )(parts, w, m, v)


def _all_gather(xs, *, name):
    n = len(xs)

    def body(*refs):
        ins, outs = refs[:n], refs[n:2 * n]
        send_sems, recv_sems, local_sems = refs[2 * n:]
        x, y, c = lax.axis_index("x"), lax.axis_index("y"), lax.axis_index("c")
        me, sibling = (x, y, c), (x, y, 1 - c)
        chips = [(1 - x, y), (x, 1 - y), (1 - x, 1 - y)]

        def slot(p):
            return 4 * p[0] + 2 * p[1] + p[2]

        def copy(a, k, block, to, src=None):
            dst = outs[a].at[slot(block)]
            return pltpu.make_async_remote_copy(
                src_ref=dst if src is None else src, dst_ref=dst,
                send_sem=send_sems.at[a * 7 + k], recv_sem=recv_sems.at[a * 7 + k],
                device_id=to, device_id_type=MESH)

        mine = [pltpu.make_async_copy(ins[a], outs[a].at[slot(me)], local_sems.at[a]) for a in range(n)]
        for cp in mine:
            cp.start()
        first = []
        for a in range(n):
            first.append(copy(a, 0, me, sibling, src=ins[a]))
            first += [copy(a, 1 + j, me, (*chip, c), src=ins[a]) for j, chip in enumerate(chips)]
        for cp in first:
            cp.start()
        passed = []
        for j, chip in enumerate(chips):
            for a in range(n):
                copy(a, 1 + j, (*chip, c), me).wait_recv()
                cp = copy(a, 4 + j, (*chip, c), sibling)
                cp.start()
                passed.append(cp)
        for a in range(n):
            copy(a, 0, sibling, me).wait_recv()
            for j, chip in enumerate(chips):
                copy(a, 4 + j, (*chip, 1 - c), me).wait_recv()
        for cp in first + passed:
            cp.wait_send()
        for cp in mine:
            cp.wait()

    return pl.pallas_call(
        body, name=name,
        in_specs=[ANY] * n, out_specs=[ANY] * n,
        out_shape=[jax.ShapeDtypeStruct((N_DEV,) + x.shape, x.dtype) for x in xs],
        scratch_shapes=[pltpu.SemaphoreType.DMA((7 * n,)), pltpu.SemaphoreType.DMA((7 * n,)),
                        pltpu.SemaphoreType.DMA((n,))],
        compiler_params=pltpu.CompilerParams(has_side_effects=True),
    )(*xs)


ALL_MASKS = [(mx, my, mc) for mx in (0, 1) for my in (0, 1) for mc in (0, 1)][1:]


def _flip(v, bit):
    return 1 - v if bit else v


def _exchange_copies(ins, outs, send_sems, recv_sems, local_sems, *, masks, slot):
    n, n_peers = len(ins), len(masks)
    x, y, c = lax.axis_index("x"), lax.axis_index("y"), lax.axis_index("c")
    my_slot = slot((x, y, c))
    mine = [pltpu.make_async_copy(ins[a].at[my_slot], outs[a].at[my_slot], local_sems.at[a]) for a in range(n)]
    copies = []
    for a in range(n):
        for k, (mx, my, mc) in enumerate(masks):
            peer = (_flip(x, mx), _flip(y, my), _flip(c, mc))
            peer_slot = slot(peer)
            sems = dict(send_sem=send_sems.at[a * n_peers + k], recv_sem=recv_sems.at[a * n_peers + k],
                        device_id=peer, device_id_type=MESH)
            copies.append((
                pltpu.make_async_remote_copy(src_ref=ins[a].at[peer_slot], dst_ref=outs[a].at[my_slot], **sems),
                pltpu.make_async_remote_copy(src_ref=ins[a].at[peer_slot], dst_ref=outs[a].at[peer_slot], **sems)))
    return mine, copies


def _exchange(gs, *, masks, slot, name, bcast=None):
    n, n_peers = len(gs), len(masks)
    has_bcast = bcast is not None

    def body(*refs):
        n_in = n + has_bcast
        ins, outs = refs[:n], refs[n_in:n_in + n]
        send_sems, recv_sems, local_sems = refs[2 * n_in:2 * n_in + 3]
        x, y, c = lax.axis_index("x"), lax.axis_index("y"), lax.axis_index("c")
        mine, copies = _exchange_copies(ins, outs, send_sems, recv_sems, local_sems, masks=masks, slot=slot)
        if has_bcast:
            b_in, b_out = refs[n], refs[2 * n_in - 1]
            b_send, b_recv = refs[2 * n_in + 3:]
            me = 4 * x + 2 * y + c
            mine.append(pltpu.make_async_copy(b_in, b_out.at[me], local_sems.at[n]))
            for k, (mx, my, mc) in enumerate(ALL_MASKS):
                peer = (_flip(x, mx), _flip(y, my), _flip(c, mc))
                peer_id = 4 * peer[0] + 2 * peer[1] + peer[2]
                sems = dict(send_sem=b_send.at[k], recv_sem=b_recv.at[k], device_id=peer, device_id_type=MESH)
                copies.append((pltpu.make_async_remote_copy(src_ref=b_in, dst_ref=b_out.at[me], **sems),
                               pltpu.make_async_remote_copy(src_ref=b_in, dst_ref=b_out.at[peer_id], **sems)))
        for cp in mine:
            cp.start()
        for send, _ in copies:
            send.start()
        for send, recv in copies:
            recv.wait_recv()
            send.wait_send()
        for cp in mine:
            cp.wait()

    n_io = n + has_bcast
    out_shape = [jax.ShapeDtypeStruct(g.shape, g.dtype) for g in gs]
    scratch = [pltpu.SemaphoreType.DMA((n_peers * n,)), pltpu.SemaphoreType.DMA((n_peers * n,)),
               pltpu.SemaphoreType.DMA((n_io,))]
    if has_bcast:
        out_shape.append(jax.ShapeDtypeStruct((N_DEV,) + bcast.shape, bcast.dtype))
        scratch += [pltpu.SemaphoreType.DMA((len(ALL_MASKS),)), pltpu.SemaphoreType.DMA((len(ALL_MASKS),))]
    return pl.pallas_call(
        body, name=name,
        in_specs=[ANY] * n_io, out_specs=[ANY] * n_io, out_shape=out_shape, scratch_shapes=scratch,
        compiler_params=pltpu.CompilerParams(has_side_effects=True),
    )(*gs, *([bcast] if has_bcast else []))


SWAP_ROW_CHUNKS = 4


def _core_swap(gs, *, name):
    n = len(gs)

    def body(*refs):
        ins, outs = refs[:n], refs[n:2 * n]
        send_sems, recv_sems = refs[2 * n:]
        x, y, c = lax.axis_index("x"), lax.axis_index("y"), lax.axis_index("c")
        sibling = (x, y, 1 - c)
        started = []
        for a in range(n):
            _, Q, R, _ = ins[a].shape
            rows = R // SWAP_ROW_CHUNKS
            for q in range(Q):
                for j in range(SWAP_ROW_CHUNKS):
                    cp = pltpu.make_async_remote_copy(
                        src_ref=ins[a].at[1 - c, q, pl.ds(j * rows, rows)], dst_ref=outs[a].at[q, pl.ds(j * rows, rows)],
                        send_sem=send_sems.at[a], recv_sem=recv_sems.at[a], device_id=sibling, device_id_type=MESH)
                    cp.start()
                    started.append(cp)
        for a in range(n):
            pltpu.make_async_remote_copy(
                src_ref=ins[a].at[1 - c], dst_ref=outs[a], send_sem=send_sems.at[a], recv_sem=recv_sems.at[a],
                device_id=sibling, device_id_type=MESH).wait()

    return pl.pallas_call(
        body, name=name,
        in_specs=[ANY] * n, out_specs=[ANY] * n,
        out_shape=[jax.ShapeDtypeStruct(g.shape[1:], g.dtype) for g in gs],
        scratch_shapes=[pltpu.SemaphoreType.DMA((n,)), pltpu.SemaphoreType.DMA((n,))],
        compiler_params=pltpu.CompilerParams(has_side_effects=True),
    )(*gs)


def _pair_sum(g, other, core, *, name, tr_cap=256):
    _, Q, R, C = g.shape
    tr = max(t for t in range(16, min(R, tr_cap) + 1, 16) if R % t == 0)

    def body(core_ref, g_ref, o_ref, out_ref):
        out_ref[0] = (g_ref[0, 0] + o_ref[0]).astype(BF16)

    return pl.pallas_call(
        body, name=name,
        grid_spec=pltpu.PrefetchScalarGridSpec(
            num_scalar_prefetch=1, grid=(Q, R // tr),
            in_specs=[pl.BlockSpec((1, 1, tr, C), lambda q, i, core_ref: (core_ref[0], q, i, 0)),
                      pl.BlockSpec((1, tr, C), lambda q, i, core_ref: (q, i, 0))],
            out_specs=pl.BlockSpec((1, tr, C), lambda q, i, core_ref: (q, i, 0))),
        out_shape=jax.ShapeDtypeStruct((Q, R, C), BF16),
        compiler_params=_params(("parallel", "parallel")),
    )(core, g, other)


PACK_ROWS = 8


def _pack_small(norm1, norm2, final, att, hg, qn, kn, lb=None, loss=None):
    z = lambda n: jnp.zeros((n,), F32)
    rows = [norm1.reshape(-1), norm2.reshape(-1), final.reshape(-1),
            jnp.concatenate([att.reshape(-1), z(512)]),
            jnp.concatenate([hg.reshape(-1), qn.reshape(-1), kn.reshape(-1), z(1024 - 256)]),
            z(1024) if lb is None else lb.reshape(-1),
            z(1024) if loss is None else jnp.concatenate([loss.reshape(-1), z(1023)]), z(1024)]
    return jnp.stack(rows, axis=0)


def _unpack_small(p):
    return (p[0:1, :], p[1:2, :], p[2, :], p[3:4, 0:512], p[4:5, 0:128], p[4:5, 128:192], p[4:5, 192:256])


def _fold_heads(dhg, dqn, dkn, *, name):
    def body(hg_ref, q_ref, k_ref, ohg_ref, oq_ref, ok_ref):
        def fold128(v):
            acc = v[:, 0:LANES]
            for j in range(1, v.shape[1] // LANES):
                acc = acc + v[:, j * LANES:(j + 1) * LANES]
            return acc

        ohg_ref[...] = fold128(hg_ref[...])
        q = fold128(q_ref[...])
        oq_ref[...] = q + pltpu.roll(q, ATT_DH, 1)
        k = k_ref[...]
        ok_ref[...] = k + pltpu.roll(k, ATT_DH, 1)

    return pl.pallas_call(body, name=name, out_shape=[jax.ShapeDtypeStruct((1, LANES), F32)] * 3)(dhg, dqn, dkn)


def _lb_grad(dlb_sum, lb, *, name):
    def body(d_ref, lb_ref, o_ref):
        lbv = lb_ref[...]
        gl = d_ref[...] * lbv * (1.0 - lbv)
        o_ref[0:1, :] = gl[0:1, :]
        o_ref[1:2, :] = -gl[0:1, :]
        o_ref[2:3, :] = gl[1:2, :]
        o_ref[3:4, :] = -gl[1:2, :]

    return pl.pallas_call(body, name=name, out_shape=jax.ShapeDtypeStruct((4, HG_W), F32))(dlb_sum, lb)


def _lower_bounds(lb_logits_full, *, name):
    def body(l_ref, o_ref):
        for d in range(2):
            l0, l1 = l_ref[2 * d:2 * d + 1, :], l_ref[2 * d + 1:2 * d + 2, :]
            mx = jnp.maximum(l0, l1)
            e0, e1 = jnp.exp(l0 - mx), jnp.exp(l1 - mx)
            o_ref[d:d + 1, :] = e0 / (e0 + e1)

    return pl.pallas_call(body, name=name, out_shape=jax.ShapeDtypeStruct((2, HG_W), F32))(
        lb_logits_full.reshape(4, HG_W))


def _local_step(x, target, norm1_w, w_in, lb, hg_norm_w, q_norm_w, k_norm_w, att_norm_w, w_out, norm2_w,
                w_g, w_u, w_down, final_norm_w, reduce_early=None):
    T = x.shape[0]
    cos, sin = _rope_tables(T)
    qw8 = jnp.tile(q_norm_w, (1, ATT_HEADS))
    kw2 = jnp.tile(k_norm_w, (1, ATT_KV))

    h, r1 = _rms_fwd(x, norm1_w, name="norm1_fwd")
    U = _mm_nn([(h, w_in)], name="in_proj")
    o_f, st_f = _gla_fwd(U, lb[0:1], f_block=1, reverse=False, name="gla_fwd_f")
    o_b, st_b = _gla_fwd(U, lb[1:2], f_block=2, reverse=True, name="gla_fwd_b")
    mix_hg = _hg_post_fwd(o_f, o_b, U, hg_norm_w, name="hg_post_fwd")
    q_c, qn_c, kmax2, k_c, v_c = _att_prep_fwd2(U, cos, sin, qw8, kw2, name="att_prep_fwd")
    kmax = jnp.sqrt(jnp.max(kmax2.reshape(ATT_KV, ATT_DH), axis=1))
    m_c = qn_c * (kmax * 1.001).reshape(ATT_KV, 1, 1, 1)
    o_c, lse = lax.cond(jnp.max(m_c) <= FA_BOUND_MAX,
                        lambda: _flash_fwd_bounded(q_c, k_c, v_c, m_c, name="flash_fwd_bounded"),
                        lambda: _flash_fwd2(q_c, k_c, v_c, name="flash_fwd"))
    o_att, mix_att = _att_post_fwd2(o_c, att_norm_w, name="att_post_fwd")
    x1 = _mm_nn([(mix_hg, w_out[:HG_W]), (mix_att, w_out[HG_W:])], residual=x, name="out_proj")
    h2, r2 = _rms_fwd(x1, norm2_w, name="norm2_fwd")
    gate, up, act = _ffn_up(h2, w_g, w_u, name="ffn_up")
    x2 = _mm_nn([(act, w_down)], residual=x1, name="ffn_down")
    loss, dx2, dx2b, d_final = _loss_head(x2, target, final_norm_w.reshape(1, D_MODEL), name="loss_head")

    d_gate, d_up = _ffn_act_bwd(dx2b, w_down, gate, up, name="ffn_act_bwd")
    dw_down = _mm_tn(act, dx2b, tma_cap=1408, name="dw_down")
    dh2 = _mm_nn([(d_gate, w_g), (d_up, w_u)], trans_b=True, tm=256, name="ffn_up_bwd")
    dw_g = _mm_tn(h2, d_gate, tnb_cap=1408, name="dw_gate")
    dw_u = _mm_tn(h2, d_up, tnb_cap=1408, name="dw_up")
    dx1, dx1b, d_norm2 = _rms_bwd(dh2, x1, r2, norm2_w, dx2, emit_bf16=True, name="norm2_bwd")
    dmix = _mm_nn([(dx1b, w_out)], trans_b=True, name="out_proj_bwd")
    dw_out = jnp.concatenate([_mm_tn(mix_hg, dx1b, name="dw_out_hg"), _mm_tn(mix_att, dx1b, name="dw_out_att")], axis=0)
    do_c, delta, d_att = _att_post_bwd2(dmix, o_att, att_norm_w, name="att_post_bwd")
    ride = None if reduce_early is None else reduce_early(dw_out, dw_g, dw_u, dw_down)
    dq_c, dk_c, dv_c, *rode = _flash_bwd2(q_c, k_c, v_c, do_c, lse, delta, ride=ride, name="flash_bwd")
    dU_att, d_qn, d_kn = _att_prep_bwd2(U, dq_c, dk_c, dv_c, cos, sin, qw8, kw2, name="att_prep_bwd")
    do_hg, du_g, d_hg = _hg_post_bwd(dmix, o_f, o_b, U, hg_norm_w, name="hg_post_bwd")
    dq_f, dz_f, dv_f, dlb_f = _gla_bwd(U, lb[0:1], do_hg, st_f, f_block=1, reverse=False, name="gla_bwd_f")
    dU_hg, dlb_b = _gla_bwd(U, lb[1:2], do_hg, st_b, f_block=2, reverse=True, prev=(dq_f, dz_f, dv_f, du_g),
                            name="gla_bwd_b")
    w_hg = 5 * HG_W
    dh = _mm_nn([(dU_hg, w_in[:, :w_hg]), (dU_att, w_in[:, w_hg:])], trans_b=True, name="in_proj_bwd")
    dw_in = jnp.concatenate([_mm_tn(h, dU_hg, tnb_cap=1280, name="dw_in_hg"), _mm_tn(h, dU_att, name="dw_in_att")],
                            axis=1)
    grad_x, d_norm1 = _rms_bwd(dh, x, r1, norm1_w, dx1, emit_bf16=False, name="norm1_bwd")
    d_hg, d_qn, d_kn = _fold_heads(d_hg, d_qn, d_kn, name="fold_heads")

    big = dict(w_in=dw_in, w_out=dw_out, w_g=dw_g, w_u=dw_u, w_down=dw_down)
    small = dict(norm1=d_norm1, norm2=d_norm2, final=d_final, att=d_att, hg=d_hg,
                 qn=d_qn[:, :ATT_DH], kn=d_kn[:, :ATT_DH], lb=jnp.concatenate([dlb_f, dlb_b], axis=0))
    return loss, grad_x, big, small, rode


def kernel(x, norm1_w, w_in, lb_logits, hg_norm_w, q_norm_w, k_norm_w, att_norm_w, w_out, norm2_w, w_gate_up, w_down, final_norm_w, loss_target, m_norm1_w, m_w_in, m_lb_logits, m_hg_norm_w, m_q_norm_w, m_k_norm_w, m_att_norm_w, m_w_out, m_norm2_w, m_w_gate_up, m_w_down, m_final_norm_w, v_norm1_w, v_w_in, v_lb_logits, v_hg_norm_w, v_q_norm_w, v_k_norm_w, v_att_norm_w, v_w_out, v_norm2_w, v_w_gate_up, v_w_down, v_final_norm_w):
    T = x.shape[1]
    me = 4 * lax.axis_index("x") + 2 * lax.axis_index("y") + lax.axis_index("c")
    c_in, r_out, c_gu, r_dn = w_in.shape[2], w_out.shape[1], w_gate_up.shape[2], w_down.shape[1]
    lb_cols = lb_logits.shape[2]

    g_in, g_out, g_gu, g_dn, g_lb = _all_gather(
        [w_in[0].astype(BF16), w_out[0].astype(BF16), w_gate_up[0].astype(BF16), w_down[0].astype(BF16),
         lb_logits.reshape(4, lb_cols)], name="gather_weights")
    w_in_f = g_in.transpose(1, 0, 2).reshape(D_MODEL, N_DEV * c_in)
    w_out_f = g_out.reshape(N_DEV * r_out, D_MODEL)
    half = N_DEV // 2
    w_g_f = g_gu[:half].transpose(1, 0, 2).reshape(D_MODEL, half * c_gu)
    w_u_f = g_gu[half:].transpose(1, 0, 2).reshape(D_MODEL, half * c_gu)
    w_dn_f = g_dn.reshape(N_DEV * r_dn, D_MODEL)
    lb_logits_f = g_lb.transpose(1, 0, 2).reshape(2, 2, N_DEV * lb_cols)
    lb = _lower_bounds(lb_logits_f, name="lower_bounds")

    chips = N_DEV // 2
    core = lax.axis_index("c").astype(jnp.int32).reshape(1)
    by_owner_cols = lambda g, n_q, w: g.reshape(D_MODEL, n_q, 2, w).transpose(2, 1, 0, 3)
    by_owner_rows = lambda g, r: g.reshape(chips, 2, r, D_MODEL).transpose(1, 0, 2, 3)

    def chip_sums(mine, names, call):
        theirs = _core_swap(mine, name=call)
        return [_pair_sum(g, o, core, name="pair_sum_" + nm) for g, o, nm in zip(mine, theirs, names)]

    def reduce_early(dw_out, dw_g, dw_u, dw_down):
        s_gu = jnp.concatenate([by_owner_cols(dw_g, chips // 2, c_gu), by_owner_cols(dw_u, chips // 2, c_gu)], axis=1)
        return chip_sums([by_owner_rows(dw_out, r_out), s_gu, by_owner_rows(dw_down, r_dn)],
                         ("w_out", "w_gu", "w_down"), "exchange_cores_early")

    loss, grad_x, big, small, (p_out, p_gu, p_dn) = _local_step(
        x[0], loss_target[0], norm1_w, w_in_f, lb, hg_norm_w, q_norm_w, k_norm_w, att_norm_w, w_out_f, norm2_w,
        w_g_f, w_u_f, w_dn_f, final_norm_w, reduce_early=reduce_early)

    packed = _pack_small(small["norm1"], small["norm2"], small["final"], small["att"], small["hg"],
                         small["qn"], small["kn"], small["lb"], loss)
    p_in, all_small = _exchange(chip_sums([by_owner_cols(big["w_in"], chips, c_in)], ("w_in",), "exchange_cores"),
                                masks=CHIP_MASKS, slot=_chip_slot, bcast=packed, name="exchange_chips")

    g_w_in, d_w_in, nm_w_in, nv_w_in = _adamw(p_in, w_in[0], m_w_in[0], v_w_in[0], name="adamw_w_in")
    g_w_out, d_w_out, nm_w_out, nv_w_out = _adamw(p_out, w_out[0], m_w_out[0], v_w_out[0], name="adamw_w_out")
    g_w_gu, d_w_gu, nm_w_gu, nv_w_gu = _adamw(p_gu, w_gate_up[0], m_w_gate_up[0], v_w_gate_up[0], name="adamw_w_gu")
    g_w_dn, d_w_dn, nm_w_dn, nv_w_dn = _adamw(p_dn, w_down[0], m_w_down[0], v_w_down[0], name="adamw_w_down")

    pk = lambda vecs: _pack_small(*vecs)
    w_pk = pk([norm1_w, norm2_w, final_norm_w, att_norm_w, hg_norm_w, q_norm_w, k_norm_w])
    m_pk = pk([m_norm1_w, m_norm2_w, m_final_norm_w, m_att_norm_w, m_hg_norm_w, m_q_norm_w, m_k_norm_w])
    v_pk = pk([v_norm1_w, v_norm2_w, v_final_norm_w, v_att_norm_w, v_hg_norm_w, v_q_norm_w, v_k_norm_w])
    g_pk, d_pk, nm_pk, nv_pk = _adamw(all_small, w_pk, m_pk, v_pk, name="adamw_small")

    dlb_sum = g_pk[5:6, :].reshape(2, HG_W)
    g_lb_full = _lb_grad(dlb_sum, lb, name="lb_grad")
    g_lb_mine = lax.dynamic_slice_in_dim(g_lb_full, me * lb_cols, lb_cols, axis=1)
    g_lb_s, d_lb, nm_lb, nv_lb = _adamw(g_lb_mine[None], lb_logits.reshape(4, lb_cols),
                                        m_lb_logits.reshape(4, lb_cols), v_lb_logits.reshape(4, lb_cols),
                                        name="adamw_lb")

    loss_total = g_pk[6, 0]

    def outs(big4, lb_arr, pk_arr):
        n1, n2, fin, att, hg, qn, kn = _unpack_small(pk_arr)
        b_in, b_out, b_gu, b_dn = big4
        return [n1, b_in[None], lb_arr.reshape(2, 2, lb_cols), hg, qn, kn, att, b_out[None], n2, b_gu[None],
                b_dn[None], fin]

    return (loss_total, grad_x[None],
            *outs((g_w_in, g_w_out, g_w_gu, g_w_dn), g_lb_s, g_pk),
            *outs((d_w_in, d_w_out, d_w_gu, d_w_dn), d_lb, d_pk),
            *outs((nm_w_in, nm_w_out, nm_w_gu, nm_w_dn), nm_lb, nm_pk),
            *outs((nv_w_in, nv_w_out, nv_w_gu, nv_w_dn), nv_lb, nv_pk))
```

```python
import math

import jax
import jax.numpy as jnp
import numpy as np
from jax import lax
from jax.experimental import pallas as pl
from jax.experimental.pallas import tpu as pltpu

F32 = jnp.float32
BF16 = jnp.bfloat16

N_DEV = 8
D_MODEL = 1024
EPS = 1e-6
HG_HEADS = 4
HG_D = 128
HG_W = HG_HEADS * HG_D
CHUNK = 64
ATT_HEADS = 8
ATT_KV = 2
ATT_G = ATT_HEADS // ATT_KV
ATT_DH = 64
ATT_QW = ATT_HEADS * ATT_DH
ATT_KW = ATT_KV * ATT_DH
GRID_W = 64
ROPE_THETA = 10000.0
D_FF = 2816
ADAM_LR, ADAM_B1, ADAM_B2, ADAM_EPS, ADAM_WD, ADAM_STEP = 0.001, 0.9, 0.999, 1e-08, 0.01, 10

LANES = 128
VMEM_LIMIT = 48 * 1024 * 1024
MESH = pl.DeviceIdType.MESH
ANY = pl.BlockSpec(memory_space=pl.ANY)


def _params(sem=None):
    return pltpu.CompilerParams(dimension_semantics=sem, vmem_limit_bytes=VMEM_LIMIT)


def _pick(n, cap):
    best = None
    for t in range(LANES, cap + 1, LANES):
        if n % t == 0:
            best = t
    assert best is not None, (n, cap)
    return best


def _sigmoid(x):
    return 1.0 / (1.0 + jnp.exp(-x))


def _dot(a, b):
    return jnp.dot(a.astype(BF16), b.astype(BF16), preferred_element_type=F32)


def _dot_nt(a, b):
    return lax.dot_general(a.astype(BF16), b.astype(BF16), (((1,), (1,)), ((), ())),
                           preferred_element_type=F32)


def _dot_tn(a, b):
    return lax.dot_general(a.astype(BF16), b.astype(BF16), (((0,), (0,)), ((), ())),
                           preferred_element_type=F32)


def _mm_nn(pairs, *, name, out_dtype=F32, residual=None, tm=512, tn_cap=None, trans_b=False):
    M = pairs[0][0].shape[0]
    N = pairs[0][1].shape[0 if trans_b else 1]
    tn = N if tn_cap is None else _pick(N, tn_cap)
    n_pairs = len(pairs)
    has_res = residual is not None
    dims = (((1,), (1,)), ((), ())) if trans_b else (((1,), (0,)), ((), ()))

    def body(*refs):
        acc = None
        for i in range(n_pairs):
            d = lax.dot_general(refs[2 * i][...], refs[2 * i + 1][...], dims, preferred_element_type=F32)
            acc = d if acc is None else acc + d
        if has_res:
            acc = acc + refs[2 * n_pairs][...]
        refs[-1][...] = acc.astype(out_dtype)

    in_specs, args = [], []
    for a, b in pairs:
        k = a.shape[1]
        b_spec = pl.BlockSpec((tn, k), lambda i, j: (j, 0)) if trans_b else pl.BlockSpec((k, tn), lambda i, j: (0, j))
        in_specs += [pl.BlockSpec((tm, k), lambda i, j: (i, 0)), b_spec]
        args += [a, b]
    if has_res:
        in_specs.append(pl.BlockSpec((tm, tn), lambda i, j: (i, j)))
        args.append(residual)
    return pl.pallas_call(
        body, name=name, grid=(M // tm, N // tn), in_specs=in_specs,
        out_specs=pl.BlockSpec((tm, tn), lambda i, j: (i, j)),
        out_shape=jax.ShapeDtypeStruct((M, N), out_dtype),
        compiler_params=_params(("parallel", "arbitrary")),
    )(*args)


def _mm_tn(a, b, *, name, tma_cap=1024, tnb_cap=1024, tk=1024):
    T, Ma = a.shape
    Nb = b.shape[1]
    tma, tnb = _pick(Ma, tma_cap), _pick(Nb, tnb_cap)
    tk = min(tk, T)
    n_k = T // tk

    def body(a_ref, b_ref, o_ref, acc_ref):
        k = pl.program_id(2)

        @pl.when(k == 0)
        def _():
            acc_ref[...] = jnp.zeros_like(acc_ref)

        acc_ref[...] += lax.dot_general(a_ref[...], b_ref[...], (((0,), (0,)), ((), ())),
                                        preferred_element_type=F32)

        @pl.when(k == n_k - 1)
        def _():
            o_ref[...] = acc_ref[...]

    return pl.pallas_call(
        body, name=name, grid=(Ma // tma, Nb // tnb, n_k),
        in_specs=[pl.BlockSpec((tk, tma), lambda i, j, k: (k, i)), pl.BlockSpec((tk, tnb), lambda i, j, k: (k, j))],
        out_specs=pl.BlockSpec((tma, tnb), lambda i, j, k: (i, j)),
        out_shape=jax.ShapeDtypeStruct((Ma, Nb), F32),
        scratch_shapes=[pltpu.VMEM((tma, tnb), F32)],
        compiler_params=_params(("parallel", "parallel", "arbitrary")),
    )(a, b)


def _rms_fwd(x, w, *, name, tm=512):
    T, Dm = x.shape

    def body(x_ref, w_ref, h_ref, r_ref):
        xv = x_ref[...]
        r = lax.rsqrt(jnp.mean(xv * xv, axis=-1, keepdims=True) + EPS)
        h_ref[...] = (xv * r * w_ref[...]).astype(BF16)
        r_ref[...] = r

    return pl.pallas_call(
        body, name=name, grid=(T // tm,),
        in_specs=[pl.BlockSpec((tm, Dm), lambda i: (i, 0)), pl.BlockSpec((1, Dm), lambda i: (0, 0))],
        out_specs=[pl.BlockSpec((tm, Dm), lambda i: (i, 0)), pl.BlockSpec((tm, 1), lambda i: (i, 0))],
        out_shape=[jax.ShapeDtypeStruct((T, Dm), BF16), jax.ShapeDtypeStruct((T, 1), F32)],
        compiler_params=_params(("parallel",)),
    )(x, w)


def _rms_bwd(dh, x, r, w, dres, *, name, emit_bf16, tm=512):
    T, Dm = x.shape

    def body(dh_ref, x_ref, r_ref, w_ref, dres_ref, *outs):
        dx_ref, dw_ref = outs[0], outs[-1]

        @pl.when(pl.program_id(0) == 0)
        def _():
            dw_ref[...] = jnp.zeros_like(dw_ref)

        rv = r_ref[...]
        xh = x_ref[...] * rv
        dhv = dh_ref[...]
        dxh = dhv * w_ref[...]
        t = jnp.mean(dxh * xh, axis=-1, keepdims=True)
        dx = dres_ref[...] + rv * (dxh - xh * t)
        dx_ref[...] = dx
        if emit_bf16:
            outs[1][...] = dx.astype(BF16)
        dw_ref[...] += jnp.sum(dhv * xh, axis=0, keepdims=True)

    row = pl.BlockSpec((tm, Dm), lambda i: (i, 0))
    vec = pl.BlockSpec((1, Dm), lambda i: (0, 0))
    out_specs = [row] + ([row] if emit_bf16 else []) + [vec]
    out_shape = ([jax.ShapeDtypeStruct((T, Dm), F32)] + ([jax.ShapeDtypeStruct((T, Dm), BF16)] if emit_bf16 else [])
                 + [jax.ShapeDtypeStruct((1, Dm), F32)])
    return pl.pallas_call(
        body, name=name, grid=(T // tm,),
        in_specs=[row, row, pl.BlockSpec((tm, 1), lambda i: (i, 0)), vec, row],
        out_specs=out_specs, out_shape=out_shape,
        compiler_params=_params(("arbitrary",)),
    )(dh, x, r, w, dres)


def _loss_head(x2, target, w, *, name, tm=512):
    T, Dm = x2.shape

    def body(x_ref, t_ref, w_ref, loss_ref, dx_ref, dxb_ref, dw_ref):
        @pl.when(pl.program_id(0) == 0)
        def _():
            loss_ref[...] = jnp.zeros_like(loss_ref)
            dw_ref[...] = jnp.zeros_like(dw_ref)

        xv = x_ref[...]
        r = lax.rsqrt(jnp.mean(xv * xv, axis=-1, keepdims=True) + EPS)
        xh = xv * r
        wv = w_ref[...]
        err = xh * wv - t_ref[...]
        row_loss = jnp.mean(err * err, axis=-1, keepdims=True)
        loss_ref[...] += 0.5 * jnp.sum(row_loss, axis=0, keepdims=True)
        dy = err * (1.0 / Dm)
        dxh = dy * wv
        t = jnp.mean(dxh * xh, axis=-1, keepdims=True)
        dx = r * (dxh - xh * t)
        dx_ref[...] = dx
        dxb_ref[...] = dx.astype(BF16)
        dw_ref[...] += jnp.sum(dy * xh, axis=0, keepdims=True)

    row = pl.BlockSpec((tm, Dm), lambda i: (i, 0))
    vec = pl.BlockSpec((1, Dm), lambda i: (0, 0))
    return pl.pallas_call(
        body, name=name, grid=(T // tm,),
        in_specs=[row, row, vec],
        out_specs=[pl.BlockSpec((1, 1), lambda i: (0, 0)), row, row, vec],
        out_shape=[jax.ShapeDtypeStruct((1, 1), F32), jax.ShapeDtypeStruct((T, Dm), F32),
                   jax.ShapeDtypeStruct((T, Dm), BF16), jax.ShapeDtypeStruct((1, Dm), F32)],
        compiler_params=_params(("arbitrary",)),
    )(x2, target, w)


GLA_TB = 512
GLA_NC = GLA_TB // CHUNK
GLA_UNROLL = 4


def _cumsum_rows(x, row, reverse):
    n = x.shape[0]
    s = 1
    while s < n:
        if not reverse:
            x = x + jnp.where(row >= s, pltpu.roll(x, s, 0), 0.0)
        else:
            x = x + jnp.where(row < n - s, pltpu.roll(x, n - s, 0), 0.0)
        s *= 2
    return x


def _gla_gates(uq, z, lbv):
    q = uq * _sigmoid(uq)
    sg = _sigmoid(z)
    sgn = _sigmoid(-z)
    f = lbv + (1.0 - lbv) * sg
    k = (1.0 - lbv) * sgn
    return q, sg, sgn, f, k


def _gla_decays(f, row, reverse):
    b = _cumsum_rows(jnp.log(f), row, reverse)
    if not reverse:
        bref, blast = b[CHUNK // 2 - 1:CHUNK // 2, :], b[CHUNK - 1:CHUNK, :]
    else:
        bref, blast = b[CHUNK // 2:CHUNK // 2 + 1, :], b[0:1, :]
    return b, bref, blast


def _gla_fwd(U, lb, *, f_block, reverse, name):
    T = U.shape[0]
    nb = T // GLA_TB

    def body(uq_ref, uf_ref, ui_ref, lb_ref, o_ref, st_ref, s_ref):
        @pl.when(pl.program_id(0) == 0)
        def _():
            s_ref[...] = jnp.zeros_like(s_ref)

        row = lax.broadcasted_iota(jnp.int32, (CHUNK, HG_D), 0)
        ri = lax.broadcasted_iota(jnp.int32, (CHUNK, CHUNK), 0)
        ci = lax.broadcasted_iota(jnp.int32, (CHUNK, CHUNK), 1)
        mask = (ri <= ci) if reverse else (ri >= ci)

        def chunk(j, carry):
            c = (GLA_NC - 1 - j) if reverse else j
            rows = pl.ds(pl.multiple_of(c * CHUNK, CHUNK), CHUNK)
            for h in range(HG_HEADS):
                cols = pl.ds(h * HG_D, HG_D)
                v = ui_ref[rows, cols]
                q, _, _, f, k = _gla_gates(uq_ref[rows, cols], uf_ref[rows, cols], lb_ref[:, cols])
                b, bref, blast = _gla_decays(f, row, reverse)
                s = jnp.where(mask, _dot_nt(q * jnp.exp(b - bref), k * jnp.exp(bref - b)), 0.0)
                st = s_ref[h]
                st_ref[c, h] = st
                o_ref[rows, cols] = _dot(s, v) + _dot_nt(q * jnp.exp(b), st)
                s_ref[h] = st * jnp.exp(blast) + _dot_tn(v, k * jnp.exp(blast - b))
            return carry

        lax.fori_loop(0, GLA_NC, chunk, 0, unroll=GLA_NC)

    blk = (lambda i: nb - 1 - i) if reverse else (lambda i: i)
    ucol = lambda cb: pl.BlockSpec((GLA_TB, HG_W), lambda i: (blk(i), cb))
    return pl.pallas_call(
        body, name=name, grid=(nb,),
        in_specs=[ucol(0), ucol(f_block), ucol(3), pl.BlockSpec((1, HG_W), lambda i: (0, 0))],
        out_specs=[pl.BlockSpec((GLA_TB, HG_W), lambda i: (blk(i), 0)),
                   pl.BlockSpec((GLA_NC, HG_HEADS, HG_D, HG_D), lambda i: (blk(i), 0, 0, 0))],
        out_shape=[jax.ShapeDtypeStruct((T, HG_W), F32),
                   jax.ShapeDtypeStruct((T // CHUNK, HG_HEADS, HG_D, HG_D), F32)],
        scratch_shapes=[pltpu.VMEM((HG_HEADS, HG_D, HG_D), F32)],
        compiler_params=_params(("arbitrary",)),
    )(U, U, U, lb)


def _gla_bwd(U, lb, do, states, *, f_block, reverse, name, prev=None):
    T = U.shape[0]
    nb = T // GLA_TB
    final = prev is not None

    def body(uq_ref, uf_ref, ui_ref, lb_ref, do_ref, st_ref, *rest):
        if final:
            dqp_ref, dzp_ref, dvp_ref, dug_ref, out_ref, dlb_ref, ds_ref = rest
        else:
            dq_ref, dz_ref, dv_ref, dlb_ref, ds_ref = rest

        @pl.when(pl.program_id(0) == 0)
        def _():
            ds_ref[...] = jnp.zeros_like(ds_ref)
            dlb_ref[...] = jnp.zeros_like(dlb_ref)

        row = lax.broadcasted_iota(jnp.int32, (CHUNK, HG_D), 0)
        ri = lax.broadcasted_iota(jnp.int32, (CHUNK, CHUNK), 0)
        ci = lax.broadcasted_iota(jnp.int32, (CHUNK, CHUNK), 1)
        mask = (ri <= ci) if reverse else (ri >= ci)

        def chunk(j, carry):
            c = j if reverse else (GLA_NC - 1 - j)
            rows = pl.ds(pl.multiple_of(c * CHUNK, CHUNK), CHUNK)
            for h in range(HG_HEADS):
                cols = pl.ds(h * HG_D, HG_D)
                v = ui_ref[rows, cols]
                lbv = lb_ref[:, cols]
                uq = uq_ref[rows, cols]
                q, sg, sgn, f, k = _gla_gates(uq, uf_ref[rows, cols], lbv)
                b, bref, blast = _gla_decays(f, row, reverse)
                eq, ek, eb, el, dec = (jnp.exp(b - bref), jnp.exp(bref - b), jnp.exp(b), jnp.exp(blast - b),
                                       jnp.exp(blast))
                qin, kin, qb, klast = q * eq, k * ek, q * eb, k * el
                dov = do_ref[rows, cols]
                st = st_ref[c, h]
                dst = ds_ref[h]
                p = jnp.where(mask, _dot_nt(qin, kin), 0.0)
                dp = jnp.where(mask, _dot_nt(dov, v), 0.0)
                dqin = _dot(dp, kin)
                dkin = _dot_tn(dp, qin)
                dv = _dot_tn(p, dov) + _dot_nt(klast, dst)
                dqb = _dot(dov, st)
                dklast = _dot(v, dst)
                ds_ref[h] = _dot_tn(dov, qb) + dst * dec
                db = dqin * qin - dkin * kin + dqb * qb - dklast * klast
                extra = (jnp.sum(dklast * klast, axis=0, keepdims=True)
                         + dec * jnp.sum(st * dst, axis=0, keepdims=True))
                dg = _cumsum_rows(db, row, not reverse) + extra
                dq = dqin * eq + dqb * eb
                dk = dkin * ek + dklast * el
                dfk = dg / f - dk
                dz = (dfk * (1.0 - lbv) * sg * sgn).astype(BF16)
                dlb_ref[:, cols] += jnp.sum(dfk * sgn, axis=0, keepdims=True)
                if final:
                    sq = _sigmoid(uq)
                    col = lambda blk: pl.ds(blk * HG_W + h * HG_D, HG_D)
                    out_ref[rows, col(0)] = ((dq + dqp_ref[rows, cols]) * (sq * (1.0 + uq * (1.0 - sq)))).astype(BF16)
                    out_ref[rows, col(1)] = dzp_ref[rows, cols]
                    out_ref[rows, col(2)] = dz
                    out_ref[rows, col(3)] = (dv + dvp_ref[rows, cols]).astype(BF16)
                    out_ref[rows, col(4)] = dug_ref[rows, cols]
                else:
                    dq_ref[rows, cols] = dq
                    dz_ref[rows, cols] = dz
                    dv_ref[rows, cols] = dv
            return carry

        lax.fori_loop(0, GLA_NC, chunk, 0, unroll=GLA_UNROLL)

    blk = (lambda i: i) if reverse else (lambda i: nb - 1 - i)
    ucol = lambda cb: pl.BlockSpec((GLA_TB, HG_W), lambda i: (blk(i), cb))
    tok = pl.BlockSpec((GLA_TB, HG_W), lambda i: (blk(i), 0))
    vec = pl.BlockSpec((1, HG_W), lambda i: (0, 0))
    in_specs = [ucol(0), ucol(f_block), ucol(3), vec, tok,
                pl.BlockSpec((GLA_NC, HG_HEADS, HG_D, HG_D), lambda i: (blk(i), 0, 0, 0))]
    vec_shape = jax.ShapeDtypeStruct((1, HG_W), F32)
    if final:
        in_specs += [tok] * 4
        out_specs = [pl.BlockSpec((GLA_TB, 5 * HG_W), lambda i: (blk(i), 0)), vec]
        out_shape = [jax.ShapeDtypeStruct((T, 5 * HG_W), BF16), vec_shape]
    else:
        out_specs = [tok, tok, tok, vec]
        out_shape = [jax.ShapeDtypeStruct((T, HG_W), F32), jax.ShapeDtypeStruct((T, HG_W), BF16),
                     jax.ShapeDtypeStruct((T, HG_W), F32), vec_shape]
    return pl.pallas_call(
        body, name=name, grid=(nb,), in_specs=in_specs, out_specs=out_specs, out_shape=out_shape,
        scratch_shapes=[pltpu.VMEM((HG_HEADS, HG_D, HG_D), F32)],
        compiler_params=_params(("arbitrary",)),
    )(U, U, U, lb, do, states, *(prev if final else ()))


def _hg_post_fwd(o_f, o_b, U, w, *, name, tm=512):
    T = o_f.shape[0]

    def body(of_ref, ob_ref, ug_ref, w_ref, out_ref):
        wv = w_ref[...]
        for h in range(HG_HEADS):
            cols = pl.ds(h * HG_D, HG_D)
            o = of_ref[:, cols] + ob_ref[:, cols]
            r = lax.rsqrt(jnp.mean(o * o, axis=-1, keepdims=True) + EPS)
            ug = ug_ref[:, cols]
            out_ref[:, cols] = (o * r * wv * (ug * _sigmoid(ug))).astype(BF16)

    tok = pl.BlockSpec((tm, HG_W), lambda i: (i, 0))
    return pl.pallas_call(
        body, name=name, grid=(T // tm,),
        in_specs=[tok, tok, pl.BlockSpec((tm, HG_W), lambda i: (i, 4)), pl.BlockSpec((1, HG_D), lambda i: (0, 0))],
        out_specs=tok, out_shape=jax.ShapeDtypeStruct((T, HG_W), BF16),
        compiler_params=_params(("parallel",)),
    )(o_f, o_b, U, w)


def _hg_post_bwd(dmix, o_f, o_b, U, w, *, name, tm=512):
    T = o_f.shape[0]

    def body(dm_ref, of_ref, ob_ref, ug_ref, w_ref, do_ref, dug_ref, dw_ref):
        @pl.when(pl.program_id(0) == 0)
        def _():
            dw_ref[...] = jnp.zeros_like(dw_ref)

        wv = w_ref[...]
        for h in range(HG_HEADS):
            cols = pl.ds(h * HG_D, HG_D)
            o = of_ref[:, cols] + ob_ref[:, cols]
            r = lax.rsqrt(jnp.mean(o * o, axis=-1, keepdims=True) + EPS)
            xh = o * r
            ug = ug_ref[:, cols]
            sg = _sigmoid(ug)
            dm = dm_ref[:, cols]
            dn = dm * (ug * sg)
            dug_ref[:, cols] = (dm * (xh * wv) * (sg * (1.0 + ug * (1.0 - sg)))).astype(BF16)
            dxh = dn * wv
            t = jnp.mean(dxh * xh, axis=-1, keepdims=True)
            do_ref[:, cols] = r * (dxh - xh * t)
            dw_ref[:, cols] += jnp.sum(dn * xh, axis=0, keepdims=True)

    tok = pl.BlockSpec((tm, HG_W), lambda i: (i, 0))
    vec = pl.BlockSpec((1, HG_W), lambda i: (0, 0))
    return pl.pallas_call(
        body, name=name, grid=(T // tm,),
        in_specs=[tok, tok, tok, pl.BlockSpec((tm, HG_W), lambda i: (i, 4)), pl.BlockSpec((1, HG_D), lambda i: (0, 0))],
        out_specs=[tok, tok, vec],
        out_shape=[jax.ShapeDtypeStruct((T, HG_W), F32), jax.ShapeDtypeStruct((T, HG_W), BF16),
                   jax.ShapeDtypeStruct((1, HG_W), F32)],
        compiler_params=_params(("arbitrary",)),
    )(dmix, o_f, o_b, U, w)


def _rope_tables(T):
    rows = T // GRID_W
    row = np.repeat(np.arange(rows), GRID_W).astype(np.float32)
    col = np.tile(np.arange(GRID_W), rows).astype(np.float32)
    axis_dim = ATT_DH // 2
    freqs = (np.float32(ROPE_THETA) ** (-np.arange(0, axis_dim, 2, dtype=np.float32) / np.float32(axis_dim))
             ).astype(np.float32)
    ang = np.concatenate([row[:, None] * freqs, col[:, None] * freqs], axis=-1).astype(np.float32)
    cos, sin = np.cos(ang), np.sin(ang)
    c = np.repeat(cos, 2, axis=-1)
    s = np.stack([-sin, sin], axis=-1).reshape(T, ATT_DH)
    return jnp.asarray(np.tile(c, (1, 2)), F32), jnp.asarray(np.tile(s, (1, 2)), F32)


def _head_blockdiag(width):
    shift = ATT_DH.bit_length() - 1
    ri = jnp.right_shift(lax.broadcasted_iota(jnp.int32, (width, width), 0), shift)
    ci = jnp.right_shift(lax.broadcasted_iota(jnp.int32, (width, width), 1), shift)
    return jnp.where(ri == ci, 1.0, 0.0).astype(BF16)


def _head_sum(x, bd):
    hi = x.astype(BF16)
    lo = (x - hi.astype(F32)).astype(BF16)
    return jnp.dot(hi, bd, preferred_element_type=F32) + jnp.dot(lo, bd, preferred_element_type=F32)


def _pair_swap(x, even):
    n = x.shape[-1]
    return jnp.where(even, pltpu.roll(x, n - 1, 1), pltpu.roll(x, 1, 1))


FA_TQ = 256


FA_TK = 512


def _cols_from_tokens(x, kv):
    w = ATT_G * ATT_DH
    xt = x[:, kv * w:(kv + 1) * w].T
    return jnp.concatenate([xt[g * ATT_DH:(g + 1) * ATT_DH, :] for g in range(ATT_G)], axis=1)


def _tokens_from_cols(c):
    tq = c.shape[1] // ATT_G
    return jnp.concatenate([c[:, g * tq:(g + 1) * tq] for g in range(ATT_G)], axis=0).T


def _store_cols(ref, x, norm_ref=None):
    for kv in range(ATT_KV):
        cols = _cols_from_tokens(x, kv).astype(BF16)
        ref[kv, 0] = cols
        if norm_ref is not None:
            cf = cols.astype(F32)
            norm_ref[kv, 0] = jnp.sqrt(jnp.sum(cf * cf, axis=0, keepdims=True))


def _att_prep_fwd(U, cos, sin, qw, kw, *, name):
    T = U.shape[0]
    tm = min(FA_TQ, T)
    R = ATT_G * tm
    scale = ATT_DH ** -0.5

    def head_rows(ref, x):
        xt = x.astype(F32).T
        for kv in range(ATT_KV):
            ref[kv, 0] = xt[kv * ATT_DH:(kv + 1) * ATT_DH, :].astype(BF16)

    def body(aq_ref, ak_ref, av_ref, c_ref, s_ref, qw_ref, kw_ref, q_ref, qn_ref, kmax_ref, kc_ref, vc_ref):
        @pl.when(pl.program_id(0) == 0)
        def _():
            kmax_ref[...] = jnp.zeros_like(kmax_ref)

        bd = _head_blockdiag(ATT_QW)
        c2, s2 = c_ref[...], s_ref[...]
        c8, s8 = jnp.tile(c2, (1, 4)), jnp.tile(s2, (1, 4))

        def norm_rope(x, w, c, s, bdm):
            r = lax.rsqrt(_head_sum(x * x, bdm) * (1.0 / ATT_DH) + EPS)
            y = x * r * w
            even = (lax.broadcasted_iota(jnp.int32, y.shape, 1) & 1) == 0
            return y * c + _pair_swap(y, even) * s

        _store_cols(q_ref, norm_rope(aq_ref[...], qw_ref[...], c8, s8, bd) * scale, qn_ref)
        kb = norm_rope(ak_ref[...], kw_ref[...], c2, s2, bd[:ATT_KW, :ATT_KW]).astype(BF16)
        kf = kb.astype(F32)
        ksq = _head_sum(kf * kf, bd[:ATT_KW, :ATT_KW])
        kmax_ref[...] = jnp.maximum(kmax_ref[...], jnp.max(ksq, axis=0, keepdims=True))
        head_rows(kc_ref, kb)
        head_rows(vc_ref, av_ref[...].astype(BF16))

    kv_spec = pl.BlockSpec((tm, ATT_KW), lambda i: (i, 0))
    tk = min(FA_TK, T)
    per = tk // tm
    c_spec = pl.BlockSpec((ATT_KV, 1, ATT_DH, tm), lambda i: (0, i // per, 0, i % per))
    c_shape = jax.ShapeDtypeStruct((ATT_KV, T // tk, ATT_DH, tk), BF16)
    return pl.pallas_call(
        body, name=name, grid=(T // tm,),
        in_specs=[pl.BlockSpec((tm, ATT_QW), lambda i: (i, 5)),
                  pl.BlockSpec((tm, ATT_KW), lambda i: (i, 24)), pl.BlockSpec((tm, ATT_KW), lambda i: (i, 25)),
                  kv_spec, kv_spec,
                  pl.BlockSpec((1, ATT_QW), lambda i: (0, 0)), pl.BlockSpec((1, ATT_KW), lambda i: (0, 0))],
        out_specs=[pl.BlockSpec((ATT_KV, 1, ATT_DH, R), lambda i: (0, i, 0, 0)),
                   pl.BlockSpec((ATT_KV, 1, 1, R), lambda i: (0, i, 0, 0)), pl.BlockSpec((1, ATT_KW), lambda i: (0, 0)),
                   c_spec, c_spec],
        out_shape=[jax.ShapeDtypeStruct((ATT_KV, T // tm, ATT_DH, R), BF16),
                   jax.ShapeDtypeStruct((ATT_KV, T // tm, 1, R), F32), jax.ShapeDtypeStruct((1, ATT_KW), F32),
                   c_shape, c_shape],
        compiler_params=_params(("arbitrary",)),
    )(U, U, U, cos, sin, qw, kw)


def _att_prep_bwd(U, dq_c, dk_c, dv_c, cos, sin, qw, kw, *, name):
    T = U.shape[0]
    tm = min(FA_TQ, T)
    R = ATT_G * tm
    scale = ATT_DH ** -0.5

    def body(aq_ref, ak_ref, dq_ref, dk_ref, dv_ref, c_ref, s_ref, qw_ref, kw_ref, out_ref, dqw_ref, dkw_ref):
        @pl.when(pl.program_id(0) == 0)
        def _():
            dqw_ref[...] = jnp.zeros_like(dqw_ref)
            dkw_ref[...] = jnp.zeros_like(dkw_ref)

        bd = _head_blockdiag(ATT_QW)
        c2, s2 = c_ref[...], s_ref[...]
        c8, s8 = jnp.tile(c2, (1, 4)), jnp.tile(s2, (1, 4))

        def bwd(x, dy, w, c, s, bdm):
            even = (lax.broadcasted_iota(jnp.int32, x.shape, 1) & 1) == 0
            dn = dy * c - _pair_swap(dy, even) * s
            r = lax.rsqrt(_head_sum(x * x, bdm) * (1.0 / ATT_DH) + EPS)
            xh = x * r
            dxh = dn * w
            t = _head_sum(dxh * xh, bdm) * (1.0 / ATT_DH)
            return r * (dxh - xh * t), jnp.sum(dn * xh, axis=0, keepdims=True)

        dq = jnp.concatenate([_tokens_from_cols(dq_ref[kv, 0]) for kv in range(ATT_KV)], axis=1)
        da, dw = bwd(aq_ref[...], dq * scale, qw_ref[...], c8, s8, bd)
        out_ref[:, 0:ATT_QW] = da.astype(BF16)
        dqw_ref[...] += dw
        tokens = lambda ref: jnp.concatenate([ref[kv, 0] for kv in range(ATT_KV)], axis=0).T
        da, dw = bwd(ak_ref[...], tokens(dk_ref), kw_ref[...], c2, s2, bd[:ATT_KW, :ATT_KW])
        out_ref[:, ATT_QW:ATT_QW + ATT_KW] = da.astype(BF16)
        dkw_ref[...] += dw
        out_ref[:, ATT_QW + ATT_KW:ATT_QW + 2 * ATT_KW] = tokens(dv_ref).astype(BF16)

    kv_spec = pl.BlockSpec((tm, ATT_KW), lambda i: (i, 0))
    qv = pl.BlockSpec((1, ATT_QW), lambda i: (0, 0))
    kv = pl.BlockSpec((1, ATT_KW), lambda i: (0, 0))
    w_att = ATT_QW + 2 * ATT_KW
    per = dk_c.shape[3] // tm
    c_spec = pl.BlockSpec((ATT_KV, 1, ATT_DH, tm), lambda i: (0, i // per, 0, i % per))
    return pl.pallas_call(
        body, name=name, grid=(T // tm,),
        in_specs=[pl.BlockSpec((tm, ATT_QW), lambda i: (i, 5)), pl.BlockSpec((tm, ATT_KW), lambda i: (i, 24)),
                  pl.BlockSpec((ATT_KV, 1, ATT_DH, R), lambda i: (0, i, 0, 0)), c_spec, c_spec, kv_spec, kv_spec, qv, kv],
        out_specs=[pl.BlockSpec((tm, w_att), lambda i: (i, 0)), qv, kv],
        out_shape=[jax.ShapeDtypeStruct((T, w_att), BF16),
                   jax.ShapeDtypeStruct((1, ATT_QW), F32), jax.ShapeDtypeStruct((1, ATT_KW), F32)],
        compiler_params=_params(("arbitrary",)),
    )(U, U, dq_c, dk_c, dv_c, cos, sin, qw, kw)


def _scores(k_ref, j, qv):
    return lax.dot_general(k_ref[0, j], qv, (((0,), (0,)), ((), ())), preferred_element_type=F32)


def _flash_fwd(q_c, k_c, v_c, *, name):
    _, nq, _, R = q_c.shape
    _, n_k, _, tk = v_c.shape

    def body(q_ref, k_ref, v_ref, o_ref, lse_ref, acc_ref):
        qv = q_ref[0, 0]
        acc_ref[...] = jnp.zeros_like(acc_ref)

        def step(j, carry):
            m, l = carry
            s = _scores(k_ref, j, qv)
            m_new = jnp.maximum(m, jnp.max(s, axis=0, keepdims=True))
            alpha = jnp.exp(m - m_new)
            p = jnp.exp(s - m_new)
            l = alpha * l + jnp.sum(p, axis=0, keepdims=True)
            acc_ref[...] = alpha * acc_ref[...] + jnp.dot(v_ref[0, j], p.astype(BF16), preferred_element_type=F32)
            return m_new, l

        m, l = lax.fori_loop(0, n_k, step, (jnp.full((1, R), -jnp.inf, F32), jnp.zeros((1, R), F32)))
        o_ref[0, 0] = acc_ref[...] / l
        lse_ref[0, 0] = m + jnp.log(l)

    cspec = pl.BlockSpec((1, 1, ATT_DH, R), lambda h, i: (h, i, 0, 0))
    kspec = pl.BlockSpec((1, n_k, ATT_DH, tk), lambda h, i: (h, 0, 0, 0))
    return pl.pallas_call(
        body, name=name, grid=(ATT_KV, nq),
        in_specs=[cspec, kspec, kspec],
        out_specs=[cspec, pl.BlockSpec((1, 1, 1, R), lambda h, i: (h, i, 0, 0))],
        out_shape=[jax.ShapeDtypeStruct((ATT_KV, nq, ATT_DH, R), F32), jax.ShapeDtypeStruct((ATT_KV, nq, 1, R), F32)],
        scratch_shapes=[pltpu.VMEM((ATT_DH, R), F32)],
        compiler_params=_params(("parallel", "parallel")),
    )(q_c, k_c, v_c)


FA_BOUND_MAX = 40.0


def _flash_fwd_bounded(q_c, k_c, v_c, m_c, *, name):
    _, nq, _, R = q_c.shape
    _, n_k, _, tk = v_c.shape

    def body(q_ref, k_ref, v_ref, m_ref, o_ref, lse_ref, acc_ref):
        qv = q_ref[0, 0]
        m = m_ref[0, 0]
        acc_ref[...] = jnp.zeros_like(acc_ref)

        per = math.gcd(n_k, 4)

        def step(jj, l8):
            pv = None
            for u in range(per):
                j = per * jj + u
                p = jnp.exp(_scores(k_ref, j, qv) - m)
                l8 = l8 + jnp.sum(p.reshape(tk // 8, 8, R), axis=0)
                d = jnp.dot(v_ref[0, j], p.astype(BF16), preferred_element_type=F32)
                pv = d if pv is None else pv + d
            acc_ref[...] += pv
            return l8

        l8 = lax.fori_loop(0, n_k // per, step, jnp.zeros((8, R), F32))
        l = jnp.sum(l8, axis=0, keepdims=True)
        o_ref[0, 0] = acc_ref[...] / l
        lse_ref[0, 0] = m + jnp.log(l)

    cspec = pl.BlockSpec((1, 1, ATT_DH, R), lambda h, i: (h, i, 0, 0))
    kspec = pl.BlockSpec((1, n_k, ATT_DH, tk), lambda h, i: (h, 0, 0, 0))
    vspec = pl.BlockSpec((1, 1, 1, R), lambda h, i: (h, i, 0, 0))
    return pl.pallas_call(
        body, name=name, grid=(ATT_KV, nq),
        in_specs=[cspec, kspec, kspec, vspec],
        out_specs=[cspec, vspec],
        out_shape=[jax.ShapeDtypeStruct((ATT_KV, nq, ATT_DH, R), F32), jax.ShapeDtypeStruct((ATT_KV, nq, 1, R), F32)],
        scratch_shapes=[pltpu.VMEM((ATT_DH, R), F32)],
        compiler_params=_params(("parallel", "parallel")),
    )(q_c, k_c, v_c, m_c)


CHIP_MASKS = [(1, 0, 0), (0, 1, 0), (1, 1, 0)]


def _chip_slot(p):
    return 2 * p[0] + p[1]


def _flash_bwd(q_c, k_c, v_c, do_c, lse, delta, *, name, ride=None):
    _, nq, _, R = q_c.shape
    _, n_k, _, tk = k_c.shape
    n_ride = 0 if ride is None else len(ride)

    def body(qc_ref, kc_ref, vc_ref, doc_ref, lse_ref, delta_ref, *rest):
        ride_in, rest = rest[:n_ride], rest[n_ride:]
        dq_ref, dk_ref, dv_ref = rest[:3]
        ride_out, rest = rest[3:3 + n_ride], rest[3 + n_ride:]
        acc_ref = rest[0]
        kv = pl.program_id(0)
        first = (kv == 0) & (pl.program_id(1) == 0)

        if n_ride:
            mine, copies = _exchange_copies(ride_in, ride_out, *rest[1:], masks=CHIP_MASKS, slot=_chip_slot)

            @pl.when(first)
            def _():
                for cp in mine:
                    cp.start()
                for send, _ in copies:
                    send.start()

        @pl.when(pl.program_id(1) == 0)
        def _():
            dk_ref[...] = jnp.zeros_like(dk_ref)
            dv_ref[...] = jnp.zeros_like(dv_ref)

        qc, doc = qc_ref[0, 0], doc_ref[0, 0]
        lsev, delta = lse_ref[0, 0], delta_ref[0, 0]
        acc_ref[...] = jnp.zeros_like(acc_ref)
        nt = (((1,), (1,)), ((), ()))

        def step(j, carry):
            p = jnp.exp(_scores(kc_ref, j, qc) - lsev)
            dp = _scores(vc_ref, j, doc)
            ds = (p * (dp - delta)).astype(BF16)
            acc_ref[...] += jnp.dot(kc_ref[0, j], ds, preferred_element_type=F32)
            dk_ref[0, j] += lax.dot_general(qc, ds, nt, preferred_element_type=F32)
            dv_ref[0, j] += lax.dot_general(doc, p.astype(BF16), nt, preferred_element_type=F32)
            return carry

        lax.fori_loop(0, n_k, step, 0, unroll=2)
        dq_ref[0, 0] = acc_ref[...]

        if n_ride:
            @pl.when((kv == ATT_KV - 1) & (pl.program_id(1) == nq - 1))
            def _():
                for send, recv in copies:
                    recv.wait_recv()
                    send.wait_send()
                for cp in mine:
                    cp.wait()

    cspec = pl.BlockSpec((1, 1, ATT_DH, R), lambda h, i: (h, i, 0, 0))
    vspec = pl.BlockSpec((1, 1, 1, R), lambda h, i: (h, i, 0, 0))
    kspec = pl.BlockSpec((1, n_k, ATT_DH, tk), lambda h, i: (h, 0, 0, 0))
    ride = [] if ride is None else list(ride)
    scratch = [pltpu.VMEM((ATT_DH, R), F32)]
    if n_ride:
        n_sem = len(CHIP_MASKS) * n_ride
        scratch += [pltpu.SemaphoreType.DMA((n_sem,)), pltpu.SemaphoreType.DMA((n_sem,)),
                    pltpu.SemaphoreType.DMA((n_ride,))]
    k_shape = jax.ShapeDtypeStruct(k_c.shape, F32)
    return pl.pallas_call(
        body, name=name, grid=(ATT_KV, nq),
        in_specs=[cspec, kspec, kspec, cspec, vspec, vspec] + [ANY] * n_ride,
        out_specs=[cspec, kspec, kspec] + [ANY] * n_ride,
        out_shape=[jax.ShapeDtypeStruct((ATT_KV, nq, ATT_DH, R), F32), k_shape, k_shape]
                  + [jax.ShapeDtypeStruct(g.shape, g.dtype) for g in ride],
        scratch_shapes=scratch,
        compiler_params=pltpu.CompilerParams(dimension_semantics=("arbitrary", "arbitrary"),
                                             vmem_limit_bytes=VMEM_LIMIT, has_side_effects=bool(n_ride)),
    )(q_c, k_c, v_c, do_c, lse, delta, *ride)


def _att_post_fwd(o_c, w, *, name):
    _, nq, _, R = o_c.shape
    tm = R // ATT_G
    T = nq * tm

    def body(oc_ref, w_ref, o_ref, out_ref):
        ov = jnp.concatenate([_tokens_from_cols(oc_ref[kv, 0]) for kv in range(ATT_KV)], axis=1)
        r = lax.rsqrt(jnp.mean(ov * ov, axis=-1, keepdims=True) + EPS)
        o_ref[...] = ov
        out_ref[...] = (ov * r * w_ref[...]).astype(BF16)

    tok = pl.BlockSpec((tm, ATT_QW), lambda i: (i, 0))
    return pl.pallas_call(
        body, name=name, grid=(nq,),
        in_specs=[pl.BlockSpec((ATT_KV, 1, ATT_DH, R), lambda i: (0, i, 0, 0)), pl.BlockSpec((1, ATT_QW), lambda i: (0, 0))],
        out_specs=[tok, tok],
        out_shape=[jax.ShapeDtypeStruct((T, ATT_QW), F32), jax.ShapeDtypeStruct((T, ATT_QW), BF16)],
        compiler_params=_params(("parallel",)),
    )(o_c, w)


def _att_post_bwd(dmix, o, w, *, name):
    T = o.shape[0]
    tm = min(FA_TQ, T)
    R = ATT_G * tm

    def body(dm_ref, o_ref, w_ref, do_ref, delta_ref, dw_ref):
        @pl.when(pl.program_id(0) == 0)
        def _():
            dw_ref[...] = jnp.zeros_like(dw_ref)

        ov = o_ref[...]
        r = lax.rsqrt(jnp.mean(ov * ov, axis=-1, keepdims=True) + EPS)
        xh = ov * r
        dm = dm_ref[...]
        dxh = dm * w_ref[...]
        t = jnp.mean(dxh * xh, axis=-1, keepdims=True)
        do = r * (dxh - xh * t)
        _store_cols(do_ref, do)
        dob = do.astype(BF16).astype(F32)
        for kv in range(ATT_KV):
            delta_ref[kv, 0] = jnp.sum(_cols_from_tokens(dob * ov, kv), axis=0, keepdims=True)
        dw_ref[...] += jnp.sum(dm * xh, axis=0, keepdims=True)

    tok = pl.BlockSpec((tm, ATT_QW), lambda i: (i, 0))
    vec = pl.BlockSpec((1, ATT_QW), lambda i: (0, 0))
    return pl.pallas_call(
        body, name=name, grid=(T // tm,),
        in_specs=[pl.BlockSpec((tm, ATT_QW), lambda i: (i, 1)), tok, vec],
        out_specs=[pl.BlockSpec((ATT_KV, 1, ATT_DH, R), lambda i: (0, i, 0, 0)),
                   pl.BlockSpec((ATT_KV, 1, 1, R), lambda i: (0, i, 0, 0)), vec],
        out_shape=[jax.ShapeDtypeStruct((ATT_KV, T // tm, ATT_DH, R), BF16),
                   jax.ShapeDtypeStruct((ATT_KV, T // tm, 1, R), F32), jax.ShapeDtypeStruct((1, ATT_QW), F32)],
        compiler_params=_params(("arbitrary",)),
    )(dmix, o, w)


def _ffn_up(h2, wg_t, wu_t, *, name, tm=512):
    T = h2.shape[0]
    tn = _pick(D_FF, 1408)
    nt = (((1,), (1,)), ((), ()))

    def body(h_ref, wg_ref, wu_ref, g_ref, u_ref, a_ref):
        hv = h_ref[...]
        g = lax.dot_general(hv, wg_ref[...], nt, preferred_element_type=F32)
        u = lax.dot_general(hv, wu_ref[...], nt, preferred_element_type=F32)
        g_ref[...] = g.astype(BF16)
        u_ref[...] = u.astype(BF16)
        a_ref[...] = (g * _sigmoid(g) * u).astype(BF16)

    wspec = pl.BlockSpec((tn, D_MODEL), lambda i, j: (j, 0))
    ospec = pl.BlockSpec((tm, tn), lambda i, j: (i, j))
    return pl.pallas_call(
        body, name=name, grid=(T // tm, D_FF // tn),
        in_specs=[pl.BlockSpec((tm, D_MODEL), lambda i, j: (i, 0)), wspec, wspec],
        out_specs=[ospec] * 3, out_shape=[jax.ShapeDtypeStruct((T, D_FF), BF16)] * 3,
        compiler_params=_params(("parallel", "arbitrary")),
    )(h2, wg_t, wu_t)


def _ffn_act_bwd(dx2b, w_down, gate, up, *, name, tm=512):
    T = dx2b.shape[0]
    tn = _pick(D_FF, 1408)

    def body(dx_ref, w_ref, g_ref, u_ref, dg_ref, du_ref):
        da = lax.dot_general(dx_ref[...], w_ref[...], (((1,), (1,)), ((), ())), preferred_element_type=F32)
        g = g_ref[...].astype(F32)
        u = u_ref[...].astype(F32)
        sg = _sigmoid(g)
        dg_ref[...] = (da * u * (sg * (1.0 + g * (1.0 - sg)))).astype(BF16)
        du_ref[...] = (da * (g * sg)).astype(BF16)

    ospec = pl.BlockSpec((tm, tn), lambda i, j: (i, j))
    return pl.pallas_call(
        body, name=name, grid=(T // tm, D_FF // tn),
        in_specs=[pl.BlockSpec((tm, D_MODEL), lambda i, j: (i, 0)),
                  pl.BlockSpec((tn, D_MODEL), lambda i, j: (j, 0)), ospec, ospec],
        out_specs=[ospec] * 2, out_shape=[jax.ShapeDtypeStruct((T, D_FF), BF16)] * 2,
        compiler_params=_params(("parallel", "arbitrary")),
    )(dx2b, w_down, gate, up)


def _adam_math(w, g, m, v):
    m = ADAM_B1 * m + (1.0 - ADAM_B1) * g
    v = ADAM_B2 * v + (1.0 - ADAM_B2) * (g * g)
    m_hat = m / (1.0 - ADAM_B1 ** ADAM_STEP)
    v_hat = v / (1.0 - ADAM_B2 ** ADAM_STEP)
    delta = -ADAM_LR * (m_hat / (jnp.sqrt(v_hat) + ADAM_EPS) + ADAM_WD * w)
    return delta, m, v


def _adamw(parts, w, m, v, *, name, tr_cap=256):
    P, R, C = parts.shape
    tr = R
    for t in range(8, min(R, tr_cap) + 1, 8):
        if R % t == 0:
            tr = t

    def body(p_ref, w_ref, m_ref, v_ref, g_ref, d_ref, nm_ref, nv_ref):
        g = p_ref[0].astype(F32)
        for j in range(1, P):
            g = g + p_ref[j].astype(F32)
        d, nm, nv = _adam_math(w_ref[...], g, m_ref[...], v_ref[...])
        g_ref[...] = g
        d_ref[...] = d
        nm_ref[...] = nm
        nv_ref[...] = nv

    blk = pl.BlockSpec((tr, C), lambda i: (i, 0))
    return pl.pallas_call(
        body, name=name, grid=(R // tr,),
        in_specs=[pl.BlockSpec((P, tr, C), lambda i: (0, i, 0)), blk, blk, blk],
        out_specs=[blk] * 4, out_shape=[jax.ShapeDtypeStruct((R, C), F32)] * 4,
        compiler_params=_params(("parallel",)),
    )(parts, w, m, v)


def _all_gather(xs, *, name):
    n = len(xs)

    def body(*refs):
        ins, outs = refs[:n], refs[n:2 * n]
        send_sems, recv_sems, local_sems = refs[2 * n:]
        x, y, c = lax.axis_index("x"), lax.axis_index("y"), lax.axis_index("c")
        me, sibling = (x, y, c), (x, y, 1 - c)
        chips = [(1 - x, y), (x, 1 - y), (1 - x, 1 - y)]

        def slot(p):
            return 4 * p[0] + 2 * p[1] + p[2]

        def copy(a, k, block, to, src=None):
            dst = outs[a].at[slot(block)]
            return pltpu.make_async_remote_copy(
                src_ref=dst if src is None else src, dst_ref=dst,
                send_sem=send_sems.at[a * 7 + k], recv_sem=recv_sems.at[a * 7 + k],
                device_id=to, device_id_type=MESH)

        mine = [pltpu.make_async_copy(ins[a], outs[a].at[slot(me)], local_sems.at[a]) for a in range(n)]
        for cp in mine:
            cp.start()
        first = []
        for a in range(n):
            first.append(copy(a, 0, me, sibling, src=ins[a]))
            first += [copy(a, 1 + j, me, (*chip, c), src=ins[a]) for j, chip in enumerate(chips)]
        for cp in first:
            cp.start()
        passed = []
        for j, chip in enumerate(chips):
            for a in range(n):
                copy(a, 1 + j, (*chip, c), me).wait_recv()
                cp = copy(a, 4 + j, (*chip, c), sibling)
                cp.start()
                passed.append(cp)
        for a in range(n):
            copy(a, 0, sibling, me).wait_recv()
            for j, chip in enumerate(chips):
                copy(a, 4 + j, (*chip, 1 - c), me).wait_recv()
        for cp in first + passed:
            cp.wait_send()
        for cp in mine:
            cp.wait()

    return pl.pallas_call(
        body, name=name,
        in_specs=[ANY] * n, out_specs=[ANY] * n,
        out_shape=[jax.ShapeDtypeStruct((N_DEV,) + x.shape, x.dtype) for x in xs],
        scratch_shapes=[pltpu.SemaphoreType.DMA((7 * n,)), pltpu.SemaphoreType.DMA((7 * n,)),
                        pltpu.SemaphoreType.DMA((n,))],
        compiler_params=pltpu.CompilerParams(has_side_effects=True),
    )(*xs)


ALL_MASKS = [(mx, my, mc) for mx in (0, 1) for my in (0, 1) for mc in (0, 1)][1:]


def _flip(v, bit):
    return 1 - v if bit else v


def _exchange_copies(ins, outs, send_sems, recv_sems, local_sems, *, masks, slot):
    n, n_peers = len(ins), len(masks)
    x, y, c = lax.axis_index("x"), lax.axis_index("y"), lax.axis_index("c")
    my_slot = slot((x, y, c))
    mine = [pltpu.make_async_copy(ins[a].at[my_slot], outs[a].at[my_slot], local_sems.at[a]) for a in range(n)]
    copies = []
    for a in range(n):
        for k, (mx, my, mc) in enumerate(masks):
            peer = (_flip(x, mx), _flip(y, my), _flip(c, mc))
            peer_slot = slot(peer)
            sems = dict(send_sem=send_sems.at[a * n_peers + k], recv_sem=recv_sems.at[a * n_peers + k],
                        device_id=peer, device_id_type=MESH)
            copies.append((
                pltpu.make_async_remote_copy(src_ref=ins[a].at[peer_slot], dst_ref=outs[a].at[my_slot], **sems),
                pltpu.make_async_remote_copy(src_ref=ins[a].at[peer_slot], dst_ref=outs[a].at[peer_slot], **sems)))
    return mine, copies


def _exchange(gs, *, masks, slot, name, bcast=None):
    n, n_peers = len(gs), len(masks)
    has_bcast = bcast is not None

    def body(*refs):
        n_in = n + has_bcast
        ins, outs = refs[:n], refs[n_in:n_in + n]
        send_sems, recv_sems, local_sems = refs[2 * n_in:2 * n_in + 3]
        x, y, c = lax.axis_index("x"), lax.axis_index("y"), lax.axis_index("c")
        mine, copies = _exchange_copies(ins, outs, send_sems, recv_sems, local_sems, masks=masks, slot=slot)
        if has_bcast:
            b_in, b_out = refs[n], refs[2 * n_in - 1]
            b_send, b_recv = refs[2 * n_in + 3:]
            me = 4 * x + 2 * y + c
            mine.append(pltpu.make_async_copy(b_in, b_out.at[me], local_sems.at[n]))
            for k, (mx, my, mc) in enumerate(ALL_MASKS):
                peer = (_flip(x, mx), _flip(y, my), _flip(c, mc))
                peer_id = 4 * peer[0] + 2 * peer[1] + peer[2]
                sems = dict(send_sem=b_send.at[k], recv_sem=b_recv.at[k], device_id=peer, device_id_type=MESH)
                copies.append((pltpu.make_async_remote_copy(src_ref=b_in, dst_ref=b_out.at[me], **sems),
                               pltpu.make_async_remote_copy(src_ref=b_in, dst_ref=b_out.at[peer_id], **sems)))
        for cp in mine:
            cp.start()
        for send, _ in copies:
            send.start()
        for send, recv in copies:
            recv.wait_recv()
            send.wait_send()
        for cp in mine:
            cp.wait()

    n_io = n + has_bcast
    out_shape = [jax.ShapeDtypeStruct(g.shape, g.dtype) for g in gs]
    scratch = [pltpu.SemaphoreType.DMA((n_peers * n,)), pltpu.SemaphoreType.DMA((n_peers * n,)),
               pltpu.SemaphoreType.DMA((n_io,))]
    if has_bcast:
        out_shape.append(jax.ShapeDtypeStruct((N_DEV,) + bcast.shape, bcast.dtype))
        scratch += [pltpu.SemaphoreType.DMA((len(ALL_MASKS),)), pltpu.SemaphoreType.DMA((len(ALL_MASKS),))]
    return pl.pallas_call(
        body, name=name,
        in_specs=[ANY] * n_io, out_specs=[ANY] * n_io, out_shape=out_shape, scratch_shapes=scratch,
        compiler_params=pltpu.CompilerParams(has_side_effects=True),
    )(*gs, *([bcast] if has_bcast else []))


SWAP_ROW_CHUNKS = 4


def _core_swap(gs, *, name):
    n = len(gs)

    def body(*refs):
        ins, outs = refs[:n], refs[n:2 * n]
        send_sems, recv_sems = refs[2 * n:]
        x, y, c = lax.axis_index("x"), lax.axis_index("y"), lax.axis_index("c")
        sibling = (x, y, 1 - c)
        for a in range(n):
            Q, _, R, _ = ins[a].shape
            rows = R // SWAP_ROW_CHUNKS
            for q in range(Q):
                for j in range(SWAP_ROW_CHUNKS):
                    pltpu.make_async_remote_copy(
                        src_ref=ins[a].at[q, 1 - c, pl.ds(j * rows, rows)], dst_ref=outs[a].at[q, pl.ds(j * rows, rows)],
                        send_sem=send_sems.at[a], recv_sem=recv_sems.at[a], device_id=sibling, device_id_type=MESH
                    ).start()
        for a in range(n):
            pltpu.make_async_remote_copy(
                src_ref=outs[a], dst_ref=outs[a], send_sem=send_sems.at[a], recv_sem=recv_sems.at[a],
                device_id=sibling, device_id_type=MESH).wait()

    return pl.pallas_call(
        body, name=name,
        in_specs=[ANY] * n, out_specs=[ANY] * n,
        out_shape=[jax.ShapeDtypeStruct(g.shape[:1] + g.shape[2:], g.dtype) for g in gs],
        scratch_shapes=[pltpu.SemaphoreType.DMA((n,)), pltpu.SemaphoreType.DMA((n,))],
        compiler_params=pltpu.CompilerParams(has_side_effects=True),
    )(*gs)


def _pair_sum(g, other, core, *, name, tr_cap=256):
    Q, _, R, C = g.shape
    tr = max(t for t in range(16, min(R, tr_cap) + 1, 16) if R % t == 0)

    def body(core_ref, g_ref, o_ref, out_ref):
        out_ref[0] = (g_ref[0, 0] + o_ref[0]).astype(BF16)

    return pl.pallas_call(
        body, name=name,
        grid_spec=pltpu.PrefetchScalarGridSpec(
            num_scalar_prefetch=1, grid=(Q, R // tr),
            in_specs=[pl.BlockSpec((1, 1, tr, C), lambda q, i, core_ref: (q, core_ref[0], i, 0)),
                      pl.BlockSpec((1, tr, C), lambda q, i, core_ref: (q, i, 0))],
            out_specs=pl.BlockSpec((1, tr, C), lambda q, i, core_ref: (q, i, 0))),
        out_shape=jax.ShapeDtypeStruct((Q, R, C), BF16),
        compiler_params=_params(("parallel", "parallel")),
    )(core, g, other)


def _pack_small(norm1, norm2, final, att, hg, qn, kn, lb=None, loss=None):
    z = lambda n: jnp.zeros((n,), F32)
    rows = [norm1.reshape(-1), norm2.reshape(-1), final.reshape(-1),
            jnp.concatenate([att.reshape(-1), z(512)]),
            jnp.concatenate([hg.reshape(-1), qn.reshape(-1), kn.reshape(-1), z(1024 - 256)]),
            z(1024) if lb is None else lb.reshape(-1),
            z(1024) if loss is None else jnp.concatenate([loss.reshape(-1), z(1023)]), z(1024)]
    return jnp.stack(rows, axis=0)


def _unpack_small(p):
    return (p[0:1, :], p[1:2, :], p[2, :], p[3:4, 0:512], p[4:5, 0:128], p[4:5, 128:192], p[4:5, 192:256])


def _fold_heads(dhg, dqn, dkn, *, name):
    def body(hg_ref, q_ref, k_ref, ohg_ref, oq_ref, ok_ref):
        def fold128(v):
            acc = v[:, 0:LANES]
            for j in range(1, v.shape[1] // LANES):
                acc = acc + v[:, j * LANES:(j + 1) * LANES]
            return acc

        ohg_ref[...] = fold128(hg_ref[...])
        q = fold128(q_ref[...])
        oq_ref[...] = q + pltpu.roll(q, ATT_DH, 1)
        k = k_ref[...]
        ok_ref[...] = k + pltpu.roll(k, ATT_DH, 1)

    return pl.pallas_call(body, name=name, out_shape=[jax.ShapeDtypeStruct((1, LANES), F32)] * 3)(dhg, dqn, dkn)


def _lb_grad(dlb_sum, lb, *, name):
    def body(d_ref, lb_ref, o_ref):
        lbv = lb_ref[...]
        gl = d_ref[...] * lbv * (1.0 - lbv)
        o_ref[0:1, :] = gl[0:1, :]
        o_ref[1:2, :] = -gl[0:1, :]
        o_ref[2:3, :] = gl[1:2, :]
        o_ref[3:4, :] = -gl[1:2, :]

    return pl.pallas_call(body, name=name, out_shape=jax.ShapeDtypeStruct((4, HG_W), F32))(dlb_sum, lb)


def _lower_bounds(lb_logits_full, *, name):
    def body(l_ref, o_ref):
        for d in range(2):
            l0, l1 = l_ref[2 * d:2 * d + 1, :], l_ref[2 * d + 1:2 * d + 2, :]
            mx = jnp.maximum(l0, l1)
            e0, e1 = jnp.exp(l0 - mx), jnp.exp(l1 - mx)
            o_ref[d:d + 1, :] = e0 / (e0 + e1)

    return pl.pallas_call(body, name=name, out_shape=jax.ShapeDtypeStruct((2, HG_W), F32))(
        lb_logits_full.reshape(4, HG_W))


def _local_step(x, target, norm1_w, w_in_t, lb, hg_norm_w, q_norm_w, k_norm_w, att_norm_w, w_out, norm2_w,
                w_g_t, w_u_t, w_down, final_norm_w, reduce_early=None):
    T = x.shape[0]
    cos, sin = _rope_tables(T)
    qw8 = jnp.tile(q_norm_w, (1, ATT_HEADS))
    kw2 = jnp.tile(k_norm_w, (1, ATT_KV))

    h, r1 = _rms_fwd(x, norm1_w, name="norm1_fwd")
    U = _mm_nn([(h, w_in_t)], trans_b=True, name="in_proj")
    o_f, st_f = _gla_fwd(U, lb[0:1], f_block=1, reverse=False, name="gla_fwd_f")
    o_b, st_b = _gla_fwd(U, lb[1:2], f_block=2, reverse=True, name="gla_fwd_b")
    mix_hg = _hg_post_fwd(o_f, o_b, U, hg_norm_w, name="hg_post_fwd")
    q_c, qn_c, kmax2, k_c, v_c = _att_prep_fwd(U, cos, sin, qw8, kw2, name="att_prep_fwd")
    kmax = jnp.sqrt(jnp.max(kmax2.reshape(ATT_KV, ATT_DH), axis=1))
    m_c = qn_c * (kmax * 1.001).reshape(ATT_KV, 1, 1, 1)
    o_c, lse = lax.cond(jnp.max(m_c) <= FA_BOUND_MAX,
                        lambda: _flash_fwd_bounded(q_c, k_c, v_c, m_c, name="flash_fwd_bounded"),
                        lambda: _flash_fwd(q_c, k_c, v_c, name="flash_fwd"))
    o_att, mix_att = _att_post_fwd(o_c, att_norm_w, name="att_post_fwd")
    x1 = _mm_nn([(mix_hg, w_out[:HG_W]), (mix_att, w_out[HG_W:])], residual=x, name="out_proj")
    h2, r2 = _rms_fwd(x1, norm2_w, name="norm2_fwd")
    gate, up, act = _ffn_up(h2, w_g_t, w_u_t, name="ffn_up")
    x2 = _mm_nn([(act, w_down)], residual=x1, name="ffn_down")
    loss, dx2, dx2b, d_final = _loss_head(x2, target, final_norm_w.reshape(1, D_MODEL), name="loss_head")

    d_gate, d_up = _ffn_act_bwd(dx2b, w_down, gate, up, name="ffn_act_bwd")
    dw_down = _mm_tn(act, dx2b, tma_cap=1408, name="dw_down")
    dh2 = _mm_nn([(d_gate, w_g_t), (d_up, w_u_t)], tm=256, name="ffn_up_bwd")
    dw_g = _mm_tn(d_gate, h2, tma_cap=1408, name="dw_gate")
    dw_u = _mm_tn(d_up, h2, tma_cap=1408, name="dw_up")
    dx1, dx1b, d_norm2 = _rms_bwd(dh2, x1, r2, norm2_w, dx2, emit_bf16=True, name="norm2_bwd")
    dmix = _mm_nn([(dx1b, w_out)], trans_b=True, name="out_proj_bwd")
    dw_out = jnp.concatenate([_mm_tn(mix_hg, dx1b, name="dw_out_hg"), _mm_tn(mix_att, dx1b, name="dw_out_att")], axis=0)
    do_c, delta, d_att = _att_post_bwd(dmix, o_att, att_norm_w, name="att_post_bwd")
    ride = None if reduce_early is None else reduce_early(dw_out, dw_g, dw_u, dw_down)
    dq_c, dk_c, dv_c, *rode = _flash_bwd(q_c, k_c, v_c, do_c, lse, delta, ride=ride, name="flash_bwd")
    dU_att, d_qn, d_kn = _att_prep_bwd(U, dq_c, dk_c, dv_c, cos, sin, qw8, kw2, name="att_prep_bwd")
    do_hg, du_g, d_hg = _hg_post_bwd(dmix, o_f, o_b, U, hg_norm_w, name="hg_post_bwd")
    dq_f, dz_f, dv_f, dlb_f = _gla_bwd(U, lb[0:1], do_hg, st_f, f_block=1, reverse=False, name="gla_bwd_f")
    dU_hg, dlb_b = _gla_bwd(U, lb[1:2], do_hg, st_b, f_block=2, reverse=True, prev=(dq_f, dz_f, dv_f, du_g),
                            name="gla_bwd_b")
    w_hg = 5 * HG_W
    dh = _mm_nn([(dU_hg, w_in_t[:w_hg]), (dU_att, w_in_t[w_hg:])], name="in_proj_bwd")
    dw_in = jnp.concatenate([_mm_tn(dU_hg, h, tma_cap=1280, name="dw_in_hg"), _mm_tn(dU_att, h, name="dw_in_att")],
                            axis=0)
    grad_x, d_norm1 = _rms_bwd(dh, x, r1, norm1_w, dx1, emit_bf16=False, name="norm1_bwd")
    d_hg, d_qn, d_kn = _fold_heads(d_hg, d_qn, d_kn, name="fold_heads")

    big = dict(w_in=dw_in, w_out=dw_out, w_g=dw_g, w_u=dw_u, w_down=dw_down)
    small = dict(norm1=d_norm1, norm2=d_norm2, final=d_final, att=d_att, hg=d_hg,
                 qn=d_qn[:, :ATT_DH], kn=d_kn[:, :ATT_DH], lb=jnp.concatenate([dlb_f, dlb_b], axis=0))
    return loss, grad_x, big, small, rode


def kernel(x, norm1_w, w_in, lb_logits, hg_norm_w, q_norm_w, k_norm_w, att_norm_w, w_out, norm2_w, w_gate_up, w_down, final_norm_w, loss_target, m_norm1_w, m_w_in, m_lb_logits, m_hg_norm_w, m_q_norm_w, m_k_norm_w, m_att_norm_w, m_w_out, m_norm2_w, m_w_gate_up, m_w_down, m_final_norm_w, v_norm1_w, v_w_in, v_lb_logits, v_hg_norm_w, v_q_norm_w, v_k_norm_w, v_att_norm_w, v_w_out, v_norm2_w, v_w_gate_up, v_w_down, v_final_norm_w):
    T = x.shape[1]
    me = 4 * lax.axis_index("x") + 2 * lax.axis_index("y") + lax.axis_index("c")
    c_in, r_out, c_gu, r_dn = w_in.shape[2], w_out.shape[1], w_gate_up.shape[2], w_down.shape[1]
    lb_cols = lb_logits.shape[2]

    g_in, g_out, g_gu, g_dn, g_lb = _all_gather(
        [w_in[0].T.astype(BF16), w_out[0].astype(BF16), w_gate_up[0].T.astype(BF16), w_down[0].astype(BF16),
         lb_logits.reshape(4, lb_cols)], name="gather_weights")
    w_in_t = g_in.reshape(N_DEV * c_in, D_MODEL)
    w_out_f = g_out.reshape(N_DEV * r_out, D_MODEL)
    g_gu = g_gu.reshape(2, (N_DEV // 2) * c_gu, D_MODEL)
    w_g_t, w_u_t = g_gu[0], g_gu[1]
    w_dn_f = g_dn.reshape(N_DEV * r_dn, D_MODEL)
    lb_logits_f = g_lb.transpose(1, 0, 2).reshape(2, 2, N_DEV * lb_cols)
    lb = _lower_bounds(lb_logits_f, name="lower_bounds")

    chips = N_DEV // 2
    core = lax.axis_index("c").astype(jnp.int32).reshape(1)
    by_owner = lambda g, r: g.reshape(chips, 2, r, D_MODEL)

    def chip_sums(mine, names, call):
        theirs = _core_swap(mine, name=call)
        return [_pair_sum(g, o, core, name="pair_sum_" + nm) for g, o, nm in zip(mine, theirs, names)]

    def reduce_early(dw_out, dw_g_t, dw_u_t, dw_down):
        return chip_sums([by_owner(dw_out, r_out), by_owner(jnp.concatenate([dw_g_t, dw_u_t], axis=0), c_gu),
                          by_owner(dw_down, r_dn)], ("w_out", "w_gu", "w_down"), "exchange_cores_early")

    loss, grad_x, big, small, (p_out, p_gu, p_dn) = _local_step(
        x[0], loss_target[0], norm1_w, w_in_t, lb, hg_norm_w, q_norm_w, k_norm_w, att_norm_w, w_out_f, norm2_w,
        w_g_t, w_u_t, w_dn_f, final_norm_w, reduce_early=reduce_early)
    p_gu = p_gu.transpose(0, 2, 1)

    packed = _pack_small(small["norm1"], small["norm2"], small["final"], small["att"], small["hg"],
                         small["qn"], small["kn"], small["lb"], loss)
    p_in, all_small = _exchange(chip_sums([by_owner(big["w_in"], c_in)], ("w_in",), "exchange_cores"),
                                masks=CHIP_MASKS, slot=_chip_slot, bcast=packed, name="exchange_chips")
    p_in = p_in.transpose(0, 2, 1)

    g_w_in, d_w_in, nm_w_in, nv_w_in = _adamw(p_in, w_in[0], m_w_in[0], v_w_in[0], name="adamw_w_in")
    g_w_out, d_w_out, nm_w_out, nv_w_out = _adamw(p_out, w_out[0], m_w_out[0], v_w_out[0], name="adamw_w_out")
    g_w_gu, d_w_gu, nm_w_gu, nv_w_gu = _adamw(p_gu, w_gate_up[0], m_w_gate_up[0], v_w_gate_up[0], name="adamw_w_gu")
    g_w_dn, d_w_dn, nm_w_dn, nv_w_dn = _adamw(p_dn, w_down[0], m_w_down[0], v_w_down[0], name="adamw_w_down")

    pk = lambda vecs: _pack_small(*vecs)
    w_pk = pk([norm1_w, norm2_w, final_norm_w, att_norm_w, hg_norm_w, q_norm_w, k_norm_w])
    m_pk = pk([m_norm1_w, m_norm2_w, m_final_norm_w, m_att_norm_w, m_hg_norm_w, m_q_norm_w, m_k_norm_w])
    v_pk = pk([v_norm1_w, v_norm2_w, v_final_norm_w, v_att_norm_w, v_hg_norm_w, v_q_norm_w, v_k_norm_w])
    g_pk, d_pk, nm_pk, nv_pk = _adamw(all_small, w_pk, m_pk, v_pk, name="adamw_small")

    dlb_sum = g_pk[5:6, :].reshape(2, HG_W)
    g_lb_full = _lb_grad(dlb_sum, lb, name="lb_grad")
    g_lb_mine = lax.dynamic_slice_in_dim(g_lb_full, me * lb_cols, lb_cols, axis=1)
    g_lb_s, d_lb, nm_lb, nv_lb = _adamw(g_lb_mine[None], lb_logits.reshape(4, lb_cols),
                                        m_lb_logits.reshape(4, lb_cols), v_lb_logits.reshape(4, lb_cols),
                                        name="adamw_lb")

    loss_total = g_pk[6, 0]

    def outs(big4, lb_arr, pk_arr):
        n1, n2, fin, att, hg, qn, kn = _unpack_small(pk_arr)
        b_in, b_out, b_gu, b_dn = big4
        return [n1, b_in[None], lb_arr.reshape(2, 2, lb_cols), hg, qn, kn, att, b_out[None], n2, b_gu[None],
                b_dn[None], fin]

    return (loss_total, grad_x[None],
            *outs((g_w_in, g_w_out, g_w_gu, g_w_dn), g_lb_s, g_pk),
            *outs((d_w_in, d_w_out, d_w_gu, d_w_dn), d_lb, d_pk),
            *outs((nm_w_in, nm_w_out, nm_w_gu, nm_w_dn), nm_lb, nm_pk),
            *outs((nv_w_in, nv_w_out, nv_w_gu, nv_w_dn), nv_lb, nv_pk))
```

```python
import math

import jax
import jax.numpy as jnp
import numpy as np
from jax import lax
from jax.experimental import pallas as pl
from jax.experimental.pallas import tpu as pltpu

F32 = jnp.float32
BF16 = jnp.bfloat16

N_DEV = 8
D_MODEL = 1024
EPS = 1e-6
HG_HEADS = 4
HG_D = 128
HG_W = HG_HEADS * HG_D
CHUNK = 64
ATT_HEADS = 8
ATT_KV = 2
ATT_G = ATT_HEADS // ATT_KV
ATT_DH = 64
ATT_QW = ATT_HEADS * ATT_DH
ATT_KW = ATT_KV * ATT_DH
GRID_W = 64
ROPE_THETA = 10000.0
D_FF = 2816
ADAM_LR, ADAM_B1, ADAM_B2, ADAM_EPS, ADAM_WD, ADAM_STEP = 0.001, 0.9, 0.999, 1e-08, 0.01, 10

LANES = 128
VMEM_LIMIT = 48 * 1024 * 1024
MESH = pl.DeviceIdType.MESH
ANY = pl.BlockSpec(memory_space=pl.ANY)


def _params(sem=None):
    return pltpu.CompilerParams(dimension_semantics=sem, vmem_limit_bytes=VMEM_LIMIT)


def _pick(n, cap):
    best = None
    for t in range(LANES, cap + 1, LANES):
        if n % t == 0:
            best = t
    assert best is not None, (n, cap)
    return best


def _sigmoid(x):
    return 1.0 / (1.0 + jnp.exp(-x))


def _dot(a, b):
    return jnp.dot(a.astype(BF16), b.astype(BF16), preferred_element_type=F32)


def _dot_nt(a, b):
    return lax.dot_general(a.astype(BF16), b.astype(BF16), (((1,), (1,)), ((), ())),
                           preferred_element_type=F32)


def _dot_tn(a, b):
    return lax.dot_general(a.astype(BF16), b.astype(BF16), (((0,), (0,)), ((), ())),
                           preferred_element_type=F32)


def _mm_nn(pairs, *, name, out_dtype=F32, residual=None, tm=512, tn_cap=None, trans_b=False, tail=None):
    M = pairs[0][0].shape[0]
    N = pairs[0][1].shape[0 if trans_b else 1]
    tn = N if tn_cap is None else _pick(N, tn_cap)
    n_pairs = len(pairs)
    has_res = residual is not None
    dims = (((1,), (1,)), ((), ())) if trans_b else (((1,), (0,)), ((), ()))
    assert tail is None or tn == N

    def body(*refs):
        acc = None
        for i in range(n_pairs):
            d = lax.dot_general(refs[2 * i][...], refs[2 * i + 1][...], dims, preferred_element_type=F32)
            acc = d if acc is None else acc + d
        if has_res:
            acc = acc + refs[2 * n_pairs][...]
        if tail is None:
            refs[-1][...] = acc.astype(out_dtype)
        else:
            tail["fn"](acc, pl.program_id(0) == 0, *refs[2 * n_pairs + has_res:])

    kinds = {"row": ((tm, N), (M, N), lambda i, j: (i, 0)), "col": ((tm, 1), (M, 1), lambda i, j: (i, 0)),
             "vec": ((1, N), (1, N), lambda i, j: (0, 0)), "one": ((1, 1), (1, 1), lambda i, j: (0, 0))}
    in_specs, args = [], []
    for a, b in pairs:
        k = a.shape[1]
        b_spec = pl.BlockSpec((tn, k), lambda i, j: (j, 0)) if trans_b else pl.BlockSpec((k, tn), lambda i, j: (0, j))
        in_specs += [pl.BlockSpec((tm, k), lambda i, j: (i, 0)), b_spec]
        args += [a, b]
    if has_res:
        in_specs.append(pl.BlockSpec((tm, tn), lambda i, j: (i, j)))
        args.append(residual)
    if tail is None:
        out_specs = pl.BlockSpec((tm, tn), lambda i, j: (i, j))
        out_shape = jax.ShapeDtypeStruct((M, N), out_dtype)
    else:
        for arr, kind in tail["ins"]:
            in_specs.append(pl.BlockSpec(kinds[kind][0], kinds[kind][2]))
            args.append(arr)
        out_specs = [pl.BlockSpec(kinds[kind][0], kinds[kind][2]) for _, kind in tail["outs"]]
        out_shape = [jax.ShapeDtypeStruct(kinds[kind][1], dt) for dt, kind in tail["outs"]]
    return pl.pallas_call(
        body, name=name, grid=(M // tm, N // tn), in_specs=in_specs, out_specs=out_specs, out_shape=out_shape,
        compiler_params=_params(("parallel" if tail is None else "arbitrary", "arbitrary")),
    )(*args)


def _mm_tn(a, b, *, name, tma_cap=1024, tnb_cap=1024, tk=1024):
    T, Ma = a.shape
    Nb = b.shape[1]
    tma, tnb = _pick(Ma, tma_cap), _pick(Nb, tnb_cap)
    tk = min(tk, T)
    n_k = T // tk

    def body(a_ref, b_ref, o_ref, acc_ref):
        k = pl.program_id(2)

        @pl.when(k == 0)
        def _():
            acc_ref[...] = jnp.zeros_like(acc_ref)

        acc_ref[...] += lax.dot_general(a_ref[...], b_ref[...], (((0,), (0,)), ((), ())),
                                        preferred_element_type=F32)

        @pl.when(k == n_k - 1)
        def _():
            o_ref[...] = acc_ref[...]

    return pl.pallas_call(
        body, name=name, grid=(Ma // tma, Nb // tnb, n_k),
        in_specs=[pl.BlockSpec((tk, tma), lambda i, j, k: (k, i)), pl.BlockSpec((tk, tnb), lambda i, j, k: (k, j))],
        out_specs=pl.BlockSpec((tma, tnb), lambda i, j, k: (i, j)),
        out_shape=jax.ShapeDtypeStruct((Ma, Nb), F32),
        scratch_shapes=[pltpu.VMEM((tma, tnb), F32)],
        compiler_params=_params(("parallel", "parallel", "arbitrary")),
    )(a, b)


def _rms_fwd(x, w, *, name, tm=512):
    T, Dm = x.shape

    def body(x_ref, w_ref, h_ref, r_ref):
        xv = x_ref[...]
        r = lax.rsqrt(jnp.mean(xv * xv, axis=-1, keepdims=True) + EPS)
        h_ref[...] = (xv * r * w_ref[...]).astype(BF16)
        r_ref[...] = r

    return pl.pallas_call(
        body, name=name, grid=(T // tm,),
        in_specs=[pl.BlockSpec((tm, Dm), lambda i: (i, 0)), pl.BlockSpec((1, Dm), lambda i: (0, 0))],
        out_specs=[pl.BlockSpec((tm, Dm), lambda i: (i, 0)), pl.BlockSpec((tm, 1), lambda i: (i, 0))],
        out_shape=[jax.ShapeDtypeStruct((T, Dm), BF16), jax.ShapeDtypeStruct((T, 1), F32)],
        compiler_params=_params(("parallel",)),
    )(x, w)


def _tail_rms_fwd(w):
    def fn(xv, first, w_ref, x_ref, h_ref, r_ref):
        r = lax.rsqrt(jnp.mean(xv * xv, axis=-1, keepdims=True) + EPS)
        x_ref[...] = xv
        h_ref[...] = (xv * r * w_ref[...]).astype(BF16)
        r_ref[...] = r

    return dict(fn=fn, ins=[(w, "vec")], outs=[(F32, "row"), (BF16, "row"), (F32, "col")])


def _tail_rms_bwd(x, r, w, dres, *, emit_bf16):
    def fn(dhv, first, x_ref, r_ref, w_ref, dres_ref, *outs):
        dx_ref, dw_ref = outs[0], outs[-1]

        @pl.when(first)
        def _():
            dw_ref[...] = jnp.zeros_like(dw_ref)

        rv = r_ref[...]
        xh = x_ref[...] * rv
        dxh = dhv * w_ref[...]
        t = jnp.mean(dxh * xh, axis=-1, keepdims=True)
        dx = dres_ref[...] + rv * (dxh - xh * t)
        dx_ref[...] = dx
        if emit_bf16:
            outs[1][...] = dx.astype(BF16)
        dw_ref[...] += jnp.sum(dhv * xh, axis=0, keepdims=True)

    outs = [(F32, "row")] + ([(BF16, "row")] if emit_bf16 else []) + [(F32, "vec")]
    return dict(fn=fn, ins=[(x, "row"), (r, "col"), (w, "vec"), (dres, "row")], outs=outs)


def _tail_loss(target, w):
    def fn(xv, first, t_ref, w_ref, loss_ref, dx_ref, dxb_ref, dw_ref):
        @pl.when(first)
        def _():
            loss_ref[...] = jnp.zeros_like(loss_ref)
            dw_ref[...] = jnp.zeros_like(dw_ref)

        r = lax.rsqrt(jnp.mean(xv * xv, axis=-1, keepdims=True) + EPS)
        xh = xv * r
        wv = w_ref[...]
        err = xh * wv - t_ref[...]
        row_loss = jnp.mean(err * err, axis=-1, keepdims=True)
        loss_ref[...] += 0.5 * jnp.sum(row_loss, axis=0, keepdims=True)
        dy = err * (1.0 / xv.shape[-1])
        dxh = dy * wv
        t = jnp.mean(dxh * xh, axis=-1, keepdims=True)
        dx = r * (dxh - xh * t)
        dx_ref[...] = dx
        dxb_ref[...] = dx.astype(BF16)
        dw_ref[...] += jnp.sum(dy * xh, axis=0, keepdims=True)

    return dict(fn=fn, ins=[(target, "row"), (w, "vec")],
                outs=[(F32, "one"), (F32, "row"), (BF16, "row"), (F32, "vec")])


GLA_TB = 512
GLA_NC = GLA_TB // CHUNK
GLA_UNROLL = 4


def _cumsum_rows(x, row, reverse):
    n = x.shape[0]
    s = 1
    while s < n:
        if not reverse:
            x = x + jnp.where(row >= s, pltpu.roll(x, s, 0), 0.0)
        else:
            x = x + jnp.where(row < n - s, pltpu.roll(x, n - s, 0), 0.0)
        s *= 2
    return x


def _gla_gates(uq, z, lbv):
    q = uq * _sigmoid(uq)
    sg = _sigmoid(z)
    sgn = _sigmoid(-z)
    f = lbv + (1.0 - lbv) * sg
    k = (1.0 - lbv) * sgn
    return q, sg, sgn, f, k


def _gla_decays(f, row, reverse):
    b = _cumsum_rows(jnp.log(f), row, reverse)
    if not reverse:
        bref, blast = b[CHUNK // 2 - 1:CHUNK // 2, :], b[CHUNK - 1:CHUNK, :]
    else:
        bref, blast = b[CHUNK // 2:CHUNK // 2 + 1, :], b[0:1, :]
    return b, bref, blast


def _gla_fwd(U, lb, *, f_block, reverse, name):
    T = U.shape[0]
    nb = T // GLA_TB

    def body(uq_ref, uf_ref, ui_ref, lb_ref, o_ref, st_ref, s_ref):
        @pl.when(pl.program_id(0) == 0)
        def _():
            s_ref[...] = jnp.zeros_like(s_ref)

        row = lax.broadcasted_iota(jnp.int32, (CHUNK, HG_D), 0)
        ri = lax.broadcasted_iota(jnp.int32, (CHUNK, CHUNK), 0)
        ci = lax.broadcasted_iota(jnp.int32, (CHUNK, CHUNK), 1)
        mask = (ri <= ci) if reverse else (ri >= ci)

        def chunk(j, carry):
            c = (GLA_NC - 1 - j) if reverse else j
            rows = pl.ds(pl.multiple_of(c * CHUNK, CHUNK), CHUNK)
            for h in range(HG_HEADS):
                cols = pl.ds(h * HG_D, HG_D)
                v = ui_ref[rows, cols]
                q, _, _, f, k = _gla_gates(uq_ref[rows, cols], uf_ref[rows, cols], lb_ref[:, cols])
                b, bref, blast = _gla_decays(f, row, reverse)
                s = jnp.where(mask, _dot_nt(q * jnp.exp(b - bref), k * jnp.exp(bref - b)), 0.0)
                st = s_ref[h]
                st_ref[c, h] = st
                o_ref[rows, cols] = _dot(s, v) + _dot_nt(q * jnp.exp(b), st)
                s_ref[h] = st * jnp.exp(blast) + _dot_tn(v, k * jnp.exp(blast - b))
            return carry

        lax.fori_loop(0, GLA_NC, chunk, 0, unroll=GLA_NC)

    blk = (lambda i: nb - 1 - i) if reverse else (lambda i: i)
    ucol = lambda cb: pl.BlockSpec((GLA_TB, HG_W), lambda i: (blk(i), cb))
    return pl.pallas_call(
        body, name=name, grid=(nb,),
        in_specs=[ucol(0), ucol(f_block), ucol(3), pl.BlockSpec((1, HG_W), lambda i: (0, 0))],
        out_specs=[pl.BlockSpec((GLA_TB, HG_W), lambda i: (blk(i), 0)),
                   pl.BlockSpec((GLA_NC, HG_HEADS, HG_D, HG_D), lambda i: (blk(i), 0, 0, 0))],
        out_shape=[jax.ShapeDtypeStruct((T, HG_W), F32),
                   jax.ShapeDtypeStruct((T // CHUNK, HG_HEADS, HG_D, HG_D), F32)],
        scratch_shapes=[pltpu.VMEM((HG_HEADS, HG_D, HG_D), F32)],
        compiler_params=_params(("arbitrary",)),
    )(U, U, U, lb)


def _gla_bwd(U, lb, do, states, *, f_block, reverse, name, prev=None):
    T = U.shape[0]
    nb = T // GLA_TB
    final = prev is not None

    def body(uq_ref, uf_ref, ui_ref, lb_ref, do_ref, st_ref, *rest):
        if final:
            dqp_ref, dzp_ref, dvp_ref, dug_ref, out_ref, dlb_ref, ds_ref = rest
        else:
            dq_ref, dz_ref, dv_ref, dlb_ref, ds_ref = rest

        @pl.when(pl.program_id(0) == 0)
        def _():
            ds_ref[...] = jnp.zeros_like(ds_ref)
            dlb_ref[...] = jnp.zeros_like(dlb_ref)

        row = lax.broadcasted_iota(jnp.int32, (CHUNK, HG_D), 0)
        ri = lax.broadcasted_iota(jnp.int32, (CHUNK, CHUNK), 0)
        ci = lax.broadcasted_iota(jnp.int32, (CHUNK, CHUNK), 1)
        mask = (ri <= ci) if reverse else (ri >= ci)

        def chunk(j, carry):
            c = j if reverse else (GLA_NC - 1 - j)
            rows = pl.ds(pl.multiple_of(c * CHUNK, CHUNK), CHUNK)
            for h in range(HG_HEADS):
                cols = pl.ds(h * HG_D, HG_D)
                v = ui_ref[rows, cols]
                lbv = lb_ref[:, cols]
                uq = uq_ref[rows, cols]
                q, sg, sgn, f, k = _gla_gates(uq, uf_ref[rows, cols], lbv)
                b, bref, blast = _gla_decays(f, row, reverse)
                eq, ek, eb, el, dec = (jnp.exp(b - bref), jnp.exp(bref - b), jnp.exp(b), jnp.exp(blast - b),
                                       jnp.exp(blast))
                qin, kin, qb, klast = q * eq, k * ek, q * eb, k * el
                dov = do_ref[rows, cols]
                st = st_ref[c, h]
                dst = ds_ref[h]
                p = jnp.where(mask, _dot_nt(qin, kin), 0.0)
                dp = jnp.where(mask, _dot_nt(dov, v), 0.0)
                dqin = _dot(dp, kin)
                dkin = _dot_tn(dp, qin)
                dv = _dot_tn(p, dov) + _dot_nt(klast, dst)
                dqb = _dot(dov, st)
                dklast = _dot(v, dst)
                ds_ref[h] = _dot_tn(dov, qb) + dst * dec
                db = dqin * qin - dkin * kin + dqb * qb - dklast * klast
                extra = (jnp.sum(dklast * klast, axis=0, keepdims=True)
                         + dec * jnp.sum(st * dst, axis=0, keepdims=True))
                dg = _cumsum_rows(db, row, not reverse) + extra
                dq = dqin * eq + dqb * eb
                dk = dkin * ek + dklast * el
                dfk = dg / f - dk
                dz = (dfk * (1.0 - lbv) * sg * sgn).astype(BF16)
                dlb_ref[:, cols] += jnp.sum(dfk * sgn, axis=0, keepdims=True)
                if final:
                    sq = _sigmoid(uq)
                    col = lambda blk: pl.ds(blk * HG_W + h * HG_D, HG_D)
                    out_ref[rows, col(0)] = ((dq + dqp_ref[rows, cols]) * (sq * (1.0 + uq * (1.0 - sq)))).astype(BF16)
                    out_ref[rows, col(1)] = dzp_ref[rows, cols]
                    out_ref[rows, col(2)] = dz
                    out_ref[rows, col(3)] = (dv + dvp_ref[rows, cols]).astype(BF16)
                    out_ref[rows, col(4)] = dug_ref[rows, cols]
                else:
                    dq_ref[rows, cols] = dq
                    dz_ref[rows, cols] = dz
                    dv_ref[rows, cols] = dv
            return carry

        lax.fori_loop(0, GLA_NC, chunk, 0, unroll=GLA_UNROLL)

    blk = (lambda i: i) if reverse else (lambda i: nb - 1 - i)
    ucol = lambda cb: pl.BlockSpec((GLA_TB, HG_W), lambda i: (blk(i), cb))
    tok = pl.BlockSpec((GLA_TB, HG_W), lambda i: (blk(i), 0))
    vec = pl.BlockSpec((1, HG_W), lambda i: (0, 0))
    in_specs = [ucol(0), ucol(f_block), ucol(3), vec, tok,
                pl.BlockSpec((GLA_NC, HG_HEADS, HG_D, HG_D), lambda i: (blk(i), 0, 0, 0))]
    vec_shape = jax.ShapeDtypeStruct((1, HG_W), F32)
    if final:
        in_specs += [tok] * 4
        out_specs = [pl.BlockSpec((GLA_TB, 5 * HG_W), lambda i: (blk(i), 0)), vec]
        out_shape = [jax.ShapeDtypeStruct((T, 5 * HG_W), BF16), vec_shape]
    else:
        out_specs = [tok, tok, tok, vec]
        out_shape = [jax.ShapeDtypeStruct((T, HG_W), F32), jax.ShapeDtypeStruct((T, HG_W), BF16),
                     jax.ShapeDtypeStruct((T, HG_W), F32), vec_shape]
    return pl.pallas_call(
        body, name=name, grid=(nb,), in_specs=in_specs, out_specs=out_specs, out_shape=out_shape,
        scratch_shapes=[pltpu.VMEM((HG_HEADS, HG_D, HG_D), F32)],
        compiler_params=_params(("arbitrary",)),
    )(U, U, U, lb, do, states, *(prev if final else ()))


def _hg_post_fwd(o_f, o_b, U, w, *, name, tm=512):
    T = o_f.shape[0]

    def body(of_ref, ob_ref, ug_ref, w_ref, out_ref):
        wv = w_ref[...]
        for h in range(HG_HEADS):
            cols = pl.ds(h * HG_D, HG_D)
            o = of_ref[:, cols] + ob_ref[:, cols]
            r = lax.rsqrt(jnp.mean(o * o, axis=-1, keepdims=True) + EPS)
            ug = ug_ref[:, cols]
            out_ref[:, cols] = (o * r * wv * (ug * _sigmoid(ug))).astype(BF16)

    tok = pl.BlockSpec((tm, HG_W), lambda i: (i, 0))
    return pl.pallas_call(
        body, name=name, grid=(T // tm,),
        in_specs=[tok, tok, pl.BlockSpec((tm, HG_W), lambda i: (i, 4)), pl.BlockSpec((1, HG_D), lambda i: (0, 0))],
        out_specs=tok, out_shape=jax.ShapeDtypeStruct((T, HG_W), BF16),
        compiler_params=_params(("parallel",)),
    )(o_f, o_b, U, w)


def _hg_post_bwd(dmix, o_f, o_b, U, w, *, name, tm=512):
    T = o_f.shape[0]

    def body(dm_ref, of_ref, ob_ref, ug_ref, w_ref, do_ref, dug_ref, dw_ref):
        @pl.when(pl.program_id(0) == 0)
        def _():
            dw_ref[...] = jnp.zeros_like(dw_ref)

        wv = w_ref[...]
        for h in range(HG_HEADS):
            cols = pl.ds(h * HG_D, HG_D)
            o = of_ref[:, cols] + ob_ref[:, cols]
            r = lax.rsqrt(jnp.mean(o * o, axis=-1, keepdims=True) + EPS)
            xh = o * r
            ug = ug_ref[:, cols]
            sg = _sigmoid(ug)
            dm = dm_ref[:, cols]
            dn = dm * (ug * sg)
            dug_ref[:, cols] = (dm * (xh * wv) * (sg * (1.0 + ug * (1.0 - sg)))).astype(BF16)
            dxh = dn * wv
            t = jnp.mean(dxh * xh, axis=-1, keepdims=True)
            do_ref[:, cols] = r * (dxh - xh * t)
            dw_ref[:, cols] += jnp.sum(dn * xh, axis=0, keepdims=True)

    tok = pl.BlockSpec((tm, HG_W), lambda i: (i, 0))
    vec = pl.BlockSpec((1, HG_W), lambda i: (0, 0))
    return pl.pallas_call(
        body, name=name, grid=(T // tm,),
        in_specs=[tok, tok, tok, pl.BlockSpec((tm, HG_W), lambda i: (i, 4)), pl.BlockSpec((1, HG_D), lambda i: (0, 0))],
        out_specs=[tok, tok, vec],
        out_shape=[jax.ShapeDtypeStruct((T, HG_W), F32), jax.ShapeDtypeStruct((T, HG_W), BF16),
                   jax.ShapeDtypeStruct((1, HG_W), F32)],
        compiler_params=_params(("arbitrary",)),
    )(dmix, o_f, o_b, U, w)


def _rope_tables(T):
    rows = T // GRID_W
    row = np.repeat(np.arange(rows), GRID_W).astype(np.float32)
    col = np.tile(np.arange(GRID_W), rows).astype(np.float32)
    axis_dim = ATT_DH // 2
    freqs = (np.float32(ROPE_THETA) ** (-np.arange(0, axis_dim, 2, dtype=np.float32) / np.float32(axis_dim))
             ).astype(np.float32)
    ang = np.concatenate([row[:, None] * freqs, col[:, None] * freqs], axis=-1).astype(np.float32)
    cos, sin = np.cos(ang), np.sin(ang)
    c = np.repeat(cos, 2, axis=-1)
    s = np.stack([-sin, sin], axis=-1).reshape(T, ATT_DH)
    return jnp.asarray(np.tile(c, (1, 2)), F32), jnp.asarray(np.tile(s, (1, 2)), F32)


def _head_blockdiag(width):
    shift = ATT_DH.bit_length() - 1
    ri = jnp.right_shift(lax.broadcasted_iota(jnp.int32, (width, width), 0), shift)
    ci = jnp.right_shift(lax.broadcasted_iota(jnp.int32, (width, width), 1), shift)
    return jnp.where(ri == ci, 1.0, 0.0).astype(BF16)


def _head_sum(x, bd):
    hi = x.astype(BF16)
    lo = (x - hi.astype(F32)).astype(BF16)
    return jnp.dot(hi, bd, preferred_element_type=F32) + jnp.dot(lo, bd, preferred_element_type=F32)


def _pair_swap(x, even):
    n = x.shape[-1]
    return jnp.where(even, pltpu.roll(x, n - 1, 1), pltpu.roll(x, 1, 1))


FA_TQ = 512


FA_TK = 512


def _cols_from_tokens(x, kv):
    w = ATT_G * ATT_DH
    xt = x[:, kv * w:(kv + 1) * w].T
    return jnp.concatenate([xt[g * ATT_DH:(g + 1) * ATT_DH, :] for g in range(ATT_G)], axis=1)


def _tokens_from_cols(c):
    tq = c.shape[1] // ATT_G
    return jnp.concatenate([c[:, g * tq:(g + 1) * tq] for g in range(ATT_G)], axis=0).T


def _store_cols(ref, x, norm_ref=None):
    for kv in range(ATT_KV):
        cols = _cols_from_tokens(x, kv).astype(BF16)
        ref[kv, 0] = cols
        if norm_ref is not None:
            cf = cols.astype(F32)
            norm_ref[kv, 0] = jnp.sqrt(jnp.sum(cf * cf, axis=0, keepdims=True))


def _att_prep_fwd(U, cos, sin, qw, kw, *, name):
    T = U.shape[0]
    tm = min(FA_TQ, T)
    R = ATT_G * tm
    scale = ATT_DH ** -0.5

    def head_rows(ref, x):
        xt = x.astype(F32).T
        for kv in range(ATT_KV):
            ref[kv, 0] = xt[kv * ATT_DH:(kv + 1) * ATT_DH, :].astype(BF16)

    def body(aq_ref, ak_ref, av_ref, c_ref, s_ref, qw_ref, kw_ref, q_ref, qn_ref, kmax_ref, kc_ref, vc_ref):
        @pl.when(pl.program_id(0) == 0)
        def _():
            kmax_ref[...] = jnp.zeros_like(kmax_ref)

        bd = _head_blockdiag(ATT_QW)
        c2, s2 = c_ref[...], s_ref[...]
        c8, s8 = jnp.tile(c2, (1, 4)), jnp.tile(s2, (1, 4))

        def norm_rope(x, w, c, s, bdm):
            r = lax.rsqrt(_head_sum(x * x, bdm) * (1.0 / ATT_DH) + EPS)
            y = x * r * w
            even = (lax.broadcasted_iota(jnp.int32, y.shape, 1) & 1) == 0
            return y * c + _pair_swap(y, even) * s

        _store_cols(q_ref, norm_rope(aq_ref[...], qw_ref[...], c8, s8, bd) * scale, qn_ref)
        kb = norm_rope(ak_ref[...], kw_ref[...], c2, s2, bd[:ATT_KW, :ATT_KW]).astype(BF16)
        kf = kb.astype(F32)
        ksq = _head_sum(kf * kf, bd[:ATT_KW, :ATT_KW])
        kmax_ref[...] = jnp.maximum(kmax_ref[...], jnp.max(ksq, axis=0, keepdims=True))
        head_rows(kc_ref, kb)
        head_rows(vc_ref, av_ref[...].astype(BF16))

    kv_spec = pl.BlockSpec((tm, ATT_KW), lambda i: (i, 0))
    tk = min(FA_TK, T)
    per = tk // tm
    c_spec = pl.BlockSpec((ATT_KV, 1, ATT_DH, tm), lambda i: (0, i // per, 0, i % per))
    c_shape = jax.ShapeDtypeStruct((ATT_KV, T // tk, ATT_DH, tk), BF16)
    return pl.pallas_call(
        body, name=name, grid=(T // tm,),
        in_specs=[pl.BlockSpec((tm, ATT_QW), lambda i: (i, 5)),
                  pl.BlockSpec((tm, ATT_KW), lambda i: (i, 24)), pl.BlockSpec((tm, ATT_KW), lambda i: (i, 25)),
                  kv_spec, kv_spec,
                  pl.BlockSpec((1, ATT_QW), lambda i: (0, 0)), pl.BlockSpec((1, ATT_KW), lambda i: (0, 0))],
        out_specs=[pl.BlockSpec((ATT_KV, 1, ATT_DH, R), lambda i: (0, i, 0, 0)),
                   pl.BlockSpec((ATT_KV, 1, 1, R), lambda i: (0, i, 0, 0)), pl.BlockSpec((1, ATT_KW), lambda i: (0, 0)),
                   c_spec, c_spec],
        out_shape=[jax.ShapeDtypeStruct((ATT_KV, T // tm, ATT_DH, R), BF16),
                   jax.ShapeDtypeStruct((ATT_KV, T // tm, 1, R), F32), jax.ShapeDtypeStruct((1, ATT_KW), F32),
                   c_shape, c_shape],
        compiler_params=_params(("arbitrary",)),
    )(U, U, U, cos, sin, qw, kw)


def _att_prep_bwd(U, dq_c, dk_c, dv_c, cos, sin, qw, kw, *, name):
    T = U.shape[0]
    tm = min(FA_TQ, T)
    R = ATT_G * tm
    scale = ATT_DH ** -0.5

    def body(aq_ref, ak_ref, dq_ref, dk_ref, dv_ref, c_ref, s_ref, qw_ref, kw_ref, out_ref, dqw_ref, dkw_ref):
        @pl.when(pl.program_id(0) == 0)
        def _():
            dqw_ref[...] = jnp.zeros_like(dqw_ref)
            dkw_ref[...] = jnp.zeros_like(dkw_ref)

        bd = _head_blockdiag(ATT_QW)
        c2, s2 = c_ref[...], s_ref[...]
        c8, s8 = jnp.tile(c2, (1, 4)), jnp.tile(s2, (1, 4))

        def bwd(x, dy, w, c, s, bdm):
            even = (lax.broadcasted_iota(jnp.int32, x.shape, 1) & 1) == 0
            dn = dy * c - _pair_swap(dy, even) * s
            r = lax.rsqrt(_head_sum(x * x, bdm) * (1.0 / ATT_DH) + EPS)
            xh = x * r
            dxh = dn * w
            t = _head_sum(dxh * xh, bdm) * (1.0 / ATT_DH)
            return r * (dxh - xh * t), jnp.sum(dn * xh, axis=0, keepdims=True)

        dq = jnp.concatenate([_tokens_from_cols(dq_ref[kv, 0]) for kv in range(ATT_KV)], axis=1)
        da, dw = bwd(aq_ref[...], dq * scale, qw_ref[...], c8, s8, bd)
        out_ref[:, 0:ATT_QW] = da.astype(BF16)
        dqw_ref[...] += dw
        tokens = lambda ref: jnp.concatenate([ref[kv, 0] for kv in range(ATT_KV)], axis=0).T
        da, dw = bwd(ak_ref[...], tokens(dk_ref), kw_ref[...], c2, s2, bd[:ATT_KW, :ATT_KW])
        out_ref[:, ATT_QW:ATT_QW + ATT_KW] = da.astype(BF16)
        dkw_ref[...] += dw
        out_ref[:, ATT_QW + ATT_KW:ATT_QW + 2 * ATT_KW] = tokens(dv_ref).astype(BF16)

    kv_spec = pl.BlockSpec((tm, ATT_KW), lambda i: (i, 0))
    qv = pl.BlockSpec((1, ATT_QW), lambda i: (0, 0))
    kv = pl.BlockSpec((1, ATT_KW), lambda i: (0, 0))
    w_att = ATT_QW + 2 * ATT_KW
    per = dk_c.shape[3] // tm
    c_spec = pl.BlockSpec((ATT_KV, 1, ATT_DH, tm), lambda i: (0, i // per, 0, i % per))
    return pl.pallas_call(
        body, name=name, grid=(T // tm,),
        in_specs=[pl.BlockSpec((tm, ATT_QW), lambda i: (i, 5)), pl.BlockSpec((tm, ATT_KW), lambda i: (i, 24)),
                  pl.BlockSpec((ATT_KV, 1, ATT_DH, R), lambda i: (0, i, 0, 0)), c_spec, c_spec, kv_spec, kv_spec, qv, kv],
        out_specs=[pl.BlockSpec((tm, w_att), lambda i: (i, 0)), qv, kv],
        out_shape=[jax.ShapeDtypeStruct((T, w_att), BF16),
                   jax.ShapeDtypeStruct((1, ATT_QW), F32), jax.ShapeDtypeStruct((1, ATT_KW), F32)],
        compiler_params=_params(("arbitrary",)),
    )(U, U, dq_c, dk_c, dv_c, cos, sin, qw, kw)


def _scores(k_ref, j, qv):
    return lax.dot_general(k_ref[0, j], qv, (((0,), (0,)), ((), ())), preferred_element_type=F32)


def _flash_fwd(q_c, k_c, v_c, *, name):
    _, nq, _, R = q_c.shape
    _, n_k, _, tk = v_c.shape

    def body(q_ref, k_ref, v_ref, o_ref, lse_ref, acc_ref):
        qv = q_ref[0, 0]
        acc_ref[...] = jnp.zeros_like(acc_ref)

        def step(j, carry):
            m, l = carry
            s = _scores(k_ref, j, qv)
            m_new = jnp.maximum(m, jnp.max(s, axis=0, keepdims=True))
            alpha = jnp.exp(m - m_new)
            p = jnp.exp(s - m_new)
            l = alpha * l + jnp.sum(p, axis=0, keepdims=True)
            acc_ref[...] = alpha * acc_ref[...] + jnp.dot(v_ref[0, j], p.astype(BF16), preferred_element_type=F32)
            return m_new, l

        m, l = lax.fori_loop(0, n_k, step, (jnp.full((1, R), -jnp.inf, F32), jnp.zeros((1, R), F32)))
        o_ref[0, 0] = acc_ref[...] / l
        lse_ref[0, 0] = m + jnp.log(l)

    cspec = pl.BlockSpec((1, 1, ATT_DH, R), lambda h, i: (h, i, 0, 0))
    kspec = pl.BlockSpec((1, n_k, ATT_DH, tk), lambda h, i: (h, 0, 0, 0))
    return pl.pallas_call(
        body, name=name, grid=(ATT_KV, nq),
        in_specs=[cspec, kspec, kspec],
        out_specs=[cspec, pl.BlockSpec((1, 1, 1, R), lambda h, i: (h, i, 0, 0))],
        out_shape=[jax.ShapeDtypeStruct((ATT_KV, nq, ATT_DH, R), F32), jax.ShapeDtypeStruct((ATT_KV, nq, 1, R), F32)],
        scratch_shapes=[pltpu.VMEM((ATT_DH, R), F32)],
        compiler_params=_params(("parallel", "parallel")),
    )(q_c, k_c, v_c)


FA_BOUND_MAX = 40.0


def _flash_fwd_bounded(q_c, k_c, v_c, m_c, *, name):
    _, nq, _, R = q_c.shape
    _, n_k, _, tk = v_c.shape

    def body(q_ref, k_ref, v_ref, m_ref, o_ref, lse_ref, acc_ref):
        qv = q_ref[0, 0]
        m = m_ref[0, 0]
        acc_ref[...] = jnp.zeros_like(acc_ref)

        per = math.gcd(n_k, 4)

        def step(jj, l8):
            pv = None
            for u in range(per):
                j = per * jj + u
                p = jnp.exp(_scores(k_ref, j, qv) - m)
                l8 = l8 + jnp.sum(p.reshape(tk // 8, 8, R), axis=0)
                d = jnp.dot(v_ref[0, j], p.astype(BF16), preferred_element_type=F32)
                pv = d if pv is None else pv + d
            acc_ref[...] += pv
            return l8

        l8 = lax.fori_loop(0, n_k // per, step, jnp.zeros((8, R), F32))
        l = jnp.sum(l8, axis=0, keepdims=True)
        o_ref[0, 0] = acc_ref[...] / l
        lse_ref[0, 0] = m + jnp.log(l)

    cspec = pl.BlockSpec((1, 1, ATT_DH, R), lambda h, i: (h, i, 0, 0))
    kspec = pl.BlockSpec((1, n_k, ATT_DH, tk), lambda h, i: (h, 0, 0, 0))
    vspec = pl.BlockSpec((1, 1, 1, R), lambda h, i: (h, i, 0, 0))
    return pl.pallas_call(
        body, name=name, grid=(ATT_KV, nq),
        in_specs=[cspec, kspec, kspec, vspec],
        out_specs=[cspec, vspec],
        out_shape=[jax.ShapeDtypeStruct((ATT_KV, nq, ATT_DH, R), F32), jax.ShapeDtypeStruct((ATT_KV, nq, 1, R), F32)],
        scratch_shapes=[pltpu.VMEM((ATT_DH, R), F32)],
        compiler_params=_params(("parallel", "parallel")),
    )(q_c, k_c, v_c, m_c)


CHIP_MASKS = [(1, 0, 0), (0, 1, 0), (1, 1, 0)]


def _chip_slot(p):
    return 2 * p[0] + p[1]


def _flash_bwd(q_c, k_c, v_c, do_c, lse, delta, *, name, ride=None):
    _, nq, _, R = q_c.shape
    _, n_k, _, tk = k_c.shape
    n_ride = 0 if ride is None else len(ride)

    def body(qc_ref, kc_ref, vc_ref, doc_ref, lse_ref, delta_ref, *rest):
        ride_in, rest = rest[:n_ride], rest[n_ride:]
        dq_ref, dk_ref, dv_ref = rest[:3]
        ride_out, rest = rest[3:3 + n_ride], rest[3 + n_ride:]
        acc_ref = rest[0]
        kv = pl.program_id(0)
        first = (kv == 0) & (pl.program_id(1) == 0)

        if n_ride:
            mine, copies = _exchange_copies(ride_in, ride_out, *rest[1:], masks=CHIP_MASKS, slot=_chip_slot)

            @pl.when(first)
            def _():
                for cp in mine:
                    cp.start()
                for send, _ in copies:
                    send.start()

        @pl.when(pl.program_id(1) == 0)
        def _():
            dk_ref[...] = jnp.zeros_like(dk_ref)
            dv_ref[...] = jnp.zeros_like(dv_ref)

        qc, doc = qc_ref[0, 0], doc_ref[0, 0]
        lsev, delta = lse_ref[0, 0], delta_ref[0, 0]
        acc_ref[...] = jnp.zeros_like(acc_ref)
        nt = (((1,), (1,)), ((), ()))

        def step(j, carry):
            p = jnp.exp(_scores(kc_ref, j, qc) - lsev)
            dp = _scores(vc_ref, j, doc)
            ds = (p * (dp - delta)).astype(BF16)
            acc_ref[...] += jnp.dot(kc_ref[0, j], ds, preferred_element_type=F32)
            dk_ref[0, j] += lax.dot_general(qc, ds, nt, preferred_element_type=F32)
            dv_ref[0, j] += lax.dot_general(doc, p.astype(BF16), nt, preferred_element_type=F32)
            return carry

        lax.fori_loop(0, n_k, step, 0, unroll=2)
        dq_ref[0, 0] = acc_ref[...]

        if n_ride:
            @pl.when((kv == ATT_KV - 1) & (pl.program_id(1) == nq - 1))
            def _():
                for send, recv in copies:
                    recv.wait_recv()
                    send.wait_send()
                for cp in mine:
                    cp.wait()

    cspec = pl.BlockSpec((1, 1, ATT_DH, R), lambda h, i: (h, i, 0, 0))
    vspec = pl.BlockSpec((1, 1, 1, R), lambda h, i: (h, i, 0, 0))
    kspec = pl.BlockSpec((1, n_k, ATT_DH, tk), lambda h, i: (h, 0, 0, 0))
    ride = [] if ride is None else list(ride)
    scratch = [pltpu.VMEM((ATT_DH, R), F32)]
    if n_ride:
        n_sem = len(CHIP_MASKS) * n_ride
        scratch += [pltpu.SemaphoreType.DMA((n_sem,)), pltpu.SemaphoreType.DMA((n_sem,)),
                    pltpu.SemaphoreType.DMA((n_ride,))]
    k_shape = jax.ShapeDtypeStruct(k_c.shape, F32)
    return pl.pallas_call(
        body, name=name, grid=(ATT_KV, nq),
        in_specs=[cspec, kspec, kspec, cspec, vspec, vspec] + [ANY] * n_ride,
        out_specs=[cspec, kspec, kspec] + [ANY] * n_ride,
        out_shape=[jax.ShapeDtypeStruct((ATT_KV, nq, ATT_DH, R), F32), k_shape, k_shape]
                  + [jax.ShapeDtypeStruct(g.shape, g.dtype) for g in ride],
        scratch_shapes=scratch,
        compiler_params=pltpu.CompilerParams(dimension_semantics=("arbitrary", "arbitrary"),
                                             vmem_limit_bytes=VMEM_LIMIT, has_side_effects=bool(n_ride)),
    )(q_c, k_c, v_c, do_c, lse, delta, *ride)


def _att_post_fwd(o_c, w, *, name):
    _, nq, _, R = o_c.shape
    tm = R // ATT_G
    T = nq * tm

    def body(oc_ref, w_ref, o_ref, out_ref):
        ov = jnp.concatenate([_tokens_from_cols(oc_ref[kv, 0]) for kv in range(ATT_KV)], axis=1)
        r = lax.rsqrt(jnp.mean(ov * ov, axis=-1, keepdims=True) + EPS)
        o_ref[...] = ov
        out_ref[...] = (ov * r * w_ref[...]).astype(BF16)

    tok = pl.BlockSpec((tm, ATT_QW), lambda i: (i, 0))
    return pl.pallas_call(
        body, name=name, grid=(nq,),
        in_specs=[pl.BlockSpec((ATT_KV, 1, ATT_DH, R), lambda i: (0, i, 0, 0)), pl.BlockSpec((1, ATT_QW), lambda i: (0, 0))],
        out_specs=[tok, tok],
        out_shape=[jax.ShapeDtypeStruct((T, ATT_QW), F32), jax.ShapeDtypeStruct((T, ATT_QW), BF16)],
        compiler_params=_params(("parallel",)),
    )(o_c, w)


def _att_post_bwd(dmix, o, w, *, name):
    T = o.shape[0]
    tm = min(FA_TQ, T)
    R = ATT_G * tm

    def body(dm_ref, o_ref, w_ref, do_ref, delta_ref, dw_ref):
        @pl.when(pl.program_id(0) == 0)
        def _():
            dw_ref[...] = jnp.zeros_like(dw_ref)

        ov = o_ref[...]
        r = lax.rsqrt(jnp.mean(ov * ov, axis=-1, keepdims=True) + EPS)
        xh = ov * r
        dm = dm_ref[...]
        dxh = dm * w_ref[...]
        t = jnp.mean(dxh * xh, axis=-1, keepdims=True)
        do = r * (dxh - xh * t)
        _store_cols(do_ref, do)
        dob = do.astype(BF16).astype(F32)
        for kv in range(ATT_KV):
            delta_ref[kv, 0] = jnp.sum(_cols_from_tokens(dob * ov, kv), axis=0, keepdims=True)
        dw_ref[...] += jnp.sum(dm * xh, axis=0, keepdims=True)

    tok = pl.BlockSpec((tm, ATT_QW), lambda i: (i, 0))
    vec = pl.BlockSpec((1, ATT_QW), lambda i: (0, 0))
    return pl.pallas_call(
        body, name=name, grid=(T // tm,),
        in_specs=[pl.BlockSpec((tm, ATT_QW), lambda i: (i, 1)), tok, vec],
        out_specs=[pl.BlockSpec((ATT_KV, 1, ATT_DH, R), lambda i: (0, i, 0, 0)),
                   pl.BlockSpec((ATT_KV, 1, 1, R), lambda i: (0, i, 0, 0)), vec],
        out_shape=[jax.ShapeDtypeStruct((ATT_KV, T // tm, ATT_DH, R), BF16),
                   jax.ShapeDtypeStruct((ATT_KV, T // tm, 1, R), F32), jax.ShapeDtypeStruct((1, ATT_QW), F32)],
        compiler_params=_params(("arbitrary",)),
    )(dmix, o, w)


def _ffn_up(h2, wg_t, wu_t, *, name, tm=512):
    T = h2.shape[0]
    tn = _pick(D_FF, 1408)
    nt = (((1,), (1,)), ((), ()))

    def body(h_ref, wg_ref, wu_ref, g_ref, u_ref, a_ref):
        hv = h_ref[...]
        g = lax.dot_general(hv, wg_ref[...], nt, preferred_element_type=F32)
        u = lax.dot_general(hv, wu_ref[...], nt, preferred_element_type=F32)
        g_ref[...] = g.astype(BF16)
        u_ref[...] = u.astype(BF16)
        a_ref[...] = (g * _sigmoid(g) * u).astype(BF16)

    wspec = pl.BlockSpec((tn, D_MODEL), lambda i, j: (j, 0))
    ospec = pl.BlockSpec((tm, tn), lambda i, j: (i, j))
    return pl.pallas_call(
        body, name=name, grid=(T // tm, D_FF // tn),
        in_specs=[pl.BlockSpec((tm, D_MODEL), lambda i, j: (i, 0)), wspec, wspec],
        out_specs=[ospec] * 3, out_shape=[jax.ShapeDtypeStruct((T, D_FF), BF16)] * 3,
        compiler_params=_params(("parallel", "arbitrary")),
    )(h2, wg_t, wu_t)


def _ffn_act_bwd(dx2b, w_down, gate, up, *, name, tm=512):
    T = dx2b.shape[0]
    tn = _pick(D_FF, 1408)

    def body(dx_ref, w_ref, g_ref, u_ref, dg_ref, du_ref):
        da = lax.dot_general(dx_ref[...], w_ref[...], (((1,), (1,)), ((), ())), preferred_element_type=F32)
        g = g_ref[...].astype(F32)
        u = u_ref[...].astype(F32)
        sg = _sigmoid(g)
        dg_ref[...] = (da * u * (sg * (1.0 + g * (1.0 - sg)))).astype(BF16)
        du_ref[...] = (da * (g * sg)).astype(BF16)

    ospec = pl.BlockSpec((tm, tn), lambda i, j: (i, j))
    return pl.pallas_call(
        body, name=name, grid=(T // tm, D_FF // tn),
        in_specs=[pl.BlockSpec((tm, D_MODEL), lambda i, j: (i, 0)),
                  pl.BlockSpec((tn, D_MODEL), lambda i, j: (j, 0)), ospec, ospec],
        out_specs=[ospec] * 2, out_shape=[jax.ShapeDtypeStruct((T, D_FF), BF16)] * 2,
        compiler_params=_params(("parallel", "arbitrary")),
    )(dx2b, w_down, gate, up)


def _adam_math(w, g, m, v):
    m = ADAM_B1 * m + (1.0 - ADAM_B1) * g
    v = ADAM_B2 * v + (1.0 - ADAM_B2) * (g * g)
    m_hat = m / (1.0 - ADAM_B1 ** ADAM_STEP)
    v_hat = v / (1.0 - ADAM_B2 ** ADAM_STEP)
    delta = -ADAM_LR * (m_hat / (jnp.sqrt(v_hat) + ADAM_EPS) + ADAM_WD * w)
    return delta, m, v


def _adamw(parts, w, m, v, *, name, tr_cap=256):
    P, R, C = parts.shape
    tr = R
    for t in range(8, min(R, tr_cap) + 1, 8):
        if R % t == 0:
            tr = t

    def body(p_ref, w_ref, m_ref, v_ref, g_ref, d_ref, nm_ref, nv_ref):
        g = p_ref[0].astype(F32)
        for j in range(1, P):
            g = g + p_ref[j].astype(F32)
        d, nm, nv = _adam_math(w_ref[...], g, m_ref[...], v_ref[...])
        g_ref[...] = g
        d_ref[...] = d
        nm_ref[...] = nm
        nv_ref[...] = nv

    blk = pl.BlockSpec((tr, C), lambda i: (i, 0))
    return pl.pallas_call(
        body, name=name, grid=(R // tr,),
        in_specs=[pl.BlockSpec((P, tr, C), lambda i: (0, i, 0)), blk, blk, blk],
        out_specs=[blk] * 4, out_shape=[jax.ShapeDtypeStruct((R, C), F32)] * 4,
        compiler_params=_params(("parallel",)),
    )(parts, w, m, v)


def _all_gather(xs, *, name):
    n = len(xs)

    def body(*refs):
        ins, outs = refs[:n], refs[n:2 * n]
        send_sems, recv_sems, local_sems = refs[2 * n:]
        x, y, c = lax.axis_index("x"), lax.axis_index("y"), lax.axis_index("c")
        me, sibling = (x, y, c), (x, y, 1 - c)
        chips = [(1 - x, y), (x, 1 - y), (1 - x, 1 - y)]

        def slot(p):
            return 4 * p[0] + 2 * p[1] + p[2]

        def copy(a, k, block, to, src=None):
            dst = outs[a].at[slot(block)]
            return pltpu.make_async_remote_copy(
                src_ref=dst if src is None else src, dst_ref=dst,
                send_sem=send_sems.at[a * 7 + k], recv_sem=recv_sems.at[a * 7 + k],
                device_id=to, device_id_type=MESH)

        mine = [pltpu.make_async_copy(ins[a], outs[a].at[slot(me)], local_sems.at[a]) for a in range(n)]
        for cp in mine:
            cp.start()
        first = []
        for a in range(n):
            first.append(copy(a, 0, me, sibling, src=ins[a]))
            first += [copy(a, 1 + j, me, (*chip, c), src=ins[a]) for j, chip in enumerate(chips)]
        for cp in first:
            cp.start()
        passed = []
        for j, chip in enumerate(chips):
            for a in range(n):
                copy(a, 1 + j, (*chip, c), me).wait_recv()
                cp = copy(a, 4 + j, (*chip, c), sibling)
                cp.start()
                passed.append(cp)
        for a in range(n):
            copy(a, 0, sibling, me).wait_recv()
            for j, chip in enumerate(chips):
                copy(a, 4 + j, (*chip, 1 - c), me).wait_recv()
        for cp in first + passed:
            cp.wait_send()
        for cp in mine:
            cp.wait()

    return pl.pallas_call(
        body, name=name,
        in_specs=[ANY] * n, out_specs=[ANY] * n,
        out_shape=[jax.ShapeDtypeStruct((N_DEV,) + x.shape, x.dtype) for x in xs],
        scratch_shapes=[pltpu.SemaphoreType.DMA((7 * n,)), pltpu.SemaphoreType.DMA((7 * n,)),
                        pltpu.SemaphoreType.DMA((n,))],
        compiler_params=pltpu.CompilerParams(has_side_effects=True),
    )(*xs)


ALL_MASKS = [(mx, my, mc) for mx in (0, 1) for my in (0, 1) for mc in (0, 1)][1:]


def _flip(v, bit):
    return 1 - v if bit else v


def _exchange_copies(ins, outs, send_sems, recv_sems, local_sems, *, masks, slot):
    n, n_peers = len(ins), len(masks)
    x, y, c = lax.axis_index("x"), lax.axis_index("y"), lax.axis_index("c")
    my_slot = slot((x, y, c))
    mine = [pltpu.make_async_copy(ins[a].at[my_slot], outs[a].at[my_slot], local_sems.at[a]) for a in range(n)]
    copies = []
    for a in range(n):
        for k, (mx, my, mc) in enumerate(masks):
            peer = (_flip(x, mx), _flip(y, my), _flip(c, mc))
            peer_slot = slot(peer)
            sems = dict(send_sem=send_sems.at[a * n_peers + k], recv_sem=recv_sems.at[a * n_peers + k],
                        device_id=peer, device_id_type=MESH)
            copies.append((
                pltpu.make_async_remote_copy(src_ref=ins[a].at[peer_slot], dst_ref=outs[a].at[my_slot], **sems),
                pltpu.make_async_remote_copy(src_ref=ins[a].at[peer_slot], dst_ref=outs[a].at[peer_slot], **sems)))
    return mine, copies


def _exchange(gs, *, masks, slot, name, bcast=None):
    n, n_peers = len(gs), len(masks)
    has_bcast = bcast is not None

    def body(*refs):
        n_in = n + has_bcast
        ins, outs = refs[:n], refs[n_in:n_in + n]
        send_sems, recv_sems, local_sems = refs[2 * n_in:2 * n_in + 3]
        x, y, c = lax.axis_index("x"), lax.axis_index("y"), lax.axis_index("c")
        mine, copies = _exchange_copies(ins, outs, send_sems, recv_sems, local_sems, masks=masks, slot=slot)
        if has_bcast:
            b_in, b_out = refs[n], refs[2 * n_in - 1]
            b_send, b_recv = refs[2 * n_in + 3:]
            me = 4 * x + 2 * y + c
            mine.append(pltpu.make_async_copy(b_in, b_out.at[me], local_sems.at[n]))
            for k, (mx, my, mc) in enumerate(ALL_MASKS):
                peer = (_flip(x, mx), _flip(y, my), _flip(c, mc))
                peer_id = 4 * peer[0] + 2 * peer[1] + peer[2]
                sems = dict(send_sem=b_send.at[k], recv_sem=b_recv.at[k], device_id=peer, device_id_type=MESH)
                copies.append((pltpu.make_async_remote_copy(src_ref=b_in, dst_ref=b_out.at[me], **sems),
                               pltpu.make_async_remote_copy(src_ref=b_in, dst_ref=b_out.at[peer_id], **sems)))
        for cp in mine:
            cp.start()
        for send, _ in copies:
            send.start()
        for send, recv in copies:
            recv.wait_recv()
            send.wait_send()
        for cp in mine:
            cp.wait()

    n_io = n + has_bcast
    out_shape = [jax.ShapeDtypeStruct(g.shape, g.dtype) for g in gs]
    scratch = [pltpu.SemaphoreType.DMA((n_peers * n,)), pltpu.SemaphoreType.DMA((n_peers * n,)),
               pltpu.SemaphoreType.DMA((n_io,))]
    if has_bcast:
        out_shape.append(jax.ShapeDtypeStruct((N_DEV,) + bcast.shape, bcast.dtype))
        scratch += [pltpu.SemaphoreType.DMA((len(ALL_MASKS),)), pltpu.SemaphoreType.DMA((len(ALL_MASKS),))]
    return pl.pallas_call(
        body, name=name,
        in_specs=[ANY] * n_io, out_specs=[ANY] * n_io, out_shape=out_shape, scratch_shapes=scratch,
        compiler_params=pltpu.CompilerParams(has_side_effects=True),
    )(*gs, *([bcast] if has_bcast else []))


SWAP_ROW_CHUNKS = 4


def _core_swap(gs, *, name):
    n = len(gs)

    def body(*refs):
        ins, outs = refs[:n], refs[n:2 * n]
        send_sems, recv_sems = refs[2 * n:]
        x, y, c = lax.axis_index("x"), lax.axis_index("y"), lax.axis_index("c")
        sibling = (x, y, 1 - c)
        for a in range(n):
            Q, _, R, _ = ins[a].shape
            rows = R // SWAP_ROW_CHUNKS
            for q in range(Q):
                for j in range(SWAP_ROW_CHUNKS):
                    pltpu.make_async_remote_copy(
                        src_ref=ins[a].at[q, 1 - c, pl.ds(j * rows, rows)], dst_ref=outs[a].at[q, pl.ds(j * rows, rows)],
                        send_sem=send_sems.at[a], recv_sem=recv_sems.at[a], device_id=sibling, device_id_type=MESH
                    ).start()
        for a in range(n):
            pltpu.make_async_remote_copy(
                src_ref=outs[a], dst_ref=outs[a], send_sem=send_sems.at[a], recv_sem=recv_sems.at[a],
                device_id=sibling, device_id_type=MESH).wait()

    return pl.pallas_call(
        body, name=name,
        in_specs=[ANY] * n, out_specs=[ANY] * n,
        out_shape=[jax.ShapeDtypeStruct(g.shape[:1] + g.shape[2:], g.dtype) for g in gs],
        scratch_shapes=[pltpu.SemaphoreType.DMA((n,)), pltpu.SemaphoreType.DMA((n,))],
        compiler_params=pltpu.CompilerParams(has_side_effects=True),
    )(*gs)


def _pair_sum(g, other, core, *, name, tr_cap=256):
    Q, _, R, C = g.shape
    tr = max(t for t in range(16, min(R, tr_cap) + 1, 16) if R % t == 0)

    def body(core_ref, g_ref, o_ref, out_ref):
        out_ref[0] = (g_ref[0, 0] + o_ref[0]).astype(BF16)

    return pl.pallas_call(
        body, name=name,
        grid_spec=pltpu.PrefetchScalarGridSpec(
            num_scalar_prefetch=1, grid=(Q, R // tr),
            in_specs=[pl.BlockSpec((1, 1, tr, C), lambda q, i, core_ref: (q, core_ref[0], i, 0)),
                      pl.BlockSpec((1, tr, C), lambda q, i, core_ref: (q, i, 0))],
            out_specs=pl.BlockSpec((1, tr, C), lambda q, i, core_ref: (q, i, 0))),
        out_shape=jax.ShapeDtypeStruct((Q, R, C), BF16),
        compiler_params=_params(("parallel", "parallel")),
    )(core, g, other)


def _pack_small(norm1, norm2, final, att, hg, qn, kn, lb=None, loss=None):
    z = lambda n: jnp.zeros((n,), F32)
    rows = [norm1.reshape(-1), norm2.reshape(-1), final.reshape(-1),
            jnp.concatenate([att.reshape(-1), z(512)]),
            jnp.concatenate([hg.reshape(-1), qn.reshape(-1), kn.reshape(-1), z(1024 - 256)]),
            z(1024) if lb is None else lb.reshape(-1),
            z(1024) if loss is None else jnp.concatenate([loss.reshape(-1), z(1023)]), z(1024)]
    return jnp.stack(rows, axis=0)


def _unpack_small(p):
    return (p[0:1, :], p[1:2, :], p[2, :], p[3:4, 0:512], p[4:5, 0:128], p[4:5, 128:192], p[4:5, 192:256])


def _fold_heads(dhg, dqn, dkn, *, name):
    def body(hg_ref, q_ref, k_ref, ohg_ref, oq_ref, ok_ref):
        def fold128(v):
            acc = v[:, 0:LANES]
            for j in range(1, v.shape[1] // LANES):
                acc = acc + v[:, j * LANES:(j + 1) * LANES]
            return acc

        ohg_ref[...] = fold128(hg_ref[...])
        q = fold128(q_ref[...])
        oq_ref[...] = q + pltpu.roll(q, ATT_DH, 1)
        k = k_ref[...]
        ok_ref[...] = k + pltpu.roll(k, ATT_DH, 1)

    return pl.pallas_call(body, name=name, out_shape=[jax.ShapeDtypeStruct((1, LANES), F32)] * 3)(dhg, dqn, dkn)


def _lb_grad(dlb_sum, lb, *, name):
    def body(d_ref, lb_ref, o_ref):
        lbv = lb_ref[...]
        gl = d_ref[...] * lbv * (1.0 - lbv)
        o_ref[0:1, :] = gl[0:1, :]
        o_ref[1:2, :] = -gl[0:1, :]
        o_ref[2:3, :] = gl[1:2, :]
        o_ref[3:4, :] = -gl[1:2, :]

    return pl.pallas_call(body, name=name, out_shape=jax.ShapeDtypeStruct((4, HG_W), F32))(dlb_sum, lb)


def _lower_bounds(lb_logits_full, *, name):
    def body(l_ref, o_ref):
        for d in range(2):
            l0, l1 = l_ref[2 * d:2 * d + 1, :], l_ref[2 * d + 1:2 * d + 2, :]
            mx = jnp.maximum(l0, l1)
            e0, e1 = jnp.exp(l0 - mx), jnp.exp(l1 - mx)
            o_ref[d:d + 1, :] = e0 / (e0 + e1)

    return pl.pallas_call(body, name=name, out_shape=jax.ShapeDtypeStruct((2, HG_W), F32))(
        lb_logits_full.reshape(4, HG_W))


def _local_step(x, target, norm1_w, w_in_t, lb, hg_norm_w, q_norm_w, k_norm_w, att_norm_w, w_out, norm2_w,
                w_g_t, w_u_t, w_down, final_norm_w, reduce_early=None):
    T = x.shape[0]
    cos, sin = _rope_tables(T)
    qw8 = jnp.tile(q_norm_w, (1, ATT_HEADS))
    kw2 = jnp.tile(k_norm_w, (1, ATT_KV))

    h, r1 = _rms_fwd(x, norm1_w, name="norm1_fwd")
    U = _mm_nn([(h, w_in_t)], trans_b=True, name="in_proj")
    o_f, st_f = _gla_fwd(U, lb[0:1], f_block=1, reverse=False, name="gla_fwd_f")
    o_b, st_b = _gla_fwd(U, lb[1:2], f_block=2, reverse=True, name="gla_fwd_b")
    mix_hg = _hg_post_fwd(o_f, o_b, U, hg_norm_w, name="hg_post_fwd")
    q_c, qn_c, kmax2, k_c, v_c = _att_prep_fwd(U, cos, sin, qw8, kw2, name="att_prep_fwd")
    kmax = jnp.sqrt(jnp.max(kmax2.reshape(ATT_KV, ATT_DH), axis=1))
    m_c = qn_c * (kmax * 1.001).reshape(ATT_KV, 1, 1, 1)
    o_c, lse = lax.cond(jnp.max(m_c) <= FA_BOUND_MAX,
                        lambda: _flash_fwd_bounded(q_c, k_c, v_c, m_c, name="flash_fwd_bounded"),
                        lambda: _flash_fwd(q_c, k_c, v_c, name="flash_fwd"))
    o_att, mix_att = _att_post_fwd(o_c, att_norm_w, name="att_post_fwd")
    x1, h2, r2 = _mm_nn([(mix_hg, w_out[:HG_W]), (mix_att, w_out[HG_W:])], residual=x, tail=_tail_rms_fwd(norm2_w),
                        name="out_proj")
    gate, up, act = _ffn_up(h2, w_g_t, w_u_t, name="ffn_up")
    loss, dx2, dx2b, d_final = _mm_nn([(act, w_down)], residual=x1,
                                      tail=_tail_loss(target, final_norm_w.reshape(1, D_MODEL)), name="ffn_down")

    d_gate, d_up = _ffn_act_bwd(dx2b, w_down, gate, up, name="ffn_act_bwd")
    dw_down = _mm_tn(act, dx2b, tma_cap=1408, name="dw_down")
    dx1, dx1b, d_norm2 = _mm_nn([(d_gate, w_g_t), (d_up, w_u_t)], tm=256,
                                tail=_tail_rms_bwd(x1, r2, norm2_w, dx2, emit_bf16=True), name="ffn_up_bwd")
    dw_g = _mm_tn(d_gate, h2, tma_cap=1408, name="dw_gate")
    dw_u = _mm_tn(d_up, h2, tma_cap=1408, name="dw_up")
    dmix = _mm_nn([(dx1b, w_out)], trans_b=True, name="out_proj_bwd")
    dw_out = jnp.concatenate([_mm_tn(mix_hg, dx1b, name="dw_out_hg"), _mm_tn(mix_att, dx1b, name="dw_out_att")], axis=0)
    do_c, delta, d_att = _att_post_bwd(dmix, o_att, att_norm_w, name="att_post_bwd")
    ride = None if reduce_early is None else reduce_early(dw_out, dw_g, dw_u, dw_down)
    dq_c, dk_c, dv_c, *rode = _flash_bwd(q_c, k_c, v_c, do_c, lse, delta, ride=ride, name="flash_bwd")
    dU_att, d_qn, d_kn = _att_prep_bwd(U, dq_c, dk_c, dv_c, cos, sin, qw8, kw2, name="att_prep_bwd")
    do_hg, du_g, d_hg = _hg_post_bwd(dmix, o_f, o_b, U, hg_norm_w, name="hg_post_bwd")
    dq_f, dz_f, dv_f, dlb_f = _gla_bwd(U, lb[0:1], do_hg, st_f, f_block=1, reverse=False, name="gla_bwd_f")
    dU_hg, dlb_b = _gla_bwd(U, lb[1:2], do_hg, st_b, f_block=2, reverse=True, prev=(dq_f, dz_f, dv_f, du_g),
                            name="gla_bwd_b")
    w_hg = 5 * HG_W
    grad_x, d_norm1 = _mm_nn([(dU_hg, w_in_t[:w_hg]), (dU_att, w_in_t[w_hg:])],
                             tail=_tail_rms_bwd(x, r1, norm1_w, dx1, emit_bf16=False), name="in_proj_bwd")
    dw_in = jnp.concatenate([_mm_tn(dU_hg, h, tma_cap=1280, name="dw_in_hg"), _mm_tn(dU_att, h, name="dw_in_att")],
                            axis=0)
    d_hg, d_qn, d_kn = _fold_heads(d_hg, d_qn, d_kn, name="fold_heads")

    big = dict(w_in=dw_in, w_out=dw_out, w_g=dw_g, w_u=dw_u, w_down=dw_down)
    small = dict(norm1=d_norm1, norm2=d_norm2, final=d_final, att=d_att, hg=d_hg,
                 qn=d_qn[:, :ATT_DH], kn=d_kn[:, :ATT_DH], lb=jnp.concatenate([dlb_f, dlb_b], axis=0))
    return loss, grad_x, big, small, rode


def kernel(x, norm1_w, w_in, lb_logits, hg_norm_w, q_norm_w, k_norm_w, att_norm_w, w_out, norm2_w, w_gate_up, w_down, final_norm_w, loss_target, m_norm1_w, m_w_in, m_lb_logits, m_hg_norm_w, m_q_norm_w, m_k_norm_w, m_att_norm_w, m_w_out, m_norm2_w, m_w_gate_up, m_w_down, m_final_norm_w, v_norm1_w, v_w_in, v_lb_logits, v_hg_norm_w, v_q_norm_w, v_k_norm_w, v_att_norm_w, v_w_out, v_norm2_w, v_w_gate_up, v_w_down, v_final_norm_w):
    T = x.shape[1]
    me = 4 * lax.axis_index("x") + 2 * lax.axis_index("y") + lax.axis_index("c")
    c_in, r_out, c_gu, r_dn = w_in.shape[2], w_out.shape[1], w_gate_up.shape[2], w_down.shape[1]
    lb_cols = lb_logits.shape[2]

    g_in, g_out, g_gu, g_dn, g_lb = _all_gather(
        [w_in[0].T.astype(BF16), w_out[0].astype(BF16), w_gate_up[0].T.astype(BF16), w_down[0].astype(BF16),
         lb_logits.reshape(4, lb_cols)], name="gather_weights")
    w_in_t = g_in.reshape(N_DEV * c_in, D_MODEL)
    w_out_f = g_out.reshape(N_DEV * r_out, D_MODEL)
    g_gu = g_gu.reshape(2, (N_DEV // 2) * c_gu, D_MODEL)
    w_g_t, w_u_t = g_gu[0], g_gu[1]
    w_dn_f = g_dn.reshape(N_DEV * r_dn, D_MODEL)
    lb_logits_f = g_lb.transpose(1, 0, 2).reshape(2, 2, N_DEV * lb_cols)
    lb = _lower_bounds(lb_logits_f, name="lower_bounds")

    chips = N_DEV // 2
    core = lax.axis_index("c").astype(jnp.int32).reshape(1)
    by_owner = lambda g, r: g.reshape(chips, 2, r, D_MODEL)

    def chip_sums(mine, names, call):
        theirs = _core_swap(mine, name=call)
        return [_pair_sum(g, o, core, name="pair_sum_" + nm) for g, o, nm in zip(mine, theirs, names)]

    def reduce_early(dw_out, dw_g_t, dw_u_t, dw_down):
        return chip_sums([by_owner(dw_out, r_out), by_owner(jnp.concatenate([dw_g_t, dw_u_t], axis=0), c_gu),
                          by_owner(dw_down, r_dn)], ("w_out", "w_gu", "w_down"), "exchange_cores_early")

    loss, grad_x, big, small, (p_out, p_gu, p_dn) = _local_step(
        x[0], loss_target[0], norm1_w, w_in_t, lb, hg_norm_w, q_norm_w, k_norm_w, att_norm_w, w_out_f, norm2_w,
        w_g_t, w_u_t, w_dn_f, final_norm_w, reduce_early=reduce_early)
    p_gu = p_gu.transpose(0, 2, 1)

    packed = _pack_small(small["norm1"], small["norm2"], small["final"], small["att"], small["hg"],
                         small["qn"], small["kn"], small["lb"], loss)
    p_in, all_small = _exchange(chip_sums([by_owner(big["w_in"], c_in)], ("w_in",), "exchange_cores"),
                                masks=CHIP_MASKS, slot=_chip_slot, bcast=packed, name="exchange_chips")
    p_in = p_in.transpose(0, 2, 1)

    g_w_in, d_w_in, nm_w_in, nv_w_in = _adamw(p_in, w_in[0], m_w_in[0], v_w_in[0], name="adamw_w_in")
    g_w_out, d_w_out, nm_w_out, nv_w_out = _adamw(p_out, w_out[0], m_w_out[0], v_w_out[0], name="adamw_w_out")
    g_w_gu, d_w_gu, nm_w_gu, nv_w_gu = _adamw(p_gu, w_gate_up[0], m_w_gate_up[0], v_w_gate_up[0], name="adamw_w_gu")
    g_w_dn, d_w_dn, nm_w_dn, nv_w_dn = _adamw(p_dn, w_down[0], m_w_down[0], v_w_down[0], name="adamw_w_down")

    pk = lambda vecs: _pack_small(*vecs)
    w_pk = pk([norm1_w, norm2_w, final_norm_w, att_norm_w, hg_norm_w, q_norm_w, k_norm_w])
    m_pk = pk([m_norm1_w, m_norm2_w, m_final_norm_w, m_att_norm_w, m_hg_norm_w, m_q_norm_w, m_k_norm_w])
    v_pk = pk([v_norm1_w, v_norm2_w, v_final_norm_w, v_att_norm_w, v_hg_norm_w, v_q_norm_w, v_k_norm_w])
    g_pk, d_pk, nm_pk, nv_pk = _adamw(all_small, w_pk, m_pk, v_pk, name="adamw_small")

    dlb_sum = g_pk[5:6, :].reshape(2, HG_W)
    g_lb_full = _lb_grad(dlb_sum, lb, name="lb_grad")
    g_lb_mine = lax.dynamic_slice_in_dim(g_lb_full, me * lb_cols, lb_cols, axis=1)
    g_lb_s, d_lb, nm_lb, nv_lb = _adamw(g_lb_mine[None], lb_logits.reshape(4, lb_cols),
                                        m_lb_logits.reshape(4, lb_cols), v_lb_logits.reshape(4, lb_cols),
                                        name="adamw_lb")

    loss_total = g_pk[6, 0]

    def outs(big4, lb_arr, pk_arr):
        n1, n2, fin, att, hg, qn, kn = _unpack_small(pk_arr)
        b_in, b_out, b_gu, b_dn = big4
        return [n1, b_in[None], lb_arr.reshape(2, 2, lb_cols), hg, qn, kn, att, b_out[None], n2, b_gu[None],
                b_dn[None], fin]

    return (loss_total, grad_x[None],
            *outs((g_w_in, g_w_out, g_w_gu, g_w_dn), g_lb_s, g_pk),
            *outs((d_w_in, d_w_out, d_w_gu, d_w_dn), d_lb, d_pk),
            *outs((nm_w_in, nm_w_out, nm_w_gu, nm_w_dn), nm_lb, nm_pk),
            *outs((nv_w_in, nv_w_out, nv_w_gu, nv_w_dn), nv_lb, nv_pk))
```

```python
import math

import jax
import jax.numpy as jnp
import numpy as np
from jax import lax
from jax.experimental import pallas as pl
from jax.experimental.pallas import tpu as pltpu

F32 = jnp.float32
BF16 = jnp.bfloat16

N_DEV = 8
D_MODEL = 1024
EPS = 1e-6
HG_HEADS = 4
HG_D = 128
HG_W = HG_HEADS * HG_D
CHUNK = 64
ATT_HEADS = 8
ATT_KV = 2
ATT_G = ATT_HEADS // ATT_KV
ATT_DH = 64
ATT_QW = ATT_HEADS * ATT_DH
ATT_KW = ATT_KV * ATT_DH
GRID_W = 64
ROPE_THETA = 10000.0
D_FF = 2816
ADAM_LR, ADAM_B1, ADAM_B2, ADAM_EPS, ADAM_WD, ADAM_STEP = 0.001, 0.9, 0.999, 1e-08, 0.01, 10

LANES = 128
VMEM_LIMIT = 48 * 1024 * 1024
MESH = pl.DeviceIdType.MESH
ANY = pl.BlockSpec(memory_space=pl.ANY)


def _params(sem=None):
    return pltpu.CompilerParams(dimension_semantics=sem, vmem_limit_bytes=VMEM_LIMIT)


def _pick(n, cap):
    best = None
    for t in range(LANES, cap + 1, LANES):
        if n % t == 0:
            best = t
    assert best is not None, (n, cap)
    return best


def _sigmoid(x):
    return 1.0 / (1.0 + jnp.exp(-x))


def _dot(a, b):
    return jnp.dot(a.astype(BF16), b.astype(BF16), preferred_element_type=F32)


def _dot_nt(a, b):
    return lax.dot_general(a.astype(BF16), b.astype(BF16), (((1,), (1,)), ((), ())),
                           preferred_element_type=F32)


def _dot_tn(a, b):
    return lax.dot_general(a.astype(BF16), b.astype(BF16), (((0,), (0,)), ((), ())),
                           preferred_element_type=F32)


def _mm_nn(pairs, *, name, out_dtype=F32, residual=None, tm=512, tn_cap=None, trans_b=False, tail=None, gather=None):
    M = pairs[0][0].shape[0]
    N = pairs[0][1].shape[0 if trans_b else 1]
    tn = N if tn_cap is None else _pick(N, tn_cap)
    n_pairs = len(pairs)
    has_res = residual is not None
    dims = (((1,), (1,)), ((), ())) if trans_b else (((1,), (0,)), ((), ()))
    assert tail is None or tn == N
    n_g = 0 if gather is None else len(gather)
    assert n_g == 0 or (tail is None and tn == N)
    n_main = 2 * n_pairs + has_res

    def body(*refs):
        if n_g:
            start, finish = _gather_halves(refs[n_main:n_main + n_g], refs[n_main + n_g + 1:n_main + 2 * n_g + 1],
                                           *refs[n_main + 2 * n_g + 1:])
            pl.when(pl.program_id(0) == 0)(start)
        acc = None
        for i in range(n_pairs):
            d = lax.dot_general(refs[2 * i][...], refs[2 * i + 1][...], dims, preferred_element_type=F32)
            acc = d if acc is None else acc + d
        if has_res:
            acc = acc + refs[2 * n_pairs][...]
        if tail is None:
            refs[n_main + n_g][...] = acc.astype(out_dtype)
        else:
            tail["fn"](acc, pl.program_id(0) == 0, *refs[n_main:])
        if n_g:
            pl.when(pl.program_id(0) == M // tm - 1)(finish)

    kinds = {"row": ((tm, N), (M, N), lambda i, j: (i, 0)), "col": ((tm, 1), (M, 1), lambda i, j: (i, 0)),
             "vec": ((1, N), (1, N), lambda i, j: (0, 0)), "one": ((1, 1), (1, 1), lambda i, j: (0, 0))}
    in_specs, args = [], []
    for a, b in pairs:
        k = a.shape[1]
        b_spec = pl.BlockSpec((tn, k), lambda i, j: (j, 0)) if trans_b else pl.BlockSpec((k, tn), lambda i, j: (0, j))
        in_specs += [pl.BlockSpec((tm, k), lambda i, j: (i, 0)), b_spec]
        args += [a, b]
    if has_res:
        in_specs.append(pl.BlockSpec((tm, tn), lambda i, j: (i, j)))
        args.append(residual)
    if tail is None:
        out_specs = pl.BlockSpec((tm, tn), lambda i, j: (i, j))
        out_shape = jax.ShapeDtypeStruct((M, N), out_dtype)
    else:
        for arr, kind in tail["ins"]:
            in_specs.append(pl.BlockSpec(kinds[kind][0], kinds[kind][2]))
            args.append(arr)
        out_specs = [pl.BlockSpec(kinds[kind][0], kinds[kind][2]) for _, kind in tail["outs"]]
        out_shape = [jax.ShapeDtypeStruct(kinds[kind][1], dt) for dt, kind in tail["outs"]]
    if n_g:
        return pl.pallas_call(
            body, name=name, grid=(M // tm, N // tn), in_specs=in_specs + [ANY] * n_g,
            out_specs=[out_specs] + [ANY] * n_g, out_shape=[out_shape] + _gathered_shapes(gather),
            scratch_shapes=_gather_scratch(n_g),
            compiler_params=pltpu.CompilerParams(dimension_semantics=("arbitrary", "arbitrary"),
                                                 vmem_limit_bytes=VMEM_LIMIT, has_side_effects=True),
        )(*args, *gather)
    return pl.pallas_call(
        body, name=name, grid=(M // tm, N // tn), in_specs=in_specs, out_specs=out_specs, out_shape=out_shape,
        compiler_params=_params(("parallel" if tail is None else "arbitrary", "arbitrary")),
    )(*args)


def _mm_tn(a, b, *, name, tma_cap=1024, tnb_cap=1024, tk=1024):
    T, Ma = a.shape
    Nb = b.shape[1]
    tma, tnb = _pick(Ma, tma_cap), _pick(Nb, tnb_cap)
    tk = min(tk, T)
    n_k = T // tk

    def body(a_ref, b_ref, o_ref, acc_ref):
        k = pl.program_id(2)

        @pl.when(k == 0)
        def _():
            acc_ref[...] = jnp.zeros_like(acc_ref)

        acc_ref[...] += lax.dot_general(a_ref[...], b_ref[...], (((0,), (0,)), ((), ())),
                                        preferred_element_type=F32)

        @pl.when(k == n_k - 1)
        def _():
            o_ref[...] = acc_ref[...]

    return pl.pallas_call(
        body, name=name, grid=(Ma // tma, Nb // tnb, n_k),
        in_specs=[pl.BlockSpec((tk, tma), lambda i, j, k: (k, i)), pl.BlockSpec((tk, tnb), lambda i, j, k: (k, j))],
        out_specs=pl.BlockSpec((tma, tnb), lambda i, j, k: (i, j)),
        out_shape=jax.ShapeDtypeStruct((Ma, Nb), F32),
        scratch_shapes=[pltpu.VMEM((tma, tnb), F32)],
        compiler_params=_params(("parallel", "parallel", "arbitrary")),
    )(a, b)


def _rms_fwd(x, w, *, name, tm=512):
    T, Dm = x.shape

    def body(x_ref, w_ref, h_ref, r_ref):
        xv = x_ref[...]
        r = lax.rsqrt(jnp.mean(xv * xv, axis=-1, keepdims=True) + EPS)
        h_ref[...] = (xv * r * w_ref[...]).astype(BF16)
        r_ref[...] = r

    return pl.pallas_call(
        body, name=name, grid=(T // tm,),
        in_specs=[pl.BlockSpec((tm, Dm), lambda i: (i, 0)), pl.BlockSpec((1, Dm), lambda i: (0, 0))],
        out_specs=[pl.BlockSpec((tm, Dm), lambda i: (i, 0)), pl.BlockSpec((tm, 1), lambda i: (i, 0))],
        out_shape=[jax.ShapeDtypeStruct((T, Dm), BF16), jax.ShapeDtypeStruct((T, 1), F32)],
        compiler_params=_params(("parallel",)),
    )(x, w)


def _tail_rms_fwd(w):
    def fn(xv, first, w_ref, x_ref, h_ref, r_ref):
        r = lax.rsqrt(jnp.mean(xv * xv, axis=-1, keepdims=True) + EPS)
        x_ref[...] = xv
        h_ref[...] = (xv * r * w_ref[...]).astype(BF16)
        r_ref[...] = r

    return dict(fn=fn, ins=[(w, "vec")], outs=[(F32, "row"), (BF16, "row"), (F32, "col")])


def _tail_rms_bwd(x, r, w, dres, *, emit_bf16):
    def fn(dhv, first, x_ref, r_ref, w_ref, dres_ref, *outs):
        dx_ref, dw_ref = outs[0], outs[-1]

        @pl.when(first)
        def _():
            dw_ref[...] = jnp.zeros_like(dw_ref)

        rv = r_ref[...]
        xh = x_ref[...] * rv
        dxh = dhv * w_ref[...]
        t = jnp.mean(dxh * xh, axis=-1, keepdims=True)
        dx = dres_ref[...] + rv * (dxh - xh * t)
        dx_ref[...] = dx
        if emit_bf16:
            outs[1][...] = dx.astype(BF16)
        dw_ref[...] += jnp.sum(dhv * xh, axis=0, keepdims=True)

    outs = [(F32, "row")] + ([(BF16, "row")] if emit_bf16 else []) + [(F32, "vec")]
    return dict(fn=fn, ins=[(x, "row"), (r, "col"), (w, "vec"), (dres, "row")], outs=outs)


def _tail_loss(target, w):
    def fn(xv, first, t_ref, w_ref, loss_ref, dx_ref, dxb_ref, dw_ref):
        @pl.when(first)
        def _():
            loss_ref[...] = jnp.zeros_like(loss_ref)
            dw_ref[...] = jnp.zeros_like(dw_ref)

        r = lax.rsqrt(jnp.mean(xv * xv, axis=-1, keepdims=True) + EPS)
        xh = xv * r
        wv = w_ref[...]
        err = xh * wv - t_ref[...]
        row_loss = jnp.mean(err * err, axis=-1, keepdims=True)
        loss_ref[...] += 0.5 * jnp.sum(row_loss, axis=0, keepdims=True)
        dy = err * (1.0 / xv.shape[-1])
        dxh = dy * wv
        t = jnp.mean(dxh * xh, axis=-1, keepdims=True)
        dx = r * (dxh - xh * t)
        dx_ref[...] = dx
        dxb_ref[...] = dx.astype(BF16)
        dw_ref[...] += jnp.sum(dy * xh, axis=0, keepdims=True)

    return dict(fn=fn, ins=[(target, "row"), (w, "vec")],
                outs=[(F32, "one"), (F32, "row"), (BF16, "row"), (F32, "vec")])


GLA_TB = 512
GLA_NC = GLA_TB // CHUNK
GLA_UNROLL = 4


def _cumsum_rows(x, row, reverse):
    n = x.shape[0]
    s = 1
    while s < n:
        if not reverse:
            x = x + jnp.where(row >= s, pltpu.roll(x, s, 0), 0.0)
        else:
            x = x + jnp.where(row < n - s, pltpu.roll(x, n - s, 0), 0.0)
        s *= 2
    return x


def _gla_gates(uq, z, lbv):
    q = uq * _sigmoid(uq)
    sg = _sigmoid(z)
    sgn = _sigmoid(-z)
    f = lbv + (1.0 - lbv) * sg
    k = (1.0 - lbv) * sgn
    return q, sg, sgn, f, k


def _gla_decays(f, row, reverse):
    b = _cumsum_rows(jnp.log(f), row, reverse)
    if not reverse:
        bref, blast = b[CHUNK // 2 - 1:CHUNK // 2, :], b[CHUNK - 1:CHUNK, :]
    else:
        bref, blast = b[CHUNK // 2:CHUNK // 2 + 1, :], b[0:1, :]
    return b, bref, blast


def _gla_fwd(U, lb, *, f_block, reverse, name, gather=None):
    T = U.shape[0]
    nb = T // GLA_TB
    n_g = 0 if gather is None else len(gather)

    def body(uq_ref, uf_ref, ui_ref, lb_ref, *rest):
        g_in, rest = rest[:n_g], rest[n_g:]
        o_ref, st_ref = rest[:2]
        g_out, rest = rest[2:2 + n_g], rest[2 + n_g:]
        s_ref = rest[0]
        if n_g:
            start, finish = _gather_halves(g_in, g_out, *rest[1:])
            pl.when(pl.program_id(0) == 0)(start)

        @pl.when(pl.program_id(0) == 0)
        def _():
            s_ref[...] = jnp.zeros_like(s_ref)

        row = lax.broadcasted_iota(jnp.int32, (CHUNK, HG_D), 0)
        ri = lax.broadcasted_iota(jnp.int32, (CHUNK, CHUNK), 0)
        ci = lax.broadcasted_iota(jnp.int32, (CHUNK, CHUNK), 1)
        mask = (ri <= ci) if reverse else (ri >= ci)

        def chunk(j, carry):
            c = (GLA_NC - 1 - j) if reverse else j
            rows = pl.ds(pl.multiple_of(c * CHUNK, CHUNK), CHUNK)
            for h in range(HG_HEADS):
                cols = pl.ds(h * HG_D, HG_D)
                v = ui_ref[rows, cols]
                q, _, _, f, k = _gla_gates(uq_ref[rows, cols], uf_ref[rows, cols], lb_ref[:, cols])
                b, bref, blast = _gla_decays(f, row, reverse)
                s = jnp.where(mask, _dot_nt(q * jnp.exp(b - bref), k * jnp.exp(bref - b)), 0.0)
                st = s_ref[h]
                st_ref[c, h] = st
                o_ref[rows, cols] = _dot(s, v) + _dot_nt(q * jnp.exp(b), st)
                s_ref[h] = st * jnp.exp(blast) + _dot_tn(v, k * jnp.exp(blast - b))
            return carry

        lax.fori_loop(0, GLA_NC, chunk, 0, unroll=GLA_NC)
        if n_g:
            pl.when(pl.program_id(0) == nb - 1)(finish)

    blk = (lambda i: nb - 1 - i) if reverse else (lambda i: i)
    ucol = lambda cb: pl.BlockSpec((GLA_TB, HG_W), lambda i: (blk(i), cb))
    gather = [] if gather is None else list(gather)
    return pl.pallas_call(
        body, name=name, grid=(nb,),
        in_specs=[ucol(0), ucol(f_block), ucol(3), pl.BlockSpec((1, HG_W), lambda i: (0, 0))] + [ANY] * n_g,
        out_specs=[pl.BlockSpec((GLA_TB, HG_W), lambda i: (blk(i), 0)),
                   pl.BlockSpec((GLA_NC, HG_HEADS, HG_D, HG_D), lambda i: (blk(i), 0, 0, 0))] + [ANY] * n_g,
        out_shape=[jax.ShapeDtypeStruct((T, HG_W), F32),
                   jax.ShapeDtypeStruct((T // CHUNK, HG_HEADS, HG_D, HG_D), F32)] + _gathered_shapes(gather),
        scratch_shapes=[pltpu.VMEM((HG_HEADS, HG_D, HG_D), F32)] + (_gather_scratch(n_g) if n_g else []),
        compiler_params=pltpu.CompilerParams(dimension_semantics=("arbitrary",), vmem_limit_bytes=VMEM_LIMIT,
                                             has_side_effects=bool(n_g)),
    )(U, U, U, lb, *gather)


def _gla_bwd(U, lb, do, states, *, f_block, reverse, name, prev=None):
    T = U.shape[0]
    nb = T // GLA_TB
    final = prev is not None

    def body(uq_ref, uf_ref, ui_ref, lb_ref, do_ref, st_ref, *rest):
        if final:
            dqp_ref, dzp_ref, dvp_ref, dug_ref, out_ref, dlb_ref, ds_ref = rest
        else:
            dq_ref, dz_ref, dv_ref, dlb_ref, ds_ref = rest

        @pl.when(pl.program_id(0) == 0)
        def _():
            ds_ref[...] = jnp.zeros_like(ds_ref)
            dlb_ref[...] = jnp.zeros_like(dlb_ref)

        row = lax.broadcasted_iota(jnp.int32, (CHUNK, HG_D), 0)
        ri = lax.broadcasted_iota(jnp.int32, (CHUNK, CHUNK), 0)
        ci = lax.broadcasted_iota(jnp.int32, (CHUNK, CHUNK), 1)
        mask = (ri <= ci) if reverse else (ri >= ci)

        def chunk(j, carry):
            c = j if reverse else (GLA_NC - 1 - j)
            rows = pl.ds(pl.multiple_of(c * CHUNK, CHUNK), CHUNK)
            for h in range(HG_HEADS):
                cols = pl.ds(h * HG_D, HG_D)
                v = ui_ref[rows, cols]
                lbv = lb_ref[:, cols]
                uq = uq_ref[rows, cols]
                q, sg, sgn, f, k = _gla_gates(uq, uf_ref[rows, cols], lbv)
                b, bref, blast = _gla_decays(f, row, reverse)
                eq, ek, eb, el, dec = (jnp.exp(b - bref), jnp.exp(bref - b), jnp.exp(b), jnp.exp(blast - b),
                                       jnp.exp(blast))
                qin, kin, qb, klast = q * eq, k * ek, q * eb, k * el
                dov = do_ref[rows, cols]
                st = st_ref[c, h]
                dst = ds_ref[h]
                p = jnp.where(mask, _dot_nt(qin, kin), 0.0)
                dp = jnp.where(mask, _dot_nt(dov, v), 0.0)
                dqin = _dot(dp, kin)
                dkin = _dot_tn(dp, qin)
                dv = _dot_tn(p, dov) + _dot_nt(klast, dst)
                dqb = _dot(dov, st)
                dklast = _dot(v, dst)
                ds_ref[h] = _dot_tn(dov, qb) + dst * dec
                db = dqin * qin - dkin * kin + dqb * qb - dklast * klast
                extra = (jnp.sum(dklast * klast, axis=0, keepdims=True)
                         + dec * jnp.sum(st * dst, axis=0, keepdims=True))
                dg = _cumsum_rows(db, row, not reverse) + extra
                dq = dqin * eq + dqb * eb
                dk = dkin * ek + dklast * el
                dfk = dg / f - dk
                dz = (dfk * (1.0 - lbv) * sg * sgn).astype(BF16)
                dlb_ref[:, cols] += jnp.sum(dfk * sgn, axis=0, keepdims=True)
                if final:
                    sq = _sigmoid(uq)
                    col = lambda blk: pl.ds(blk * HG_W + h * HG_D, HG_D)
                    out_ref[rows, col(0)] = ((dq + dqp_ref[rows, cols]) * (sq * (1.0 + uq * (1.0 - sq)))).astype(BF16)
                    out_ref[rows, col(1)] = dzp_ref[rows, cols]
                    out_ref[rows, col(2)] = dz
                    out_ref[rows, col(3)] = (dv + dvp_ref[rows, cols]).astype(BF16)
                    out_ref[rows, col(4)] = dug_ref[rows, cols]
                else:
                    dq_ref[rows, cols] = dq
                    dz_ref[rows, cols] = dz
                    dv_ref[rows, cols] = dv
            return carry

        lax.fori_loop(0, GLA_NC, chunk, 0, unroll=GLA_UNROLL)

    blk = (lambda i: i) if reverse else (lambda i: nb - 1 - i)
    ucol = lambda cb: pl.BlockSpec((GLA_TB, HG_W), lambda i: (blk(i), cb))
    tok = pl.BlockSpec((GLA_TB, HG_W), lambda i: (blk(i), 0))
    vec = pl.BlockSpec((1, HG_W), lambda i: (0, 0))
    in_specs = [ucol(0), ucol(f_block), ucol(3), vec, tok,
                pl.BlockSpec((GLA_NC, HG_HEADS, HG_D, HG_D), lambda i: (blk(i), 0, 0, 0))]
    vec_shape = jax.ShapeDtypeStruct((1, HG_W), F32)
    if final:
        in_specs += [tok] * 4
        out_specs = [pl.BlockSpec((GLA_TB, 5 * HG_W), lambda i: (blk(i), 0)), vec]
        out_shape = [jax.ShapeDtypeStruct((T, 5 * HG_W), BF16), vec_shape]
    else:
        out_specs = [tok, tok, tok, vec]
        out_shape = [jax.ShapeDtypeStruct((T, HG_W), F32), jax.ShapeDtypeStruct((T, HG_W), BF16),
                     jax.ShapeDtypeStruct((T, HG_W), F32), vec_shape]
    return pl.pallas_call(
        body, name=name, grid=(nb,), in_specs=in_specs, out_specs=out_specs, out_shape=out_shape,
        scratch_shapes=[pltpu.VMEM((HG_HEADS, HG_D, HG_D), F32)],
        compiler_params=_params(("arbitrary",)),
    )(U, U, U, lb, do, states, *(prev if final else ()))


def _hg_post_fwd(o_f, o_b, U, w, *, name, tm=512):
    T = o_f.shape[0]

    def body(of_ref, ob_ref, ug_ref, w_ref, out_ref):
        wv = w_ref[...]
        for h in range(HG_HEADS):
            cols = pl.ds(h * HG_D, HG_D)
            o = of_ref[:, cols] + ob_ref[:, cols]
            r = lax.rsqrt(jnp.mean(o * o, axis=-1, keepdims=True) + EPS)
            ug = ug_ref[:, cols]
            out_ref[:, cols] = (o * r * wv * (ug * _sigmoid(ug))).astype(BF16)

    tok = pl.BlockSpec((tm, HG_W), lambda i: (i, 0))
    return pl.pallas_call(
        body, name=name, grid=(T // tm,),
        in_specs=[tok, tok, pl.BlockSpec((tm, HG_W), lambda i: (i, 4)), pl.BlockSpec((1, HG_D), lambda i: (0, 0))],
        out_specs=tok, out_shape=jax.ShapeDtypeStruct((T, HG_W), BF16),
        compiler_params=_params(("parallel",)),
    )(o_f, o_b, U, w)


def _hg_post_bwd(dmix, o_f, o_b, U, w, *, name, tm=512):
    T = o_f.shape[0]

    def body(dm_ref, of_ref, ob_ref, ug_ref, w_ref, do_ref, dug_ref, dw_ref):
        @pl.when(pl.program_id(0) == 0)
        def _():
            dw_ref[...] = jnp.zeros_like(dw_ref)

        wv = w_ref[...]
        for h in range(HG_HEADS):
            cols = pl.ds(h * HG_D, HG_D)
            o = of_ref[:, cols] + ob_ref[:, cols]
            r = lax.rsqrt(jnp.mean(o * o, axis=-1, keepdims=True) + EPS)
            xh = o * r
            ug = ug_ref[:, cols]
            sg = _sigmoid(ug)
            dm = dm_ref[:, cols]
            dn = dm * (ug * sg)
            dug_ref[:, cols] = (dm * (xh * wv) * (sg * (1.0 + ug * (1.0 - sg)))).astype(BF16)
            dxh = dn * wv
            t = jnp.mean(dxh * xh, axis=-1, keepdims=True)
            do_ref[:, cols] = r * (dxh - xh * t)
            dw_ref[:, cols] += jnp.sum(dn * xh, axis=0, keepdims=True)

    tok = pl.BlockSpec((tm, HG_W), lambda i: (i, 0))
    vec = pl.BlockSpec((1, HG_W), lambda i: (0, 0))
    return pl.pallas_call(
        body, name=name, grid=(T // tm,),
        in_specs=[tok, tok, tok, pl.BlockSpec((tm, HG_W), lambda i: (i, 4)), pl.BlockSpec((1, HG_D), lambda i: (0, 0))],
        out_specs=[tok, tok, vec],
        out_shape=[jax.ShapeDtypeStruct((T, HG_W), F32), jax.ShapeDtypeStruct((T, HG_W), BF16),
                   jax.ShapeDtypeStruct((1, HG_W), F32)],
        compiler_params=_params(("arbitrary",)),
    )(dmix, o_f, o_b, U, w)


def _rope_tables(T):
    rows = T // GRID_W
    row = np.repeat(np.arange(rows), GRID_W).astype(np.float32)
    col = np.tile(np.arange(GRID_W), rows).astype(np.float32)
    axis_dim = ATT_DH // 2
    freqs = (np.float32(ROPE_THETA) ** (-np.arange(0, axis_dim, 2, dtype=np.float32) / np.float32(axis_dim))
             ).astype(np.float32)
    ang = np.concatenate([row[:, None] * freqs, col[:, None] * freqs], axis=-1).astype(np.float32)
    cos, sin = np.cos(ang), np.sin(ang)
    c = np.repeat(cos, 2, axis=-1)
    s = np.stack([-sin, sin], axis=-1).reshape(T, ATT_DH)
    return jnp.asarray(np.tile(c, (1, 2)), F32), jnp.asarray(np.tile(s, (1, 2)), F32)


def _head_blockdiag(width):
    shift = ATT_DH.bit_length() - 1
    ri = jnp.right_shift(lax.broadcasted_iota(jnp.int32, (width, width), 0), shift)
    ci = jnp.right_shift(lax.broadcasted_iota(jnp.int32, (width, width), 1), shift)
    return jnp.where(ri == ci, 1.0, 0.0).astype(BF16)


def _head_sum(x, bd):
    hi = x.astype(BF16)
    lo = (x - hi.astype(F32)).astype(BF16)
    return jnp.dot(hi, bd, preferred_element_type=F32) + jnp.dot(lo, bd, preferred_element_type=F32)


def _pair_swap(x, even):
    n = x.shape[-1]
    return jnp.where(even, pltpu.roll(x, n - 1, 1), pltpu.roll(x, 1, 1))


FA_TQ = 512


FA_TK = 512


def _cols_from_tokens(x, kv):
    w = ATT_G * ATT_DH
    xt = x[:, kv * w:(kv + 1) * w].T
    return jnp.concatenate([xt[g * ATT_DH:(g + 1) * ATT_DH, :] for g in range(ATT_G)], axis=1)


def _tokens_from_cols(c):
    tq = c.shape[1] // ATT_G
    return jnp.concatenate([c[:, g * tq:(g + 1) * tq] for g in range(ATT_G)], axis=0).T


def _store_cols(ref, x, norm_ref=None):
    for kv in range(ATT_KV):
        cols = _cols_from_tokens(x, kv).astype(BF16)
        ref[kv, 0] = cols
        if norm_ref is not None:
            cf = cols.astype(F32)
            norm_ref[kv, 0] = jnp.sqrt(jnp.sum(cf * cf, axis=0, keepdims=True))


def _att_prep_fwd(U, cos, sin, qw, kw, *, name):
    T = U.shape[0]
    tm = min(FA_TQ, T)
    R = ATT_G * tm
    scale = ATT_DH ** -0.5

    def head_rows(ref, x):
        xt = x.astype(F32).T
        for kv in range(ATT_KV):
            ref[kv, 0] = xt[kv * ATT_DH:(kv + 1) * ATT_DH, :].astype(BF16)

    def body(aq_ref, ak_ref, av_ref, c_ref, s_ref, qw_ref, kw_ref, q_ref, qn_ref, kmax_ref, kc_ref, vc_ref):
        @pl.when(pl.program_id(0) == 0)
        def _():
            kmax_ref[...] = jnp.zeros_like(kmax_ref)

        bd = _head_blockdiag(ATT_QW)
        c2, s2 = c_ref[...], s_ref[...]
        c8, s8 = jnp.tile(c2, (1, 4)), jnp.tile(s2, (1, 4))

        def norm_rope(x, w, c, s, bdm):
            r = lax.rsqrt(_head_sum(x * x, bdm) * (1.0 / ATT_DH) + EPS)
            y = x * r * w
            even = (lax.broadcasted_iota(jnp.int32, y.shape, 1) & 1) == 0
            return y * c + _pair_swap(y, even) * s

        _store_cols(q_ref, norm_rope(aq_ref[...], qw_ref[...], c8, s8, bd) * scale, qn_ref)
        kb = norm_rope(ak_ref[...], kw_ref[...], c2, s2, bd[:ATT_KW, :ATT_KW]).astype(BF16)
        kf = kb.astype(F32)
        ksq = _head_sum(kf * kf, bd[:ATT_KW, :ATT_KW])
        kmax_ref[...] = jnp.maximum(kmax_ref[...], jnp.max(ksq, axis=0, keepdims=True))
        head_rows(kc_ref, kb)
        head_rows(vc_ref, av_ref[...].astype(BF16))

    kv_spec = pl.BlockSpec((tm, ATT_KW), lambda i: (i, 0))
    tk = min(FA_TK, T)
    per = tk // tm
    c_spec = pl.BlockSpec((ATT_KV, 1, ATT_DH, tm), lambda i: (0, i // per, 0, i % per))
    c_shape = jax.ShapeDtypeStruct((ATT_KV, T // tk, ATT_DH, tk), BF16)
    return pl.pallas_call(
        body, name=name, grid=(T // tm,),
        in_specs=[pl.BlockSpec((tm, ATT_QW), lambda i: (i, 5)),
                  pl.BlockSpec((tm, ATT_KW), lambda i: (i, 24)), pl.BlockSpec((tm, ATT_KW), lambda i: (i, 25)),
                  kv_spec, kv_spec,
                  pl.BlockSpec((1, ATT_QW), lambda i: (0, 0)), pl.BlockSpec((1, ATT_KW), lambda i: (0, 0))],
        out_specs=[pl.BlockSpec((ATT_KV, 1, ATT_DH, R), lambda i: (0, i, 0, 0)),
                   pl.BlockSpec((ATT_KV, 1, 1, R), lambda i: (0, i, 0, 0)), pl.BlockSpec((1, ATT_KW), lambda i: (0, 0)),
                   c_spec, c_spec],
        out_shape=[jax.ShapeDtypeStruct((ATT_KV, T // tm, ATT_DH, R), BF16),
                   jax.ShapeDtypeStruct((ATT_KV, T // tm, 1, R), F32), jax.ShapeDtypeStruct((1, ATT_KW), F32),
                   c_shape, c_shape],
        compiler_params=_params(("arbitrary",)),
    )(U, U, U, cos, sin, qw, kw)


def _att_prep_bwd(U, dq_c, dk_c, dv_c, cos, sin, qw, kw, *, name):
    T = U.shape[0]
    tm = min(FA_TQ, T)
    R = ATT_G * tm
    scale = ATT_DH ** -0.5

    def body(aq_ref, ak_ref, dq_ref, dk_ref, dv_ref, c_ref, s_ref, qw_ref, kw_ref, out_ref, dqw_ref, dkw_ref):
        @pl.when(pl.program_id(0) == 0)
        def _():
            dqw_ref[...] = jnp.zeros_like(dqw_ref)
            dkw_ref[...] = jnp.zeros_like(dkw_ref)

        bd = _head_blockdiag(ATT_QW)
        c2, s2 = c_ref[...], s_ref[...]
        c8, s8 = jnp.tile(c2, (1, 4)), jnp.tile(s2, (1, 4))

        def bwd(x, dy, w, c, s, bdm):
            even = (lax.broadcasted_iota(jnp.int32, x.shape, 1) & 1) == 0
            dn = dy * c - _pair_swap(dy, even) * s
            r = lax.rsqrt(_head_sum(x * x, bdm) * (1.0 / ATT_DH) + EPS)
            xh = x * r
            dxh = dn * w
            t = _head_sum(dxh * xh, bdm) * (1.0 / ATT_DH)
            return r * (dxh - xh * t), jnp.sum(dn * xh, axis=0, keepdims=True)

        dq = jnp.concatenate([_tokens_from_cols(dq_ref[kv, 0]) for kv in range(ATT_KV)], axis=1)
        da, dw = bwd(aq_ref[...], dq * scale, qw_ref[...], c8, s8, bd)
        out_ref[:, 0:ATT_QW] = da.astype(BF16)
        dqw_ref[...] += dw
        tokens = lambda ref: jnp.concatenate([ref[kv, 0] for kv in range(ATT_KV)], axis=0).T
        da, dw = bwd(ak_ref[...], tokens(dk_ref), kw_ref[...], c2, s2, bd[:ATT_KW, :ATT_KW])
        out_ref[:, ATT_QW:ATT_QW + ATT_KW] = da.astype(BF16)
        dkw_ref[...] += dw
        out_ref[:, ATT_QW + ATT_KW:ATT_QW + 2 * ATT_KW] = tokens(dv_ref).astype(BF16)

    kv_spec = pl.BlockSpec((tm, ATT_KW), lambda i: (i, 0))
    qv = pl.BlockSpec((1, ATT_QW), lambda i: (0, 0))
    kv = pl.BlockSpec((1, ATT_KW), lambda i: (0, 0))
    w_att = ATT_QW + 2 * ATT_KW
    per = dk_c.shape[3] // tm
    c_spec = pl.BlockSpec((ATT_KV, 1, ATT_DH, tm), lambda i: (0, i // per, 0, i % per))
    return pl.pallas_call(
        body, name=name, grid=(T // tm,),
        in_specs=[pl.BlockSpec((tm, ATT_QW), lambda i: (i, 5)), pl.BlockSpec((tm, ATT_KW), lambda i: (i, 24)),
                  pl.BlockSpec((ATT_KV, 1, ATT_DH, R), lambda i: (0, i, 0, 0)), c_spec, c_spec, kv_spec, kv_spec, qv, kv],
        out_specs=[pl.BlockSpec((tm, w_att), lambda i: (i, 0)), qv, kv],
        out_shape=[jax.ShapeDtypeStruct((T, w_att), BF16),
                   jax.ShapeDtypeStruct((1, ATT_QW), F32), jax.ShapeDtypeStruct((1, ATT_KW), F32)],
        compiler_params=_params(("arbitrary",)),
    )(U, U, dq_c, dk_c, dv_c, cos, sin, qw, kw)


def _scores(k_ref, j, qv):
    return lax.dot_general(k_ref[0, j], qv, (((0,), (0,)), ((), ())), preferred_element_type=F32)


def _flash_fwd(q_c, k_c, v_c, *, name):
    _, nq, _, R = q_c.shape
    _, n_k, _, tk = v_c.shape

    def body(q_ref, k_ref, v_ref, o_ref, lse_ref, acc_ref):
        qv = q_ref[0, 0]
        acc_ref[...] = jnp.zeros_like(acc_ref)

        def step(j, carry):
            m, l = carry
            s = _scores(k_ref, j, qv)
            m_new = jnp.maximum(m, jnp.max(s, axis=0, keepdims=True))
            alpha = jnp.exp(m - m_new)
            p = jnp.exp(s - m_new)
            l = alpha * l + jnp.sum(p, axis=0, keepdims=True)
            acc_ref[...] = alpha * acc_ref[...] + jnp.dot(v_ref[0, j], p.astype(BF16), preferred_element_type=F32)
            return m_new, l

        m, l = lax.fori_loop(0, n_k, step, (jnp.full((1, R), -jnp.inf, F32), jnp.zeros((1, R), F32)))
        o_ref[0, 0] = acc_ref[...] / l
        lse_ref[0, 0] = m + jnp.log(l)

    cspec = pl.BlockSpec((1, 1, ATT_DH, R), lambda h, i: (h, i, 0, 0))
    kspec = pl.BlockSpec((1, n_k, ATT_DH, tk), lambda h, i: (h, 0, 0, 0))
    return pl.pallas_call(
        body, name=name, grid=(ATT_KV, nq),
        in_specs=[cspec, kspec, kspec],
        out_specs=[cspec, pl.BlockSpec((1, 1, 1, R), lambda h, i: (h, i, 0, 0))],
        out_shape=[jax.ShapeDtypeStruct((ATT_KV, nq, ATT_DH, R), F32), jax.ShapeDtypeStruct((ATT_KV, nq, 1, R), F32)],
        scratch_shapes=[pltpu.VMEM((ATT_DH, R), F32)],
        compiler_params=_params(("parallel", "parallel")),
    )(q_c, k_c, v_c)


FA_BOUND_MAX = 40.0


def _flash_fwd_bounded(q_c, k_c, v_c, m_c, *, name):
    _, nq, _, R = q_c.shape
    _, n_k, _, tk = v_c.shape

    def body(q_ref, k_ref, v_ref, m_ref, o_ref, lse_ref, acc_ref):
        qv = q_ref[0, 0]
        m = m_ref[0, 0]
        acc_ref[...] = jnp.zeros_like(acc_ref)

        per = math.gcd(n_k, 4)

        def step(jj, l8):
            pv = None
            for u in range(per):
                j = per * jj + u
                p = jnp.exp(_scores(k_ref, j, qv) - m)
                l8 = l8 + jnp.sum(p.reshape(tk // 8, 8, R), axis=0)
                d = jnp.dot(v_ref[0, j], p.astype(BF16), preferred_element_type=F32)
                pv = d if pv is None else pv + d
            acc_ref[...] += pv
            return l8

        l8 = lax.fori_loop(0, n_k // per, step, jnp.zeros((8, R), F32))
        l = jnp.sum(l8, axis=0, keepdims=True)
        o_ref[0, 0] = acc_ref[...] / l
        lse_ref[0, 0] = m + jnp.log(l)

    cspec = pl.BlockSpec((1, 1, ATT_DH, R), lambda h, i: (h, i, 0, 0))
    kspec = pl.BlockSpec((1, n_k, ATT_DH, tk), lambda h, i: (h, 0, 0, 0))
    vspec = pl.BlockSpec((1, 1, 1, R), lambda h, i: (h, i, 0, 0))
    return pl.pallas_call(
        body, name=name, grid=(ATT_KV, nq),
        in_specs=[cspec, kspec, kspec, vspec],
        out_specs=[cspec, vspec],
        out_shape=[jax.ShapeDtypeStruct((ATT_KV, nq, ATT_DH, R), F32), jax.ShapeDtypeStruct((ATT_KV, nq, 1, R), F32)],
        scratch_shapes=[pltpu.VMEM((ATT_DH, R), F32)],
        compiler_params=_params(("parallel", "parallel")),
    )(q_c, k_c, v_c, m_c)


CHIP_MASKS = [(1, 0, 0), (0, 1, 0), (1, 1, 0)]


def _chip_slot(p):
    return 2 * p[0] + p[1]


def _flash_bwd(q_c, k_c, v_c, do_c, lse, delta, *, name, ride=None):
    _, nq, _, R = q_c.shape
    _, n_k, _, tk = k_c.shape
    n_ride = 0 if ride is None else len(ride)

    def body(qc_ref, kc_ref, vc_ref, doc_ref, lse_ref, delta_ref, *rest):
        ride_in, rest = rest[:n_ride], rest[n_ride:]
        dq_ref, dk_ref, dv_ref = rest[:3]
        ride_out, rest = rest[3:3 + n_ride], rest[3 + n_ride:]
        acc_ref = rest[0]
        kv = pl.program_id(0)
        first = (kv == 0) & (pl.program_id(1) == 0)

        if n_ride:
            mine, copies = _exchange_copies(ride_in, ride_out, *rest[1:], masks=CHIP_MASKS, slot=_chip_slot)

            @pl.when(first)
            def _():
                for cp in mine:
                    cp.start()
                for send, _ in copies:
                    send.start()

        @pl.when(pl.program_id(1) == 0)
        def _():
            dk_ref[...] = jnp.zeros_like(dk_ref)
            dv_ref[...] = jnp.zeros_like(dv_ref)

        qc, doc = qc_ref[0, 0], doc_ref[0, 0]
        lsev, delta = lse_ref[0, 0], delta_ref[0, 0]
        acc_ref[...] = jnp.zeros_like(acc_ref)
        nt = (((1,), (1,)), ((), ()))

        def step(j, carry):
            p = jnp.exp(_scores(kc_ref, j, qc) - lsev)
            dp = _scores(vc_ref, j, doc)
            ds = (p * (dp - delta)).astype(BF16)
            acc_ref[...] += jnp.dot(kc_ref[0, j], ds, preferred_element_type=F32)
            dk_ref[0, j] += lax.dot_general(qc, ds, nt, preferred_element_type=F32)
            dv_ref[0, j] += lax.dot_general(doc, p.astype(BF16), nt, preferred_element_type=F32)
            return carry

        lax.fori_loop(0, n_k, step, 0, unroll=2)
        dq_ref[0, 0] = acc_ref[...]

        if n_ride:
            @pl.when((kv == ATT_KV - 1) & (pl.program_id(1) == nq - 1))
            def _():
                for send, recv in copies:
                    recv.wait_recv()
                    send.wait_send()
                for cp in mine:
                    cp.wait()

    cspec = pl.BlockSpec((1, 1, ATT_DH, R), lambda h, i: (h, i, 0, 0))
    vspec = pl.BlockSpec((1, 1, 1, R), lambda h, i: (h, i, 0, 0))
    kspec = pl.BlockSpec((1, n_k, ATT_DH, tk), lambda h, i: (h, 0, 0, 0))
    ride = [] if ride is None else list(ride)
    scratch = [pltpu.VMEM((ATT_DH, R), F32)]
    if n_ride:
        n_sem = len(CHIP_MASKS) * n_ride
        scratch += [pltpu.SemaphoreType.DMA((n_sem,)), pltpu.SemaphoreType.DMA((n_sem,)),
                    pltpu.SemaphoreType.DMA((n_ride,))]
    k_shape = jax.ShapeDtypeStruct(k_c.shape, F32)
    return pl.pallas_call(
        body, name=name, grid=(ATT_KV, nq),
        in_specs=[cspec, kspec, kspec, cspec, vspec, vspec] + [ANY] * n_ride,
        out_specs=[cspec, kspec, kspec] + [ANY] * n_ride,
        out_shape=[jax.ShapeDtypeStruct((ATT_KV, nq, ATT_DH, R), F32), k_shape, k_shape]
                  + [jax.ShapeDtypeStruct(g.shape, g.dtype) for g in ride],
        scratch_shapes=scratch,
        compiler_params=pltpu.CompilerParams(dimension_semantics=("arbitrary", "arbitrary"),
                                             vmem_limit_bytes=VMEM_LIMIT, has_side_effects=bool(n_ride)),
    )(q_c, k_c, v_c, do_c, lse, delta, *ride)


def _att_post_fwd(o_c, w, *, name):
    _, nq, _, R = o_c.shape
    tm = R // ATT_G
    T = nq * tm

    def body(oc_ref, w_ref, o_ref, out_ref):
        ov = jnp.concatenate([_tokens_from_cols(oc_ref[kv, 0]) for kv in range(ATT_KV)], axis=1)
        r = lax.rsqrt(jnp.mean(ov * ov, axis=-1, keepdims=True) + EPS)
        o_ref[...] = ov
        out_ref[...] = (ov * r * w_ref[...]).astype(BF16)

    tok = pl.BlockSpec((tm, ATT_QW), lambda i: (i, 0))
    return pl.pallas_call(
        body, name=name, grid=(nq,),
        in_specs=[pl.BlockSpec((ATT_KV, 1, ATT_DH, R), lambda i: (0, i, 0, 0)), pl.BlockSpec((1, ATT_QW), lambda i: (0, 0))],
        out_specs=[tok, tok],
        out_shape=[jax.ShapeDtypeStruct((T, ATT_QW), F32), jax.ShapeDtypeStruct((T, ATT_QW), BF16)],
        compiler_params=_params(("parallel",)),
    )(o_c, w)


def _att_post_bwd(dmix, o, w, *, name):
    T = o.shape[0]
    tm = min(FA_TQ, T)
    R = ATT_G * tm

    def body(dm_ref, o_ref, w_ref, do_ref, delta_ref, dw_ref):
        @pl.when(pl.program_id(0) == 0)
        def _():
            dw_ref[...] = jnp.zeros_like(dw_ref)

        ov = o_ref[...]
        r = lax.rsqrt(jnp.mean(ov * ov, axis=-1, keepdims=True) + EPS)
        xh = ov * r
        dm = dm_ref[...]
        dxh = dm * w_ref[...]
        t = jnp.mean(dxh * xh, axis=-1, keepdims=True)
        do = r * (dxh - xh * t)
        _store_cols(do_ref, do)
        dob = do.astype(BF16).astype(F32)
        for kv in range(ATT_KV):
            delta_ref[kv, 0] = jnp.sum(_cols_from_tokens(dob * ov, kv), axis=0, keepdims=True)
        dw_ref[...] += jnp.sum(dm * xh, axis=0, keepdims=True)

    tok = pl.BlockSpec((tm, ATT_QW), lambda i: (i, 0))
    vec = pl.BlockSpec((1, ATT_QW), lambda i: (0, 0))
    return pl.pallas_call(
        body, name=name, grid=(T // tm,),
        in_specs=[pl.BlockSpec((tm, ATT_QW), lambda i: (i, 1)), tok, vec],
        out_specs=[pl.BlockSpec((ATT_KV, 1, ATT_DH, R), lambda i: (0, i, 0, 0)),
                   pl.BlockSpec((ATT_KV, 1, 1, R), lambda i: (0, i, 0, 0)), vec],
        out_shape=[jax.ShapeDtypeStruct((ATT_KV, T // tm, ATT_DH, R), BF16),
                   jax.ShapeDtypeStruct((ATT_KV, T // tm, 1, R), F32), jax.ShapeDtypeStruct((1, ATT_QW), F32)],
        compiler_params=_params(("arbitrary",)),
    )(dmix, o, w)


def _ffn_up(h2, wg_t, wu_t, *, name, tm=512):
    T = h2.shape[0]
    tn = _pick(D_FF, 1408)
    nt = (((1,), (1,)), ((), ()))

    def body(h_ref, wg_ref, wu_ref, g_ref, u_ref, a_ref):
        hv = h_ref[...]
        g = lax.dot_general(hv, wg_ref[...], nt, preferred_element_type=F32)
        u = lax.dot_general(hv, wu_ref[...], nt, preferred_element_type=F32)
        g_ref[...] = g.astype(BF16)
        u_ref[...] = u.astype(BF16)
        a_ref[...] = (g * _sigmoid(g) * u).astype(BF16)

    wspec = pl.BlockSpec((tn, D_MODEL), lambda i, j: (j, 0))
    ospec = pl.BlockSpec((tm, tn), lambda i, j: (i, j))
    return pl.pallas_call(
        body, name=name, grid=(T // tm, D_FF // tn),
        in_specs=[pl.BlockSpec((tm, D_MODEL), lambda i, j: (i, 0)), wspec, wspec],
        out_specs=[ospec] * 3, out_shape=[jax.ShapeDtypeStruct((T, D_FF), BF16)] * 3,
        compiler_params=_params(("parallel", "arbitrary")),
    )(h2, wg_t, wu_t)


def _ffn_act_bwd(dx2b, w_down, gate, up, *, name, tm=512):
    T = dx2b.shape[0]
    tn = _pick(D_FF, 1408)

    def body(dx_ref, w_ref, g_ref, u_ref, dg_ref, du_ref):
        da = lax.dot_general(dx_ref[...], w_ref[...], (((1,), (1,)), ((), ())), preferred_element_type=F32)
        g = g_ref[...].astype(F32)
        u = u_ref[...].astype(F32)
        sg = _sigmoid(g)
        dg_ref[...] = (da * u * (sg * (1.0 + g * (1.0 - sg)))).astype(BF16)
        du_ref[...] = (da * (g * sg)).astype(BF16)

    ospec = pl.BlockSpec((tm, tn), lambda i, j: (i, j))
    return pl.pallas_call(
        body, name=name, grid=(T // tm, D_FF // tn),
        in_specs=[pl.BlockSpec((tm, D_MODEL), lambda i, j: (i, 0)),
                  pl.BlockSpec((tn, D_MODEL), lambda i, j: (j, 0)), ospec, ospec],
        out_specs=[ospec] * 2, out_shape=[jax.ShapeDtypeStruct((T, D_FF), BF16)] * 2,
        compiler_params=_params(("parallel", "arbitrary")),
    )(dx2b, w_down, gate, up)


def _adam_math(w, g, m, v):
    m = ADAM_B1 * m + (1.0 - ADAM_B1) * g
    v = ADAM_B2 * v + (1.0 - ADAM_B2) * (g * g)
    m_hat = m / (1.0 - ADAM_B1 ** ADAM_STEP)
    v_hat = v / (1.0 - ADAM_B2 ** ADAM_STEP)
    delta = -ADAM_LR * (m_hat / (jnp.sqrt(v_hat) + ADAM_EPS) + ADAM_WD * w)
    return delta, m, v


def _adamw(parts, w, m, v, *, name, tr_cap=256):
    P, R, C = parts.shape
    tr = R
    for t in range(8, min(R, tr_cap) + 1, 8):
        if R % t == 0:
            tr = t

    def body(p_ref, w_ref, m_ref, v_ref, g_ref, d_ref, nm_ref, nv_ref):
        g = p_ref[0].astype(F32)
        for j in range(1, P):
            g = g + p_ref[j].astype(F32)
        d, nm, nv = _adam_math(w_ref[...], g, m_ref[...], v_ref[...])
        g_ref[...] = g
        d_ref[...] = d
        nm_ref[...] = nm
        nv_ref[...] = nv

    blk = pl.BlockSpec((tr, C), lambda i: (i, 0))
    return pl.pallas_call(
        body, name=name, grid=(R // tr,),
        in_specs=[pl.BlockSpec((P, tr, C), lambda i: (0, i, 0)), blk, blk, blk],
        out_specs=[blk] * 4, out_shape=[jax.ShapeDtypeStruct((R, C), F32)] * 4,
        compiler_params=_params(("parallel",)),
    )(parts, w, m, v)


def _gather_halves(ins, outs, send_sems, recv_sems, local_sems):
    n = len(ins)
    x, y, c = lax.axis_index("x"), lax.axis_index("y"), lax.axis_index("c")
    me, sibling = (x, y, c), (x, y, 1 - c)
    chips = [(1 - x, y), (x, 1 - y), (1 - x, 1 - y)]

    def slot(p):
        return 4 * p[0] + 2 * p[1] + p[2]

    def copy(a, k, block, to, src=None):
        dst = outs[a].at[slot(block)]
        return pltpu.make_async_remote_copy(
            src_ref=dst if src is None else src, dst_ref=dst,
            send_sem=send_sems.at[a * 7 + k], recv_sem=recv_sems.at[a * 7 + k],
            device_id=to, device_id_type=MESH)

    mine = [pltpu.make_async_copy(ins[a], outs[a].at[slot(me)], local_sems.at[a]) for a in range(n)]
    first = []
    for a in range(n):
        first.append(copy(a, 0, me, sibling, src=ins[a]))
        first += [copy(a, 1 + j, me, (*chip, c), src=ins[a]) for j, chip in enumerate(chips)]

    def start():
        for cp in mine + first:
            cp.start()

    def finish():
        passed = []
        for j, chip in enumerate(chips):
            for a in range(n):
                copy(a, 1 + j, (*chip, c), me).wait_recv()
                cp = copy(a, 4 + j, (*chip, c), sibling)
                cp.start()
                passed.append(cp)
        for a in range(n):
            copy(a, 0, sibling, me).wait_recv()
            for j, chip in enumerate(chips):
                copy(a, 4 + j, (*chip, 1 - c), me).wait_recv()
        for cp in first + passed:
            cp.wait_send()
        for cp in mine:
            cp.wait()

    return start, finish


def _gather_scratch(n):
    return [pltpu.SemaphoreType.DMA((7 * n,)), pltpu.SemaphoreType.DMA((7 * n,)), pltpu.SemaphoreType.DMA((n,))]


def _gathered_shapes(xs):
    return [jax.ShapeDtypeStruct((N_DEV,) + x.shape, x.dtype) for x in xs]


def _all_gather(xs, *, name):
    n = len(xs)

    def body(*refs):
        start, finish = _gather_halves(refs[:n], refs[n:2 * n], *refs[2 * n:])
        start()
        finish()

    return pl.pallas_call(
        body, name=name,
        in_specs=[ANY] * n, out_specs=[ANY] * n, out_shape=_gathered_shapes(xs), scratch_shapes=_gather_scratch(n),
        compiler_params=pltpu.CompilerParams(has_side_effects=True),
    )(*xs)


ALL_MASKS = [(mx, my, mc) for mx in (0, 1) for my in (0, 1) for mc in (0, 1)][1:]


def _flip(v, bit):
    return 1 - v if bit else v


def _exchange_copies(ins, outs, send_sems, recv_sems, local_sems, *, masks, slot):
    n, n_peers = len(ins), len(masks)
    x, y, c = lax.axis_index("x"), lax.axis_index("y"), lax.axis_index("c")
    my_slot = slot((x, y, c))
    mine = [pltpu.make_async_copy(ins[a].at[my_slot], outs[a].at[my_slot], local_sems.at[a]) for a in range(n)]
    copies = []
    for a in range(n):
        for k, (mx, my, mc) in enumerate(masks):
            peer = (_flip(x, mx), _flip(y, my), _flip(c, mc))
            peer_slot = slot(peer)
            sems = dict(send_sem=send_sems.at[a * n_peers + k], recv_sem=recv_sems.at[a * n_peers + k],
                        device_id=peer, device_id_type=MESH)
            copies.append((
                pltpu.make_async_remote_copy(src_ref=ins[a].at[peer_slot], dst_ref=outs[a].at[my_slot], **sems),
                pltpu.make_async_remote_copy(src_ref=ins[a].at[peer_slot], dst_ref=outs[a].at[peer_slot], **sems)))
    return mine, copies


def _exchange(gs, *, masks, slot, name, bcast=None):
    n, n_peers = len(gs), len(masks)
    has_bcast = bcast is not None

    def body(*refs):
        n_in = n + has_bcast
        ins, outs = refs[:n], refs[n_in:n_in + n]
        send_sems, recv_sems, local_sems = refs[2 * n_in:2 * n_in + 3]
        x, y, c = lax.axis_index("x"), lax.axis_index("y"), lax.axis_index("c")
        mine, copies = _exchange_copies(ins, outs, send_sems, recv_sems, local_sems, masks=masks, slot=slot)
        if has_bcast:
            b_in, b_out = refs[n], refs[2 * n_in - 1]
            b_send, b_recv = refs[2 * n_in + 3:]
            me = 4 * x + 2 * y + c
            mine.append(pltpu.make_async_copy(b_in, b_out.at[me], local_sems.at[n]))
            for k, (mx, my, mc) in enumerate(ALL_MASKS):
                peer = (_flip(x, mx), _flip(y, my), _flip(c, mc))
                peer_id = 4 * peer[0] + 2 * peer[1] + peer[2]
                sems = dict(send_sem=b_send.at[k], recv_sem=b_recv.at[k], device_id=peer, device_id_type=MESH)
                copies.append((pltpu.make_async_remote_copy(src_ref=b_in, dst_ref=b_out.at[me], **sems),
                               pltpu.make_async_remote_copy(src_ref=b_in, dst_ref=b_out.at[peer_id], **sems)))
        for cp in mine:
            cp.start()
        for send, _ in copies:
            send.start()
        for send, recv in copies:
            recv.wait_recv()
            send.wait_send()
        for cp in mine:
            cp.wait()

    n_io = n + has_bcast
    out_shape = [jax.ShapeDtypeStruct(g.shape, g.dtype) for g in gs]
    scratch = [pltpu.SemaphoreType.DMA((n_peers * n,)), pltpu.SemaphoreType.DMA((n_peers * n,)),
               pltpu.SemaphoreType.DMA((n_io,))]
    if has_bcast:
        out_shape.append(jax.ShapeDtypeStruct((N_DEV,) + bcast.shape, bcast.dtype))
        scratch += [pltpu.SemaphoreType.DMA((len(ALL_MASKS),)), pltpu.SemaphoreType.DMA((len(ALL_MASKS),))]
    return pl.pallas_call(
        body, name=name,
        in_specs=[ANY] * n_io, out_specs=[ANY] * n_io, out_shape=out_shape, scratch_shapes=scratch,
        compiler_params=pltpu.CompilerParams(has_side_effects=True),
    )(*gs, *([bcast] if has_bcast else []))


SWAP_ROW_CHUNKS = 4


def _core_swap(gs, *, name):
    n = len(gs)

    def body(*refs):
        ins, outs = refs[:n], refs[n:2 * n]
        send_sems, recv_sems = refs[2 * n:]
        x, y, c = lax.axis_index("x"), lax.axis_index("y"), lax.axis_index("c")
        sibling = (x, y, 1 - c)
        for a in range(n):
            Q, _, R, _ = ins[a].shape
            rows = R // SWAP_ROW_CHUNKS
            for q in range(Q):
                for j in range(SWAP_ROW_CHUNKS):
                    pltpu.make_async_remote_copy(
                        src_ref=ins[a].at[q, 1 - c, pl.ds(j * rows, rows)], dst_ref=outs[a].at[q, pl.ds(j * rows, rows)],
                        send_sem=send_sems.at[a], recv_sem=recv_sems.at[a], device_id=sibling, device_id_type=MESH
                    ).start()
        for a in range(n):
            pltpu.make_async_remote_copy(
                src_ref=outs[a], dst_ref=outs[a], send_sem=send_sems.at[a], recv_sem=recv_sems.at[a],
                device_id=sibling, device_id_type=MESH).wait()

    return pl.pallas_call(
        body, name=name,
        in_specs=[ANY] * n, out_specs=[ANY] * n,
        out_shape=[jax.ShapeDtypeStruct(g.shape[:1] + g.shape[2:], g.dtype) for g in gs],
        scratch_shapes=[pltpu.SemaphoreType.DMA((n,)), pltpu.SemaphoreType.DMA((n,))],
        compiler_params=pltpu.CompilerParams(has_side_effects=True),
    )(*gs)


def _pair_sum(g, other, core, *, name, tr_cap=256):
    Q, _, R, C = g.shape
    tr = max(t for t in range(16, min(R, tr_cap) + 1, 16) if R % t == 0)

    def body(core_ref, g_ref, o_ref, out_ref):
        out_ref[0] = (g_ref[0, 0] + o_ref[0]).astype(BF16)

    return pl.pallas_call(
        body, name=name,
        grid_spec=pltpu.PrefetchScalarGridSpec(
            num_scalar_prefetch=1, grid=(Q, R // tr),
            in_specs=[pl.BlockSpec((1, 1, tr, C), lambda q, i, core_ref: (q, core_ref[0], i, 0)),
                      pl.BlockSpec((1, tr, C), lambda q, i, core_ref: (q, i, 0))],
            out_specs=pl.BlockSpec((1, tr, C), lambda q, i, core_ref: (q, i, 0))),
        out_shape=jax.ShapeDtypeStruct((Q, R, C), BF16),
        compiler_params=_params(("parallel", "parallel")),
    )(core, g, other)


def _pack_small(norm1, norm2, final, att, hg, qn, kn, lb=None, loss=None):
    z = lambda n: jnp.zeros((n,), F32)
    rows = [norm1.reshape(-1), norm2.reshape(-1), final.reshape(-1),
            jnp.concatenate([att.reshape(-1), z(512)]),
            jnp.concatenate([hg.reshape(-1), qn.reshape(-1), kn.reshape(-1), z(1024 - 256)]),
            z(1024) if lb is None else lb.reshape(-1),
            z(1024) if loss is None else jnp.concatenate([loss.reshape(-1), z(1023)]), z(1024)]
    return jnp.stack(rows, axis=0)


def _unpack_small(p):
    return (p[0:1, :], p[1:2, :], p[2, :], p[3:4, 0:512], p[4:5, 0:128], p[4:5, 128:192], p[4:5, 192:256])


def _fold_heads(dhg, dqn, dkn, *, name):
    def body(hg_ref, q_ref, k_ref, ohg_ref, oq_ref, ok_ref):
        def fold128(v):
            acc = v[:, 0:LANES]
            for j in range(1, v.shape[1] // LANES):
                acc = acc + v[:, j * LANES:(j + 1) * LANES]
            return acc

        ohg_ref[...] = fold128(hg_ref[...])
        q = fold128(q_ref[...])
        oq_ref[...] = q + pltpu.roll(q, ATT_DH, 1)
        k = k_ref[...]
        ok_ref[...] = k + pltpu.roll(k, ATT_DH, 1)

    return pl.pallas_call(body, name=name, out_shape=[jax.ShapeDtypeStruct((1, LANES), F32)] * 3)(dhg, dqn, dkn)


def _lb_grad(dlb_sum, lb, *, name):
    def body(d_ref, lb_ref, o_ref):
        lbv = lb_ref[...]
        gl = d_ref[...] * lbv * (1.0 - lbv)
        o_ref[0:1, :] = gl[0:1, :]
        o_ref[1:2, :] = -gl[0:1, :]
        o_ref[2:3, :] = gl[1:2, :]
        o_ref[3:4, :] = -gl[1:2, :]

    return pl.pallas_call(body, name=name, out_shape=jax.ShapeDtypeStruct((4, HG_W), F32))(dlb_sum, lb)


def _lower_bounds(lb_logits_full, *, name):
    def body(l_ref, o_ref):
        for d in range(2):
            l0, l1 = l_ref[2 * d:2 * d + 1, :], l_ref[2 * d + 1:2 * d + 2, :]
            mx = jnp.maximum(l0, l1)
            e0, e1 = jnp.exp(l0 - mx), jnp.exp(l1 - mx)
            o_ref[d:d + 1, :] = e0 / (e0 + e1)

    return pl.pallas_call(body, name=name, out_shape=jax.ShapeDtypeStruct((2, HG_W), F32))(
        lb_logits_full.reshape(4, HG_W))


def _local_step(x, target, norm1_w, w_in_t, lb, hg_norm_w, q_norm_w, k_norm_w, att_norm_w, w_out, norm2_w,
                w_g_t, w_u_t, w_down, final_norm_w, reduce_early=None, shards=None):
    T = x.shape[0]
    cos, sin = _rope_tables(T)
    qw8 = jnp.tile(q_norm_w, (1, ATT_HEADS))
    kw2 = jnp.tile(k_norm_w, (1, ATT_KV))

    h, r1 = _rms_fwd(x, norm1_w, name="norm1_fwd")
    if shards is None:
        U = _mm_nn([(h, w_in_t)], trans_b=True, name="in_proj")
        o_f, st_f = _gla_fwd(U, lb[0:1], f_block=1, reverse=False, name="gla_fwd_f")
    else:
        U, g_gu = _mm_nn([(h, w_in_t)], trans_b=True, gather=[shards["w_gu_t"]], name="in_proj")
        o_f, st_f, g_out, g_dn = _gla_fwd(U, lb[0:1], f_block=1, reverse=False,
                                          gather=[shards["w_out"], shards["w_down"]], name="gla_fwd_f")
        g_gu = g_gu.reshape(2, -1, D_MODEL)
        w_g_t, w_u_t = g_gu[0], g_gu[1]
        w_out, w_down = g_out.reshape(-1, D_MODEL), g_dn.reshape(-1, D_MODEL)
    o_b, st_b = _gla_fwd(U, lb[1:2], f_block=2, reverse=True, name="gla_fwd_b")
    mix_hg = _hg_post_fwd(o_f, o_b, U, hg_norm_w, name="hg_post_fwd")
    q_c, qn_c, kmax2, k_c, v_c = _att_prep_fwd(U, cos, sin, qw8, kw2, name="att_prep_fwd")
    kmax = jnp.sqrt(jnp.max(kmax2.reshape(ATT_KV, ATT_DH), axis=1))
    m_c = qn_c * (kmax * 1.001).reshape(ATT_KV, 1, 1, 1)
    o_c, lse = lax.cond(jnp.max(m_c) <= FA_BOUND_MAX,
                        lambda: _flash_fwd_bounded(q_c, k_c, v_c, m_c, name="flash_fwd_bounded"),
                        lambda: _flash_fwd(q_c, k_c, v_c, name="flash_fwd"))
    o_att, mix_att = _att_post_fwd(o_c, att_norm_w, name="att_post_fwd")
    x1, h2, r2 = _mm_nn([(mix_hg, w_out[:HG_W]), (mix_att, w_out[HG_W:])], residual=x, tail=_tail_rms_fwd(norm2_w),
                        name="out_proj")
    gate, up, act = _ffn_up(h2, w_g_t, w_u_t, name="ffn_up")
    loss, dx2, dx2b, d_final = _mm_nn([(act, w_down)], residual=x1,
                                      tail=_tail_loss(target, final_norm_w.reshape(1, D_MODEL)), name="ffn_down")

    d_gate, d_up = _ffn_act_bwd(dx2b, w_down, gate, up, name="ffn_act_bwd")
    dw_down = _mm_tn(act, dx2b, tma_cap=1408, name="dw_down")
    dx1, dx1b, d_norm2 = _mm_nn([(d_gate, w_g_t), (d_up, w_u_t)], tm=256,
                                tail=_tail_rms_bwd(x1, r2, norm2_w, dx2, emit_bf16=True), name="ffn_up_bwd")
    dw_g = _mm_tn(d_gate, h2, tma_cap=1408, name="dw_gate")
    dw_u = _mm_tn(d_up, h2, tma_cap=1408, name="dw_up")
    dmix = _mm_nn([(dx1b, w_out)], trans_b=True, name="out_proj_bwd")
    dw_out = jnp.concatenate([_mm_tn(mix_hg, dx1b, name="dw_out_hg"), _mm_tn(mix_att, dx1b, name="dw_out_att")], axis=0)
    do_c, delta, d_att = _att_post_bwd(dmix, o_att, att_norm_w, name="att_post_bwd")
    ride = None if reduce_early is None else reduce_early(dw_out, dw_g, dw_u, dw_down)
    dq_c, dk_c, dv_c, *rode = _flash_bwd(q_c, k_c, v_c, do_c, lse, delta, ride=ride, name="flash_bwd")
    dU_att, d_qn, d_kn = _att_prep_bwd(U, dq_c, dk_c, dv_c, cos, sin, qw8, kw2, name="att_prep_bwd")
    do_hg, du_g, d_hg = _hg_post_bwd(dmix, o_f, o_b, U, hg_norm_w, name="hg_post_bwd")
    dq_f, dz_f, dv_f, dlb_f = _gla_bwd(U, lb[0:1], do_hg, st_f, f_block=1, reverse=False, name="gla_bwd_f")
    dU_hg, dlb_b = _gla_bwd(U, lb[1:2], do_hg, st_b, f_block=2, reverse=True, prev=(dq_f, dz_f, dv_f, du_g),
                            name="gla_bwd_b")
    w_hg = 5 * HG_W
    grad_x, d_norm1 = _mm_nn([(dU_hg, w_in_t[:w_hg]), (dU_att, w_in_t[w_hg:])],
                             tail=_tail_rms_bwd(x, r1, norm1_w, dx1, emit_bf16=False), name="in_proj_bwd")
    dw_in = jnp.concatenate([_mm_tn(dU_hg, h, tma_cap=1280, name="dw_in_hg"), _mm_tn(dU_att, h, name="dw_in_att")],
                            axis=0)
    d_hg, d_qn, d_kn = _fold_heads(d_hg, d_qn, d_kn, name="fold_heads")

    big = dict(w_in=dw_in, w_out=dw_out, w_g=dw_g, w_u=dw_u, w_down=dw_down)
    small = dict(norm1=d_norm1, norm2=d_norm2, final=d_final, att=d_att, hg=d_hg,
                 qn=d_qn[:, :ATT_DH], kn=d_kn[:, :ATT_DH], lb=jnp.concatenate([dlb_f, dlb_b], axis=0))
    return loss, grad_x, big, small, rode


def kernel(x, norm1_w, w_in, lb_logits, hg_norm_w, q_norm_w, k_norm_w, att_norm_w, w_out, norm2_w, w_gate_up, w_down, final_norm_w, loss_target, m_norm1_w, m_w_in, m_lb_logits, m_hg_norm_w, m_q_norm_w, m_k_norm_w, m_att_norm_w, m_w_out, m_norm2_w, m_w_gate_up, m_w_down, m_final_norm_w, v_norm1_w, v_w_in, v_lb_logits, v_hg_norm_w, v_q_norm_w, v_k_norm_w, v_att_norm_w, v_w_out, v_norm2_w, v_w_gate_up, v_w_down, v_final_norm_w):
    T = x.shape[1]
    me = 4 * lax.axis_index("x") + 2 * lax.axis_index("y") + lax.axis_index("c")
    c_in, r_out, c_gu, r_dn = w_in.shape[2], w_out.shape[1], w_gate_up.shape[2], w_down.shape[1]
    lb_cols = lb_logits.shape[2]

    g_in, g_lb = _all_gather([w_in[0].T.astype(BF16), lb_logits.reshape(4, lb_cols)], name="gather_weights")
    w_in_t = g_in.reshape(N_DEV * c_in, D_MODEL)
    shards = dict(w_gu_t=w_gate_up[0].T.astype(BF16), w_out=w_out[0].astype(BF16), w_down=w_down[0].astype(BF16))
    lb_logits_f = g_lb.transpose(1, 0, 2).reshape(2, 2, N_DEV * lb_cols)
    lb = _lower_bounds(lb_logits_f, name="lower_bounds")

    chips = N_DEV // 2
    core = lax.axis_index("c").astype(jnp.int32).reshape(1)
    by_owner = lambda g, r: g.reshape(chips, 2, r, D_MODEL)

    def chip_sums(mine, names, call):
        theirs = _core_swap(mine, name=call)
        return [_pair_sum(g, o, core, name="pair_sum_" + nm) for g, o, nm in zip(mine, theirs, names)]

    def reduce_early(dw_out, dw_g_t, dw_u_t, dw_down):
        return chip_sums([by_owner(dw_out, r_out), by_owner(jnp.concatenate([dw_g_t, dw_u_t], axis=0), c_gu),
                          by_owner(dw_down, r_dn)], ("w_out", "w_gu", "w_down"), "exchange_cores_early")

    loss, grad_x, big, small, (p_out, p_gu, p_dn) = _local_step(
        x[0], loss_target[0], norm1_w, w_in_t, lb, hg_norm_w, q_norm_w, k_norm_w, att_norm_w, None, norm2_w,
        None, None, None, final_norm_w, reduce_early=reduce_early, shards=shards)
    p_gu = p_gu.transpose(0, 2, 1)

    packed = _pack_small(small["norm1"], small["norm2"], small["final"], small["att"], small["hg"],
                         small["qn"], small["kn"], small["lb"], loss)
    p_in, all_small = _exchange(chip_sums([by_owner(big["w_in"], c_in)], ("w_in",), "exchange_cores"),
                                masks=CHIP_MASKS, slot=_chip_slot, bcast=packed, name="exchange_chips")
    p_in = p_in.transpose(0, 2, 1)

    g_w_in, d_w_in, nm_w_in, nv_w_in = _adamw(p_in, w_in[0], m_w_in[0], v_w_in[0], name="adamw_w_in")
    g_w_out, d_w_out, nm_w_out, nv_w_out = _adamw(p_out, w_out[0], m_w_out[0], v_w_out[0], name="adamw_w_out")
    g_w_gu, d_w_gu, nm_w_gu, nv_w_gu = _adamw(p_gu, w_gate_up[0], m_w_gate_up[0], v_w_gate_up[0], name="adamw_w_gu")
    g_w_dn, d_w_dn, nm_w_dn, nv_w_dn = _adamw(p_dn, w_down[0], m_w_down[0], v_w_down[0], name="adamw_w_down")

    pk = lambda vecs: _pack_small(*vecs)
    w_pk = pk([norm1_w, norm2_w, final_norm_w, att_norm_w, hg_norm_w, q_norm_w, k_norm_w])
    m_pk = pk([m_norm1_w, m_norm2_w, m_final_norm_w, m_att_norm_w, m_hg_norm_w, m_q_norm_w, m_k_norm_w])
    v_pk = pk([v_norm1_w, v_norm2_w, v_final_norm_w, v_att_norm_w, v_hg_norm_w, v_q_norm_w, v_k_norm_w])
    g_pk, d_pk, nm_pk, nv_pk = _adamw(all_small, w_pk, m_pk, v_pk, name="adamw_small")

    dlb_sum = g_pk[5:6, :].reshape(2, HG_W)
    g_lb_full = _lb_grad(dlb_sum, lb, name="lb_grad")
    g_lb_mine = lax.dynamic_slice_in_dim(g_lb_full, me * lb_cols, lb_cols, axis=1)
    g_lb_s, d_lb, nm_lb, nv_lb = _adamw(g_lb_mine[None], lb_logits.reshape(4, lb_cols),
                                        m_lb_logits.reshape(4, lb_cols), v_lb_logits.reshape(4, lb_cols),
                                        name="adamw_lb")

    loss_total = g_pk[6, 0]

    def outs(big4, lb_arr, pk_arr):
        n1, n2, fin, att, hg, qn, kn = _unpack_small(pk_arr)
        b_in, b_out, b_gu, b_dn = big4
        return [n1, b_in[None], lb_arr.reshape(2, 2, lb_cols), hg, qn, kn, att, b_out[None], n2, b_gu[None],
                b_dn[None], fin]

    return (loss_total, grad_x[None],
            *outs((g_w_in, g_w_out, g_w_gu, g_w_dn), g_lb_s, g_pk),
            *outs((d_w_in, d_w_out, d_w_gu, d_w_dn), d_lb, d_pk),
            *outs((nm_w_in, nm_w_out, nm_w_gu, nm_w_dn), nm_lb, nm_pk),
            *outs((nv_w_in, nv_w_out, nv_w_gu, nv_w_dn), nv_lb, nv_pk))
```

```python
import math

import jax
import jax.numpy as jnp
import numpy as np
from jax import lax
from jax.experimental import pallas as pl
from jax.experimental.pallas import tpu as pltpu

F32 = jnp.float32
BF16 = jnp.bfloat16

N_DEV = 8
D_MODEL = 1024
EPS = 1e-6
HG_HEADS = 4
HG_D = 128
HG_W = HG_HEADS * HG_D
CHUNK = 64
ATT_HEADS = 8
ATT_KV = 2
ATT_G = ATT_HEADS // ATT_KV
ATT_DH = 64
ATT_QW = ATT_HEADS * ATT_DH
ATT_KW = ATT_KV * ATT_DH
GRID_W = 64
ROPE_THETA = 10000.0
D_FF = 2816
ADAM_LR, ADAM_B1, ADAM_B2, ADAM_EPS, ADAM_WD, ADAM_STEP = 0.001, 0.9, 0.999, 1e-08, 0.01, 10

LANES = 128
VMEM_LIMIT = 48 * 1024 * 1024
MESH = pl.DeviceIdType.MESH
ANY = pl.BlockSpec(memory_space=pl.ANY)


def _params(sem=None):
    return pltpu.CompilerParams(dimension_semantics=sem, vmem_limit_bytes=VMEM_LIMIT)


def _pick(n, cap):
    best = None
    for t in range(LANES, cap + 1, LANES):
        if n % t == 0:
            best = t
    assert best is not None, (n, cap)
    return best


def _sigmoid(x):
    return 1.0 / (1.0 + jnp.exp(-x))


def _dot(a, b):
    return jnp.dot(a.astype(BF16), b.astype(BF16), preferred_element_type=F32)


def _dot_nt(a, b):
    return lax.dot_general(a.astype(BF16), b.astype(BF16), (((1,), (1,)), ((), ())),
                           preferred_element_type=F32)


def _dot_tn(a, b):
    return lax.dot_general(a.astype(BF16), b.astype(BF16), (((0,), (0,)), ((), ())),
                           preferred_element_type=F32)


def _mm_nn(pairs, *, name, out_dtype=F32, residual=None, tm=512, tn_cap=None, trans_b=False, tail=None, ride=None):
    M = pairs[0][0].shape[0]
    N = pairs[0][1].shape[0 if trans_b else 1]
    tn = N if tn_cap is None else _pick(N, tn_cap)
    n_pairs = len(pairs)
    has_res = residual is not None
    dims = (((1,), (1,)), ((), ())) if trans_b else (((1,), (0,)), ((), ()))
    assert (tail is None and ride is None) or tn == N
    n_main = 2 * n_pairs + has_res
    n_ti = 0 if tail is None else len(tail["ins"])
    n_out = 1 if tail is None else len(tail["outs"])
    n_r = 0 if ride is None else len(ride["arrays"])
    n_in = n_main + n_ti + n_r

    def body(*refs):
        outs = refs[n_in:n_in + n_out]
        if n_r:
            start, finish = ride["halves"](refs[n_main + n_ti:n_in], refs[n_in + n_out:n_in + n_out + n_r],
                                           *refs[n_in + n_out + n_r:])
            pl.when(pl.program_id(0) == 0)(start)
        acc = None
        for i in range(n_pairs):
            d = lax.dot_general(refs[2 * i][...], refs[2 * i + 1][...], dims, preferred_element_type=F32)
            acc = d if acc is None else acc + d
        if has_res:
            acc = acc + refs[2 * n_pairs][...]
        if tail is None:
            outs[0][...] = acc.astype(out_dtype)
        else:
            tail["fn"](acc, pl.program_id(0) == 0, *refs[n_main:n_main + n_ti], *outs)
        if n_r:
            pl.when(pl.program_id(0) == M // tm - 1)(finish)

    kinds = {"row": ((tm, N), (M, N), lambda i, j: (i, 0)), "col": ((tm, 1), (M, 1), lambda i, j: (i, 0)),
             "vec": ((1, N), (1, N), lambda i, j: (0, 0)), "one": ((1, 1), (1, 1), lambda i, j: (0, 0))}
    in_specs, args = [], []
    for a, b in pairs:
        k = a.shape[1]
        b_spec = pl.BlockSpec((tn, k), lambda i, j: (j, 0)) if trans_b else pl.BlockSpec((k, tn), lambda i, j: (0, j))
        in_specs += [pl.BlockSpec((tm, k), lambda i, j: (i, 0)), b_spec]
        args += [a, b]
    if has_res:
        in_specs.append(pl.BlockSpec((tm, tn), lambda i, j: (i, j)))
        args.append(residual)
    if tail is None:
        out_specs = [pl.BlockSpec((tm, tn), lambda i, j: (i, j))]
        out_shape = [jax.ShapeDtypeStruct((M, N), out_dtype)]
    else:
        for arr, kind in tail["ins"]:
            in_specs.append(pl.BlockSpec(kinds[kind][0], kinds[kind][2]))
            args.append(arr)
        out_specs = [pl.BlockSpec(kinds[kind][0], kinds[kind][2]) for _, kind in tail["outs"]]
        out_shape = [jax.ShapeDtypeStruct(kinds[kind][1], dt) for dt, kind in tail["outs"]]
    scratch = []
    if n_r:
        in_specs += [ANY] * n_r
        args += ride["arrays"]
        out_specs += [ANY] * n_r
        out_shape += ride["out_shape"]
        scratch = ride["scratch"]
    sequential = tail is not None or n_r > 0
    res = pl.pallas_call(
        body, name=name, grid=(M // tm, N // tn), in_specs=in_specs, out_specs=out_specs, out_shape=out_shape,
        scratch_shapes=scratch,
        compiler_params=pltpu.CompilerParams(dimension_semantics=("arbitrary" if sequential else "parallel", "arbitrary"),
                                             vmem_limit_bytes=VMEM_LIMIT, has_side_effects=n_r > 0),
    )(*args)
    return res[0] if len(res) == 1 else res


def _mm_tn(a, b, *, name, tma_cap=1024, tnb_cap=1024, tk=1024):
    T, Ma = a.shape
    Nb = b.shape[1]
    tma, tnb = _pick(Ma, tma_cap), _pick(Nb, tnb_cap)
    tk = min(tk, T)
    n_k = T // tk

    def body(a_ref, b_ref, o_ref, acc_ref):
        k = pl.program_id(2)

        @pl.when(k == 0)
        def _():
            acc_ref[...] = jnp.zeros_like(acc_ref)

        acc_ref[...] += lax.dot_general(a_ref[...], b_ref[...], (((0,), (0,)), ((), ())),
                                        preferred_element_type=F32)

        @pl.when(k == n_k - 1)
        def _():
            o_ref[...] = acc_ref[...]

    return pl.pallas_call(
        body, name=name, grid=(Ma // tma, Nb // tnb, n_k),
        in_specs=[pl.BlockSpec((tk, tma), lambda i, j, k: (k, i)), pl.BlockSpec((tk, tnb), lambda i, j, k: (k, j))],
        out_specs=pl.BlockSpec((tma, tnb), lambda i, j, k: (i, j)),
        out_shape=jax.ShapeDtypeStruct((Ma, Nb), F32),
        scratch_shapes=[pltpu.VMEM((tma, tnb), F32)],
        compiler_params=_params(("parallel", "parallel", "arbitrary")),
    )(a, b)


def _rms_fwd(x, w, *, name, tm=512):
    T, Dm = x.shape

    def body(x_ref, w_ref, h_ref, r_ref):
        xv = x_ref[...]
        r = lax.rsqrt(jnp.mean(xv * xv, axis=-1, keepdims=True) + EPS)
        h_ref[...] = (xv * r * w_ref[...]).astype(BF16)
        r_ref[...] = r

    return pl.pallas_call(
        body, name=name, grid=(T // tm,),
        in_specs=[pl.BlockSpec((tm, Dm), lambda i: (i, 0)), pl.BlockSpec((1, Dm), lambda i: (0, 0))],
        out_specs=[pl.BlockSpec((tm, Dm), lambda i: (i, 0)), pl.BlockSpec((tm, 1), lambda i: (i, 0))],
        out_shape=[jax.ShapeDtypeStruct((T, Dm), BF16), jax.ShapeDtypeStruct((T, 1), F32)],
        compiler_params=_params(("parallel",)),
    )(x, w)


def _tail_rms_fwd(w):
    def fn(xv, first, w_ref, x_ref, h_ref, r_ref):
        r = lax.rsqrt(jnp.mean(xv * xv, axis=-1, keepdims=True) + EPS)
        x_ref[...] = xv
        h_ref[...] = (xv * r * w_ref[...]).astype(BF16)
        r_ref[...] = r

    return dict(fn=fn, ins=[(w, "vec")], outs=[(F32, "row"), (BF16, "row"), (F32, "col")])


def _tail_rms_bwd(x, r, w, dres, *, emit_bf16):
    def fn(dhv, first, x_ref, r_ref, w_ref, dres_ref, *outs):
        dx_ref, dw_ref = outs[0], outs[-1]

        @pl.when(first)
        def _():
            dw_ref[...] = jnp.zeros_like(dw_ref)

        rv = r_ref[...]
        xh = x_ref[...] * rv
        dxh = dhv * w_ref[...]
        t = jnp.mean(dxh * xh, axis=-1, keepdims=True)
        dx = dres_ref[...] + rv * (dxh - xh * t)
        dx_ref[...] = dx
        if emit_bf16:
            outs[1][...] = dx.astype(BF16)
        dw_ref[...] += jnp.sum(dhv * xh, axis=0, keepdims=True)

    outs = [(F32, "row")] + ([(BF16, "row")] if emit_bf16 else []) + [(F32, "vec")]
    return dict(fn=fn, ins=[(x, "row"), (r, "col"), (w, "vec"), (dres, "row")], outs=outs)


def _tail_loss(target, w):
    def fn(xv, first, t_ref, w_ref, loss_ref, dx_ref, dxb_ref, dw_ref):
        @pl.when(first)
        def _():
            loss_ref[...] = jnp.zeros_like(loss_ref)
            dw_ref[...] = jnp.zeros_like(dw_ref)

        r = lax.rsqrt(jnp.mean(xv * xv, axis=-1, keepdims=True) + EPS)
        xh = xv * r
        wv = w_ref[...]
        err = xh * wv - t_ref[...]
        row_loss = jnp.mean(err * err, axis=-1, keepdims=True)
        loss_ref[...] += 0.5 * jnp.sum(row_loss, axis=0, keepdims=True)
        dy = err * (1.0 / xv.shape[-1])
        dxh = dy * wv
        t = jnp.mean(dxh * xh, axis=-1, keepdims=True)
        dx = r * (dxh - xh * t)
        dx_ref[...] = dx
        dxb_ref[...] = dx.astype(BF16)
        dw_ref[...] += jnp.sum(dy * xh, axis=0, keepdims=True)

    return dict(fn=fn, ins=[(target, "row"), (w, "vec")],
                outs=[(F32, "one"), (F32, "row"), (BF16, "row"), (F32, "vec")])


GLA_TB = 512
GLA_NC = GLA_TB // CHUNK
GLA_UNROLL = 4


def _cumsum_rows(x, row, reverse):
    n = x.shape[0]
    s = 1
    while s < n:
        if not reverse:
            x = x + jnp.where(row >= s, pltpu.roll(x, s, 0), 0.0)
        else:
            x = x + jnp.where(row < n - s, pltpu.roll(x, n - s, 0), 0.0)
        s *= 2
    return x


def _gla_gates(uq, z, lbv):
    q = uq * _sigmoid(uq)
    sg = _sigmoid(z)
    sgn = _sigmoid(-z)
    f = lbv + (1.0 - lbv) * sg
    k = (1.0 - lbv) * sgn
    return q, sg, sgn, f, k


def _gla_decays(f, row, reverse):
    b = _cumsum_rows(jnp.log(f), row, reverse)
    if not reverse:
        bref, blast = b[CHUNK // 2 - 1:CHUNK // 2, :], b[CHUNK - 1:CHUNK, :]
    else:
        bref, blast = b[CHUNK // 2:CHUNK // 2 + 1, :], b[0:1, :]
    return b, bref, blast


def _gla_fwd(U, lb, *, f_block, reverse, name, ride=None):
    T = U.shape[0]
    nb = T // GLA_TB
    n_g = 0 if ride is None else len(ride["arrays"])

    def body(uq_ref, uf_ref, ui_ref, lb_ref, *rest):
        g_in, rest = rest[:n_g], rest[n_g:]
        o_ref, st_ref = rest[:2]
        g_out, rest = rest[2:2 + n_g], rest[2 + n_g:]
        s_ref = rest[0]
        if n_g:
            start, finish = ride["halves"](g_in, g_out, *rest[1:])
            pl.when(pl.program_id(0) == 0)(start)

        @pl.when(pl.program_id(0) == 0)
        def _():
            s_ref[...] = jnp.zeros_like(s_ref)

        row = lax.broadcasted_iota(jnp.int32, (CHUNK, HG_D), 0)
        ri = lax.broadcasted_iota(jnp.int32, (CHUNK, CHUNK), 0)
        ci = lax.broadcasted_iota(jnp.int32, (CHUNK, CHUNK), 1)
        mask = (ri <= ci) if reverse else (ri >= ci)

        def chunk(j, carry):
            c = (GLA_NC - 1 - j) if reverse else j
            rows = pl.ds(pl.multiple_of(c * CHUNK, CHUNK), CHUNK)
            for h in range(HG_HEADS):
                cols = pl.ds(h * HG_D, HG_D)
                v = ui_ref[rows, cols]
                q, _, _, f, k = _gla_gates(uq_ref[rows, cols], uf_ref[rows, cols], lb_ref[:, cols])
                b, bref, blast = _gla_decays(f, row, reverse)
                s = jnp.where(mask, _dot_nt(q * jnp.exp(b - bref), k * jnp.exp(bref - b)), 0.0)
                st = s_ref[h]
                st_ref[c, h] = st
                o_ref[rows, cols] = _dot(s, v) + _dot_nt(q * jnp.exp(b), st)
                s_ref[h] = st * jnp.exp(blast) + _dot_tn(v, k * jnp.exp(blast - b))
            return carry

        lax.fori_loop(0, GLA_NC, chunk, 0, unroll=GLA_NC)
        if n_g:
            pl.when(pl.program_id(0) == nb - 1)(finish)

    blk = (lambda i: nb - 1 - i) if reverse else (lambda i: i)
    ucol = lambda cb: pl.BlockSpec((GLA_TB, HG_W), lambda i: (blk(i), cb))
    return pl.pallas_call(
        body, name=name, grid=(nb,),
        in_specs=[ucol(0), ucol(f_block), ucol(3), pl.BlockSpec((1, HG_W), lambda i: (0, 0))] + [ANY] * n_g,
        out_specs=[pl.BlockSpec((GLA_TB, HG_W), lambda i: (blk(i), 0)),
                   pl.BlockSpec((GLA_NC, HG_HEADS, HG_D, HG_D), lambda i: (blk(i), 0, 0, 0))] + [ANY] * n_g,
        out_shape=[jax.ShapeDtypeStruct((T, HG_W), F32),
                   jax.ShapeDtypeStruct((T // CHUNK, HG_HEADS, HG_D, HG_D), F32)] + (ride["out_shape"] if n_g else []),
        scratch_shapes=[pltpu.VMEM((HG_HEADS, HG_D, HG_D), F32)] + (ride["scratch"] if n_g else []),
        compiler_params=pltpu.CompilerParams(dimension_semantics=("arbitrary",), vmem_limit_bytes=VMEM_LIMIT,
                                             has_side_effects=bool(n_g)),
    )(U, U, U, lb, *(ride["arrays"] if n_g else []))


def _gla_bwd(U, lb, do, states, *, f_block, reverse, name, prev=None):
    T = U.shape[0]
    nb = T // GLA_TB
    final = prev is not None

    def body(uq_ref, uf_ref, ui_ref, lb_ref, do_ref, st_ref, *rest):
        if final:
            dqp_ref, dzp_ref, dvp_ref, dug_ref, out_ref, dlb_ref, ds_ref = rest
        else:
            dq_ref, dz_ref, dv_ref, dlb_ref, ds_ref = rest

        @pl.when(pl.program_id(0) == 0)
        def _():
            ds_ref[...] = jnp.zeros_like(ds_ref)
            dlb_ref[...] = jnp.zeros_like(dlb_ref)

        row = lax.broadcasted_iota(jnp.int32, (CHUNK, HG_D), 0)
        ri = lax.broadcasted_iota(jnp.int32, (CHUNK, CHUNK), 0)
        ci = lax.broadcasted_iota(jnp.int32, (CHUNK, CHUNK), 1)
        mask = (ri <= ci) if reverse else (ri >= ci)

        def chunk(j, carry):
            c = j if reverse else (GLA_NC - 1 - j)
            rows = pl.ds(pl.multiple_of(c * CHUNK, CHUNK), CHUNK)
            for h in range(HG_HEADS):
                cols = pl.ds(h * HG_D, HG_D)
                v = ui_ref[rows, cols]
                lbv = lb_ref[:, cols]
                uq = uq_ref[rows, cols]
                q, sg, sgn, f, k = _gla_gates(uq, uf_ref[rows, cols], lbv)
                b, bref, blast = _gla_decays(f, row, reverse)
                eq, ek, eb, el, dec = (jnp.exp(b - bref), jnp.exp(bref - b), jnp.exp(b), jnp.exp(blast - b),
                                       jnp.exp(blast))
                qin, kin, qb, klast = q * eq, k * ek, q * eb, k * el
                dov = do_ref[rows, cols]
                st = st_ref[c, h]
                dst = ds_ref[h]
                p = jnp.where(mask, _dot_nt(qin, kin), 0.0)
                dp = jnp.where(mask, _dot_nt(dov, v), 0.0)
                dqin = _dot(dp, kin)
                dkin = _dot_tn(dp, qin)
                dv = _dot_tn(p, dov) + _dot_nt(klast, dst)
                dqb = _dot(dov, st)
                dklast = _dot(v, dst)
                ds_ref[h] = _dot_tn(dov, qb) + dst * dec
                db = dqin * qin - dkin * kin + dqb * qb - dklast * klast
                extra = (jnp.sum(dklast * klast, axis=0, keepdims=True)
                         + dec * jnp.sum(st * dst, axis=0, keepdims=True))
                dg = _cumsum_rows(db, row, not reverse) + extra
                dq = dqin * eq + dqb * eb
                dk = dkin * ek + dklast * el
                dfk = dg / f - dk
                dz = (dfk * (1.0 - lbv) * sg * sgn).astype(BF16)
                dlb_ref[:, cols] += jnp.sum(dfk * sgn, axis=0, keepdims=True)
                if final:
                    sq = _sigmoid(uq)
                    col = lambda blk: pl.ds(blk * HG_W + h * HG_D, HG_D)
                    out_ref[rows, col(0)] = ((dq + dqp_ref[rows, cols]) * (sq * (1.0 + uq * (1.0 - sq)))).astype(BF16)
                    out_ref[rows, col(1)] = dzp_ref[rows, cols]
                    out_ref[rows, col(2)] = dz
                    out_ref[rows, col(3)] = (dv + dvp_ref[rows, cols]).astype(BF16)
                    out_ref[rows, col(4)] = dug_ref[rows, cols]
                else:
                    dq_ref[rows, cols] = dq
                    dz_ref[rows, cols] = dz
                    dv_ref[rows, cols] = dv
            return carry

        lax.fori_loop(0, GLA_NC, chunk, 0, unroll=GLA_UNROLL)

    blk = (lambda i: i) if reverse else (lambda i: nb - 1 - i)
    ucol = lambda cb: pl.BlockSpec((GLA_TB, HG_W), lambda i: (blk(i), cb))
    tok = pl.BlockSpec((GLA_TB, HG_W), lambda i: (blk(i), 0))
    vec = pl.BlockSpec((1, HG_W), lambda i: (0, 0))
    in_specs = [ucol(0), ucol(f_block), ucol(3), vec, tok,
                pl.BlockSpec((GLA_NC, HG_HEADS, HG_D, HG_D), lambda i: (blk(i), 0, 0, 0))]
    vec_shape = jax.ShapeDtypeStruct((1, HG_W), F32)
    if final:
        in_specs += [tok] * 4
        out_specs = [pl.BlockSpec((GLA_TB, 5 * HG_W), lambda i: (blk(i), 0)), vec]
        out_shape = [jax.ShapeDtypeStruct((T, 5 * HG_W), BF16), vec_shape]
    else:
        out_specs = [tok, tok, tok, vec]
        out_shape = [jax.ShapeDtypeStruct((T, HG_W), F32), jax.ShapeDtypeStruct((T, HG_W), BF16),
                     jax.ShapeDtypeStruct((T, HG_W), F32), vec_shape]
    return pl.pallas_call(
        body, name=name, grid=(nb,), in_specs=in_specs, out_specs=out_specs, out_shape=out_shape,
        scratch_shapes=[pltpu.VMEM((HG_HEADS, HG_D, HG_D), F32)],
        compiler_params=_params(("arbitrary",)),
    )(U, U, U, lb, do, states, *(prev if final else ()))


def _hg_post_fwd(o_f, o_b, U, w, *, name, tm=512):
    T = o_f.shape[0]

    def body(of_ref, ob_ref, ug_ref, w_ref, out_ref):
        wv = w_ref[...]
        for h in range(HG_HEADS):
            cols = pl.ds(h * HG_D, HG_D)
            o = of_ref[:, cols] + ob_ref[:, cols]
            r = lax.rsqrt(jnp.mean(o * o, axis=-1, keepdims=True) + EPS)
            ug = ug_ref[:, cols]
            out_ref[:, cols] = (o * r * wv * (ug * _sigmoid(ug))).astype(BF16)

    tok = pl.BlockSpec((tm, HG_W), lambda i: (i, 0))
    return pl.pallas_call(
        body, name=name, grid=(T // tm,),
        in_specs=[tok, tok, pl.BlockSpec((tm, HG_W), lambda i: (i, 4)), pl.BlockSpec((1, HG_D), lambda i: (0, 0))],
        out_specs=tok, out_shape=jax.ShapeDtypeStruct((T, HG_W), BF16),
        compiler_params=_params(("parallel",)),
    )(o_f, o_b, U, w)


def _hg_post_bwd(dmix, o_f, o_b, U, w, *, name, tm=512):
    T = o_f.shape[0]

    def body(dm_ref, of_ref, ob_ref, ug_ref, w_ref, do_ref, dug_ref, dw_ref):
        @pl.when(pl.program_id(0) == 0)
        def _():
            dw_ref[...] = jnp.zeros_like(dw_ref)

        wv = w_ref[...]
        for h in range(HG_HEADS):
            cols = pl.ds(h * HG_D, HG_D)
            o = of_ref[:, cols] + ob_ref[:, cols]
            r = lax.rsqrt(jnp.mean(o * o, axis=-1, keepdims=True) + EPS)
            xh = o * r
            ug = ug_ref[:, cols]
            sg = _sigmoid(ug)
            dm = dm_ref[:, cols]
            dn = dm * (ug * sg)
            dug_ref[:, cols] = (dm * (xh * wv) * (sg * (1.0 + ug * (1.0 - sg)))).astype(BF16)
            dxh = dn * wv
            t = jnp.mean(dxh * xh, axis=-1, keepdims=True)
            do_ref[:, cols] = r * (dxh - xh * t)
            dw_ref[:, cols] += jnp.sum(dn * xh, axis=0, keepdims=True)

    tok = pl.BlockSpec((tm, HG_W), lambda i: (i, 0))
    vec = pl.BlockSpec((1, HG_W), lambda i: (0, 0))
    return pl.pallas_call(
        body, name=name, grid=(T // tm,),
        in_specs=[tok, tok, tok, pl.BlockSpec((tm, HG_W), lambda i: (i, 4)), pl.BlockSpec((1, HG_D), lambda i: (0, 0))],
        out_specs=[tok, tok, vec],
        out_shape=[jax.ShapeDtypeStruct((T, HG_W), F32), jax.ShapeDtypeStruct((T, HG_W), BF16),
                   jax.ShapeDtypeStruct((1, HG_W), F32)],
        compiler_params=_params(("arbitrary",)),
    )(dmix, o_f, o_b, U, w)


def _rope_tables(T):
    rows = T // GRID_W
    row = np.repeat(np.arange(rows), GRID_W).astype(np.float32)
    col = np.tile(np.arange(GRID_W), rows).astype(np.float32)
    axis_dim = ATT_DH // 2
    freqs = (np.float32(ROPE_THETA) ** (-np.arange(0, axis_dim, 2, dtype=np.float32) / np.float32(axis_dim))
             ).astype(np.float32)
    ang = np.concatenate([row[:, None] * freqs, col[:, None] * freqs], axis=-1).astype(np.float32)
    cos, sin = np.cos(ang), np.sin(ang)
    c = np.repeat(cos, 2, axis=-1)
    s = np.stack([-sin, sin], axis=-1).reshape(T, ATT_DH)
    return jnp.asarray(np.tile(c, (1, 2)), F32), jnp.asarray(np.tile(s, (1, 2)), F32)


def _head_blockdiag(width):
    shift = ATT_DH.bit_length() - 1
    ri = jnp.right_shift(lax.broadcasted_iota(jnp.int32, (width, width), 0), shift)
    ci = jnp.right_shift(lax.broadcasted_iota(jnp.int32, (width, width), 1), shift)
    return jnp.where(ri == ci, 1.0, 0.0).astype(BF16)


def _head_sum(x, bd):
    hi = x.astype(BF16)
    lo = (x - hi.astype(F32)).astype(BF16)
    return jnp.dot(hi, bd, preferred_element_type=F32) + jnp.dot(lo, bd, preferred_element_type=F32)


def _pair_swap(x, even):
    n = x.shape[-1]
    return jnp.where(even, pltpu.roll(x, n - 1, 1), pltpu.roll(x, 1, 1))


FA_TQ = 512


FA_TK = 512


def _cols_from_tokens(x, kv):
    w = ATT_G * ATT_DH
    xt = x[:, kv * w:(kv + 1) * w].T
    return jnp.concatenate([xt[g * ATT_DH:(g + 1) * ATT_DH, :] for g in range(ATT_G)], axis=1)


def _tokens_from_cols(c):
    tq = c.shape[1] // ATT_G
    return jnp.concatenate([c[:, g * tq:(g + 1) * tq] for g in range(ATT_G)], axis=0).T


def _store_cols(ref, x, norm_ref=None):
    for kv in range(ATT_KV):
        cols = _cols_from_tokens(x, kv).astype(BF16)
        ref[kv, 0] = cols
        if norm_ref is not None:
            cf = cols.astype(F32)
            norm_ref[kv, 0] = jnp.sqrt(jnp.sum(cf * cf, axis=0, keepdims=True))


def _att_prep_fwd(U, cos, sin, qw, kw, *, name):
    T = U.shape[0]
    tm = min(FA_TQ, T)
    R = ATT_G * tm
    scale = ATT_DH ** -0.5

    def head_rows(ref, x):
        xt = x.astype(F32).T
        for kv in range(ATT_KV):
            ref[kv, 0] = xt[kv * ATT_DH:(kv + 1) * ATT_DH, :].astype(BF16)

    def body(aq_ref, ak_ref, av_ref, c_ref, s_ref, qw_ref, kw_ref, q_ref, qn_ref, kmax_ref, kc_ref, vc_ref):
        @pl.when(pl.program_id(0) == 0)
        def _():
            kmax_ref[...] = jnp.zeros_like(kmax_ref)

        bd = _head_blockdiag(ATT_QW)
        c2, s2 = c_ref[...], s_ref[...]
        c8, s8 = jnp.tile(c2, (1, 4)), jnp.tile(s2, (1, 4))

        def norm_rope(x, w, c, s, bdm):
            r = lax.rsqrt(_head_sum(x * x, bdm) * (1.0 / ATT_DH) + EPS)
            y = x * r * w
            even = (lax.broadcasted_iota(jnp.int32, y.shape, 1) & 1) == 0
            return y * c + _pair_swap(y, even) * s

        _store_cols(q_ref, norm_rope(aq_ref[...], qw_ref[...], c8, s8, bd) * scale, qn_ref)
        kb = norm_rope(ak_ref[...], kw_ref[...], c2, s2, bd[:ATT_KW, :ATT_KW]).astype(BF16)
        kf = kb.astype(F32)
        ksq = _head_sum(kf * kf, bd[:ATT_KW, :ATT_KW])
        kmax_ref[...] = jnp.maximum(kmax_ref[...], jnp.max(ksq, axis=0, keepdims=True))
        head_rows(kc_ref, kb)
        head_rows(vc_ref, av_ref[...].astype(BF16))

    kv_spec = pl.BlockSpec((tm, ATT_KW), lambda i: (i, 0))
    tk = min(FA_TK, T)
    per = tk // tm
    c_spec = pl.BlockSpec((ATT_KV, 1, ATT_DH, tm), lambda i: (0, i // per, 0, i % per))
    c_shape = jax.ShapeDtypeStruct((ATT_KV, T // tk, ATT_DH, tk), BF16)
    return pl.pallas_call(
        body, name=name, grid=(T // tm,),
        in_specs=[pl.BlockSpec((tm, ATT_QW), lambda i: (i, 5)),
                  pl.BlockSpec((tm, ATT_KW), lambda i: (i, 24)), pl.BlockSpec((tm, ATT_KW), lambda i: (i, 25)),
                  kv_spec, kv_spec,
                  pl.BlockSpec((1, ATT_QW), lambda i: (0, 0)), pl.BlockSpec((1, ATT_KW), lambda i: (0, 0))],
        out_specs=[pl.BlockSpec((ATT_KV, 1, ATT_DH, R), lambda i: (0, i, 0, 0)),
                   pl.BlockSpec((ATT_KV, 1, 1, R), lambda i: (0, i, 0, 0)), pl.BlockSpec((1, ATT_KW), lambda i: (0, 0)),
                   c_spec, c_spec],
        out_shape=[jax.ShapeDtypeStruct((ATT_KV, T // tm, ATT_DH, R), BF16),
                   jax.ShapeDtypeStruct((ATT_KV, T // tm, 1, R), F32), jax.ShapeDtypeStruct((1, ATT_KW), F32),
                   c_shape, c_shape],
        compiler_params=_params(("arbitrary",)),
    )(U, U, U, cos, sin, qw, kw)


def _att_prep_bwd(U, dq_c, dk_c, dv_c, cos, sin, qw, kw, *, name):
    T = U.shape[0]
    tm = min(FA_TQ, T)
    R = ATT_G * tm
    scale = ATT_DH ** -0.5

    def body(aq_ref, ak_ref, dq_ref, dk_ref, dv_ref, c_ref, s_ref, qw_ref, kw_ref, out_ref, dqw_ref, dkw_ref):
        @pl.when(pl.program_id(0) == 0)
        def _():
            dqw_ref[...] = jnp.zeros_like(dqw_ref)
            dkw_ref[...] = jnp.zeros_like(dkw_ref)

        bd = _head_blockdiag(ATT_QW)
        c2, s2 = c_ref[...], s_ref[...]
        c8, s8 = jnp.tile(c2, (1, 4)), jnp.tile(s2, (1, 4))

        def bwd(x, dy, w, c, s, bdm):
            even = (lax.broadcasted_iota(jnp.int32, x.shape, 1) & 1) == 0
            dn = dy * c - _pair_swap(dy, even) * s
            r = lax.rsqrt(_head_sum(x * x, bdm) * (1.0 / ATT_DH) + EPS)
            xh = x * r
            dxh = dn * w
            t = _head_sum(dxh * xh, bdm) * (1.0 / ATT_DH)
            return r * (dxh - xh * t), jnp.sum(dn * xh, axis=0, keepdims=True)

        dq = jnp.concatenate([_tokens_from_cols(dq_ref[kv, 0]) for kv in range(ATT_KV)], axis=1)
        da, dw = bwd(aq_ref[...], dq * scale, qw_ref[...], c8, s8, bd)
        out_ref[:, 0:ATT_QW] = da.astype(BF16)
        dqw_ref[...] += dw
        tokens = lambda ref: jnp.concatenate([ref[kv, 0] for kv in range(ATT_KV)], axis=0).T
        da, dw = bwd(ak_ref[...], tokens(dk_ref), kw_ref[...], c2, s2, bd[:ATT_KW, :ATT_KW])
        out_ref[:, ATT_QW:ATT_QW + ATT_KW] = da.astype(BF16)
        dkw_ref[...] += dw
        out_ref[:, ATT_QW + ATT_KW:ATT_QW + 2 * ATT_KW] = tokens(dv_ref).astype(BF16)

    kv_spec = pl.BlockSpec((tm, ATT_KW), lambda i: (i, 0))
    qv = pl.BlockSpec((1, ATT_QW), lambda i: (0, 0))
    kv = pl.BlockSpec((1, ATT_KW), lambda i: (0, 0))
    w_att = ATT_QW + 2 * ATT_KW
    per = dk_c.shape[3] // tm
    c_spec = pl.BlockSpec((ATT_KV, 1, ATT_DH, tm), lambda i: (0, i // per, 0, i % per))
    return pl.pallas_call(
        body, name=name, grid=(T // tm,),
        in_specs=[pl.BlockSpec((tm, ATT_QW), lambda i: (i, 5)), pl.BlockSpec((tm, ATT_KW), lambda i: (i, 24)),
                  pl.BlockSpec((ATT_KV, 1, ATT_DH, R), lambda i: (0, i, 0, 0)), c_spec, c_spec, kv_spec, kv_spec, qv, kv],
        out_specs=[pl.BlockSpec((tm, w_att), lambda i: (i, 0)), qv, kv],
        out_shape=[jax.ShapeDtypeStruct((T, w_att), BF16),
                   jax.ShapeDtypeStruct((1, ATT_QW), F32), jax.ShapeDtypeStruct((1, ATT_KW), F32)],
        compiler_params=_params(("arbitrary",)),
    )(U, U, dq_c, dk_c, dv_c, cos, sin, qw, kw)


def _scores(k_ref, j, qv):
    return lax.dot_general(k_ref[0, j], qv, (((0,), (0,)), ((), ())), preferred_element_type=F32)


def _flash_fwd(q_c, k_c, v_c, *, name):
    _, nq, _, R = q_c.shape
    _, n_k, _, tk = v_c.shape

    def body(q_ref, k_ref, v_ref, o_ref, lse_ref, acc_ref):
        qv = q_ref[0, 0]
        acc_ref[...] = jnp.zeros_like(acc_ref)

        def step(j, carry):
            m, l = carry
            s = _scores(k_ref, j, qv)
            m_new = jnp.maximum(m, jnp.max(s, axis=0, keepdims=True))
            alpha = jnp.exp(m - m_new)
            p = jnp.exp(s - m_new)
            l = alpha * l + jnp.sum(p, axis=0, keepdims=True)
            acc_ref[...] = alpha * acc_ref[...] + jnp.dot(v_ref[0, j], p.astype(BF16), preferred_element_type=F32)
            return m_new, l

        m, l = lax.fori_loop(0, n_k, step, (jnp.full((1, R), -jnp.inf, F32), jnp.zeros((1, R), F32)))
        o_ref[0, 0] = acc_ref[...] / l
        lse_ref[0, 0] = m + jnp.log(l)

    cspec = pl.BlockSpec((1, 1, ATT_DH, R), lambda h, i: (h, i, 0, 0))
    kspec = pl.BlockSpec((1, n_k, ATT_DH, tk), lambda h, i: (h, 0, 0, 0))
    return pl.pallas_call(
        body, name=name, grid=(ATT_KV, nq),
        in_specs=[cspec, kspec, kspec],
        out_specs=[cspec, pl.BlockSpec((1, 1, 1, R), lambda h, i: (h, i, 0, 0))],
        out_shape=[jax.ShapeDtypeStruct((ATT_KV, nq, ATT_DH, R), F32), jax.ShapeDtypeStruct((ATT_KV, nq, 1, R), F32)],
        scratch_shapes=[pltpu.VMEM((ATT_DH, R), F32)],
        compiler_params=_params(("parallel", "parallel")),
    )(q_c, k_c, v_c)


FA_BOUND_MAX = 40.0


def _flash_fwd_bounded(q_c, k_c, v_c, m_c, *, name):
    _, nq, _, R = q_c.shape
    _, n_k, _, tk = v_c.shape

    def body(q_ref, k_ref, v_ref, m_ref, o_ref, lse_ref, acc_ref):
        qv = q_ref[0, 0]
        m = m_ref[0, 0]
        acc_ref[...] = jnp.zeros_like(acc_ref)

        per = math.gcd(n_k, 4)

        def step(jj, l8):
            pv = None
            for u in range(per):
                j = per * jj + u
                p = jnp.exp(_scores(k_ref, j, qv) - m)
                l8 = l8 + jnp.sum(p.reshape(tk // 8, 8, R), axis=0)
                d = jnp.dot(v_ref[0, j], p.astype(BF16), preferred_element_type=F32)
                pv = d if pv is None else pv + d
            acc_ref[...] += pv
            return l8

        l8 = lax.fori_loop(0, n_k // per, step, jnp.zeros((8, R), F32))
        l = jnp.sum(l8, axis=0, keepdims=True)
        o_ref[0, 0] = acc_ref[...] / l
        lse_ref[0, 0] = m + jnp.log(l)

    cspec = pl.BlockSpec((1, 1, ATT_DH, R), lambda h, i: (h, i, 0, 0))
    kspec = pl.BlockSpec((1, n_k, ATT_DH, tk), lambda h, i: (h, 0, 0, 0))
    vspec = pl.BlockSpec((1, 1, 1, R), lambda h, i: (h, i, 0, 0))
    return pl.pallas_call(
        body, name=name, grid=(ATT_KV, nq),
        in_specs=[cspec, kspec, kspec, vspec],
        out_specs=[cspec, vspec],
        out_shape=[jax.ShapeDtypeStruct((ATT_KV, nq, ATT_DH, R), F32), jax.ShapeDtypeStruct((ATT_KV, nq, 1, R), F32)],
        scratch_shapes=[pltpu.VMEM((ATT_DH, R), F32)],
        compiler_params=_params(("parallel", "parallel")),
    )(q_c, k_c, v_c, m_c)


CHIP_MASKS = [(1, 0, 0), (0, 1, 0), (1, 1, 0)]


def _chip_slot(p):
    return 2 * p[0] + p[1]


def _flash_bwd(q_c, k_c, v_c, do_c, lse, delta, *, name, ride=None):
    _, nq, _, R = q_c.shape
    _, n_k, _, tk = k_c.shape
    n_ride = 0 if ride is None else len(ride["arrays"])

    def body(qc_ref, kc_ref, vc_ref, doc_ref, lse_ref, delta_ref, *rest):
        ride_in, rest = rest[:n_ride], rest[n_ride:]
        dq_ref, dk_ref, dv_ref = rest[:3]
        ride_out, rest = rest[3:3 + n_ride], rest[3 + n_ride:]
        acc_ref = rest[0]
        kv = pl.program_id(0)
        if n_ride:
            start, finish = ride["halves"](ride_in, ride_out, *rest[1:])
            pl.when((kv == 0) & (pl.program_id(1) == 0))(start)

        @pl.when(pl.program_id(1) == 0)
        def _():
            dk_ref[...] = jnp.zeros_like(dk_ref)
            dv_ref[...] = jnp.zeros_like(dv_ref)

        qc, doc = qc_ref[0, 0], doc_ref[0, 0]
        lsev, delta = lse_ref[0, 0], delta_ref[0, 0]
        acc_ref[...] = jnp.zeros_like(acc_ref)
        nt = (((1,), (1,)), ((), ()))

        def step(j, carry):
            p = jnp.exp(_scores(kc_ref, j, qc) - lsev)
            dp = _scores(vc_ref, j, doc)
            ds = (p * (dp - delta)).astype(BF16)
            acc_ref[...] += jnp.dot(kc_ref[0, j], ds, preferred_element_type=F32)
            dk_ref[0, j] += lax.dot_general(qc, ds, nt, preferred_element_type=F32)
            dv_ref[0, j] += lax.dot_general(doc, p.astype(BF16), nt, preferred_element_type=F32)
            return carry

        lax.fori_loop(0, n_k, step, 0, unroll=2)
        dq_ref[0, 0] = acc_ref[...]

        if n_ride:
            pl.when((kv == ATT_KV - 1) & (pl.program_id(1) == nq - 1))(finish)

    cspec = pl.BlockSpec((1, 1, ATT_DH, R), lambda h, i: (h, i, 0, 0))
    vspec = pl.BlockSpec((1, 1, 1, R), lambda h, i: (h, i, 0, 0))
    kspec = pl.BlockSpec((1, n_k, ATT_DH, tk), lambda h, i: (h, 0, 0, 0))
    k_shape = jax.ShapeDtypeStruct(k_c.shape, F32)
    return pl.pallas_call(
        body, name=name, grid=(ATT_KV, nq),
        in_specs=[cspec, kspec, kspec, cspec, vspec, vspec] + [ANY] * n_ride,
        out_specs=[cspec, kspec, kspec] + [ANY] * n_ride,
        out_shape=[jax.ShapeDtypeStruct((ATT_KV, nq, ATT_DH, R), F32), k_shape, k_shape]
                  + (ride["out_shape"] if n_ride else []),
        scratch_shapes=[pltpu.VMEM((ATT_DH, R), F32)] + (ride["scratch"] if n_ride else []),
        compiler_params=pltpu.CompilerParams(dimension_semantics=("arbitrary", "arbitrary"),
                                             vmem_limit_bytes=VMEM_LIMIT, has_side_effects=bool(n_ride)),
    )(q_c, k_c, v_c, do_c, lse, delta, *(ride["arrays"] if n_ride else []))


def _att_post_fwd(o_c, w, *, name):
    _, nq, _, R = o_c.shape
    tm = R // ATT_G
    T = nq * tm

    def body(oc_ref, w_ref, o_ref, out_ref):
        ov = jnp.concatenate([_tokens_from_cols(oc_ref[kv, 0]) for kv in range(ATT_KV)], axis=1)
        r = lax.rsqrt(jnp.mean(ov * ov, axis=-1, keepdims=True) + EPS)
        o_ref[...] = ov
        out_ref[...] = (ov * r * w_ref[...]).astype(BF16)

    tok = pl.BlockSpec((tm, ATT_QW), lambda i: (i, 0))
    return pl.pallas_call(
        body, name=name, grid=(nq,),
        in_specs=[pl.BlockSpec((ATT_KV, 1, ATT_DH, R), lambda i: (0, i, 0, 0)), pl.BlockSpec((1, ATT_QW), lambda i: (0, 0))],
        out_specs=[tok, tok],
        out_shape=[jax.ShapeDtypeStruct((T, ATT_QW), F32), jax.ShapeDtypeStruct((T, ATT_QW), BF16)],
        compiler_params=_params(("parallel",)),
    )(o_c, w)


def _att_post_bwd(dmix, o, w, *, name):
    T = o.shape[0]
    tm = min(FA_TQ, T)
    R = ATT_G * tm

    def body(dm_ref, o_ref, w_ref, do_ref, delta_ref, dw_ref):
        @pl.when(pl.program_id(0) == 0)
        def _():
            dw_ref[...] = jnp.zeros_like(dw_ref)

        ov = o_ref[...]
        r = lax.rsqrt(jnp.mean(ov * ov, axis=-1, keepdims=True) + EPS)
        xh = ov * r
        dm = dm_ref[...]
        dxh = dm * w_ref[...]
        t = jnp.mean(dxh * xh, axis=-1, keepdims=True)
        do = r * (dxh - xh * t)
        _store_cols(do_ref, do)
        dob = do.astype(BF16).astype(F32)
        for kv in range(ATT_KV):
            delta_ref[kv, 0] = jnp.sum(_cols_from_tokens(dob * ov, kv), axis=0, keepdims=True)
        dw_ref[...] += jnp.sum(dm * xh, axis=0, keepdims=True)

    tok = pl.BlockSpec((tm, ATT_QW), lambda i: (i, 0))
    vec = pl.BlockSpec((1, ATT_QW), lambda i: (0, 0))
    return pl.pallas_call(
        body, name=name, grid=(T // tm,),
        in_specs=[pl.BlockSpec((tm, ATT_QW), lambda i: (i, 1)), tok, vec],
        out_specs=[pl.BlockSpec((ATT_KV, 1, ATT_DH, R), lambda i: (0, i, 0, 0)),
                   pl.BlockSpec((ATT_KV, 1, 1, R), lambda i: (0, i, 0, 0)), vec],
        out_shape=[jax.ShapeDtypeStruct((ATT_KV, T // tm, ATT_DH, R), BF16),
                   jax.ShapeDtypeStruct((ATT_KV, T // tm, 1, R), F32), jax.ShapeDtypeStruct((1, ATT_QW), F32)],
        compiler_params=_params(("arbitrary",)),
    )(dmix, o, w)


def _ffn_up(h2, wg_t, wu_t, *, name, tm=512):
    T = h2.shape[0]
    tn = _pick(D_FF, 1408)
    nt = (((1,), (1,)), ((), ()))

    def body(h_ref, wg_ref, wu_ref, g_ref, u_ref, a_ref):
        hv = h_ref[...]
        g = lax.dot_general(hv, wg_ref[...], nt, preferred_element_type=F32)
        u = lax.dot_general(hv, wu_ref[...], nt, preferred_element_type=F32)
        g_ref[...] = g.astype(BF16)
        u_ref[...] = u.astype(BF16)
        a_ref[...] = (g * _sigmoid(g) * u).astype(BF16)

    wspec = pl.BlockSpec((tn, D_MODEL), lambda i, j: (j, 0))
    ospec = pl.BlockSpec((tm, tn), lambda i, j: (i, j))
    return pl.pallas_call(
        body, name=name, grid=(T // tm, D_FF // tn),
        in_specs=[pl.BlockSpec((tm, D_MODEL), lambda i, j: (i, 0)), wspec, wspec],
        out_specs=[ospec] * 3, out_shape=[jax.ShapeDtypeStruct((T, D_FF), BF16)] * 3,
        compiler_params=_params(("parallel", "arbitrary")),
    )(h2, wg_t, wu_t)


def _ffn_act_bwd(dx2b, w_down, gate, up, *, name, tm=512):
    T = dx2b.shape[0]
    tn = _pick(D_FF, 1408)

    def body(dx_ref, w_ref, g_ref, u_ref, dg_ref, du_ref):
        da = lax.dot_general(dx_ref[...], w_ref[...], (((1,), (1,)), ((), ())), preferred_element_type=F32)
        g = g_ref[...].astype(F32)
        u = u_ref[...].astype(F32)
        sg = _sigmoid(g)
        dg_ref[...] = (da * u * (sg * (1.0 + g * (1.0 - sg)))).astype(BF16)
        du_ref[...] = (da * (g * sg)).astype(BF16)

    ospec = pl.BlockSpec((tm, tn), lambda i, j: (i, j))
    return pl.pallas_call(
        body, name=name, grid=(T // tm, D_FF // tn),
        in_specs=[pl.BlockSpec((tm, D_MODEL), lambda i, j: (i, 0)),
                  pl.BlockSpec((tn, D_MODEL), lambda i, j: (j, 0)), ospec, ospec],
        out_specs=[ospec] * 2, out_shape=[jax.ShapeDtypeStruct((T, D_FF), BF16)] * 2,
        compiler_params=_params(("parallel", "arbitrary")),
    )(dx2b, w_down, gate, up)


def _adam_math(w, g, m, v):
    m = ADAM_B1 * m + (1.0 - ADAM_B1) * g
    v = ADAM_B2 * v + (1.0 - ADAM_B2) * (g * g)
    m_hat = m / (1.0 - ADAM_B1 ** ADAM_STEP)
    v_hat = v / (1.0 - ADAM_B2 ** ADAM_STEP)
    delta = -ADAM_LR * (m_hat / (jnp.sqrt(v_hat) + ADAM_EPS) + ADAM_WD * w)
    return delta, m, v


def _adamw(parts, w, m, v, *, name, tr_cap=256):
    P, R, C = parts.shape
    tr = R
    for t in range(8, min(R, tr_cap) + 1, 8):
        if R % t == 0:
            tr = t

    def body(p_ref, w_ref, m_ref, v_ref, g_ref, d_ref, nm_ref, nv_ref):
        g = p_ref[0].astype(F32)
        for j in range(1, P):
            g = g + p_ref[j].astype(F32)
        d, nm, nv = _adam_math(w_ref[...], g, m_ref[...], v_ref[...])
        g_ref[...] = g
        d_ref[...] = d
        nm_ref[...] = nm
        nv_ref[...] = nv

    blk = pl.BlockSpec((tr, C), lambda i: (i, 0))
    return pl.pallas_call(
        body, name=name, grid=(R // tr,),
        in_specs=[pl.BlockSpec((P, tr, C), lambda i: (0, i, 0)), blk, blk, blk],
        out_specs=[blk] * 4, out_shape=[jax.ShapeDtypeStruct((R, C), F32)] * 4,
        compiler_params=_params(("parallel",)),
    )(parts, w, m, v)


def _gather_halves(ins, outs, send_sems, recv_sems, local_sems):
    n = len(ins)
    x, y, c = lax.axis_index("x"), lax.axis_index("y"), lax.axis_index("c")
    me, sibling = (x, y, c), (x, y, 1 - c)
    chips = [(1 - x, y), (x, 1 - y), (1 - x, 1 - y)]

    def slot(p):
        return 4 * p[0] + 2 * p[1] + p[2]

    def copy(a, k, block, to, src=None):
        dst = outs[a].at[slot(block)]
        return pltpu.make_async_remote_copy(
            src_ref=dst if src is None else src, dst_ref=dst,
            send_sem=send_sems.at[a * 7 + k], recv_sem=recv_sems.at[a * 7 + k],
            device_id=to, device_id_type=MESH)

    mine = [pltpu.make_async_copy(ins[a], outs[a].at[slot(me)], local_sems.at[a]) for a in range(n)]
    first = []
    for a in range(n):
        first.append(copy(a, 0, me, sibling, src=ins[a]))
        first += [copy(a, 1 + j, me, (*chip, c), src=ins[a]) for j, chip in enumerate(chips)]

    def start():
        for cp in mine + first:
            cp.start()

    def finish():
        passed = []
        for j, chip in enumerate(chips):
            for a in range(n):
                copy(a, 1 + j, (*chip, c), me).wait_recv()
                cp = copy(a, 4 + j, (*chip, c), sibling)
                cp.start()
                passed.append(cp)
        for a in range(n):
            copy(a, 0, sibling, me).wait_recv()
            for j, chip in enumerate(chips):
                copy(a, 4 + j, (*chip, 1 - c), me).wait_recv()
        for cp in first + passed:
            cp.wait_send()
        for cp in mine:
            cp.wait()

    return start, finish


def _gather_scratch(n):
    return [pltpu.SemaphoreType.DMA((7 * n,)), pltpu.SemaphoreType.DMA((7 * n,)), pltpu.SemaphoreType.DMA((n,))]


def _gathered_shapes(xs):
    return [jax.ShapeDtypeStruct((N_DEV,) + x.shape, x.dtype) for x in xs]


def _ride_gather(xs):
    xs = list(xs)
    return dict(arrays=xs, out_shape=_gathered_shapes(xs), scratch=_gather_scratch(len(xs)), halves=_gather_halves)


def _ride_chips(gs):
    gs = list(gs)
    n = len(gs)

    def halves(ins, outs, send_sems, recv_sems, local_sems):
        mine, copies = _exchange_copies(ins, outs, send_sems, recv_sems, local_sems, masks=CHIP_MASKS, slot=_chip_slot)

        def start():
            for cp in mine:
                cp.start()
            for send, _ in copies:
                send.start()

        def finish():
            for send, recv in copies:
                recv.wait_recv()
                send.wait_send()
            for cp in mine:
                cp.wait()

        return start, finish

    n_sem = len(CHIP_MASKS) * n
    return dict(arrays=gs, out_shape=[jax.ShapeDtypeStruct(g.shape, g.dtype) for g in gs], halves=halves,
                scratch=[pltpu.SemaphoreType.DMA((n_sem,)), pltpu.SemaphoreType.DMA((n_sem,)),
                         pltpu.SemaphoreType.DMA((n,))])


def _all_gather(xs, *, name):
    n = len(xs)

    def body(*refs):
        start, finish = _gather_halves(refs[:n], refs[n:2 * n], *refs[2 * n:])
        start()
        finish()

    return pl.pallas_call(
        body, name=name,
        in_specs=[ANY] * n, out_specs=[ANY] * n, out_shape=_gathered_shapes(xs), scratch_shapes=_gather_scratch(n),
        compiler_params=pltpu.CompilerParams(has_side_effects=True),
    )(*xs)


ALL_MASKS = [(mx, my, mc) for mx in (0, 1) for my in (0, 1) for mc in (0, 1)][1:]


def _flip(v, bit):
    return 1 - v if bit else v


def _exchange_copies(ins, outs, send_sems, recv_sems, local_sems, *, masks, slot):
    n, n_peers = len(ins), len(masks)
    x, y, c = lax.axis_index("x"), lax.axis_index("y"), lax.axis_index("c")
    my_slot = slot((x, y, c))
    mine = [pltpu.make_async_copy(ins[a].at[my_slot], outs[a].at[my_slot], local_sems.at[a]) for a in range(n)]
    copies = []
    for a in range(n):
        for k, (mx, my, mc) in enumerate(masks):
            peer = (_flip(x, mx), _flip(y, my), _flip(c, mc))
            peer_slot = slot(peer)
            sems = dict(send_sem=send_sems.at[a * n_peers + k], recv_sem=recv_sems.at[a * n_peers + k],
                        device_id=peer, device_id_type=MESH)
            copies.append((
                pltpu.make_async_remote_copy(src_ref=ins[a].at[peer_slot], dst_ref=outs[a].at[my_slot], **sems),
                pltpu.make_async_remote_copy(src_ref=ins[a].at[peer_slot], dst_ref=outs[a].at[peer_slot], **sems)))
    return mine, copies


def _send_to_all(v, *, name):
    def body(v_ref, out_ref, send_sems, recv_sems, local_sem):
        x, y, c = lax.axis_index("x"), lax.axis_index("y"), lax.axis_index("c")
        me = 4 * x + 2 * y + c
        mine = pltpu.make_async_copy(v_ref, out_ref.at[me], local_sem)
        mine.start()
        copies = []
        for k, (mx, my, mc) in enumerate(ALL_MASKS):
            peer = (_flip(x, mx), _flip(y, my), _flip(c, mc))
            peer_id = 4 * peer[0] + 2 * peer[1] + peer[2]
            sems = dict(send_sem=send_sems.at[k], recv_sem=recv_sems.at[k], device_id=peer, device_id_type=MESH)
            copies.append((pltpu.make_async_remote_copy(src_ref=v_ref, dst_ref=out_ref.at[me], **sems),
                           pltpu.make_async_remote_copy(src_ref=v_ref, dst_ref=out_ref.at[peer_id], **sems)))
        for send, _ in copies:
            send.start()
        for send, recv in copies:
            recv.wait_recv()
            send.wait_send()
        mine.wait()

    n_peers = len(ALL_MASKS)
    return pl.pallas_call(
        body, name=name, in_specs=[ANY], out_specs=ANY,
        out_shape=jax.ShapeDtypeStruct((N_DEV,) + v.shape, v.dtype),
        scratch_shapes=[pltpu.SemaphoreType.DMA((n_peers,)), pltpu.SemaphoreType.DMA((n_peers,)),
                        pltpu.SemaphoreType.DMA],
        compiler_params=pltpu.CompilerParams(has_side_effects=True),
    )(v)


SWAP_ROW_CHUNKS = 4


def _core_swap(gs, *, name):
    n = len(gs)

    def body(*refs):
        ins, outs = refs[:n], refs[n:2 * n]
        send_sems, recv_sems = refs[2 * n:]
        x, y, c = lax.axis_index("x"), lax.axis_index("y"), lax.axis_index("c")
        sibling = (x, y, 1 - c)
        for a in range(n):
            Q, _, R, _ = ins[a].shape
            rows = R // SWAP_ROW_CHUNKS
            for q in range(Q):
                for j in range(SWAP_ROW_CHUNKS):
                    pltpu.make_async_remote_copy(
                        src_ref=ins[a].at[q, 1 - c, pl.ds(j * rows, rows)], dst_ref=outs[a].at[q, pl.ds(j * rows, rows)],
                        send_sem=send_sems.at[a], recv_sem=recv_sems.at[a], device_id=sibling, device_id_type=MESH
                    ).start()
        for a in range(n):
            pltpu.make_async_remote_copy(
                src_ref=outs[a], dst_ref=outs[a], send_sem=send_sems.at[a], recv_sem=recv_sems.at[a],
                device_id=sibling, device_id_type=MESH).wait()

    return pl.pallas_call(
        body, name=name,
        in_specs=[ANY] * n, out_specs=[ANY] * n,
        out_shape=[jax.ShapeDtypeStruct(g.shape[:1] + g.shape[2:], g.dtype) for g in gs],
        scratch_shapes=[pltpu.SemaphoreType.DMA((n,)), pltpu.SemaphoreType.DMA((n,))],
        compiler_params=pltpu.CompilerParams(has_side_effects=True),
    )(*gs)


def _pair_sum(g, other, core, *, name, tr_cap=256):
    Q, _, R, C = g.shape
    tr = max(t for t in range(16, min(R, tr_cap) + 1, 16) if R % t == 0)

    def body(core_ref, g_ref, o_ref, out_ref):
        out_ref[0] = (g_ref[0, 0] + o_ref[0]).astype(BF16)

    return pl.pallas_call(
        body, name=name,
        grid_spec=pltpu.PrefetchScalarGridSpec(
            num_scalar_prefetch=1, grid=(Q, R // tr),
            in_specs=[pl.BlockSpec((1, 1, tr, C), lambda q, i, core_ref: (q, core_ref[0], i, 0)),
                      pl.BlockSpec((1, tr, C), lambda q, i, core_ref: (q, i, 0))],
            out_specs=pl.BlockSpec((1, tr, C), lambda q, i, core_ref: (q, i, 0))),
        out_shape=jax.ShapeDtypeStruct((Q, R, C), BF16),
        compiler_params=_params(("parallel", "parallel")),
    )(core, g, other)


def _pack_small(norm1, norm2, final, att, hg, qn, kn, lb=None, loss=None):
    z = lambda n: jnp.zeros((n,), F32)
    rows = [norm1.reshape(-1), norm2.reshape(-1), final.reshape(-1),
            jnp.concatenate([att.reshape(-1), z(512)]),
            jnp.concatenate([hg.reshape(-1), qn.reshape(-1), kn.reshape(-1), z(1024 - 256)]),
            z(1024) if lb is None else lb.reshape(-1),
            z(1024) if loss is None else jnp.concatenate([loss.reshape(-1), z(1023)]), z(1024)]
    return jnp.stack(rows, axis=0)


def _unpack_small(p):
    return (p[0:1, :], p[1:2, :], p[2, :], p[3:4, 0:512], p[4:5, 0:128], p[4:5, 128:192], p[4:5, 192:256])


def _fold_heads(dhg, dqn, dkn, *, name):
    def body(hg_ref, q_ref, k_ref, ohg_ref, oq_ref, ok_ref):
        def fold128(v):
            acc = v[:, 0:LANES]
            for j in range(1, v.shape[1] // LANES):
                acc = acc + v[:, j * LANES:(j + 1) * LANES]
            return acc

        ohg_ref[...] = fold128(hg_ref[...])
        q = fold128(q_ref[...])
        oq_ref[...] = q + pltpu.roll(q, ATT_DH, 1)
        k = k_ref[...]
        ok_ref[...] = k + pltpu.roll(k, ATT_DH, 1)

    return pl.pallas_call(body, name=name, out_shape=[jax.ShapeDtypeStruct((1, LANES), F32)] * 3)(dhg, dqn, dkn)


def _lb_grad(dlb_sum, lb, *, name):
    def body(d_ref, lb_ref, o_ref):
        lbv = lb_ref[...]
        gl = d_ref[...] * lbv * (1.0 - lbv)
        o_ref[0:1, :] = gl[0:1, :]
        o_ref[1:2, :] = -gl[0:1, :]
        o_ref[2:3, :] = gl[1:2, :]
        o_ref[3:4, :] = -gl[1:2, :]

    return pl.pallas_call(body, name=name, out_shape=jax.ShapeDtypeStruct((4, HG_W), F32))(dlb_sum, lb)


def _lower_bounds(lb_logits_full, *, name):
    def body(l_ref, o_ref):
        for d in range(2):
            l0, l1 = l_ref[2 * d:2 * d + 1, :], l_ref[2 * d + 1:2 * d + 2, :]
            mx = jnp.maximum(l0, l1)
            e0, e1 = jnp.exp(l0 - mx), jnp.exp(l1 - mx)
            o_ref[d:d + 1, :] = e0 / (e0 + e1)

    return pl.pallas_call(body, name=name, out_shape=jax.ShapeDtypeStruct((2, HG_W), F32))(
        lb_logits_full.reshape(4, HG_W))


def _local_step(x, target, norm1_w, w_in_t, lb, hg_norm_w, q_norm_w, k_norm_w, att_norm_w, w_out, norm2_w,
                w_g_t, w_u_t, w_down, final_norm_w, reduce_early=None, reduce_late=None, shards=None):
    T = x.shape[0]
    cos, sin = _rope_tables(T)
    qw8 = jnp.tile(q_norm_w, (1, ATT_HEADS))
    kw2 = jnp.tile(k_norm_w, (1, ATT_KV))

    h, r1 = _rms_fwd(x, norm1_w, name="norm1_fwd")
    if shards is None:
        U = _mm_nn([(h, w_in_t)], trans_b=True, name="in_proj")
        o_f, st_f = _gla_fwd(U, lb[0:1], f_block=1, reverse=False, name="gla_fwd_f")
    else:
        U, g_gu = _mm_nn([(h, w_in_t)], trans_b=True, ride=_ride_gather([shards["w_gu_t"]]), name="in_proj")
        o_f, st_f, g_out, g_dn = _gla_fwd(U, lb[0:1], f_block=1, reverse=False,
                                          ride=_ride_gather([shards["w_out"], shards["w_down"]]), name="gla_fwd_f")
        g_gu = g_gu.reshape(2, -1, D_MODEL)
        w_g_t, w_u_t = g_gu[0], g_gu[1]
        w_out, w_down = g_out.reshape(-1, D_MODEL), g_dn.reshape(-1, D_MODEL)
    o_b, st_b = _gla_fwd(U, lb[1:2], f_block=2, reverse=True, name="gla_fwd_b")
    mix_hg = _hg_post_fwd(o_f, o_b, U, hg_norm_w, name="hg_post_fwd")
    q_c, qn_c, kmax2, k_c, v_c = _att_prep_fwd(U, cos, sin, qw8, kw2, name="att_prep_fwd")
    kmax = jnp.sqrt(jnp.max(kmax2.reshape(ATT_KV, ATT_DH), axis=1))
    m_c = qn_c * (kmax * 1.001).reshape(ATT_KV, 1, 1, 1)
    o_c, lse = lax.cond(jnp.max(m_c) <= FA_BOUND_MAX,
                        lambda: _flash_fwd_bounded(q_c, k_c, v_c, m_c, name="flash_fwd_bounded"),
                        lambda: _flash_fwd(q_c, k_c, v_c, name="flash_fwd"))
    o_att, mix_att = _att_post_fwd(o_c, att_norm_w, name="att_post_fwd")
    x1, h2, r2 = _mm_nn([(mix_hg, w_out[:HG_W]), (mix_att, w_out[HG_W:])], residual=x, tail=_tail_rms_fwd(norm2_w),
                        name="out_proj")
    gate, up, act = _ffn_up(h2, w_g_t, w_u_t, name="ffn_up")
    loss, dx2, dx2b, d_final = _mm_nn([(act, w_down)], residual=x1,
                                      tail=_tail_loss(target, final_norm_w.reshape(1, D_MODEL)), name="ffn_down")

    d_gate, d_up = _ffn_act_bwd(dx2b, w_down, gate, up, name="ffn_act_bwd")
    dw_down = _mm_tn(act, dx2b, tma_cap=1408, name="dw_down")
    dx1, dx1b, d_norm2 = _mm_nn([(d_gate, w_g_t), (d_up, w_u_t)], tm=256,
                                tail=_tail_rms_bwd(x1, r2, norm2_w, dx2, emit_bf16=True), name="ffn_up_bwd")
    dw_g = _mm_tn(d_gate, h2, tma_cap=1408, name="dw_gate")
    dw_u = _mm_tn(d_up, h2, tma_cap=1408, name="dw_up")
    dmix = _mm_nn([(dx1b, w_out)], trans_b=True, name="out_proj_bwd")
    dw_out = jnp.concatenate([_mm_tn(mix_hg, dx1b, name="dw_out_hg"), _mm_tn(mix_att, dx1b, name="dw_out_att")], axis=0)
    do_c, delta, d_att = _att_post_bwd(dmix, o_att, att_norm_w, name="att_post_bwd")
    ride = None if reduce_early is None else _ride_chips(reduce_early(dw_out, dw_g, dw_u, dw_down))
    dq_c, dk_c, dv_c, *rode = _flash_bwd(q_c, k_c, v_c, do_c, lse, delta, ride=ride, name="flash_bwd")
    dU_att, d_qn, d_kn = _att_prep_bwd(U, dq_c, dk_c, dv_c, cos, sin, qw8, kw2, name="att_prep_bwd")
    do_hg, du_g, d_hg = _hg_post_bwd(dmix, o_f, o_b, U, hg_norm_w, name="hg_post_bwd")
    dq_f, dz_f, dv_f, dlb_f = _gla_bwd(U, lb[0:1], do_hg, st_f, f_block=1, reverse=False, name="gla_bwd_f")
    dU_hg, dlb_b = _gla_bwd(U, lb[1:2], do_hg, st_b, f_block=2, reverse=True, prev=(dq_f, dz_f, dv_f, du_g),
                            name="gla_bwd_b")
    w_hg = 5 * HG_W
    dw_in = jnp.concatenate([_mm_tn(dU_hg, h, tma_cap=1280, name="dw_in_hg"), _mm_tn(dU_att, h, name="dw_in_att")],
                            axis=0)
    late = None if reduce_late is None else _ride_chips(reduce_late(dw_in))
    grad_x, d_norm1, *rode_late = _mm_nn([(dU_hg, w_in_t[:w_hg]), (dU_att, w_in_t[w_hg:])], ride=late,
                                         tail=_tail_rms_bwd(x, r1, norm1_w, dx1, emit_bf16=False), name="in_proj_bwd")
    d_hg, d_qn, d_kn = _fold_heads(d_hg, d_qn, d_kn, name="fold_heads")

    big = dict(w_in=dw_in, w_out=dw_out, w_g=dw_g, w_u=dw_u, w_down=dw_down)
    small = dict(norm1=d_norm1, norm2=d_norm2, final=d_final, att=d_att, hg=d_hg,
                 qn=d_qn[:, :ATT_DH], kn=d_kn[:, :ATT_DH], lb=jnp.concatenate([dlb_f, dlb_b], axis=0))
    return loss, grad_x, big, small, rode + rode_late


def kernel(x, norm1_w, w_in, lb_logits, hg_norm_w, q_norm_w, k_norm_w, att_norm_w, w_out, norm2_w, w_gate_up, w_down, final_norm_w, loss_target, m_norm1_w, m_w_in, m_lb_logits, m_hg_norm_w, m_q_norm_w, m_k_norm_w, m_att_norm_w, m_w_out, m_norm2_w, m_w_gate_up, m_w_down, m_final_norm_w, v_norm1_w, v_w_in, v_lb_logits, v_hg_norm_w, v_q_norm_w, v_k_norm_w, v_att_norm_w, v_w_out, v_norm2_w, v_w_gate_up, v_w_down, v_final_norm_w):
    T = x.shape[1]
    me = 4 * lax.axis_index("x") + 2 * lax.axis_index("y") + lax.axis_index("c")
    c_in, r_out, c_gu, r_dn = w_in.shape[2], w_out.shape[1], w_gate_up.shape[2], w_down.shape[1]
    lb_cols = lb_logits.shape[2]

    g_in, g_lb = _all_gather([w_in[0].T.astype(BF16), lb_logits.reshape(4, lb_cols)], name="gather_weights")
    w_in_t = g_in.reshape(N_DEV * c_in, D_MODEL)
    shards = dict(w_gu_t=w_gate_up[0].T.astype(BF16), w_out=w_out[0].astype(BF16), w_down=w_down[0].astype(BF16))
    lb_logits_f = g_lb.transpose(1, 0, 2).reshape(2, 2, N_DEV * lb_cols)
    lb = _lower_bounds(lb_logits_f, name="lower_bounds")

    chips = N_DEV // 2
    core = lax.axis_index("c").astype(jnp.int32).reshape(1)
    by_owner = lambda g, r: g.reshape(chips, 2, r, D_MODEL)

    def chip_sums(mine, names, call):
        theirs = _core_swap(mine, name=call)
        return [_pair_sum(g, o, core, name="pair_sum_" + nm) for g, o, nm in zip(mine, theirs, names)]

    def reduce_early(dw_out, dw_g_t, dw_u_t, dw_down):
        return chip_sums([by_owner(dw_out, r_out), by_owner(jnp.concatenate([dw_g_t, dw_u_t], axis=0), c_gu),
                          by_owner(dw_down, r_dn)], ("w_out", "w_gu", "w_down"), "exchange_cores_early")

    def reduce_late(dw_in_t):
        return chip_sums([by_owner(dw_in_t, c_in)], ("w_in",), "exchange_cores")

    loss, grad_x, big, small, (p_out, p_gu, p_dn, p_in) = _local_step(
        x[0], loss_target[0], norm1_w, w_in_t, lb, hg_norm_w, q_norm_w, k_norm_w, att_norm_w, None, norm2_w,
        None, None, None, final_norm_w, reduce_early=reduce_early, reduce_late=reduce_late, shards=shards)
    p_gu, p_in = p_gu.transpose(0, 2, 1), p_in.transpose(0, 2, 1)

    packed = _pack_small(small["norm1"], small["norm2"], small["final"], small["att"], small["hg"],
                         small["qn"], small["kn"], small["lb"], loss)
    all_small = _send_to_all(packed, name="exchange_small")

    g_w_in, d_w_in, nm_w_in, nv_w_in = _adamw(p_in, w_in[0], m_w_in[0], v_w_in[0], name="adamw_w_in")
    g_w_out, d_w_out, nm_w_out, nv_w_out = _adamw(p_out, w_out[0], m_w_out[0], v_w_out[0], name="adamw_w_out")
    g_w_gu, d_w_gu, nm_w_gu, nv_w_gu = _adamw(p_gu, w_gate_up[0], m_w_gate_up[0], v_w_gate_up[0], name="adamw_w_gu")
    g_w_dn, d_w_dn, nm_w_dn, nv_w_dn = _adamw(p_dn, w_down[0], m_w_down[0], v_w_down[0], name="adamw_w_down")

    pk = lambda vecs: _pack_small(*vecs)
    w_pk = pk([norm1_w, norm2_w, final_norm_w, att_norm_w, hg_norm_w, q_norm_w, k_norm_w])
    m_pk = pk([m_norm1_w, m_norm2_w, m_final_norm_w, m_att_norm_w, m_hg_norm_w, m_q_norm_w, m_k_norm_w])
    v_pk = pk([v_norm1_w, v_norm2_w, v_final_norm_w, v_att_norm_w, v_hg_norm_w, v_q_norm_w, v_k_norm_w])
    g_pk, d_pk, nm_pk, nv_pk = _adamw(all_small, w_pk, m_pk, v_pk, name="adamw_small")

    dlb_sum = g_pk[5:6, :].reshape(2, HG_W)
    g_lb_full = _lb_grad(dlb_sum, lb, name="lb_grad")
    g_lb_mine = lax.dynamic_slice_in_dim(g_lb_full, me * lb_cols, lb_cols, axis=1)
    g_lb_s, d_lb, nm_lb, nv_lb = _adamw(g_lb_mine[None], lb_logits.reshape(4, lb_cols),
                                        m_lb_logits.reshape(4, lb_cols), v_lb_logits.reshape(4, lb_cols),
                                        name="adamw_lb")

    loss_total = g_pk[6, 0]

    def outs(big4, lb_arr, pk_arr):
        n1, n2, fin, att, hg, qn, kn = _unpack_small(pk_arr)
        b_in, b_out, b_gu, b_dn = big4
        return [n1, b_in[None], lb_arr.reshape(2, 2, lb_cols), hg, qn, kn, att, b_out[None], n2, b_gu[None],
                b_dn[None], fin]

    return (loss_total, grad_x[None],
            *outs((g_w_in, g_w_out, g_w_gu, g_w_dn), g_lb_s, g_pk),
            *outs((d_w_in, d_w_out, d_w_gu, d_w_dn), d_lb, d_pk),
            *outs((nm_w_in, nm_w_out, nm_w_gu, nm_w_dn), nm_lb, nm_pk),
            *outs((nv_w_in, nv_w_out, nv_w_gu, nv_w_dn), nv_lb, nv_pk))
```

```python
import math

import jax
import jax.numpy as jnp
import numpy as np
from jax import lax
from jax.experimental import pallas as pl
from jax.experimental.pallas import tpu as pltpu

F32 = jnp.float32
BF16 = jnp.bfloat16

N_DEV = 8
D_MODEL = 1024
EPS = 1e-6
HG_HEADS = 4
HG_D = 128
HG_W = HG_HEADS * HG_D
CHUNK = 64
ATT_HEADS = 8
ATT_KV = 2
ATT_G = ATT_HEADS // ATT_KV
ATT_DH = 64
ATT_QW = ATT_HEADS * ATT_DH
ATT_KW = ATT_KV * ATT_DH
GRID_W = 64
ROPE_THETA = 10000.0
D_FF = 2816
ADAM_LR, ADAM_B1, ADAM_B2, ADAM_EPS, ADAM_WD, ADAM_STEP = 0.001, 0.9, 0.999, 1e-08, 0.01, 10

LANES = 128
VMEM_LIMIT = 48 * 1024 * 1024
MESH = pl.DeviceIdType.MESH
ANY = pl.BlockSpec(memory_space=pl.ANY)


def _params(sem=None):
    return pltpu.CompilerParams(dimension_semantics=sem, vmem_limit_bytes=VMEM_LIMIT)


def _pick(n, cap):
    best = None
    for t in range(LANES, cap + 1, LANES):
        if n % t == 0:
            best = t
    assert best is not None, (n, cap)
    return best


def _sigmoid(x):
    return 1.0 / (1.0 + jnp.exp(-x))


def _dot(a, b):
    return jnp.dot(a.astype(BF16), b.astype(BF16), preferred_element_type=F32)


def _dot_nt(a, b):
    return lax.dot_general(a.astype(BF16), b.astype(BF16), (((1,), (1,)), ((), ())),
                           preferred_element_type=F32)


def _dot_tn(a, b):
    return lax.dot_general(a.astype(BF16), b.astype(BF16), (((0,), (0,)), ((), ())),
                           preferred_element_type=F32)


def _mm_nn(pairs, *, name, out_dtype=F32, residual=None, tm=512, tn_cap=None, trans_b=False, tail=None, ride=None):
    M = pairs[0][0].shape[0]
    N = pairs[0][1].shape[0 if trans_b else 1]
    tn = N if tn_cap is None else _pick(N, tn_cap)
    n_pairs = len(pairs)
    has_res = residual is not None
    dims = (((1,), (1,)), ((), ())) if trans_b else (((1,), (0,)), ((), ()))
    assert (tail is None and ride is None) or tn == N
    n_main = 2 * n_pairs + has_res
    n_ti = 0 if tail is None else len(tail["ins"])
    n_out = 1 if tail is None else len(tail["outs"])
    n_r = 0 if ride is None else len(ride["arrays"])
    n_in = n_main + n_ti + n_r

    def body(*refs):
        outs = refs[n_in:n_in + n_out]
        if n_r:
            start, finish = ride["halves"](refs[n_main + n_ti:n_in], refs[n_in + n_out:n_in + n_out + n_r],
                                           *refs[n_in + n_out + n_r:])
            pl.when(pl.program_id(0) == 0)(start)
        acc = None
        for i in range(n_pairs):
            d = lax.dot_general(refs[2 * i][...], refs[2 * i + 1][...], dims, preferred_element_type=F32)
            acc = d if acc is None else acc + d
        if has_res:
            acc = acc + refs[2 * n_pairs][...]
        if tail is None:
            outs[0][...] = acc.astype(out_dtype)
        else:
            tail["fn"](acc, pl.program_id(0) == 0, *refs[n_main:n_main + n_ti], *outs)
        if n_r:
            pl.when(pl.program_id(0) == M // tm - 1)(finish)

    kinds = {"row": ((tm, N), (M, N), lambda i, j: (i, 0)), "col": ((tm, 1), (M, 1), lambda i, j: (i, 0)),
             "vec": ((1, N), (1, N), lambda i, j: (0, 0)), "one": ((1, 1), (1, 1), lambda i, j: (0, 0))}
    in_specs, args = [], []
    for a, b in pairs:
        k = a.shape[1]
        b_spec = pl.BlockSpec((tn, k), lambda i, j: (j, 0)) if trans_b else pl.BlockSpec((k, tn), lambda i, j: (0, j))
        in_specs += [pl.BlockSpec((tm, k), lambda i, j: (i, 0)), b_spec]
        args += [a, b]
    if has_res:
        in_specs.append(pl.BlockSpec((tm, tn), lambda i, j: (i, j)))
        args.append(residual)
    if tail is None:
        out_specs = [pl.BlockSpec((tm, tn), lambda i, j: (i, j))]
        out_shape = [jax.ShapeDtypeStruct((M, N), out_dtype)]
    else:
        for arr, kind in tail["ins"]:
            in_specs.append(pl.BlockSpec(kinds[kind][0], kinds[kind][2]))
            args.append(arr)
        out_specs = [pl.BlockSpec(kinds[kind][0], kinds[kind][2]) for _, kind in tail["outs"]]
        out_shape = [jax.ShapeDtypeStruct(kinds[kind][1], dt) for dt, kind in tail["outs"]]
    scratch = []
    if n_r:
        in_specs += [ANY] * n_r
        args += ride["arrays"]
        out_specs += [ANY] * n_r
        out_shape += ride["out_shape"]
        scratch = ride["scratch"]
    sequential = tail is not None or n_r > 0
    res = pl.pallas_call(
        body, name=name, grid=(M // tm, N // tn), in_specs=in_specs, out_specs=out_specs, out_shape=out_shape,
        scratch_shapes=scratch,
        compiler_params=pltpu.CompilerParams(dimension_semantics=("arbitrary" if sequential else "parallel", "arbitrary"),
                                             vmem_limit_bytes=VMEM_LIMIT, has_side_effects=n_r > 0),
    )(*args)
    return res[0] if len(res) == 1 else res


def _mm_tn(a, b, *, name, tma_cap=1024, tnb_cap=1024, tk=1024):
    T, Ma = a.shape
    Nb = b.shape[1]
    tma, tnb = _pick(Ma, tma_cap), _pick(Nb, tnb_cap)
    tk = min(tk, T)
    n_k = T // tk

    def body(a_ref, b_ref, o_ref, acc_ref):
        k = pl.program_id(2)

        @pl.when(k == 0)
        def _():
            acc_ref[...] = jnp.zeros_like(acc_ref)

        acc_ref[...] += lax.dot_general(a_ref[...], b_ref[...], (((0,), (0,)), ((), ())),
                                        preferred_element_type=F32)

        @pl.when(k == n_k - 1)
        def _():
            o_ref[...] = acc_ref[...]

    return pl.pallas_call(
        body, name=name, grid=(Ma // tma, Nb // tnb, n_k),
        in_specs=[pl.BlockSpec((tk, tma), lambda i, j, k: (k, i)), pl.BlockSpec((tk, tnb), lambda i, j, k: (k, j))],
        out_specs=pl.BlockSpec((tma, tnb), lambda i, j, k: (i, j)),
        out_shape=jax.ShapeDtypeStruct((Ma, Nb), F32),
        scratch_shapes=[pltpu.VMEM((tma, tnb), F32)],
        compiler_params=_params(("parallel", "parallel", "arbitrary")),
    )(a, b)


def _rms_fwd(x, w, *, name, tm=512, ride=None):
    T, Dm = x.shape
    n_r = 0 if ride is None else len(ride["arrays"])

    def body(x_ref, w_ref, *rest):
        h_ref, r_ref = rest[n_r:n_r + 2]
        if n_r:
            start, finish = ride["halves"](rest[:n_r], rest[n_r + 2:2 * n_r + 2], *rest[2 * n_r + 2:])
            pl.when(pl.program_id(0) == 0)(start)
        xv = x_ref[...]
        r = lax.rsqrt(jnp.mean(xv * xv, axis=-1, keepdims=True) + EPS)
        h_ref[...] = (xv * r * w_ref[...]).astype(BF16)
        r_ref[...] = r
        if n_r:
            pl.when(pl.program_id(0) == T // tm - 1)(finish)

    return pl.pallas_call(
        body, name=name, grid=(T // tm,),
        in_specs=[pl.BlockSpec((tm, Dm), lambda i: (i, 0)), pl.BlockSpec((1, Dm), lambda i: (0, 0))] + [ANY] * n_r,
        out_specs=[pl.BlockSpec((tm, Dm), lambda i: (i, 0)), pl.BlockSpec((tm, 1), lambda i: (i, 0))] + [ANY] * n_r,
        out_shape=[jax.ShapeDtypeStruct((T, Dm), BF16), jax.ShapeDtypeStruct((T, 1), F32)]
                  + (ride["out_shape"] if n_r else []),
        scratch_shapes=ride["scratch"] if n_r else [],
        compiler_params=pltpu.CompilerParams(dimension_semantics=("arbitrary" if n_r else "parallel",),
                                             vmem_limit_bytes=VMEM_LIMIT, has_side_effects=n_r > 0),
    )(x, w, *(ride["arrays"] if n_r else []))


def _tail_rms_fwd(w):
    def fn(xv, first, w_ref, x_ref, h_ref, r_ref):
        r = lax.rsqrt(jnp.mean(xv * xv, axis=-1, keepdims=True) + EPS)
        x_ref[...] = xv
        h_ref[...] = (xv * r * w_ref[...]).astype(BF16)
        r_ref[...] = r

    return dict(fn=fn, ins=[(w, "vec")], outs=[(F32, "row"), (BF16, "row"), (F32, "col")])


def _tail_rms_bwd(x, r, w, dres, *, emit_bf16):
    def fn(dhv, first, x_ref, r_ref, w_ref, dres_ref, *outs):
        dx_ref, dw_ref = outs[0], outs[-1]

        @pl.when(first)
        def _():
            dw_ref[...] = jnp.zeros_like(dw_ref)

        rv = r_ref[...]
        xh = x_ref[...] * rv
        dxh = dhv * w_ref[...]
        t = jnp.mean(dxh * xh, axis=-1, keepdims=True)
        dx = dres_ref[...] + rv * (dxh - xh * t)
        dx_ref[...] = dx
        if emit_bf16:
            outs[1][...] = dx.astype(BF16)
        dw_ref[...] += jnp.sum(dhv * xh, axis=0, keepdims=True)

    outs = [(F32, "row")] + ([(BF16, "row")] if emit_bf16 else []) + [(F32, "vec")]
    return dict(fn=fn, ins=[(x, "row"), (r, "col"), (w, "vec"), (dres, "row")], outs=outs)


def _tail_loss(target, w):
    def fn(xv, first, t_ref, w_ref, loss_ref, dx_ref, dxb_ref, dw_ref):
        @pl.when(first)
        def _():
            loss_ref[...] = jnp.zeros_like(loss_ref)
            dw_ref[...] = jnp.zeros_like(dw_ref)

        r = lax.rsqrt(jnp.mean(xv * xv, axis=-1, keepdims=True) + EPS)
        xh = xv * r
        wv = w_ref[...]
        err = xh * wv - t_ref[...]
        row_loss = jnp.mean(err * err, axis=-1, keepdims=True)
        loss_ref[...] += 0.5 * jnp.sum(row_loss, axis=0, keepdims=True)
        dy = err * (1.0 / xv.shape[-1])
        dxh = dy * wv
        t = jnp.mean(dxh * xh, axis=-1, keepdims=True)
        dx = r * (dxh - xh * t)
        dx_ref[...] = dx
        dxb_ref[...] = dx.astype(BF16)
        dw_ref[...] += jnp.sum(dy * xh, axis=0, keepdims=True)

    return dict(fn=fn, ins=[(target, "row"), (w, "vec")],
                outs=[(F32, "one"), (F32, "row"), (BF16, "row"), (F32, "vec")])


GLA_TB = 512
GLA_NC = GLA_TB // CHUNK
GLA_UNROLL = 4


def _cumsum_rows(x, row, reverse):
    n = x.shape[0]
    s = 1
    while s < n:
        if not reverse:
            x = x + jnp.where(row >= s, pltpu.roll(x, s, 0), 0.0)
        else:
            x = x + jnp.where(row < n - s, pltpu.roll(x, n - s, 0), 0.0)
        s *= 2
    return x


def _gla_gates(uq, z, lbv):
    q = uq * _sigmoid(uq)
    sg = _sigmoid(z)
    sgn = _sigmoid(-z)
    f = lbv + (1.0 - lbv) * sg
    k = (1.0 - lbv) * sgn
    return q, sg, sgn, f, k


def _gla_decays(f, row, reverse):
    b = _cumsum_rows(jnp.log(f), row, reverse)
    if not reverse:
        bref, blast = b[CHUNK // 2 - 1:CHUNK // 2, :], b[CHUNK - 1:CHUNK, :]
    else:
        bref, blast = b[CHUNK // 2:CHUNK // 2 + 1, :], b[0:1, :]
    return b, bref, blast


def _gla_fwd(U, lb, *, f_block, reverse, name, ride=None):
    T = U.shape[0]
    nb = T // GLA_TB
    n_g = 0 if ride is None else len(ride["arrays"])

    def body(uq_ref, uf_ref, ui_ref, lb_ref, *rest):
        g_in, rest = rest[:n_g], rest[n_g:]
        o_ref, st_ref = rest[:2]
        g_out, rest = rest[2:2 + n_g], rest[2 + n_g:]
        s_ref = rest[0]
        if n_g:
            start, finish = ride["halves"](g_in, g_out, *rest[1:])
            pl.when(pl.program_id(0) == 0)(start)

        @pl.when(pl.program_id(0) == 0)
        def _():
            s_ref[...] = jnp.zeros_like(s_ref)

        row = lax.broadcasted_iota(jnp.int32, (CHUNK, HG_D), 0)
        ri = lax.broadcasted_iota(jnp.int32, (CHUNK, CHUNK), 0)
        ci = lax.broadcasted_iota(jnp.int32, (CHUNK, CHUNK), 1)
        mask = (ri <= ci) if reverse else (ri >= ci)

        def chunk(j, carry):
            c = (GLA_NC - 1 - j) if reverse else j
            rows = pl.ds(pl.multiple_of(c * CHUNK, CHUNK), CHUNK)
            for h in range(HG_HEADS):
                cols = pl.ds(h * HG_D, HG_D)
                v = ui_ref[rows, cols]
                q, _, _, f, k = _gla_gates(uq_ref[rows, cols], uf_ref[rows, cols], lb_ref[:, cols])
                b, bref, blast = _gla_decays(f, row, reverse)
                s = jnp.where(mask, _dot_nt(q * jnp.exp(b - bref), k * jnp.exp(bref - b)), 0.0)
                st = s_ref[h]
                st_ref[c, h] = st
                o_ref[rows, cols] = _dot(s, v) + _dot_nt(q * jnp.exp(b), st)
                s_ref[h] = st * jnp.exp(blast) + _dot_tn(v, k * jnp.exp(blast - b))
            return carry

        lax.fori_loop(0, GLA_NC, chunk, 0, unroll=GLA_NC)
        if n_g:
            pl.when(pl.program_id(0) == nb - 1)(finish)

    blk = (lambda i: nb - 1 - i) if reverse else (lambda i: i)
    ucol = lambda cb: pl.BlockSpec((GLA_TB, HG_W), lambda i: (blk(i), cb))
    return pl.pallas_call(
        body, name=name, grid=(nb,),
        in_specs=[ucol(0), ucol(f_block), ucol(3), pl.BlockSpec((1, HG_W), lambda i: (0, 0))] + [ANY] * n_g,
        out_specs=[pl.BlockSpec((GLA_TB, HG_W), lambda i: (blk(i), 0)),
                   pl.BlockSpec((GLA_NC, HG_HEADS, HG_D, HG_D), lambda i: (blk(i), 0, 0, 0))] + [ANY] * n_g,
        out_shape=[jax.ShapeDtypeStruct((T, HG_W), F32),
                   jax.ShapeDtypeStruct((T // CHUNK, HG_HEADS, HG_D, HG_D), F32)] + (ride["out_shape"] if n_g else []),
        scratch_shapes=[pltpu.VMEM((HG_HEADS, HG_D, HG_D), F32)] + (ride["scratch"] if n_g else []),
        compiler_params=pltpu.CompilerParams(dimension_semantics=("arbitrary",), vmem_limit_bytes=VMEM_LIMIT,
                                             has_side_effects=bool(n_g)),
    )(U, U, U, lb, *(ride["arrays"] if n_g else []))


def _gla_bwd(U, lb, do, states, *, f_block, reverse, name, prev=None):
    T = U.shape[0]
    nb = T // GLA_TB
    final = prev is not None

    def body(uq_ref, uf_ref, ui_ref, lb_ref, do_ref, st_ref, *rest):
        if final:
            dqp_ref, dzp_ref, dvp_ref, dug_ref, out_ref, dlb_ref, ds_ref = rest
        else:
            dq_ref, dz_ref, dv_ref, dlb_ref, ds_ref = rest

        @pl.when(pl.program_id(0) == 0)
        def _():
            ds_ref[...] = jnp.zeros_like(ds_ref)
            dlb_ref[...] = jnp.zeros_like(dlb_ref)

        row = lax.broadcasted_iota(jnp.int32, (CHUNK, HG_D), 0)
        ri = lax.broadcasted_iota(jnp.int32, (CHUNK, CHUNK), 0)
        ci = lax.broadcasted_iota(jnp.int32, (CHUNK, CHUNK), 1)
        mask = (ri <= ci) if reverse else (ri >= ci)

        def chunk(j, carry):
            c = j if reverse else (GLA_NC - 1 - j)
            rows = pl.ds(pl.multiple_of(c * CHUNK, CHUNK), CHUNK)
            for h in range(HG_HEADS):
                cols = pl.ds(h * HG_D, HG_D)
                v = ui_ref[rows, cols]
                lbv = lb_ref[:, cols]
                uq = uq_ref[rows, cols]
                q, sg, sgn, f, k = _gla_gates(uq, uf_ref[rows, cols], lbv)
                b, bref, blast = _gla_decays(f, row, reverse)
                eq, ek, eb, el, dec = (jnp.exp(b - bref), jnp.exp(bref - b), jnp.exp(b), jnp.exp(blast - b),
                                       jnp.exp(blast))
                qin, kin, qb, klast = q * eq, k * ek, q * eb, k * el
                dov = do_ref[rows, cols]
                st = st_ref[c, h]
                dst = ds_ref[h]
                p = jnp.where(mask, _dot_nt(qin, kin), 0.0)
                dp = jnp.where(mask, _dot_nt(dov, v), 0.0)
                dqin = _dot(dp, kin)
                dkin = _dot_tn(dp, qin)
                dv = _dot_tn(p, dov) + _dot_nt(klast, dst)
                dqb = _dot(dov, st)
                dklast = _dot(v, dst)
                ds_ref[h] = _dot_tn(dov, qb) + dst * dec
                db = dqin * qin - dkin * kin + dqb * qb - dklast * klast
                extra = (jnp.sum(dklast * klast, axis=0, keepdims=True)
                         + dec * jnp.sum(st * dst, axis=0, keepdims=True))
                dg = _cumsum_rows(db, row, not reverse) + extra
                dq = dqin * eq + dqb * eb
                dk = dkin * ek + dklast * el
                dfk = dg / f - dk
                dz = (dfk * (1.0 - lbv) * sg * sgn).astype(BF16)
                dlb_ref[:, cols] += jnp.sum(dfk * sgn, axis=0, keepdims=True)
                if final:
                    sq = _sigmoid(uq)
                    col = lambda blk: pl.ds(blk * HG_W + h * HG_D, HG_D)
                    out_ref[rows, col(0)] = ((dq + dqp_ref[rows, cols]) * (sq * (1.0 + uq * (1.0 - sq)))).astype(BF16)
                    out_ref[rows, col(1)] = dzp_ref[rows, cols]
                    out_ref[rows, col(2)] = dz
                    out_ref[rows, col(3)] = (dv + dvp_ref[rows, cols]).astype(BF16)
                    out_ref[rows, col(4)] = dug_ref[rows, cols]
                else:
                    dq_ref[rows, cols] = dq
                    dz_ref[rows, cols] = dz
                    dv_ref[rows, cols] = dv
            return carry

        lax.fori_loop(0, GLA_NC, chunk, 0, unroll=GLA_UNROLL)

    blk = (lambda i: i) if reverse else (lambda i: nb - 1 - i)
    ucol = lambda cb: pl.BlockSpec((GLA_TB, HG_W), lambda i: (blk(i), cb))
    tok = pl.BlockSpec((GLA_TB, HG_W), lambda i: (blk(i), 0))
    vec = pl.BlockSpec((1, HG_W), lambda i: (0, 0))
    in_specs = [ucol(0), ucol(f_block), ucol(3), vec, tok,
                pl.BlockSpec((GLA_NC, HG_HEADS, HG_D, HG_D), lambda i: (blk(i), 0, 0, 0))]
    vec_shape = jax.ShapeDtypeStruct((1, HG_W), F32)
    if final:
        in_specs += [tok] * 4
        out_specs = [pl.BlockSpec((GLA_TB, 5 * HG_W), lambda i: (blk(i), 0)), vec]
        out_shape = [jax.ShapeDtypeStruct((T, 5 * HG_W), BF16), vec_shape]
    else:
        out_specs = [tok, tok, tok, vec]
        out_shape = [jax.ShapeDtypeStruct((T, HG_W), F32), jax.ShapeDtypeStruct((T, HG_W), BF16),
                     jax.ShapeDtypeStruct((T, HG_W), F32), vec_shape]
    return pl.pallas_call(
        body, name=name, grid=(nb,), in_specs=in_specs, out_specs=out_specs, out_shape=out_shape,
        scratch_shapes=[pltpu.VMEM((HG_HEADS, HG_D, HG_D), F32)],
        compiler_params=_params(("arbitrary",)),
    )(U, U, U, lb, do, states, *(prev if final else ()))


def _hg_post_fwd(o_f, o_b, U, w, *, name, tm=512):
    T = o_f.shape[0]

    def body(of_ref, ob_ref, ug_ref, w_ref, out_ref):
        wv = w_ref[...]
        for h in range(HG_HEADS):
            cols = pl.ds(h * HG_D, HG_D)
            o = of_ref[:, cols] + ob_ref[:, cols]
            r = lax.rsqrt(jnp.mean(o * o, axis=-1, keepdims=True) + EPS)
            ug = ug_ref[:, cols]
            out_ref[:, cols] = (o * r * wv * (ug * _sigmoid(ug))).astype(BF16)

    tok = pl.BlockSpec((tm, HG_W), lambda i: (i, 0))
    return pl.pallas_call(
        body, name=name, grid=(T // tm,),
        in_specs=[tok, tok, pl.BlockSpec((tm, HG_W), lambda i: (i, 4)), pl.BlockSpec((1, HG_D), lambda i: (0, 0))],
        out_specs=tok, out_shape=jax.ShapeDtypeStruct((T, HG_W), BF16),
        compiler_params=_params(("parallel",)),
    )(o_f, o_b, U, w)


def _hg_post_bwd(dmix, o_f, o_b, U, w, *, name, tm=512):
    T = o_f.shape[0]

    def body(dm_ref, of_ref, ob_ref, ug_ref, w_ref, do_ref, dug_ref, dw_ref):
        @pl.when(pl.program_id(0) == 0)
        def _():
            dw_ref[...] = jnp.zeros_like(dw_ref)

        wv = w_ref[...]
        for h in range(HG_HEADS):
            cols = pl.ds(h * HG_D, HG_D)
            o = of_ref[:, cols] + ob_ref[:, cols]
            r = lax.rsqrt(jnp.mean(o * o, axis=-1, keepdims=True) + EPS)
            xh = o * r
            ug = ug_ref[:, cols]
            sg = _sigmoid(ug)
            dm = dm_ref[:, cols]
            dn = dm * (ug * sg)
            dug_ref[:, cols] = (dm * (xh * wv) * (sg * (1.0 + ug * (1.0 - sg)))).astype(BF16)
            dxh = dn * wv
            t = jnp.mean(dxh * xh, axis=-1, keepdims=True)
            do_ref[:, cols] = r * (dxh - xh * t)
            dw_ref[:, cols] += jnp.sum(dn * xh, axis=0, keepdims=True)

    tok = pl.BlockSpec((tm, HG_W), lambda i: (i, 0))
    vec = pl.BlockSpec((1, HG_W), lambda i: (0, 0))
    return pl.pallas_call(
        body, name=name, grid=(T // tm,),
        in_specs=[tok, tok, tok, pl.BlockSpec((tm, HG_W), lambda i: (i, 4)), pl.BlockSpec((1, HG_D), lambda i: (0, 0))],
        out_specs=[tok, tok, vec],
        out_shape=[jax.ShapeDtypeStruct((T, HG_W), F32), jax.ShapeDtypeStruct((T, HG_W), BF16),
                   jax.ShapeDtypeStruct((1, HG_W), F32)],
        compiler_params=_params(("arbitrary",)),
    )(dmix, o_f, o_b, U, w)


def _rope_tables(T):
    rows = T // GRID_W
    row = np.repeat(np.arange(rows), GRID_W).astype(np.float32)
    col = np.tile(np.arange(GRID_W), rows).astype(np.float32)
    axis_dim = ATT_DH // 2
    freqs = (np.float32(ROPE_THETA) ** (-np.arange(0, axis_dim, 2, dtype=np.float32) / np.float32(axis_dim))
             ).astype(np.float32)
    ang = np.concatenate([row[:, None] * freqs, col[:, None] * freqs], axis=-1).astype(np.float32)
    cos, sin = np.cos(ang), np.sin(ang)
    c = np.repeat(cos, 2, axis=-1)
    s = np.stack([-sin, sin], axis=-1).reshape(T, ATT_DH)
    return jnp.asarray(np.tile(c, (1, 2)), F32), jnp.asarray(np.tile(s, (1, 2)), F32)


def _head_blockdiag(width):
    shift = ATT_DH.bit_length() - 1
    ri = jnp.right_shift(lax.broadcasted_iota(jnp.int32, (width, width), 0), shift)
    ci = jnp.right_shift(lax.broadcasted_iota(jnp.int32, (width, width), 1), shift)
    return jnp.where(ri == ci, 1.0, 0.0).astype(BF16)


def _head_sum(x, bd):
    hi = x.astype(BF16)
    lo = (x - hi.astype(F32)).astype(BF16)
    return jnp.dot(hi, bd, preferred_element_type=F32) + jnp.dot(lo, bd, preferred_element_type=F32)


def _pair_swap(x, even):
    n = x.shape[-1]
    return jnp.where(even, pltpu.roll(x, n - 1, 1), pltpu.roll(x, 1, 1))


FA_TQ = 512


FA_TK = 512


def _cols_from_tokens(x, kv):
    w = ATT_G * ATT_DH
    xt = x[:, kv * w:(kv + 1) * w].T
    return jnp.concatenate([xt[g * ATT_DH:(g + 1) * ATT_DH, :] for g in range(ATT_G)], axis=1)


def _tokens_from_cols(c):
    tq = c.shape[1] // ATT_G
    return jnp.concatenate([c[:, g * tq:(g + 1) * tq] for g in range(ATT_G)], axis=0).T


def _store_cols(ref, x, norm_ref=None):
    for kv in range(ATT_KV):
        cols = _cols_from_tokens(x, kv).astype(BF16)
        ref[kv, 0] = cols
        if norm_ref is not None:
            cf = cols.astype(F32)
            norm_ref[kv, 0] = jnp.sqrt(jnp.sum(cf * cf, axis=0, keepdims=True))


def _att_prep_fwd(U, cos, sin, qw, kw, *, name):
    T = U.shape[0]
    tm = min(FA_TQ, T)
    R = ATT_G * tm
    scale = ATT_DH ** -0.5

    def head_rows(ref, x):
        xt = x.astype(F32).T
        for kv in range(ATT_KV):
            ref[kv, 0] = xt[kv * ATT_DH:(kv + 1) * ATT_DH, :].astype(BF16)

    def body(aq_ref, ak_ref, av_ref, c_ref, s_ref, qw_ref, kw_ref, q_ref, qn_ref, kmax_ref, kc_ref, vc_ref):
        @pl.when(pl.program_id(0) == 0)
        def _():
            kmax_ref[...] = jnp.zeros_like(kmax_ref)

        bd = _head_blockdiag(ATT_QW)
        c2, s2 = c_ref[...], s_ref[...]
        c8, s8 = jnp.tile(c2, (1, 4)), jnp.tile(s2, (1, 4))

        def norm_rope(x, w, c, s, bdm):
            r = lax.rsqrt(_head_sum(x * x, bdm) * (1.0 / ATT_DH) + EPS)
            y = x * r * w
            even = (lax.broadcasted_iota(jnp.int32, y.shape, 1) & 1) == 0
            return y * c + _pair_swap(y, even) * s

        _store_cols(q_ref, norm_rope(aq_ref[...], qw_ref[...], c8, s8, bd) * scale, qn_ref)
        kb = norm_rope(ak_ref[...], kw_ref[...], c2, s2, bd[:ATT_KW, :ATT_KW]).astype(BF16)
        kf = kb.astype(F32)
        ksq = _head_sum(kf * kf, bd[:ATT_KW, :ATT_KW])
        kmax_ref[...] = jnp.maximum(kmax_ref[...], jnp.max(ksq, axis=0, keepdims=True))
        head_rows(kc_ref, kb)
        head_rows(vc_ref, av_ref[...].astype(BF16))

    kv_spec = pl.BlockSpec((tm, ATT_KW), lambda i: (i, 0))
    tk = min(FA_TK, T)
    per = tk // tm
    c_spec = pl.BlockSpec((ATT_KV, 1, ATT_DH, tm), lambda i: (0, i // per, 0, i % per))
    c_shape = jax.ShapeDtypeStruct((ATT_KV, T // tk, ATT_DH, tk), BF16)
    return pl.pallas_call(
        body, name=name, grid=(T // tm,),
        in_specs=[pl.BlockSpec((tm, ATT_QW), lambda i: (i, 5)),
                  pl.BlockSpec((tm, ATT_KW), lambda i: (i, 24)), pl.BlockSpec((tm, ATT_KW), lambda i: (i, 25)),
                  kv_spec, kv_spec,
                  pl.BlockSpec((1, ATT_QW), lambda i: (0, 0)), pl.BlockSpec((1, ATT_KW), lambda i: (0, 0))],
        out_specs=[pl.BlockSpec((ATT_KV, 1, ATT_DH, R), lambda i: (0, i, 0, 0)),
                   pl.BlockSpec((ATT_KV, 1, 1, R), lambda i: (0, i, 0, 0)), pl.BlockSpec((1, ATT_KW), lambda i: (0, 0)),
                   c_spec, c_spec],
        out_shape=[jax.ShapeDtypeStruct((ATT_KV, T // tm, ATT_DH, R), BF16),
                   jax.ShapeDtypeStruct((ATT_KV, T // tm, 1, R), F32), jax.ShapeDtypeStruct((1, ATT_KW), F32),
                   c_shape, c_shape],
        compiler_params=_params(("arbitrary",)),
    )(U, U, U, cos, sin, qw, kw)


def _att_prep_bwd(U, dq_c, dk_c, dv_c, cos, sin, qw, kw, *, name):
    T = U.shape[0]
    tm = min(FA_TQ, T)
    R = ATT_G * tm
    scale = ATT_DH ** -0.5

    def body(aq_ref, ak_ref, dq_ref, dk_ref, dv_ref, c_ref, s_ref, qw_ref, kw_ref, out_ref, dqw_ref, dkw_ref):
        @pl.when(pl.program_id(0) == 0)
        def _():
            dqw_ref[...] = jnp.zeros_like(dqw_ref)
            dkw_ref[...] = jnp.zeros_like(dkw_ref)

        bd = _head_blockdiag(ATT_QW)
        c2, s2 = c_ref[...], s_ref[...]
        c8, s8 = jnp.tile(c2, (1, 4)), jnp.tile(s2, (1, 4))

        def bwd(x, dy, w, c, s, bdm):
            even = (lax.broadcasted_iota(jnp.int32, x.shape, 1) & 1) == 0
            dn = dy * c - _pair_swap(dy, even) * s
            r = lax.rsqrt(_head_sum(x * x, bdm) * (1.0 / ATT_DH) + EPS)
            xh = x * r
            dxh = dn * w
            t = _head_sum(dxh * xh, bdm) * (1.0 / ATT_DH)
            return r * (dxh - xh * t), jnp.sum(dn * xh, axis=0, keepdims=True)

        dq = jnp.concatenate([_tokens_from_cols(dq_ref[kv, 0]) for kv in range(ATT_KV)], axis=1)
        da, dw = bwd(aq_ref[...], dq * scale, qw_ref[...], c8, s8, bd)
        out_ref[:, 0:ATT_QW] = da.astype(BF16)
        dqw_ref[...] += dw
        tokens = lambda ref: jnp.concatenate([ref[kv, 0] for kv in range(ATT_KV)], axis=0).T
        da, dw = bwd(ak_ref[...], tokens(dk_ref), kw_ref[...], c2, s2, bd[:ATT_KW, :ATT_KW])
        out_ref[:, ATT_QW:ATT_QW + ATT_KW] = da.astype(BF16)
        dkw_ref[...] += dw
        out_ref[:, ATT_QW + ATT_KW:ATT_QW + 2 * ATT_KW] = tokens(dv_ref).astype(BF16)

    kv_spec = pl.BlockSpec((tm, ATT_KW), lambda i: (i, 0))
    qv = pl.BlockSpec((1, ATT_QW), lambda i: (0, 0))
    kv = pl.BlockSpec((1, ATT_KW), lambda i: (0, 0))
    w_att = ATT_QW + 2 * ATT_KW
    per = dk_c.shape[3] // tm
    c_spec = pl.BlockSpec((ATT_KV, 1, ATT_DH, tm), lambda i: (0, i // per, 0, i % per))
    return pl.pallas_call(
        body, name=name, grid=(T // tm,),
        in_specs=[pl.BlockSpec((tm, ATT_QW), lambda i: (i, 5)), pl.BlockSpec((tm, ATT_KW), lambda i: (i, 24)),
                  pl.BlockSpec((ATT_KV, 1, ATT_DH, R), lambda i: (0, i, 0, 0)), c_spec, c_spec, kv_spec, kv_spec, qv, kv],
        out_specs=[pl.BlockSpec((tm, w_att), lambda i: (i, 0)), qv, kv],
        out_shape=[jax.ShapeDtypeStruct((T, w_att), BF16),
                   jax.ShapeDtypeStruct((1, ATT_QW), F32), jax.ShapeDtypeStruct((1, ATT_KW), F32)],
        compiler_params=_params(("arbitrary",)),
    )(U, U, dq_c, dk_c, dv_c, cos, sin, qw, kw)


def _scores(k_ref, j, qv):
    return lax.dot_general(k_ref[0, j], qv, (((0,), (0,)), ((), ())), preferred_element_type=F32)


def _flash_fwd(q_c, k_c, v_c, *, name):
    _, nq, _, R = q_c.shape
    _, n_k, _, tk = v_c.shape

    def body(q_ref, k_ref, v_ref, o_ref, lse_ref, acc_ref):
        qv = q_ref[0, 0]
        acc_ref[...] = jnp.zeros_like(acc_ref)

        def step(j, carry):
            m, l = carry
            s = _scores(k_ref, j, qv)
            m_new = jnp.maximum(m, jnp.max(s, axis=0, keepdims=True))
            alpha = jnp.exp(m - m_new)
            p = jnp.exp(s - m_new)
            l = alpha * l + jnp.sum(p, axis=0, keepdims=True)
            acc_ref[...] = alpha * acc_ref[...] + jnp.dot(v_ref[0, j], p.astype(BF16), preferred_element_type=F32)
            return m_new, l

        m, l = lax.fori_loop(0, n_k, step, (jnp.full((1, R), -jnp.inf, F32), jnp.zeros((1, R), F32)))
        o_ref[0, 0] = acc_ref[...] / l
        lse_ref[0, 0] = m + jnp.log(l)

    cspec = pl.BlockSpec((1, 1, ATT_DH, R), lambda h, i: (h, i, 0, 0))
    kspec = pl.BlockSpec((1, n_k, ATT_DH, tk), lambda h, i: (h, 0, 0, 0))
    return pl.pallas_call(
        body, name=name, grid=(ATT_KV, nq),
        in_specs=[cspec, kspec, kspec],
        out_specs=[cspec, pl.BlockSpec((1, 1, 1, R), lambda h, i: (h, i, 0, 0))],
        out_shape=[jax.ShapeDtypeStruct((ATT_KV, nq, ATT_DH, R), F32), jax.ShapeDtypeStruct((ATT_KV, nq, 1, R), F32)],
        scratch_shapes=[pltpu.VMEM((ATT_DH, R), F32)],
        compiler_params=_params(("parallel", "parallel")),
    )(q_c, k_c, v_c)


FA_BOUND_MAX = 40.0


def _flash_fwd_bounded(q_c, k_c, v_c, m_c, *, name):
    _, nq, _, R = q_c.shape
    _, n_k, _, tk = v_c.shape

    def body(q_ref, k_ref, v_ref, m_ref, o_ref, lse_ref, acc_ref):
        qv = q_ref[0, 0]
        m = m_ref[0, 0]
        acc_ref[...] = jnp.zeros_like(acc_ref)

        per = math.gcd(n_k, 4)

        def step(jj, l8):
            pv = None
            for u in range(per):
                j = per * jj + u
                p = jnp.exp(_scores(k_ref, j, qv) - m)
                l8 = l8 + jnp.sum(p.reshape(tk // 8, 8, R), axis=0)
                d = jnp.dot(v_ref[0, j], p.astype(BF16), preferred_element_type=F32)
                pv = d if pv is None else pv + d
            acc_ref[...] += pv
            return l8

        l8 = lax.fori_loop(0, n_k // per, step, jnp.zeros((8, R), F32))
        l = jnp.sum(l8, axis=0, keepdims=True)
        o_ref[0, 0] = acc_ref[...] / l
        lse_ref[0, 0] = m + jnp.log(l)

    cspec = pl.BlockSpec((1, 1, ATT_DH, R), lambda h, i: (h, i, 0, 0))
    kspec = pl.BlockSpec((1, n_k, ATT_DH, tk), lambda h, i: (h, 0, 0, 0))
    vspec = pl.BlockSpec((1, 1, 1, R), lambda h, i: (h, i, 0, 0))
    return pl.pallas_call(
        body, name=name, grid=(ATT_KV, nq),
        in_specs=[cspec, kspec, kspec, vspec],
        out_specs=[cspec, vspec],
        out_shape=[jax.ShapeDtypeStruct((ATT_KV, nq, ATT_DH, R), F32), jax.ShapeDtypeStruct((ATT_KV, nq, 1, R), F32)],
        scratch_shapes=[pltpu.VMEM((ATT_DH, R), F32)],
        compiler_params=_params(("parallel", "parallel")),
    )(q_c, k_c, v_c, m_c)


CHIP_MASKS = [(1, 0, 0), (0, 1, 0), (1, 1, 0)]


def _chip_slot(p):
    return 2 * p[0] + p[1]


def _flash_bwd(q_c, k_c, v_c, do_c, lse, delta, *, name, ride=None):
    _, nq, _, R = q_c.shape
    _, n_k, _, tk = k_c.shape
    n_ride = 0 if ride is None else len(ride["arrays"])

    def body(qc_ref, kc_ref, vc_ref, doc_ref, lse_ref, delta_ref, *rest):
        ride_in, rest = rest[:n_ride], rest[n_ride:]
        dq_ref, dk_ref, dv_ref = rest[:3]
        ride_out, rest = rest[3:3 + n_ride], rest[3 + n_ride:]
        acc_ref = rest[0]
        kv = pl.program_id(0)
        if n_ride:
            start, finish = ride["halves"](ride_in, ride_out, *rest[1:])
            pl.when((kv == 0) & (pl.program_id(1) == 0))(start)

        @pl.when(pl.program_id(1) == 0)
        def _():
            dk_ref[...] = jnp.zeros_like(dk_ref)
            dv_ref[...] = jnp.zeros_like(dv_ref)

        qc, doc = qc_ref[0, 0], doc_ref[0, 0]
        lsev, delta = lse_ref[0, 0], delta_ref[0, 0]
        acc_ref[...] = jnp.zeros_like(acc_ref)
        nt = (((1,), (1,)), ((), ()))

        def step(j, carry):
            p = jnp.exp(_scores(kc_ref, j, qc) - lsev)
            dp = _scores(vc_ref, j, doc)
            ds = (p * (dp - delta)).astype(BF16)
            acc_ref[...] += jnp.dot(kc_ref[0, j], ds, preferred_element_type=F32)
            dk_ref[0, j] += lax.dot_general(qc, ds, nt, preferred_element_type=F32)
            dv_ref[0, j] += lax.dot_general(doc, p.astype(BF16), nt, preferred_element_type=F32)
            return carry

        lax.fori_loop(0, n_k, step, 0, unroll=2)
        dq_ref[0, 0] = acc_ref[...]

        if n_ride:
            pl.when((kv == ATT_KV - 1) & (pl.program_id(1) == nq - 1))(finish)

    cspec = pl.BlockSpec((1, 1, ATT_DH, R), lambda h, i: (h, i, 0, 0))
    vspec = pl.BlockSpec((1, 1, 1, R), lambda h, i: (h, i, 0, 0))
    kspec = pl.BlockSpec((1, n_k, ATT_DH, tk), lambda h, i: (h, 0, 0, 0))
    k_shape = jax.ShapeDtypeStruct(k_c.shape, F32)
    return pl.pallas_call(
        body, name=name, grid=(ATT_KV, nq),
        in_specs=[cspec, kspec, kspec, cspec, vspec, vspec] + [ANY] * n_ride,
        out_specs=[cspec, kspec, kspec] + [ANY] * n_ride,
        out_shape=[jax.ShapeDtypeStruct((ATT_KV, nq, ATT_DH, R), F32), k_shape, k_shape]
                  + (ride["out_shape"] if n_ride else []),
        scratch_shapes=[pltpu.VMEM((ATT_DH, R), F32)] + (ride["scratch"] if n_ride else []),
        compiler_params=pltpu.CompilerParams(dimension_semantics=("arbitrary", "arbitrary"),
                                             vmem_limit_bytes=VMEM_LIMIT, has_side_effects=bool(n_ride)),
    )(q_c, k_c, v_c, do_c, lse, delta, *(ride["arrays"] if n_ride else []))


def _att_post_fwd(o_c, w, *, name):
    _, nq, _, R = o_c.shape
    tm = R // ATT_G
    T = nq * tm

    def body(oc_ref, w_ref, o_ref, out_ref):
        ov = jnp.concatenate([_tokens_from_cols(oc_ref[kv, 0]) for kv in range(ATT_KV)], axis=1)
        r = lax.rsqrt(jnp.mean(ov * ov, axis=-1, keepdims=True) + EPS)
        o_ref[...] = ov
        out_ref[...] = (ov * r * w_ref[...]).astype(BF16)

    tok = pl.BlockSpec((tm, ATT_QW), lambda i: (i, 0))
    return pl.pallas_call(
        body, name=name, grid=(nq,),
        in_specs=[pl.BlockSpec((ATT_KV, 1, ATT_DH, R), lambda i: (0, i, 0, 0)), pl.BlockSpec((1, ATT_QW), lambda i: (0, 0))],
        out_specs=[tok, tok],
        out_shape=[jax.ShapeDtypeStruct((T, ATT_QW), F32), jax.ShapeDtypeStruct((T, ATT_QW), BF16)],
        compiler_params=_params(("parallel",)),
    )(o_c, w)


def _att_post_bwd(dmix, o, w, *, name):
    T = o.shape[0]
    tm = min(FA_TQ, T)
    R = ATT_G * tm

    def body(dm_ref, o_ref, w_ref, do_ref, delta_ref, dw_ref):
        @pl.when(pl.program_id(0) == 0)
        def _():
            dw_ref[...] = jnp.zeros_like(dw_ref)

        ov = o_ref[...]
        r = lax.rsqrt(jnp.mean(ov * ov, axis=-1, keepdims=True) + EPS)
        xh = ov * r
        dm = dm_ref[...]
        dxh = dm * w_ref[...]
        t = jnp.mean(dxh * xh, axis=-1, keepdims=True)
        do = r * (dxh - xh * t)
        _store_cols(do_ref, do)
        dob = do.astype(BF16).astype(F32)
        for kv in range(ATT_KV):
            delta_ref[kv, 0] = jnp.sum(_cols_from_tokens(dob * ov, kv), axis=0, keepdims=True)
        dw_ref[...] += jnp.sum(dm * xh, axis=0, keepdims=True)

    tok = pl.BlockSpec((tm, ATT_QW), lambda i: (i, 0))
    vec = pl.BlockSpec((1, ATT_QW), lambda i: (0, 0))
    return pl.pallas_call(
        body, name=name, grid=(T // tm,),
        in_specs=[pl.BlockSpec((tm, ATT_QW), lambda i: (i, 1)), tok, vec],
        out_specs=[pl.BlockSpec((ATT_KV, 1, ATT_DH, R), lambda i: (0, i, 0, 0)),
                   pl.BlockSpec((ATT_KV, 1, 1, R), lambda i: (0, i, 0, 0)), vec],
        out_shape=[jax.ShapeDtypeStruct((ATT_KV, T // tm, ATT_DH, R), BF16),
                   jax.ShapeDtypeStruct((ATT_KV, T // tm, 1, R), F32), jax.ShapeDtypeStruct((1, ATT_QW), F32)],
        compiler_params=_params(("arbitrary",)),
    )(dmix, o, w)


def _ffn_up(h2, wg_t, wu_t, *, name, tm=512):
    T = h2.shape[0]
    tn = _pick(D_FF, 1408)
    nt = (((1,), (1,)), ((), ()))

    def body(h_ref, wg_ref, wu_ref, g_ref, u_ref, a_ref):
        hv = h_ref[...]
        g = lax.dot_general(hv, wg_ref[...], nt, preferred_element_type=F32)
        u = lax.dot_general(hv, wu_ref[...], nt, preferred_element_type=F32)
        g_ref[...] = g.astype(BF16)
        u_ref[...] = u.astype(BF16)
        a_ref[...] = (g * _sigmoid(g) * u).astype(BF16)

    wspec = pl.BlockSpec((tn, D_MODEL), lambda i, j: (j, 0))
    ospec = pl.BlockSpec((tm, tn), lambda i, j: (i, j))
    return pl.pallas_call(
        body, name=name, grid=(T // tm, D_FF // tn),
        in_specs=[pl.BlockSpec((tm, D_MODEL), lambda i, j: (i, 0)), wspec, wspec],
        out_specs=[ospec] * 3, out_shape=[jax.ShapeDtypeStruct((T, D_FF), BF16)] * 3,
        compiler_params=_params(("parallel", "arbitrary")),
    )(h2, wg_t, wu_t)


def _ffn_act_bwd(dx2b, w_down, gate, up, *, name, tm=512):
    T = dx2b.shape[0]
    tn = _pick(D_FF, 1408)

    def body(dx_ref, w_ref, g_ref, u_ref, dg_ref, du_ref):
        da = lax.dot_general(dx_ref[...], w_ref[...], (((1,), (1,)), ((), ())), preferred_element_type=F32)
        g = g_ref[...].astype(F32)
        u = u_ref[...].astype(F32)
        sg = _sigmoid(g)
        dg_ref[...] = (da * u * (sg * (1.0 + g * (1.0 - sg)))).astype(BF16)
        du_ref[...] = (da * (g * sg)).astype(BF16)

    ospec = pl.BlockSpec((tm, tn), lambda i, j: (i, j))
    return pl.pallas_call(
        body, name=name, grid=(T // tm, D_FF // tn),
        in_specs=[pl.BlockSpec((tm, D_MODEL), lambda i, j: (i, 0)),
                  pl.BlockSpec((tn, D_MODEL), lambda i, j: (j, 0)), ospec, ospec],
        out_specs=[ospec] * 2, out_shape=[jax.ShapeDtypeStruct((T, D_FF), BF16)] * 2,
        compiler_params=_params(("parallel", "arbitrary")),
    )(dx2b, w_down, gate, up)


def _adam_math(w, g, m, v):
    m = ADAM_B1 * m + (1.0 - ADAM_B1) * g
    v = ADAM_B2 * v + (1.0 - ADAM_B2) * (g * g)
    m_hat = m / (1.0 - ADAM_B1 ** ADAM_STEP)
    v_hat = v / (1.0 - ADAM_B2 ** ADAM_STEP)
    delta = -ADAM_LR * (m_hat / (jnp.sqrt(v_hat) + ADAM_EPS) + ADAM_WD * w)
    return delta, m, v


def _adamw(parts, w, m, v, *, name, tr_cap=256):
    P, R, C = parts.shape
    tr = R
    for t in range(8, min(R, tr_cap) + 1, 8):
        if R % t == 0:
            tr = t

    def body(p_ref, w_ref, m_ref, v_ref, g_ref, d_ref, nm_ref, nv_ref):
        g = p_ref[0].astype(F32)
        for j in range(1, P):
            g = g + p_ref[j].astype(F32)
        d, nm, nv = _adam_math(w_ref[...], g, m_ref[...], v_ref[...])
        g_ref[...] = g
        d_ref[...] = d
        nm_ref[...] = nm
        nv_ref[...] = nv

    blk = pl.BlockSpec((tr, C), lambda i: (i, 0))
    return pl.pallas_call(
        body, name=name, grid=(R // tr,),
        in_specs=[pl.BlockSpec((P, tr, C), lambda i: (0, i, 0)), blk, blk, blk],
        out_specs=[blk] * 4, out_shape=[jax.ShapeDtypeStruct((R, C), F32)] * 4,
        compiler_params=_params(("parallel",)),
    )(parts, w, m, v)


def _gather_halves(ins, outs, send_sems, recv_sems, local_sems):
    n = len(ins)
    x, y, c = lax.axis_index("x"), lax.axis_index("y"), lax.axis_index("c")
    me, sibling = (x, y, c), (x, y, 1 - c)
    chips = [(1 - x, y), (x, 1 - y), (1 - x, 1 - y)]

    def slot(p):
        return 4 * p[0] + 2 * p[1] + p[2]

    def copy(a, k, block, to, src=None):
        dst = outs[a].at[slot(block)]
        return pltpu.make_async_remote_copy(
            src_ref=dst if src is None else src, dst_ref=dst,
            send_sem=send_sems.at[a * 7 + k], recv_sem=recv_sems.at[a * 7 + k],
            device_id=to, device_id_type=MESH)

    mine = [pltpu.make_async_copy(ins[a], outs[a].at[slot(me)], local_sems.at[a]) for a in range(n)]
    first = []
    for a in range(n):
        first.append(copy(a, 0, me, sibling, src=ins[a]))
        first += [copy(a, 1 + j, me, (*chip, c), src=ins[a]) for j, chip in enumerate(chips)]

    def start():
        for cp in mine + first:
            cp.start()

    def finish():
        passed = []
        for j, chip in enumerate(chips):
            for a in range(n):
                copy(a, 1 + j, (*chip, c), me).wait_recv()
                cp = copy(a, 4 + j, (*chip, c), sibling)
                cp.start()
                passed.append(cp)
        for a in range(n):
            copy(a, 0, sibling, me).wait_recv()
            for j, chip in enumerate(chips):
                copy(a, 4 + j, (*chip, 1 - c), me).wait_recv()
        for cp in first + passed:
            cp.wait_send()
        for cp in mine:
            cp.wait()

    return start, finish


def _gather_scratch(n):
    return [pltpu.SemaphoreType.DMA((7 * n,)), pltpu.SemaphoreType.DMA((7 * n,)), pltpu.SemaphoreType.DMA((n,))]


def _gathered_shapes(xs):
    return [jax.ShapeDtypeStruct((N_DEV,) + x.shape, x.dtype) for x in xs]


def _ride_gather(xs):
    xs = list(xs)
    return dict(arrays=xs, out_shape=_gathered_shapes(xs), scratch=_gather_scratch(len(xs)), halves=_gather_halves)


def _ride_chips(gs):
    gs = list(gs)
    n = len(gs)

    def halves(ins, outs, send_sems, recv_sems, local_sems):
        mine, copies = _exchange_copies(ins, outs, send_sems, recv_sems, local_sems, masks=CHIP_MASKS, slot=_chip_slot)

        def start():
            for cp in mine:
                cp.start()
            for send, _ in copies:
                send.start()

        def finish():
            for send, recv in copies:
                recv.wait_recv()
                send.wait_send()
            for cp in mine:
                cp.wait()

        return start, finish

    n_sem = len(CHIP_MASKS) * n
    return dict(arrays=gs, out_shape=[jax.ShapeDtypeStruct(g.shape, g.dtype) for g in gs], halves=halves,
                scratch=[pltpu.SemaphoreType.DMA((n_sem,)), pltpu.SemaphoreType.DMA((n_sem,)),
                         pltpu.SemaphoreType.DMA((n,))])


ALL_MASKS = [(mx, my, mc) for mx in (0, 1) for my in (0, 1) for mc in (0, 1)][1:]


def _flip(v, bit):
    return 1 - v if bit else v


def _exchange_copies(ins, outs, send_sems, recv_sems, local_sems, *, masks, slot):
    n, n_peers = len(ins), len(masks)
    x, y, c = lax.axis_index("x"), lax.axis_index("y"), lax.axis_index("c")
    my_slot = slot((x, y, c))
    mine = [pltpu.make_async_copy(ins[a].at[my_slot], outs[a].at[my_slot], local_sems.at[a]) for a in range(n)]
    copies = []
    for a in range(n):
        for k, (mx, my, mc) in enumerate(masks):
            peer = (_flip(x, mx), _flip(y, my), _flip(c, mc))
            peer_slot = slot(peer)
            sems = dict(send_sem=send_sems.at[a * n_peers + k], recv_sem=recv_sems.at[a * n_peers + k],
                        device_id=peer, device_id_type=MESH)
            copies.append((
                pltpu.make_async_remote_copy(src_ref=ins[a].at[peer_slot], dst_ref=outs[a].at[my_slot], **sems),
                pltpu.make_async_remote_copy(src_ref=ins[a].at[peer_slot], dst_ref=outs[a].at[peer_slot], **sems)))
    return mine, copies


def _send_to_all(v, *, name):
    def body(v_ref, out_ref, send_sems, recv_sems, local_sem):
        x, y, c = lax.axis_index("x"), lax.axis_index("y"), lax.axis_index("c")
        me = 4 * x + 2 * y + c
        mine = pltpu.make_async_copy(v_ref, out_ref.at[me], local_sem)
        mine.start()
        copies = []
        for k, (mx, my, mc) in enumerate(ALL_MASKS):
            peer = (_flip(x, mx), _flip(y, my), _flip(c, mc))
            peer_id = 4 * peer[0] + 2 * peer[1] + peer[2]
            sems = dict(send_sem=send_sems.at[k], recv_sem=recv_sems.at[k], device_id=peer, device_id_type=MESH)
            copies.append((pltpu.make_async_remote_copy(src_ref=v_ref, dst_ref=out_ref.at[me], **sems),
                           pltpu.make_async_remote_copy(src_ref=v_ref, dst_ref=out_ref.at[peer_id], **sems)))
        for send, _ in copies:
            send.start()
        for send, recv in copies:
            recv.wait_recv()
            send.wait_send()
        mine.wait()

    n_peers = len(ALL_MASKS)
    return pl.pallas_call(
        body, name=name, in_specs=[ANY], out_specs=ANY,
        out_shape=jax.ShapeDtypeStruct((N_DEV,) + v.shape, v.dtype),
        scratch_shapes=[pltpu.SemaphoreType.DMA((n_peers,)), pltpu.SemaphoreType.DMA((n_peers,)),
                        pltpu.SemaphoreType.DMA],
        compiler_params=pltpu.CompilerParams(has_side_effects=True),
    )(v)


SWAP_ROW_CHUNKS = 4


def _ride_swap(gs):
    gs = list(gs)
    n = len(gs)

    def halves(ins, outs, send_sems, recv_sems):
        x, y, c = lax.axis_index("x"), lax.axis_index("y"), lax.axis_index("c")
        sibling = dict(device_id=(x, y, 1 - c), device_id_type=MESH)

        def start():
            for a in range(n):
                Q, _, R, _ = ins[a].shape
                rows = R // SWAP_ROW_CHUNKS
                for q in range(Q):
                    for j in range(SWAP_ROW_CHUNKS):
                        part = pl.ds(j * rows, rows)
                        pltpu.make_async_remote_copy(src_ref=ins[a].at[q, 1 - c, part], dst_ref=outs[a].at[q, part],
                                                     send_sem=send_sems.at[a], recv_sem=recv_sems.at[a], **sibling).start()

        def finish():
            for a in range(n):
                pltpu.make_async_remote_copy(src_ref=outs[a], dst_ref=outs[a], send_sem=send_sems.at[a],
                                             recv_sem=recv_sems.at[a], **sibling).wait()

        return start, finish

    return dict(arrays=gs, out_shape=[jax.ShapeDtypeStruct(g.shape[:1] + g.shape[2:], g.dtype) for g in gs],
                scratch=[pltpu.SemaphoreType.DMA((n,)), pltpu.SemaphoreType.DMA((n,))], halves=halves)


def _core_swap(gs, *, name):
    ride = _ride_swap(gs)
    n = len(gs)

    def body(*refs):
        start, finish = ride["halves"](refs[:n], refs[n:2 * n], *refs[2 * n:])
        start()
        finish()

    return pl.pallas_call(
        body, name=name, in_specs=[ANY] * n, out_specs=[ANY] * n, out_shape=ride["out_shape"],
        scratch_shapes=ride["scratch"], compiler_params=pltpu.CompilerParams(has_side_effects=True),
    )(*gs)


def _pair_sum(g, other, core, *, name, tr_cap=256):
    Q, _, R, C = g.shape
    tr = max(t for t in range(16, min(R, tr_cap) + 1, 16) if R % t == 0)

    def body(core_ref, g_ref, o_ref, out_ref):
        out_ref[0] = (g_ref[0, 0] + o_ref[0]).astype(BF16)

    return pl.pallas_call(
        body, name=name,
        grid_spec=pltpu.PrefetchScalarGridSpec(
            num_scalar_prefetch=1, grid=(Q, R // tr),
            in_specs=[pl.BlockSpec((1, 1, tr, C), lambda q, i, core_ref: (q, core_ref[0], i, 0)),
                      pl.BlockSpec((1, tr, C), lambda q, i, core_ref: (q, i, 0))],
            out_specs=pl.BlockSpec((1, tr, C), lambda q, i, core_ref: (q, i, 0))),
        out_shape=jax.ShapeDtypeStruct((Q, R, C), BF16),
        compiler_params=_params(("parallel", "parallel")),
    )(core, g, other)


def _pack_small(norm1, norm2, final, att, hg, qn, kn, lb=None, loss=None):
    z = lambda n: jnp.zeros((n,), F32)
    rows = [norm1.reshape(-1), norm2.reshape(-1), final.reshape(-1),
            jnp.concatenate([att.reshape(-1), z(512)]),
            jnp.concatenate([hg.reshape(-1), qn.reshape(-1), kn.reshape(-1), z(1024 - 256)]),
            z(1024) if lb is None else lb.reshape(-1),
            z(1024) if loss is None else jnp.concatenate([loss.reshape(-1), z(1023)]), z(1024)]
    return jnp.stack(rows, axis=0)


def _unpack_small(p):
    return (p[0:1, :], p[1:2, :], p[2, :], p[3:4, 0:512], p[4:5, 0:128], p[4:5, 128:192], p[4:5, 192:256])


def _fold_heads(dhg, dqn, dkn, *, name):
    def body(hg_ref, q_ref, k_ref, ohg_ref, oq_ref, ok_ref):
        def fold128(v):
            acc = v[:, 0:LANES]
            for j in range(1, v.shape[1] // LANES):
                acc = acc + v[:, j * LANES:(j + 1) * LANES]
            return acc

        ohg_ref[...] = fold128(hg_ref[...])
        q = fold128(q_ref[...])
        oq_ref[...] = q + pltpu.roll(q, ATT_DH, 1)
        k = k_ref[...]
        ok_ref[...] = k + pltpu.roll(k, ATT_DH, 1)

    return pl.pallas_call(body, name=name, out_shape=[jax.ShapeDtypeStruct((1, LANES), F32)] * 3)(dhg, dqn, dkn)


def _lb_grad(dlb_sum, lb, *, name):
    def body(d_ref, lb_ref, o_ref):
        lbv = lb_ref[...]
        gl = d_ref[...] * lbv * (1.0 - lbv)
        o_ref[0:1, :] = gl[0:1, :]
        o_ref[1:2, :] = -gl[0:1, :]
        o_ref[2:3, :] = gl[1:2, :]
        o_ref[3:4, :] = -gl[1:2, :]

    return pl.pallas_call(body, name=name, out_shape=jax.ShapeDtypeStruct((4, HG_W), F32))(dlb_sum, lb)


def _lower_bounds(lb_logits_full, *, name):
    def body(l_ref, o_ref):
        for d in range(2):
            l0, l1 = l_ref[2 * d:2 * d + 1, :], l_ref[2 * d + 1:2 * d + 2, :]
            mx = jnp.maximum(l0, l1)
            e0, e1 = jnp.exp(l0 - mx), jnp.exp(l1 - mx)
            o_ref[d:d + 1, :] = e0 / (e0 + e1)

    return pl.pallas_call(body, name=name, out_shape=jax.ShapeDtypeStruct((2, HG_W), F32))(
        lb_logits_full.reshape(4, HG_W))


def _local_step(x, target, norm1_w, w_in_t, lb, hg_norm_w, q_norm_w, k_norm_w, att_norm_w, w_out, norm2_w,
                w_g_t, w_u_t, w_down, final_norm_w, reduce_early=None, reduce_late=None, shards=None):
    T = x.shape[0]
    cos, sin = _rope_tables(T)
    qw8 = jnp.tile(q_norm_w, (1, ATT_HEADS))
    kw2 = jnp.tile(k_norm_w, (1, ATT_KV))

    if shards is None:
        h, r1 = _rms_fwd(x, norm1_w, name="norm1_fwd")
        U = _mm_nn([(h, w_in_t)], trans_b=True, name="in_proj")
        o_f, st_f = _gla_fwd(U, lb[0:1], f_block=1, reverse=False, name="gla_fwd_f")
    else:
        h, r1, g_in, g_lb = _rms_fwd(x, norm1_w, ride=_ride_gather([shards["w_in_t"], shards["lb_logits"]]),
                                     name="norm1_fwd")
        w_in_t = g_in.reshape(-1, D_MODEL)
        lb = _lower_bounds(g_lb.transpose(1, 0, 2).reshape(2, 2, -1), name="lower_bounds")
        U, g_gu = _mm_nn([(h, w_in_t)], trans_b=True, ride=_ride_gather([shards["w_gu_t"]]), name="in_proj")
        o_f, st_f, g_out, g_dn = _gla_fwd(U, lb[0:1], f_block=1, reverse=False,
                                          ride=_ride_gather([shards["w_out"], shards["w_down"]]), name="gla_fwd_f")
        g_gu = g_gu.reshape(2, -1, D_MODEL)
        w_g_t, w_u_t = g_gu[0], g_gu[1]
        w_out, w_down = g_out.reshape(-1, D_MODEL), g_dn.reshape(-1, D_MODEL)
    o_b, st_b = _gla_fwd(U, lb[1:2], f_block=2, reverse=True, name="gla_fwd_b")
    mix_hg = _hg_post_fwd(o_f, o_b, U, hg_norm_w, name="hg_post_fwd")
    q_c, qn_c, kmax2, k_c, v_c = _att_prep_fwd(U, cos, sin, qw8, kw2, name="att_prep_fwd")
    kmax = jnp.sqrt(jnp.max(kmax2.reshape(ATT_KV, ATT_DH), axis=1))
    m_c = qn_c * (kmax * 1.001).reshape(ATT_KV, 1, 1, 1)
    o_c, lse = lax.cond(jnp.max(m_c) <= FA_BOUND_MAX,
                        lambda: _flash_fwd_bounded(q_c, k_c, v_c, m_c, name="flash_fwd_bounded"),
                        lambda: _flash_fwd(q_c, k_c, v_c, name="flash_fwd"))
    o_att, mix_att = _att_post_fwd(o_c, att_norm_w, name="att_post_fwd")
    x1, h2, r2 = _mm_nn([(mix_hg, w_out[:HG_W]), (mix_att, w_out[HG_W:])], residual=x, tail=_tail_rms_fwd(norm2_w),
                        name="out_proj")
    gate, up, act = _ffn_up(h2, w_g_t, w_u_t, name="ffn_up")
    loss, dx2, dx2b, d_final = _mm_nn([(act, w_down)], residual=x1,
                                      tail=_tail_loss(target, final_norm_w.reshape(1, D_MODEL)), name="ffn_down")

    d_gate, d_up = _ffn_act_bwd(dx2b, w_down, gate, up, name="ffn_act_bwd")
    dw_down = _mm_tn(act, dx2b, tma_cap=1408, name="dw_down")
    dw_g = _mm_tn(d_gate, h2, tma_cap=1408, name="dw_gate")
    dw_u = _mm_tn(d_up, h2, tma_cap=1408, name="dw_up")
    mine = None if reduce_early is None else reduce_early["slabs"](dw_g, dw_u, dw_down)
    dx1, dx1b, d_norm2, *theirs = _mm_nn([(d_gate, w_g_t), (d_up, w_u_t)], tm=256,
                                         ride=None if mine is None else _ride_swap(mine),
                                         tail=_tail_rms_bwd(x1, r2, norm2_w, dx2, emit_bf16=True), name="ffn_up_bwd")
    dmix = _mm_nn([(dx1b, w_out)], trans_b=True, name="out_proj_bwd")
    dw_out = jnp.concatenate([_mm_tn(mix_hg, dx1b, name="dw_out_hg"), _mm_tn(mix_att, dx1b, name="dw_out_att")], axis=0)
    do_c, delta, d_att = _att_post_bwd(dmix, o_att, att_norm_w, name="att_post_bwd")
    ride = None if reduce_early is None else _ride_chips(reduce_early["sums"](mine, theirs, dw_out))
    dq_c, dk_c, dv_c, *rode = _flash_bwd(q_c, k_c, v_c, do_c, lse, delta, ride=ride, name="flash_bwd")
    dU_att, d_qn, d_kn = _att_prep_bwd(U, dq_c, dk_c, dv_c, cos, sin, qw8, kw2, name="att_prep_bwd")
    do_hg, du_g, d_hg = _hg_post_bwd(dmix, o_f, o_b, U, hg_norm_w, name="hg_post_bwd")
    dq_f, dz_f, dv_f, dlb_f = _gla_bwd(U, lb[0:1], do_hg, st_f, f_block=1, reverse=False, name="gla_bwd_f")
    dU_hg, dlb_b = _gla_bwd(U, lb[1:2], do_hg, st_b, f_block=2, reverse=True, prev=(dq_f, dz_f, dv_f, du_g),
                            name="gla_bwd_b")
    w_hg = 5 * HG_W
    dw_in = jnp.concatenate([_mm_tn(dU_hg, h, tma_cap=1280, name="dw_in_hg"), _mm_tn(dU_att, h, name="dw_in_att")],
                            axis=0)
    late = None if reduce_late is None else _ride_chips(reduce_late(dw_in))
    grad_x, d_norm1, *rode_late = _mm_nn([(dU_hg, w_in_t[:w_hg]), (dU_att, w_in_t[w_hg:])], ride=late,
                                         tail=_tail_rms_bwd(x, r1, norm1_w, dx1, emit_bf16=False), name="in_proj_bwd")
    d_hg, d_qn, d_kn = _fold_heads(d_hg, d_qn, d_kn, name="fold_heads")

    big = dict(w_in=dw_in, w_out=dw_out, w_g=dw_g, w_u=dw_u, w_down=dw_down)
    small = dict(norm1=d_norm1, norm2=d_norm2, final=d_final, att=d_att, hg=d_hg,
                 qn=d_qn[:, :ATT_DH], kn=d_kn[:, :ATT_DH], lb=jnp.concatenate([dlb_f, dlb_b], axis=0))
    return loss, grad_x, big, small, rode + rode_late, lb


def kernel(x, norm1_w, w_in, lb_logits, hg_norm_w, q_norm_w, k_norm_w, att_norm_w, w_out, norm2_w, w_gate_up, w_down, final_norm_w, loss_target, m_norm1_w, m_w_in, m_lb_logits, m_hg_norm_w, m_q_norm_w, m_k_norm_w, m_att_norm_w, m_w_out, m_norm2_w, m_w_gate_up, m_w_down, m_final_norm_w, v_norm1_w, v_w_in, v_lb_logits, v_hg_norm_w, v_q_norm_w, v_k_norm_w, v_att_norm_w, v_w_out, v_norm2_w, v_w_gate_up, v_w_down, v_final_norm_w):
    T = x.shape[1]
    me = 4 * lax.axis_index("x") + 2 * lax.axis_index("y") + lax.axis_index("c")
    c_in, r_out, c_gu, r_dn = w_in.shape[2], w_out.shape[1], w_gate_up.shape[2], w_down.shape[1]
    lb_cols = lb_logits.shape[2]

    shards = dict(w_in_t=w_in[0].T.astype(BF16), lb_logits=lb_logits.reshape(4, lb_cols),
                  w_gu_t=w_gate_up[0].T.astype(BF16), w_out=w_out[0].astype(BF16), w_down=w_down[0].astype(BF16))

    chips = N_DEV // 2
    core = lax.axis_index("c").astype(jnp.int32).reshape(1)
    by_owner = lambda g, r: g.reshape(chips, 2, r, D_MODEL)

    def pair_sums(mine, theirs, names):
        return [_pair_sum(g, o, core, name="pair_sum_" + nm) for g, o, nm in zip(mine, theirs, names)]

    def early_slabs(dw_g_t, dw_u_t, dw_down):
        return [by_owner(jnp.concatenate([dw_g_t, dw_u_t], axis=0), c_gu), by_owner(dw_down, r_dn)]

    def early_sums(mine, theirs, dw_out):
        s_out = by_owner(dw_out, r_out)
        return pair_sums([s_out] + mine, list(_core_swap([s_out], name="exchange_cores_out")) + list(theirs),
                         ("w_out", "w_gu", "w_down"))

    def reduce_late(dw_in_t):
        mine = [by_owner(dw_in_t, c_in)]
        return pair_sums(mine, _core_swap(mine, name="exchange_cores_in"), ("w_in",))

    loss, grad_x, big, small, (p_out, p_gu, p_dn, p_in), lb = _local_step(
        x[0], loss_target[0], norm1_w, None, None, hg_norm_w, q_norm_w, k_norm_w, att_norm_w, None, norm2_w,
        None, None, None, final_norm_w, reduce_early=dict(slabs=early_slabs, sums=early_sums),
        reduce_late=reduce_late, shards=shards)
    p_gu, p_in = p_gu.transpose(0, 2, 1), p_in.transpose(0, 2, 1)

    packed = _pack_small(small["norm1"], small["norm2"], small["final"], small["att"], small["hg"],
                         small["qn"], small["kn"], small["lb"], loss)
    all_small = _send_to_all(packed, name="exchange_small")

    g_w_in, d_w_in, nm_w_in, nv_w_in = _adamw(p_in, w_in[0], m_w_in[0], v_w_in[0], name="adamw_w_in")
    g_w_out, d_w_out, nm_w_out, nv_w_out = _adamw(p_out, w_out[0], m_w_out[0], v_w_out[0], name="adamw_w_out")
    g_w_gu, d_w_gu, nm_w_gu, nv_w_gu = _adamw(p_gu, w_gate_up[0], m_w_gate_up[0], v_w_gate_up[0], name="adamw_w_gu")
    g_w_dn, d_w_dn, nm_w_dn, nv_w_dn = _adamw(p_dn, w_down[0], m_w_down[0], v_w_down[0], name="adamw_w_down")

    pk = lambda vecs: _pack_small(*vecs)
    w_pk = pk([norm1_w, norm2_w, final_norm_w, att_norm_w, hg_norm_w, q_norm_w, k_norm_w])
    m_pk = pk([m_norm1_w, m_norm2_w, m_final_norm_w, m_att_norm_w, m_hg_norm_w, m_q_norm_w, m_k_norm_w])
    v_pk = pk([v_norm1_w, v_norm2_w, v_final_norm_w, v_att_norm_w, v_hg_norm_w, v_q_norm_w, v_k_norm_w])
    g_pk, d_pk, nm_pk, nv_pk = _adamw(all_small, w_pk, m_pk, v_pk, name="adamw_small")

    dlb_sum = g_pk[5:6, :].reshape(2, HG_W)
    g_lb_full = _lb_grad(dlb_sum, lb, name="lb_grad")
    g_lb_mine = lax.dynamic_slice_in_dim(g_lb_full, me * lb_cols, lb_cols, axis=1)
    g_lb_s, d_lb, nm_lb, nv_lb = _adamw(g_lb_mine[None], lb_logits.reshape(4, lb_cols),
                                        m_lb_logits.reshape(4, lb_cols), v_lb_logits.reshape(4, lb_cols),
                                        name="adamw_lb")

    loss_total = g_pk[6, 0]

    def outs(big4, lb_arr, pk_arr):
        n1, n2, fin, att, hg, qn, kn = _unpack_small(pk_arr)
        b_in, b_out, b_gu, b_dn = big4
        return [n1, b_in[None], lb_arr.reshape(2, 2, lb_cols), hg, qn, kn, att, b_out[None], n2, b_gu[None],
                b_dn[None], fin]

    return (loss_total, grad_x[None],
            *outs((g_w_in, g_w_out, g_w_gu, g_w_dn), g_lb_s, g_pk),
            *outs((d_w_in, d_w_out, d_w_gu, d_w_dn), d_lb, d_pk),
            *outs((nm_w_in, nm_w_out, nm_w_gu, nm_w_dn), nm_lb, nm_pk),
            *outs((nv_w_in, nv_w_out, nv_w_gu, nv_w_dn), nv_lb, nv_pk))
```

```python
import math

import jax
import jax.numpy as jnp
import numpy as np
from jax import lax
from jax.experimental import pallas as pl
from jax.experimental.pallas import tpu as pltpu

F32 = jnp.float32
BF16 = jnp.bfloat16

N_DEV = 8
D_MODEL = 1024
EPS = 1e-6
HG_HEADS = 4
HG_D = 128
HG_W = HG_HEADS * HG_D
CHUNK = 64
ATT_HEADS = 8
ATT_KV = 2
ATT_G = ATT_HEADS // ATT_KV
ATT_DH = 64
ATT_QW = ATT_HEADS * ATT_DH
ATT_KW = ATT_KV * ATT_DH
GRID_W = 64
ROPE_THETA = 10000.0
D_FF = 2816
ADAM_LR, ADAM_B1, ADAM_B2, ADAM_EPS, ADAM_WD, ADAM_STEP = 0.001, 0.9, 0.999, 1e-08, 0.01, 10

LOG2E = math.log2(math.e)
LANES = 128
VMEM_LIMIT = 48 * 1024 * 1024
MESH = pl.DeviceIdType.MESH
ANY = pl.BlockSpec(memory_space=pl.ANY)


def _params(sem=None):
    return pltpu.CompilerParams(dimension_semantics=sem, vmem_limit_bytes=VMEM_LIMIT)


def _pick(n, cap):
    best = None
    for t in range(LANES, cap + 1, LANES):
        if n % t == 0:
            best = t
    assert best is not None, (n, cap)
    return best


def _sigmoid(x):
    return 1.0 / (1.0 + jnp.exp(-x))


def _dot(a, b):
    return jnp.dot(a.astype(BF16), b.astype(BF16), preferred_element_type=F32)


def _dot_nt(a, b):
    return lax.dot_general(a.astype(BF16), b.astype(BF16), (((1,), (1,)), ((), ())),
                           preferred_element_type=F32)


def _dot_tn(a, b):
    return lax.dot_general(a.astype(BF16), b.astype(BF16), (((0,), (0,)), ((), ())),
                           preferred_element_type=F32)


def _mm_nn(pairs, *, name, out_dtype=F32, residual=None, tm=512, tn_cap=None, trans_b=False, tail=None, ride=None):
    M = pairs[0][0].shape[0]
    N = pairs[0][1].shape[0 if trans_b else 1]
    tn = N if tn_cap is None else _pick(N, tn_cap)
    n_pairs = len(pairs)
    has_res = residual is not None
    dims = (((1,), (1,)), ((), ())) if trans_b else (((1,), (0,)), ((), ()))
    assert (tail is None and ride is None) or tn == N
    n_main = 2 * n_pairs + has_res
    n_ti = 0 if tail is None else len(tail["ins"])
    n_out = 1 if tail is None else len(tail["outs"])
    n_r = 0 if ride is None else len(ride["arrays"])
    n_in = n_main + n_ti + n_r

    def body(*refs):
        outs = refs[n_in:n_in + n_out]
        if n_r:
            start, finish = ride["halves"](refs[n_main + n_ti:n_in], refs[n_in + n_out:n_in + n_out + n_r],
                                           *refs[n_in + n_out + n_r:])
            pl.when(pl.program_id(0) == 0)(start)
        acc = None
        for i in range(n_pairs):
            d = lax.dot_general(refs[2 * i][...], refs[2 * i + 1][...], dims, preferred_element_type=F32)
            acc = d if acc is None else acc + d
        if has_res:
            acc = acc + refs[2 * n_pairs][...]
        if tail is None:
            outs[0][...] = acc.astype(out_dtype)
        else:
            tail["fn"](acc, pl.program_id(0) == 0, *refs[n_main:n_main + n_ti], *outs)
        if n_r:
            pl.when(pl.program_id(0) == M // tm - 1)(finish)

    kinds = {"row": ((tm, N), (M, N), lambda i, j: (i, 0)), "col": ((tm, 1), (M, 1), lambda i, j: (i, 0)),
             "vec": ((1, N), (1, N), lambda i, j: (0, 0)), "one": ((1, 1), (1, 1), lambda i, j: (0, 0))}
    in_specs, args = [], []
    for a, b in pairs:
        k = a.shape[1]
        b_spec = pl.BlockSpec((tn, k), lambda i, j: (j, 0)) if trans_b else pl.BlockSpec((k, tn), lambda i, j: (0, j))
        in_specs += [pl.BlockSpec((tm, k), lambda i, j: (i, 0)), b_spec]
        args += [a, b]
    if has_res:
        in_specs.append(pl.BlockSpec((tm, tn), lambda i, j: (i, j)))
        args.append(residual)
    if tail is None:
        out_specs = [pl.BlockSpec((tm, tn), lambda i, j: (i, j))]
        out_shape = [jax.ShapeDtypeStruct((M, N), out_dtype)]
    else:
        for arr, kind in tail["ins"]:
            in_specs.append(pl.BlockSpec(kinds[kind][0], kinds[kind][2]))
            args.append(arr)
        out_specs = [pl.BlockSpec(kinds[kind][0], kinds[kind][2]) for _, kind in tail["outs"]]
        out_shape = [jax.ShapeDtypeStruct(kinds[kind][1], dt) for dt, kind in tail["outs"]]
    scratch = []
    if n_r:
        in_specs += [ANY] * n_r
        args += ride["arrays"]
        out_specs += [ANY] * n_r
        out_shape += ride["out_shape"]
        scratch = ride["scratch"]
    sequential = tail is not None or n_r > 0
    res = pl.pallas_call(
        body, name=name, grid=(M // tm, N // tn), in_specs=in_specs, out_specs=out_specs, out_shape=out_shape,
        scratch_shapes=scratch,
        compiler_params=pltpu.CompilerParams(dimension_semantics=("arbitrary" if sequential else "parallel", "arbitrary"),
                                             vmem_limit_bytes=VMEM_LIMIT, has_side_effects=n_r > 0),
    )(*args)
    return res[0] if len(res) == 1 else res


def _mm_tn(a, b, *, name, tma_cap=1024, tnb_cap=1024, tk=1024):
    T, Ma = a.shape
    Nb = b.shape[1]
    tma, tnb = _pick(Ma, tma_cap), _pick(Nb, tnb_cap)
    tk = min(tk, T)
    n_k = T // tk

    def body(a_ref, b_ref, o_ref, acc_ref):
        k = pl.program_id(2)

        @pl.when(k == 0)
        def _():
            acc_ref[...] = jnp.zeros_like(acc_ref)

        acc_ref[...] += lax.dot_general(a_ref[...], b_ref[...], (((0,), (0,)), ((), ())),
                                        preferred_element_type=F32)

        @pl.when(k == n_k - 1)
        def _():
            o_ref[...] = acc_ref[...]

    return pl.pallas_call(
        body, name=name, grid=(Ma // tma, Nb // tnb, n_k),
        in_specs=[pl.BlockSpec((tk, tma), lambda i, j, k: (k, i)), pl.BlockSpec((tk, tnb), lambda i, j, k: (k, j))],
        out_specs=pl.BlockSpec((tma, tnb), lambda i, j, k: (i, j)),
        out_shape=jax.ShapeDtypeStruct((Ma, Nb), F32),
        scratch_shapes=[pltpu.VMEM((tma, tnb), F32)],
        compiler_params=_params(("parallel", "parallel", "arbitrary")),
    )(a, b)


def _rms_fwd(x, w, *, name, tm=512, ride=None):
    T, Dm = x.shape
    n_r = 0 if ride is None else len(ride["arrays"])

    def body(x_ref, w_ref, *rest):
        h_ref, r_ref = rest[n_r:n_r + 2]
        if n_r:
            start, finish = ride["halves"](rest[:n_r], rest[n_r + 2:2 * n_r + 2], *rest[2 * n_r + 2:])
            pl.when(pl.program_id(0) == 0)(start)
        xv = x_ref[...]
        r = lax.rsqrt(jnp.mean(xv * xv, axis=-1, keepdims=True) + EPS)
        h_ref[...] = (xv * r * w_ref[...]).astype(BF16)
        r_ref[...] = r
        if n_r:
            pl.when(pl.program_id(0) == T // tm - 1)(finish)

    return pl.pallas_call(
        body, name=name, grid=(T // tm,),
        in_specs=[pl.BlockSpec((tm, Dm), lambda i: (i, 0)), pl.BlockSpec((1, Dm), lambda i: (0, 0))] + [ANY] * n_r,
        out_specs=[pl.BlockSpec((tm, Dm), lambda i: (i, 0)), pl.BlockSpec((tm, 1), lambda i: (i, 0))] + [ANY] * n_r,
        out_shape=[jax.ShapeDtypeStruct((T, Dm), BF16), jax.ShapeDtypeStruct((T, 1), F32)]
                  + (ride["out_shape"] if n_r else []),
        scratch_shapes=ride["scratch"] if n_r else [],
        compiler_params=pltpu.CompilerParams(dimension_semantics=("arbitrary" if n_r else "parallel",),
                                             vmem_limit_bytes=VMEM_LIMIT, has_side_effects=n_r > 0),
    )(x, w, *(ride["arrays"] if n_r else []))


def _tail_rms_fwd(w):
    def fn(xv, first, w_ref, x_ref, h_ref, r_ref):
        r = lax.rsqrt(jnp.mean(xv * xv, axis=-1, keepdims=True) + EPS)
        x_ref[...] = xv
        h_ref[...] = (xv * r * w_ref[...]).astype(BF16)
        r_ref[...] = r

    return dict(fn=fn, ins=[(w, "vec")], outs=[(F32, "row"), (BF16, "row"), (F32, "col")])


def _tail_rms_bwd(x, r, w, dres, *, emit_bf16):
    def fn(dhv, first, x_ref, r_ref, w_ref, dres_ref, *outs):
        dx_ref, dw_ref = outs[0], outs[-1]

        @pl.when(first)
        def _():
            dw_ref[...] = jnp.zeros_like(dw_ref)

        rv = r_ref[...]
        xh = x_ref[...] * rv
        dxh = dhv * w_ref[...]
        t = jnp.mean(dxh * xh, axis=-1, keepdims=True)
        dx = dres_ref[...] + rv * (dxh - xh * t)
        dx_ref[...] = dx
        if emit_bf16:
            outs[1][...] = dx.astype(BF16)
        dw_ref[...] += jnp.sum(dhv * xh, axis=0, keepdims=True)

    outs = [(F32, "row")] + ([(BF16, "row")] if emit_bf16 else []) + [(F32, "vec")]
    return dict(fn=fn, ins=[(x, "row"), (r, "col"), (w, "vec"), (dres, "row")], outs=outs)


def _tail_loss(target, w):
    def fn(xv, first, t_ref, w_ref, loss_ref, dx_ref, dxb_ref, dw_ref):
        @pl.when(first)
        def _():
            loss_ref[...] = jnp.zeros_like(loss_ref)
            dw_ref[...] = jnp.zeros_like(dw_ref)

        r = lax.rsqrt(jnp.mean(xv * xv, axis=-1, keepdims=True) + EPS)
        xh = xv * r
        wv = w_ref[...]
        err = xh * wv - t_ref[...]
        row_loss = jnp.mean(err * err, axis=-1, keepdims=True)
        loss_ref[...] += 0.5 * jnp.sum(row_loss, axis=0, keepdims=True)
        dy = err * (1.0 / xv.shape[-1])
        dxh = dy * wv
        t = jnp.mean(dxh * xh, axis=-1, keepdims=True)
        dx = r * (dxh - xh * t)
        dx_ref[...] = dx
        dxb_ref[...] = dx.astype(BF16)
        dw_ref[...] += jnp.sum(dy * xh, axis=0, keepdims=True)

    return dict(fn=fn, ins=[(target, "row"), (w, "vec")],
                outs=[(F32, "one"), (F32, "row"), (BF16, "row"), (F32, "vec")])


GLA_TB = 512
GLA_NC = GLA_TB // CHUNK
GLA_UNROLL = 4


def _cumsum_rows(x, row, reverse):
    n = x.shape[0]
    s = 1
    while s < n:
        if not reverse:
            x = x + jnp.where(row >= s, pltpu.roll(x, s, 0), 0.0)
        else:
            x = x + jnp.where(row < n - s, pltpu.roll(x, n - s, 0), 0.0)
        s *= 2
    return x


def _gla_gates(uq, z, lbv):
    q = uq * _sigmoid(uq)
    sg = _sigmoid(z)
    sgn = _sigmoid(-z)
    f = lbv + (1.0 - lbv) * sg
    k = (1.0 - lbv) * sgn
    return q, sg, sgn, f, k


def _gla_decays(f, row, reverse):
    b = _cumsum_rows(jnp.log(f), row, reverse)
    if not reverse:
        bref, blast = b[CHUNK // 2 - 1:CHUNK // 2, :], b[CHUNK - 1:CHUNK, :]
    else:
        bref, blast = b[CHUNK // 2:CHUNK // 2 + 1, :], b[0:1, :]
    return b, bref, blast


def _gla_fwd(U, lb, *, f_block, reverse, name, ride=None):
    T = U.shape[0]
    nb = T // GLA_TB
    n_g = 0 if ride is None else len(ride["arrays"])

    def body(uq_ref, uf_ref, ui_ref, lb_ref, *rest):
        g_in, rest = rest[:n_g], rest[n_g:]
        o_ref, st_ref = rest[:2]
        g_out, rest = rest[2:2 + n_g], rest[2 + n_g:]
        s_ref = rest[0]
        if n_g:
            start, finish = ride["halves"](g_in, g_out, *rest[1:])
            pl.when(pl.program_id(0) == 0)(start)

        @pl.when(pl.program_id(0) == 0)
        def _():
            s_ref[...] = jnp.zeros_like(s_ref)

        row = lax.broadcasted_iota(jnp.int32, (CHUNK, HG_D), 0)
        ri = lax.broadcasted_iota(jnp.int32, (CHUNK, CHUNK), 0)
        ci = lax.broadcasted_iota(jnp.int32, (CHUNK, CHUNK), 1)
        mask = (ri <= ci) if reverse else (ri >= ci)

        def chunk(j, carry):
            c = (GLA_NC - 1 - j) if reverse else j
            rows = pl.ds(pl.multiple_of(c * CHUNK, CHUNK), CHUNK)
            for h in range(HG_HEADS):
                cols = pl.ds(h * HG_D, HG_D)
                v = ui_ref[rows, cols]
                q, _, _, f, k = _gla_gates(uq_ref[rows, cols], uf_ref[rows, cols], lb_ref[:, cols])
                b, bref, blast = _gla_decays(f, row, reverse)
                s = jnp.where(mask, _dot_nt(q * jnp.exp(b - bref), k * jnp.exp(bref - b)), 0.0)
                st = s_ref[h]
                st_ref[c, h] = st
                o_ref[rows, cols] = _dot(s, v) + _dot_nt(q * jnp.exp(b), st)
                s_ref[h] = st * jnp.exp(blast) + _dot_tn(v, k * jnp.exp(blast - b))
            return carry

        lax.fori_loop(0, GLA_NC, chunk, 0, unroll=GLA_NC)
        if n_g:
            pl.when(pl.program_id(0) == nb - 1)(finish)

    blk = (lambda i: nb - 1 - i) if reverse else (lambda i: i)
    ucol = lambda cb: pl.BlockSpec((GLA_TB, HG_W), lambda i: (blk(i), cb))
    return pl.pallas_call(
        body, name=name, grid=(nb,),
        in_specs=[ucol(0), ucol(f_block), ucol(3), pl.BlockSpec((1, HG_W), lambda i: (0, 0))] + [ANY] * n_g,
        out_specs=[pl.BlockSpec((GLA_TB, HG_W), lambda i: (blk(i), 0)),
                   pl.BlockSpec((GLA_NC, HG_HEADS, HG_D, HG_D), lambda i: (blk(i), 0, 0, 0))] + [ANY] * n_g,
        out_shape=[jax.ShapeDtypeStruct((T, HG_W), F32),
                   jax.ShapeDtypeStruct((T // CHUNK, HG_HEADS, HG_D, HG_D), F32)] + (ride["out_shape"] if n_g else []),
        scratch_shapes=[pltpu.VMEM((HG_HEADS, HG_D, HG_D), F32)] + (ride["scratch"] if n_g else []),
        compiler_params=pltpu.CompilerParams(dimension_semantics=("arbitrary",), vmem_limit_bytes=VMEM_LIMIT,
                                             has_side_effects=bool(n_g)),
    )(U, U, U, lb, *(ride["arrays"] if n_g else []))


def _gla_bwd(U, lb, do, states, *, f_block, reverse, name, prev=None):
    T = U.shape[0]
    nb = T // GLA_TB
    final = prev is not None

    def body(uq_ref, uf_ref, ui_ref, lb_ref, do_ref, st_ref, *rest):
        if final:
            dqp_ref, dzp_ref, dvp_ref, dug_ref, out_ref, dlb_ref, ds_ref = rest
        else:
            dq_ref, dz_ref, dv_ref, dlb_ref, ds_ref = rest

        @pl.when(pl.program_id(0) == 0)
        def _():
            ds_ref[...] = jnp.zeros_like(ds_ref)
            dlb_ref[...] = jnp.zeros_like(dlb_ref)

        row = lax.broadcasted_iota(jnp.int32, (CHUNK, HG_D), 0)
        ri = lax.broadcasted_iota(jnp.int32, (CHUNK, CHUNK), 0)
        ci = lax.broadcasted_iota(jnp.int32, (CHUNK, CHUNK), 1)
        mask = (ri <= ci) if reverse else (ri >= ci)

        def chunk(j, carry):
            c = j if reverse else (GLA_NC - 1 - j)
            rows = pl.ds(pl.multiple_of(c * CHUNK, CHUNK), CHUNK)
            for h in range(HG_HEADS):
                cols = pl.ds(h * HG_D, HG_D)
                v = ui_ref[rows, cols]
                lbv = lb_ref[:, cols]
                uq = uq_ref[rows, cols]
                q, sg, sgn, f, k = _gla_gates(uq, uf_ref[rows, cols], lbv)
                b, bref, blast = _gla_decays(f, row, reverse)
                eq, ek, eb, el, dec = (jnp.exp(b - bref), jnp.exp(bref - b), jnp.exp(b), jnp.exp(blast - b),
                                       jnp.exp(blast))
                qin, kin, qb, klast = q * eq, k * ek, q * eb, k * el
                dov = do_ref[rows, cols]
                st = st_ref[c, h]
                dst = ds_ref[h]
                p = jnp.where(mask, _dot_nt(qin, kin), 0.0)
                dp = jnp.where(mask, _dot_nt(dov, v), 0.0)
                dqin = _dot(dp, kin)
                dkin = _dot_tn(dp, qin)
                dv = _dot_tn(p, dov) + _dot_nt(klast, dst)
                dqb = _dot(dov, st)
                dklast = _dot(v, dst)
                ds_ref[h] = _dot_tn(dov, qb) + dst * dec
                db = dqin * qin - dkin * kin + dqb * qb - dklast * klast
                extra = (jnp.sum(dklast * klast, axis=0, keepdims=True)
                         + dec * jnp.sum(st * dst, axis=0, keepdims=True))
                dg = _cumsum_rows(db, row, not reverse) + extra
                dq = dqin * eq + dqb * eb
                dk = dkin * ek + dklast * el
                dfk = dg / f - dk
                dz = (dfk * (1.0 - lbv) * sg * sgn).astype(BF16)
                dlb_ref[:, cols] += jnp.sum(dfk * sgn, axis=0, keepdims=True)
                if final:
                    sq = _sigmoid(uq)
                    col = lambda blk: pl.ds(blk * HG_W + h * HG_D, HG_D)
                    out_ref[rows, col(0)] = ((dq + dqp_ref[rows, cols]) * (sq * (1.0 + uq * (1.0 - sq)))).astype(BF16)
                    out_ref[rows, col(1)] = dzp_ref[rows, cols]
                    out_ref[rows, col(2)] = dz
                    out_ref[rows, col(3)] = (dv + dvp_ref[rows, cols]).astype(BF16)
                    out_ref[rows, col(4)] = dug_ref[rows, cols]
                else:
                    dq_ref[rows, cols] = dq
                    dz_ref[rows, cols] = dz
                    dv_ref[rows, cols] = dv
            return carry

        lax.fori_loop(0, GLA_NC, chunk, 0, unroll=GLA_UNROLL)

    blk = (lambda i: i) if reverse else (lambda i: nb - 1 - i)
    ucol = lambda cb: pl.BlockSpec((GLA_TB, HG_W), lambda i: (blk(i), cb))
    tok = pl.BlockSpec((GLA_TB, HG_W), lambda i: (blk(i), 0))
    vec = pl.BlockSpec((1, HG_W), lambda i: (0, 0))
    in_specs = [ucol(0), ucol(f_block), ucol(3), vec, tok,
                pl.BlockSpec((GLA_NC, HG_HEADS, HG_D, HG_D), lambda i: (blk(i), 0, 0, 0))]
    vec_shape = jax.ShapeDtypeStruct((1, HG_W), F32)
    if final:
        in_specs += [tok] * 4
        out_specs = [pl.BlockSpec((GLA_TB, 5 * HG_W), lambda i: (blk(i), 0)), vec]
        out_shape = [jax.ShapeDtypeStruct((T, 5 * HG_W), BF16), vec_shape]
    else:
        out_specs = [tok, tok, tok, vec]
        out_shape = [jax.ShapeDtypeStruct((T, HG_W), F32), jax.ShapeDtypeStruct((T, HG_W), BF16),
                     jax.ShapeDtypeStruct((T, HG_W), F32), vec_shape]
    return pl.pallas_call(
        body, name=name, grid=(nb,), in_specs=in_specs, out_specs=out_specs, out_shape=out_shape,
        scratch_shapes=[pltpu.VMEM((HG_HEADS, HG_D, HG_D), F32)],
        compiler_params=_params(("arbitrary",)),
    )(U, U, U, lb, do, states, *(prev if final else ()))


def _hg_post_fwd(o_f, o_b, U, w, *, name, tm=512):
    T = o_f.shape[0]

    def body(of_ref, ob_ref, ug_ref, w_ref, out_ref):
        wv = w_ref[...]
        for h in range(HG_HEADS):
            cols = pl.ds(h * HG_D, HG_D)
            o = of_ref[:, cols] + ob_ref[:, cols]
            r = lax.rsqrt(jnp.mean(o * o, axis=-1, keepdims=True) + EPS)
            ug = ug_ref[:, cols]
            out_ref[:, cols] = (o * r * wv * (ug * _sigmoid(ug))).astype(BF16)

    tok = pl.BlockSpec((tm, HG_W), lambda i: (i, 0))
    return pl.pallas_call(
        body, name=name, grid=(T // tm,),
        in_specs=[tok, tok, pl.BlockSpec((tm, HG_W), lambda i: (i, 4)), pl.BlockSpec((1, HG_D), lambda i: (0, 0))],
        out_specs=tok, out_shape=jax.ShapeDtypeStruct((T, HG_W), BF16),
        compiler_params=_params(("parallel",)),
    )(o_f, o_b, U, w)


def _hg_post_bwd(dmix, o_f, o_b, U, w, *, name, tm=512):
    T = o_f.shape[0]

    def body(dm_ref, of_ref, ob_ref, ug_ref, w_ref, do_ref, dug_ref, dw_ref):
        @pl.when(pl.program_id(0) == 0)
        def _():
            dw_ref[...] = jnp.zeros_like(dw_ref)

        wv = w_ref[...]
        for h in range(HG_HEADS):
            cols = pl.ds(h * HG_D, HG_D)
            o = of_ref[:, cols] + ob_ref[:, cols]
            r = lax.rsqrt(jnp.mean(o * o, axis=-1, keepdims=True) + EPS)
            xh = o * r
            ug = ug_ref[:, cols]
            sg = _sigmoid(ug)
            dm = dm_ref[:, cols]
            dn = dm * (ug * sg)
            dug_ref[:, cols] = (dm * (xh * wv) * (sg * (1.0 + ug * (1.0 - sg)))).astype(BF16)
            dxh = dn * wv
            t = jnp.mean(dxh * xh, axis=-1, keepdims=True)
            do_ref[:, cols] = r * (dxh - xh * t)
            dw_ref[:, cols] += jnp.sum(dn * xh, axis=0, keepdims=True)

    tok = pl.BlockSpec((tm, HG_W), lambda i: (i, 0))
    vec = pl.BlockSpec((1, HG_W), lambda i: (0, 0))
    return pl.pallas_call(
        body, name=name, grid=(T // tm,),
        in_specs=[tok, tok, tok, pl.BlockSpec((tm, HG_W), lambda i: (i, 4)), pl.BlockSpec((1, HG_D), lambda i: (0, 0))],
        out_specs=[tok, tok, vec],
        out_shape=[jax.ShapeDtypeStruct((T, HG_W), F32), jax.ShapeDtypeStruct((T, HG_W), BF16),
                   jax.ShapeDtypeStruct((1, HG_W), F32)],
        compiler_params=_params(("arbitrary",)),
    )(dmix, o_f, o_b, U, w)


def _rope_tables(T):
    rows = T // GRID_W
    row = np.repeat(np.arange(rows), GRID_W).astype(np.float32)
    col = np.tile(np.arange(GRID_W), rows).astype(np.float32)
    axis_dim = ATT_DH // 2
    freqs = (np.float32(ROPE_THETA) ** (-np.arange(0, axis_dim, 2, dtype=np.float32) / np.float32(axis_dim))
             ).astype(np.float32)
    ang = np.concatenate([row[:, None] * freqs, col[:, None] * freqs], axis=-1).astype(np.float32)
    cos, sin = np.cos(ang), np.sin(ang)
    c = np.repeat(cos, 2, axis=-1)
    s = np.stack([-sin, sin], axis=-1).reshape(T, ATT_DH)
    return jnp.asarray(np.tile(c, (1, 2)), F32), jnp.asarray(np.tile(s, (1, 2)), F32)


def _head_blockdiag(width):
    shift = ATT_DH.bit_length() - 1
    ri = jnp.right_shift(lax.broadcasted_iota(jnp.int32, (width, width), 0), shift)
    ci = jnp.right_shift(lax.broadcasted_iota(jnp.int32, (width, width), 1), shift)
    return jnp.where(ri == ci, 1.0, 0.0).astype(BF16)


def _head_sum(x, bd):
    hi = x.astype(BF16)
    lo = (x - hi.astype(F32)).astype(BF16)
    return jnp.dot(hi, bd, preferred_element_type=F32) + jnp.dot(lo, bd, preferred_element_type=F32)


def _pair_swap(x, even):
    n = x.shape[-1]
    return jnp.where(even, pltpu.roll(x, n - 1, 1), pltpu.roll(x, 1, 1))


FA_TQ = 512


FA_TK = 512


def _cols_from_tokens(x, kv):
    w = ATT_G * ATT_DH
    xt = x[:, kv * w:(kv + 1) * w].T
    return jnp.concatenate([xt[g * ATT_DH:(g + 1) * ATT_DH, :] for g in range(ATT_G)], axis=1)


def _tokens_from_cols(c):
    tq = c.shape[1] // ATT_G
    return jnp.concatenate([c[:, g * tq:(g + 1) * tq] for g in range(ATT_G)], axis=0).T


def _store_cols(ref, x, norm_ref=None):
    for kv in range(ATT_KV):
        cols = _cols_from_tokens(x, kv).astype(BF16)
        ref[kv, 0] = cols
        if norm_ref is not None:
            cf = cols.astype(F32)
            norm_ref[kv, 0] = jnp.sqrt(jnp.sum(cf * cf, axis=0, keepdims=True))


def _att_prep_fwd(U, cos, sin, qw, kw, *, name):
    T = U.shape[0]
    tm = min(FA_TQ, T)
    R = ATT_G * tm
    scale = ATT_DH ** -0.5

    def head_rows(ref, x):
        xt = x.astype(F32).T
        for kv in range(ATT_KV):
            ref[kv, 0] = xt[kv * ATT_DH:(kv + 1) * ATT_DH, :].astype(BF16)

    def body(aq_ref, ak_ref, av_ref, c_ref, s_ref, qw_ref, kw_ref, q_ref, qn_ref, kmax_ref, kc_ref, vc_ref):
        @pl.when(pl.program_id(0) == 0)
        def _():
            kmax_ref[...] = jnp.zeros_like(kmax_ref)

        bd = _head_blockdiag(ATT_QW)
        c2, s2 = c_ref[...], s_ref[...]
        c8, s8 = jnp.tile(c2, (1, 4)), jnp.tile(s2, (1, 4))

        def norm_rope(x, w, c, s, bdm):
            r = lax.rsqrt(_head_sum(x * x, bdm) * (1.0 / ATT_DH) + EPS)
            y = x * r * w
            even = (lax.broadcasted_iota(jnp.int32, y.shape, 1) & 1) == 0
            return y * c + _pair_swap(y, even) * s

        _store_cols(q_ref, norm_rope(aq_ref[...], qw_ref[...], c8, s8, bd) * (scale * LOG2E), qn_ref)
        kb = norm_rope(ak_ref[...], kw_ref[...], c2, s2, bd[:ATT_KW, :ATT_KW]).astype(BF16)
        kf = kb.astype(F32)
        ksq = _head_sum(kf * kf, bd[:ATT_KW, :ATT_KW])
        kmax_ref[...] = jnp.maximum(kmax_ref[...], jnp.max(ksq, axis=0, keepdims=True))
        head_rows(kc_ref, kb)
        head_rows(vc_ref, av_ref[...].astype(BF16))

    kv_spec = pl.BlockSpec((tm, ATT_KW), lambda i: (i, 0))
    tk = min(FA_TK, T)
    per = tk // tm
    c_spec = pl.BlockSpec((ATT_KV, 1, ATT_DH, tm), lambda i: (0, i // per, 0, i % per))
    c_shape = jax.ShapeDtypeStruct((ATT_KV, T // tk, ATT_DH, tk), BF16)
    return pl.pallas_call(
        body, name=name, grid=(T // tm,),
        in_specs=[pl.BlockSpec((tm, ATT_QW), lambda i: (i, 5)),
                  pl.BlockSpec((tm, ATT_KW), lambda i: (i, 24)), pl.BlockSpec((tm, ATT_KW), lambda i: (i, 25)),
                  kv_spec, kv_spec,
                  pl.BlockSpec((1, ATT_QW), lambda i: (0, 0)), pl.BlockSpec((1, ATT_KW), lambda i: (0, 0))],
        out_specs=[pl.BlockSpec((ATT_KV, 1, ATT_DH, R), lambda i: (0, i, 0, 0)),
                   pl.BlockSpec((ATT_KV, 1, 1, R), lambda i: (0, i, 0, 0)), pl.BlockSpec((1, ATT_KW), lambda i: (0, 0)),
                   c_spec, c_spec],
        out_shape=[jax.ShapeDtypeStruct((ATT_KV, T // tm, ATT_DH, R), BF16),
                   jax.ShapeDtypeStruct((ATT_KV, T // tm, 1, R), F32), jax.ShapeDtypeStruct((1, ATT_KW), F32),
                   c_shape, c_shape],
        compiler_params=_params(("arbitrary",)),
    )(U, U, U, cos, sin, qw, kw)


def _att_prep_bwd(U, dq_c, dk_c, dv_c, cos, sin, qw, kw, *, name):
    T = U.shape[0]
    tm = min(FA_TQ, T)
    R = ATT_G * tm
    scale = ATT_DH ** -0.5

    def body(aq_ref, ak_ref, dq_ref, dk_ref, dv_ref, c_ref, s_ref, qw_ref, kw_ref, out_ref, dqw_ref, dkw_ref):
        @pl.when(pl.program_id(0) == 0)
        def _():
            dqw_ref[...] = jnp.zeros_like(dqw_ref)
            dkw_ref[...] = jnp.zeros_like(dkw_ref)

        bd = _head_blockdiag(ATT_QW)
        c2, s2 = c_ref[...], s_ref[...]
        c8, s8 = jnp.tile(c2, (1, 4)), jnp.tile(s2, (1, 4))

        def bwd(x, dy, w, c, s, bdm):
            even = (lax.broadcasted_iota(jnp.int32, x.shape, 1) & 1) == 0
            dn = dy * c - _pair_swap(dy, even) * s
            r = lax.rsqrt(_head_sum(x * x, bdm) * (1.0 / ATT_DH) + EPS)
            xh = x * r
            dxh = dn * w
            t = _head_sum(dxh * xh, bdm) * (1.0 / ATT_DH)
            return r * (dxh - xh * t), jnp.sum(dn * xh, axis=0, keepdims=True)

        dq = jnp.concatenate([_tokens_from_cols(dq_ref[kv, 0]) for kv in range(ATT_KV)], axis=1)
        da, dw = bwd(aq_ref[...], dq * scale, qw_ref[...], c8, s8, bd)
        out_ref[:, 0:ATT_QW] = da.astype(BF16)
        dqw_ref[...] += dw
        tokens = lambda ref: jnp.concatenate([ref[kv, 0] for kv in range(ATT_KV)], axis=0).T
        da, dw = bwd(ak_ref[...], tokens(dk_ref) * (1.0 / LOG2E), kw_ref[...], c2, s2, bd[:ATT_KW, :ATT_KW])
        out_ref[:, ATT_QW:ATT_QW + ATT_KW] = da.astype(BF16)
        dkw_ref[...] += dw
        out_ref[:, ATT_QW + ATT_KW:ATT_QW + 2 * ATT_KW] = tokens(dv_ref).astype(BF16)

    kv_spec = pl.BlockSpec((tm, ATT_KW), lambda i: (i, 0))
    qv = pl.BlockSpec((1, ATT_QW), lambda i: (0, 0))
    kv = pl.BlockSpec((1, ATT_KW), lambda i: (0, 0))
    w_att = ATT_QW + 2 * ATT_KW
    per = dk_c.shape[3] // tm
    c_spec = pl.BlockSpec((ATT_KV, 1, ATT_DH, tm), lambda i: (0, i // per, 0, i % per))
    return pl.pallas_call(
        body, name=name, grid=(T // tm,),
        in_specs=[pl.BlockSpec((tm, ATT_QW), lambda i: (i, 5)), pl.BlockSpec((tm, ATT_KW), lambda i: (i, 24)),
                  pl.BlockSpec((ATT_KV, 1, ATT_DH, R), lambda i: (0, i, 0, 0)), c_spec, c_spec, kv_spec, kv_spec, qv, kv],
        out_specs=[pl.BlockSpec((tm, w_att), lambda i: (i, 0)), qv, kv],
        out_shape=[jax.ShapeDtypeStruct((T, w_att), BF16),
                   jax.ShapeDtypeStruct((1, ATT_QW), F32), jax.ShapeDtypeStruct((1, ATT_KW), F32)],
        compiler_params=_params(("arbitrary",)),
    )(U, U, dq_c, dk_c, dv_c, cos, sin, qw, kw)


def _scores(k_ref, j, qv):
    return lax.dot_general(k_ref[0, j], qv, (((0,), (0,)), ((), ())), preferred_element_type=F32)


def _flash_fwd(q_c, k_c, v_c, *, name):
    _, nq, _, R = q_c.shape
    _, n_k, _, tk = v_c.shape

    def body(q_ref, k_ref, v_ref, o_ref, lse_ref, acc_ref):
        qv = q_ref[0, 0]
        acc_ref[...] = jnp.zeros_like(acc_ref)

        def step(j, carry):
            m, l = carry
            s = _scores(k_ref, j, qv)
            m_new = jnp.maximum(m, jnp.max(s, axis=0, keepdims=True))
            alpha = jnp.exp2(m - m_new)
            p = jnp.exp2(s - m_new)
            l = alpha * l + jnp.sum(p, axis=0, keepdims=True)
            acc_ref[...] = alpha * acc_ref[...] + jnp.dot(v_ref[0, j], p.astype(BF16), preferred_element_type=F32)
            return m_new, l

        m, l = lax.fori_loop(0, n_k, step, (jnp.full((1, R), -jnp.inf, F32), jnp.zeros((1, R), F32)))
        o_ref[0, 0] = acc_ref[...] / l
        lse_ref[0, 0] = m + jnp.log2(l)

    cspec = pl.BlockSpec((1, 1, ATT_DH, R), lambda h, i: (h, i, 0, 0))
    kspec = pl.BlockSpec((1, n_k, ATT_DH, tk), lambda h, i: (h, 0, 0, 0))
    return pl.pallas_call(
        body, name=name, grid=(ATT_KV, nq),
        in_specs=[cspec, kspec, kspec],
        out_specs=[cspec, pl.BlockSpec((1, 1, 1, R), lambda h, i: (h, i, 0, 0))],
        out_shape=[jax.ShapeDtypeStruct((ATT_KV, nq, ATT_DH, R), F32), jax.ShapeDtypeStruct((ATT_KV, nq, 1, R), F32)],
        scratch_shapes=[pltpu.VMEM((ATT_DH, R), F32)],
        compiler_params=_params(("parallel", "parallel")),
    )(q_c, k_c, v_c)


FA_BOUND_MAX = 40.0 * LOG2E


def _flash_fwd_bounded(q_c, k_c, v_c, m_c, *, name):
    _, nq, _, R = q_c.shape
    _, n_k, _, tk = v_c.shape

    def body(q_ref, k_ref, v_ref, m_ref, o_ref, lse_ref, acc_ref):
        qv = q_ref[0, 0]
        m = m_ref[0, 0]
        acc_ref[...] = jnp.zeros_like(acc_ref)

        per = math.gcd(n_k, 4)

        def step(jj, l8):
            pv = None
            for u in range(per):
                j = per * jj + u
                p = jnp.exp2(_scores(k_ref, j, qv) - m)
                l8 = l8 + jnp.sum(p.reshape(tk // 8, 8, R), axis=0)
                d = jnp.dot(v_ref[0, j], p.astype(BF16), preferred_element_type=F32)
                pv = d if pv is None else pv + d
            acc_ref[...] += pv
            return l8

        l8 = lax.fori_loop(0, n_k // per, step, jnp.zeros((8, R), F32))
        l = jnp.sum(l8, axis=0, keepdims=True)
        o_ref[0, 0] = acc_ref[...] / l
        lse_ref[0, 0] = m + jnp.log2(l)

    cspec = pl.BlockSpec((1, 1, ATT_DH, R), lambda h, i: (h, i, 0, 0))
    kspec = pl.BlockSpec((1, n_k, ATT_DH, tk), lambda h, i: (h, 0, 0, 0))
    vspec = pl.BlockSpec((1, 1, 1, R), lambda h, i: (h, i, 0, 0))
    return pl.pallas_call(
        body, name=name, grid=(ATT_KV, nq),
        in_specs=[cspec, kspec, kspec, vspec],
        out_specs=[cspec, vspec],
        out_shape=[jax.ShapeDtypeStruct((ATT_KV, nq, ATT_DH, R), F32), jax.ShapeDtypeStruct((ATT_KV, nq, 1, R), F32)],
        scratch_shapes=[pltpu.VMEM((ATT_DH, R), F32)],
        compiler_params=_params(("parallel", "parallel")),
    )(q_c, k_c, v_c, m_c)


CHIP_MASKS = [(1, 0, 0), (0, 1, 0), (1, 1, 0)]


def _chip_slot(p):
    return 2 * p[0] + p[1]


def _flash_bwd(q_c, k_c, v_c, do_c, lse, delta, *, name, ride=None):
    _, nq, _, R = q_c.shape
    _, n_k, _, tk = k_c.shape
    n_ride = 0 if ride is None else len(ride["arrays"])

    def body(qc_ref, kc_ref, vc_ref, doc_ref, lse_ref, delta_ref, *rest):
        ride_in, rest = rest[:n_ride], rest[n_ride:]
        dq_ref, dk_ref, dv_ref = rest[:3]
        ride_out, rest = rest[3:3 + n_ride], rest[3 + n_ride:]
        acc_ref = rest[0]
        kv = pl.program_id(0)
        if n_ride:
            start, finish = ride["halves"](ride_in, ride_out, *rest[1:])
            pl.when((kv == 0) & (pl.program_id(1) == 0))(start)

        @pl.when(pl.program_id(1) == 0)
        def _():
            dk_ref[...] = jnp.zeros_like(dk_ref)
            dv_ref[...] = jnp.zeros_like(dv_ref)

        qc, doc = qc_ref[0, 0], doc_ref[0, 0]
        lsev, delta = lse_ref[0, 0], delta_ref[0, 0]
        acc_ref[...] = jnp.zeros_like(acc_ref)
        nt = (((1,), (1,)), ((), ()))

        def step(j, carry):
            p = jnp.exp2(_scores(kc_ref, j, qc) - lsev)
            dp = _scores(vc_ref, j, doc)
            ds = (p * (dp - delta)).astype(BF16)
            acc_ref[...] += jnp.dot(kc_ref[0, j], ds, preferred_element_type=F32)
            dk_ref[0, j] += lax.dot_general(qc, ds, nt, preferred_element_type=F32)
            dv_ref[0, j] += lax.dot_general(doc, p.astype(BF16), nt, preferred_element_type=F32)
            return carry

        lax.fori_loop(0, n_k, step, 0, unroll=2)
        dq_ref[0, 0] = acc_ref[...]

        if n_ride:
            pl.when((kv == ATT_KV - 1) & (pl.program_id(1) == nq - 1))(finish)

    cspec = pl.BlockSpec((1, 1, ATT_DH, R), lambda h, i: (h, i, 0, 0))
    vspec = pl.BlockSpec((1, 1, 1, R), lambda h, i: (h, i, 0, 0))
    kspec = pl.BlockSpec((1, n_k, ATT_DH, tk), lambda h, i: (h, 0, 0, 0))
    k_shape = jax.ShapeDtypeStruct(k_c.shape, F32)
    return pl.pallas_call(
        body, name=name, grid=(ATT_KV, nq),
        in_specs=[cspec, kspec, kspec, cspec, vspec, vspec] + [ANY] * n_ride,
        out_specs=[cspec, kspec, kspec] + [ANY] * n_ride,
        out_shape=[jax.ShapeDtypeStruct((ATT_KV, nq, ATT_DH, R), F32), k_shape, k_shape]
                  + (ride["out_shape"] if n_ride else []),
        scratch_shapes=[pltpu.VMEM((ATT_DH, R), F32)] + (ride["scratch"] if n_ride else []),
        compiler_params=pltpu.CompilerParams(dimension_semantics=("arbitrary", "arbitrary"),
                                             vmem_limit_bytes=VMEM_LIMIT, has_side_effects=bool(n_ride)),
    )(q_c, k_c, v_c, do_c, lse, delta, *(ride["arrays"] if n_ride else []))


def _att_post_fwd(o_c, w, *, name):
    _, nq, _, R = o_c.shape
    tm = R // ATT_G
    T = nq * tm

    def body(oc_ref, w_ref, o_ref, out_ref):
        ov = jnp.concatenate([_tokens_from_cols(oc_ref[kv, 0]) for kv in range(ATT_KV)], axis=1)
        r = lax.rsqrt(jnp.mean(ov * ov, axis=-1, keepdims=True) + EPS)
        o_ref[...] = ov
        out_ref[...] = (ov * r * w_ref[...]).astype(BF16)

    tok = pl.BlockSpec((tm, ATT_QW), lambda i: (i, 0))
    return pl.pallas_call(
        body, name=name, grid=(nq,),
        in_specs=[pl.BlockSpec((ATT_KV, 1, ATT_DH, R), lambda i: (0, i, 0, 0)), pl.BlockSpec((1, ATT_QW), lambda i: (0, 0))],
        out_specs=[tok, tok],
        out_shape=[jax.ShapeDtypeStruct((T, ATT_QW), F32), jax.ShapeDtypeStruct((T, ATT_QW), BF16)],
        compiler_params=_params(("parallel",)),
    )(o_c, w)


def _att_post_bwd(dmix, o, w, *, name):
    T = o.shape[0]
    tm = min(FA_TQ, T)
    R = ATT_G * tm

    def body(dm_ref, o_ref, w_ref, do_ref, delta_ref, dw_ref):
        @pl.when(pl.program_id(0) == 0)
        def _():
            dw_ref[...] = jnp.zeros_like(dw_ref)

        ov = o_ref[...]
        r = lax.rsqrt(jnp.mean(ov * ov, axis=-1, keepdims=True) + EPS)
        xh = ov * r
        dm = dm_ref[...]
        dxh = dm * w_ref[...]
        t = jnp.mean(dxh * xh, axis=-1, keepdims=True)
        do = r * (dxh - xh * t)
        _store_cols(do_ref, do)
        dob = do.astype(BF16).astype(F32)
        for kv in range(ATT_KV):
            delta_ref[kv, 0] = jnp.sum(_cols_from_tokens(dob * ov, kv), axis=0, keepdims=True)
        dw_ref[...] += jnp.sum(dm * xh, axis=0, keepdims=True)

    tok = pl.BlockSpec((tm, ATT_QW), lambda i: (i, 0))
    vec = pl.BlockSpec((1, ATT_QW), lambda i: (0, 0))
    return pl.pallas_call(
        body, name=name, grid=(T // tm,),
        in_specs=[pl.BlockSpec((tm, ATT_QW), lambda i: (i, 1)), tok, vec],
        out_specs=[pl.BlockSpec((ATT_KV, 1, ATT_DH, R), lambda i: (0, i, 0, 0)),
                   pl.BlockSpec((ATT_KV, 1, 1, R), lambda i: (0, i, 0, 0)), vec],
        out_shape=[jax.ShapeDtypeStruct((ATT_KV, T // tm, ATT_DH, R), BF16),
                   jax.ShapeDtypeStruct((ATT_KV, T // tm, 1, R), F32), jax.ShapeDtypeStruct((1, ATT_QW), F32)],
        compiler_params=_params(("arbitrary",)),
    )(dmix, o, w)


def _ffn_up(h2, wg_t, wu_t, *, name, tm=512):
    T = h2.shape[0]
    tn = _pick(D_FF, 1408)
    nt = (((1,), (1,)), ((), ()))

    def body(h_ref, wg_ref, wu_ref, g_ref, u_ref, a_ref):
        hv = h_ref[...]
        g = lax.dot_general(hv, wg_ref[...], nt, preferred_element_type=F32)
        u = lax.dot_general(hv, wu_ref[...], nt, preferred_element_type=F32)
        g_ref[...] = g.astype(BF16)
        u_ref[...] = u.astype(BF16)
        a_ref[...] = (g * _sigmoid(g) * u).astype(BF16)

    wspec = pl.BlockSpec((tn, D_MODEL), lambda i, j: (j, 0))
    ospec = pl.BlockSpec((tm, tn), lambda i, j: (i, j))
    return pl.pallas_call(
        body, name=name, grid=(T // tm, D_FF // tn),
        in_specs=[pl.BlockSpec((tm, D_MODEL), lambda i, j: (i, 0)), wspec, wspec],
        out_specs=[ospec] * 3, out_shape=[jax.ShapeDtypeStruct((T, D_FF), BF16)] * 3,
        compiler_params=_params(("parallel", "arbitrary")),
    )(h2, wg_t, wu_t)


def _ffn_act_bwd(dx2b, w_down, gate, up, *, name, tm=512):
    T = dx2b.shape[0]
    tn = _pick(D_FF, 1408)

    def body(dx_ref, w_ref, g_ref, u_ref, dg_ref, du_ref):
        da = lax.dot_general(dx_ref[...], w_ref[...], (((1,), (1,)), ((), ())), preferred_element_type=F32)
        g = g_ref[...].astype(F32)
        u = u_ref[...].astype(F32)
        sg = _sigmoid(g)
        dg_ref[...] = (da * u * (sg * (1.0 + g * (1.0 - sg)))).astype(BF16)
        du_ref[...] = (da * (g * sg)).astype(BF16)

    ospec = pl.BlockSpec((tm, tn), lambda i, j: (i, j))
    return pl.pallas_call(
        body, name=name, grid=(T // tm, D_FF // tn),
        in_specs=[pl.BlockSpec((tm, D_MODEL), lambda i, j: (i, 0)),
                  pl.BlockSpec((tn, D_MODEL), lambda i, j: (j, 0)), ospec, ospec],
        out_specs=[ospec] * 2, out_shape=[jax.ShapeDtypeStruct((T, D_FF), BF16)] * 2,
        compiler_params=_params(("parallel", "arbitrary")),
    )(dx2b, w_down, gate, up)


def _adam_math(w, g, m, v):
    m = ADAM_B1 * m + (1.0 - ADAM_B1) * g
    v = ADAM_B2 * v + (1.0 - ADAM_B2) * (g * g)
    m_hat = m / (1.0 - ADAM_B1 ** ADAM_STEP)
    v_hat = v / (1.0 - ADAM_B2 ** ADAM_STEP)
    delta = -ADAM_LR * (m_hat / (jnp.sqrt(v_hat) + ADAM_EPS) + ADAM_WD * w)
    return delta, m, v


def _adamw(parts, w, m, v, *, name, tr_cap=256):
    P, R, C = parts.shape
    tr = R
    for t in range(8, min(R, tr_cap) + 1, 8):
        if R % t == 0:
            tr = t

    def body(p_ref, w_ref, m_ref, v_ref, g_ref, d_ref, nm_ref, nv_ref):
        g = p_ref[0].astype(F32)
        for j in range(1, P):
            g = g + p_ref[j].astype(F32)
        d, nm, nv = _adam_math(w_ref[...], g, m_ref[...], v_ref[...])
        g_ref[...] = g
        d_ref[...] = d
        nm_ref[...] = nm
        nv_ref[...] = nv

    blk = pl.BlockSpec((tr, C), lambda i: (i, 0))
    return pl.pallas_call(
        body, name=name, grid=(R // tr,),
        in_specs=[pl.BlockSpec((P, tr, C), lambda i: (0, i, 0)), blk, blk, blk],
        out_specs=[blk] * 4, out_shape=[jax.ShapeDtypeStruct((R, C), F32)] * 4,
        compiler_params=_params(("parallel",)),
    )(parts, w, m, v)


def _gather_halves(ins, outs, send_sems, recv_sems, local_sems):
    n = len(ins)
    x, y, c = lax.axis_index("x"), lax.axis_index("y"), lax.axis_index("c")
    me, sibling = (x, y, c), (x, y, 1 - c)
    chips = [(1 - x, y), (x, 1 - y), (1 - x, 1 - y)]

    def slot(p):
        return 4 * p[0] + 2 * p[1] + p[2]

    def copy(a, k, block, to, src=None):
        dst = outs[a].at[slot(block)]
        return pltpu.make_async_remote_copy(
            src_ref=dst if src is None else src, dst_ref=dst,
            send_sem=send_sems.at[a * 7 + k], recv_sem=recv_sems.at[a * 7 + k],
            device_id=to, device_id_type=MESH)

    mine = [pltpu.make_async_copy(ins[a], outs[a].at[slot(me)], local_sems.at[a]) for a in range(n)]
    first = []
    for a in range(n):
        first.append(copy(a, 0, me, sibling, src=ins[a]))
        first += [copy(a, 1 + j, me, (*chip, c), src=ins[a]) for j, chip in enumerate(chips)]

    def start():
        for cp in mine + first:
            cp.start()

    def finish():
        passed = []
        for j, chip in enumerate(chips):
            for a in range(n):
                copy(a, 1 + j, (*chip, c), me).wait_recv()
                cp = copy(a, 4 + j, (*chip, c), sibling)
                cp.start()
                passed.append(cp)
        for a in range(n):
            copy(a, 0, sibling, me).wait_recv()
            for j, chip in enumerate(chips):
                copy(a, 4 + j, (*chip, 1 - c), me).wait_recv()
        for cp in first + passed:
            cp.wait_send()
        for cp in mine:
            cp.wait()

    return start, finish


def _gather_scratch(n):
    return [pltpu.SemaphoreType.DMA((7 * n,)), pltpu.SemaphoreType.DMA((7 * n,)), pltpu.SemaphoreType.DMA((n,))]


def _gathered_shapes(xs):
    return [jax.ShapeDtypeStruct((N_DEV,) + x.shape, x.dtype) for x in xs]


def _ride_gather(xs):
    xs = list(xs)
    return dict(arrays=xs, out_shape=_gathered_shapes(xs), scratch=_gather_scratch(len(xs)), halves=_gather_halves)


def _ride_chips(gs):
    gs = list(gs)
    n = len(gs)

    def halves(ins, outs, send_sems, recv_sems, local_sems):
        mine, copies = _exchange_copies(ins, outs, send_sems, recv_sems, local_sems, masks=CHIP_MASKS, slot=_chip_slot)

        def start():
            for cp in mine:
                cp.start()
            for send, _ in copies:
                send.start()

        def finish():
            for send, recv in copies:
                recv.wait_recv()
                send.wait_send()
            for cp in mine:
                cp.wait()

        return start, finish

    n_sem = len(CHIP_MASKS) * n
    return dict(arrays=gs, out_shape=[jax.ShapeDtypeStruct(g.shape, g.dtype) for g in gs], halves=halves,
                scratch=[pltpu.SemaphoreType.DMA((n_sem,)), pltpu.SemaphoreType.DMA((n_sem,)),
                         pltpu.SemaphoreType.DMA((n,))])


ALL_MASKS = [(mx, my, mc) for mx in (0, 1) for my in (0, 1) for mc in (0, 1)][1:]


def _flip(v, bit):
    return 1 - v if bit else v


def _exchange_copies(ins, outs, send_sems, recv_sems, local_sems, *, masks, slot):
    n, n_peers = len(ins), len(masks)
    x, y, c = lax.axis_index("x"), lax.axis_index("y"), lax.axis_index("c")
    my_slot = slot((x, y, c))
    mine = [pltpu.make_async_copy(ins[a].at[my_slot], outs[a].at[my_slot], local_sems.at[a]) for a in range(n)]
    copies = []
    for a in range(n):
        for k, (mx, my, mc) in enumerate(masks):
            peer = (_flip(x, mx), _flip(y, my), _flip(c, mc))
            peer_slot = slot(peer)
            sems = dict(send_sem=send_sems.at[a * n_peers + k], recv_sem=recv_sems.at[a * n_peers + k],
                        device_id=peer, device_id_type=MESH)
            copies.append((
                pltpu.make_async_remote_copy(src_ref=ins[a].at[peer_slot], dst_ref=outs[a].at[my_slot], **sems),
                pltpu.make_async_remote_copy(src_ref=ins[a].at[peer_slot], dst_ref=outs[a].at[peer_slot], **sems)))
    return mine, copies


def _send_to_all(v, *, name):
    def body(v_ref, out_ref, send_sems, recv_sems, local_sem):
        x, y, c = lax.axis_index("x"), lax.axis_index("y"), lax.axis_index("c")
        me = 4 * x + 2 * y + c
        mine = pltpu.make_async_copy(v_ref, out_ref.at[me], local_sem)
        mine.start()
        copies = []
        for k, (mx, my, mc) in enumerate(ALL_MASKS):
            peer = (_flip(x, mx), _flip(y, my), _flip(c, mc))
            peer_id = 4 * peer[0] + 2 * peer[1] + peer[2]
            sems = dict(send_sem=send_sems.at[k], recv_sem=recv_sems.at[k], device_id=peer, device_id_type=MESH)
            copies.append((pltpu.make_async_remote_copy(src_ref=v_ref, dst_ref=out_ref.at[me], **sems),
                           pltpu.make_async_remote_copy(src_ref=v_ref, dst_ref=out_ref.at[peer_id], **sems)))
        for send, _ in copies:
            send.start()
        for send, recv in copies:
            recv.wait_recv()
            send.wait_send()
        mine.wait()

    n_peers = len(ALL_MASKS)
    return pl.pallas_call(
        body, name=name, in_specs=[ANY], out_specs=ANY,
        out_shape=jax.ShapeDtypeStruct((N_DEV,) + v.shape, v.dtype),
        scratch_shapes=[pltpu.SemaphoreType.DMA((n_peers,)), pltpu.SemaphoreType.DMA((n_peers,)),
                        pltpu.SemaphoreType.DMA],
        compiler_params=pltpu.CompilerParams(has_side_effects=True),
    )(v)


SWAP_ROW_CHUNKS = 4


def _ride_swap(gs):
    gs = list(gs)
    n = len(gs)

    def halves(ins, outs, send_sems, recv_sems):
        x, y, c = lax.axis_index("x"), lax.axis_index("y"), lax.axis_index("c")
        sibling = dict(device_id=(x, y, 1 - c), device_id_type=MESH)

        def start():
            for a in range(n):
                Q, _, R, _ = ins[a].shape
                rows = R // SWAP_ROW_CHUNKS
                for q in range(Q):
                    for j in range(SWAP_ROW_CHUNKS):
                        part = pl.ds(j * rows, rows)
                        pltpu.make_async_remote_copy(src_ref=ins[a].at[q, 1 - c, part], dst_ref=outs[a].at[q, part],
                                                     send_sem=send_sems.at[a], recv_sem=recv_sems.at[a], **sibling).start()

        def finish():
            for a in range(n):
                pltpu.make_async_remote_copy(src_ref=outs[a], dst_ref=outs[a], send_sem=send_sems.at[a],
                                             recv_sem=recv_sems.at[a], **sibling).wait()

        return start, finish

    return dict(arrays=gs, out_shape=[jax.ShapeDtypeStruct(g.shape[:1] + g.shape[2:], g.dtype) for g in gs],
                scratch=[pltpu.SemaphoreType.DMA((n,)), pltpu.SemaphoreType.DMA((n,))], halves=halves)


def _core_swap(gs, *, name):
    ride = _ride_swap(gs)
    n = len(gs)

    def body(*refs):
        start, finish = ride["halves"](refs[:n], refs[n:2 * n], *refs[2 * n:])
        start()
        finish()

    return pl.pallas_call(
        body, name=name, in_specs=[ANY] * n, out_specs=[ANY] * n, out_shape=ride["out_shape"],
        scratch_shapes=ride["scratch"], compiler_params=pltpu.CompilerParams(has_side_effects=True),
    )(*gs)


def _pair_sum(g, other, core, *, name, tr_cap=256):
    Q, _, R, C = g.shape
    tr = max(t for t in range(16, min(R, tr_cap) + 1, 16) if R % t == 0)

    def body(core_ref, g_ref, o_ref, out_ref):
        out_ref[0] = (g_ref[0, 0] + o_ref[0]).astype(BF16)

    return pl.pallas_call(
        body, name=name,
        grid_spec=pltpu.PrefetchScalarGridSpec(
            num_scalar_prefetch=1, grid=(Q, R // tr),
            in_specs=[pl.BlockSpec((1, 1, tr, C), lambda q, i, core_ref: (q, core_ref[0], i, 0)),
                      pl.BlockSpec((1, tr, C), lambda q, i, core_ref: (q, i, 0))],
            out_specs=pl.BlockSpec((1, tr, C), lambda q, i, core_ref: (q, i, 0))),
        out_shape=jax.ShapeDtypeStruct((Q, R, C), BF16),
        compiler_params=_params(("parallel", "parallel")),
    )(core, g, other)


def _pack_small(norm1, norm2, final, att, hg, qn, kn, lb=None, loss=None):
    z = lambda n: jnp.zeros((n,), F32)
    rows = [norm1.reshape(-1), norm2.reshape(-1), final.reshape(-1),
            jnp.concatenate([att.reshape(-1), z(512)]),
            jnp.concatenate([hg.reshape(-1), qn.reshape(-1), kn.reshape(-1), z(1024 - 256)]),
            z(1024) if lb is None else lb.reshape(-1),
            z(1024) if loss is None else jnp.concatenate([loss.reshape(-1), z(1023)]), z(1024)]
    return jnp.stack(rows, axis=0)


def _unpack_small(p):
    return (p[0:1, :], p[1:2, :], p[2, :], p[3:4, 0:512], p[4:5, 0:128], p[4:5, 128:192], p[4:5, 192:256])


def _fold_heads(dhg, dqn, dkn, *, name):
    def body(hg_ref, q_ref, k_ref, ohg_ref, oq_ref, ok_ref):
        def fold128(v):
            acc = v[:, 0:LANES]
            for j in range(1, v.shape[1] // LANES):
                acc = acc + v[:, j * LANES:(j + 1) * LANES]
            return acc

        ohg_ref[...] = fold128(hg_ref[...])
        q = fold128(q_ref[...])
        oq_ref[...] = q + pltpu.roll(q, ATT_DH, 1)
        k = k_ref[...]
        ok_ref[...] = k + pltpu.roll(k, ATT_DH, 1)

    return pl.pallas_call(body, name=name, out_shape=[jax.ShapeDtypeStruct((1, LANES), F32)] * 3)(dhg, dqn, dkn)


def _lb_grad(dlb_sum, lb, *, name):
    def body(d_ref, lb_ref, o_ref):
        lbv = lb_ref[...]
        gl = d_ref[...] * lbv * (1.0 - lbv)
        o_ref[0:1, :] = gl[0:1, :]
        o_ref[1:2, :] = -gl[0:1, :]
        o_ref[2:3, :] = gl[1:2, :]
        o_ref[3:4, :] = -gl[1:2, :]

    return pl.pallas_call(body, name=name, out_shape=jax.ShapeDtypeStruct((4, HG_W), F32))(dlb_sum, lb)


def _lower_bounds(lb_logits_full, *, name):
    def body(l_ref, o_ref):
        for d in range(2):
            l0, l1 = l_ref[2 * d:2 * d + 1, :], l_ref[2 * d + 1:2 * d + 2, :]
            mx = jnp.maximum(l0, l1)
            e0, e1 = jnp.exp(l0 - mx), jnp.exp(l1 - mx)
            o_ref[d:d + 1, :] = e0 / (e0 + e1)

    return pl.pallas_call(body, name=name, out_shape=jax.ShapeDtypeStruct((2, HG_W), F32))(
        lb_logits_full.reshape(4, HG_W))


def _local_step(x, target, norm1_w, w_in_t, lb, hg_norm_w, q_norm_w, k_norm_w, att_norm_w, w_out, norm2_w,
                w_g_t, w_u_t, w_down, final_norm_w, reduce_early=None, reduce_late=None, shards=None):
    T = x.shape[0]
    cos, sin = _rope_tables(T)
    qw8 = jnp.tile(q_norm_w, (1, ATT_HEADS))
    kw2 = jnp.tile(k_norm_w, (1, ATT_KV))

    if shards is None:
        h, r1 = _rms_fwd(x, norm1_w, name="norm1_fwd")
        U = _mm_nn([(h, w_in_t)], trans_b=True, name="in_proj")
        o_f, st_f = _gla_fwd(U, lb[0:1], f_block=1, reverse=False, name="gla_fwd_f")
    else:
        h, r1, g_in, g_lb = _rms_fwd(x, norm1_w, ride=_ride_gather([shards["w_in_t"], shards["lb_logits"]]),
                                     name="norm1_fwd")
        w_in_t = g_in.reshape(-1, D_MODEL)
        lb = _lower_bounds(g_lb.transpose(1, 0, 2).reshape(2, 2, -1), name="lower_bounds")
        U, g_gu = _mm_nn([(h, w_in_t)], trans_b=True, ride=_ride_gather([shards["w_gu_t"]]), name="in_proj")
        o_f, st_f, g_out, g_dn = _gla_fwd(U, lb[0:1], f_block=1, reverse=False,
                                          ride=_ride_gather([shards["w_out"], shards["w_down"]]), name="gla_fwd_f")
        g_gu = g_gu.reshape(2, -1, D_MODEL)
        w_g_t, w_u_t = g_gu[0], g_gu[1]
        w_out, w_down = g_out.reshape(-1, D_MODEL), g_dn.reshape(-1, D_MODEL)
    o_b, st_b = _gla_fwd(U, lb[1:2], f_block=2, reverse=True, name="gla_fwd_b")
    mix_hg = _hg_post_fwd(o_f, o_b, U, hg_norm_w, name="hg_post_fwd")
    q_c, qn_c, kmax2, k_c, v_c = _att_prep_fwd(U, cos, sin, qw8, kw2, name="att_prep_fwd")
    kmax = jnp.sqrt(jnp.max(kmax2.reshape(ATT_KV, ATT_DH), axis=1))
    m_c = qn_c * (kmax * 1.001).reshape(ATT_KV, 1, 1, 1)
    o_c, lse = lax.cond(jnp.max(m_c) <= FA_BOUND_MAX,
                        lambda: _flash_fwd_bounded(q_c, k_c, v_c, m_c, name="flash_fwd_bounded"),
                        lambda: _flash_fwd(q_c, k_c, v_c, name="flash_fwd"))
    o_att, mix_att = _att_post_fwd(o_c, att_norm_w, name="att_post_fwd")
    x1, h2, r2 = _mm_nn([(mix_hg, w_out[:HG_W]), (mix_att, w_out[HG_W:])], residual=x, tail=_tail_rms_fwd(norm2_w),
                        name="out_proj")
    gate, up, act = _ffn_up(h2, w_g_t, w_u_t, name="ffn_up")
    loss, dx2, dx2b, d_final = _mm_nn([(act, w_down)], residual=x1,
                                      tail=_tail_loss(target, final_norm_w.reshape(1, D_MODEL)), name="ffn_down")

    d_gate, d_up = _ffn_act_bwd(dx2b, w_down, gate, up, name="ffn_act_bwd")
    dw_down = _mm_tn(act, dx2b, tma_cap=1408, name="dw_down")
    dw_g = _mm_tn(d_gate, h2, tma_cap=1408, name="dw_gate")
    dw_u = _mm_tn(d_up, h2, tma_cap=1408, name="dw_up")
    mine = None if reduce_early is None else reduce_early["slabs"](dw_g, dw_u, dw_down)
    dx1, dx1b, d_norm2, *theirs = _mm_nn([(d_gate, w_g_t), (d_up, w_u_t)], tm=256,
                                         ride=None if mine is None else _ride_swap(mine),
                                         tail=_tail_rms_bwd(x1, r2, norm2_w, dx2, emit_bf16=True), name="ffn_up_bwd")
    dmix = _mm_nn([(dx1b, w_out)], trans_b=True, name="out_proj_bwd")
    dw_out = jnp.concatenate([_mm_tn(mix_hg, dx1b, name="dw_out_hg"), _mm_tn(mix_att, dx1b, name="dw_out_att")], axis=0)
    do_c, delta, d_att = _att_post_bwd(dmix, o_att, att_norm_w, name="att_post_bwd")
    ride = None if reduce_early is None else _ride_chips(reduce_early["sums"](mine, theirs, dw_out))
    dq_c, dk_c, dv_c, *rode = _flash_bwd(q_c, k_c, v_c, do_c, lse, delta, ride=ride, name="flash_bwd")
    dU_att, d_qn, d_kn = _att_prep_bwd(U, dq_c, dk_c, dv_c, cos, sin, qw8, kw2, name="att_prep_bwd")
    do_hg, du_g, d_hg = _hg_post_bwd(dmix, o_f, o_b, U, hg_norm_w, name="hg_post_bwd")
    dq_f, dz_f, dv_f, dlb_f = _gla_bwd(U, lb[0:1], do_hg, st_f, f_block=1, reverse=False, name="gla_bwd_f")
    dU_hg, dlb_b = _gla_bwd(U, lb[1:2], do_hg, st_b, f_block=2, reverse=True, prev=(dq_f, dz_f, dv_f, du_g),
                            name="gla_bwd_b")
    w_hg = 5 * HG_W
    dw_in = jnp.concatenate([_mm_tn(dU_hg, h, tma_cap=1280, name="dw_in_hg"), _mm_tn(dU_att, h, name="dw_in_att")],
                            axis=0)
    late = None if reduce_late is None else _ride_chips(reduce_late(dw_in))
    grad_x, d_norm1, *rode_late = _mm_nn([(dU_hg, w_in_t[:w_hg]), (dU_att, w_in_t[w_hg:])], ride=late,
                                         tail=_tail_rms_bwd(x, r1, norm1_w, dx1, emit_bf16=False), name="in_proj_bwd")
    d_hg, d_qn, d_kn = _fold_heads(d_hg, d_qn, d_kn, name="fold_heads")

    big = dict(w_in=dw_in, w_out=dw_out, w_g=dw_g, w_u=dw_u, w_down=dw_down)
    small = dict(norm1=d_norm1, norm2=d_norm2, final=d_final, att=d_att, hg=d_hg,
                 qn=d_qn[:, :ATT_DH], kn=d_kn[:, :ATT_DH], lb=jnp.concatenate([dlb_f, dlb_b], axis=0))
    return loss, grad_x, big, small, rode + rode_late, lb


def kernel(x, norm1_w, w_in, lb_logits, hg_norm_w, q_norm_w, k_norm_w, att_norm_w, w_out, norm2_w, w_gate_up, w_down, final_norm_w, loss_target, m_norm1_w, m_w_in, m_lb_logits, m_hg_norm_w, m_q_norm_w, m_k_norm_w, m_att_norm_w, m_w_out, m_norm2_w, m_w_gate_up, m_w_down, m_final_norm_w, v_norm1_w, v_w_in, v_lb_logits, v_hg_norm_w, v_q_norm_w, v_k_norm_w, v_att_norm_w, v_w_out, v_norm2_w, v_w_gate_up, v_w_down, v_final_norm_w):
    T = x.shape[1]
    me = 4 * lax.axis_index("x") + 2 * lax.axis_index("y") + lax.axis_index("c")
    c_in, r_out, c_gu, r_dn = w_in.shape[2], w_out.shape[1], w_gate_up.shape[2], w_down.shape[1]
    lb_cols = lb_logits.shape[2]

    shards = dict(w_in_t=w_in[0].T.astype(BF16), lb_logits=lb_logits.reshape(4, lb_cols),
                  w_gu_t=w_gate_up[0].T.astype(BF16), w_out=w_out[0].astype(BF16), w_down=w_down[0].astype(BF16))

    chips = N_DEV // 2
    core = lax.axis_index("c").astype(jnp.int32).reshape(1)
    by_owner = lambda g, r: g.reshape(chips, 2, r, D_MODEL)

    def pair_sums(mine, theirs, names):
        return [_pair_sum(g, o, core, name="pair_sum_" + nm) for g, o, nm in zip(mine, theirs, names)]

    def early_slabs(dw_g_t, dw_u_t, dw_down):
        return [by_owner(jnp.concatenate([dw_g_t, dw_u_t], axis=0), c_gu), by_owner(dw_down, r_dn)]

    def early_sums(mine, theirs, dw_out):
        s_out = by_owner(dw_out, r_out)
        return pair_sums([s_out] + mine, list(_core_swap([s_out], name="exchange_cores_out")) + list(theirs),
                         ("w_out", "w_gu", "w_down"))

    def reduce_late(dw_in_t):
        mine = [by_owner(dw_in_t, c_in)]
        return pair_sums(mine, _core_swap(mine, name="exchange_cores_in"), ("w_in",))

    loss, grad_x, big, small, (p_out, p_gu, p_dn, p_in), lb = _local_step(
        x[0], loss_target[0], norm1_w, None, None, hg_norm_w, q_norm_w, k_norm_w, att_norm_w, None, norm2_w,
        None, None, None, final_norm_w, reduce_early=dict(slabs=early_slabs, sums=early_sums),
        reduce_late=reduce_late, shards=shards)
    p_gu, p_in = p_gu.transpose(0, 2, 1), p_in.transpose(0, 2, 1)

    packed = _pack_small(small["norm1"], small["norm2"], small["final"], small["att"], small["hg"],
                         small["qn"], small["kn"], small["lb"], loss)
    all_small = _send_to_all(packed, name="exchange_small")

    g_w_in, d_w_in, nm_w_in, nv_w_in = _adamw(p_in, w_in[0], m_w_in[0], v_w_in[0], name="adamw_w_in")
    g_w_out, d_w_out, nm_w_out, nv_w_out = _adamw(p_out, w_out[0], m_w_out[0], v_w_out[0], name="adamw_w_out")
    g_w_gu, d_w_gu, nm_w_gu, nv_w_gu = _adamw(p_gu, w_gate_up[0], m_w_gate_up[0], v_w_gate_up[0], name="adamw_w_gu")
    g_w_dn, d_w_dn, nm_w_dn, nv_w_dn = _adamw(p_dn, w_down[0], m_w_down[0], v_w_down[0], name="adamw_w_down")

    pk = lambda vecs: _pack_small(*vecs)
    w_pk = pk([norm1_w, norm2_w, final_norm_w, att_norm_w, hg_norm_w, q_norm_w, k_norm_w])
    m_pk = pk([m_norm1_w, m_norm2_w, m_final_norm_w, m_att_norm_w, m_hg_norm_w, m_q_norm_w, m_k_norm_w])
    v_pk = pk([v_norm1_w, v_norm2_w, v_final_norm_w, v_att_norm_w, v_hg_norm_w, v_q_norm_w, v_k_norm_w])
    g_pk, d_pk, nm_pk, nv_pk = _adamw(all_small, w_pk, m_pk, v_pk, name="adamw_small")

    dlb_sum = g_pk[5:6, :].reshape(2, HG_W)
    g_lb_full = _lb_grad(dlb_sum, lb, name="lb_grad")
    g_lb_mine = lax.dynamic_slice_in_dim(g_lb_full, me * lb_cols, lb_cols, axis=1)
    g_lb_s, d_lb, nm_lb, nv_lb = _adamw(g_lb_mine[None], lb_logits.reshape(4, lb_cols),
                                        m_lb_logits.reshape(4, lb_cols), v_lb_logits.reshape(4, lb_cols),
                                        name="adamw_lb")

    loss_total = g_pk[6, 0]

    def outs(big4, lb_arr, pk_arr):
        n1, n2, fin, att, hg, qn, kn = _unpack_small(pk_arr)
        b_in, b_out, b_gu, b_dn = big4
        return [n1, b_in[None], lb_arr.reshape(2, 2, lb_cols), hg, qn, kn, att, b_out[None], n2, b_gu[None],
                b_dn[None], fin]

    return (loss_total, grad_x[None],
            *outs((g_w_in, g_w_out, g_w_gu, g_w_dn), g_lb_s, g_pk),
            *outs((d_w_in, d_w_out, d_w_gu, d_w_dn), d_lb, d_pk),
            *outs((nm_w_in, nm_w_out, nm_w_gu, nm_w_dn), nm_lb, nm_pk),
            *outs((nv_w_in, nv_w_out, nv_w_gu, nv_w_dn), nv_lb, nv_pk))
```

```python
import math

import jax
import jax.numpy as jnp
import numpy as np
from jax import lax
from jax.experimental import pallas as pl
from jax.experimental.pallas import tpu as pltpu

F32 = jnp.float32
BF16 = jnp.bfloat16

N_DEV = 8
D_MODEL = 1024
EPS = 1e-6
HG_HEADS = 4
HG_D = 128
HG_W = HG_HEADS * HG_D
CHUNK = 64
ATT_HEADS = 8
ATT_KV = 2
ATT_G = ATT_HEADS // ATT_KV
ATT_DH = 64
ATT_QW = ATT_HEADS * ATT_DH
ATT_KW = ATT_KV * ATT_DH
GRID_W = 64
ROPE_THETA = 10000.0
D_FF = 2816
ADAM_LR, ADAM_B1, ADAM_B2, ADAM_EPS, ADAM_WD, ADAM_STEP = 0.001, 0.9, 0.999, 1e-08, 0.01, 10

LOG2E = math.log2(math.e)
LANES = 128
VMEM_LIMIT = 48 * 1024 * 1024
MESH = pl.DeviceIdType.MESH
ANY = pl.BlockSpec(memory_space=pl.ANY)


def _params(sem=None):
    return pltpu.CompilerParams(dimension_semantics=sem, vmem_limit_bytes=VMEM_LIMIT)


def _pick(n, cap):
    best = None
    for t in range(LANES, cap + 1, LANES):
        if n % t == 0:
            best = t
    assert best is not None, (n, cap)
    return best


def _sigmoid(x):
    return 1.0 / (1.0 + jnp.exp(-x))


def _dot(a, b):
    return jnp.dot(a.astype(BF16), b.astype(BF16), preferred_element_type=F32)


def _dot_nt(a, b):
    return lax.dot_general(a.astype(BF16), b.astype(BF16), (((1,), (1,)), ((), ())),
                           preferred_element_type=F32)


def _dot_tn(a, b):
    return lax.dot_general(a.astype(BF16), b.astype(BF16), (((0,), (0,)), ((), ())),
                           preferred_element_type=F32)


def _mm_nn(pairs, *, name, out_dtype=F32, residual=None, tm=512, tn_cap=None, trans_b=False, tail=None, ride=None):
    M = pairs[0][0].shape[0]
    N = pairs[0][1].shape[0 if trans_b else 1]
    tn = N if tn_cap is None else _pick(N, tn_cap)
    n_pairs = len(pairs)
    has_res = residual is not None
    dims = (((1,), (1,)), ((), ())) if trans_b else (((1,), (0,)), ((), ()))
    assert (tail is None and ride is None) or tn == N
    n_main = 2 * n_pairs + has_res
    n_ti = 0 if tail is None else len(tail["ins"])
    n_out = 1 if tail is None else len(tail["outs"])
    n_r = 0 if ride is None else len(ride["arrays"])
    n_in = n_main + n_ti + n_r

    def body(*refs):
        outs = refs[n_in:n_in + n_out]
        if n_r:
            start, finish = ride["halves"](refs[n_main + n_ti:n_in], refs[n_in + n_out:n_in + n_out + n_r],
                                           *refs[n_in + n_out + n_r:])
            pl.when(pl.program_id(0) == 0)(start)
        acc = None
        for i in range(n_pairs):
            d = lax.dot_general(refs[2 * i][...], refs[2 * i + 1][...], dims, preferred_element_type=F32)
            acc = d if acc is None else acc + d
        if has_res:
            acc = acc + refs[2 * n_pairs][...]
        if tail is None:
            outs[0][...] = acc.astype(out_dtype)
        else:
            tail["fn"](acc, pl.program_id(0) == 0, *refs[n_main:n_main + n_ti], *outs)
        if n_r:
            pl.when(pl.program_id(0) == M // tm - 1)(finish)

    kinds = {"row": ((tm, N), (M, N), lambda i, j: (i, 0)), "col": ((tm, 1), (M, 1), lambda i, j: (i, 0)),
             "vec": ((1, N), (1, N), lambda i, j: (0, 0)), "one": ((1, 1), (1, 1), lambda i, j: (0, 0))}
    in_specs, args = [], []
    for a, b in pairs:
        k = a.shape[1]
        b_spec = pl.BlockSpec((tn, k), lambda i, j: (j, 0)) if trans_b else pl.BlockSpec((k, tn), lambda i, j: (0, j))
        in_specs += [pl.BlockSpec((tm, k), lambda i, j: (i, 0)), b_spec]
        args += [a, b]
    if has_res:
        in_specs.append(pl.BlockSpec((tm, tn), lambda i, j: (i, j)))
        args.append(residual)
    if tail is None:
        out_specs = [pl.BlockSpec((tm, tn), lambda i, j: (i, j))]
        out_shape = [jax.ShapeDtypeStruct((M, N), out_dtype)]
    else:
        for arr, kind in tail["ins"]:
            in_specs.append(pl.BlockSpec(kinds[kind][0], kinds[kind][2]))
            args.append(arr)
        out_specs = [pl.BlockSpec(kinds[kind][0], kinds[kind][2]) for _, kind in tail["outs"]]
        out_shape = [jax.ShapeDtypeStruct(kinds[kind][1], dt) for dt, kind in tail["outs"]]
    scratch = []
    if n_r:
        in_specs += [ANY] * n_r
        args += ride["arrays"]
        out_specs += [ANY] * n_r
        out_shape += ride["out_shape"]
        scratch = ride["scratch"]
    sequential = tail is not None or n_r > 0
    res = pl.pallas_call(
        body, name=name, grid=(M // tm, N // tn), in_specs=in_specs, out_specs=out_specs, out_shape=out_shape,
        scratch_shapes=scratch,
        compiler_params=pltpu.CompilerParams(dimension_semantics=("arbitrary" if sequential else "parallel", "arbitrary"),
                                             vmem_limit_bytes=VMEM_LIMIT, has_side_effects=n_r > 0),
    )(*args)
    return res[0] if len(res) == 1 else res


def _mm_tn(a, b, *, name, tma_cap=1024, tnb_cap=1024, tk=1024, rows=None, into=None):
    T, Ma = a.shape
    Nb = b.shape[1]
    tma, tnb = _pick(Ma, tma_cap), _pick(Nb, tnb_cap)
    tk = min(tk, T)
    n_k = T // tk
    first_row, total = (0, Ma) if rows is None else rows
    assert first_row % tma == 0
    i0 = first_row // tma

    def body(a_ref, b_ref, *rest):
        o_ref, acc_ref = rest[-2:]
        k = pl.program_id(2)

        @pl.when(k == 0)
        def _():
            acc_ref[...] = jnp.zeros_like(acc_ref)

        acc_ref[...] += lax.dot_general(a_ref[...], b_ref[...], (((0,), (0,)), ((), ())),
                                        preferred_element_type=F32)

        @pl.when(k == n_k - 1)
        def _():
            o_ref[...] = acc_ref[...]

    in_specs = [pl.BlockSpec((tk, tma), lambda i, j, k: (k, i)), pl.BlockSpec((tk, tnb), lambda i, j, k: (k, j))]
    args = [a, b]
    if into is not None:
        in_specs.append(ANY)
        args.append(into)
    return pl.pallas_call(
        body, name=name, grid=(Ma // tma, Nb // tnb, n_k), in_specs=in_specs,
        out_specs=pl.BlockSpec((tma, tnb), lambda i, j, k: (i0 + i, j)),
        out_shape=jax.ShapeDtypeStruct((total, Nb), F32),
        scratch_shapes=[pltpu.VMEM((tma, tnb), F32)],
        input_output_aliases={} if into is None else {2: 0},
        compiler_params=_params(("parallel", "parallel", "arbitrary")),
    )(*args)


def _rms_fwd(x, w, *, name, tm=512, ride=None):
    T, Dm = x.shape
    n_r = 0 if ride is None else len(ride["arrays"])

    def body(x_ref, w_ref, *rest):
        h_ref, r_ref = rest[n_r:n_r + 2]
        if n_r:
            start, finish = ride["halves"](rest[:n_r], rest[n_r + 2:2 * n_r + 2], *rest[2 * n_r + 2:])
            pl.when(pl.program_id(0) == 0)(start)
        xv = x_ref[...]
        r = lax.rsqrt(jnp.mean(xv * xv, axis=-1, keepdims=True) + EPS)
        h_ref[...] = (xv * r * w_ref[...]).astype(BF16)
        r_ref[...] = r
        if n_r:
            pl.when(pl.program_id(0) == T // tm - 1)(finish)

    return pl.pallas_call(
        body, name=name, grid=(T // tm,),
        in_specs=[pl.BlockSpec((tm, Dm), lambda i: (i, 0)), pl.BlockSpec((1, Dm), lambda i: (0, 0))] + [ANY] * n_r,
        out_specs=[pl.BlockSpec((tm, Dm), lambda i: (i, 0)), pl.BlockSpec((tm, 1), lambda i: (i, 0))] + [ANY] * n_r,
        out_shape=[jax.ShapeDtypeStruct((T, Dm), BF16), jax.ShapeDtypeStruct((T, 1), F32)]
                  + (ride["out_shape"] if n_r else []),
        scratch_shapes=ride["scratch"] if n_r else [],
        compiler_params=pltpu.CompilerParams(dimension_semantics=("arbitrary" if n_r else "parallel",),
                                             vmem_limit_bytes=VMEM_LIMIT, has_side_effects=n_r > 0),
    )(x, w, *(ride["arrays"] if n_r else []))


def _tail_rms_fwd(w):
    def fn(xv, first, w_ref, x_ref, h_ref, r_ref):
        r = lax.rsqrt(jnp.mean(xv * xv, axis=-1, keepdims=True) + EPS)
        x_ref[...] = xv
        h_ref[...] = (xv * r * w_ref[...]).astype(BF16)
        r_ref[...] = r

    return dict(fn=fn, ins=[(w, "vec")], outs=[(F32, "row"), (BF16, "row"), (F32, "col")])


def _tail_rms_bwd(x, r, w, dres, *, emit_bf16):
    def fn(dhv, first, x_ref, r_ref, w_ref, dres_ref, *outs):
        dx_ref, dw_ref = outs[0], outs[-1]

        @pl.when(first)
        def _():
            dw_ref[...] = jnp.zeros_like(dw_ref)

        rv = r_ref[...]
        xh = x_ref[...] * rv
        dxh = dhv * w_ref[...]
        t = jnp.mean(dxh * xh, axis=-1, keepdims=True)
        dx = dres_ref[...] + rv * (dxh - xh * t)
        dx_ref[...] = dx
        if emit_bf16:
            outs[1][...] = dx.astype(BF16)
        dw_ref[...] += jnp.sum(dhv * xh, axis=0, keepdims=True)

    outs = [(F32, "row")] + ([(BF16, "row")] if emit_bf16 else []) + [(F32, "vec")]
    return dict(fn=fn, ins=[(x, "row"), (r, "col"), (w, "vec"), (dres, "row")], outs=outs)


def _tail_loss(target, w):
    def fn(xv, first, t_ref, w_ref, loss_ref, dx_ref, dxb_ref, dw_ref):
        @pl.when(first)
        def _():
            loss_ref[...] = jnp.zeros_like(loss_ref)
            dw_ref[...] = jnp.zeros_like(dw_ref)

        r = lax.rsqrt(jnp.mean(xv * xv, axis=-1, keepdims=True) + EPS)
        xh = xv * r
        wv = w_ref[...]
        err = xh * wv - t_ref[...]
        row_loss = jnp.mean(err * err, axis=-1, keepdims=True)
        loss_ref[...] += 0.5 * jnp.sum(row_loss, axis=0, keepdims=True)
        dy = err * (1.0 / xv.shape[-1])
        dxh = dy * wv
        t = jnp.mean(dxh * xh, axis=-1, keepdims=True)
        dx = r * (dxh - xh * t)
        dx_ref[...] = dx
        dxb_ref[...] = dx.astype(BF16)
        dw_ref[...] += jnp.sum(dy * xh, axis=0, keepdims=True)

    return dict(fn=fn, ins=[(target, "row"), (w, "vec")],
                outs=[(F32, "one"), (F32, "row"), (BF16, "row"), (F32, "vec")])


GLA_TB = 512
GLA_NC = GLA_TB // CHUNK
GLA_UNROLL = 4


def _cumsum_rows(x, row, reverse):
    n = x.shape[0]
    s = 1
    while s < n:
        if not reverse:
            x = x + jnp.where(row >= s, pltpu.roll(x, s, 0), 0.0)
        else:
            x = x + jnp.where(row < n - s, pltpu.roll(x, n - s, 0), 0.0)
        s *= 2
    return x


def _gla_gates(uq, z, lbv):
    q = uq * _sigmoid(uq)
    sg = _sigmoid(z)
    sgn = _sigmoid(-z)
    f = lbv + (1.0 - lbv) * sg
    k = (1.0 - lbv) * sgn
    return q, sg, sgn, f, k


def _gla_decays(f, row, reverse):
    b = _cumsum_rows(jnp.log(f), row, reverse)
    if not reverse:
        bref, blast = b[CHUNK // 2 - 1:CHUNK // 2, :], b[CHUNK - 1:CHUNK, :]
    else:
        bref, blast = b[CHUNK // 2:CHUNK // 2 + 1, :], b[0:1, :]
    return b, bref, blast


def _gla_fwd(U, lb, *, f_block, reverse, name, ride=None):
    T = U.shape[0]
    nb = T // GLA_TB
    n_g = 0 if ride is None else len(ride["arrays"])

    def body(uq_ref, uf_ref, ui_ref, lb_ref, *rest):
        g_in, rest = rest[:n_g], rest[n_g:]
        o_ref, st_ref = rest[:2]
        g_out, rest = rest[2:2 + n_g], rest[2 + n_g:]
        s_ref = rest[0]
        if n_g:
            start, finish = ride["halves"](g_in, g_out, *rest[1:])
            pl.when(pl.program_id(0) == 0)(start)

        @pl.when(pl.program_id(0) == 0)
        def _():
            s_ref[...] = jnp.zeros_like(s_ref)

        row = lax.broadcasted_iota(jnp.int32, (CHUNK, HG_D), 0)
        ri = lax.broadcasted_iota(jnp.int32, (CHUNK, CHUNK), 0)
        ci = lax.broadcasted_iota(jnp.int32, (CHUNK, CHUNK), 1)
        mask = (ri <= ci) if reverse else (ri >= ci)

        def chunk(j, carry):
            c = (GLA_NC - 1 - j) if reverse else j
            rows = pl.ds(pl.multiple_of(c * CHUNK, CHUNK), CHUNK)
            for h in range(HG_HEADS):
                cols = pl.ds(h * HG_D, HG_D)
                v = ui_ref[rows, cols]
                q, _, _, f, k = _gla_gates(uq_ref[rows, cols], uf_ref[rows, cols], lb_ref[:, cols])
                b, bref, blast = _gla_decays(f, row, reverse)
                s = jnp.where(mask, _dot_nt(q * jnp.exp(b - bref), k * jnp.exp(bref - b)), 0.0)
                st = s_ref[h]
                st_ref[c, h] = st
                o_ref[rows, cols] = _dot(s, v) + _dot_nt(q * jnp.exp(b), st)
                s_ref[h] = st * jnp.exp(blast) + _dot_tn(v, k * jnp.exp(blast - b))
            return carry

        lax.fori_loop(0, GLA_NC, chunk, 0, unroll=GLA_NC)
        if n_g:
            pl.when(pl.program_id(0) == nb - 1)(finish)

    blk = (lambda i: nb - 1 - i) if reverse else (lambda i: i)
    ucol = lambda cb: pl.BlockSpec((GLA_TB, HG_W), lambda i: (blk(i), cb))
    return pl.pallas_call(
        body, name=name, grid=(nb,),
        in_specs=[ucol(0), ucol(f_block), ucol(3), pl.BlockSpec((1, HG_W), lambda i: (0, 0))] + [ANY] * n_g,
        out_specs=[pl.BlockSpec((GLA_TB, HG_W), lambda i: (blk(i), 0)),
                   pl.BlockSpec((GLA_NC, HG_HEADS, HG_D, HG_D), lambda i: (blk(i), 0, 0, 0))] + [ANY] * n_g,
        out_shape=[jax.ShapeDtypeStruct((T, HG_W), F32),
                   jax.ShapeDtypeStruct((T // CHUNK, HG_HEADS, HG_D, HG_D), F32)] + (ride["out_shape"] if n_g else []),
        scratch_shapes=[pltpu.VMEM((HG_HEADS, HG_D, HG_D), F32)] + (ride["scratch"] if n_g else []),
        compiler_params=pltpu.CompilerParams(dimension_semantics=("arbitrary",), vmem_limit_bytes=VMEM_LIMIT,
                                             has_side_effects=bool(n_g)),
    )(U, U, U, lb, *(ride["arrays"] if n_g else []))


def _gla_bwd(U, lb, do, states, *, f_block, reverse, name, prev=None):
    T = U.shape[0]
    nb = T // GLA_TB
    final = prev is not None

    def body(uq_ref, uf_ref, ui_ref, lb_ref, do_ref, st_ref, *rest):
        if final:
            dqp_ref, dzp_ref, dvp_ref, dug_ref, out_ref, dlb_ref, ds_ref = rest
        else:
            dq_ref, dz_ref, dv_ref, dlb_ref, ds_ref = rest

        @pl.when(pl.program_id(0) == 0)
        def _():
            ds_ref[...] = jnp.zeros_like(ds_ref)
            dlb_ref[...] = jnp.zeros_like(dlb_ref)

        row = lax.broadcasted_iota(jnp.int32, (CHUNK, HG_D), 0)
        ri = lax.broadcasted_iota(jnp.int32, (CHUNK, CHUNK), 0)
        ci = lax.broadcasted_iota(jnp.int32, (CHUNK, CHUNK), 1)
        mask = (ri <= ci) if reverse else (ri >= ci)

        def chunk(j, carry):
            c = j if reverse else (GLA_NC - 1 - j)
            rows = pl.ds(pl.multiple_of(c * CHUNK, CHUNK), CHUNK)
            for h in range(HG_HEADS):
                cols = pl.ds(h * HG_D, HG_D)
                v = ui_ref[rows, cols]
                lbv = lb_ref[:, cols]
                uq = uq_ref[rows, cols]
                q, sg, sgn, f, k = _gla_gates(uq, uf_ref[rows, cols], lbv)
                b, bref, blast = _gla_decays(f, row, reverse)
                eq, ek, eb, el, dec = (jnp.exp(b - bref), jnp.exp(bref - b), jnp.exp(b), jnp.exp(blast - b),
                                       jnp.exp(blast))
                qin, kin, qb, klast = q * eq, k * ek, q * eb, k * el
                dov = do_ref[rows, cols]
                st = st_ref[c, h]
                dst = ds_ref[h]
                p = jnp.where(mask, _dot_nt(qin, kin), 0.0)
                dp = jnp.where(mask, _dot_nt(dov, v), 0.0)
                dqin = _dot(dp, kin)
                dkin = _dot_tn(dp, qin)
                dv = _dot_tn(p, dov) + _dot_nt(klast, dst)
                dqb = _dot(dov, st)
                dklast = _dot(v, dst)
                ds_ref[h] = _dot_tn(dov, qb) + dst * dec
                db = dqin * qin - dkin * kin + dqb * qb - dklast * klast
                extra = (jnp.sum(dklast * klast, axis=0, keepdims=True)
                         + dec * jnp.sum(st * dst, axis=0, keepdims=True))
                dg = _cumsum_rows(db, row, not reverse) + extra
                dq = dqin * eq + dqb * eb
                dk = dkin * ek + dklast * el
                dfk = dg / f - dk
                dz = (dfk * (1.0 - lbv) * sg * sgn).astype(BF16)
                dlb_ref[:, cols] += jnp.sum(dfk * sgn, axis=0, keepdims=True)
                if final:
                    sq = _sigmoid(uq)
                    col = lambda blk: pl.ds(blk * HG_W + h * HG_D, HG_D)
                    out_ref[rows, col(0)] = ((dq + dqp_ref[rows, cols]) * (sq * (1.0 + uq * (1.0 - sq)))).astype(BF16)
                    out_ref[rows, col(1)] = dzp_ref[rows, cols]
                    out_ref[rows, col(2)] = dz
                    out_ref[rows, col(3)] = (dv + dvp_ref[rows, cols]).astype(BF16)
                    out_ref[rows, col(4)] = dug_ref[rows, cols]
                else:
                    dq_ref[rows, cols] = dq
                    dz_ref[rows, cols] = dz
                    dv_ref[rows, cols] = dv
            return carry

        lax.fori_loop(0, GLA_NC, chunk, 0, unroll=GLA_UNROLL)

    blk = (lambda i: i) if reverse else (lambda i: nb - 1 - i)
    ucol = lambda cb: pl.BlockSpec((GLA_TB, HG_W), lambda i: (blk(i), cb))
    tok = pl.BlockSpec((GLA_TB, HG_W), lambda i: (blk(i), 0))
    vec = pl.BlockSpec((1, HG_W), lambda i: (0, 0))
    in_specs = [ucol(0), ucol(f_block), ucol(3), vec, tok,
                pl.BlockSpec((GLA_NC, HG_HEADS, HG_D, HG_D), lambda i: (blk(i), 0, 0, 0))]
    vec_shape = jax.ShapeDtypeStruct((1, HG_W), F32)
    if final:
        in_specs += [tok] * 4
        out_specs = [pl.BlockSpec((GLA_TB, 5 * HG_W), lambda i: (blk(i), 0)), vec]
        out_shape = [jax.ShapeDtypeStruct((T, 5 * HG_W), BF16), vec_shape]
    else:
        out_specs = [tok, tok, tok, vec]
        out_shape = [jax.ShapeDtypeStruct((T, HG_W), F32), jax.ShapeDtypeStruct((T, HG_W), BF16),
                     jax.ShapeDtypeStruct((T, HG_W), F32), vec_shape]
    return pl.pallas_call(
        body, name=name, grid=(nb,), in_specs=in_specs, out_specs=out_specs, out_shape=out_shape,
        scratch_shapes=[pltpu.VMEM((HG_HEADS, HG_D, HG_D), F32)],
        compiler_params=_params(("arbitrary",)),
    )(U, U, U, lb, do, states, *(prev if final else ()))


def _hg_post_fwd(o_f, o_b, U, w, *, name, tm=512):
    T = o_f.shape[0]

    def body(of_ref, ob_ref, ug_ref, w_ref, out_ref):
        wv = w_ref[...]
        for h in range(HG_HEADS):
            cols = pl.ds(h * HG_D, HG_D)
            o = of_ref[:, cols] + ob_ref[:, cols]
            r = lax.rsqrt(jnp.mean(o * o, axis=-1, keepdims=True) + EPS)
            ug = ug_ref[:, cols]
            out_ref[:, cols] = (o * r * wv * (ug * _sigmoid(ug))).astype(BF16)

    tok = pl.BlockSpec((tm, HG_W), lambda i: (i, 0))
    return pl.pallas_call(
        body, name=name, grid=(T // tm,),
        in_specs=[tok, tok, pl.BlockSpec((tm, HG_W), lambda i: (i, 4)), pl.BlockSpec((1, HG_D), lambda i: (0, 0))],
        out_specs=tok, out_shape=jax.ShapeDtypeStruct((T, HG_W), BF16),
        compiler_params=_params(("parallel",)),
    )(o_f, o_b, U, w)


def _hg_post_bwd(dmix, o_f, o_b, U, w, *, name, tm=512):
    T = o_f.shape[0]

    def body(dm_ref, of_ref, ob_ref, ug_ref, w_ref, do_ref, dug_ref, dw_ref):
        @pl.when(pl.program_id(0) == 0)
        def _():
            dw_ref[...] = jnp.zeros_like(dw_ref)

        wv = w_ref[...]
        for h in range(HG_HEADS):
            cols = pl.ds(h * HG_D, HG_D)
            o = of_ref[:, cols] + ob_ref[:, cols]
            r = lax.rsqrt(jnp.mean(o * o, axis=-1, keepdims=True) + EPS)
            xh = o * r
            ug = ug_ref[:, cols]
            sg = _sigmoid(ug)
            dm = dm_ref[:, cols]
            dn = dm * (ug * sg)
            dug_ref[:, cols] = (dm * (xh * wv) * (sg * (1.0 + ug * (1.0 - sg)))).astype(BF16)
            dxh = dn * wv
            t = jnp.mean(dxh * xh, axis=-1, keepdims=True)
            do_ref[:, cols] = r * (dxh - xh * t)
            dw_ref[:, cols] += jnp.sum(dn * xh, axis=0, keepdims=True)

    tok = pl.BlockSpec((tm, HG_W), lambda i: (i, 0))
    vec = pl.BlockSpec((1, HG_W), lambda i: (0, 0))
    return pl.pallas_call(
        body, name=name, grid=(T // tm,),
        in_specs=[tok, tok, tok, pl.BlockSpec((tm, HG_W), lambda i: (i, 4)), pl.BlockSpec((1, HG_D), lambda i: (0, 0))],
        out_specs=[tok, tok, vec],
        out_shape=[jax.ShapeDtypeStruct((T, HG_W), F32), jax.ShapeDtypeStruct((T, HG_W), BF16),
                   jax.ShapeDtypeStruct((1, HG_W), F32)],
        compiler_params=_params(("arbitrary",)),
    )(dmix, o_f, o_b, U, w)


def _rope_tables(T):
    rows = T // GRID_W
    row = np.repeat(np.arange(rows), GRID_W).astype(np.float32)
    col = np.tile(np.arange(GRID_W), rows).astype(np.float32)
    axis_dim = ATT_DH // 2
    freqs = (np.float32(ROPE_THETA) ** (-np.arange(0, axis_dim, 2, dtype=np.float32) / np.float32(axis_dim))
             ).astype(np.float32)
    ang = np.concatenate([row[:, None] * freqs, col[:, None] * freqs], axis=-1).astype(np.float32)
    cos, sin = np.cos(ang), np.sin(ang)
    c = np.repeat(cos, 2, axis=-1)
    s = np.stack([-sin, sin], axis=-1).reshape(T, ATT_DH)
    return jnp.asarray(np.tile(c, (1, 2)), F32), jnp.asarray(np.tile(s, (1, 2)), F32)


def _head_blockdiag(width):
    shift = ATT_DH.bit_length() - 1
    ri = jnp.right_shift(lax.broadcasted_iota(jnp.int32, (width, width), 0), shift)
    ci = jnp.right_shift(lax.broadcasted_iota(jnp.int32, (width, width), 1), shift)
    return jnp.where(ri == ci, 1.0, 0.0).astype(BF16)


def _head_sum(x, bd):
    hi = x.astype(BF16)
    lo = (x - hi.astype(F32)).astype(BF16)
    return jnp.dot(hi, bd, preferred_element_type=F32) + jnp.dot(lo, bd, preferred_element_type=F32)


def _pair_swap(x, even):
    n = x.shape[-1]
    return jnp.where(even, pltpu.roll(x, n - 1, 1), pltpu.roll(x, 1, 1))


FA_TQ = 512


FA_TK = 512


def _cols_from_tokens(x, kv):
    w = ATT_G * ATT_DH
    xt = x[:, kv * w:(kv + 1) * w].T
    return jnp.concatenate([xt[g * ATT_DH:(g + 1) * ATT_DH, :] for g in range(ATT_G)], axis=1)


def _tokens_from_cols(c):
    tq = c.shape[1] // ATT_G
    return jnp.concatenate([c[:, g * tq:(g + 1) * tq] for g in range(ATT_G)], axis=0).T


def _store_cols(ref, x, norm_ref=None):
    for kv in range(ATT_KV):
        cols = _cols_from_tokens(x, kv).astype(BF16)
        ref[kv, 0] = cols
        if norm_ref is not None:
            cf = cols.astype(F32)
            norm_ref[kv, 0] = jnp.sqrt(jnp.sum(cf * cf, axis=0, keepdims=True))


def _att_prep_fwd(U, cos, sin, qw, kw, *, name):
    T = U.shape[0]
    tm = min(FA_TQ, T)
    R = ATT_G * tm
    scale = ATT_DH ** -0.5

    def head_rows(ref, x):
        xt = x.astype(F32).T
        for kv in range(ATT_KV):
            ref[kv, 0] = xt[kv * ATT_DH:(kv + 1) * ATT_DH, :].astype(BF16)

    def body(aq_ref, ak_ref, av_ref, c_ref, s_ref, qw_ref, kw_ref, q_ref, qn_ref, kmax_ref, kc_ref, vc_ref):
        @pl.when(pl.program_id(0) == 0)
        def _():
            kmax_ref[...] = jnp.zeros_like(kmax_ref)

        bd = _head_blockdiag(ATT_QW)
        c2, s2 = c_ref[...], s_ref[...]
        c8, s8 = jnp.tile(c2, (1, 4)), jnp.tile(s2, (1, 4))

        def norm_rope(x, w, c, s, bdm):
            r = lax.rsqrt(_head_sum(x * x, bdm) * (1.0 / ATT_DH) + EPS)
            y = x * r * w
            even = (lax.broadcasted_iota(jnp.int32, y.shape, 1) & 1) == 0
            return y * c + _pair_swap(y, even) * s

        _store_cols(q_ref, norm_rope(aq_ref[...], qw_ref[...], c8, s8, bd) * (scale * LOG2E), qn_ref)
        kb = norm_rope(ak_ref[...], kw_ref[...], c2, s2, bd[:ATT_KW, :ATT_KW]).astype(BF16)
        kf = kb.astype(F32)
        ksq = _head_sum(kf * kf, bd[:ATT_KW, :ATT_KW])
        kmax_ref[...] = jnp.maximum(kmax_ref[...], jnp.max(ksq, axis=0, keepdims=True))
        head_rows(kc_ref, kb)
        head_rows(vc_ref, av_ref[...].astype(BF16))

    kv_spec = pl.BlockSpec((tm, ATT_KW), lambda i: (i, 0))
    tk = min(FA_TK, T)
    per = tk // tm
    c_spec = pl.BlockSpec((ATT_KV, 1, ATT_DH, tm), lambda i: (0, i // per, 0, i % per))
    c_shape = jax.ShapeDtypeStruct((ATT_KV, T // tk, ATT_DH, tk), BF16)
    return pl.pallas_call(
        body, name=name, grid=(T // tm,),
        in_specs=[pl.BlockSpec((tm, ATT_QW), lambda i: (i, 5)),
                  pl.BlockSpec((tm, ATT_KW), lambda i: (i, 24)), pl.BlockSpec((tm, ATT_KW), lambda i: (i, 25)),
                  kv_spec, kv_spec,
                  pl.BlockSpec((1, ATT_QW), lambda i: (0, 0)), pl.BlockSpec((1, ATT_KW), lambda i: (0, 0))],
        out_specs=[pl.BlockSpec((ATT_KV, 1, ATT_DH, R), lambda i: (0, i, 0, 0)),
                   pl.BlockSpec((ATT_KV, 1, 1, R), lambda i: (0, i, 0, 0)), pl.BlockSpec((1, ATT_KW), lambda i: (0, 0)),
                   c_spec, c_spec],
        out_shape=[jax.ShapeDtypeStruct((ATT_KV, T // tm, ATT_DH, R), BF16),
                   jax.ShapeDtypeStruct((ATT_KV, T // tm, 1, R), F32), jax.ShapeDtypeStruct((1, ATT_KW), F32),
                   c_shape, c_shape],
        compiler_params=_params(("arbitrary",)),
    )(U, U, U, cos, sin, qw, kw)


def _att_prep_bwd(U, dq_c, dk_c, dv_c, cos, sin, qw, kw, *, name):
    T = U.shape[0]
    tm = min(FA_TQ, T)
    R = ATT_G * tm
    scale = ATT_DH ** -0.5

    def body(aq_ref, ak_ref, dq_ref, dk_ref, dv_ref, c_ref, s_ref, qw_ref, kw_ref, out_ref, dqw_ref, dkw_ref):
        @pl.when(pl.program_id(0) == 0)
        def _():
            dqw_ref[...] = jnp.zeros_like(dqw_ref)
            dkw_ref[...] = jnp.zeros_like(dkw_ref)

        bd = _head_blockdiag(ATT_QW)
        c2, s2 = c_ref[...], s_ref[...]
        c8, s8 = jnp.tile(c2, (1, 4)), jnp.tile(s2, (1, 4))

        def bwd(x, dy, w, c, s, bdm):
            even = (lax.broadcasted_iota(jnp.int32, x.shape, 1) & 1) == 0
            dn = dy * c - _pair_swap(dy, even) * s
            r = lax.rsqrt(_head_sum(x * x, bdm) * (1.0 / ATT_DH) + EPS)
            xh = x * r
            dxh = dn * w
            t = _head_sum(dxh * xh, bdm) * (1.0 / ATT_DH)
            return r * (dxh - xh * t), jnp.sum(dn * xh, axis=0, keepdims=True)

        dq = jnp.concatenate([_tokens_from_cols(dq_ref[kv, 0]) for kv in range(ATT_KV)], axis=1)
        da, dw = bwd(aq_ref[...], dq * scale, qw_ref[...], c8, s8, bd)
        out_ref[:, 0:ATT_QW] = da.astype(BF16)
        dqw_ref[...] += dw
        tokens = lambda ref: jnp.concatenate([ref[kv, 0] for kv in range(ATT_KV)], axis=0).T
        da, dw = bwd(ak_ref[...], tokens(dk_ref) * (1.0 / LOG2E), kw_ref[...], c2, s2, bd[:ATT_KW, :ATT_KW])
        out_ref[:, ATT_QW:ATT_QW + ATT_KW] = da.astype(BF16)
        dkw_ref[...] += dw
        out_ref[:, ATT_QW + ATT_KW:ATT_QW + 2 * ATT_KW] = tokens(dv_ref).astype(BF16)

    kv_spec = pl.BlockSpec((tm, ATT_KW), lambda i: (i, 0))
    qv = pl.BlockSpec((1, ATT_QW), lambda i: (0, 0))
    kv = pl.BlockSpec((1, ATT_KW), lambda i: (0, 0))
    w_att = ATT_QW + 2 * ATT_KW
    per = dk_c.shape[3] // tm
    c_spec = pl.BlockSpec((ATT_KV, 1, ATT_DH, tm), lambda i: (0, i // per, 0, i % per))
    return pl.pallas_call(
        body, name=name, grid=(T // tm,),
        in_specs=[pl.BlockSpec((tm, ATT_QW), lambda i: (i, 5)), pl.BlockSpec((tm, ATT_KW), lambda i: (i, 24)),
                  pl.BlockSpec((ATT_KV, 1, ATT_DH, R), lambda i: (0, i, 0, 0)), c_spec, c_spec, kv_spec, kv_spec, qv, kv],
        out_specs=[pl.BlockSpec((tm, w_att), lambda i: (i, 0)), qv, kv],
        out_shape=[jax.ShapeDtypeStruct((T, w_att), BF16),
                   jax.ShapeDtypeStruct((1, ATT_QW), F32), jax.ShapeDtypeStruct((1, ATT_KW), F32)],
        compiler_params=_params(("arbitrary",)),
    )(U, U, dq_c, dk_c, dv_c, cos, sin, qw, kw)


def _scores(k_ref, j, qv):
    return lax.dot_general(k_ref[0, j], qv, (((0,), (0,)), ((), ())), preferred_element_type=F32)


def _flash_fwd(q_c, k_c, v_c, *, name):
    _, nq, _, R = q_c.shape
    _, n_k, _, tk = v_c.shape

    def body(q_ref, k_ref, v_ref, o_ref, lse_ref, acc_ref):
        qv = q_ref[0, 0]
        acc_ref[...] = jnp.zeros_like(acc_ref)

        def step(j, carry):
            m, l = carry
            s = _scores(k_ref, j, qv)
            m_new = jnp.maximum(m, jnp.max(s, axis=0, keepdims=True))
            alpha = jnp.exp2(m - m_new)
            p = jnp.exp2(s - m_new)
            l = alpha * l + jnp.sum(p, axis=0, keepdims=True)
            acc_ref[...] = alpha * acc_ref[...] + jnp.dot(v_ref[0, j], p.astype(BF16), preferred_element_type=F32)
            return m_new, l

        m, l = lax.fori_loop(0, n_k, step, (jnp.full((1, R), -jnp.inf, F32), jnp.zeros((1, R), F32)))
        o_ref[0, 0] = acc_ref[...] / l
        lse_ref[0, 0] = m + jnp.log2(l)

    cspec = pl.BlockSpec((1, 1, ATT_DH, R), lambda h, i: (h, i, 0, 0))
    kspec = pl.BlockSpec((1, n_k, ATT_DH, tk), lambda h, i: (h, 0, 0, 0))
    return pl.pallas_call(
        body, name=name, grid=(ATT_KV, nq),
        in_specs=[cspec, kspec, kspec],
        out_specs=[cspec, pl.BlockSpec((1, 1, 1, R), lambda h, i: (h, i, 0, 0))],
        out_shape=[jax.ShapeDtypeStruct((ATT_KV, nq, ATT_DH, R), F32), jax.ShapeDtypeStruct((ATT_KV, nq, 1, R), F32)],
        scratch_shapes=[pltpu.VMEM((ATT_DH, R), F32)],
        compiler_params=_params(("parallel", "parallel")),
    )(q_c, k_c, v_c)


FA_BOUND_MAX = 40.0 * LOG2E


def _flash_fwd_bounded(q_c, k_c, v_c, m_c, *, name):
    _, nq, _, R = q_c.shape
    _, n_k, _, tk = v_c.shape

    def body(q_ref, k_ref, v_ref, m_ref, o_ref, lse_ref, acc_ref):
        qv = q_ref[0, 0]
        m = m_ref[0, 0]
        acc_ref[...] = jnp.zeros_like(acc_ref)

        per = math.gcd(n_k, 4)

        def step(jj, l8):
            pv = None
            for u in range(per):
                j = per * jj + u
                p = jnp.exp2(_scores(k_ref, j, qv) - m)
                l8 = l8 + jnp.sum(p.reshape(tk // 8, 8, R), axis=0)
                d = jnp.dot(v_ref[0, j], p.astype(BF16), preferred_element_type=F32)
                pv = d if pv is None else pv + d
            acc_ref[...] += pv
            return l8

        l8 = lax.fori_loop(0, n_k // per, step, jnp.zeros((8, R), F32))
        l = jnp.sum(l8, axis=0, keepdims=True)
        o_ref[0, 0] = acc_ref[...] / l
        lse_ref[0, 0] = m + jnp.log2(l)

    cspec = pl.BlockSpec((1, 1, ATT_DH, R), lambda h, i: (h, i, 0, 0))
    kspec = pl.BlockSpec((1, n_k, ATT_DH, tk), lambda h, i: (h, 0, 0, 0))
    vspec = pl.BlockSpec((1, 1, 1, R), lambda h, i: (h, i, 0, 0))
    return pl.pallas_call(
        body, name=name, grid=(ATT_KV, nq),
        in_specs=[cspec, kspec, kspec, vspec],
        out_specs=[cspec, vspec],
        out_shape=[jax.ShapeDtypeStruct((ATT_KV, nq, ATT_DH, R), F32), jax.ShapeDtypeStruct((ATT_KV, nq, 1, R), F32)],
        scratch_shapes=[pltpu.VMEM((ATT_DH, R), F32)],
        compiler_params=_params(("parallel", "parallel")),
    )(q_c, k_c, v_c, m_c)


CHIP_MASKS = [(1, 0, 0), (0, 1, 0), (1, 1, 0)]


def _chip_slot(p):
    return 2 * p[0] + p[1]


def _flash_bwd(q_c, k_c, v_c, do_c, lse, delta, *, name, ride=None):
    _, nq, _, R = q_c.shape
    _, n_k, _, tk = k_c.shape
    n_ride = 0 if ride is None else len(ride["arrays"])

    def body(qc_ref, kc_ref, vc_ref, doc_ref, lse_ref, delta_ref, *rest):
        ride_in, rest = rest[:n_ride], rest[n_ride:]
        dq_ref, dk_ref, dv_ref = rest[:3]
        ride_out, rest = rest[3:3 + n_ride], rest[3 + n_ride:]
        acc_ref = rest[0]
        kv = pl.program_id(0)
        if n_ride:
            start, finish = ride["halves"](ride_in, ride_out, *rest[1:])
            pl.when((kv == 0) & (pl.program_id(1) == 0))(start)

        @pl.when(pl.program_id(1) == 0)
        def _():
            dk_ref[...] = jnp.zeros_like(dk_ref)
            dv_ref[...] = jnp.zeros_like(dv_ref)

        qc, doc = qc_ref[0, 0], doc_ref[0, 0]
        lsev, delta = lse_ref[0, 0], delta_ref[0, 0]
        acc_ref[...] = jnp.zeros_like(acc_ref)
        nt = (((1,), (1,)), ((), ()))

        def step(j, carry):
            p = jnp.exp2(_scores(kc_ref, j, qc) - lsev)
            dp = _scores(vc_ref, j, doc)
            ds = (p * (dp - delta)).astype(BF16)
            acc_ref[...] += jnp.dot(kc_ref[0, j], ds, preferred_element_type=F32)
            dk_ref[0, j] += lax.dot_general(qc, ds, nt, preferred_element_type=F32)
            dv_ref[0, j] += lax.dot_general(doc, p.astype(BF16), nt, preferred_element_type=F32)
            return carry

        lax.fori_loop(0, n_k, step, 0, unroll=2)
        dq_ref[0, 0] = acc_ref[...]

        if n_ride:
            pl.when((kv == ATT_KV - 1) & (pl.program_id(1) == nq - 1))(finish)

    cspec = pl.BlockSpec((1, 1, ATT_DH, R), lambda h, i: (h, i, 0, 0))
    vspec = pl.BlockSpec((1, 1, 1, R), lambda h, i: (h, i, 0, 0))
    kspec = pl.BlockSpec((1, n_k, ATT_DH, tk), lambda h, i: (h, 0, 0, 0))
    k_shape = jax.ShapeDtypeStruct(k_c.shape, F32)
    return pl.pallas_call(
        body, name=name, grid=(ATT_KV, nq),
        in_specs=[cspec, kspec, kspec, cspec, vspec, vspec] + [ANY] * n_ride,
        out_specs=[cspec, kspec, kspec] + [ANY] * n_ride,
        out_shape=[jax.ShapeDtypeStruct((ATT_KV, nq, ATT_DH, R), F32), k_shape, k_shape]
                  + (ride["out_shape"] if n_ride else []),
        scratch_shapes=[pltpu.VMEM((ATT_DH, R), F32)] + (ride["scratch"] if n_ride else []),
        compiler_params=pltpu.CompilerParams(dimension_semantics=("arbitrary", "arbitrary"),
                                             vmem_limit_bytes=VMEM_LIMIT, has_side_effects=bool(n_ride)),
    )(q_c, k_c, v_c, do_c, lse, delta, *(ride["arrays"] if n_ride else []))


def _att_post_fwd(o_c, w, *, name):
    _, nq, _, R = o_c.shape
    tm = R // ATT_G
    T = nq * tm

    def body(oc_ref, w_ref, o_ref, out_ref):
        ov = jnp.concatenate([_tokens_from_cols(oc_ref[kv, 0]) for kv in range(ATT_KV)], axis=1)
        r = lax.rsqrt(jnp.mean(ov * ov, axis=-1, keepdims=True) + EPS)
        o_ref[...] = ov
        out_ref[...] = (ov * r * w_ref[...]).astype(BF16)

    tok = pl.BlockSpec((tm, ATT_QW), lambda i: (i, 0))
    return pl.pallas_call(
        body, name=name, grid=(nq,),
        in_specs=[pl.BlockSpec((ATT_KV, 1, ATT_DH, R), lambda i: (0, i, 0, 0)), pl.BlockSpec((1, ATT_QW), lambda i: (0, 0))],
        out_specs=[tok, tok],
        out_shape=[jax.ShapeDtypeStruct((T, ATT_QW), F32), jax.ShapeDtypeStruct((T, ATT_QW), BF16)],
        compiler_params=_params(("parallel",)),
    )(o_c, w)


def _att_post_bwd(dmix, o, w, *, name):
    T = o.shape[0]
    tm = min(FA_TQ, T)
    R = ATT_G * tm

    def body(dm_ref, o_ref, w_ref, do_ref, delta_ref, dw_ref):
        @pl.when(pl.program_id(0) == 0)
        def _():
            dw_ref[...] = jnp.zeros_like(dw_ref)

        ov = o_ref[...]
        r = lax.rsqrt(jnp.mean(ov * ov, axis=-1, keepdims=True) + EPS)
        xh = ov * r
        dm = dm_ref[...]
        dxh = dm * w_ref[...]
        t = jnp.mean(dxh * xh, axis=-1, keepdims=True)
        do = r * (dxh - xh * t)
        _store_cols(do_ref, do)
        dob = do.astype(BF16).astype(F32)
        for kv in range(ATT_KV):
            delta_ref[kv, 0] = jnp.sum(_cols_from_tokens(dob * ov, kv), axis=0, keepdims=True)
        dw_ref[...] += jnp.sum(dm * xh, axis=0, keepdims=True)

    tok = pl.BlockSpec((tm, ATT_QW), lambda i: (i, 0))
    vec = pl.BlockSpec((1, ATT_QW), lambda i: (0, 0))
    return pl.pallas_call(
        body, name=name, grid=(T // tm,),
        in_specs=[pl.BlockSpec((tm, ATT_QW), lambda i: (i, 1)), tok, vec],
        out_specs=[pl.BlockSpec((ATT_KV, 1, ATT_DH, R), lambda i: (0, i, 0, 0)),
                   pl.BlockSpec((ATT_KV, 1, 1, R), lambda i: (0, i, 0, 0)), vec],
        out_shape=[jax.ShapeDtypeStruct((ATT_KV, T // tm, ATT_DH, R), BF16),
                   jax.ShapeDtypeStruct((ATT_KV, T // tm, 1, R), F32), jax.ShapeDtypeStruct((1, ATT_QW), F32)],
        compiler_params=_params(("arbitrary",)),
    )(dmix, o, w)


def _ffn_up(h2, wg_t, wu_t, *, name, tm=512):
    T = h2.shape[0]
    tn = _pick(D_FF, 1408)
    nt = (((1,), (1,)), ((), ()))

    def body(h_ref, wg_ref, wu_ref, g_ref, u_ref, a_ref):
        hv = h_ref[...]
        g = lax.dot_general(hv, wg_ref[...], nt, preferred_element_type=F32)
        u = lax.dot_general(hv, wu_ref[...], nt, preferred_element_type=F32)
        g_ref[...] = g.astype(BF16)
        u_ref[...] = u.astype(BF16)
        a_ref[...] = (g * _sigmoid(g) * u).astype(BF16)

    wspec = pl.BlockSpec((tn, D_MODEL), lambda i, j: (j, 0))
    ospec = pl.BlockSpec((tm, tn), lambda i, j: (i, j))
    return pl.pallas_call(
        body, name=name, grid=(T // tm, D_FF // tn),
        in_specs=[pl.BlockSpec((tm, D_MODEL), lambda i, j: (i, 0)), wspec, wspec],
        out_specs=[ospec] * 3, out_shape=[jax.ShapeDtypeStruct((T, D_FF), BF16)] * 3,
        compiler_params=_params(("parallel", "arbitrary")),
    )(h2, wg_t, wu_t)


def _ffn_act_bwd(dx2b, w_down, gate, up, *, name, tm=512):
    T = dx2b.shape[0]
    tn = _pick(D_FF, 1408)

    def body(dx_ref, w_ref, g_ref, u_ref, dg_ref, du_ref):
        da = lax.dot_general(dx_ref[...], w_ref[...], (((1,), (1,)), ((), ())), preferred_element_type=F32)
        g = g_ref[...].astype(F32)
        u = u_ref[...].astype(F32)
        sg = _sigmoid(g)
        dg_ref[...] = (da * u * (sg * (1.0 + g * (1.0 - sg)))).astype(BF16)
        du_ref[...] = (da * (g * sg)).astype(BF16)

    ospec = pl.BlockSpec((tm, tn), lambda i, j: (i, j))
    return pl.pallas_call(
        body, name=name, grid=(T // tm, D_FF // tn),
        in_specs=[pl.BlockSpec((tm, D_MODEL), lambda i, j: (i, 0)),
                  pl.BlockSpec((tn, D_MODEL), lambda i, j: (j, 0)), ospec, ospec],
        out_specs=[ospec] * 2, out_shape=[jax.ShapeDtypeStruct((T, D_FF), BF16)] * 2,
        compiler_params=_params(("parallel", "arbitrary")),
    )(dx2b, w_down, gate, up)


def _adam_math(w, g, m, v):
    m = ADAM_B1 * m + (1.0 - ADAM_B1) * g
    v = ADAM_B2 * v + (1.0 - ADAM_B2) * (g * g)
    m_hat = m / (1.0 - ADAM_B1 ** ADAM_STEP)
    v_hat = v / (1.0 - ADAM_B2 ** ADAM_STEP)
    delta = -ADAM_LR * (m_hat / (jnp.sqrt(v_hat) + ADAM_EPS) + ADAM_WD * w)
    return delta, m, v


def _adamw(parts, w, m, v, *, name, tr_cap=256):
    P, R, C = parts.shape
    tr = R
    for t in range(8, min(R, tr_cap) + 1, 8):
        if R % t == 0:
            tr = t

    def body(p_ref, w_ref, m_ref, v_ref, g_ref, d_ref, nm_ref, nv_ref):
        g = p_ref[0].astype(F32)
        for j in range(1, P):
            g = g + p_ref[j].astype(F32)
        d, nm, nv = _adam_math(w_ref[...], g, m_ref[...], v_ref[...])
        g_ref[...] = g
        d_ref[...] = d
        nm_ref[...] = nm
        nv_ref[...] = nv

    blk = pl.BlockSpec((tr, C), lambda i: (i, 0))
    return pl.pallas_call(
        body, name=name, grid=(R // tr,),
        in_specs=[pl.BlockSpec((P, tr, C), lambda i: (0, i, 0)), blk, blk, blk],
        out_specs=[blk] * 4, out_shape=[jax.ShapeDtypeStruct((R, C), F32)] * 4,
        compiler_params=_params(("parallel",)),
    )(parts, w, m, v)


def _gather_halves(ins, outs, send_sems, recv_sems, local_sems):
    n = len(ins)
    x, y, c = lax.axis_index("x"), lax.axis_index("y"), lax.axis_index("c")
    me, sibling = (x, y, c), (x, y, 1 - c)
    chips = [(1 - x, y), (x, 1 - y), (1 - x, 1 - y)]

    def slot(p):
        return 4 * p[0] + 2 * p[1] + p[2]

    def copy(a, k, block, to, src=None):
        dst = outs[a].at[slot(block)]
        return pltpu.make_async_remote_copy(
            src_ref=dst if src is None else src, dst_ref=dst,
            send_sem=send_sems.at[a * 7 + k], recv_sem=recv_sems.at[a * 7 + k],
            device_id=to, device_id_type=MESH)

    mine = [pltpu.make_async_copy(ins[a], outs[a].at[slot(me)], local_sems.at[a]) for a in range(n)]
    first = []
    for a in range(n):
        first.append(copy(a, 0, me, sibling, src=ins[a]))
        first += [copy(a, 1 + j, me, (*chip, c), src=ins[a]) for j, chip in enumerate(chips)]

    def start():
        for cp in mine + first:
            cp.start()

    def finish():
        passed = []
        for j, chip in enumerate(chips):
            for a in range(n):
                copy(a, 1 + j, (*chip, c), me).wait_recv()
                cp = copy(a, 4 + j, (*chip, c), sibling)
                cp.start()
                passed.append(cp)
        for a in range(n):
            copy(a, 0, sibling, me).wait_recv()
            for j, chip in enumerate(chips):
                copy(a, 4 + j, (*chip, 1 - c), me).wait_recv()
        for cp in first + passed:
            cp.wait_send()
        for cp in mine:
            cp.wait()

    return start, finish


def _gather_scratch(n):
    return [pltpu.SemaphoreType.DMA((7 * n,)), pltpu.SemaphoreType.DMA((7 * n,)), pltpu.SemaphoreType.DMA((n,))]


def _gathered_shapes(xs):
    return [jax.ShapeDtypeStruct((N_DEV,) + x.shape, x.dtype) for x in xs]


def _ride_gather(xs):
    xs = list(xs)
    return dict(arrays=xs, out_shape=_gathered_shapes(xs), scratch=_gather_scratch(len(xs)), halves=_gather_halves)


def _ride_chips(gs):
    gs = list(gs)
    n = len(gs)

    def halves(ins, outs, send_sems, recv_sems, local_sems):
        mine, copies = _exchange_copies(ins, outs, send_sems, recv_sems, local_sems, masks=CHIP_MASKS, slot=_chip_slot)

        def start():
            for cp in mine:
                cp.start()
            for send, _ in copies:
                send.start()

        def finish():
            for send, recv in copies:
                recv.wait_recv()
                send.wait_send()
            for cp in mine:
                cp.wait()

        return start, finish

    n_sem = len(CHIP_MASKS) * n
    return dict(arrays=gs, out_shape=[jax.ShapeDtypeStruct(g.shape, g.dtype) for g in gs], halves=halves,
                scratch=[pltpu.SemaphoreType.DMA((n_sem,)), pltpu.SemaphoreType.DMA((n_sem,)),
                         pltpu.SemaphoreType.DMA((n,))])


ALL_MASKS = [(mx, my, mc) for mx in (0, 1) for my in (0, 1) for mc in (0, 1)][1:]


def _flip(v, bit):
    return 1 - v if bit else v


def _exchange_copies(ins, outs, send_sems, recv_sems, local_sems, *, masks, slot):
    n, n_peers = len(ins), len(masks)
    x, y, c = lax.axis_index("x"), lax.axis_index("y"), lax.axis_index("c")
    my_slot = slot((x, y, c))
    mine = [pltpu.make_async_copy(ins[a].at[my_slot], outs[a].at[my_slot], local_sems.at[a]) for a in range(n)]
    copies = []
    for a in range(n):
        for k, (mx, my, mc) in enumerate(masks):
            peer = (_flip(x, mx), _flip(y, my), _flip(c, mc))
            peer_slot = slot(peer)
            sems = dict(send_sem=send_sems.at[a * n_peers + k], recv_sem=recv_sems.at[a * n_peers + k],
                        device_id=peer, device_id_type=MESH)
            copies.append((
                pltpu.make_async_remote_copy(src_ref=ins[a].at[peer_slot], dst_ref=outs[a].at[my_slot], **sems),
                pltpu.make_async_remote_copy(src_ref=ins[a].at[peer_slot], dst_ref=outs[a].at[peer_slot], **sems)))
    return mine, copies


def _send_to_all(v, *, name):
    def body(v_ref, out_ref, send_sems, recv_sems, local_sem):
        x, y, c = lax.axis_index("x"), lax.axis_index("y"), lax.axis_index("c")
        me = 4 * x + 2 * y + c
        mine = pltpu.make_async_copy(v_ref, out_ref.at[me], local_sem)
        mine.start()
        copies = []
        for k, (mx, my, mc) in enumerate(ALL_MASKS):
            peer = (_flip(x, mx), _flip(y, my), _flip(c, mc))
            peer_id = 4 * peer[0] + 2 * peer[1] + peer[2]
            sems = dict(send_sem=send_sems.at[k], recv_sem=recv_sems.at[k], device_id=peer, device_id_type=MESH)
            copies.append((pltpu.make_async_remote_copy(src_ref=v_ref, dst_ref=out_ref.at[me], **sems),
                           pltpu.make_async_remote_copy(src_ref=v_ref, dst_ref=out_ref.at[peer_id], **sems)))
        for send, _ in copies:
            send.start()
        for send, recv in copies:
            recv.wait_recv()
            send.wait_send()
        mine.wait()

    n_peers = len(ALL_MASKS)
    return pl.pallas_call(
        body, name=name, in_specs=[ANY], out_specs=ANY,
        out_shape=jax.ShapeDtypeStruct((N_DEV,) + v.shape, v.dtype),
        scratch_shapes=[pltpu.SemaphoreType.DMA((n_peers,)), pltpu.SemaphoreType.DMA((n_peers,)),
                        pltpu.SemaphoreType.DMA],
        compiler_params=pltpu.CompilerParams(has_side_effects=True),
    )(v)


SWAP_ROW_CHUNKS = 4


def _ride_swap(gs):
    gs = list(gs)
    n = len(gs)

    def halves(ins, outs, send_sems, recv_sems):
        x, y, c = lax.axis_index("x"), lax.axis_index("y"), lax.axis_index("c")
        sibling = dict(device_id=(x, y, 1 - c), device_id_type=MESH)

        def start():
            for a in range(n):
                Q, _, R, _ = ins[a].shape
                rows = R // SWAP_ROW_CHUNKS
                for q in range(Q):
                    for j in range(SWAP_ROW_CHUNKS):
                        part = pl.ds(j * rows, rows)
                        pltpu.make_async_remote_copy(src_ref=ins[a].at[q, 1 - c, part], dst_ref=outs[a].at[q, part],
                                                     send_sem=send_sems.at[a], recv_sem=recv_sems.at[a], **sibling).start()

        def finish():
            for a in range(n):
                pltpu.make_async_remote_copy(src_ref=outs[a], dst_ref=outs[a], send_sem=send_sems.at[a],
                                             recv_sem=recv_sems.at[a], **sibling).wait()

        return start, finish

    return dict(arrays=gs, out_shape=[jax.ShapeDtypeStruct(g.shape[:1] + g.shape[2:], g.dtype) for g in gs],
                scratch=[pltpu.SemaphoreType.DMA((n,)), pltpu.SemaphoreType.DMA((n,))], halves=halves)


def _core_swap(gs, *, name):
    ride = _ride_swap(gs)
    n = len(gs)

    def body(*refs):
        start, finish = ride["halves"](refs[:n], refs[n:2 * n], *refs[2 * n:])
        start()
        finish()

    return pl.pallas_call(
        body, name=name, in_specs=[ANY] * n, out_specs=[ANY] * n, out_shape=ride["out_shape"],
        scratch_shapes=ride["scratch"], compiler_params=pltpu.CompilerParams(has_side_effects=True),
    )(*gs)


def _pair_sum(g, other, core, *, name, tr_cap=256):
    Q, _, R, C = g.shape
    tr = max(t for t in range(16, min(R, tr_cap) + 1, 16) if R % t == 0)

    def body(core_ref, g_ref, o_ref, out_ref):
        out_ref[0] = (g_ref[0, 0] + o_ref[0]).astype(BF16)

    return pl.pallas_call(
        body, name=name,
        grid_spec=pltpu.PrefetchScalarGridSpec(
            num_scalar_prefetch=1, grid=(Q, R // tr),
            in_specs=[pl.BlockSpec((1, 1, tr, C), lambda q, i, core_ref: (q, core_ref[0], i, 0)),
                      pl.BlockSpec((1, tr, C), lambda q, i, core_ref: (q, i, 0))],
            out_specs=pl.BlockSpec((1, tr, C), lambda q, i, core_ref: (q, i, 0))),
        out_shape=jax.ShapeDtypeStruct((Q, R, C), BF16),
        compiler_params=_params(("parallel", "parallel")),
    )(core, g, other)


def _pack_small(norm1, norm2, final, att, hg, qn, kn, lb=None, loss=None):
    z = lambda n: jnp.zeros((n,), F32)
    rows = [norm1.reshape(-1), norm2.reshape(-1), final.reshape(-1),
            jnp.concatenate([att.reshape(-1), z(512)]),
            jnp.concatenate([hg.reshape(-1), qn.reshape(-1), kn.reshape(-1), z(1024 - 256)]),
            z(1024) if lb is None else lb.reshape(-1),
            z(1024) if loss is None else jnp.concatenate([loss.reshape(-1), z(1023)]), z(1024)]
    return jnp.stack(rows, axis=0)


def _unpack_small(p):
    return (p[0:1, :], p[1:2, :], p[2, :], p[3:4, 0:512], p[4:5, 0:128], p[4:5, 128:192], p[4:5, 192:256])


def _fold_heads(dhg, dqn, dkn, *, name):
    def body(hg_ref, q_ref, k_ref, ohg_ref, oq_ref, ok_ref):
        def fold128(v):
            acc = v[:, 0:LANES]
            for j in range(1, v.shape[1] // LANES):
                acc = acc + v[:, j * LANES:(j + 1) * LANES]
            return acc

        ohg_ref[...] = fold128(hg_ref[...])
        q = fold128(q_ref[...])
        oq_ref[...] = q + pltpu.roll(q, ATT_DH, 1)
        k = k_ref[...]
        ok_ref[...] = k + pltpu.roll(k, ATT_DH, 1)

    return pl.pallas_call(body, name=name, out_shape=[jax.ShapeDtypeStruct((1, LANES), F32)] * 3)(dhg, dqn, dkn)


def _lb_grad(dlb_sum, lb, *, name):
    def body(d_ref, lb_ref, o_ref):
        lbv = lb_ref[...]
        gl = d_ref[...] * lbv * (1.0 - lbv)
        o_ref[0:1, :] = gl[0:1, :]
        o_ref[1:2, :] = -gl[0:1, :]
        o_ref[2:3, :] = gl[1:2, :]
        o_ref[3:4, :] = -gl[1:2, :]

    return pl.pallas_call(body, name=name, out_shape=jax.ShapeDtypeStruct((4, HG_W), F32))(dlb_sum, lb)


def _lower_bounds(lb_logits_full, *, name):
    def body(l_ref, o_ref):
        for d in range(2):
            l0, l1 = l_ref[2 * d:2 * d + 1, :], l_ref[2 * d + 1:2 * d + 2, :]
            mx = jnp.maximum(l0, l1)
            e0, e1 = jnp.exp(l0 - mx), jnp.exp(l1 - mx)
            o_ref[d:d + 1, :] = e0 / (e0 + e1)

    return pl.pallas_call(body, name=name, out_shape=jax.ShapeDtypeStruct((2, HG_W), F32))(
        lb_logits_full.reshape(4, HG_W))


def _local_step(x, target, norm1_w, w_in_t, lb, hg_norm_w, q_norm_w, k_norm_w, att_norm_w, w_out, norm2_w,
                w_g_t, w_u_t, w_down, final_norm_w, reduce_early=None, reduce_late=None, shards=None):
    T = x.shape[0]
    cos, sin = _rope_tables(T)
    qw8 = jnp.tile(q_norm_w, (1, ATT_HEADS))
    kw2 = jnp.tile(k_norm_w, (1, ATT_KV))

    if shards is None:
        h, r1 = _rms_fwd(x, norm1_w, name="norm1_fwd")
        U = _mm_nn([(h, w_in_t)], trans_b=True, name="in_proj")
        o_f, st_f = _gla_fwd(U, lb[0:1], f_block=1, reverse=False, name="gla_fwd_f")
    else:
        h, r1, g_in, g_lb = _rms_fwd(x, norm1_w, ride=_ride_gather([shards["w_in_t"], shards["lb_logits"]]),
                                     name="norm1_fwd")
        w_in_t = g_in.reshape(-1, D_MODEL)
        lb = _lower_bounds(g_lb.transpose(1, 0, 2).reshape(2, 2, -1), name="lower_bounds")
        U, g_gu = _mm_nn([(h, w_in_t)], trans_b=True, ride=_ride_gather([shards["w_gu_t"]]), name="in_proj")
        o_f, st_f, g_out, g_dn = _gla_fwd(U, lb[0:1], f_block=1, reverse=False,
                                          ride=_ride_gather([shards["w_out"], shards["w_down"]]), name="gla_fwd_f")
        g_gu = g_gu.reshape(2, -1, D_MODEL)
        w_g_t, w_u_t = g_gu[0], g_gu[1]
        w_out, w_down = g_out.reshape(-1, D_MODEL), g_dn.reshape(-1, D_MODEL)
    o_b, st_b = _gla_fwd(U, lb[1:2], f_block=2, reverse=True, name="gla_fwd_b")
    mix_hg = _hg_post_fwd(o_f, o_b, U, hg_norm_w, name="hg_post_fwd")
    q_c, qn_c, kmax2, k_c, v_c = _att_prep_fwd(U, cos, sin, qw8, kw2, name="att_prep_fwd")
    kmax = jnp.sqrt(jnp.max(kmax2.reshape(ATT_KV, ATT_DH), axis=1))
    m_c = qn_c * (kmax * 1.001).reshape(ATT_KV, 1, 1, 1)
    o_c, lse = lax.cond(jnp.max(m_c) <= FA_BOUND_MAX,
                        lambda: _flash_fwd_bounded(q_c, k_c, v_c, m_c, name="flash_fwd_bounded"),
                        lambda: _flash_fwd(q_c, k_c, v_c, name="flash_fwd"))
    o_att, mix_att = _att_post_fwd(o_c, att_norm_w, name="att_post_fwd")
    x1, h2, r2 = _mm_nn([(mix_hg, w_out[:HG_W]), (mix_att, w_out[HG_W:])], residual=x, tail=_tail_rms_fwd(norm2_w),
                        name="out_proj")
    gate, up, act = _ffn_up(h2, w_g_t, w_u_t, name="ffn_up")
    loss, dx2, dx2b, d_final = _mm_nn([(act, w_down)], residual=x1,
                                      tail=_tail_loss(target, final_norm_w.reshape(1, D_MODEL)), name="ffn_down")

    d_gate, d_up = _ffn_act_bwd(dx2b, w_down, gate, up, name="ffn_act_bwd")
    dw_down = _mm_tn(act, dx2b, tma_cap=1408, name="dw_down")
    dw_g = _mm_tn(d_gate, h2, tma_cap=1408, name="dw_gate")
    dw_u = _mm_tn(d_up, h2, tma_cap=1408, name="dw_up")
    mine = None if reduce_early is None else reduce_early["slabs"](dw_g, dw_u, dw_down)
    dx1, dx1b, d_norm2, *theirs = _mm_nn([(d_gate, w_g_t), (d_up, w_u_t)], tm=256,
                                         ride=None if mine is None else _ride_swap(mine),
                                         tail=_tail_rms_bwd(x1, r2, norm2_w, dx2, emit_bf16=True), name="ffn_up_bwd")
    dmix = _mm_nn([(dx1b, w_out)], trans_b=True, name="out_proj_bwd")
    dw_out = _mm_tn(mix_att, dx1b, rows=(HG_W, D_MODEL), name="dw_out_att",
                    into=_mm_tn(mix_hg, dx1b, rows=(0, D_MODEL), name="dw_out_hg"))
    do_c, delta, d_att = _att_post_bwd(dmix, o_att, att_norm_w, name="att_post_bwd")
    ride = None if reduce_early is None else _ride_chips(reduce_early["sums"](mine, theirs, dw_out))
    dq_c, dk_c, dv_c, *rode = _flash_bwd(q_c, k_c, v_c, do_c, lse, delta, ride=ride, name="flash_bwd")
    dU_att, d_qn, d_kn = _att_prep_bwd(U, dq_c, dk_c, dv_c, cos, sin, qw8, kw2, name="att_prep_bwd")
    do_hg, du_g, d_hg = _hg_post_bwd(dmix, o_f, o_b, U, hg_norm_w, name="hg_post_bwd")
    dq_f, dz_f, dv_f, dlb_f = _gla_bwd(U, lb[0:1], do_hg, st_f, f_block=1, reverse=False, name="gla_bwd_f")
    dU_hg, dlb_b = _gla_bwd(U, lb[1:2], do_hg, st_b, f_block=2, reverse=True, prev=(dq_f, dz_f, dv_f, du_g),
                            name="gla_bwd_b")
    w_hg = 5 * HG_W
    n_in = w_hg + dU_att.shape[1]
    dw_in = _mm_tn(dU_att, h, tma_cap=256, rows=(w_hg, n_in), name="dw_in_att",
                   into=_mm_tn(dU_hg, h, tma_cap=1280, rows=(0, n_in), name="dw_in_hg"))
    late = None if reduce_late is None else _ride_chips(reduce_late(dw_in))
    grad_x, d_norm1, *rode_late = _mm_nn([(dU_hg, w_in_t[:w_hg]), (dU_att, w_in_t[w_hg:])], ride=late,
                                         tail=_tail_rms_bwd(x, r1, norm1_w, dx1, emit_bf16=False), name="in_proj_bwd")
    d_hg, d_qn, d_kn = _fold_heads(d_hg, d_qn, d_kn, name="fold_heads")

    big = dict(w_in=dw_in, w_out=dw_out, w_g=dw_g, w_u=dw_u, w_down=dw_down)
    small = dict(norm1=d_norm1, norm2=d_norm2, final=d_final, att=d_att, hg=d_hg,
                 qn=d_qn[:, :ATT_DH], kn=d_kn[:, :ATT_DH], lb=jnp.concatenate([dlb_f, dlb_b], axis=0))
    return loss, grad_x, big, small, rode + rode_late, lb


def kernel(x, norm1_w, w_in, lb_logits, hg_norm_w, q_norm_w, k_norm_w, att_norm_w, w_out, norm2_w, w_gate_up, w_down, final_norm_w, loss_target, m_norm1_w, m_w_in, m_lb_logits, m_hg_norm_w, m_q_norm_w, m_k_norm_w, m_att_norm_w, m_w_out, m_norm2_w, m_w_gate_up, m_w_down, m_final_norm_w, v_norm1_w, v_w_in, v_lb_logits, v_hg_norm_w, v_q_norm_w, v_k_norm_w, v_att_norm_w, v_w_out, v_norm2_w, v_w_gate_up, v_w_down, v_final_norm_w):
    T = x.shape[1]
    me = 4 * lax.axis_index("x") + 2 * lax.axis_index("y") + lax.axis_index("c")
    c_in, r_out, c_gu, r_dn = w_in.shape[2], w_out.shape[1], w_gate_up.shape[2], w_down.shape[1]
    lb_cols = lb_logits.shape[2]

    shards = dict(w_in_t=w_in[0].T.astype(BF16), lb_logits=lb_logits.reshape(4, lb_cols),
                  w_gu_t=w_gate_up[0].T.astype(BF16), w_out=w_out[0].astype(BF16), w_down=w_down[0].astype(BF16))

    chips = N_DEV // 2
    core = lax.axis_index("c").astype(jnp.int32).reshape(1)
    by_owner = lambda g, r: g.reshape(chips, 2, r, D_MODEL)

    def pair_sums(mine, theirs, names):
        return [_pair_sum(g, o, core, name="pair_sum_" + nm) for g, o, nm in zip(mine, theirs, names)]

    def early_slabs(dw_g_t, dw_u_t, dw_down):
        half = lambda g: g.reshape(chips // 2, 2, c_gu, D_MODEL)
        return [half(dw_g_t), half(dw_u_t), by_owner(dw_down, r_dn)]

    def early_sums(mine, theirs, dw_out):
        s_out = by_owner(dw_out, r_out)
        c_out, c_g, c_u, c_dn = pair_sums([s_out] + mine, list(_core_swap([s_out], name="exchange_cores_out"))
                                          + list(theirs), ("w_out", "w_gate", "w_up", "w_down"))
        return [c_out, jnp.concatenate([c_g, c_u], axis=0), c_dn]

    def reduce_late(dw_in_t):
        mine = [by_owner(dw_in_t, c_in)]
        return pair_sums(mine, _core_swap(mine, name="exchange_cores_in"), ("w_in",))

    loss, grad_x, big, small, (p_out, p_gu, p_dn, p_in), lb = _local_step(
        x[0], loss_target[0], norm1_w, None, None, hg_norm_w, q_norm_w, k_norm_w, att_norm_w, None, norm2_w,
        None, None, None, final_norm_w, reduce_early=dict(slabs=early_slabs, sums=early_sums),
        reduce_late=reduce_late, shards=shards)
    p_gu, p_in = p_gu.transpose(0, 2, 1), p_in.transpose(0, 2, 1)

    packed = _pack_small(small["norm1"], small["norm2"], small["final"], small["att"], small["hg"],
                         small["qn"], small["kn"], small["lb"], loss)
    all_small = _send_to_all(packed, name="exchange_small")

    g_w_in, d_w_in, nm_w_in, nv_w_in = _adamw(p_in, w_in[0], m_w_in[0], v_w_in[0], name="adamw_w_in")
    g_w_out, d_w_out, nm_w_out, nv_w_out = _adamw(p_out, w_out[0], m_w_out[0], v_w_out[0], name="adamw_w_out")
    g_w_gu, d_w_gu, nm_w_gu, nv_w_gu = _adamw(p_gu, w_gate_up[0], m_w_gate_up[0], v_w_gate_up[0], name="adamw_w_gu")
    g_w_dn, d_w_dn, nm_w_dn, nv_w_dn = _adamw(p_dn, w_down[0], m_w_down[0], v_w_down[0], name="adamw_w_down")

    pk = lambda vecs: _pack_small(*vecs)
    w_pk = pk([norm1_w, norm2_w, final_norm_w, att_norm_w, hg_norm_w, q_norm_w, k_norm_w])
    m_pk = pk([m_norm1_w, m_norm2_w, m_final_norm_w, m_att_norm_w, m_hg_norm_w, m_q_norm_w, m_k_norm_w])
    v_pk = pk([v_norm1_w, v_norm2_w, v_final_norm_w, v_att_norm_w, v_hg_norm_w, v_q_norm_w, v_k_norm_w])
    g_pk, d_pk, nm_pk, nv_pk = _adamw(all_small, w_pk, m_pk, v_pk, name="adamw_small")

    dlb_sum = g_pk[5:6, :].reshape(2, HG_W)
    g_lb_full = _lb_grad(dlb_sum, lb, name="lb_grad")
    g_lb_mine = lax.dynamic_slice_in_dim(g_lb_full, me * lb_cols, lb_cols, axis=1)
    g_lb_s, d_lb, nm_lb, nv_lb = _adamw(g_lb_mine[None], lb_logits.reshape(4, lb_cols),
                                        m_lb_logits.reshape(4, lb_cols), v_lb_logits.reshape(4, lb_cols),
                                        name="adamw_lb")

    loss_total = g_pk[6, 0]

    def outs(big4, lb_arr, pk_arr):
        n1, n2, fin, att, hg, qn, kn = _unpack_small(pk_arr)
        b_in, b_out, b_gu, b_dn = big4
        return [n1, b_in[None], lb_arr.reshape(2, 2, lb_cols), hg, qn, kn, att, b_out[None], n2, b_gu[None],
                b_dn[None], fin]

    return (loss_total, grad_x[None],
            *outs((g_w_in, g_w_out, g_w_gu, g_w_dn), g_lb_s, g_pk),
            *outs((d_w_in, d_w_out, d_w_gu, d_w_dn), d_lb, d_pk),
            *outs((nm_w_in, nm_w_out, nm_w_gu, nm_w_dn), nm_lb, nm_pk),
            *outs((nv_w_in, nv_w_out, nv_w_gu, nv_w_dn), nv_lb, nv_pk))
```

```python
import math

import jax
import jax.numpy as jnp
import numpy as np
from jax import lax
from jax.experimental import pallas as pl
from jax.experimental.pallas import tpu as pltpu

F32 = jnp.float32
BF16 = jnp.bfloat16

N_DEV = 8
D_MODEL = 1024
EPS = 1e-6
HG_HEADS = 4
HG_D = 128
HG_W = HG_HEADS * HG_D
CHUNK = 64
ATT_HEADS = 8
ATT_KV = 2
ATT_G = ATT_HEADS // ATT_KV
ATT_DH = 64
ATT_QW = ATT_HEADS * ATT_DH
ATT_KW = ATT_KV * ATT_DH
GRID_W = 64
ROPE_THETA = 10000.0
D_FF = 2816
ADAM_LR, ADAM_B1, ADAM_B2, ADAM_EPS, ADAM_WD, ADAM_STEP = 0.001, 0.9, 0.999, 1e-08, 0.01, 10

LOG2E = math.log2(math.e)
LANES = 128
VMEM_LIMIT = 48 * 1024 * 1024
MESH = pl.DeviceIdType.MESH
ANY = pl.BlockSpec(memory_space=pl.ANY)


def _params(sem=None):
    return pltpu.CompilerParams(dimension_semantics=sem, vmem_limit_bytes=VMEM_LIMIT)


def _pick(n, cap):
    best = None
    for t in range(LANES, cap + 1, LANES):
        if n % t == 0:
            best = t
    assert best is not None, (n, cap)
    return best


def _sigmoid(x):
    return 1.0 / (1.0 + jnp.exp(-x))


def _dot(a, b):
    return jnp.dot(a.astype(BF16), b.astype(BF16), preferred_element_type=F32)


def _dot_nt(a, b):
    return lax.dot_general(a.astype(BF16), b.astype(BF16), (((1,), (1,)), ((), ())),
                           preferred_element_type=F32)


def _dot_tn(a, b):
    return lax.dot_general(a.astype(BF16), b.astype(BF16), (((0,), (0,)), ((), ())),
                           preferred_element_type=F32)


def _mm_nn(pairs, *, name, out_dtype=F32, residual=None, tm=512, tn_cap=None, trans_b=False, tail=None, ride=None):
    M = pairs[0][0].shape[0]
    N = pairs[0][1].shape[0 if trans_b else 1]
    tn = N if tn_cap is None else _pick(N, tn_cap)
    n_pairs = len(pairs)
    has_res = residual is not None
    dims = (((1,), (1,)), ((), ())) if trans_b else (((1,), (0,)), ((), ()))
    assert (tail is None and ride is None) or tn == N
    n_main = 2 * n_pairs + has_res
    n_ti = 0 if tail is None else len(tail["ins"])
    n_out = 1 if tail is None else len(tail["outs"])
    n_r = 0 if ride is None else len(ride["arrays"])
    n_in = n_main + n_ti + n_r

    def body(*refs):
        outs = refs[n_in:n_in + n_out]
        if n_r:
            start, finish = ride["halves"](refs[n_main + n_ti:n_in], refs[n_in + n_out:n_in + n_out + n_r],
                                           *refs[n_in + n_out + n_r:])
            pl.when(pl.program_id(0) == 0)(start)
        acc = None
        for i in range(n_pairs):
            d = lax.dot_general(refs[2 * i][...], refs[2 * i + 1][...], dims, preferred_element_type=F32)
            acc = d if acc is None else acc + d
        if has_res:
            acc = acc + refs[2 * n_pairs][...]
        if tail is None:
            outs[0][...] = acc.astype(out_dtype)
        else:
            tail["fn"](acc, pl.program_id(0) == 0, *refs[n_main:n_main + n_ti], *outs)
        if n_r:
            pl.when(pl.program_id(0) == M // tm - 1)(finish)

    kinds = {"row": ((tm, N), (M, N), lambda i, j: (i, 0)), "col": ((tm, 1), (M, 1), lambda i, j: (i, 0)),
             "vec": ((1, N), (1, N), lambda i, j: (0, 0)), "one": ((1, 1), (1, 1), lambda i, j: (0, 0))}
    in_specs, args = [], []
    for a, b in pairs:
        k = a.shape[1]
        b_spec = pl.BlockSpec((tn, k), lambda i, j: (j, 0)) if trans_b else pl.BlockSpec((k, tn), lambda i, j: (0, j))
        in_specs += [pl.BlockSpec((tm, k), lambda i, j: (i, 0)), b_spec]
        args += [a, b]
    if has_res:
        in_specs.append(pl.BlockSpec((tm, tn), lambda i, j: (i, j)))
        args.append(residual)
    if tail is None:
        out_specs = [pl.BlockSpec((tm, tn), lambda i, j: (i, j))]
        out_shape = [jax.ShapeDtypeStruct((M, N), out_dtype)]
    else:
        for arr, kind in tail["ins"]:
            in_specs.append(pl.BlockSpec(kinds[kind][0], kinds[kind][2]))
            args.append(arr)
        out_specs = [pl.BlockSpec(kinds[kind][0], kinds[kind][2]) for _, kind in tail["outs"]]
        out_shape = [jax.ShapeDtypeStruct(kinds[kind][1], dt) for dt, kind in tail["outs"]]
    scratch = []
    if n_r:
        in_specs += [ANY] * n_r
        args += ride["arrays"]
        out_specs += [ANY] * n_r
        out_shape += ride["out_shape"]
        scratch = ride["scratch"]
    sequential = tail is not None or n_r > 0
    res = pl.pallas_call(
        body, name=name, grid=(M // tm, N // tn), in_specs=in_specs, out_specs=out_specs, out_shape=out_shape,
        scratch_shapes=scratch,
        compiler_params=pltpu.CompilerParams(dimension_semantics=("arbitrary" if sequential else "parallel", "arbitrary"),
                                             vmem_limit_bytes=VMEM_LIMIT, has_side_effects=n_r > 0),
    )(*args)
    return res[0] if len(res) == 1 else res


def _mm_tn(a, b, *, name, tma_cap=1024, tnb_cap=1024, tk=1024, rows=None, into=None):
    T, Ma = a.shape
    Nb = b.shape[1]
    tma, tnb = _pick(Ma, tma_cap), _pick(Nb, tnb_cap)
    tk = min(tk, T)
    n_k = T // tk
    first_row, total = (0, Ma) if rows is None else rows
    assert first_row % tma == 0
    i0 = first_row // tma

    def body(a_ref, b_ref, *rest):
        o_ref, acc_ref = rest[-2:]
        k = pl.program_id(2)

        @pl.when(k == 0)
        def _():
            acc_ref[...] = jnp.zeros_like(acc_ref)

        acc_ref[...] += lax.dot_general(a_ref[...], b_ref[...], (((0,), (0,)), ((), ())),
                                        preferred_element_type=F32)

        @pl.when(k == n_k - 1)
        def _():
            o_ref[...] = acc_ref[...]

    in_specs = [pl.BlockSpec((tk, tma), lambda i, j, k: (k, i)), pl.BlockSpec((tk, tnb), lambda i, j, k: (k, j))]
    args = [a, b]
    if into is not None:
        in_specs.append(ANY)
        args.append(into)
    return pl.pallas_call(
        body, name=name, grid=(Ma // tma, Nb // tnb, n_k), in_specs=in_specs,
        out_specs=pl.BlockSpec((tma, tnb), lambda i, j, k: (i0 + i, j)),
        out_shape=jax.ShapeDtypeStruct((total, Nb), F32),
        scratch_shapes=[pltpu.VMEM((tma, tnb), F32)],
        input_output_aliases={} if into is None else {2: 0},
        compiler_params=_params(("parallel", "parallel", "arbitrary")),
    )(*args)


def _rms_fwd(x, w, *, name, tm=512, ride=None):
    T, Dm = x.shape
    n_r = 0 if ride is None else len(ride["arrays"])

    def body(x_ref, w_ref, *rest):
        h_ref, r_ref = rest[n_r:n_r + 2]
        if n_r:
            start, finish = ride["halves"](rest[:n_r], rest[n_r + 2:2 * n_r + 2], *rest[2 * n_r + 2:])
            pl.when(pl.program_id(0) == 0)(start)
        xv = x_ref[...]
        r = lax.rsqrt(jnp.mean(xv * xv, axis=-1, keepdims=True) + EPS)
        h_ref[...] = (xv * r * w_ref[...]).astype(BF16)
        r_ref[...] = r
        if n_r:
            pl.when(pl.program_id(0) == T // tm - 1)(finish)

    return pl.pallas_call(
        body, name=name, grid=(T // tm,),
        in_specs=[pl.BlockSpec((tm, Dm), lambda i: (i, 0)), pl.BlockSpec((1, Dm), lambda i: (0, 0))] + [ANY] * n_r,
        out_specs=[pl.BlockSpec((tm, Dm), lambda i: (i, 0)), pl.BlockSpec((tm, 1), lambda i: (i, 0))] + [ANY] * n_r,
        out_shape=[jax.ShapeDtypeStruct((T, Dm), BF16), jax.ShapeDtypeStruct((T, 1), F32)]
                  + (ride["out_shape"] if n_r else []),
        scratch_shapes=ride["scratch"] if n_r else [],
        compiler_params=pltpu.CompilerParams(dimension_semantics=("arbitrary" if n_r else "parallel",),
                                             vmem_limit_bytes=VMEM_LIMIT, has_side_effects=n_r > 0),
    )(x, w, *(ride["arrays"] if n_r else []))


def _tail_rms_fwd(w):
    def fn(xv, first, w_ref, x_ref, h_ref, r_ref):
        r = lax.rsqrt(jnp.mean(xv * xv, axis=-1, keepdims=True) + EPS)
        x_ref[...] = xv
        h_ref[...] = (xv * r * w_ref[...]).astype(BF16)
        r_ref[...] = r

    return dict(fn=fn, ins=[(w, "vec")], outs=[(F32, "row"), (BF16, "row"), (F32, "col")])


def _tail_rms_bwd(x, r, w, dres, *, emit_bf16):
    def fn(dhv, first, x_ref, r_ref, w_ref, dres_ref, *outs):
        dx_ref, dw_ref = outs[0], outs[-1]

        @pl.when(first)
        def _():
            dw_ref[...] = jnp.zeros_like(dw_ref)

        rv = r_ref[...]
        xh = x_ref[...] * rv
        dxh = dhv * w_ref[...]
        t = jnp.mean(dxh * xh, axis=-1, keepdims=True)
        dx = dres_ref[...] + rv * (dxh - xh * t)
        dx_ref[...] = dx
        if emit_bf16:
            outs[1][...] = dx.astype(BF16)
        dw_ref[...] += jnp.sum(dhv * xh, axis=0, keepdims=True)

    outs = [(F32, "row")] + ([(BF16, "row")] if emit_bf16 else []) + [(F32, "vec")]
    return dict(fn=fn, ins=[(x, "row"), (r, "col"), (w, "vec"), (dres, "row")], outs=outs)


def _tail_loss(target, w):
    def fn(xv, first, t_ref, w_ref, loss_ref, dx_ref, dxb_ref, dw_ref):
        @pl.when(first)
        def _():
            loss_ref[...] = jnp.zeros_like(loss_ref)
            dw_ref[...] = jnp.zeros_like(dw_ref)

        r = lax.rsqrt(jnp.mean(xv * xv, axis=-1, keepdims=True) + EPS)
        xh = xv * r
        wv = w_ref[...]
        err = xh * wv - t_ref[...]
        row_loss = jnp.mean(err * err, axis=-1, keepdims=True)
        loss_ref[...] += 0.5 * jnp.sum(row_loss, axis=0, keepdims=True)
        dy = err * (1.0 / xv.shape[-1])
        dxh = dy * wv
        t = jnp.mean(dxh * xh, axis=-1, keepdims=True)
        dx = r * (dxh - xh * t)
        dx_ref[...] = dx
        dxb_ref[...] = dx.astype(BF16)
        dw_ref[...] += jnp.sum(dy * xh, axis=0, keepdims=True)

    return dict(fn=fn, ins=[(target, "row"), (w, "vec")],
                outs=[(F32, "one"), (F32, "row"), (BF16, "row"), (F32, "vec")])


GLA_TB = 512
GLA_NC = GLA_TB // CHUNK
GLA_UNROLL = 4


def _cumsum_rows(x, row, reverse):
    n = x.shape[0]
    s = 1
    while s < n:
        if not reverse:
            x = x + jnp.where(row >= s, pltpu.roll(x, s, 0), 0.0)
        else:
            x = x + jnp.where(row < n - s, pltpu.roll(x, n - s, 0), 0.0)
        s *= 2
    return x


def _gla_gates(uq, z, lbv):
    q = uq * _sigmoid(uq)
    sg = _sigmoid(z)
    sgn = _sigmoid(-z)
    f = lbv + (1.0 - lbv) * sg
    k = (1.0 - lbv) * sgn
    return q, sg, sgn, f, k


def _gla_decays(f, row, reverse):
    b = _cumsum_rows(jnp.log(f), row, reverse)
    if not reverse:
        bref, blast = b[CHUNK // 2 - 1:CHUNK // 2, :], b[CHUNK - 1:CHUNK, :]
    else:
        bref, blast = b[CHUNK // 2:CHUNK // 2 + 1, :], b[0:1, :]
    return b, bref, blast


def _gla_fwd(U, lb, *, f_block, reverse, name, ride=None, post=None):
    T = U.shape[0]
    nb = T // GLA_TB
    n_g = 0 if ride is None else len(ride["arrays"])
    n_p = 0 if post is None else 3

    def body(uq_ref, uf_ref, ui_ref, lb_ref, *rest):
        post_in, rest = rest[:n_p], rest[n_p:]
        g_in, rest = rest[:n_g], rest[n_g:]
        o_ref, st_ref = rest[:2]
        mix_ref = rest[2] if n_p else None
        rest = rest[2 + (n_p > 0):]
        g_out, rest = rest[:n_g], rest[n_g:]
        s_ref = rest[0]
        if n_g:
            start, finish = ride["halves"](g_in, g_out, *rest[1:])
            pl.when(pl.program_id(0) == 0)(start)

        @pl.when(pl.program_id(0) == 0)
        def _():
            s_ref[...] = jnp.zeros_like(s_ref)

        row = lax.broadcasted_iota(jnp.int32, (CHUNK, HG_D), 0)
        ri = lax.broadcasted_iota(jnp.int32, (CHUNK, CHUNK), 0)
        ci = lax.broadcasted_iota(jnp.int32, (CHUNK, CHUNK), 1)
        mask = (ri <= ci) if reverse else (ri >= ci)

        def chunk(j, carry):
            c = (GLA_NC - 1 - j) if reverse else j
            rows = pl.ds(pl.multiple_of(c * CHUNK, CHUNK), CHUNK)
            for h in range(HG_HEADS):
                cols = pl.ds(h * HG_D, HG_D)
                v = ui_ref[rows, cols]
                q, _, _, f, k = _gla_gates(uq_ref[rows, cols], uf_ref[rows, cols], lb_ref[:, cols])
                b, bref, blast = _gla_decays(f, row, reverse)
                s = jnp.where(mask, _dot_nt(q * jnp.exp(b - bref), k * jnp.exp(bref - b)), 0.0)
                st = s_ref[h]
                st_ref[c, h] = st
                o = _dot(s, v) + _dot_nt(q * jnp.exp(b), st)
                if n_p:
                    other_ref, ug_ref, w_ref = post_in
                    o = o + other_ref[rows, cols]
                    r = lax.rsqrt(jnp.mean(o * o, axis=-1, keepdims=True) + EPS)
                    ug = ug_ref[rows, cols]
                    mix_ref[rows, cols] = (o * r * w_ref[...] * (ug * _sigmoid(ug))).astype(BF16)
                o_ref[rows, cols] = o
                s_ref[h] = st * jnp.exp(blast) + _dot_tn(v, k * jnp.exp(blast - b))
            return carry

        lax.fori_loop(0, GLA_NC, chunk, 0, unroll=GLA_NC)
        if n_g:
            pl.when(pl.program_id(0) == nb - 1)(finish)

    blk = (lambda i: nb - 1 - i) if reverse else (lambda i: i)
    ucol = lambda cb: pl.BlockSpec((GLA_TB, HG_W), lambda i: (blk(i), cb))
    tok = pl.BlockSpec((GLA_TB, HG_W), lambda i: (blk(i), 0))
    in_specs = [ucol(0), ucol(f_block), ucol(3), pl.BlockSpec((1, HG_W), lambda i: (0, 0))]
    args = [U, U, U, lb]
    out_specs = [tok, pl.BlockSpec((GLA_NC, HG_HEADS, HG_D, HG_D), lambda i: (blk(i), 0, 0, 0))]
    out_shape = [jax.ShapeDtypeStruct((T, HG_W), F32), jax.ShapeDtypeStruct((T // CHUNK, HG_HEADS, HG_D, HG_D), F32)]
    if n_p:
        in_specs += [tok, ucol(4), pl.BlockSpec((1, HG_D), lambda i: (0, 0))]
        args += [post[0], U, post[1]]
        out_specs.append(tok)
        out_shape.append(jax.ShapeDtypeStruct((T, HG_W), BF16))
    return pl.pallas_call(
        body, name=name, grid=(nb,), in_specs=in_specs + [ANY] * n_g, out_specs=out_specs + [ANY] * n_g,
        out_shape=out_shape + (ride["out_shape"] if n_g else []),
        scratch_shapes=[pltpu.VMEM((HG_HEADS, HG_D, HG_D), F32)] + (ride["scratch"] if n_g else []),
        compiler_params=pltpu.CompilerParams(dimension_semantics=("arbitrary",), vmem_limit_bytes=VMEM_LIMIT,
                                             has_side_effects=bool(n_g)),
    )(*args, *(ride["arrays"] if n_g else []))


def _gla_bwd(U, lb, do, states, *, f_block, reverse, name, prev=None):
    T = U.shape[0]
    nb = T // GLA_TB
    final = prev is not None

    def body(uq_ref, uf_ref, ui_ref, lb_ref, do_ref, st_ref, *rest):
        if final:
            dqp_ref, dzp_ref, dvp_ref, dug_ref, out_ref, dlb_ref, ds_ref = rest
        else:
            dq_ref, dz_ref, dv_ref, dlb_ref, ds_ref = rest

        @pl.when(pl.program_id(0) == 0)
        def _():
            ds_ref[...] = jnp.zeros_like(ds_ref)
            dlb_ref[...] = jnp.zeros_like(dlb_ref)

        row = lax.broadcasted_iota(jnp.int32, (CHUNK, HG_D), 0)
        ri = lax.broadcasted_iota(jnp.int32, (CHUNK, CHUNK), 0)
        ci = lax.broadcasted_iota(jnp.int32, (CHUNK, CHUNK), 1)
        mask = (ri <= ci) if reverse else (ri >= ci)

        def chunk(j, carry):
            c = j if reverse else (GLA_NC - 1 - j)
            rows = pl.ds(pl.multiple_of(c * CHUNK, CHUNK), CHUNK)
            for h in range(HG_HEADS):
                cols = pl.ds(h * HG_D, HG_D)
                v = ui_ref[rows, cols]
                lbv = lb_ref[:, cols]
                uq = uq_ref[rows, cols]
                q, sg, sgn, f, k = _gla_gates(uq, uf_ref[rows, cols], lbv)
                b, bref, blast = _gla_decays(f, row, reverse)
                eq, ek, eb, el, dec = (jnp.exp(b - bref), jnp.exp(bref - b), jnp.exp(b), jnp.exp(blast - b),
                                       jnp.exp(blast))
                qin, kin, qb, klast = q * eq, k * ek, q * eb, k * el
                dov = do_ref[rows, cols]
                st = st_ref[c, h]
                dst = ds_ref[h]
                p = jnp.where(mask, _dot_nt(qin, kin), 0.0)
                dp = jnp.where(mask, _dot_nt(dov, v), 0.0)
                dqin = _dot(dp, kin)
                dkin = _dot_tn(dp, qin)
                dv = _dot_tn(p, dov) + _dot_nt(klast, dst)
                dqb = _dot(dov, st)
                dklast = _dot(v, dst)
                ds_ref[h] = _dot_tn(dov, qb) + dst * dec
                db = dqin * qin - dkin * kin + dqb * qb - dklast * klast
                extra = (jnp.sum(dklast * klast, axis=0, keepdims=True)
                         + dec * jnp.sum(st * dst, axis=0, keepdims=True))
                dg = _cumsum_rows(db, row, not reverse) + extra
                dq = dqin * eq + dqb * eb
                dk = dkin * ek + dklast * el
                dfk = dg / f - dk
                dz = (dfk * (1.0 - lbv) * sg * sgn).astype(BF16)
                dlb_ref[:, cols] += jnp.sum(dfk * sgn, axis=0, keepdims=True)
                if final:
                    sq = _sigmoid(uq)
                    col = lambda blk: pl.ds(blk * HG_W + h * HG_D, HG_D)
                    out_ref[rows, col(0)] = ((dq + dqp_ref[rows, cols]) * (sq * (1.0 + uq * (1.0 - sq)))).astype(BF16)
                    out_ref[rows, col(1)] = dzp_ref[rows, cols]
                    out_ref[rows, col(2)] = dz
                    out_ref[rows, col(3)] = (dv + dvp_ref[rows, cols]).astype(BF16)
                    out_ref[rows, col(4)] = dug_ref[rows, cols]
                else:
                    dq_ref[rows, cols] = dq
                    dz_ref[rows, cols] = dz
                    dv_ref[rows, cols] = dv
            return carry

        lax.fori_loop(0, GLA_NC, chunk, 0, unroll=GLA_UNROLL)

    blk = (lambda i: i) if reverse else (lambda i: nb - 1 - i)
    ucol = lambda cb: pl.BlockSpec((GLA_TB, HG_W), lambda i: (blk(i), cb))
    tok = pl.BlockSpec((GLA_TB, HG_W), lambda i: (blk(i), 0))
    vec = pl.BlockSpec((1, HG_W), lambda i: (0, 0))
    in_specs = [ucol(0), ucol(f_block), ucol(3), vec, tok,
                pl.BlockSpec((GLA_NC, HG_HEADS, HG_D, HG_D), lambda i: (blk(i), 0, 0, 0))]
    vec_shape = jax.ShapeDtypeStruct((1, HG_W), F32)
    if final:
        in_specs += [tok] * 4
        out_specs = [pl.BlockSpec((GLA_TB, 5 * HG_W), lambda i: (blk(i), 0)), vec]
        out_shape = [jax.ShapeDtypeStruct((T, 5 * HG_W), BF16), vec_shape]
    else:
        out_specs = [tok, tok, tok, vec]
        out_shape = [jax.ShapeDtypeStruct((T, HG_W), F32), jax.ShapeDtypeStruct((T, HG_W), BF16),
                     jax.ShapeDtypeStruct((T, HG_W), F32), vec_shape]
    return pl.pallas_call(
        body, name=name, grid=(nb,), in_specs=in_specs, out_specs=out_specs, out_shape=out_shape,
        scratch_shapes=[pltpu.VMEM((HG_HEADS, HG_D, HG_D), F32)],
        compiler_params=_params(("arbitrary",)),
    )(U, U, U, lb, do, states, *(prev if final else ()))


def _hg_post_bwd(dmix, o_sum, U, w, *, name, tm=512):
    T = o_sum.shape[0]

    def body(dm_ref, o_ref, ug_ref, w_ref, do_ref, dug_ref, dw_ref):
        @pl.when(pl.program_id(0) == 0)
        def _():
            dw_ref[...] = jnp.zeros_like(dw_ref)

        wv = w_ref[...]
        for h in range(HG_HEADS):
            cols = pl.ds(h * HG_D, HG_D)
            o = o_ref[:, cols]
            r = lax.rsqrt(jnp.mean(o * o, axis=-1, keepdims=True) + EPS)
            xh = o * r
            ug = ug_ref[:, cols]
            sg = _sigmoid(ug)
            dm = dm_ref[:, cols]
            dn = dm * (ug * sg)
            dug_ref[:, cols] = (dm * (xh * wv) * (sg * (1.0 + ug * (1.0 - sg)))).astype(BF16)
            dxh = dn * wv
            t = jnp.mean(dxh * xh, axis=-1, keepdims=True)
            do_ref[:, cols] = r * (dxh - xh * t)
            dw_ref[:, cols] += jnp.sum(dn * xh, axis=0, keepdims=True)

    tok = pl.BlockSpec((tm, HG_W), lambda i: (i, 0))
    vec = pl.BlockSpec((1, HG_W), lambda i: (0, 0))
    return pl.pallas_call(
        body, name=name, grid=(T // tm,),
        in_specs=[tok, tok, pl.BlockSpec((tm, HG_W), lambda i: (i, 4)), pl.BlockSpec((1, HG_D), lambda i: (0, 0))],
        out_specs=[tok, tok, vec],
        out_shape=[jax.ShapeDtypeStruct((T, HG_W), F32), jax.ShapeDtypeStruct((T, HG_W), BF16),
                   jax.ShapeDtypeStruct((1, HG_W), F32)],
        compiler_params=_params(("arbitrary",)),
    )(dmix, o_sum, U, w)


def _rope_tables(T):
    rows = T // GRID_W
    row = np.repeat(np.arange(rows), GRID_W).astype(np.float32)
    col = np.tile(np.arange(GRID_W), rows).astype(np.float32)
    axis_dim = ATT_DH // 2
    freqs = (np.float32(ROPE_THETA) ** (-np.arange(0, axis_dim, 2, dtype=np.float32) / np.float32(axis_dim))
             ).astype(np.float32)
    ang = np.concatenate([row[:, None] * freqs, col[:, None] * freqs], axis=-1).astype(np.float32)
    cos, sin = np.cos(ang), np.sin(ang)
    c = np.repeat(cos, 2, axis=-1)
    s = np.stack([-sin, sin], axis=-1).reshape(T, ATT_DH)
    return jnp.asarray(np.tile(c, (1, 2)), F32), jnp.asarray(np.tile(s, (1, 2)), F32)


def _head_blockdiag(width):
    shift = ATT_DH.bit_length() - 1
    ri = jnp.right_shift(lax.broadcasted_iota(jnp.int32, (width, width), 0), shift)
    ci = jnp.right_shift(lax.broadcasted_iota(jnp.int32, (width, width), 1), shift)
    return jnp.where(ri == ci, 1.0, 0.0).astype(BF16)


def _head_sum(x, bd):
    hi = x.astype(BF16)
    lo = (x - hi.astype(F32)).astype(BF16)
    return jnp.dot(hi, bd, preferred_element_type=F32) + jnp.dot(lo, bd, preferred_element_type=F32)


def _pair_swap(x, even):
    n = x.shape[-1]
    return jnp.where(even, pltpu.roll(x, n - 1, 1), pltpu.roll(x, 1, 1))


FA_TQ = 512


FA_TK = 512


def _cols_from_tokens(x, kv):
    w = ATT_G * ATT_DH
    xt = x[:, kv * w:(kv + 1) * w].T
    return jnp.concatenate([xt[g * ATT_DH:(g + 1) * ATT_DH, :] for g in range(ATT_G)], axis=1)


def _tokens_from_cols(c):
    tq = c.shape[1] // ATT_G
    return jnp.concatenate([c[:, g * tq:(g + 1) * tq] for g in range(ATT_G)], axis=0).T


def _store_cols(ref, x, norm_ref=None):
    for kv in range(ATT_KV):
        cols = _cols_from_tokens(x, kv).astype(BF16)
        ref[kv, 0] = cols
        if norm_ref is not None:
            cf = cols.astype(F32)
            norm_ref[kv, 0] = jnp.sqrt(jnp.sum(cf * cf, axis=0, keepdims=True))


def _att_prep_fwd(U, cos, sin, qw, kw, *, name):
    T = U.shape[0]
    tm = min(FA_TQ, T)
    R = ATT_G * tm
    scale = ATT_DH ** -0.5

    def head_rows(ref, x):
        xt = x.astype(F32).T
        for kv in range(ATT_KV):
            ref[kv, 0] = xt[kv * ATT_DH:(kv + 1) * ATT_DH, :].astype(BF16)

    def body(aq_ref, ak_ref, av_ref, c_ref, s_ref, qw_ref, kw_ref, q_ref, qn_ref, kmax_ref, kc_ref, vc_ref):
        @pl.when(pl.program_id(0) == 0)
        def _():
            kmax_ref[...] = jnp.zeros_like(kmax_ref)

        bd = _head_blockdiag(ATT_QW)
        c2, s2 = c_ref[...], s_ref[...]
        c8, s8 = jnp.tile(c2, (1, 4)), jnp.tile(s2, (1, 4))

        def norm_rope(x, w, c, s, bdm):
            r = lax.rsqrt(_head_sum(x * x, bdm) * (1.0 / ATT_DH) + EPS)
            y = x * r * w
            even = (lax.broadcasted_iota(jnp.int32, y.shape, 1) & 1) == 0
            return y * c + _pair_swap(y, even) * s

        _store_cols(q_ref, norm_rope(aq_ref[...], qw_ref[...], c8, s8, bd) * (scale * LOG2E), qn_ref)
        kb = norm_rope(ak_ref[...], kw_ref[...], c2, s2, bd[:ATT_KW, :ATT_KW]).astype(BF16)
        kf = kb.astype(F32)
        ksq = _head_sum(kf * kf, bd[:ATT_KW, :ATT_KW])
        kmax_ref[...] = jnp.maximum(kmax_ref[...], jnp.max(ksq, axis=0, keepdims=True))
        head_rows(kc_ref, kb)
        head_rows(vc_ref, av_ref[...].astype(BF16))

    kv_spec = pl.BlockSpec((tm, ATT_KW), lambda i: (i, 0))
    tk = min(FA_TK, T)
    per = tk // tm
    c_spec = pl.BlockSpec((ATT_KV, 1, ATT_DH, tm), lambda i: (0, i // per, 0, i % per))
    c_shape = jax.ShapeDtypeStruct((ATT_KV, T // tk, ATT_DH, tk), BF16)
    return pl.pallas_call(
        body, name=name, grid=(T // tm,),
        in_specs=[pl.BlockSpec((tm, ATT_QW), lambda i: (i, 5)),
                  pl.BlockSpec((tm, ATT_KW), lambda i: (i, 24)), pl.BlockSpec((tm, ATT_KW), lambda i: (i, 25)),
                  kv_spec, kv_spec,
                  pl.BlockSpec((1, ATT_QW), lambda i: (0, 0)), pl.BlockSpec((1, ATT_KW), lambda i: (0, 0))],
        out_specs=[pl.BlockSpec((ATT_KV, 1, ATT_DH, R), lambda i: (0, i, 0, 0)),
                   pl.BlockSpec((ATT_KV, 1, 1, R), lambda i: (0, i, 0, 0)), pl.BlockSpec((1, ATT_KW), lambda i: (0, 0)),
                   c_spec, c_spec],
        out_shape=[jax.ShapeDtypeStruct((ATT_KV, T // tm, ATT_DH, R), BF16),
                   jax.ShapeDtypeStruct((ATT_KV, T // tm, 1, R), F32), jax.ShapeDtypeStruct((1, ATT_KW), F32),
                   c_shape, c_shape],
        compiler_params=_params(("arbitrary",)),
    )(U, U, U, cos, sin, qw, kw)


def _att_prep_bwd(U, dq_c, dk_c, dv_c, cos, sin, qw, kw, *, name):
    T = U.shape[0]
    tm = min(FA_TQ, T)
    R = ATT_G * tm
    scale = ATT_DH ** -0.5

    def body(aq_ref, ak_ref, dq_ref, dk_ref, dv_ref, c_ref, s_ref, qw_ref, kw_ref, out_ref, dqw_ref, dkw_ref):
        @pl.when(pl.program_id(0) == 0)
        def _():
            dqw_ref[...] = jnp.zeros_like(dqw_ref)
            dkw_ref[...] = jnp.zeros_like(dkw_ref)

        bd = _head_blockdiag(ATT_QW)
        c2, s2 = c_ref[...], s_ref[...]
        c8, s8 = jnp.tile(c2, (1, 4)), jnp.tile(s2, (1, 4))

        def bwd(x, dy, w, c, s, bdm):
            even = (lax.broadcasted_iota(jnp.int32, x.shape, 1) & 1) == 0
            dn = dy * c - _pair_swap(dy, even) * s
            r = lax.rsqrt(_head_sum(x * x, bdm) * (1.0 / ATT_DH) + EPS)
            xh = x * r
            dxh = dn * w
            t = _head_sum(dxh * xh, bdm) * (1.0 / ATT_DH)
            return r * (dxh - xh * t), jnp.sum(dn * xh, axis=0, keepdims=True)

        dq = jnp.concatenate([_tokens_from_cols(dq_ref[kv, 0]) for kv in range(ATT_KV)], axis=1)
        da, dw = bwd(aq_ref[...], dq * scale, qw_ref[...], c8, s8, bd)
        out_ref[:, 0:ATT_QW] = da.astype(BF16)
        dqw_ref[...] += dw
        tokens = lambda ref: jnp.concatenate([ref[kv, 0] for kv in range(ATT_KV)], axis=0).T
        da, dw = bwd(ak_ref[...], tokens(dk_ref) * (1.0 / LOG2E), kw_ref[...], c2, s2, bd[:ATT_KW, :ATT_KW])
        out_ref[:, ATT_QW:ATT_QW + ATT_KW] = da.astype(BF16)
        dkw_ref[...] += dw
        out_ref[:, ATT_QW + ATT_KW:ATT_QW + 2 * ATT_KW] = tokens(dv_ref).astype(BF16)

    kv_spec = pl.BlockSpec((tm, ATT_KW), lambda i: (i, 0))
    qv = pl.BlockSpec((1, ATT_QW), lambda i: (0, 0))
    kv = pl.BlockSpec((1, ATT_KW), lambda i: (0, 0))
    w_att = ATT_QW + 2 * ATT_KW
    per = dk_c.shape[3] // tm
    c_spec = pl.BlockSpec((ATT_KV, 1, ATT_DH, tm), lambda i: (0, i // per, 0, i % per))
    return pl.pallas_call(
        body, name=name, grid=(T // tm,),
        in_specs=[pl.BlockSpec((tm, ATT_QW), lambda i: (i, 5)), pl.BlockSpec((tm, ATT_KW), lambda i: (i, 24)),
                  pl.BlockSpec((ATT_KV, 1, ATT_DH, R), lambda i: (0, i, 0, 0)), c_spec, c_spec, kv_spec, kv_spec, qv, kv],
        out_specs=[pl.BlockSpec((tm, w_att), lambda i: (i, 0)), qv, kv],
        out_shape=[jax.ShapeDtypeStruct((T, w_att), BF16),
                   jax.ShapeDtypeStruct((1, ATT_QW), F32), jax.ShapeDtypeStruct((1, ATT_KW), F32)],
        compiler_params=_params(("arbitrary",)),
    )(U, U, dq_c, dk_c, dv_c, cos, sin, qw, kw)


def _scores(k_ref, j, qv):
    return lax.dot_general(k_ref[0, j], qv, (((0,), (0,)), ((), ())), preferred_element_type=F32)


def _flash_fwd(q_c, k_c, v_c, *, name):
    _, nq, _, R = q_c.shape
    _, n_k, _, tk = v_c.shape

    def body(q_ref, k_ref, v_ref, o_ref, lse_ref, acc_ref):
        qv = q_ref[0, 0]
        acc_ref[...] = jnp.zeros_like(acc_ref)

        def step(j, carry):
            m, l = carry
            s = _scores(k_ref, j, qv)
            m_new = jnp.maximum(m, jnp.max(s, axis=0, keepdims=True))
            alpha = jnp.exp2(m - m_new)
            p = jnp.exp2(s - m_new)
            l = alpha * l + jnp.sum(p, axis=0, keepdims=True)
            acc_ref[...] = alpha * acc_ref[...] + jnp.dot(v_ref[0, j], p.astype(BF16), preferred_element_type=F32)
            return m_new, l

        m, l = lax.fori_loop(0, n_k, step, (jnp.full((1, R), -jnp.inf, F32), jnp.zeros((1, R), F32)))
        o_ref[0, 0] = acc_ref[...] / l
        lse_ref[0, 0] = m + jnp.log2(l)

    cspec = pl.BlockSpec((1, 1, ATT_DH, R), lambda h, i: (h, i, 0, 0))
    kspec = pl.BlockSpec((1, n_k, ATT_DH, tk), lambda h, i: (h, 0, 0, 0))
    return pl.pallas_call(
        body, name=name, grid=(ATT_KV, nq),
        in_specs=[cspec, kspec, kspec],
        out_specs=[cspec, pl.BlockSpec((1, 1, 1, R), lambda h, i: (h, i, 0, 0))],
        out_shape=[jax.ShapeDtypeStruct((ATT_KV, nq, ATT_DH, R), F32), jax.ShapeDtypeStruct((ATT_KV, nq, 1, R), F32)],
        scratch_shapes=[pltpu.VMEM((ATT_DH, R), F32)],
        compiler_params=_params(("parallel", "parallel")),
    )(q_c, k_c, v_c)


FA_BOUND_MAX = 40.0 * LOG2E


def _flash_fwd_bounded(q_c, k_c, v_c, m_c, *, name):
    _, nq, _, R = q_c.shape
    _, n_k, _, tk = v_c.shape

    def body(q_ref, k_ref, v_ref, m_ref, o_ref, lse_ref, acc_ref):
        qv = q_ref[0, 0]
        m = m_ref[0, 0]
        acc_ref[...] = jnp.zeros_like(acc_ref)

        per = math.gcd(n_k, 4)

        def step(jj, l8):
            pv = None
            for u in range(per):
                j = per * jj + u
                p = jnp.exp2(_scores(k_ref, j, qv) - m)
                l8 = l8 + jnp.sum(p.reshape(tk // 8, 8, R), axis=0)
                d = jnp.dot(v_ref[0, j], p.astype(BF16), preferred_element_type=F32)
                pv = d if pv is None else pv + d
            acc_ref[...] += pv
            return l8

        l8 = lax.fori_loop(0, n_k // per, step, jnp.zeros((8, R), F32))
        l = jnp.sum(l8, axis=0, keepdims=True)
        o_ref[0, 0] = acc_ref[...] / l
        lse_ref[0, 0] = m + jnp.log2(l)

    cspec = pl.BlockSpec((1, 1, ATT_DH, R), lambda h, i: (h, i, 0, 0))
    kspec = pl.BlockSpec((1, n_k, ATT_DH, tk), lambda h, i: (h, 0, 0, 0))
    vspec = pl.BlockSpec((1, 1, 1, R), lambda h, i: (h, i, 0, 0))
    return pl.pallas_call(
        body, name=name, grid=(ATT_KV, nq),
        in_specs=[cspec, kspec, kspec, vspec],
        out_specs=[cspec, vspec],
        out_shape=[jax.ShapeDtypeStruct((ATT_KV, nq, ATT_DH, R), F32), jax.ShapeDtypeStruct((ATT_KV, nq, 1, R), F32)],
        scratch_shapes=[pltpu.VMEM((ATT_DH, R), F32)],
        compiler_params=_params(("parallel", "parallel")),
    )(q_c, k_c, v_c, m_c)


CHIP_MASKS = [(1, 0, 0), (0, 1, 0), (1, 1, 0)]


def _chip_slot(p):
    return 2 * p[0] + p[1]


def _flash_bwd(q_c, k_c, v_c, do_c, lse, delta, *, name, ride=None):
    _, nq, _, R = q_c.shape
    _, n_k, _, tk = k_c.shape
    n_ride = 0 if ride is None else len(ride["arrays"])

    def body(qc_ref, kc_ref, vc_ref, doc_ref, lse_ref, delta_ref, *rest):
        ride_in, rest = rest[:n_ride], rest[n_ride:]
        dq_ref, dk_ref, dv_ref = rest[:3]
        ride_out, rest = rest[3:3 + n_ride], rest[3 + n_ride:]
        acc_ref = rest[0]
        kv = pl.program_id(0)
        if n_ride:
            start, finish = ride["halves"](ride_in, ride_out, *rest[1:])
            pl.when((kv == 0) & (pl.program_id(1) == 0))(start)

        @pl.when(pl.program_id(1) == 0)
        def _():
            dk_ref[...] = jnp.zeros_like(dk_ref)
            dv_ref[...] = jnp.zeros_like(dv_ref)

        qc, doc = qc_ref[0, 0], doc_ref[0, 0]
        lsev, delta = lse_ref[0, 0], delta_ref[0, 0]
        acc_ref[...] = jnp.zeros_like(acc_ref)
        nt = (((1,), (1,)), ((), ()))

        def step(j, carry):
            p = jnp.exp2(_scores(kc_ref, j, qc) - lsev)
            dp = _scores(vc_ref, j, doc)
            ds = (p * (dp - delta)).astype(BF16)
            acc_ref[...] += jnp.dot(kc_ref[0, j], ds, preferred_element_type=F32)
            dk_ref[0, j] += lax.dot_general(qc, ds, nt, preferred_element_type=F32)
            dv_ref[0, j] += lax.dot_general(doc, p.astype(BF16), nt, preferred_element_type=F32)
            return carry

        lax.fori_loop(0, n_k, step, 0, unroll=2)
        dq_ref[0, 0] = acc_ref[...]

        if n_ride:
            pl.when((kv == ATT_KV - 1) & (pl.program_id(1) == nq - 1))(finish)

    cspec = pl.BlockSpec((1, 1, ATT_DH, R), lambda h, i: (h, i, 0, 0))
    vspec = pl.BlockSpec((1, 1, 1, R), lambda h, i: (h, i, 0, 0))
    kspec = pl.BlockSpec((1, n_k, ATT_DH, tk), lambda h, i: (h, 0, 0, 0))
    k_shape = jax.ShapeDtypeStruct(k_c.shape, F32)
    return pl.pallas_call(
        body, name=name, grid=(ATT_KV, nq),
        in_specs=[cspec, kspec, kspec, cspec, vspec, vspec] + [ANY] * n_ride,
        out_specs=[cspec, kspec, kspec] + [ANY] * n_ride,
        out_shape=[jax.ShapeDtypeStruct((ATT_KV, nq, ATT_DH, R), F32), k_shape, k_shape]
                  + (ride["out_shape"] if n_ride else []),
        scratch_shapes=[pltpu.VMEM((ATT_DH, R), F32)] + (ride["scratch"] if n_ride else []),
        compiler_params=pltpu.CompilerParams(dimension_semantics=("arbitrary", "arbitrary"),
                                             vmem_limit_bytes=VMEM_LIMIT, has_side_effects=bool(n_ride)),
    )(q_c, k_c, v_c, do_c, lse, delta, *(ride["arrays"] if n_ride else []))


def _att_post_fwd(o_c, w, *, name):
    _, nq, _, R = o_c.shape
    tm = R // ATT_G
    T = nq * tm

    def body(oc_ref, w_ref, o_ref, out_ref):
        ov = jnp.concatenate([_tokens_from_cols(oc_ref[kv, 0]) for kv in range(ATT_KV)], axis=1)
        r = lax.rsqrt(jnp.mean(ov * ov, axis=-1, keepdims=True) + EPS)
        o_ref[...] = ov
        out_ref[...] = (ov * r * w_ref[...]).astype(BF16)

    tok = pl.BlockSpec((tm, ATT_QW), lambda i: (i, 0))
    return pl.pallas_call(
        body, name=name, grid=(nq,),
        in_specs=[pl.BlockSpec((ATT_KV, 1, ATT_DH, R), lambda i: (0, i, 0, 0)), pl.BlockSpec((1, ATT_QW), lambda i: (0, 0))],
        out_specs=[tok, tok],
        out_shape=[jax.ShapeDtypeStruct((T, ATT_QW), F32), jax.ShapeDtypeStruct((T, ATT_QW), BF16)],
        compiler_params=_params(("parallel",)),
    )(o_c, w)


def _att_post_bwd(dmix, o, w, *, name):
    T = o.shape[0]
    tm = min(FA_TQ, T)
    R = ATT_G * tm

    def body(dm_ref, o_ref, w_ref, do_ref, delta_ref, dw_ref):
        @pl.when(pl.program_id(0) == 0)
        def _():
            dw_ref[...] = jnp.zeros_like(dw_ref)

        ov = o_ref[...]
        r = lax.rsqrt(jnp.mean(ov * ov, axis=-1, keepdims=True) + EPS)
        xh = ov * r
        dm = dm_ref[...]
        dxh = dm * w_ref[...]
        t = jnp.mean(dxh * xh, axis=-1, keepdims=True)
        do = r * (dxh - xh * t)
        _store_cols(do_ref, do)
        dob = do.astype(BF16).astype(F32)
        for kv in range(ATT_KV):
            delta_ref[kv, 0] = jnp.sum(_cols_from_tokens(dob * ov, kv), axis=0, keepdims=True)
        dw_ref[...] += jnp.sum(dm * xh, axis=0, keepdims=True)

    tok = pl.BlockSpec((tm, ATT_QW), lambda i: (i, 0))
    vec = pl.BlockSpec((1, ATT_QW), lambda i: (0, 0))
    return pl.pallas_call(
        body, name=name, grid=(T // tm,),
        in_specs=[pl.BlockSpec((tm, ATT_QW), lambda i: (i, 1)), tok, vec],
        out_specs=[pl.BlockSpec((ATT_KV, 1, ATT_DH, R), lambda i: (0, i, 0, 0)),
                   pl.BlockSpec((ATT_KV, 1, 1, R), lambda i: (0, i, 0, 0)), vec],
        out_shape=[jax.ShapeDtypeStruct((ATT_KV, T // tm, ATT_DH, R), BF16),
                   jax.ShapeDtypeStruct((ATT_KV, T // tm, 1, R), F32), jax.ShapeDtypeStruct((1, ATT_QW), F32)],
        compiler_params=_params(("arbitrary",)),
    )(dmix, o, w)


def _ffn_up(h2, wg_t, wu_t, *, name, tm=512):
    T = h2.shape[0]
    tn = _pick(D_FF, 1408)
    nt = (((1,), (1,)), ((), ()))

    def body(h_ref, wg_ref, wu_ref, g_ref, u_ref, a_ref):
        hv = h_ref[...]
        g = lax.dot_general(hv, wg_ref[...], nt, preferred_element_type=F32)
        u = lax.dot_general(hv, wu_ref[...], nt, preferred_element_type=F32)
        g_ref[...] = g.astype(BF16)
        u_ref[...] = u.astype(BF16)
        a_ref[...] = (g * _sigmoid(g) * u).astype(BF16)

    wspec = pl.BlockSpec((tn, D_MODEL), lambda i, j: (j, 0))
    ospec = pl.BlockSpec((tm, tn), lambda i, j: (i, j))
    return pl.pallas_call(
        body, name=name, grid=(T // tm, D_FF // tn),
        in_specs=[pl.BlockSpec((tm, D_MODEL), lambda i, j: (i, 0)), wspec, wspec],
        out_specs=[ospec] * 3, out_shape=[jax.ShapeDtypeStruct((T, D_FF), BF16)] * 3,
        compiler_params=_params(("parallel", "arbitrary")),
    )(h2, wg_t, wu_t)


def _ffn_act_bwd(dx2b, w_down, gate, up, *, name, tm=512):
    T = dx2b.shape[0]
    tn = _pick(D_FF, 1408)

    def body(dx_ref, w_ref, g_ref, u_ref, dg_ref, du_ref):
        da = lax.dot_general(dx_ref[...], w_ref[...], (((1,), (1,)), ((), ())), preferred_element_type=F32)
        g = g_ref[...].astype(F32)
        u = u_ref[...].astype(F32)
        sg = _sigmoid(g)
        dg_ref[...] = (da * u * (sg * (1.0 + g * (1.0 - sg)))).astype(BF16)
        du_ref[...] = (da * (g * sg)).astype(BF16)

    ospec = pl.BlockSpec((tm, tn), lambda i, j: (i, j))
    return pl.pallas_call(
        body, name=name, grid=(T // tm, D_FF // tn),
        in_specs=[pl.BlockSpec((tm, D_MODEL), lambda i, j: (i, 0)),
                  pl.BlockSpec((tn, D_MODEL), lambda i, j: (j, 0)), ospec, ospec],
        out_specs=[ospec] * 2, out_shape=[jax.ShapeDtypeStruct((T, D_FF), BF16)] * 2,
        compiler_params=_params(("parallel", "arbitrary")),
    )(dx2b, w_down, gate, up)


def _adam_math(w, g, m, v):
    m = ADAM_B1 * m + (1.0 - ADAM_B1) * g
    v = ADAM_B2 * v + (1.0 - ADAM_B2) * (g * g)
    m_hat = m / (1.0 - ADAM_B1 ** ADAM_STEP)
    v_hat = v / (1.0 - ADAM_B2 ** ADAM_STEP)
    delta = -ADAM_LR * (m_hat / (jnp.sqrt(v_hat) + ADAM_EPS) + ADAM_WD * w)
    return delta, m, v


def _adamw(parts, w, m, v, *, name, tr_cap=256):
    P, R, C = parts.shape
    tr = R
    for t in range(8, min(R, tr_cap) + 1, 8):
        if R % t == 0:
            tr = t

    def body(p_ref, w_ref, m_ref, v_ref, g_ref, d_ref, nm_ref, nv_ref):
        g = p_ref[0].astype(F32)
        for j in range(1, P):
            g = g + p_ref[j].astype(F32)
        d, nm, nv = _adam_math(w_ref[...], g, m_ref[...], v_ref[...])
        g_ref[...] = g
        d_ref[...] = d
        nm_ref[...] = nm
        nv_ref[...] = nv

    blk = pl.BlockSpec((tr, C), lambda i: (i, 0))
    return pl.pallas_call(
        body, name=name, grid=(R // tr,),
        in_specs=[pl.BlockSpec((P, tr, C), lambda i: (0, i, 0)), blk, blk, blk],
        out_specs=[blk] * 4, out_shape=[jax.ShapeDtypeStruct((R, C), F32)] * 4,
        compiler_params=_params(("parallel",)),
    )(parts, w, m, v)


def _gather_halves(ins, outs, send_sems, recv_sems, local_sems):
    n = len(ins)
    x, y, c = lax.axis_index("x"), lax.axis_index("y"), lax.axis_index("c")
    me, sibling = (x, y, c), (x, y, 1 - c)
    chips = [(1 - x, y), (x, 1 - y), (1 - x, 1 - y)]

    def slot(p):
        return 4 * p[0] + 2 * p[1] + p[2]

    def copy(a, k, block, to, src=None):
        dst = outs[a].at[slot(block)]
        return pltpu.make_async_remote_copy(
            src_ref=dst if src is None else src, dst_ref=dst,
            send_sem=send_sems.at[a * 7 + k], recv_sem=recv_sems.at[a * 7 + k],
            device_id=to, device_id_type=MESH)

    mine = [pltpu.make_async_copy(ins[a], outs[a].at[slot(me)], local_sems.at[a]) for a in range(n)]
    first = []
    for a in range(n):
        first.append(copy(a, 0, me, sibling, src=ins[a]))
        first += [copy(a, 1 + j, me, (*chip, c), src=ins[a]) for j, chip in enumerate(chips)]

    def start():
        for cp in mine + first:
            cp.start()

    def finish():
        passed = []
        for j, chip in enumerate(chips):
            for a in range(n):
                copy(a, 1 + j, (*chip, c), me).wait_recv()
                cp = copy(a, 4 + j, (*chip, c), sibling)
                cp.start()
                passed.append(cp)
        for a in range(n):
            copy(a, 0, sibling, me).wait_recv()
            for j, chip in enumerate(chips):
                copy(a, 4 + j, (*chip, 1 - c), me).wait_recv()
        for cp in first + passed:
            cp.wait_send()
        for cp in mine:
            cp.wait()

    return start, finish


def _gather_scratch(n):
    return [pltpu.SemaphoreType.DMA((7 * n,)), pltpu.SemaphoreType.DMA((7 * n,)), pltpu.SemaphoreType.DMA((n,))]


def _gathered_shapes(xs):
    return [jax.ShapeDtypeStruct((N_DEV,) + x.shape, x.dtype) for x in xs]


def _ride_gather(xs):
    xs = list(xs)
    return dict(arrays=xs, out_shape=_gathered_shapes(xs), scratch=_gather_scratch(len(xs)), halves=_gather_halves)


def _ride_chips(gs):
    gs = list(gs)
    n = len(gs)

    def halves(ins, outs, send_sems, recv_sems, local_sems):
        mine, copies = _exchange_copies(ins, outs, send_sems, recv_sems, local_sems, masks=CHIP_MASKS, slot=_chip_slot)

        def start():
            for cp in mine:
                cp.start()
            for send, _ in copies:
                send.start()

        def finish():
            for send, recv in copies:
                recv.wait_recv()
                send.wait_send()
            for cp in mine:
                cp.wait()

        return start, finish

    n_sem = len(CHIP_MASKS) * n
    return dict(arrays=gs, out_shape=[jax.ShapeDtypeStruct(g.shape, g.dtype) for g in gs], halves=halves,
                scratch=[pltpu.SemaphoreType.DMA((n_sem,)), pltpu.SemaphoreType.DMA((n_sem,)),
                         pltpu.SemaphoreType.DMA((n,))])


ALL_MASKS = [(mx, my, mc) for mx in (0, 1) for my in (0, 1) for mc in (0, 1)][1:]


def _flip(v, bit):
    return 1 - v if bit else v


def _exchange_copies(ins, outs, send_sems, recv_sems, local_sems, *, masks, slot):
    n, n_peers = len(ins), len(masks)
    x, y, c = lax.axis_index("x"), lax.axis_index("y"), lax.axis_index("c")
    my_slot = slot((x, y, c))
    mine = [pltpu.make_async_copy(ins[a].at[my_slot], outs[a].at[my_slot], local_sems.at[a]) for a in range(n)]
    copies = []
    for a in range(n):
        for k, (mx, my, mc) in enumerate(masks):
            peer = (_flip(x, mx), _flip(y, my), _flip(c, mc))
            peer_slot = slot(peer)
            sems = dict(send_sem=send_sems.at[a * n_peers + k], recv_sem=recv_sems.at[a * n_peers + k],
                        device_id=peer, device_id_type=MESH)
            copies.append((
                pltpu.make_async_remote_copy(src_ref=ins[a].at[peer_slot], dst_ref=outs[a].at[my_slot], **sems),
                pltpu.make_async_remote_copy(src_ref=ins[a].at[peer_slot], dst_ref=outs[a].at[peer_slot], **sems)))
    return mine, copies


def _send_to_all(v, *, name):
    def body(v_ref, out_ref, send_sems, recv_sems, local_sem):
        x, y, c = lax.axis_index("x"), lax.axis_index("y"), lax.axis_index("c")
        me = 4 * x + 2 * y + c
        mine = pltpu.make_async_copy(v_ref, out_ref.at[me], local_sem)
        mine.start()
        copies = []
        for k, (mx, my, mc) in enumerate(ALL_MASKS):
            peer = (_flip(x, mx), _flip(y, my), _flip(c, mc))
            peer_id = 4 * peer[0] + 2 * peer[1] + peer[2]
            sems = dict(send_sem=send_sems.at[k], recv_sem=recv_sems.at[k], device_id=peer, device_id_type=MESH)
            copies.append((pltpu.make_async_remote_copy(src_ref=v_ref, dst_ref=out_ref.at[me], **sems),
                           pltpu.make_async_remote_copy(src_ref=v_ref, dst_ref=out_ref.at[peer_id], **sems)))
        for send, _ in copies:
            send.start()
        for send, recv in copies:
            recv.wait_recv()
            send.wait_send()
        mine.wait()

    n_peers = len(ALL_MASKS)
    return pl.pallas_call(
        body, name=name, in_specs=[ANY], out_specs=ANY,
        out_shape=jax.ShapeDtypeStruct((N_DEV,) + v.shape, v.dtype),
        scratch_shapes=[pltpu.SemaphoreType.DMA((n_peers,)), pltpu.SemaphoreType.DMA((n_peers,)),
                        pltpu.SemaphoreType.DMA],
        compiler_params=pltpu.CompilerParams(has_side_effects=True),
    )(v)


SWAP_ROW_CHUNKS = 4


def _ride_swap(gs):
    gs = list(gs)
    n = len(gs)

    def halves(ins, outs, send_sems, recv_sems):
        x, y, c = lax.axis_index("x"), lax.axis_index("y"), lax.axis_index("c")
        sibling = dict(device_id=(x, y, 1 - c), device_id_type=MESH)

        def start():
            for a in range(n):
                Q, _, R, _ = ins[a].shape
                rows = R // SWAP_ROW_CHUNKS
                for q in range(Q):
                    for j in range(SWAP_ROW_CHUNKS):
                        part = pl.ds(j * rows, rows)
                        pltpu.make_async_remote_copy(src_ref=ins[a].at[q, 1 - c, part], dst_ref=outs[a].at[q, part],
                                                     send_sem=send_sems.at[a], recv_sem=recv_sems.at[a], **sibling).start()

        def finish():
            for a in range(n):
                pltpu.make_async_remote_copy(src_ref=outs[a], dst_ref=outs[a], send_sem=send_sems.at[a],
                                             recv_sem=recv_sems.at[a], **sibling).wait()

        return start, finish

    return dict(arrays=gs, out_shape=[jax.ShapeDtypeStruct(g.shape[:1] + g.shape[2:], g.dtype) for g in gs],
                scratch=[pltpu.SemaphoreType.DMA((n,)), pltpu.SemaphoreType.DMA((n,))], halves=halves)


def _core_swap(gs, *, name):
    ride = _ride_swap(gs)
    n = len(gs)

    def body(*refs):
        start, finish = ride["halves"](refs[:n], refs[n:2 * n], *refs[2 * n:])
        start()
        finish()

    return pl.pallas_call(
        body, name=name, in_specs=[ANY] * n, out_specs=[ANY] * n, out_shape=ride["out_shape"],
        scratch_shapes=ride["scratch"], compiler_params=pltpu.CompilerParams(has_side_effects=True),
    )(*gs)


def _pair_sum(g, other, core, *, name, tr_cap=256):
    Q, _, R, C = g.shape
    tr = max(t for t in range(16, min(R, tr_cap) + 1, 16) if R % t == 0)

    def body(core_ref, g_ref, o_ref, out_ref):
        out_ref[0] = (g_ref[0, 0] + o_ref[0]).astype(BF16)

    return pl.pallas_call(
        body, name=name,
        grid_spec=pltpu.PrefetchScalarGridSpec(
            num_scalar_prefetch=1, grid=(Q, R // tr),
            in_specs=[pl.BlockSpec((1, 1, tr, C), lambda q, i, core_ref: (q, core_ref[0], i, 0)),
                      pl.BlockSpec((1, tr, C), lambda q, i, core_ref: (q, i, 0))],
            out_specs=pl.BlockSpec((1, tr, C), lambda q, i, core_ref: (q, i, 0))),
        out_shape=jax.ShapeDtypeStruct((Q, R, C), BF16),
        compiler_params=_params(("parallel", "parallel")),
    )(core, g, other)


def _pack_small(norm1, norm2, final, att, hg, qn, kn, lb=None, loss=None):
    z = lambda n: jnp.zeros((n,), F32)
    rows = [norm1.reshape(-1), norm2.reshape(-1), final.reshape(-1),
            jnp.concatenate([att.reshape(-1), z(512)]),
            jnp.concatenate([hg.reshape(-1), qn.reshape(-1), kn.reshape(-1), z(1024 - 256)]),
            z(1024) if lb is None else lb.reshape(-1),
            z(1024) if loss is None else jnp.concatenate([loss.reshape(-1), z(1023)]), z(1024)]
    return jnp.stack(rows, axis=0)


def _unpack_small(p):
    return (p[0:1, :], p[1:2, :], p[2, :], p[3:4, 0:512], p[4:5, 0:128], p[4:5, 128:192], p[4:5, 192:256])


def _fold_heads(dhg, dqn, dkn, *, name):
    def body(hg_ref, q_ref, k_ref, ohg_ref, oq_ref, ok_ref):
        def fold128(v):
            acc = v[:, 0:LANES]
            for j in range(1, v.shape[1] // LANES):
                acc = acc + v[:, j * LANES:(j + 1) * LANES]
            return acc

        ohg_ref[...] = fold128(hg_ref[...])
        q = fold128(q_ref[...])
        oq_ref[...] = q + pltpu.roll(q, ATT_DH, 1)
        k = k_ref[...]
        ok_ref[...] = k + pltpu.roll(k, ATT_DH, 1)

    return pl.pallas_call(body, name=name, out_shape=[jax.ShapeDtypeStruct((1, LANES), F32)] * 3)(dhg, dqn, dkn)


def _lb_grad(dlb_sum, lb, *, name):
    def body(d_ref, lb_ref, o_ref):
        lbv = lb_ref[...]
        gl = d_ref[...] * lbv * (1.0 - lbv)
        o_ref[0:1, :] = gl[0:1, :]
        o_ref[1:2, :] = -gl[0:1, :]
        o_ref[2:3, :] = gl[1:2, :]
        o_ref[3:4, :] = -gl[1:2, :]

    return pl.pallas_call(body, name=name, out_shape=jax.ShapeDtypeStruct((4, HG_W), F32))(dlb_sum, lb)


def _lower_bounds(lb_logits_full, *, name):
    def body(l_ref, o_ref):
        for d in range(2):
            l0, l1 = l_ref[2 * d:2 * d + 1, :], l_ref[2 * d + 1:2 * d + 2, :]
            mx = jnp.maximum(l0, l1)
            e0, e1 = jnp.exp(l0 - mx), jnp.exp(l1 - mx)
            o_ref[d:d + 1, :] = e0 / (e0 + e1)

    return pl.pallas_call(body, name=name, out_shape=jax.ShapeDtypeStruct((2, HG_W), F32))(
        lb_logits_full.reshape(4, HG_W))


def _local_step(x, target, norm1_w, w_in_t, lb, hg_norm_w, q_norm_w, k_norm_w, att_norm_w, w_out, norm2_w,
                w_g_t, w_u_t, w_down, final_norm_w, reduce_early=None, reduce_late=None, shards=None):
    T = x.shape[0]
    cos, sin = _rope_tables(T)
    qw8 = jnp.tile(q_norm_w, (1, ATT_HEADS))
    kw2 = jnp.tile(k_norm_w, (1, ATT_KV))

    if shards is None:
        h, r1 = _rms_fwd(x, norm1_w, name="norm1_fwd")
        U = _mm_nn([(h, w_in_t)], trans_b=True, name="in_proj")
        o_f, st_f = _gla_fwd(U, lb[0:1], f_block=1, reverse=False, name="gla_fwd_f")
    else:
        h, r1, g_in, g_lb = _rms_fwd(x, norm1_w, ride=_ride_gather([shards["w_in_t"], shards["lb_logits"]]),
                                     name="norm1_fwd")
        w_in_t = g_in.reshape(-1, D_MODEL)
        lb = _lower_bounds(g_lb.transpose(1, 0, 2).reshape(2, 2, -1), name="lower_bounds")
        U, g_gu = _mm_nn([(h, w_in_t)], trans_b=True, ride=_ride_gather([shards["w_gu_t"]]), name="in_proj")
        o_f, st_f, g_out, g_dn = _gla_fwd(U, lb[0:1], f_block=1, reverse=False,
                                          ride=_ride_gather([shards["w_out"], shards["w_down"]]), name="gla_fwd_f")
        g_gu = g_gu.reshape(2, -1, D_MODEL)
        w_g_t, w_u_t = g_gu[0], g_gu[1]
        w_out, w_down = g_out.reshape(-1, D_MODEL), g_dn.reshape(-1, D_MODEL)
    o_sum, st_b, mix_hg = _gla_fwd(U, lb[1:2], f_block=2, reverse=True, post=(o_f, hg_norm_w), name="gla_fwd_b")
    q_c, qn_c, kmax2, k_c, v_c = _att_prep_fwd(U, cos, sin, qw8, kw2, name="att_prep_fwd")
    kmax = jnp.sqrt(jnp.max(kmax2.reshape(ATT_KV, ATT_DH), axis=1))
    m_c = qn_c * (kmax * 1.001).reshape(ATT_KV, 1, 1, 1)
    o_c, lse = lax.cond(jnp.max(m_c) <= FA_BOUND_MAX,
                        lambda: _flash_fwd_bounded(q_c, k_c, v_c, m_c, name="flash_fwd_bounded"),
                        lambda: _flash_fwd(q_c, k_c, v_c, name="flash_fwd"))
    o_att, mix_att = _att_post_fwd(o_c, att_norm_w, name="att_post_fwd")
    x1, h2, r2 = _mm_nn([(mix_hg, w_out[:HG_W]), (mix_att, w_out[HG_W:])], residual=x, tail=_tail_rms_fwd(norm2_w),
                        name="out_proj")
    gate, up, act = _ffn_up(h2, w_g_t, w_u_t, name="ffn_up")
    loss, dx2, dx2b, d_final = _mm_nn([(act, w_down)], residual=x1,
                                      tail=_tail_loss(target, final_norm_w.reshape(1, D_MODEL)), name="ffn_down")

    d_gate, d_up = _ffn_act_bwd(dx2b, w_down, gate, up, name="ffn_act_bwd")
    dw_down = _mm_tn(act, dx2b, tma_cap=1408, name="dw_down")
    dw_g = _mm_tn(d_gate, h2, tma_cap=1408, name="dw_gate")
    dw_u = _mm_tn(d_up, h2, tma_cap=1408, name="dw_up")
    mine = None if reduce_early is None else reduce_early["slabs"](dw_g, dw_u, dw_down)
    dx1, dx1b, d_norm2, *theirs = _mm_nn([(d_gate, w_g_t), (d_up, w_u_t)], tm=256,
                                         ride=None if mine is None else _ride_swap(mine),
                                         tail=_tail_rms_bwd(x1, r2, norm2_w, dx2, emit_bf16=True), name="ffn_up_bwd")
    dmix = _mm_nn([(dx1b, w_out)], trans_b=True, name="out_proj_bwd")
    dw_out = _mm_tn(mix_att, dx1b, rows=(HG_W, D_MODEL), name="dw_out_att",
                    into=_mm_tn(mix_hg, dx1b, rows=(0, D_MODEL), name="dw_out_hg"))
    do_c, delta, d_att = _att_post_bwd(dmix, o_att, att_norm_w, name="att_post_bwd")
    ride = None if reduce_early is None else _ride_chips(reduce_early["sums"](mine, theirs, dw_out))
    dq_c, dk_c, dv_c, *rode = _flash_bwd(q_c, k_c, v_c, do_c, lse, delta, ride=ride, name="flash_bwd")
    dU_att, d_qn, d_kn = _att_prep_bwd(U, dq_c, dk_c, dv_c, cos, sin, qw8, kw2, name="att_prep_bwd")
    do_hg, du_g, d_hg = _hg_post_bwd(dmix, o_sum, U, hg_norm_w, name="hg_post_bwd")
    dq_f, dz_f, dv_f, dlb_f = _gla_bwd(U, lb[0:1], do_hg, st_f, f_block=1, reverse=False, name="gla_bwd_f")
    dU_hg, dlb_b = _gla_bwd(U, lb[1:2], do_hg, st_b, f_block=2, reverse=True, prev=(dq_f, dz_f, dv_f, du_g),
                            name="gla_bwd_b")
    w_hg = 5 * HG_W
    n_in = w_hg + dU_att.shape[1]
    dw_in = _mm_tn(dU_att, h, tma_cap=256, rows=(w_hg, n_in), name="dw_in_att",
                   into=_mm_tn(dU_hg, h, tma_cap=1280, rows=(0, n_in), name="dw_in_hg"))
    late = None if reduce_late is None else _ride_chips(reduce_late(dw_in))
    grad_x, d_norm1, *rode_late = _mm_nn([(dU_hg, w_in_t[:w_hg]), (dU_att, w_in_t[w_hg:])], ride=late,
                                         tail=_tail_rms_bwd(x, r1, norm1_w, dx1, emit_bf16=False), name="in_proj_bwd")
    d_hg, d_qn, d_kn = _fold_heads(d_hg, d_qn, d_kn, name="fold_heads")

    big = dict(w_in=dw_in, w_out=dw_out, w_g=dw_g, w_u=dw_u, w_down=dw_down)
    small = dict(norm1=d_norm1, norm2=d_norm2, final=d_final, att=d_att, hg=d_hg,
                 qn=d_qn[:, :ATT_DH], kn=d_kn[:, :ATT_DH], lb=jnp.concatenate([dlb_f, dlb_b], axis=0))
    return loss, grad_x, big, small, rode + rode_late, lb


def kernel(x, norm1_w, w_in, lb_logits, hg_norm_w, q_norm_w, k_norm_w, att_norm_w, w_out, norm2_w, w_gate_up, w_down, final_norm_w, loss_target, m_norm1_w, m_w_in, m_lb_logits, m_hg_norm_w, m_q_norm_w, m_k_norm_w, m_att_norm_w, m_w_out, m_norm2_w, m_w_gate_up, m_w_down, m_final_norm_w, v_norm1_w, v_w_in, v_lb_logits, v_hg_norm_w, v_q_norm_w, v_k_norm_w, v_att_norm_w, v_w_out, v_norm2_w, v_w_gate_up, v_w_down, v_final_norm_w):
    T = x.shape[1]
    me = 4 * lax.axis_index("x") + 2 * lax.axis_index("y") + lax.axis_index("c")
    c_in, r_out, c_gu, r_dn = w_in.shape[2], w_out.shape[1], w_gate_up.shape[2], w_down.shape[1]
    lb_cols = lb_logits.shape[2]

    shards = dict(w_in_t=w_in[0].T.astype(BF16), lb_logits=lb_logits.reshape(4, lb_cols),
                  w_gu_t=w_gate_up[0].T.astype(BF16), w_out=w_out[0].astype(BF16), w_down=w_down[0].astype(BF16))

    chips = N_DEV // 2
    core = lax.axis_index("c").astype(jnp.int32).reshape(1)
    by_owner = lambda g, r: g.reshape(chips, 2, r, D_MODEL)

    def pair_sums(mine, theirs, names):
        return [_pair_sum(g, o, core, name="pair_sum_" + nm) for g, o, nm in zip(mine, theirs, names)]

    def early_slabs(dw_g_t, dw_u_t, dw_down):
        half = lambda g: g.reshape(chips // 2, 2, c_gu, D_MODEL)
        return [half(dw_g_t), half(dw_u_t), by_owner(dw_down, r_dn)]

    def early_sums(mine, theirs, dw_out):
        s_out = by_owner(dw_out, r_out)
        c_out, c_g, c_u, c_dn = pair_sums([s_out] + mine, list(_core_swap([s_out], name="exchange_cores_out"))
                                          + list(theirs), ("w_out", "w_gate", "w_up", "w_down"))
        return [c_out, jnp.concatenate([c_g, c_u], axis=0), c_dn]

    def reduce_late(dw_in_t):
        mine = [by_owner(dw_in_t, c_in)]
        return pair_sums(mine, _core_swap(mine, name="exchange_cores_in"), ("w_in",))

    loss, grad_x, big, small, (p_out, p_gu, p_dn, p_in), lb = _local_step(
        x[0], loss_target[0], norm1_w, None, None, hg_norm_w, q_norm_w, k_norm_w, att_norm_w, None, norm2_w,
        None, None, None, final_norm_w, reduce_early=dict(slabs=early_slabs, sums=early_sums),
        reduce_late=reduce_late, shards=shards)
    p_gu, p_in = p_gu.transpose(0, 2, 1), p_in.transpose(0, 2, 1)

    packed = _pack_small(small["norm1"], small["norm2"], small["final"], small["att"], small["hg"],
                         small["qn"], small["kn"], small["lb"], loss)
    all_small = _send_to_all(packed, name="exchange_small")

    g_w_in, d_w_in, nm_w_in, nv_w_in = _adamw(p_in, w_in[0], m_w_in[0], v_w_in[0], name="adamw_w_in")
    g_w_out, d_w_out, nm_w_out, nv_w_out = _adamw(p_out, w_out[0], m_w_out[0], v_w_out[0], name="adamw_w_out")
    g_w_gu, d_w_gu, nm_w_gu, nv_w_gu = _adamw(p_gu, w_gate_up[0], m_w_gate_up[0], v_w_gate_up[0], name="adamw_w_gu")
    g_w_dn, d_w_dn, nm_w_dn, nv_w_dn = _adamw(p_dn, w_down[0], m_w_down[0], v_w_down[0], name="adamw_w_down")

    pk = lambda vecs: _pack_small(*vecs)
    w_pk = pk([norm1_w, norm2_w, final_norm_w, att_norm_w, hg_norm_w, q_norm_w, k_norm_w])
    m_pk = pk([m_norm1_w, m_norm2_w, m_final_norm_w, m_att_norm_w, m_hg_norm_w, m_q_norm_w, m_k_norm_w])
    v_pk = pk([v_norm1_w, v_norm2_w, v_final_norm_w, v_att_norm_w, v_hg_norm_w, v_q_norm_w, v_k_norm_w])
    g_pk, d_pk, nm_pk, nv_pk = _adamw(all_small, w_pk, m_pk, v_pk, name="adamw_small")

    dlb_sum = g_pk[5:6, :].reshape(2, HG_W)
    g_lb_full = _lb_grad(dlb_sum, lb, name="lb_grad")
    g_lb_mine = lax.dynamic_slice_in_dim(g_lb_full, me * lb_cols, lb_cols, axis=1)
    g_lb_s, d_lb, nm_lb, nv_lb = _adamw(g_lb_mine[None], lb_logits.reshape(4, lb_cols),
                                        m_lb_logits.reshape(4, lb_cols), v_lb_logits.reshape(4, lb_cols),
                                        name="adamw_lb")

    loss_total = g_pk[6, 0]

    def outs(big4, lb_arr, pk_arr):
        n1, n2, fin, att, hg, qn, kn = _unpack_small(pk_arr)
        b_in, b_out, b_gu, b_dn = big4
        return [n1, b_in[None], lb_arr.reshape(2, 2, lb_cols), hg, qn, kn, att, b_out[None], n2, b_gu[None],
                b_dn[None], fin]

    return (loss_total, grad_x[None],
            *outs((g_w_in, g_w_out, g_w_gu, g_w_dn), g_lb_s, g_pk),
            *outs((d_w_in, d_w_out, d_w_gu, d_w_dn), d_lb, d_pk),
            *outs((nm_w_in, nm_w_out, nm_w_gu, nm_w_dn), nm_lb, nm_pk),
            *outs((nv_w_in, nv_w_out, nv_w_gu, nv_w_dn), nv_lb, nv_pk))
```

```python
import math

import jax
import jax.numpy as jnp
import numpy as np
from jax import lax
from jax.experimental import pallas as pl
from jax.experimental.pallas import tpu as pltpu

F32 = jnp.float32
BF16 = jnp.bfloat16

N_DEV = 8
D_MODEL = 1024
EPS = 1e-6
HG_HEADS = 4
HG_D = 128
HG_W = HG_HEADS * HG_D
CHUNK = 64
ATT_HEADS = 8
ATT_KV = 2
ATT_G = ATT_HEADS // ATT_KV
ATT_DH = 64
ATT_QW = ATT_HEADS * ATT_DH
ATT_KW = ATT_KV * ATT_DH
GRID_W = 64
ROPE_THETA = 10000.0
D_FF = 2816
ADAM_LR, ADAM_B1, ADAM_B2, ADAM_EPS, ADAM_WD, ADAM_STEP = 0.001, 0.9, 0.999, 1e-08, 0.01, 10

LOG2E = math.log2(math.e)
LANES = 128
VMEM_LIMIT = 48 * 1024 * 1024
MESH = pl.DeviceIdType.MESH
ANY = pl.BlockSpec(memory_space=pl.ANY)


def _params(sem=None):
    return pltpu.CompilerParams(dimension_semantics=sem, vmem_limit_bytes=VMEM_LIMIT)


def _pick(n, cap):
    best = None
    for t in range(LANES, cap + 1, LANES):
        if n % t == 0:
            best = t
    assert best is not None, (n, cap)
    return best


def _sigmoid(x):
    return 1.0 / (1.0 + jnp.exp(-x))


def _dot(a, b):
    return jnp.dot(a.astype(BF16), b.astype(BF16), preferred_element_type=F32)


def _dot_nt(a, b):
    return lax.dot_general(a.astype(BF16), b.astype(BF16), (((1,), (1,)), ((), ())),
                           preferred_element_type=F32)


def _dot_tn(a, b):
    return lax.dot_general(a.astype(BF16), b.astype(BF16), (((0,), (0,)), ((), ())),
                           preferred_element_type=F32)


def _mm_nn(pairs, *, name, out_dtype=F32, residual=None, tm=512, tn_cap=None, trans_b=False, tail=None, ride=None):
    M = pairs[0][0].shape[0]
    N = pairs[0][1].shape[0 if trans_b else 1]
    tn = N if tn_cap is None else _pick(N, tn_cap)
    n_pairs = len(pairs)
    has_res = residual is not None
    dims = (((1,), (1,)), ((), ())) if trans_b else (((1,), (0,)), ((), ()))
    assert (tail is None and ride is None) or tn == N
    n_main = 2 * n_pairs + has_res
    n_ti = 0 if tail is None else len(tail["ins"])
    n_out = 1 if tail is None else len(tail["outs"])
    n_r = 0 if ride is None else len(ride["arrays"])
    n_in = n_main + n_ti + n_r

    def body(*refs):
        outs = refs[n_in:n_in + n_out]
        if n_r:
            start, finish = ride["halves"](refs[n_main + n_ti:n_in], refs[n_in + n_out:n_in + n_out + n_r],
                                           *refs[n_in + n_out + n_r:])
            pl.when(pl.program_id(0) == 0)(start)
        acc = None
        for i in range(n_pairs):
            d = lax.dot_general(refs[2 * i][...], refs[2 * i + 1][...], dims, preferred_element_type=F32)
            acc = d if acc is None else acc + d
        if has_res:
            acc = acc + refs[2 * n_pairs][...]
        if tail is None:
            outs[0][...] = acc.astype(out_dtype)
        else:
            tail["fn"](acc, pl.program_id(0) == 0, *refs[n_main:n_main + n_ti], *outs)
        if n_r:
            pl.when(pl.program_id(0) == M // tm - 1)(finish)

    kinds = {"row": ((tm, N), (M, N), lambda i, j: (i, 0)), "col": ((tm, 1), (M, 1), lambda i, j: (i, 0)),
             "vec": ((1, N), (1, N), lambda i, j: (0, 0)), "one": ((1, 1), (1, 1), lambda i, j: (0, 0))}
    in_specs, args = [], []
    for a, b in pairs:
        k = a.shape[1]
        b_spec = pl.BlockSpec((tn, k), lambda i, j: (j, 0)) if trans_b else pl.BlockSpec((k, tn), lambda i, j: (0, j))
        in_specs += [pl.BlockSpec((tm, k), lambda i, j: (i, 0)), b_spec]
        args += [a, b]
    if has_res:
        in_specs.append(pl.BlockSpec((tm, tn), lambda i, j: (i, j)))
        args.append(residual)
    if tail is None:
        out_specs = [pl.BlockSpec((tm, tn), lambda i, j: (i, j))]
        out_shape = [jax.ShapeDtypeStruct((M, N), out_dtype)]
    else:
        for arr, kind in tail["ins"]:
            in_specs.append(pl.BlockSpec(kinds[kind][0], kinds[kind][2]))
            args.append(arr)
        out_specs = [pl.BlockSpec(kinds[kind][0], kinds[kind][2]) for _, kind in tail["outs"]]
        out_shape = [jax.ShapeDtypeStruct(kinds[kind][1], dt) for dt, kind in tail["outs"]]
    scratch = []
    if n_r:
        in_specs += [ANY] * n_r
        args += ride["arrays"]
        out_specs += [ANY] * n_r
        out_shape += ride["out_shape"]
        scratch = ride["scratch"]
    sequential = tail is not None or n_r > 0
    res = pl.pallas_call(
        body, name=name, grid=(M // tm, N // tn), in_specs=in_specs, out_specs=out_specs, out_shape=out_shape,
        scratch_shapes=scratch,
        compiler_params=pltpu.CompilerParams(dimension_semantics=("arbitrary" if sequential else "parallel", "arbitrary"),
                                             vmem_limit_bytes=VMEM_LIMIT, has_side_effects=n_r > 0),
    )(*args)
    return res[0] if len(res) == 1 else res


def _mm_tn(a, b, *, name, tma_cap=1024, tnb_cap=1024, tk=2048, rows=None, into=None):
    T, Ma = a.shape
    Nb = b.shape[1]
    tma, tnb = _pick(Ma, tma_cap), _pick(Nb, tnb_cap)
    tk = min(tk, T)
    n_k = T // tk
    first_row, total = (0, Ma) if rows is None else rows
    assert first_row % tma == 0
    i0 = first_row // tma

    def body(a_ref, b_ref, *rest):
        o_ref, acc_ref = rest[-2:]
        k = pl.program_id(2)

        @pl.when(k == 0)
        def _():
            acc_ref[...] = jnp.zeros_like(acc_ref)

        acc_ref[...] += lax.dot_general(a_ref[...], b_ref[...], (((0,), (0,)), ((), ())),
                                        preferred_element_type=F32)

        @pl.when(k == n_k - 1)
        def _():
            o_ref[...] = acc_ref[...]

    in_specs = [pl.BlockSpec((tk, tma), lambda i, j, k: (k, i)), pl.BlockSpec((tk, tnb), lambda i, j, k: (k, j))]
    args = [a, b]
    if into is not None:
        in_specs.append(ANY)
        args.append(into)
    return pl.pallas_call(
        body, name=name, grid=(Ma // tma, Nb // tnb, n_k), in_specs=in_specs,
        out_specs=pl.BlockSpec((tma, tnb), lambda i, j, k: (i0 + i, j)),
        out_shape=jax.ShapeDtypeStruct((total, Nb), F32),
        scratch_shapes=[pltpu.VMEM((tma, tnb), F32)],
        input_output_aliases={} if into is None else {2: 0},
        compiler_params=_params(("parallel", "parallel", "arbitrary")),
    )(*args)


def _rms_fwd(x, w, *, name, tm=512, ride=None):
    T, Dm = x.shape
    n_r = 0 if ride is None else len(ride["arrays"])

    def body(x_ref, w_ref, *rest):
        h_ref, r_ref = rest[n_r:n_r + 2]
        if n_r:
            start, finish = ride["halves"](rest[:n_r], rest[n_r + 2:2 * n_r + 2], *rest[2 * n_r + 2:])
            pl.when(pl.program_id(0) == 0)(start)
        xv = x_ref[...]
        r = lax.rsqrt(jnp.mean(xv * xv, axis=-1, keepdims=True) + EPS)
        h_ref[...] = (xv * r * w_ref[...]).astype(BF16)
        r_ref[...] = r
        if n_r:
            pl.when(pl.program_id(0) == T // tm - 1)(finish)

    return pl.pallas_call(
        body, name=name, grid=(T // tm,),
        in_specs=[pl.BlockSpec((tm, Dm), lambda i: (i, 0)), pl.BlockSpec((1, Dm), lambda i: (0, 0))] + [ANY] * n_r,
        out_specs=[pl.BlockSpec((tm, Dm), lambda i: (i, 0)), pl.BlockSpec((tm, 1), lambda i: (i, 0))] + [ANY] * n_r,
        out_shape=[jax.ShapeDtypeStruct((T, Dm), BF16), jax.ShapeDtypeStruct((T, 1), F32)]
                  + (ride["out_shape"] if n_r else []),
        scratch_shapes=ride["scratch"] if n_r else [],
        compiler_params=pltpu.CompilerParams(dimension_semantics=("arbitrary" if n_r else "parallel",),
                                             vmem_limit_bytes=VMEM_LIMIT, has_side_effects=n_r > 0),
    )(x, w, *(ride["arrays"] if n_r else []))


def _tail_rms_fwd(w):
    def fn(xv, first, w_ref, x_ref, h_ref, r_ref):
        r = lax.rsqrt(jnp.mean(xv * xv, axis=-1, keepdims=True) + EPS)
        x_ref[...] = xv
        h_ref[...] = (xv * r * w_ref[...]).astype(BF16)
        r_ref[...] = r

    return dict(fn=fn, ins=[(w, "vec")], outs=[(F32, "row"), (BF16, "row"), (F32, "col")])


def _tail_rms_bwd(x, r, w, dres, *, emit_bf16):
    def fn(dhv, first, x_ref, r_ref, w_ref, dres_ref, *outs):
        dx_ref, dw_ref = outs[0], outs[-1]

        @pl.when(first)
        def _():
            dw_ref[...] = jnp.zeros_like(dw_ref)

        rv = r_ref[...]
        xh = x_ref[...] * rv
        dxh = dhv * w_ref[...]
        t = jnp.mean(dxh * xh, axis=-1, keepdims=True)
        dx = dres_ref[...] + rv * (dxh - xh * t)
        dx_ref[...] = dx
        if emit_bf16:
            outs[1][...] = dx.astype(BF16)
        dw_ref[...] += jnp.sum(dhv * xh, axis=0, keepdims=True)

    outs = [(F32, "row")] + ([(BF16, "row")] if emit_bf16 else []) + [(F32, "vec")]
    return dict(fn=fn, ins=[(x, "row"), (r, "col"), (w, "vec"), (dres, "row")], outs=outs)


def _tail_loss(target, w):
    def fn(xv, first, t_ref, w_ref, loss_ref, dx_ref, dxb_ref, dw_ref):
        @pl.when(first)
        def _():
            loss_ref[...] = jnp.zeros_like(loss_ref)
            dw_ref[...] = jnp.zeros_like(dw_ref)

        r = lax.rsqrt(jnp.mean(xv * xv, axis=-1, keepdims=True) + EPS)
        xh = xv * r
        wv = w_ref[...]
        err = xh * wv - t_ref[...]
        row_loss = jnp.mean(err * err, axis=-1, keepdims=True)
        loss_ref[...] += 0.5 * jnp.sum(row_loss, axis=0, keepdims=True)
        dy = err * (1.0 / xv.shape[-1])
        dxh = dy * wv
        t = jnp.mean(dxh * xh, axis=-1, keepdims=True)
        dx = r * (dxh - xh * t)
        dx_ref[...] = dx
        dxb_ref[...] = dx.astype(BF16)
        dw_ref[...] += jnp.sum(dy * xh, axis=0, keepdims=True)

    return dict(fn=fn, ins=[(target, "row"), (w, "vec")],
                outs=[(F32, "one"), (F32, "row"), (BF16, "row"), (F32, "vec")])


GLA_TB = 512
GLA_NC = GLA_TB // CHUNK
GLA_UNROLL = 4


def _cumsum_rows(x, row, reverse):
    n = x.shape[0]
    s = 1
    while s < n:
        if not reverse:
            x = x + jnp.where(row >= s, pltpu.roll(x, s, 0), 0.0)
        else:
            x = x + jnp.where(row < n - s, pltpu.roll(x, n - s, 0), 0.0)
        s *= 2
    return x


def _gla_gates(uq, z, lbv):
    q = uq * _sigmoid(uq)
    sg = _sigmoid(z)
    sgn = _sigmoid(-z)
    f = lbv + (1.0 - lbv) * sg
    k = (1.0 - lbv) * sgn
    return q, sg, sgn, f, k


def _gla_decays(f, row, reverse):
    b = _cumsum_rows(jnp.log(f), row, reverse)
    if not reverse:
        bref, blast = b[CHUNK // 2 - 1:CHUNK // 2, :], b[CHUNK - 1:CHUNK, :]
    else:
        bref, blast = b[CHUNK // 2:CHUNK // 2 + 1, :], b[0:1, :]
    return b, bref, blast


def _gla_fwd(U, lb, *, f_block, reverse, name, ride=None, post=None):
    T = U.shape[0]
    nb = T // GLA_TB
    n_g = 0 if ride is None else len(ride["arrays"])
    n_p = 0 if post is None else 3

    def body(uq_ref, uf_ref, ui_ref, lb_ref, *rest):
        post_in, rest = rest[:n_p], rest[n_p:]
        g_in, rest = rest[:n_g], rest[n_g:]
        o_ref, st_ref = rest[:2]
        mix_ref = rest[2] if n_p else None
        rest = rest[2 + (n_p > 0):]
        g_out, rest = rest[:n_g], rest[n_g:]
        s_ref = rest[0]
        if n_g:
            start, finish = ride["halves"](g_in, g_out, *rest[1:])
            pl.when(pl.program_id(0) == 0)(start)

        @pl.when(pl.program_id(0) == 0)
        def _():
            s_ref[...] = jnp.zeros_like(s_ref)

        row = lax.broadcasted_iota(jnp.int32, (CHUNK, HG_D), 0)
        ri = lax.broadcasted_iota(jnp.int32, (CHUNK, CHUNK), 0)
        ci = lax.broadcasted_iota(jnp.int32, (CHUNK, CHUNK), 1)
        mask = (ri <= ci) if reverse else (ri >= ci)

        def chunk(j, carry):
            c = (GLA_NC - 1 - j) if reverse else j
            rows = pl.ds(pl.multiple_of(c * CHUNK, CHUNK), CHUNK)
            for h in range(HG_HEADS):
                cols = pl.ds(h * HG_D, HG_D)
                v = ui_ref[rows, cols]
                q, _, _, f, k = _gla_gates(uq_ref[rows, cols], uf_ref[rows, cols], lb_ref[:, cols])
                b, bref, blast = _gla_decays(f, row, reverse)
                s = jnp.where(mask, _dot_nt(q * jnp.exp(b - bref), k * jnp.exp(bref - b)), 0.0)
                st = s_ref[h]
                st_ref[c, h] = st
                o = _dot(s, v) + _dot_nt(q * jnp.exp(b), st)
                if n_p:
                    other_ref, ug_ref, w_ref = post_in
                    o = o + other_ref[rows, cols]
                    r = lax.rsqrt(jnp.mean(o * o, axis=-1, keepdims=True) + EPS)
                    ug = ug_ref[rows, cols]
                    mix_ref[rows, cols] = (o * r * w_ref[...] * (ug * _sigmoid(ug))).astype(BF16)
                o_ref[rows, cols] = o
                s_ref[h] = st * jnp.exp(blast) + _dot_tn(v, k * jnp.exp(blast - b))
            return carry

        lax.fori_loop(0, GLA_NC, chunk, 0, unroll=GLA_NC)
        if n_g:
            pl.when(pl.program_id(0) == nb - 1)(finish)

    blk = (lambda i: nb - 1 - i) if reverse else (lambda i: i)
    ucol = lambda cb: pl.BlockSpec((GLA_TB, HG_W), lambda i: (blk(i), cb))
    tok = pl.BlockSpec((GLA_TB, HG_W), lambda i: (blk(i), 0))
    in_specs = [ucol(0), ucol(f_block), ucol(3), pl.BlockSpec((1, HG_W), lambda i: (0, 0))]
    args = [U, U, U, lb]
    out_specs = [tok, pl.BlockSpec((GLA_NC, HG_HEADS, HG_D, HG_D), lambda i: (blk(i), 0, 0, 0))]
    out_shape = [jax.ShapeDtypeStruct((T, HG_W), F32), jax.ShapeDtypeStruct((T // CHUNK, HG_HEADS, HG_D, HG_D), F32)]
    if n_p:
        in_specs += [tok, ucol(4), pl.BlockSpec((1, HG_D), lambda i: (0, 0))]
        args += [post[0], U, post[1]]
        out_specs.append(tok)
        out_shape.append(jax.ShapeDtypeStruct((T, HG_W), BF16))
    return pl.pallas_call(
        body, name=name, grid=(nb,), in_specs=in_specs + [ANY] * n_g, out_specs=out_specs + [ANY] * n_g,
        out_shape=out_shape + (ride["out_shape"] if n_g else []),
        scratch_shapes=[pltpu.VMEM((HG_HEADS, HG_D, HG_D), F32)] + (ride["scratch"] if n_g else []),
        compiler_params=pltpu.CompilerParams(dimension_semantics=("arbitrary",), vmem_limit_bytes=VMEM_LIMIT,
                                             has_side_effects=bool(n_g)),
    )(*args, *(ride["arrays"] if n_g else []))


def _gla_bwd(U, lb, do, states, *, f_block, reverse, name, prev=None):
    T = U.shape[0]
    nb = T // GLA_TB
    final = prev is not None

    def body(uq_ref, uf_ref, ui_ref, lb_ref, do_ref, st_ref, *rest):
        if final:
            dqp_ref, dzp_ref, dvp_ref, dug_ref, out_ref, dlb_ref, ds_ref = rest
        else:
            dq_ref, dz_ref, dv_ref, dlb_ref, ds_ref = rest

        @pl.when(pl.program_id(0) == 0)
        def _():
            ds_ref[...] = jnp.zeros_like(ds_ref)
            dlb_ref[...] = jnp.zeros_like(dlb_ref)

        row = lax.broadcasted_iota(jnp.int32, (CHUNK, HG_D), 0)
        ri = lax.broadcasted_iota(jnp.int32, (CHUNK, CHUNK), 0)
        ci = lax.broadcasted_iota(jnp.int32, (CHUNK, CHUNK), 1)
        mask = (ri <= ci) if reverse else (ri >= ci)

        def chunk(j, carry):
            c = j if reverse else (GLA_NC - 1 - j)
            rows = pl.ds(pl.multiple_of(c * CHUNK, CHUNK), CHUNK)
            for h in range(HG_HEADS):
                cols = pl.ds(h * HG_D, HG_D)
                v = ui_ref[rows, cols]
                lbv = lb_ref[:, cols]
                uq = uq_ref[rows, cols]
                q, sg, sgn, f, k = _gla_gates(uq, uf_ref[rows, cols], lbv)
                b, bref, blast = _gla_decays(f, row, reverse)
                eq, ek, eb, el, dec = (jnp.exp(b - bref), jnp.exp(bref - b), jnp.exp(b), jnp.exp(blast - b),
                                       jnp.exp(blast))
                qin, kin, qb, klast = q * eq, k * ek, q * eb, k * el
                dov = do_ref[rows, cols]
                st = st_ref[c, h]
                dst = ds_ref[h]
                p = jnp.where(mask, _dot_nt(qin, kin), 0.0)
                dp = jnp.where(mask, _dot_nt(dov, v), 0.0)
                dqin = _dot(dp, kin)
                dkin = _dot_tn(dp, qin)
                dv = _dot_tn(p, dov) + _dot_nt(klast, dst)
                dqb = _dot(dov, st)
                dklast = _dot(v, dst)
                ds_ref[h] = _dot_tn(dov, qb) + dst * dec
                db = dqin * qin - dkin * kin + dqb * qb - dklast * klast
                extra = (jnp.sum(dklast * klast, axis=0, keepdims=True)
                         + dec * jnp.sum(st * dst, axis=0, keepdims=True))
                dg = _cumsum_rows(db, row, not reverse) + extra
                dq = dqin * eq + dqb * eb
                dk = dkin * ek + dklast * el
                dfk = dg / f - dk
                dz = (dfk * (1.0 - lbv) * sg * sgn).astype(BF16)
                dlb_ref[:, cols] += jnp.sum(dfk * sgn, axis=0, keepdims=True)
                if final:
                    sq = _sigmoid(uq)
                    col = lambda blk: pl.ds(blk * HG_W + h * HG_D, HG_D)
                    out_ref[rows, col(0)] = ((dq + dqp_ref[rows, cols]) * (sq * (1.0 + uq * (1.0 - sq)))).astype(BF16)
                    out_ref[rows, col(1)] = dzp_ref[rows, cols]
                    out_ref[rows, col(2)] = dz
                    out_ref[rows, col(3)] = (dv + dvp_ref[rows, cols]).astype(BF16)
                    out_ref[rows, col(4)] = dug_ref[rows, cols]
                else:
                    dq_ref[rows, cols] = dq
                    dz_ref[rows, cols] = dz
                    dv_ref[rows, cols] = dv
            return carry

        lax.fori_loop(0, GLA_NC, chunk, 0, unroll=GLA_UNROLL)

    blk = (lambda i: i) if reverse else (lambda i: nb - 1 - i)
    ucol = lambda cb: pl.BlockSpec((GLA_TB, HG_W), lambda i: (blk(i), cb))
    tok = pl.BlockSpec((GLA_TB, HG_W), lambda i: (blk(i), 0))
    vec = pl.BlockSpec((1, HG_W), lambda i: (0, 0))
    in_specs = [ucol(0), ucol(f_block), ucol(3), vec, tok,
                pl.BlockSpec((GLA_NC, HG_HEADS, HG_D, HG_D), lambda i: (blk(i), 0, 0, 0))]
    vec_shape = jax.ShapeDtypeStruct((1, HG_W), F32)
    if final:
        in_specs += [tok] * 4
        out_specs = [pl.BlockSpec((GLA_TB, 5 * HG_W), lambda i: (blk(i), 0)), vec]
        out_shape = [jax.ShapeDtypeStruct((T, 5 * HG_W), BF16), vec_shape]
    else:
        out_specs = [tok, tok, tok, vec]
        out_shape = [jax.ShapeDtypeStruct((T, HG_W), F32), jax.ShapeDtypeStruct((T, HG_W), BF16),
                     jax.ShapeDtypeStruct((T, HG_W), F32), vec_shape]
    return pl.pallas_call(
        body, name=name, grid=(nb,), in_specs=in_specs, out_specs=out_specs, out_shape=out_shape,
        scratch_shapes=[pltpu.VMEM((HG_HEADS, HG_D, HG_D), F32)],
        compiler_params=_params(("arbitrary",)),
    )(U, U, U, lb, do, states, *(prev if final else ()))


def _hg_post_bwd(dmix, o_sum, U, w, *, name, tm=512):
    T = o_sum.shape[0]

    def body(dm_ref, o_ref, ug_ref, w_ref, do_ref, dug_ref, dw_ref):
        @pl.when(pl.program_id(0) == 0)
        def _():
            dw_ref[...] = jnp.zeros_like(dw_ref)

        wv = w_ref[...]
        for h in range(HG_HEADS):
            cols = pl.ds(h * HG_D, HG_D)
            o = o_ref[:, cols]
            r = lax.rsqrt(jnp.mean(o * o, axis=-1, keepdims=True) + EPS)
            xh = o * r
            ug = ug_ref[:, cols]
            sg = _sigmoid(ug)
            dm = dm_ref[:, cols]
            dn = dm * (ug * sg)
            dug_ref[:, cols] = (dm * (xh * wv) * (sg * (1.0 + ug * (1.0 - sg)))).astype(BF16)
            dxh = dn * wv
            t = jnp.mean(dxh * xh, axis=-1, keepdims=True)
            do_ref[:, cols] = r * (dxh - xh * t)
            dw_ref[:, cols] += jnp.sum(dn * xh, axis=0, keepdims=True)

    tok = pl.BlockSpec((tm, HG_W), lambda i: (i, 0))
    vec = pl.BlockSpec((1, HG_W), lambda i: (0, 0))
    return pl.pallas_call(
        body, name=name, grid=(T // tm,),
        in_specs=[tok, tok, pl.BlockSpec((tm, HG_W), lambda i: (i, 4)), pl.BlockSpec((1, HG_D), lambda i: (0, 0))],
        out_specs=[tok, tok, vec],
        out_shape=[jax.ShapeDtypeStruct((T, HG_W), F32), jax.ShapeDtypeStruct((T, HG_W), BF16),
                   jax.ShapeDtypeStruct((1, HG_W), F32)],
        compiler_params=_params(("arbitrary",)),
    )(dmix, o_sum, U, w)


def _rope_tables(T):
    rows = T // GRID_W
    row = np.repeat(np.arange(rows), GRID_W).astype(np.float32)
    col = np.tile(np.arange(GRID_W), rows).astype(np.float32)
    axis_dim = ATT_DH // 2
    freqs = (np.float32(ROPE_THETA) ** (-np.arange(0, axis_dim, 2, dtype=np.float32) / np.float32(axis_dim))
             ).astype(np.float32)
    ang = np.concatenate([row[:, None] * freqs, col[:, None] * freqs], axis=-1).astype(np.float32)
    cos, sin = np.cos(ang), np.sin(ang)
    c = np.repeat(cos, 2, axis=-1)
    s = np.stack([-sin, sin], axis=-1).reshape(T, ATT_DH)
    return jnp.asarray(np.tile(c, (1, 2)), F32), jnp.asarray(np.tile(s, (1, 2)), F32)


def _head_blockdiag(width):
    shift = ATT_DH.bit_length() - 1
    ri = jnp.right_shift(lax.broadcasted_iota(jnp.int32, (width, width), 0), shift)
    ci = jnp.right_shift(lax.broadcasted_iota(jnp.int32, (width, width), 1), shift)
    return jnp.where(ri == ci, 1.0, 0.0).astype(BF16)


def _head_sum(x, bd):
    hi = x.astype(BF16)
    lo = (x - hi.astype(F32)).astype(BF16)
    return jnp.dot(hi, bd, preferred_element_type=F32) + jnp.dot(lo, bd, preferred_element_type=F32)


def _pair_swap(x, even):
    n = x.shape[-1]
    return jnp.where(even, pltpu.roll(x, n - 1, 1), pltpu.roll(x, 1, 1))


FA_TQ = 512


FA_TK = 512


def _cols_from_tokens(x, kv):
    w = ATT_G * ATT_DH
    xt = x[:, kv * w:(kv + 1) * w].T
    return jnp.concatenate([xt[g * ATT_DH:(g + 1) * ATT_DH, :] for g in range(ATT_G)], axis=1)


def _tokens_from_cols(c):
    tq = c.shape[1] // ATT_G
    return jnp.concatenate([c[:, g * tq:(g + 1) * tq] for g in range(ATT_G)], axis=0).T


def _store_cols(ref, x, norm_ref=None):
    for kv in range(ATT_KV):
        cols = _cols_from_tokens(x, kv).astype(BF16)
        ref[kv, 0] = cols
        if norm_ref is not None:
            cf = cols.astype(F32)
            norm_ref[kv, 0] = jnp.sqrt(jnp.sum(cf * cf, axis=0, keepdims=True))


def _att_prep_fwd(U, cos, sin, qw, kw, *, name):
    T = U.shape[0]
    tm = min(FA_TQ, T)
    R = ATT_G * tm
    scale = ATT_DH ** -0.5

    def head_rows(ref, x):
        xt = x.astype(F32).T
        for kv in range(ATT_KV):
            ref[kv, 0] = xt[kv * ATT_DH:(kv + 1) * ATT_DH, :].astype(BF16)

    def body(aq_ref, ak_ref, av_ref, c_ref, s_ref, qw_ref, kw_ref, q_ref, qn_ref, kmax_ref, kc_ref, vc_ref):
        @pl.when(pl.program_id(0) == 0)
        def _():
            kmax_ref[...] = jnp.zeros_like(kmax_ref)

        bd = _head_blockdiag(ATT_QW)
        c2, s2 = c_ref[...], s_ref[...]
        c8, s8 = jnp.tile(c2, (1, 4)), jnp.tile(s2, (1, 4))

        def norm_rope(x, w, c, s, bdm):
            r = lax.rsqrt(_head_sum(x * x, bdm) * (1.0 / ATT_DH) + EPS)
            y = x * r * w
            even = (lax.broadcasted_iota(jnp.int32, y.shape, 1) & 1) == 0
            return y * c + _pair_swap(y, even) * s

        _store_cols(q_ref, norm_rope(aq_ref[...], qw_ref[...], c8, s8, bd) * (scale * LOG2E), qn_ref)
        kb = norm_rope(ak_ref[...], kw_ref[...], c2, s2, bd[:ATT_KW, :ATT_KW]).astype(BF16)
        kf = kb.astype(F32)
        ksq = _head_sum(kf * kf, bd[:ATT_KW, :ATT_KW])
        kmax_ref[...] = jnp.maximum(kmax_ref[...], jnp.max(ksq, axis=0, keepdims=True))
        head_rows(kc_ref, kb)
        head_rows(vc_ref, av_ref[...].astype(BF16))

    kv_spec = pl.BlockSpec((tm, ATT_KW), lambda i: (i, 0))
    tk = min(FA_TK, T)
    per = tk // tm
    c_spec = pl.BlockSpec((ATT_KV, 1, ATT_DH, tm), lambda i: (0, i // per, 0, i % per))
    c_shape = jax.ShapeDtypeStruct((ATT_KV, T // tk, ATT_DH, tk), BF16)
    return pl.pallas_call(
        body, name=name, grid=(T // tm,),
        in_specs=[pl.BlockSpec((tm, ATT_QW), lambda i: (i, 5)),
                  pl.BlockSpec((tm, ATT_KW), lambda i: (i, 24)), pl.BlockSpec((tm, ATT_KW), lambda i: (i, 25)),
                  kv_spec, kv_spec,
                  pl.BlockSpec((1, ATT_QW), lambda i: (0, 0)), pl.BlockSpec((1, ATT_KW), lambda i: (0, 0))],
        out_specs=[pl.BlockSpec((ATT_KV, 1, ATT_DH, R), lambda i: (0, i, 0, 0)),
                   pl.BlockSpec((ATT_KV, 1, 1, R), lambda i: (0, i, 0, 0)), pl.BlockSpec((1, ATT_KW), lambda i: (0, 0)),
                   c_spec, c_spec],
        out_shape=[jax.ShapeDtypeStruct((ATT_KV, T // tm, ATT_DH, R), BF16),
                   jax.ShapeDtypeStruct((ATT_KV, T // tm, 1, R), F32), jax.ShapeDtypeStruct((1, ATT_KW), F32),
                   c_shape, c_shape],
        compiler_params=_params(("arbitrary",)),
    )(U, U, U, cos, sin, qw, kw)


def _att_prep_bwd(U, dq_c, dk_c, dv_c, cos, sin, qw, kw, *, name):
    T = U.shape[0]
    tm = min(FA_TQ, T)
    R = ATT_G * tm
    scale = ATT_DH ** -0.5

    def body(aq_ref, ak_ref, dq_ref, dk_ref, dv_ref, c_ref, s_ref, qw_ref, kw_ref, out_ref, dqw_ref, dkw_ref):
        @pl.when(pl.program_id(0) == 0)
        def _():
            dqw_ref[...] = jnp.zeros_like(dqw_ref)
            dkw_ref[...] = jnp.zeros_like(dkw_ref)

        bd = _head_blockdiag(ATT_QW)
        c2, s2 = c_ref[...], s_ref[...]
        c8, s8 = jnp.tile(c2, (1, 4)), jnp.tile(s2, (1, 4))

        def bwd(x, dy, w, c, s, bdm):
            even = (lax.broadcasted_iota(jnp.int32, x.shape, 1) & 1) == 0
            dn = dy * c - _pair_swap(dy, even) * s
            r = lax.rsqrt(_head_sum(x * x, bdm) * (1.0 / ATT_DH) + EPS)
            xh = x * r
            dxh = dn * w
            t = _head_sum(dxh * xh, bdm) * (1.0 / ATT_DH)
            return r * (dxh - xh * t), jnp.sum(dn * xh, axis=0, keepdims=True)

        dq = jnp.concatenate([_tokens_from_cols(dq_ref[kv, 0]) for kv in range(ATT_KV)], axis=1)
        da, dw = bwd(aq_ref[...], dq * scale, qw_ref[...], c8, s8, bd)
        out_ref[:, 0:ATT_QW] = da.astype(BF16)
        dqw_ref[...] += dw
        tokens = lambda ref: jnp.concatenate([ref[kv, 0] for kv in range(ATT_KV)], axis=0).T
        da, dw = bwd(ak_ref[...], tokens(dk_ref) * (1.0 / LOG2E), kw_ref[...], c2, s2, bd[:ATT_KW, :ATT_KW])
        out_ref[:, ATT_QW:ATT_QW + ATT_KW] = da.astype(BF16)
        dkw_ref[...] += dw
        out_ref[:, ATT_QW + ATT_KW:ATT_QW + 2 * ATT_KW] = tokens(dv_ref).astype(BF16)

    kv_spec = pl.BlockSpec((tm, ATT_KW), lambda i: (i, 0))
    qv = pl.BlockSpec((1, ATT_QW), lambda i: (0, 0))
    kv = pl.BlockSpec((1, ATT_KW), lambda i: (0, 0))
    w_att = ATT_QW + 2 * ATT_KW
    per = dk_c.shape[3] // tm
    c_spec = pl.BlockSpec((ATT_KV, 1, ATT_DH, tm), lambda i: (0, i // per, 0, i % per))
    return pl.pallas_call(
        body, name=name, grid=(T // tm,),
        in_specs=[pl.BlockSpec((tm, ATT_QW), lambda i: (i, 5)), pl.BlockSpec((tm, ATT_KW), lambda i: (i, 24)),
                  pl.BlockSpec((ATT_KV, 1, ATT_DH, R), lambda i: (0, i, 0, 0)), c_spec, c_spec, kv_spec, kv_spec, qv, kv],
        out_specs=[pl.BlockSpec((tm, w_att), lambda i: (i, 0)), qv, kv],
        out_shape=[jax.ShapeDtypeStruct((T, w_att), BF16),
                   jax.ShapeDtypeStruct((1, ATT_QW), F32), jax.ShapeDtypeStruct((1, ATT_KW), F32)],
        compiler_params=_params(("arbitrary",)),
    )(U, U, dq_c, dk_c, dv_c, cos, sin, qw, kw)


def _scores(k_ref, j, qv):
    return lax.dot_general(k_ref[0, j], qv, (((0,), (0,)), ((), ())), preferred_element_type=F32)


def _flash_fwd(q_c, k_c, v_c, *, name):
    _, nq, _, R = q_c.shape
    _, n_k, _, tk = v_c.shape

    def body(q_ref, k_ref, v_ref, o_ref, lse_ref, acc_ref):
        qv = q_ref[0, 0]
        acc_ref[...] = jnp.zeros_like(acc_ref)

        def step(j, carry):
            m, l = carry
            s = _scores(k_ref, j, qv)
            m_new = jnp.maximum(m, jnp.max(s, axis=0, keepdims=True))
            alpha = jnp.exp2(m - m_new)
            p = jnp.exp2(s - m_new)
            l = alpha * l + jnp.sum(p, axis=0, keepdims=True)
            acc_ref[...] = alpha * acc_ref[...] + jnp.dot(v_ref[0, j], p.astype(BF16), preferred_element_type=F32)
            return m_new, l

        m, l = lax.fori_loop(0, n_k, step, (jnp.full((1, R), -jnp.inf, F32), jnp.zeros((1, R), F32)))
        o_ref[0, 0] = acc_ref[...] / l
        lse_ref[0, 0] = m + jnp.log2(l)

    cspec = pl.BlockSpec((1, 1, ATT_DH, R), lambda h, i: (h, i, 0, 0))
    kspec = pl.BlockSpec((1, n_k, ATT_DH, tk), lambda h, i: (h, 0, 0, 0))
    return pl.pallas_call(
        body, name=name, grid=(ATT_KV, nq),
        in_specs=[cspec, kspec, kspec],
        out_specs=[cspec, pl.BlockSpec((1, 1, 1, R), lambda h, i: (h, i, 0, 0))],
        out_shape=[jax.ShapeDtypeStruct((ATT_KV, nq, ATT_DH, R), F32), jax.ShapeDtypeStruct((ATT_KV, nq, 1, R), F32)],
        scratch_shapes=[pltpu.VMEM((ATT_DH, R), F32)],
        compiler_params=_params(("parallel", "parallel")),
    )(q_c, k_c, v_c)


FA_BOUND_MAX = 40.0 * LOG2E


def _flash_fwd_bounded(q_c, k_c, v_c, m_c, *, name):
    _, nq, _, R = q_c.shape
    _, n_k, _, tk = v_c.shape

    def body(q_ref, k_ref, v_ref, m_ref, o_ref, lse_ref, acc_ref):
        qv = q_ref[0, 0]
        m = m_ref[0, 0]
        acc_ref[...] = jnp.zeros_like(acc_ref)

        per = math.gcd(n_k, 4)

        def step(jj, l8):
            pv = None
            for u in range(per):
                j = per * jj + u
                p = jnp.exp2(_scores(k_ref, j, qv) - m)
                l8 = l8 + jnp.sum(p.reshape(tk // 8, 8, R), axis=0)
                d = jnp.dot(v_ref[0, j], p.astype(BF16), preferred_element_type=F32)
                pv = d if pv is None else pv + d
            acc_ref[...] += pv
            return l8

        l8 = lax.fori_loop(0, n_k // per, step, jnp.zeros((8, R), F32))
        l = jnp.sum(l8, axis=0, keepdims=True)
        o_ref[0, 0] = acc_ref[...] / l
        lse_ref[0, 0] = m + jnp.log2(l)

    cspec = pl.BlockSpec((1, 1, ATT_DH, R), lambda h, i: (h, i, 0, 0))
    kspec = pl.BlockSpec((1, n_k, ATT_DH, tk), lambda h, i: (h, 0, 0, 0))
    vspec = pl.BlockSpec((1, 1, 1, R), lambda h, i: (h, i, 0, 0))
    return pl.pallas_call(
        body, name=name, grid=(ATT_KV, nq),
        in_specs=[cspec, kspec, kspec, vspec],
        out_specs=[cspec, vspec],
        out_shape=[jax.ShapeDtypeStruct((ATT_KV, nq, ATT_DH, R), F32), jax.ShapeDtypeStruct((ATT_KV, nq, 1, R), F32)],
        scratch_shapes=[pltpu.VMEM((ATT_DH, R), F32)],
        compiler_params=_params(("parallel", "parallel")),
    )(q_c, k_c, v_c, m_c)


CHIP_MASKS = [(1, 0, 0), (0, 1, 0), (1, 1, 0)]


def _chip_slot(p):
    return 2 * p[0] + p[1]


def _flash_bwd(q_c, k_c, v_c, do_c, lse, delta, *, name, ride=None):
    _, nq, _, R = q_c.shape
    _, n_k, _, tk = k_c.shape
    n_ride = 0 if ride is None else len(ride["arrays"])

    def body(qc_ref, kc_ref, vc_ref, doc_ref, lse_ref, delta_ref, *rest):
        ride_in, rest = rest[:n_ride], rest[n_ride:]
        dq_ref, dk_ref, dv_ref = rest[:3]
        ride_out, rest = rest[3:3 + n_ride], rest[3 + n_ride:]
        acc_ref = rest[0]
        kv = pl.program_id(0)
        if n_ride:
            start, finish = ride["halves"](ride_in, ride_out, *rest[1:])
            pl.when((kv == 0) & (pl.program_id(1) == 0))(start)

        @pl.when(pl.program_id(1) == 0)
        def _():
            dk_ref[...] = jnp.zeros_like(dk_ref)
            dv_ref[...] = jnp.zeros_like(dv_ref)

        qc, doc = qc_ref[0, 0], doc_ref[0, 0]
        lsev, delta = lse_ref[0, 0], delta_ref[0, 0]
        acc_ref[...] = jnp.zeros_like(acc_ref)
        nt = (((1,), (1,)), ((), ()))

        def step(j, carry):
            p = jnp.exp2(_scores(kc_ref, j, qc) - lsev)
            dp = _scores(vc_ref, j, doc)
            ds = (p * (dp - delta)).astype(BF16)
            acc_ref[...] += jnp.dot(kc_ref[0, j], ds, preferred_element_type=F32)
            dk_ref[0, j] += lax.dot_general(qc, ds, nt, preferred_element_type=F32)
            dv_ref[0, j] += lax.dot_general(doc, p.astype(BF16), nt, preferred_element_type=F32)
            return carry

        lax.fori_loop(0, n_k, step, 0, unroll=2)
        dq_ref[0, 0] = acc_ref[...]

        if n_ride:
            pl.when((kv == ATT_KV - 1) & (pl.program_id(1) == nq - 1))(finish)

    cspec = pl.BlockSpec((1, 1, ATT_DH, R), lambda h, i: (h, i, 0, 0))
    vspec = pl.BlockSpec((1, 1, 1, R), lambda h, i: (h, i, 0, 0))
    kspec = pl.BlockSpec((1, n_k, ATT_DH, tk), lambda h, i: (h, 0, 0, 0))
    k_shape = jax.ShapeDtypeStruct(k_c.shape, F32)
    return pl.pallas_call(
        body, name=name, grid=(ATT_KV, nq),
        in_specs=[cspec, kspec, kspec, cspec, vspec, vspec] + [ANY] * n_ride,
        out_specs=[cspec, kspec, kspec] + [ANY] * n_ride,
        out_shape=[jax.ShapeDtypeStruct((ATT_KV, nq, ATT_DH, R), F32), k_shape, k_shape]
                  + (ride["out_shape"] if n_ride else []),
        scratch_shapes=[pltpu.VMEM((ATT_DH, R), F32)] + (ride["scratch"] if n_ride else []),
        compiler_params=pltpu.CompilerParams(dimension_semantics=("arbitrary", "arbitrary"),
                                             vmem_limit_bytes=VMEM_LIMIT, has_side_effects=bool(n_ride)),
    )(q_c, k_c, v_c, do_c, lse, delta, *(ride["arrays"] if n_ride else []))


def _att_post_fwd(o_c, w, *, name):
    _, nq, _, R = o_c.shape
    tm = R // ATT_G
    T = nq * tm

    def body(oc_ref, w_ref, o_ref, out_ref):
        ov = jnp.concatenate([_tokens_from_cols(oc_ref[kv, 0]) for kv in range(ATT_KV)], axis=1)
        r = lax.rsqrt(jnp.mean(ov * ov, axis=-1, keepdims=True) + EPS)
        o_ref[...] = ov
        out_ref[...] = (ov * r * w_ref[...]).astype(BF16)

    tok = pl.BlockSpec((tm, ATT_QW), lambda i: (i, 0))
    return pl.pallas_call(
        body, name=name, grid=(nq,),
        in_specs=[pl.BlockSpec((ATT_KV, 1, ATT_DH, R), lambda i: (0, i, 0, 0)), pl.BlockSpec((1, ATT_QW), lambda i: (0, 0))],
        out_specs=[tok, tok],
        out_shape=[jax.ShapeDtypeStruct((T, ATT_QW), F32), jax.ShapeDtypeStruct((T, ATT_QW), BF16)],
        compiler_params=_params(("parallel",)),
    )(o_c, w)


def _att_post_bwd(dmix, o, w, *, name):
    T = o.shape[0]
    tm = min(FA_TQ, T)
    R = ATT_G * tm

    def body(dm_ref, o_ref, w_ref, do_ref, delta_ref, dw_ref):
        @pl.when(pl.program_id(0) == 0)
        def _():
            dw_ref[...] = jnp.zeros_like(dw_ref)

        ov = o_ref[...]
        r = lax.rsqrt(jnp.mean(ov * ov, axis=-1, keepdims=True) + EPS)
        xh = ov * r
        dm = dm_ref[...]
        dxh = dm * w_ref[...]
        t = jnp.mean(dxh * xh, axis=-1, keepdims=True)
        do = r * (dxh - xh * t)
        _store_cols(do_ref, do)
        dob = do.astype(BF16).astype(F32)
        for kv in range(ATT_KV):
            delta_ref[kv, 0] = jnp.sum(_cols_from_tokens(dob * ov, kv), axis=0, keepdims=True)
        dw_ref[...] += jnp.sum(dm * xh, axis=0, keepdims=True)

    tok = pl.BlockSpec((tm, ATT_QW), lambda i: (i, 0))
    vec = pl.BlockSpec((1, ATT_QW), lambda i: (0, 0))
    return pl.pallas_call(
        body, name=name, grid=(T // tm,),
        in_specs=[pl.BlockSpec((tm, ATT_QW), lambda i: (i, 1)), tok, vec],
        out_specs=[pl.BlockSpec((ATT_KV, 1, ATT_DH, R), lambda i: (0, i, 0, 0)),
                   pl.BlockSpec((ATT_KV, 1, 1, R), lambda i: (0, i, 0, 0)), vec],
        out_shape=[jax.ShapeDtypeStruct((ATT_KV, T // tm, ATT_DH, R), BF16),
                   jax.ShapeDtypeStruct((ATT_KV, T // tm, 1, R), F32), jax.ShapeDtypeStruct((1, ATT_QW), F32)],
        compiler_params=_params(("arbitrary",)),
    )(dmix, o, w)


def _ffn_up(h2, wg_t, wu_t, *, name, tm=512):
    T = h2.shape[0]
    tn = _pick(D_FF, 1408)
    nt = (((1,), (1,)), ((), ()))

    def body(h_ref, wg_ref, wu_ref, g_ref, u_ref, a_ref):
        hv = h_ref[...]
        g = lax.dot_general(hv, wg_ref[...], nt, preferred_element_type=F32)
        u = lax.dot_general(hv, wu_ref[...], nt, preferred_element_type=F32)
        g_ref[...] = g.astype(BF16)
        u_ref[...] = u.astype(BF16)
        a_ref[...] = (g * _sigmoid(g) * u).astype(BF16)

    wspec = pl.BlockSpec((tn, D_MODEL), lambda i, j: (j, 0))
    ospec = pl.BlockSpec((tm, tn), lambda i, j: (i, j))
    return pl.pallas_call(
        body, name=name, grid=(T // tm, D_FF // tn),
        in_specs=[pl.BlockSpec((tm, D_MODEL), lambda i, j: (i, 0)), wspec, wspec],
        out_specs=[ospec] * 3, out_shape=[jax.ShapeDtypeStruct((T, D_FF), BF16)] * 3,
        compiler_params=_params(("parallel", "arbitrary")),
    )(h2, wg_t, wu_t)


def _ffn_act_bwd(dx2b, w_down, gate, up, *, name, tm=512):
    T = dx2b.shape[0]
    tn = _pick(D_FF, 1408)

    def body(dx_ref, w_ref, g_ref, u_ref, dg_ref, du_ref):
        da = lax.dot_general(dx_ref[...], w_ref[...], (((1,), (1,)), ((), ())), preferred_element_type=F32)
        g = g_ref[...].astype(F32)
        u = u_ref[...].astype(F32)
        sg = _sigmoid(g)
        dg_ref[...] = (da * u * (sg * (1.0 + g * (1.0 - sg)))).astype(BF16)
        du_ref[...] = (da * (g * sg)).astype(BF16)

    ospec = pl.BlockSpec((tm, tn), lambda i, j: (i, j))
    return pl.pallas_call(
        body, name=name, grid=(T // tm, D_FF // tn),
        in_specs=[pl.BlockSpec((tm, D_MODEL), lambda i, j: (i, 0)),
                  pl.BlockSpec((tn, D_MODEL), lambda i, j: (j, 0)), ospec, ospec],
        out_specs=[ospec] * 2, out_shape=[jax.ShapeDtypeStruct((T, D_FF), BF16)] * 2,
        compiler_params=_params(("parallel", "arbitrary")),
    )(dx2b, w_down, gate, up)


def _adam_math(w, g, m, v):
    m = ADAM_B1 * m + (1.0 - ADAM_B1) * g
    v = ADAM_B2 * v + (1.0 - ADAM_B2) * (g * g)
    m_hat = m / (1.0 - ADAM_B1 ** ADAM_STEP)
    v_hat = v / (1.0 - ADAM_B2 ** ADAM_STEP)
    delta = -ADAM_LR * (m_hat / (jnp.sqrt(v_hat) + ADAM_EPS) + ADAM_WD * w)
    return delta, m, v


def _adamw(parts, w, m, v, *, name, tr_cap=256):
    P, R, C = parts.shape
    tr = R
    for t in range(8, min(R, tr_cap) + 1, 8):
        if R % t == 0:
            tr = t

    def body(p_ref, w_ref, m_ref, v_ref, g_ref, d_ref, nm_ref, nv_ref):
        g = p_ref[0].astype(F32)
        for j in range(1, P):
            g = g + p_ref[j].astype(F32)
        d, nm, nv = _adam_math(w_ref[...], g, m_ref[...], v_ref[...])
        g_ref[...] = g
        d_ref[...] = d
        nm_ref[...] = nm
        nv_ref[...] = nv

    blk = pl.BlockSpec((tr, C), lambda i: (i, 0))
    return pl.pallas_call(
        body, name=name, grid=(R // tr,),
        in_specs=[pl.BlockSpec((P, tr, C), lambda i: (0, i, 0)), blk, blk, blk],
        out_specs=[blk] * 4, out_shape=[jax.ShapeDtypeStruct((R, C), F32)] * 4,
        compiler_params=_params(("parallel",)),
    )(parts, w, m, v)


def _gather_halves(ins, outs, send_sems, recv_sems, local_sems):
    n = len(ins)
    x, y, c = lax.axis_index("x"), lax.axis_index("y"), lax.axis_index("c")
    me, sibling = (x, y, c), (x, y, 1 - c)
    chips = [(1 - x, y), (x, 1 - y), (1 - x, 1 - y)]

    def slot(p):
        return 4 * p[0] + 2 * p[1] + p[2]

    def copy(a, k, block, to, src=None):
        dst = outs[a].at[slot(block)]
        return pltpu.make_async_remote_copy(
            src_ref=dst if src is None else src, dst_ref=dst,
            send_sem=send_sems.at[a * 7 + k], recv_sem=recv_sems.at[a * 7 + k],
            device_id=to, device_id_type=MESH)

    mine = [pltpu.make_async_copy(ins[a], outs[a].at[slot(me)], local_sems.at[a]) for a in range(n)]
    first = []
    for a in range(n):
        first.append(copy(a, 0, me, sibling, src=ins[a]))
        first += [copy(a, 1 + j, me, (*chip, c), src=ins[a]) for j, chip in enumerate(chips)]

    def start():
        for cp in mine + first:
            cp.start()

    def finish():
        passed = []
        for j, chip in enumerate(chips):
            for a in range(n):
                copy(a, 1 + j, (*chip, c), me).wait_recv()
                cp = copy(a, 4 + j, (*chip, c), sibling)
                cp.start()
                passed.append(cp)
        for a in range(n):
            copy(a, 0, sibling, me).wait_recv()
            for j, chip in enumerate(chips):
                copy(a, 4 + j, (*chip, 1 - c), me).wait_recv()
        for cp in first + passed:
            cp.wait_send()
        for cp in mine:
            cp.wait()

    return start, finish


def _gather_scratch(n):
    return [pltpu.SemaphoreType.DMA((7 * n,)), pltpu.SemaphoreType.DMA((7 * n,)), pltpu.SemaphoreType.DMA((n,))]


def _gathered_shapes(xs):
    return [jax.ShapeDtypeStruct((N_DEV,) + x.shape, x.dtype) for x in xs]


def _ride_gather(xs):
    xs = list(xs)
    return dict(arrays=xs, out_shape=_gathered_shapes(xs), scratch=_gather_scratch(len(xs)), halves=_gather_halves)


def _ride_chips(gs):
    gs = list(gs)
    n = len(gs)

    def halves(ins, outs, send_sems, recv_sems, local_sems):
        mine, copies = _exchange_copies(ins, outs, send_sems, recv_sems, local_sems, masks=CHIP_MASKS, slot=_chip_slot)

        def start():
            for cp in mine:
                cp.start()
            for send, _ in copies:
                send.start()

        def finish():
            for send, recv in copies:
                recv.wait_recv()
                send.wait_send()
            for cp in mine:
                cp.wait()

        return start, finish

    n_sem = len(CHIP_MASKS) * n
    return dict(arrays=gs, out_shape=[jax.ShapeDtypeStruct(g.shape, g.dtype) for g in gs], halves=halves,
                scratch=[pltpu.SemaphoreType.DMA((n_sem,)), pltpu.SemaphoreType.DMA((n_sem,)),
                         pltpu.SemaphoreType.DMA((n,))])


ALL_MASKS = [(mx, my, mc) for mx in (0, 1) for my in (0, 1) for mc in (0, 1)][1:]


def _flip(v, bit):
    return 1 - v if bit else v


def _exchange_copies(ins, outs, send_sems, recv_sems, local_sems, *, masks, slot):
    n, n_peers = len(ins), len(masks)
    x, y, c = lax.axis_index("x"), lax.axis_index("y"), lax.axis_index("c")
    my_slot = slot((x, y, c))
    mine = [pltpu.make_async_copy(ins[a].at[my_slot], outs[a].at[my_slot], local_sems.at[a]) for a in range(n)]
    copies = []
    for a in range(n):
        for k, (mx, my, mc) in enumerate(masks):
            peer = (_flip(x, mx), _flip(y, my), _flip(c, mc))
            peer_slot = slot(peer)
            sems = dict(send_sem=send_sems.at[a * n_peers + k], recv_sem=recv_sems.at[a * n_peers + k],
                        device_id=peer, device_id_type=MESH)
            copies.append((
                pltpu.make_async_remote_copy(src_ref=ins[a].at[peer_slot], dst_ref=outs[a].at[my_slot], **sems),
                pltpu.make_async_remote_copy(src_ref=ins[a].at[peer_slot], dst_ref=outs[a].at[peer_slot], **sems)))
    return mine, copies


def _send_to_all(v, *, name):
    def body(v_ref, out_ref, send_sems, recv_sems, local_sem):
        x, y, c = lax.axis_index("x"), lax.axis_index("y"), lax.axis_index("c")
        me = 4 * x + 2 * y + c
        mine = pltpu.make_async_copy(v_ref, out_ref.at[me], local_sem)
        mine.start()
        copies = []
        for k, (mx, my, mc) in enumerate(ALL_MASKS):
            peer = (_flip(x, mx), _flip(y, my), _flip(c, mc))
            peer_id = 4 * peer[0] + 2 * peer[1] + peer[2]
            sems = dict(send_sem=send_sems.at[k], recv_sem=recv_sems.at[k], device_id=peer, device_id_type=MESH)
            copies.append((pltpu.make_async_remote_copy(src_ref=v_ref, dst_ref=out_ref.at[me], **sems),
                           pltpu.make_async_remote_copy(src_ref=v_ref, dst_ref=out_ref.at[peer_id], **sems)))
        for send, _ in copies:
            send.start()
        for send, recv in copies:
            recv.wait_recv()
            send.wait_send()
        mine.wait()

    n_peers = len(ALL_MASKS)
    return pl.pallas_call(
        body, name=name, in_specs=[ANY], out_specs=ANY,
        out_shape=jax.ShapeDtypeStruct((N_DEV,) + v.shape, v.dtype),
        scratch_shapes=[pltpu.SemaphoreType.DMA((n_peers,)), pltpu.SemaphoreType.DMA((n_peers,)),
                        pltpu.SemaphoreType.DMA],
        compiler_params=pltpu.CompilerParams(has_side_effects=True),
    )(v)


SWAP_ROW_CHUNKS = 4


def _ride_swap(gs):
    gs = list(gs)
    n = len(gs)

    def halves(ins, outs, send_sems, recv_sems):
        x, y, c = lax.axis_index("x"), lax.axis_index("y"), lax.axis_index("c")
        sibling = dict(device_id=(x, y, 1 - c), device_id_type=MESH)

        def start():
            for a in range(n):
                Q, _, R, _ = ins[a].shape
                rows = R // SWAP_ROW_CHUNKS
                for q in range(Q):
                    for j in range(SWAP_ROW_CHUNKS):
                        part = pl.ds(j * rows, rows)
                        pltpu.make_async_remote_copy(src_ref=ins[a].at[q, 1 - c, part], dst_ref=outs[a].at[q, part],
                                                     send_sem=send_sems.at[a], recv_sem=recv_sems.at[a], **sibling).start()

        def finish():
            for a in range(n):
                pltpu.make_async_remote_copy(src_ref=outs[a], dst_ref=outs[a], send_sem=send_sems.at[a],
                                             recv_sem=recv_sems.at[a], **sibling).wait()

        return start, finish

    return dict(arrays=gs, out_shape=[jax.ShapeDtypeStruct(g.shape[:1] + g.shape[2:], g.dtype) for g in gs],
                scratch=[pltpu.SemaphoreType.DMA((n,)), pltpu.SemaphoreType.DMA((n,))], halves=halves)


def _core_swap(gs, *, name):
    ride = _ride_swap(gs)
    n = len(gs)

    def body(*refs):
        start, finish = ride["halves"](refs[:n], refs[n:2 * n], *refs[2 * n:])
        start()
        finish()

    return pl.pallas_call(
        body, name=name, in_specs=[ANY] * n, out_specs=[ANY] * n, out_shape=ride["out_shape"],
        scratch_shapes=ride["scratch"], compiler_params=pltpu.CompilerParams(has_side_effects=True),
    )(*gs)


def _pair_sum(g, other, core, *, name, tr_cap=256):
    Q, _, R, C = g.shape
    tr = max(t for t in range(16, min(R, tr_cap) + 1, 16) if R % t == 0)

    def body(core_ref, g_ref, o_ref, out_ref):
        out_ref[0] = (g_ref[0, 0] + o_ref[0]).astype(BF16)

    return pl.pallas_call(
        body, name=name,
        grid_spec=pltpu.PrefetchScalarGridSpec(
            num_scalar_prefetch=1, grid=(Q, R // tr),
            in_specs=[pl.BlockSpec((1, 1, tr, C), lambda q, i, core_ref: (q, core_ref[0], i, 0)),
                      pl.BlockSpec((1, tr, C), lambda q, i, core_ref: (q, i, 0))],
            out_specs=pl.BlockSpec((1, tr, C), lambda q, i, core_ref: (q, i, 0))),
        out_shape=jax.ShapeDtypeStruct((Q, R, C), BF16),
        compiler_params=_params(("parallel", "parallel")),
    )(core, g, other)


def _pack_small(norm1, norm2, final, att, hg, qn, kn, lb=None, loss=None):
    z = lambda n: jnp.zeros((n,), F32)
    rows = [norm1.reshape(-1), norm2.reshape(-1), final.reshape(-1),
            jnp.concatenate([att.reshape(-1), z(512)]),
            jnp.concatenate([hg.reshape(-1), qn.reshape(-1), kn.reshape(-1), z(1024 - 256)]),
            z(1024) if lb is None else lb.reshape(-1),
            z(1024) if loss is None else jnp.concatenate([loss.reshape(-1), z(1023)]), z(1024)]
    return jnp.stack(rows, axis=0)


def _unpack_small(p):
    return (p[0:1, :], p[1:2, :], p[2, :], p[3:4, 0:512], p[4:5, 0:128], p[4:5, 128:192], p[4:5, 192:256])


def _fold_heads(dhg, dqn, dkn, *, name):
    def body(hg_ref, q_ref, k_ref, ohg_ref, oq_ref, ok_ref):
        def fold128(v):
            acc = v[:, 0:LANES]
            for j in range(1, v.shape[1] // LANES):
                acc = acc + v[:, j * LANES:(j + 1) * LANES]
            return acc

        ohg_ref[...] = fold128(hg_ref[...])
        q = fold128(q_ref[...])
        oq_ref[...] = q + pltpu.roll(q, ATT_DH, 1)
        k = k_ref[...]
        ok_ref[...] = k + pltpu.roll(k, ATT_DH, 1)

    return pl.pallas_call(body, name=name, out_shape=[jax.ShapeDtypeStruct((1, LANES), F32)] * 3)(dhg, dqn, dkn)


def _lb_grad(dlb_sum, lb, *, name):
    def body(d_ref, lb_ref, o_ref):
        lbv = lb_ref[...]
        gl = d_ref[...] * lbv * (1.0 - lbv)
        o_ref[0:1, :] = gl[0:1, :]
        o_ref[1:2, :] = -gl[0:1, :]
        o_ref[2:3, :] = gl[1:2, :]
        o_ref[3:4, :] = -gl[1:2, :]

    return pl.pallas_call(body, name=name, out_shape=jax.ShapeDtypeStruct((4, HG_W), F32))(dlb_sum, lb)


def _lower_bounds(lb_logits_full, *, name):
    def body(l_ref, o_ref):
        for d in range(2):
            l0, l1 = l_ref[2 * d:2 * d + 1, :], l_ref[2 * d + 1:2 * d + 2, :]
            mx = jnp.maximum(l0, l1)
            e0, e1 = jnp.exp(l0 - mx), jnp.exp(l1 - mx)
            o_ref[d:d + 1, :] = e0 / (e0 + e1)

    return pl.pallas_call(body, name=name, out_shape=jax.ShapeDtypeStruct((2, HG_W), F32))(
        lb_logits_full.reshape(4, HG_W))


def _local_step(x, target, norm1_w, w_in_t, lb, hg_norm_w, q_norm_w, k_norm_w, att_norm_w, w_out, norm2_w,
                w_g_t, w_u_t, w_down, final_norm_w, reduce_early=None, reduce_late=None, shards=None):
    T = x.shape[0]
    cos, sin = _rope_tables(T)
    qw8 = jnp.tile(q_norm_w, (1, ATT_HEADS))
    kw2 = jnp.tile(k_norm_w, (1, ATT_KV))

    if shards is None:
        h, r1 = _rms_fwd(x, norm1_w, name="norm1_fwd")
        U = _mm_nn([(h, w_in_t)], trans_b=True, name="in_proj")
        o_f, st_f = _gla_fwd(U, lb[0:1], f_block=1, reverse=False, name="gla_fwd_f")
    else:
        h, r1, g_in, g_lb = _rms_fwd(x, norm1_w, ride=_ride_gather([shards["w_in_t"], shards["lb_logits"]]),
                                     name="norm1_fwd")
        w_in_t = g_in.reshape(-1, D_MODEL)
        lb = _lower_bounds(g_lb.transpose(1, 0, 2).reshape(2, 2, -1), name="lower_bounds")
        U, g_gu = _mm_nn([(h, w_in_t)], trans_b=True, ride=_ride_gather([shards["w_gu_t"]]), name="in_proj")
        o_f, st_f, g_out, g_dn = _gla_fwd(U, lb[0:1], f_block=1, reverse=False,
                                          ride=_ride_gather([shards["w_out"], shards["w_down"]]), name="gla_fwd_f")
        g_gu = g_gu.reshape(2, -1, D_MODEL)
        w_g_t, w_u_t = g_gu[0], g_gu[1]
        w_out, w_down = g_out.reshape(-1, D_MODEL), g_dn.reshape(-1, D_MODEL)
    o_sum, st_b, mix_hg = _gla_fwd(U, lb[1:2], f_block=2, reverse=True, post=(o_f, hg_norm_w), name="gla_fwd_b")
    q_c, qn_c, kmax2, k_c, v_c = _att_prep_fwd(U, cos, sin, qw8, kw2, name="att_prep_fwd")
    kmax = jnp.sqrt(jnp.max(kmax2.reshape(ATT_KV, ATT_DH), axis=1))
    m_c = qn_c * (kmax * 1.001).reshape(ATT_KV, 1, 1, 1)
    o_c, lse = lax.cond(jnp.max(m_c) <= FA_BOUND_MAX,
                        lambda: _flash_fwd_bounded(q_c, k_c, v_c, m_c, name="flash_fwd_bounded"),
                        lambda: _flash_fwd(q_c, k_c, v_c, name="flash_fwd"))
    o_att, mix_att = _att_post_fwd(o_c, att_norm_w, name="att_post_fwd")
    x1, h2, r2 = _mm_nn([(mix_hg, w_out[:HG_W]), (mix_att, w_out[HG_W:])], residual=x, tail=_tail_rms_fwd(norm2_w),
                        name="out_proj")
    gate, up, act = _ffn_up(h2, w_g_t, w_u_t, name="ffn_up")
    loss, dx2, dx2b, d_final = _mm_nn([(act, w_down)], residual=x1,
                                      tail=_tail_loss(target, final_norm_w.reshape(1, D_MODEL)), name="ffn_down")

    d_gate, d_up = _ffn_act_bwd(dx2b, w_down, gate, up, name="ffn_act_bwd")
    dw_down = _mm_tn(act, dx2b, tma_cap=1408, name="dw_down")
    dw_g = _mm_tn(d_gate, h2, tma_cap=1408, name="dw_gate")
    dw_u = _mm_tn(d_up, h2, tma_cap=1408, name="dw_up")
    mine = None if reduce_early is None else reduce_early["slabs"](dw_g, dw_u, dw_down)
    dx1, dx1b, d_norm2, *theirs = _mm_nn([(d_gate, w_g_t), (d_up, w_u_t)], tm=256,
                                         ride=None if mine is None else _ride_swap(mine),
                                         tail=_tail_rms_bwd(x1, r2, norm2_w, dx2, emit_bf16=True), name="ffn_up_bwd")
    dmix = _mm_nn([(dx1b, w_out)], trans_b=True, name="out_proj_bwd")
    dw_out = _mm_tn(mix_att, dx1b, rows=(HG_W, D_MODEL), name="dw_out_att",
                    into=_mm_tn(mix_hg, dx1b, rows=(0, D_MODEL), name="dw_out_hg"))
    do_c, delta, d_att = _att_post_bwd(dmix, o_att, att_norm_w, name="att_post_bwd")
    ride = None if reduce_early is None else _ride_chips(reduce_early["sums"](mine, theirs, dw_out))
    dq_c, dk_c, dv_c, *rode = _flash_bwd(q_c, k_c, v_c, do_c, lse, delta, ride=ride, name="flash_bwd")
    dU_att, d_qn, d_kn = _att_prep_bwd(U, dq_c, dk_c, dv_c, cos, sin, qw8, kw2, name="att_prep_bwd")
    do_hg, du_g, d_hg = _hg_post_bwd(dmix, o_sum, U, hg_norm_w, name="hg_post_bwd")
    dq_f, dz_f, dv_f, dlb_f = _gla_bwd(U, lb[0:1], do_hg, st_f, f_block=1, reverse=False, name="gla_bwd_f")
    dU_hg, dlb_b = _gla_bwd(U, lb[1:2], do_hg, st_b, f_block=2, reverse=True, prev=(dq_f, dz_f, dv_f, du_g),
                            name="gla_bwd_b")
    w_hg = 5 * HG_W
    n_in = w_hg + dU_att.shape[1]
    dw_in = _mm_tn(dU_att, h, tma_cap=256, rows=(w_hg, n_in), name="dw_in_att",
                   into=_mm_tn(dU_hg, h, tma_cap=1280, rows=(0, n_in), name="dw_in_hg"))
    late = None if reduce_late is None else _ride_chips(reduce_late(dw_in))
    grad_x, d_norm1, *rode_late = _mm_nn([(dU_hg, w_in_t[:w_hg]), (dU_att, w_in_t[w_hg:])], ride=late,
                                         tail=_tail_rms_bwd(x, r1, norm1_w, dx1, emit_bf16=False), name="in_proj_bwd")
    d_hg, d_qn, d_kn = _fold_heads(d_hg, d_qn, d_kn, name="fold_heads")

    big = dict(w_in=dw_in, w_out=dw_out, w_g=dw_g, w_u=dw_u, w_down=dw_down)
    small = dict(norm1=d_norm1, norm2=d_norm2, final=d_final, att=d_att, hg=d_hg,
                 qn=d_qn[:, :ATT_DH], kn=d_kn[:, :ATT_DH], lb=jnp.concatenate([dlb_f, dlb_b], axis=0))
    return loss, grad_x, big, small, rode + rode_late, lb


def kernel(x, norm1_w, w_in, lb_logits, hg_norm_w, q_norm_w, k_norm_w, att_norm_w, w_out, norm2_w, w_gate_up, w_down, final_norm_w, loss_target, m_norm1_w, m_w_in, m_lb_logits, m_hg_norm_w, m_q_norm_w, m_k_norm_w, m_att_norm_w, m_w_out, m_norm2_w, m_w_gate_up, m_w_down, m_final_norm_w, v_norm1_w, v_w_in, v_lb_logits, v_hg_norm_w, v_q_norm_w, v_k_norm_w, v_att_norm_w, v_w_out, v_norm2_w, v_w_gate_up, v_w_down, v_final_norm_w):
    T = x.shape[1]
    me = 4 * lax.axis_index("x") + 2 * lax.axis_index("y") + lax.axis_index("c")
    c_in, r_out, c_gu, r_dn = w_in.shape[2], w_out.shape[1], w_gate_up.shape[2], w_down.shape[1]
    lb_cols = lb_logits.shape[2]

    shards = dict(w_in_t=w_in[0].T.astype(BF16), lb_logits=lb_logits.reshape(4, lb_cols),
                  w_gu_t=w_gate_up[0].T.astype(BF16), w_out=w_out[0].astype(BF16), w_down=w_down[0].astype(BF16))

    chips = N_DEV // 2
    core = lax.axis_index("c").astype(jnp.int32).reshape(1)
    by_owner = lambda g, r: g.reshape(chips, 2, r, D_MODEL)

    def pair_sums(mine, theirs, names):
        return [_pair_sum(g, o, core, name="pair_sum_" + nm) for g, o, nm in zip(mine, theirs, names)]

    def early_slabs(dw_g_t, dw_u_t, dw_down):
        half = lambda g: g.reshape(chips // 2, 2, c_gu, D_MODEL)
        return [half(dw_g_t), half(dw_u_t), by_owner(dw_down, r_dn)]

    def early_sums(mine, theirs, dw_out):
        s_out = by_owner(dw_out, r_out)
        c_out, c_g, c_u, c_dn = pair_sums([s_out] + mine, list(_core_swap([s_out], name="exchange_cores_out"))
                                          + list(theirs), ("w_out", "w_gate", "w_up", "w_down"))
        return [c_out, jnp.concatenate([c_g, c_u], axis=0), c_dn]

    def reduce_late(dw_in_t):
        mine = [by_owner(dw_in_t, c_in)]
        return pair_sums(mine, _core_swap(mine, name="exchange_cores_in"), ("w_in",))

    loss, grad_x, big, small, (p_out, p_gu, p_dn, p_in), lb = _local_step(
        x[0], loss_target[0], norm1_w, None, None, hg_norm_w, q_norm_w, k_norm_w, att_norm_w, None, norm2_w,
        None, None, None, final_norm_w, reduce_early=dict(slabs=early_slabs, sums=early_sums),
        reduce_late=reduce_late, shards=shards)
    p_gu, p_in = p_gu.transpose(0, 2, 1), p_in.transpose(0, 2, 1)

    packed = _pack_small(small["norm1"], small["norm2"], small["final"], small["att"], small["hg"],
                         small["qn"], small["kn"], small["lb"], loss)
    all_small = _send_to_all(packed, name="exchange_small")

    g_w_in, d_w_in, nm_w_in, nv_w_in = _adamw(p_in, w_in[0], m_w_in[0], v_w_in[0], name="adamw_w_in")
    g_w_out, d_w_out, nm_w_out, nv_w_out = _adamw(p_out, w_out[0], m_w_out[0], v_w_out[0], name="adamw_w_out")
    g_w_gu, d_w_gu, nm_w_gu, nv_w_gu = _adamw(p_gu, w_gate_up[0], m_w_gate_up[0], v_w_gate_up[0], name="adamw_w_gu")
    g_w_dn, d_w_dn, nm_w_dn, nv_w_dn = _adamw(p_dn, w_down[0], m_w_down[0], v_w_down[0], name="adamw_w_down")

    pk = lambda vecs: _pack_small(*vecs)
    w_pk = pk([norm1_w, norm2_w, final_norm_w, att_norm_w, hg_norm_w, q_norm_w, k_norm_w])
    m_pk = pk([m_norm1_w, m_norm2_w, m_final_norm_w, m_att_norm_w, m_hg_norm_w, m_q_norm_w, m_k_norm_w])
    v_pk = pk([v_norm1_w, v_norm2_w, v_final_norm_w, v_att_norm_w, v_hg_norm_w, v_q_norm_w, v_k_norm_w])
    g_pk, d_pk, nm_pk, nv_pk = _adamw(all_small, w_pk, m_pk, v_pk, name="adamw_small")

    dlb_sum = g_pk[5:6, :].reshape(2, HG_W)
    g_lb_full = _lb_grad(dlb_sum, lb, name="lb_grad")
    g_lb_mine = lax.dynamic_slice_in_dim(g_lb_full, me * lb_cols, lb_cols, axis=1)
    g_lb_s, d_lb, nm_lb, nv_lb = _adamw(g_lb_mine[None], lb_logits.reshape(4, lb_cols),
                                        m_lb_logits.reshape(4, lb_cols), v_lb_logits.reshape(4, lb_cols),
                                        name="adamw_lb")

    loss_total = g_pk[6, 0]

    def outs(big4, lb_arr, pk_arr):
        n1, n2, fin, att, hg, qn, kn = _unpack_small(pk_arr)
        b_in, b_out, b_gu, b_dn = big4
        return [n1, b_in[None], lb_arr.reshape(2, 2, lb_cols), hg, qn, kn, att, b_out[None], n2, b_gu[None],
                b_dn[None], fin]

    return (loss_total, grad_x[None],
            *outs((g_w_in, g_w_out, g_w_gu, g_w_dn), g_lb_s, g_pk),
            *outs((d_w_in, d_w_out, d_w_gu, d_w_dn), d_lb, d_pk),
            *outs((nm_w_in, nm_w_out, nm_w_gu, nm_w_dn), nm_lb, nm_pk),
            *outs((nv_w_in, nv_w_out, nv_w_gu, nv_w_dn), nv_lb, nv_pk))
```

```python
import math

import jax
import jax.numpy as jnp
import numpy as np
from jax import lax
from jax.experimental import pallas as pl
from jax.experimental.pallas import tpu as pltpu

F32 = jnp.float32
BF16 = jnp.bfloat16

N_DEV = 8
D_MODEL = 1024
EPS = 1e-6
HG_HEADS = 4
HG_D = 128
HG_W = HG_HEADS * HG_D
CHUNK = 64
ATT_HEADS = 8
ATT_KV = 2
ATT_G = ATT_HEADS // ATT_KV
ATT_DH = 64
ATT_QW = ATT_HEADS * ATT_DH
ATT_KW = ATT_KV * ATT_DH
GRID_W = 64
ROPE_THETA = 10000.0
D_FF = 2816
ADAM_LR, ADAM_B1, ADAM_B2, ADAM_EPS, ADAM_WD, ADAM_STEP = 0.001, 0.9, 0.999, 1e-08, 0.01, 10

LOG2E = math.log2(math.e)
LANES = 128
VMEM_LIMIT = 48 * 1024 * 1024
MESH = pl.DeviceIdType.MESH
ANY = pl.BlockSpec(memory_space=pl.ANY)


def _params(sem=None):
    return pltpu.CompilerParams(dimension_semantics=sem, vmem_limit_bytes=VMEM_LIMIT)


def _pick(n, cap):
    best = None
    for t in range(LANES, cap + 1, LANES):
        if n % t == 0:
            best = t
    assert best is not None, (n, cap)
    return best


def _sigmoid(x):
    return 1.0 / (1.0 + jnp.exp(-x))


def _dot(a, b):
    return jnp.dot(a.astype(BF16), b.astype(BF16), preferred_element_type=F32)


def _dot_nt(a, b):
    return lax.dot_general(a.astype(BF16), b.astype(BF16), (((1,), (1,)), ((), ())),
                           preferred_element_type=F32)


def _dot_tn(a, b):
    return lax.dot_general(a.astype(BF16), b.astype(BF16), (((0,), (0,)), ((), ())),
                           preferred_element_type=F32)


def _mm_nn(pairs, *, name, out_dtype=F32, residual=None, tm=512, tn_cap=None, trans_b=False, tail=None, ride=None):
    M = pairs[0][0].shape[0]
    N = pairs[0][1].shape[0 if trans_b else 1]
    tn = N if tn_cap is None else _pick(N, tn_cap)
    n_pairs = len(pairs)
    has_res = residual is not None
    dims = (((1,), (1,)), ((), ())) if trans_b else (((1,), (0,)), ((), ()))
    assert (tail is None and ride is None) or tn == N
    n_main = 2 * n_pairs + has_res
    n_ti = 0 if tail is None else len(tail["ins"])
    n_out = 1 if tail is None else len(tail["outs"])
    n_r = 0 if ride is None else len(ride["arrays"])
    n_in = n_main + n_ti + n_r

    def body(*refs):
        outs = refs[n_in:n_in + n_out]
        if n_r:
            start, finish = ride["halves"](refs[n_main + n_ti:n_in], refs[n_in + n_out:n_in + n_out + n_r],
                                           *refs[n_in + n_out + n_r:])
            pl.when(pl.program_id(0) == 0)(start)
        acc = None
        for i in range(n_pairs):
            d = lax.dot_general(refs[2 * i][...], refs[2 * i + 1][...], dims, preferred_element_type=F32)
            acc = d if acc is None else acc + d
        if has_res:
            acc = acc + refs[2 * n_pairs][...]
        if tail is None:
            outs[0][...] = acc.astype(out_dtype)
        else:
            tail["fn"](acc, pl.program_id(0) == 0, *refs[n_main:n_main + n_ti], *outs)
        if n_r:
            pl.when(pl.program_id(0) == M // tm - 1)(finish)

    kinds = {"row": ((tm, N), (M, N), lambda i, j: (i, 0)), "col": ((tm, 1), (M, 1), lambda i, j: (i, 0)),
             "vec": ((1, N), (1, N), lambda i, j: (0, 0)), "one": ((1, 1), (1, 1), lambda i, j: (0, 0))}
    in_specs, args = [], []
    for a, b in pairs:
        k = a.shape[1]
        b_spec = pl.BlockSpec((tn, k), lambda i, j: (j, 0)) if trans_b else pl.BlockSpec((k, tn), lambda i, j: (0, j))
        in_specs += [pl.BlockSpec((tm, k), lambda i, j: (i, 0)), b_spec]
        args += [a, b]
    if has_res:
        in_specs.append(pl.BlockSpec((tm, tn), lambda i, j: (i, j)))
        args.append(residual)
    if tail is None:
        out_specs = [pl.BlockSpec((tm, tn), lambda i, j: (i, j))]
        out_shape = [jax.ShapeDtypeStruct((M, N), out_dtype)]
    else:
        for arr, kind in tail["ins"]:
            in_specs.append(pl.BlockSpec(kinds[kind][0], kinds[kind][2]))
            args.append(arr)
        out_specs = [pl.BlockSpec(kinds[kind][0], kinds[kind][2]) for _, kind in tail["outs"]]
        out_shape = [jax.ShapeDtypeStruct(kinds[kind][1], dt) for dt, kind in tail["outs"]]
    scratch = []
    if n_r:
        in_specs += [ANY] * n_r
        args += ride["arrays"]
        out_specs += [ANY] * n_r
        out_shape += ride["out_shape"]
        scratch = ride["scratch"]
    sequential = tail is not None or n_r > 0
    res = pl.pallas_call(
        body, name=name, grid=(M // tm, N // tn), in_specs=in_specs, out_specs=out_specs, out_shape=out_shape,
        scratch_shapes=scratch,
        compiler_params=pltpu.CompilerParams(dimension_semantics=("arbitrary" if sequential else "parallel", "arbitrary"),
                                             vmem_limit_bytes=VMEM_LIMIT, has_side_effects=n_r > 0),
    )(*args)
    return res[0] if len(res) == 1 else res


def _mm_tn(a, b, *, name, tma_cap=1024, tnb_cap=1024, tk=2048, rows=None, into=None):
    T, Ma = a.shape
    Nb = b.shape[1]
    tma, tnb = _pick(Ma, tma_cap), _pick(Nb, tnb_cap)
    tk = min(tk, T)
    n_k = T // tk
    first_row, total = (0, Ma) if rows is None else rows
    assert first_row % tma == 0
    i0 = first_row // tma

    def body(a_ref, b_ref, *rest):
        o_ref, acc_ref = rest[-2:]
        k = pl.program_id(2)

        @pl.when(k == 0)
        def _():
            acc_ref[...] = jnp.zeros_like(acc_ref)

        acc_ref[...] += lax.dot_general(a_ref[...], b_ref[...], (((0,), (0,)), ((), ())),
                                        preferred_element_type=F32)

        @pl.when(k == n_k - 1)
        def _():
            o_ref[...] = acc_ref[...]

    in_specs = [pl.BlockSpec((tk, tma), lambda i, j, k: (k, i)), pl.BlockSpec((tk, tnb), lambda i, j, k: (k, j))]
    args = [a, b]
    if into is not None:
        in_specs.append(ANY)
        args.append(into)
    return pl.pallas_call(
        body, name=name, grid=(Ma // tma, Nb // tnb, n_k), in_specs=in_specs,
        out_specs=pl.BlockSpec((tma, tnb), lambda i, j, k: (i0 + i, j)),
        out_shape=jax.ShapeDtypeStruct((total, Nb), F32),
        scratch_shapes=[pltpu.VMEM((tma, tnb), F32)],
        input_output_aliases={} if into is None else {2: 0},
        compiler_params=_params(("parallel", "parallel", "arbitrary")),
    )(*args)


def _rms_fwd(x, w, *, name, tm=512, ride=None):
    T, Dm = x.shape
    n_r = 0 if ride is None else len(ride["arrays"])

    def body(x_ref, w_ref, *rest):
        h_ref, r_ref = rest[n_r:n_r + 2]
        if n_r:
            start, finish = ride["halves"](rest[:n_r], rest[n_r + 2:2 * n_r + 2], *rest[2 * n_r + 2:])
            pl.when(pl.program_id(0) == 0)(start)
        xv = x_ref[...]
        r = lax.rsqrt(jnp.mean(xv * xv, axis=-1, keepdims=True) + EPS)
        h_ref[...] = (xv * r * w_ref[...]).astype(BF16)
        r_ref[...] = r
        if n_r:
            pl.when(pl.program_id(0) == T // tm - 1)(finish)

    return pl.pallas_call(
        body, name=name, grid=(T // tm,),
        in_specs=[pl.BlockSpec((tm, Dm), lambda i: (i, 0)), pl.BlockSpec((1, Dm), lambda i: (0, 0))] + [ANY] * n_r,
        out_specs=[pl.BlockSpec((tm, Dm), lambda i: (i, 0)), pl.BlockSpec((tm, 1), lambda i: (i, 0))] + [ANY] * n_r,
        out_shape=[jax.ShapeDtypeStruct((T, Dm), BF16), jax.ShapeDtypeStruct((T, 1), F32)]
                  + (ride["out_shape"] if n_r else []),
        scratch_shapes=ride["scratch"] if n_r else [],
        compiler_params=pltpu.CompilerParams(dimension_semantics=("arbitrary" if n_r else "parallel",),
                                             vmem_limit_bytes=VMEM_LIMIT, has_side_effects=n_r > 0),
    )(x, w, *(ride["arrays"] if n_r else []))


def _tail_rms_fwd(w):
    def fn(xv, first, w_ref, x_ref, h_ref, r_ref):
        r = lax.rsqrt(jnp.mean(xv * xv, axis=-1, keepdims=True) + EPS)
        x_ref[...] = xv
        h_ref[...] = (xv * r * w_ref[...]).astype(BF16)
        r_ref[...] = r

    return dict(fn=fn, ins=[(w, "vec")], outs=[(F32, "row"), (BF16, "row"), (F32, "col")])


def _tail_rms_bwd(x, r, w, dres, *, emit_bf16):
    def fn(dhv, first, x_ref, r_ref, w_ref, dres_ref, *outs):
        dx_ref, dw_ref = outs[0], outs[-1]

        @pl.when(first)
        def _():
            dw_ref[...] = jnp.zeros_like(dw_ref)

        rv = r_ref[...]
        xh = x_ref[...] * rv
        dxh = dhv * w_ref[...]
        t = jnp.mean(dxh * xh, axis=-1, keepdims=True)
        dx = dres_ref[...] + rv * (dxh - xh * t)
        dx_ref[...] = dx
        if emit_bf16:
            outs[1][...] = dx.astype(BF16)
        dw_ref[...] += jnp.sum(dhv * xh, axis=0, keepdims=True)

    outs = [(F32, "row")] + ([(BF16, "row")] if emit_bf16 else []) + [(F32, "vec")]
    return dict(fn=fn, ins=[(x, "row"), (r, "col"), (w, "vec"), (dres, "row")], outs=outs)


def _tail_loss(target, w):
    def fn(xv, first, t_ref, w_ref, loss_ref, dx_ref, dxb_ref, dw_ref):
        @pl.when(first)
        def _():
            loss_ref[...] = jnp.zeros_like(loss_ref)
            dw_ref[...] = jnp.zeros_like(dw_ref)

        r = lax.rsqrt(jnp.mean(xv * xv, axis=-1, keepdims=True) + EPS)
        xh = xv * r
        wv = w_ref[...]
        err = xh * wv - t_ref[...]
        row_loss = jnp.mean(err * err, axis=-1, keepdims=True)
        loss_ref[...] += 0.5 * jnp.sum(row_loss, axis=0, keepdims=True)
        dy = err * (1.0 / xv.shape[-1])
        dxh = dy * wv
        t = jnp.mean(dxh * xh, axis=-1, keepdims=True)
        dx = r * (dxh - xh * t)
        dx_ref[...] = dx
        dxb_ref[...] = dx.astype(BF16)
        dw_ref[...] += jnp.sum(dy * xh, axis=0, keepdims=True)

    return dict(fn=fn, ins=[(target, "row"), (w, "vec")],
                outs=[(F32, "one"), (F32, "row"), (BF16, "row"), (F32, "vec")])


GLA_TB = 512
GLA_NC = GLA_TB // CHUNK
GLA_UNROLL = 4


def _cumsum_rows(x, row, reverse):
    n = x.shape[0]
    s = 1
    while s < n:
        if not reverse:
            x = x + jnp.where(row >= s, pltpu.roll(x, s, 0), 0.0)
        else:
            x = x + jnp.where(row < n - s, pltpu.roll(x, n - s, 0), 0.0)
        s *= 2
    return x


def _gla_gates(uq, z, lbv):
    q = uq * _sigmoid(uq)
    sg = _sigmoid(z)
    sgn = _sigmoid(-z)
    f = lbv + (1.0 - lbv) * sg
    k = (1.0 - lbv) * sgn
    return q, sg, sgn, f, k


def _gla_decays(f, row, reverse):
    b = _cumsum_rows(jnp.log(f), row, reverse)
    if not reverse:
        bref, blast = b[CHUNK // 2 - 1:CHUNK // 2, :], b[CHUNK - 1:CHUNK, :]
    else:
        bref, blast = b[CHUNK // 2:CHUNK // 2 + 1, :], b[0:1, :]
    return b, bref, blast


def _gla_fwd(U, lb, *, f_block, reverse, name, ride=None, post=None):
    T = U.shape[0]
    nb = T // GLA_TB
    n_g = 0 if ride is None else len(ride["arrays"])
    n_p = 0 if post is None else 3

    def body(uq_ref, uf_ref, ui_ref, lb_ref, *rest):
        post_in, rest = rest[:n_p], rest[n_p:]
        g_in, rest = rest[:n_g], rest[n_g:]
        o_ref, st_ref = rest[:2]
        mix_ref = rest[2] if n_p else None
        rest = rest[2 + (n_p > 0):]
        g_out, rest = rest[:n_g], rest[n_g:]
        s_ref = rest[0]
        if n_g:
            start, finish = ride["halves"](g_in, g_out, *rest[1:])
            pl.when(pl.program_id(0) == 0)(start)

        @pl.when(pl.program_id(0) == 0)
        def _():
            s_ref[...] = jnp.zeros_like(s_ref)

        row = lax.broadcasted_iota(jnp.int32, (CHUNK, HG_D), 0)
        ri = lax.broadcasted_iota(jnp.int32, (CHUNK, CHUNK), 0)
        ci = lax.broadcasted_iota(jnp.int32, (CHUNK, CHUNK), 1)
        mask = (ri <= ci) if reverse else (ri >= ci)

        def chunk(j, carry):
            c = (GLA_NC - 1 - j) if reverse else j
            rows = pl.ds(pl.multiple_of(c * CHUNK, CHUNK), CHUNK)
            for h in range(HG_HEADS):
                cols = pl.ds(h * HG_D, HG_D)
                v = ui_ref[rows, cols]
                q, _, _, f, k = _gla_gates(uq_ref[rows, cols], uf_ref[rows, cols], lb_ref[:, cols])
                b, bref, blast = _gla_decays(f, row, reverse)
                s = jnp.where(mask, _dot_nt(q * jnp.exp(b - bref), k * jnp.exp(bref - b)), 0.0)
                st = s_ref[h]
                st_ref[c, h] = st
                o = _dot(s, v) + _dot_nt(q * jnp.exp(b), st)
                if n_p:
                    other_ref, ug_ref, w_ref = post_in
                    o = o + other_ref[rows, cols]
                    r = lax.rsqrt(jnp.mean(o * o, axis=-1, keepdims=True) + EPS)
                    ug = ug_ref[rows, cols]
                    mix_ref[rows, cols] = (o * r * w_ref[...] * (ug * _sigmoid(ug))).astype(BF16)
                o_ref[rows, cols] = o
                s_ref[h] = st * jnp.exp(blast) + _dot_tn(v, k * jnp.exp(blast - b))
            return carry

        lax.fori_loop(0, GLA_NC, chunk, 0, unroll=GLA_NC)
        if n_g:
            pl.when(pl.program_id(0) == nb - 1)(finish)

    blk = (lambda i: nb - 1 - i) if reverse else (lambda i: i)
    ucol = lambda cb: pl.BlockSpec((GLA_TB, HG_W), lambda i: (blk(i), cb))
    tok = pl.BlockSpec((GLA_TB, HG_W), lambda i: (blk(i), 0))
    in_specs = [ucol(0), ucol(f_block), ucol(3), pl.BlockSpec((1, HG_W), lambda i: (0, 0))]
    args = [U, U, U, lb]
    out_specs = [tok, pl.BlockSpec((GLA_NC, HG_HEADS, HG_D, HG_D), lambda i: (blk(i), 0, 0, 0))]
    out_shape = [jax.ShapeDtypeStruct((T, HG_W), F32), jax.ShapeDtypeStruct((T // CHUNK, HG_HEADS, HG_D, HG_D), F32)]
    if n_p:
        in_specs += [tok, ucol(4), pl.BlockSpec((1, HG_D), lambda i: (0, 0))]
        args += [post[0], U, post[1]]
        out_specs.append(tok)
        out_shape.append(jax.ShapeDtypeStruct((T, HG_W), BF16))
    return pl.pallas_call(
        body, name=name, grid=(nb,), in_specs=in_specs + [ANY] * n_g, out_specs=out_specs + [ANY] * n_g,
        out_shape=out_shape + (ride["out_shape"] if n_g else []),
        scratch_shapes=[pltpu.VMEM((HG_HEADS, HG_D, HG_D), F32)] + (ride["scratch"] if n_g else []),
        compiler_params=pltpu.CompilerParams(dimension_semantics=("arbitrary",), vmem_limit_bytes=VMEM_LIMIT,
                                             has_side_effects=bool(n_g)),
    )(*args, *(ride["arrays"] if n_g else []))


def _gla_bwd(U, lb, do, states, *, f_block, reverse, name, prev=None):
    T = U.shape[0]
    nb = T // GLA_TB
    final = prev is not None

    def body(uq_ref, uf_ref, ui_ref, lb_ref, do_ref, st_ref, *rest):
        if final:
            dqp_ref, dzp_ref, dvp_ref, dug_ref, out_ref, dlb_ref, ds_ref = rest
        else:
            dq_ref, dz_ref, dv_ref, dlb_ref, ds_ref = rest

        @pl.when(pl.program_id(0) == 0)
        def _():
            ds_ref[...] = jnp.zeros_like(ds_ref)
            dlb_ref[...] = jnp.zeros_like(dlb_ref)

        row = lax.broadcasted_iota(jnp.int32, (CHUNK, HG_D), 0)
        ri = lax.broadcasted_iota(jnp.int32, (CHUNK, CHUNK), 0)
        ci = lax.broadcasted_iota(jnp.int32, (CHUNK, CHUNK), 1)
        mask = (ri <= ci) if reverse else (ri >= ci)

        def chunk(j, carry):
            c = j if reverse else (GLA_NC - 1 - j)
            rows = pl.ds(pl.multiple_of(c * CHUNK, CHUNK), CHUNK)
            for h in range(HG_HEADS):
                cols = pl.ds(h * HG_D, HG_D)
                v = ui_ref[rows, cols]
                lbv = lb_ref[:, cols]
                uq = uq_ref[rows, cols]
                q, sg, sgn, f, k = _gla_gates(uq, uf_ref[rows, cols], lbv)
                b, bref, blast = _gla_decays(f, row, reverse)
                eq, ek, eb, el, dec = (jnp.exp(b - bref), jnp.exp(bref - b), jnp.exp(b), jnp.exp(blast - b),
                                       jnp.exp(blast))
                qin, kin, qb, klast = q * eq, k * ek, q * eb, k * el
                dov = do_ref[rows, cols]
                st = st_ref[c, h]
                dst = ds_ref[h]
                p = jnp.where(mask, _dot_nt(qin, kin), 0.0)
                dp = jnp.where(mask, _dot_nt(dov, v), 0.0)
                dqin = _dot(dp, kin)
                dkin = _dot_tn(dp, qin)
                dv = _dot_tn(p, dov) + _dot_nt(klast, dst)
                dqb = _dot(dov, st)
                dklast = _dot(v, dst)
                ds_ref[h] = _dot_tn(dov, qb) + dst * dec
                db = dqin * qin - dkin * kin + dqb * qb - dklast * klast
                extra = (jnp.sum(dklast * klast, axis=0, keepdims=True)
                         + dec * jnp.sum(st * dst, axis=0, keepdims=True))
                dg = _cumsum_rows(db, row, not reverse) + extra
                dq = dqin * eq + dqb * eb
                dk = dkin * ek + dklast * el
                dfk = dg / f - dk
                dz = (dfk * (1.0 - lbv) * sg * sgn).astype(BF16)
                dlb_ref[:, cols] += jnp.sum(dfk * sgn, axis=0, keepdims=True)
                if final:
                    sq = _sigmoid(uq)
                    col = lambda blk: pl.ds(blk * HG_W + h * HG_D, HG_D)
                    out_ref[rows, col(0)] = ((dq + dqp_ref[rows, cols]) * (sq * (1.0 + uq * (1.0 - sq)))).astype(BF16)
                    out_ref[rows, col(1)] = dzp_ref[rows, cols]
                    out_ref[rows, col(2)] = dz
                    out_ref[rows, col(3)] = (dv + dvp_ref[rows, cols]).astype(BF16)
                    out_ref[rows, col(4)] = dug_ref[rows, cols]
                else:
                    dq_ref[rows, cols] = dq
                    dz_ref[rows, cols] = dz
                    dv_ref[rows, cols] = dv
            return carry

        lax.fori_loop(0, GLA_NC, chunk, 0, unroll=GLA_UNROLL)

    blk = (lambda i: i) if reverse else (lambda i: nb - 1 - i)
    ucol = lambda cb: pl.BlockSpec((GLA_TB, HG_W), lambda i: (blk(i), cb))
    tok = pl.BlockSpec((GLA_TB, HG_W), lambda i: (blk(i), 0))
    vec = pl.BlockSpec((1, HG_W), lambda i: (0, 0))
    in_specs = [ucol(0), ucol(f_block), ucol(3), vec, tok,
                pl.BlockSpec((GLA_NC, HG_HEADS, HG_D, HG_D), lambda i: (blk(i), 0, 0, 0))]
    vec_shape = jax.ShapeDtypeStruct((1, HG_W), F32)
    if final:
        in_specs += [tok] * 4
        out_specs = [pl.BlockSpec((GLA_TB, 5 * HG_W), lambda i: (blk(i), 0)), vec]
        out_shape = [jax.ShapeDtypeStruct((T, 5 * HG_W), BF16), vec_shape]
    else:
        out_specs = [tok, tok, tok, vec]
        out_shape = [jax.ShapeDtypeStruct((T, HG_W), F32), jax.ShapeDtypeStruct((T, HG_W), BF16),
                     jax.ShapeDtypeStruct((T, HG_W), F32), vec_shape]
    return pl.pallas_call(
        body, name=name, grid=(nb,), in_specs=in_specs, out_specs=out_specs, out_shape=out_shape,
        scratch_shapes=[pltpu.VMEM((HG_HEADS, HG_D, HG_D), F32)],
        compiler_params=_params(("arbitrary",)),
    )(U, U, U, lb, do, states, *(prev if final else ()))


def _hg_post_bwd(dmix, o_sum, U, w, *, name, tm=512):
    T = o_sum.shape[0]

    def body(dm_ref, o_ref, ug_ref, w_ref, do_ref, dug_ref, dw_ref):
        @pl.when(pl.program_id(0) == 0)
        def _():
            dw_ref[...] = jnp.zeros_like(dw_ref)

        wv = w_ref[...]
        for h in range(HG_HEADS):
            cols = pl.ds(h * HG_D, HG_D)
            o = o_ref[:, cols]
            r = lax.rsqrt(jnp.mean(o * o, axis=-1, keepdims=True) + EPS)
            xh = o * r
            ug = ug_ref[:, cols]
            sg = _sigmoid(ug)
            dm = dm_ref[:, cols]
            dn = dm * (ug * sg)
            dug_ref[:, cols] = (dm * (xh * wv) * (sg * (1.0 + ug * (1.0 - sg)))).astype(BF16)
            dxh = dn * wv
            t = jnp.mean(dxh * xh, axis=-1, keepdims=True)
            do_ref[:, cols] = r * (dxh - xh * t)
            dw_ref[:, cols] += jnp.sum(dn * xh, axis=0, keepdims=True)

    tok = pl.BlockSpec((tm, HG_W), lambda i: (i, 0))
    vec = pl.BlockSpec((1, HG_W), lambda i: (0, 0))
    return pl.pallas_call(
        body, name=name, grid=(T // tm,),
        in_specs=[tok, tok, pl.BlockSpec((tm, HG_W), lambda i: (i, 4)), pl.BlockSpec((1, HG_D), lambda i: (0, 0))],
        out_specs=[tok, tok, vec],
        out_shape=[jax.ShapeDtypeStruct((T, HG_W), F32), jax.ShapeDtypeStruct((T, HG_W), BF16),
                   jax.ShapeDtypeStruct((1, HG_W), F32)],
        compiler_params=_params(("arbitrary",)),
    )(dmix, o_sum, U, w)


def _rope_tables(T):
    rows = T // GRID_W
    row = np.repeat(np.arange(rows), GRID_W).astype(np.float32)
    col = np.tile(np.arange(GRID_W), rows).astype(np.float32)
    axis_dim = ATT_DH // 2
    freqs = (np.float32(ROPE_THETA) ** (-np.arange(0, axis_dim, 2, dtype=np.float32) / np.float32(axis_dim))
             ).astype(np.float32)
    ang = np.concatenate([row[:, None] * freqs, col[:, None] * freqs], axis=-1).astype(np.float32)
    cos, sin = np.cos(ang), np.sin(ang)
    c = np.repeat(cos, 2, axis=-1)
    s = np.stack([-sin, sin], axis=-1).reshape(T, ATT_DH)
    return jnp.asarray(np.tile(c, (1, 2)), F32), jnp.asarray(np.tile(s, (1, 2)), F32)


def _head_blockdiag(width):
    shift = ATT_DH.bit_length() - 1
    ri = jnp.right_shift(lax.broadcasted_iota(jnp.int32, (width, width), 0), shift)
    ci = jnp.right_shift(lax.broadcasted_iota(jnp.int32, (width, width), 1), shift)
    return jnp.where(ri == ci, 1.0, 0.0).astype(BF16)


def _head_sum(x, bd):
    hi = x.astype(BF16)
    lo = (x - hi.astype(F32)).astype(BF16)
    return jnp.dot(hi, bd, preferred_element_type=F32) + jnp.dot(lo, bd, preferred_element_type=F32)


def _pair_swap(x, even):
    n = x.shape[-1]
    return jnp.where(even, pltpu.roll(x, n - 1, 1), pltpu.roll(x, 1, 1))


FA_TQ = 512


FA_TK = 512


def _cols_from_tokens(x, kv):
    w = ATT_G * ATT_DH
    xt = x[:, kv * w:(kv + 1) * w].T
    return jnp.concatenate([xt[g * ATT_DH:(g + 1) * ATT_DH, :] for g in range(ATT_G)], axis=1)


def _tokens_from_cols(c):
    tq = c.shape[1] // ATT_G
    return jnp.concatenate([c[:, g * tq:(g + 1) * tq] for g in range(ATT_G)], axis=0).T


def _store_cols(ref, x, norm_ref=None):
    for kv in range(ATT_KV):
        cols = _cols_from_tokens(x, kv).astype(BF16)
        ref[kv, 0] = cols
        if norm_ref is not None:
            cf = cols.astype(F32)
            norm_ref[kv, 0] = jnp.sqrt(jnp.sum(cf * cf, axis=0, keepdims=True))


def _att_prep_fwd(U, cos, sin, qw, kw, *, name):
    T = U.shape[0]
    tm = min(FA_TQ, T)
    R = ATT_G * tm
    scale = ATT_DH ** -0.5

    def head_rows(ref, x):
        xt = x.astype(F32).T
        for kv in range(ATT_KV):
            ref[kv, 0] = xt[kv * ATT_DH:(kv + 1) * ATT_DH, :].astype(BF16)

    def body(aq_ref, ak_ref, av_ref, c_ref, s_ref, qw_ref, kw_ref, q_ref, qn_ref, kmax_ref, kc_ref, vc_ref):
        @pl.when(pl.program_id(0) == 0)
        def _():
            kmax_ref[...] = jnp.zeros_like(kmax_ref)

        bd = _head_blockdiag(ATT_QW)
        c2, s2 = c_ref[...], s_ref[...]
        c8, s8 = jnp.tile(c2, (1, 4)), jnp.tile(s2, (1, 4))

        def norm_rope(x, w, c, s, bdm):
            r = lax.rsqrt(_head_sum(x * x, bdm) * (1.0 / ATT_DH) + EPS)
            y = x * r * w
            even = (lax.broadcasted_iota(jnp.int32, y.shape, 1) & 1) == 0
            return y * c + _pair_swap(y, even) * s

        _store_cols(q_ref, norm_rope(aq_ref[...], qw_ref[...], c8, s8, bd) * (scale * LOG2E), qn_ref)
        kb = norm_rope(ak_ref[...], kw_ref[...], c2, s2, bd[:ATT_KW, :ATT_KW]).astype(BF16)
        kf = kb.astype(F32)
        ksq = _head_sum(kf * kf, bd[:ATT_KW, :ATT_KW])
        kmax_ref[...] = jnp.maximum(kmax_ref[...], jnp.max(ksq, axis=0, keepdims=True))
        head_rows(kc_ref, kb)
        head_rows(vc_ref, av_ref[...].astype(BF16))

    kv_spec = pl.BlockSpec((tm, ATT_KW), lambda i: (i, 0))
    tk = min(FA_TK, T)
    per = tk // tm
    c_spec = pl.BlockSpec((ATT_KV, 1, ATT_DH, tm), lambda i: (0, i // per, 0, i % per))
    c_shape = jax.ShapeDtypeStruct((ATT_KV, T // tk, ATT_DH, tk), BF16)
    return pl.pallas_call(
        body, name=name, grid=(T // tm,),
        in_specs=[pl.BlockSpec((tm, ATT_QW), lambda i: (i, 5)),
                  pl.BlockSpec((tm, ATT_KW), lambda i: (i, 24)), pl.BlockSpec((tm, ATT_KW), lambda i: (i, 25)),
                  kv_spec, kv_spec,
                  pl.BlockSpec((1, ATT_QW), lambda i: (0, 0)), pl.BlockSpec((1, ATT_KW), lambda i: (0, 0))],
        out_specs=[pl.BlockSpec((ATT_KV, 1, ATT_DH, R), lambda i: (0, i, 0, 0)),
                   pl.BlockSpec((ATT_KV, 1, 1, R), lambda i: (0, i, 0, 0)), pl.BlockSpec((1, ATT_KW), lambda i: (0, 0)),
                   c_spec, c_spec],
        out_shape=[jax.ShapeDtypeStruct((ATT_KV, T // tm, ATT_DH, R), BF16),
                   jax.ShapeDtypeStruct((ATT_KV, T // tm, 1, R), F32), jax.ShapeDtypeStruct((1, ATT_KW), F32),
                   c_shape, c_shape],
        compiler_params=_params(("arbitrary",)),
    )(U, U, U, cos, sin, qw, kw)


def _att_prep_bwd(U, dq_c, dk_c, dv_c, cos, sin, qw, kw, *, name):
    T = U.shape[0]
    tm = min(FA_TQ, T)
    R = ATT_G * tm
    scale = ATT_DH ** -0.5

    def body(aq_ref, ak_ref, dq_ref, dk_ref, dv_ref, c_ref, s_ref, qw_ref, kw_ref, out_ref, dqw_ref, dkw_ref):
        @pl.when(pl.program_id(0) == 0)
        def _():
            dqw_ref[...] = jnp.zeros_like(dqw_ref)
            dkw_ref[...] = jnp.zeros_like(dkw_ref)

        bd = _head_blockdiag(ATT_QW)
        c2, s2 = c_ref[...], s_ref[...]
        c8, s8 = jnp.tile(c2, (1, 4)), jnp.tile(s2, (1, 4))

        def bwd(x, dy, w, c, s, bdm):
            even = (lax.broadcasted_iota(jnp.int32, x.shape, 1) & 1) == 0
            dn = dy * c - _pair_swap(dy, even) * s
            r = lax.rsqrt(_head_sum(x * x, bdm) * (1.0 / ATT_DH) + EPS)
            xh = x * r
            dxh = dn * w
            t = _head_sum(dxh * xh, bdm) * (1.0 / ATT_DH)
            return r * (dxh - xh * t), jnp.sum(dn * xh, axis=0, keepdims=True)

        dq = jnp.concatenate([_tokens_from_cols(dq_ref[kv, 0]) for kv in range(ATT_KV)], axis=1)
        da, dw = bwd(aq_ref[...], dq * scale, qw_ref[...], c8, s8, bd)
        out_ref[:, 0:ATT_QW] = da.astype(BF16)
        dqw_ref[...] += dw
        tokens = lambda ref: jnp.concatenate([ref[kv, 0] for kv in range(ATT_KV)], axis=0).T
        da, dw = bwd(ak_ref[...], tokens(dk_ref) * (1.0 / LOG2E), kw_ref[...], c2, s2, bd[:ATT_KW, :ATT_KW])
        out_ref[:, ATT_QW:ATT_QW + ATT_KW] = da.astype(BF16)
        dkw_ref[...] += dw
        out_ref[:, ATT_QW + ATT_KW:ATT_QW + 2 * ATT_KW] = tokens(dv_ref).astype(BF16)

    kv_spec = pl.BlockSpec((tm, ATT_KW), lambda i: (i, 0))
    qv = pl.BlockSpec((1, ATT_QW), lambda i: (0, 0))
    kv = pl.BlockSpec((1, ATT_KW), lambda i: (0, 0))
    w_att = ATT_QW + 2 * ATT_KW
    per = dk_c.shape[3] // tm
    c_spec = pl.BlockSpec((ATT_KV, 1, ATT_DH, tm), lambda i: (0, i // per, 0, i % per))
    return pl.pallas_call(
        body, name=name, grid=(T // tm,),
        in_specs=[pl.BlockSpec((tm, ATT_QW), lambda i: (i, 5)), pl.BlockSpec((tm, ATT_KW), lambda i: (i, 24)),
                  pl.BlockSpec((ATT_KV, 1, ATT_DH, R), lambda i: (0, i, 0, 0)), c_spec, c_spec, kv_spec, kv_spec, qv, kv],
        out_specs=[pl.BlockSpec((tm, w_att), lambda i: (i, 0)), qv, kv],
        out_shape=[jax.ShapeDtypeStruct((T, w_att), BF16),
                   jax.ShapeDtypeStruct((1, ATT_QW), F32), jax.ShapeDtypeStruct((1, ATT_KW), F32)],
        compiler_params=_params(("arbitrary",)),
    )(U, U, dq_c, dk_c, dv_c, cos, sin, qw, kw)


def _scores(k_ref, j, qv):
    return lax.dot_general(k_ref[0, j], qv, (((0,), (0,)), ((), ())), preferred_element_type=F32)


def _flash_fwd(q_c, k_c, v_c, *, name):
    _, nq, _, R = q_c.shape
    _, n_k, _, tk = v_c.shape

    def body(q_ref, k_ref, v_ref, o_ref, lse_ref, acc_ref):
        qv = q_ref[0, 0]
        acc_ref[...] = jnp.zeros_like(acc_ref)

        def step(j, carry):
            m, l = carry
            s = _scores(k_ref, j, qv)
            m_new = jnp.maximum(m, jnp.max(s, axis=0, keepdims=True))
            alpha = jnp.exp2(m - m_new)
            p = jnp.exp2(s - m_new)
            l = alpha * l + jnp.sum(p, axis=0, keepdims=True)
            acc_ref[...] = alpha * acc_ref[...] + jnp.dot(v_ref[0, j], p.astype(BF16), preferred_element_type=F32)
            return m_new, l

        m, l = lax.fori_loop(0, n_k, step, (jnp.full((1, R), -jnp.inf, F32), jnp.zeros((1, R), F32)))
        o_ref[0, 0] = acc_ref[...] / l
        lse_ref[0, 0] = m + jnp.log2(l)

    cspec = pl.BlockSpec((1, 1, ATT_DH, R), lambda h, i: (h, i, 0, 0))
    kspec = pl.BlockSpec((1, n_k, ATT_DH, tk), lambda h, i: (h, 0, 0, 0))
    return pl.pallas_call(
        body, name=name, grid=(ATT_KV, nq),
        in_specs=[cspec, kspec, kspec],
        out_specs=[cspec, pl.BlockSpec((1, 1, 1, R), lambda h, i: (h, i, 0, 0))],
        out_shape=[jax.ShapeDtypeStruct((ATT_KV, nq, ATT_DH, R), F32), jax.ShapeDtypeStruct((ATT_KV, nq, 1, R), F32)],
        scratch_shapes=[pltpu.VMEM((ATT_DH, R), F32)],
        compiler_params=_params(("parallel", "parallel")),
    )(q_c, k_c, v_c)


FA_BOUND_MAX = 40.0 * LOG2E


def _flash_fwd_bounded(q_c, k_c, v_c, m_c, *, name):
    _, nq, _, R = q_c.shape
    _, n_k, _, tk = v_c.shape

    def body(q_ref, k_ref, v_ref, m_ref, o_ref, lse_ref, acc_ref):
        qv = q_ref[0, 0]
        m = m_ref[0, 0]
        acc_ref[...] = jnp.zeros_like(acc_ref)

        per = math.gcd(n_k, 8)

        def step(jj, l8):
            pv = None
            for u in range(per):
                j = per * jj + u
                p = jnp.exp2(_scores(k_ref, j, qv) - m)
                l8 = l8 + jnp.sum(p.reshape(tk // 8, 8, R), axis=0)
                d = jnp.dot(v_ref[0, j], p.astype(BF16), preferred_element_type=F32)
                pv = d if pv is None else pv + d
            acc_ref[...] += pv
            return l8

        l8 = lax.fori_loop(0, n_k // per, step, jnp.zeros((8, R), F32))
        l = jnp.sum(l8, axis=0, keepdims=True)
        o_ref[0, 0] = acc_ref[...] / l
        lse_ref[0, 0] = m + jnp.log2(l)

    cspec = pl.BlockSpec((1, 1, ATT_DH, R), lambda h, i: (h, i, 0, 0))
    kspec = pl.BlockSpec((1, n_k, ATT_DH, tk), lambda h, i: (h, 0, 0, 0))
    vspec = pl.BlockSpec((1, 1, 1, R), lambda h, i: (h, i, 0, 0))
    return pl.pallas_call(
        body, name=name, grid=(ATT_KV, nq),
        in_specs=[cspec, kspec, kspec, vspec],
        out_specs=[cspec, vspec],
        out_shape=[jax.ShapeDtypeStruct((ATT_KV, nq, ATT_DH, R), F32), jax.ShapeDtypeStruct((ATT_KV, nq, 1, R), F32)],
        scratch_shapes=[pltpu.VMEM((ATT_DH, R), F32)],
        compiler_params=_params(("parallel", "parallel")),
    )(q_c, k_c, v_c, m_c)


CHIP_MASKS = [(1, 0, 0), (0, 1, 0), (1, 1, 0)]


def _chip_slot(p):
    return 2 * p[0] + p[1]


def _flash_bwd(q_c, k_c, v_c, do_c, lse, delta, *, name, ride=None):
    _, nq, _, R = q_c.shape
    _, n_k, _, tk = k_c.shape
    n_ride = 0 if ride is None else len(ride["arrays"])

    def body(qc_ref, kc_ref, vc_ref, doc_ref, lse_ref, delta_ref, *rest):
        ride_in, rest = rest[:n_ride], rest[n_ride:]
        dq_ref, dk_ref, dv_ref = rest[:3]
        ride_out, rest = rest[3:3 + n_ride], rest[3 + n_ride:]
        acc_ref = rest[0]
        kv = pl.program_id(0)
        if n_ride:
            start, finish = ride["halves"](ride_in, ride_out, *rest[1:])
            pl.when((kv == 0) & (pl.program_id(1) == 0))(start)

        @pl.when(pl.program_id(1) == 0)
        def _():
            dk_ref[...] = jnp.zeros_like(dk_ref)
            dv_ref[...] = jnp.zeros_like(dv_ref)

        qc, doc = qc_ref[0, 0], doc_ref[0, 0]
        lsev, delta = lse_ref[0, 0], delta_ref[0, 0]
        acc_ref[...] = jnp.zeros_like(acc_ref)
        nt = (((1,), (1,)), ((), ()))

        def step(j, carry):
            p = jnp.exp2(_scores(kc_ref, j, qc) - lsev)
            dp = _scores(vc_ref, j, doc)
            ds = (p * (dp - delta)).astype(BF16)
            acc_ref[...] += jnp.dot(kc_ref[0, j], ds, preferred_element_type=F32)
            dk_ref[0, j] += lax.dot_general(qc, ds, nt, preferred_element_type=F32)
            dv_ref[0, j] += lax.dot_general(doc, p.astype(BF16), nt, preferred_element_type=F32)
            return carry

        lax.fori_loop(0, n_k, step, 0, unroll=2)
        dq_ref[0, 0] = acc_ref[...]

        if n_ride:
            pl.when((kv == ATT_KV - 1) & (pl.program_id(1) == nq - 1))(finish)

    cspec = pl.BlockSpec((1, 1, ATT_DH, R), lambda h, i: (h, i, 0, 0))
    vspec = pl.BlockSpec((1, 1, 1, R), lambda h, i: (h, i, 0, 0))
    kspec = pl.BlockSpec((1, n_k, ATT_DH, tk), lambda h, i: (h, 0, 0, 0))
    k_shape = jax.ShapeDtypeStruct(k_c.shape, F32)
    return pl.pallas_call(
        body, name=name, grid=(ATT_KV, nq),
        in_specs=[cspec, kspec, kspec, cspec, vspec, vspec] + [ANY] * n_ride,
        out_specs=[cspec, kspec, kspec] + [ANY] * n_ride,
        out_shape=[jax.ShapeDtypeStruct((ATT_KV, nq, ATT_DH, R), F32), k_shape, k_shape]
                  + (ride["out_shape"] if n_ride else []),
        scratch_shapes=[pltpu.VMEM((ATT_DH, R), F32)] + (ride["scratch"] if n_ride else []),
        compiler_params=pltpu.CompilerParams(dimension_semantics=("arbitrary", "arbitrary"),
                                             vmem_limit_bytes=VMEM_LIMIT, has_side_effects=bool(n_ride)),
    )(q_c, k_c, v_c, do_c, lse, delta, *(ride["arrays"] if n_ride else []))


def _att_post_fwd(o_c, w, *, name):
    _, nq, _, R = o_c.shape
    tm = R // ATT_G
    T = nq * tm

    def body(oc_ref, w_ref, o_ref, out_ref):
        ov = jnp.concatenate([_tokens_from_cols(oc_ref[kv, 0]) for kv in range(ATT_KV)], axis=1)
        r = lax.rsqrt(jnp.mean(ov * ov, axis=-1, keepdims=True) + EPS)
        o_ref[...] = ov
        out_ref[...] = (ov * r * w_ref[...]).astype(BF16)

    tok = pl.BlockSpec((tm, ATT_QW), lambda i: (i, 0))
    return pl.pallas_call(
        body, name=name, grid=(nq,),
        in_specs=[pl.BlockSpec((ATT_KV, 1, ATT_DH, R), lambda i: (0, i, 0, 0)), pl.BlockSpec((1, ATT_QW), lambda i: (0, 0))],
        out_specs=[tok, tok],
        out_shape=[jax.ShapeDtypeStruct((T, ATT_QW), F32), jax.ShapeDtypeStruct((T, ATT_QW), BF16)],
        compiler_params=_params(("parallel",)),
    )(o_c, w)


def _att_post_bwd(dmix, o, w, *, name):
    T = o.shape[0]
    tm = min(FA_TQ, T)
    R = ATT_G * tm

    def body(dm_ref, o_ref, w_ref, do_ref, delta_ref, dw_ref):
        @pl.when(pl.program_id(0) == 0)
        def _():
            dw_ref[...] = jnp.zeros_like(dw_ref)

        ov = o_ref[...]
        r = lax.rsqrt(jnp.mean(ov * ov, axis=-1, keepdims=True) + EPS)
        xh = ov * r
        dm = dm_ref[...]
        dxh = dm * w_ref[...]
        t = jnp.mean(dxh * xh, axis=-1, keepdims=True)
        do = r * (dxh - xh * t)
        _store_cols(do_ref, do)
        dob = do.astype(BF16).astype(F32)
        for kv in range(ATT_KV):
            delta_ref[kv, 0] = jnp.sum(_cols_from_tokens(dob * ov, kv), axis=0, keepdims=True)
        dw_ref[...] += jnp.sum(dm * xh, axis=0, keepdims=True)

    tok = pl.BlockSpec((tm, ATT_QW), lambda i: (i, 0))
    vec = pl.BlockSpec((1, ATT_QW), lambda i: (0, 0))
    return pl.pallas_call(
        body, name=name, grid=(T // tm,),
        in_specs=[pl.BlockSpec((tm, ATT_QW), lambda i: (i, 1)), tok, vec],
        out_specs=[pl.BlockSpec((ATT_KV, 1, ATT_DH, R), lambda i: (0, i, 0, 0)),
                   pl.BlockSpec((ATT_KV, 1, 1, R), lambda i: (0, i, 0, 0)), vec],
        out_shape=[jax.ShapeDtypeStruct((ATT_KV, T // tm, ATT_DH, R), BF16),
                   jax.ShapeDtypeStruct((ATT_KV, T // tm, 1, R), F32), jax.ShapeDtypeStruct((1, ATT_QW), F32)],
        compiler_params=_params(("arbitrary",)),
    )(dmix, o, w)


def _ffn_up(h2, wg_t, wu_t, *, name, tm=512):
    T = h2.shape[0]
    tn = _pick(D_FF, 1408)
    nt = (((1,), (1,)), ((), ()))

    def body(h_ref, wg_ref, wu_ref, g_ref, u_ref, a_ref):
        hv = h_ref[...]
        g = lax.dot_general(hv, wg_ref[...], nt, preferred_element_type=F32)
        u = lax.dot_general(hv, wu_ref[...], nt, preferred_element_type=F32)
        g_ref[...] = g.astype(BF16)
        u_ref[...] = u.astype(BF16)
        a_ref[...] = (g * _sigmoid(g) * u).astype(BF16)

    wspec = pl.BlockSpec((tn, D_MODEL), lambda i, j: (j, 0))
    ospec = pl.BlockSpec((tm, tn), lambda i, j: (i, j))
    return pl.pallas_call(
        body, name=name, grid=(T // tm, D_FF // tn),
        in_specs=[pl.BlockSpec((tm, D_MODEL), lambda i, j: (i, 0)), wspec, wspec],
        out_specs=[ospec] * 3, out_shape=[jax.ShapeDtypeStruct((T, D_FF), BF16)] * 3,
        compiler_params=_params(("parallel", "arbitrary")),
    )(h2, wg_t, wu_t)


def _ffn_act_bwd(dx2b, w_down, gate, up, *, name, tm=512):
    T = dx2b.shape[0]
    tn = _pick(D_FF, 1408)

    def body(dx_ref, w_ref, g_ref, u_ref, dg_ref, du_ref):
        da = lax.dot_general(dx_ref[...], w_ref[...], (((1,), (1,)), ((), ())), preferred_element_type=F32)
        g = g_ref[...].astype(F32)
        u = u_ref[...].astype(F32)
        sg = _sigmoid(g)
        dg_ref[...] = (da * u * (sg * (1.0 + g * (1.0 - sg)))).astype(BF16)
        du_ref[...] = (da * (g * sg)).astype(BF16)

    ospec = pl.BlockSpec((tm, tn), lambda i, j: (i, j))
    return pl.pallas_call(
        body, name=name, grid=(T // tm, D_FF // tn),
        in_specs=[pl.BlockSpec((tm, D_MODEL), lambda i, j: (i, 0)),
                  pl.BlockSpec((tn, D_MODEL), lambda i, j: (j, 0)), ospec, ospec],
        out_specs=[ospec] * 2, out_shape=[jax.ShapeDtypeStruct((T, D_FF), BF16)] * 2,
        compiler_params=_params(("parallel", "arbitrary")),
    )(dx2b, w_down, gate, up)


def _adam_math(w, g, m, v):
    m = ADAM_B1 * m + (1.0 - ADAM_B1) * g
    v = ADAM_B2 * v + (1.0 - ADAM_B2) * (g * g)
    m_hat = m / (1.0 - ADAM_B1 ** ADAM_STEP)
    v_hat = v / (1.0 - ADAM_B2 ** ADAM_STEP)
    delta = -ADAM_LR * (m_hat / (jnp.sqrt(v_hat) + ADAM_EPS) + ADAM_WD * w)
    return delta, m, v


def _adamw(parts, w, m, v, *, name, tr_cap=256):
    P, R, C = parts.shape
    tr = R
    for t in range(8, min(R, tr_cap) + 1, 8):
        if R % t == 0:
            tr = t

    def body(p_ref, w_ref, m_ref, v_ref, g_ref, d_ref, nm_ref, nv_ref):
        g = p_ref[0].astype(F32)
        for j in range(1, P):
            g = g + p_ref[j].astype(F32)
        d, nm, nv = _adam_math(w_ref[...], g, m_ref[...], v_ref[...])
        g_ref[...] = g
        d_ref[...] = d
        nm_ref[...] = nm
        nv_ref[...] = nv

    blk = pl.BlockSpec((tr, C), lambda i: (i, 0))
    return pl.pallas_call(
        body, name=name, grid=(R // tr,),
        in_specs=[pl.BlockSpec((P, tr, C), lambda i: (0, i, 0)), blk, blk, blk],
        out_specs=[blk] * 4, out_shape=[jax.ShapeDtypeStruct((R, C), F32)] * 4,
        compiler_params=_params(("parallel",)),
    )(parts, w, m, v)


def _gather_halves(ins, outs, send_sems, recv_sems, local_sems):
    n = len(ins)
    x, y, c = lax.axis_index("x"), lax.axis_index("y"), lax.axis_index("c")
    me, sibling = (x, y, c), (x, y, 1 - c)
    chips = [(1 - x, y), (x, 1 - y), (1 - x, 1 - y)]

    def slot(p):
        return 4 * p[0] + 2 * p[1] + p[2]

    def copy(a, k, block, to, src=None):
        dst = outs[a].at[slot(block)]
        return pltpu.make_async_remote_copy(
            src_ref=dst if src is None else src, dst_ref=dst,
            send_sem=send_sems.at[a * 7 + k], recv_sem=recv_sems.at[a * 7 + k],
            device_id=to, device_id_type=MESH)

    mine = [pltpu.make_async_copy(ins[a], outs[a].at[slot(me)], local_sems.at[a]) for a in range(n)]
    first = []
    for a in range(n):
        first.append(copy(a, 0, me, sibling, src=ins[a]))
        first += [copy(a, 1 + j, me, (*chip, c), src=ins[a]) for j, chip in enumerate(chips)]

    def start():
        for cp in mine + first:
            cp.start()

    def finish():
        passed = []
        for j, chip in enumerate(chips):
            for a in range(n):
                copy(a, 1 + j, (*chip, c), me).wait_recv()
                cp = copy(a, 4 + j, (*chip, c), sibling)
                cp.start()
                passed.append(cp)
        for a in range(n):
            copy(a, 0, sibling, me).wait_recv()
            for j, chip in enumerate(chips):
                copy(a, 4 + j, (*chip, 1 - c), me).wait_recv()
        for cp in first + passed:
            cp.wait_send()
        for cp in mine:
            cp.wait()

    return start, finish


def _gather_scratch(n):
    return [pltpu.SemaphoreType.DMA((7 * n,)), pltpu.SemaphoreType.DMA((7 * n,)), pltpu.SemaphoreType.DMA((n,))]


def _gathered_shapes(xs):
    return [jax.ShapeDtypeStruct((N_DEV,) + x.shape, x.dtype) for x in xs]


def _ride_gather(xs):
    xs = list(xs)
    return dict(arrays=xs, out_shape=_gathered_shapes(xs), scratch=_gather_scratch(len(xs)), halves=_gather_halves)


def _ride_chips(gs):
    gs = list(gs)
    n = len(gs)

    def halves(ins, outs, send_sems, recv_sems, local_sems):
        mine, copies = _exchange_copies(ins, outs, send_sems, recv_sems, local_sems, masks=CHIP_MASKS, slot=_chip_slot)

        def start():
            for cp in mine:
                cp.start()
            for send, _ in copies:
                send.start()

        def finish():
            for send, recv in copies:
                recv.wait_recv()
                send.wait_send()
            for cp in mine:
                cp.wait()

        return start, finish

    n_sem = len(CHIP_MASKS) * n
    return dict(arrays=gs, out_shape=[jax.ShapeDtypeStruct(g.shape, g.dtype) for g in gs], halves=halves,
                scratch=[pltpu.SemaphoreType.DMA((n_sem,)), pltpu.SemaphoreType.DMA((n_sem,)),
                         pltpu.SemaphoreType.DMA((n,))])


ALL_MASKS = [(mx, my, mc) for mx in (0, 1) for my in (0, 1) for mc in (0, 1)][1:]


def _flip(v, bit):
    return 1 - v if bit else v


def _exchange_copies(ins, outs, send_sems, recv_sems, local_sems, *, masks, slot):
    n, n_peers = len(ins), len(masks)
    x, y, c = lax.axis_index("x"), lax.axis_index("y"), lax.axis_index("c")
    my_slot = slot((x, y, c))
    mine = [pltpu.make_async_copy(ins[a].at[my_slot], outs[a].at[my_slot], local_sems.at[a]) for a in range(n)]
    copies = []
    for a in range(n):
        for k, (mx, my, mc) in enumerate(masks):
            peer = (_flip(x, mx), _flip(y, my), _flip(c, mc))
            peer_slot = slot(peer)
            sems = dict(send_sem=send_sems.at[a * n_peers + k], recv_sem=recv_sems.at[a * n_peers + k],
                        device_id=peer, device_id_type=MESH)
            copies.append((
                pltpu.make_async_remote_copy(src_ref=ins[a].at[peer_slot], dst_ref=outs[a].at[my_slot], **sems),
                pltpu.make_async_remote_copy(src_ref=ins[a].at[peer_slot], dst_ref=outs[a].at[peer_slot], **sems)))
    return mine, copies


def _send_to_all(v, *, name):
    def body(v_ref, out_ref, send_sems, recv_sems, local_sem):
        x, y, c = lax.axis_index("x"), lax.axis_index("y"), lax.axis_index("c")
        me = 4 * x + 2 * y + c
        mine = pltpu.make_async_copy(v_ref, out_ref.at[me], local_sem)
        mine.start()
        copies = []
        for k, (mx, my, mc) in enumerate(ALL_MASKS):
            peer = (_flip(x, mx), _flip(y, my), _flip(c, mc))
            peer_id = 4 * peer[0] + 2 * peer[1] + peer[2]
            sems = dict(send_sem=send_sems.at[k], recv_sem=recv_sems.at[k], device_id=peer, device_id_type=MESH)
            copies.append((pltpu.make_async_remote_copy(src_ref=v_ref, dst_ref=out_ref.at[me], **sems),
                           pltpu.make_async_remote_copy(src_ref=v_ref, dst_ref=out_ref.at[peer_id], **sems)))
        for send, _ in copies:
            send.start()
        for send, recv in copies:
            recv.wait_recv()
            send.wait_send()
        mine.wait()

    n_peers = len(ALL_MASKS)
    return pl.pallas_call(
        body, name=name, in_specs=[ANY], out_specs=ANY,
        out_shape=jax.ShapeDtypeStruct((N_DEV,) + v.shape, v.dtype),
        scratch_shapes=[pltpu.SemaphoreType.DMA((n_peers,)), pltpu.SemaphoreType.DMA((n_peers,)),
                        pltpu.SemaphoreType.DMA],
        compiler_params=pltpu.CompilerParams(has_side_effects=True),
    )(v)


SWAP_ROW_CHUNKS = 4


def _ride_swap(gs):
    gs = list(gs)
    n = len(gs)

    def halves(ins, outs, send_sems, recv_sems):
        x, y, c = lax.axis_index("x"), lax.axis_index("y"), lax.axis_index("c")
        sibling = dict(device_id=(x, y, 1 - c), device_id_type=MESH)

        def start():
            for a in range(n):
                Q, _, R, _ = ins[a].shape
                rows = R // SWAP_ROW_CHUNKS
                for q in range(Q):
                    for j in range(SWAP_ROW_CHUNKS):
                        part = pl.ds(j * rows, rows)
                        pltpu.make_async_remote_copy(src_ref=ins[a].at[q, 1 - c, part], dst_ref=outs[a].at[q, part],
                                                     send_sem=send_sems.at[a], recv_sem=recv_sems.at[a], **sibling).start()

        def finish():
            for a in range(n):
                pltpu.make_async_remote_copy(src_ref=outs[a], dst_ref=outs[a], send_sem=send_sems.at[a],
                                             recv_sem=recv_sems.at[a], **sibling).wait()

        return start, finish

    return dict(arrays=gs, out_shape=[jax.ShapeDtypeStruct(g.shape[:1] + g.shape[2:], g.dtype) for g in gs],
                scratch=[pltpu.SemaphoreType.DMA((n,)), pltpu.SemaphoreType.DMA((n,))], halves=halves)


def _core_swap(gs, *, name):
    ride = _ride_swap(gs)
    n = len(gs)

    def body(*refs):
        start, finish = ride["halves"](refs[:n], refs[n:2 * n], *refs[2 * n:])
        start()
        finish()

    return pl.pallas_call(
        body, name=name, in_specs=[ANY] * n, out_specs=[ANY] * n, out_shape=ride["out_shape"],
        scratch_shapes=ride["scratch"], compiler_params=pltpu.CompilerParams(has_side_effects=True),
    )(*gs)


def _pair_sum(g, other, core, *, name, tr_cap=256):
    Q, _, R, C = g.shape
    tr = max(t for t in range(16, min(R, tr_cap) + 1, 16) if R % t == 0)

    def body(core_ref, g_ref, o_ref, out_ref):
        out_ref[0] = (g_ref[0, 0] + o_ref[0]).astype(BF16)

    return pl.pallas_call(
        body, name=name,
        grid_spec=pltpu.PrefetchScalarGridSpec(
            num_scalar_prefetch=1, grid=(Q, R // tr),
            in_specs=[pl.BlockSpec((1, 1, tr, C), lambda q, i, core_ref: (q, core_ref[0], i, 0)),
                      pl.BlockSpec((1, tr, C), lambda q, i, core_ref: (q, i, 0))],
            out_specs=pl.BlockSpec((1, tr, C), lambda q, i, core_ref: (q, i, 0))),
        out_shape=jax.ShapeDtypeStruct((Q, R, C), BF16),
        compiler_params=_params(("parallel", "parallel")),
    )(core, g, other)


def _pack_small(norm1, norm2, final, att, hg, qn, kn, lb=None, loss=None):
    z = lambda n: jnp.zeros((n,), F32)
    rows = [norm1.reshape(-1), norm2.reshape(-1), final.reshape(-1),
            jnp.concatenate([att.reshape(-1), z(512)]),
            jnp.concatenate([hg.reshape(-1), qn.reshape(-1), kn.reshape(-1), z(1024 - 256)]),
            z(1024) if lb is None else lb.reshape(-1),
            z(1024) if loss is None else jnp.concatenate([loss.reshape(-1), z(1023)]), z(1024)]
    return jnp.stack(rows, axis=0)


def _unpack_small(p):
    return (p[0:1, :], p[1:2, :], p[2, :], p[3:4, 0:512], p[4:5, 0:128], p[4:5, 128:192], p[4:5, 192:256])


def _fold_heads(dhg, dqn, dkn, *, name):
    def body(hg_ref, q_ref, k_ref, ohg_ref, oq_ref, ok_ref):
        def fold128(v):
            acc = v[:, 0:LANES]
            for j in range(1, v.shape[1] // LANES):
                acc = acc + v[:, j * LANES:(j + 1) * LANES]
            return acc

        ohg_ref[...] = fold128(hg_ref[...])
        q = fold128(q_ref[...])
        oq_ref[...] = q + pltpu.roll(q, ATT_DH, 1)
        k = k_ref[...]
        ok_ref[...] = k + pltpu.roll(k, ATT_DH, 1)

    return pl.pallas_call(body, name=name, out_shape=[jax.ShapeDtypeStruct((1, LANES), F32)] * 3)(dhg, dqn, dkn)


def _lb_grad(dlb_sum, lb, *, name):
    def body(d_ref, lb_ref, o_ref):
        lbv = lb_ref[...]
        gl = d_ref[...] * lbv * (1.0 - lbv)
        o_ref[0:1, :] = gl[0:1, :]
        o_ref[1:2, :] = -gl[0:1, :]
        o_ref[2:3, :] = gl[1:2, :]
        o_ref[3:4, :] = -gl[1:2, :]

    return pl.pallas_call(body, name=name, out_shape=jax.ShapeDtypeStruct((4, HG_W), F32))(dlb_sum, lb)


def _lower_bounds(lb_logits_full, *, name):
    def body(l_ref, o_ref):
        for d in range(2):
            l0, l1 = l_ref[2 * d:2 * d + 1, :], l_ref[2 * d + 1:2 * d + 2, :]
            mx = jnp.maximum(l0, l1)
            e0, e1 = jnp.exp(l0 - mx), jnp.exp(l1 - mx)
            o_ref[d:d + 1, :] = e0 / (e0 + e1)

    return pl.pallas_call(body, name=name, out_shape=jax.ShapeDtypeStruct((2, HG_W), F32))(
        lb_logits_full.reshape(4, HG_W))


def _local_step(x, target, norm1_w, w_in_t, lb, hg_norm_w, q_norm_w, k_norm_w, att_norm_w, w_out, norm2_w,
                w_g_t, w_u_t, w_down, final_norm_w, reduce_early=None, reduce_late=None, shards=None):
    T = x.shape[0]
    cos, sin = _rope_tables(T)
    qw8 = jnp.tile(q_norm_w, (1, ATT_HEADS))
    kw2 = jnp.tile(k_norm_w, (1, ATT_KV))

    if shards is None:
        h, r1 = _rms_fwd(x, norm1_w, name="norm1_fwd")
        U = _mm_nn([(h, w_in_t)], trans_b=True, name="in_proj")
        o_f, st_f = _gla_fwd(U, lb[0:1], f_block=1, reverse=False, name="gla_fwd_f")
    else:
        h, r1, g_in, g_lb = _rms_fwd(x, norm1_w, ride=_ride_gather([shards["w_in_t"], shards["lb_logits"]]),
                                     name="norm1_fwd")
        w_in_t = g_in.reshape(-1, D_MODEL)
        lb = _lower_bounds(g_lb.transpose(1, 0, 2).reshape(2, 2, -1), name="lower_bounds")
        U, g_gu = _mm_nn([(h, w_in_t)], trans_b=True, ride=_ride_gather([shards["w_gu_t"]]), name="in_proj")
        o_f, st_f, g_out, g_dn = _gla_fwd(U, lb[0:1], f_block=1, reverse=False,
                                          ride=_ride_gather([shards["w_out"], shards["w_down"]]), name="gla_fwd_f")
        g_gu = g_gu.reshape(2, -1, D_MODEL)
        w_g_t, w_u_t = g_gu[0], g_gu[1]
        w_out, w_down = g_out.reshape(-1, D_MODEL), g_dn.reshape(-1, D_MODEL)
    o_sum, st_b, mix_hg = _gla_fwd(U, lb[1:2], f_block=2, reverse=True, post=(o_f, hg_norm_w), name="gla_fwd_b")
    q_c, qn_c, kmax2, k_c, v_c = _att_prep_fwd(U, cos, sin, qw8, kw2, name="att_prep_fwd")
    kmax = jnp.sqrt(jnp.max(kmax2.reshape(ATT_KV, ATT_DH), axis=1))
    m_c = qn_c * (kmax * 1.001).reshape(ATT_KV, 1, 1, 1)
    o_c, lse = lax.cond(jnp.max(m_c) <= FA_BOUND_MAX,
                        lambda: _flash_fwd_bounded(q_c, k_c, v_c, m_c, name="flash_fwd_bounded"),
                        lambda: _flash_fwd(q_c, k_c, v_c, name="flash_fwd"))
    o_att, mix_att = _att_post_fwd(o_c, att_norm_w, name="att_post_fwd")
    x1, h2, r2 = _mm_nn([(mix_hg, w_out[:HG_W]), (mix_att, w_out[HG_W:])], residual=x, tail=_tail_rms_fwd(norm2_w),
                        name="out_proj")
    gate, up, act = _ffn_up(h2, w_g_t, w_u_t, name="ffn_up")
    loss, dx2, dx2b, d_final = _mm_nn([(act, w_down)], residual=x1,
                                      tail=_tail_loss(target, final_norm_w.reshape(1, D_MODEL)), name="ffn_down")

    d_gate, d_up = _ffn_act_bwd(dx2b, w_down, gate, up, name="ffn_act_bwd")
    dw_down = _mm_tn(act, dx2b, tma_cap=1408, name="dw_down")
    dw_g = _mm_tn(d_gate, h2, tma_cap=1408, name="dw_gate")
    dw_u = _mm_tn(d_up, h2, tma_cap=1408, name="dw_up")
    mine = None if reduce_early is None else reduce_early["slabs"](dw_g, dw_u, dw_down)
    dx1, dx1b, d_norm2, *theirs = _mm_nn([(d_gate, w_g_t), (d_up, w_u_t)], tm=256,
                                         ride=None if mine is None else _ride_swap(mine),
                                         tail=_tail_rms_bwd(x1, r2, norm2_w, dx2, emit_bf16=True), name="ffn_up_bwd")
    dmix = _mm_nn([(dx1b, w_out)], trans_b=True, name="out_proj_bwd")
    dw_out = _mm_tn(mix_att, dx1b, rows=(HG_W, D_MODEL), name="dw_out_att",
                    into=_mm_tn(mix_hg, dx1b, rows=(0, D_MODEL), name="dw_out_hg"))
    do_c, delta, d_att = _att_post_bwd(dmix, o_att, att_norm_w, name="att_post_bwd")
    ride = None if reduce_early is None else _ride_chips(reduce_early["sums"](mine, theirs, dw_out))
    dq_c, dk_c, dv_c, *rode = _flash_bwd(q_c, k_c, v_c, do_c, lse, delta, ride=ride, name="flash_bwd")
    dU_att, d_qn, d_kn = _att_prep_bwd(U, dq_c, dk_c, dv_c, cos, sin, qw8, kw2, name="att_prep_bwd")
    do_hg, du_g, d_hg = _hg_post_bwd(dmix, o_sum, U, hg_norm_w, name="hg_post_bwd")
    dq_f, dz_f, dv_f, dlb_f = _gla_bwd(U, lb[0:1], do_hg, st_f, f_block=1, reverse=False, name="gla_bwd_f")
    dU_hg, dlb_b = _gla_bwd(U, lb[1:2], do_hg, st_b, f_block=2, reverse=True, prev=(dq_f, dz_f, dv_f, du_g),
                            name="gla_bwd_b")
    w_hg = 5 * HG_W
    n_in = w_hg + dU_att.shape[1]
    dw_in = _mm_tn(dU_att, h, tma_cap=256, rows=(w_hg, n_in), name="dw_in_att",
                   into=_mm_tn(dU_hg, h, tma_cap=1280, rows=(0, n_in), name="dw_in_hg"))
    late = None if reduce_late is None else _ride_chips(reduce_late(dw_in))
    grad_x, d_norm1, *rode_late = _mm_nn([(dU_hg, w_in_t[:w_hg]), (dU_att, w_in_t[w_hg:])], ride=late,
                                         tail=_tail_rms_bwd(x, r1, norm1_w, dx1, emit_bf16=False), name="in_proj_bwd")
    d_hg, d_qn, d_kn = _fold_heads(d_hg, d_qn, d_kn, name="fold_heads")

    big = dict(w_in=dw_in, w_out=dw_out, w_g=dw_g, w_u=dw_u, w_down=dw_down)
    small = dict(norm1=d_norm1, norm2=d_norm2, final=d_final, att=d_att, hg=d_hg,
                 qn=d_qn[:, :ATT_DH], kn=d_kn[:, :ATT_DH], lb=jnp.concatenate([dlb_f, dlb_b], axis=0))
    return loss, grad_x, big, small, rode + rode_late, lb


def kernel(x, norm1_w, w_in, lb_logits, hg_norm_w, q_norm_w, k_norm_w, att_norm_w, w_out, norm2_w, w_gate_up, w_down, final_norm_w, loss_target, m_norm1_w, m_w_in, m_lb_logits, m_hg_norm_w, m_q_norm_w, m_k_norm_w, m_att_norm_w, m_w_out, m_norm2_w, m_w_gate_up, m_w_down, m_final_norm_w, v_norm1_w, v_w_in, v_lb_logits, v_hg_norm_w, v_q_norm_w, v_k_norm_w, v_att_norm_w, v_w_out, v_norm2_w, v_w_gate_up, v_w_down, v_final_norm_w):
    T = x.shape[1]
    me = 4 * lax.axis_index("x") + 2 * lax.axis_index("y") + lax.axis_index("c")
    c_in, r_out, c_gu, r_dn = w_in.shape[2], w_out.shape[1], w_gate_up.shape[2], w_down.shape[1]
    lb_cols = lb_logits.shape[2]

    shards = dict(w_in_t=w_in[0].T.astype(BF16), lb_logits=lb_logits.reshape(4, lb_cols),
                  w_gu_t=w_gate_up[0].T.astype(BF16), w_out=w_out[0].astype(BF16), w_down=w_down[0].astype(BF16))

    chips = N_DEV // 2
    core = lax.axis_index("c").astype(jnp.int32).reshape(1)
    by_owner = lambda g, r: g.reshape(chips, 2, r, D_MODEL)

    def pair_sums(mine, theirs, names):
        return [_pair_sum(g, o, core, name="pair_sum_" + nm) for g, o, nm in zip(mine, theirs, names)]

    def early_slabs(dw_g_t, dw_u_t, dw_down):
        half = lambda g: g.reshape(chips // 2, 2, c_gu, D_MODEL)
        return [half(dw_g_t), half(dw_u_t), by_owner(dw_down, r_dn)]

    def early_sums(mine, theirs, dw_out):
        s_out = by_owner(dw_out, r_out)
        c_out, c_g, c_u, c_dn = pair_sums([s_out] + mine, list(_core_swap([s_out], name="exchange_cores_out"))
                                          + list(theirs), ("w_out", "w_gate", "w_up", "w_down"))
        return [c_out, jnp.concatenate([c_g, c_u], axis=0), c_dn]

    def reduce_late(dw_in_t):
        mine = [by_owner(dw_in_t, c_in)]
        return pair_sums(mine, _core_swap(mine, name="exchange_cores_in"), ("w_in",))

    loss, grad_x, big, small, (p_out, p_gu, p_dn, p_in), lb = _local_step(
        x[0], loss_target[0], norm1_w, None, None, hg_norm_w, q_norm_w, k_norm_w, att_norm_w, None, norm2_w,
        None, None, None, final_norm_w, reduce_early=dict(slabs=early_slabs, sums=early_sums),
        reduce_late=reduce_late, shards=shards)
    p_gu, p_in = p_gu.transpose(0, 2, 1), p_in.transpose(0, 2, 1)

    packed = _pack_small(small["norm1"], small["norm2"], small["final"], small["att"], small["hg"],
                         small["qn"], small["kn"], small["lb"], loss)
    all_small = _send_to_all(packed, name="exchange_small")

    g_w_in, d_w_in, nm_w_in, nv_w_in = _adamw(p_in, w_in[0], m_w_in[0], v_w_in[0], name="adamw_w_in")
    g_w_out, d_w_out, nm_w_out, nv_w_out = _adamw(p_out, w_out[0], m_w_out[0], v_w_out[0], name="adamw_w_out")
    g_w_gu, d_w_gu, nm_w_gu, nv_w_gu = _adamw(p_gu, w_gate_up[0], m_w_gate_up[0], v_w_gate_up[0], name="adamw_w_gu")
    g_w_dn, d_w_dn, nm_w_dn, nv_w_dn = _adamw(p_dn, w_down[0], m_w_down[0], v_w_down[0], name="adamw_w_down")

    pk = lambda vecs: _pack_small(*vecs)
    w_pk = pk([norm1_w, norm2_w, final_norm_w, att_norm_w, hg_norm_w, q_norm_w, k_norm_w])
    m_pk = pk([m_norm1_w, m_norm2_w, m_final_norm_w, m_att_norm_w, m_hg_norm_w, m_q_norm_w, m_k_norm_w])
    v_pk = pk([v_norm1_w, v_norm2_w, v_final_norm_w, v_att_norm_w, v_hg_norm_w, v_q_norm_w, v_k_norm_w])
    g_pk, d_pk, nm_pk, nv_pk = _adamw(all_small, w_pk, m_pk, v_pk, name="adamw_small")

    dlb_sum = g_pk[5:6, :].reshape(2, HG_W)
    g_lb_full = _lb_grad(dlb_sum, lb, name="lb_grad")
    g_lb_mine = lax.dynamic_slice_in_dim(g_lb_full, me * lb_cols, lb_cols, axis=1)
    g_lb_s, d_lb, nm_lb, nv_lb = _adamw(g_lb_mine[None], lb_logits.reshape(4, lb_cols),
                                        m_lb_logits.reshape(4, lb_cols), v_lb_logits.reshape(4, lb_cols),
                                        name="adamw_lb")

    loss_total = g_pk[6, 0]

    def outs(big4, lb_arr, pk_arr):
        n1, n2, fin, att, hg, qn, kn = _unpack_small(pk_arr)
        b_in, b_out, b_gu, b_dn = big4
        return [n1, b_in[None], lb_arr.reshape(2, 2, lb_cols), hg, qn, kn, att, b_out[None], n2, b_gu[None],
                b_dn[None], fin]

    return (loss_total, grad_x[None],
            *outs((g_w_in, g_w_out, g_w_gu, g_w_dn), g_lb_s, g_pk),
            *outs((d_w_in, d_w_out, d_w_gu, d_w_dn), d_lb, d_pk),
            *outs((nm_w_in, nm_w_out, nm_w_gu, nm_w_dn), nm_lb, nm_pk),
            *outs((nv_w_in, nv_w_out, nv_w_gu, nv_w_dn), nv_lb, nv_pk))
```

```python
import math

import jax
import jax.numpy as jnp
import numpy as np
from jax import lax
from jax.experimental import pallas as pl
from jax.experimental.pallas import tpu as pltpu

F32 = jnp.float32
BF16 = jnp.bfloat16

N_DEV = 8
D_MODEL = 1024
EPS = 1e-6
HG_HEADS = 4
HG_D = 128
HG_W = HG_HEADS * HG_D
CHUNK = 64
ATT_HEADS = 8
ATT_KV = 2
ATT_G = ATT_HEADS // ATT_KV
ATT_DH = 64
ATT_QW = ATT_HEADS * ATT_DH
ATT_KW = ATT_KV * ATT_DH
GRID_W = 64
ROPE_THETA = 10000.0
D_FF = 2816
ADAM_LR, ADAM_B1, ADAM_B2, ADAM_EPS, ADAM_WD, ADAM_STEP = 0.001, 0.9, 0.999, 1e-08, 0.01, 10

LOG2E = math.log2(math.e)
LANES = 128
VMEM_LIMIT = 48 * 1024 * 1024
MESH = pl.DeviceIdType.MESH
ANY = pl.BlockSpec(memory_space=pl.ANY)


def _params(sem=None):
    return pltpu.CompilerParams(dimension_semantics=sem, vmem_limit_bytes=VMEM_LIMIT)


def _pick(n, cap):
    best = None
    for t in range(LANES, cap + 1, LANES):
        if n % t == 0:
            best = t
    assert best is not None, (n, cap)
    return best


def _sigmoid(x):
    return 1.0 / (1.0 + jnp.exp(-x))


def _dot(a, b):
    return jnp.dot(a.astype(BF16), b.astype(BF16), preferred_element_type=F32)


def _dot_nt(a, b):
    return lax.dot_general(a.astype(BF16), b.astype(BF16), (((1,), (1,)), ((), ())),
                           preferred_element_type=F32)


def _dot_tn(a, b):
    return lax.dot_general(a.astype(BF16), b.astype(BF16), (((0,), (0,)), ((), ())),
                           preferred_element_type=F32)


def _mm_nn(pairs, *, name, out_dtype=F32, residual=None, tm=512, tn_cap=None, trans_b=False, tail=None, ride=None):
    M = pairs[0][0].shape[0]
    N = pairs[0][1].shape[0 if trans_b else 1]
    tn = N if tn_cap is None else _pick(N, tn_cap)
    n_pairs = len(pairs)
    has_res = residual is not None
    dims = (((1,), (1,)), ((), ())) if trans_b else (((1,), (0,)), ((), ()))
    assert (tail is None and ride is None) or tn == N
    n_main = 2 * n_pairs + has_res
    n_ti = 0 if tail is None else len(tail["ins"])
    n_out = 1 if tail is None else len(tail["outs"])
    n_r = 0 if ride is None else len(ride["arrays"])
    n_in = n_main + n_ti + n_r

    def body(*refs):
        outs = refs[n_in:n_in + n_out]
        if n_r:
            start, finish = ride["halves"](refs[n_main + n_ti:n_in], refs[n_in + n_out:n_in + n_out + n_r],
                                           *refs[n_in + n_out + n_r:])
            pl.when(pl.program_id(0) == 0)(start)
        acc = None
        for i in range(n_pairs):
            d = lax.dot_general(refs[2 * i][...], refs[2 * i + 1][...], dims, preferred_element_type=F32)
            acc = d if acc is None else acc + d
        if has_res:
            acc = acc + refs[2 * n_pairs][...]
        if tail is None:
            outs[0][...] = acc.astype(out_dtype)
        else:
            tail["fn"](acc, pl.program_id(0) == 0, *refs[n_main:n_main + n_ti], *outs)
        if n_r:
            pl.when(pl.program_id(0) == M // tm - 1)(finish)

    kinds = {"row": ((tm, N), (M, N), lambda i, j: (i, 0)), "col": ((tm, 1), (M, 1), lambda i, j: (i, 0)),
             "vec": ((1, N), (1, N), lambda i, j: (0, 0)), "one": ((1, 1), (1, 1), lambda i, j: (0, 0))}
    in_specs, args = [], []
    for a, b in pairs:
        k = a.shape[1]
        b_spec = pl.BlockSpec((tn, k), lambda i, j: (j, 0)) if trans_b else pl.BlockSpec((k, tn), lambda i, j: (0, j))
        in_specs += [pl.BlockSpec((tm, k), lambda i, j: (i, 0)), b_spec]
        args += [a, b]
    if has_res:
        in_specs.append(pl.BlockSpec((tm, tn), lambda i, j: (i, j)))
        args.append(residual)
    if tail is None:
        out_specs = [pl.BlockSpec((tm, tn), lambda i, j: (i, j))]
        out_shape = [jax.ShapeDtypeStruct((M, N), out_dtype)]
    else:
        for arr, kind in tail["ins"]:
            in_specs.append(pl.BlockSpec(kinds[kind][0], kinds[kind][2]))
            args.append(arr)
        out_specs = [pl.BlockSpec(kinds[kind][0], kinds[kind][2]) for _, kind in tail["outs"]]
        out_shape = [jax.ShapeDtypeStruct(kinds[kind][1], dt) for dt, kind in tail["outs"]]
    scratch = []
    if n_r:
        in_specs += [ANY] * n_r
        args += ride["arrays"]
        out_specs += [ANY] * n_r
        out_shape += ride["out_shape"]
        scratch = ride["scratch"]
    sequential = tail is not None or n_r > 0
    res = pl.pallas_call(
        body, name=name, grid=(M // tm, N // tn), in_specs=in_specs, out_specs=out_specs, out_shape=out_shape,
        scratch_shapes=scratch,
        compiler_params=pltpu.CompilerParams(dimension_semantics=("arbitrary" if sequential else "parallel", "arbitrary"),
                                             vmem_limit_bytes=VMEM_LIMIT, has_side_effects=n_r > 0),
    )(*args)
    return res[0] if len(res) == 1 else res


def _mm_tn(a, b, *, name, tma_cap=1024, tnb_cap=1024, tk=2048, rows=None, into=None):
    T, Ma = a.shape
    Nb = b.shape[1]
    tma, tnb = _pick(Ma, tma_cap), _pick(Nb, tnb_cap)
    tk = min(tk, T)
    n_k = T // tk
    first_row, total = (0, Ma) if rows is None else rows
    assert first_row % tma == 0
    i0 = first_row // tma

    def body(a_ref, b_ref, *rest):
        o_ref, acc_ref = rest[-2:]
        k = pl.program_id(2)

        @pl.when(k == 0)
        def _():
            acc_ref[...] = jnp.zeros_like(acc_ref)

        acc_ref[...] += lax.dot_general(a_ref[...], b_ref[...], (((0,), (0,)), ((), ())),
                                        preferred_element_type=F32)

        @pl.when(k == n_k - 1)
        def _():
            o_ref[...] = acc_ref[...]

    in_specs = [pl.BlockSpec((tk, tma), lambda i, j, k: (k, i)), pl.BlockSpec((tk, tnb), lambda i, j, k: (k, j))]
    args = [a, b]
    if into is not None:
        in_specs.append(ANY)
        args.append(into)
    return pl.pallas_call(
        body, name=name, grid=(Ma // tma, Nb // tnb, n_k), in_specs=in_specs,
        out_specs=pl.BlockSpec((tma, tnb), lambda i, j, k: (i0 + i, j)),
        out_shape=jax.ShapeDtypeStruct((total, Nb), F32),
        scratch_shapes=[pltpu.VMEM((tma, tnb), F32)],
        input_output_aliases={} if into is None else {2: 0},
        compiler_params=_params(("parallel", "parallel", "arbitrary")),
    )(*args)


def _rms_fwd(x, w, *, name, tm=512, ride=None):
    T, Dm = x.shape
    n_r = 0 if ride is None else len(ride["arrays"])

    def body(x_ref, w_ref, *rest):
        h_ref, r_ref = rest[n_r:n_r + 2]
        if n_r:
            start, finish = ride["halves"](rest[:n_r], rest[n_r + 2:2 * n_r + 2], *rest[2 * n_r + 2:])
            pl.when(pl.program_id(0) == 0)(start)
        xv = x_ref[...]
        r = lax.rsqrt(jnp.mean(xv * xv, axis=-1, keepdims=True) + EPS)
        h_ref[...] = (xv * r * w_ref[...]).astype(BF16)
        r_ref[...] = r
        if n_r:
            pl.when(pl.program_id(0) == T // tm - 1)(finish)

    return pl.pallas_call(
        body, name=name, grid=(T // tm,),
        in_specs=[pl.BlockSpec((tm, Dm), lambda i: (i, 0)), pl.BlockSpec((1, Dm), lambda i: (0, 0))] + [ANY] * n_r,
        out_specs=[pl.BlockSpec((tm, Dm), lambda i: (i, 0)), pl.BlockSpec((tm, 1), lambda i: (i, 0))] + [ANY] * n_r,
        out_shape=[jax.ShapeDtypeStruct((T, Dm), BF16), jax.ShapeDtypeStruct((T, 1), F32)]
                  + (ride["out_shape"] if n_r else []),
        scratch_shapes=ride["scratch"] if n_r else [],
        compiler_params=pltpu.CompilerParams(dimension_semantics=("arbitrary" if n_r else "parallel",),
                                             vmem_limit_bytes=VMEM_LIMIT, has_side_effects=n_r > 0),
    )(x, w, *(ride["arrays"] if n_r else []))


def _tail_rms_fwd(w):
    def fn(xv, first, w_ref, x_ref, h_ref, r_ref):
        r = lax.rsqrt(jnp.mean(xv * xv, axis=-1, keepdims=True) + EPS)
        x_ref[...] = xv
        h_ref[...] = (xv * r * w_ref[...]).astype(BF16)
        r_ref[...] = r

    return dict(fn=fn, ins=[(w, "vec")], outs=[(F32, "row"), (BF16, "row"), (F32, "col")])


def _tail_rms_bwd(x, r, w, dres, *, emit_bf16):
    def fn(dhv, first, x_ref, r_ref, w_ref, dres_ref, *outs):
        dx_ref, dw_ref = outs[0], outs[-1]

        @pl.when(first)
        def _():
            dw_ref[...] = jnp.zeros_like(dw_ref)

        rv = r_ref[...]
        xh = x_ref[...] * rv
        dxh = dhv * w_ref[...]
        t = jnp.mean(dxh * xh, axis=-1, keepdims=True)
        dx = dres_ref[...] + rv * (dxh - xh * t)
        dx_ref[...] = dx
        if emit_bf16:
            outs[1][...] = dx.astype(BF16)
        dw_ref[...] += jnp.sum(dhv * xh, axis=0, keepdims=True)

    outs = [(F32, "row")] + ([(BF16, "row")] if emit_bf16 else []) + [(F32, "vec")]
    return dict(fn=fn, ins=[(x, "row"), (r, "col"), (w, "vec"), (dres, "row")], outs=outs)


def _tail_loss(target, w):
    def fn(xv, first, t_ref, w_ref, loss_ref, dx_ref, dxb_ref, dw_ref):
        @pl.when(first)
        def _():
            loss_ref[...] = jnp.zeros_like(loss_ref)
            dw_ref[...] = jnp.zeros_like(dw_ref)

        r = lax.rsqrt(jnp.mean(xv * xv, axis=-1, keepdims=True) + EPS)
        xh = xv * r
        wv = w_ref[...]
        err = xh * wv - t_ref[...]
        row_loss = jnp.mean(err * err, axis=-1, keepdims=True)
        loss_ref[...] += 0.5 * jnp.sum(row_loss, axis=0, keepdims=True)
        dy = err * (1.0 / xv.shape[-1])
        dxh = dy * wv
        t = jnp.mean(dxh * xh, axis=-1, keepdims=True)
        dx = r * (dxh - xh * t)
        dx_ref[...] = dx
        dxb_ref[...] = dx.astype(BF16)
        dw_ref[...] += jnp.sum(dy * xh, axis=0, keepdims=True)

    return dict(fn=fn, ins=[(target, "row"), (w, "vec")],
                outs=[(F32, "one"), (F32, "row"), (BF16, "row"), (F32, "vec")])


GLA_TB = 512
GLA_NC = GLA_TB // CHUNK
GLA_UNROLL = 4


def _cumsum_rows(x, row, reverse):
    n = x.shape[0]
    s = 1
    while s < n:
        if not reverse:
            x = x + jnp.where(row >= s, pltpu.roll(x, s, 0), 0.0)
        else:
            x = x + jnp.where(row < n - s, pltpu.roll(x, n - s, 0), 0.0)
        s *= 2
    return x


def _gla_gates(uq, z, lbv):
    q = uq * _sigmoid(uq)
    sg = _sigmoid(z)
    sgn = _sigmoid(-z)
    f = lbv + (1.0 - lbv) * sg
    k = (1.0 - lbv) * sgn
    return q, sg, sgn, f, k


def _gla_decays(f, row, reverse):
    b = _cumsum_rows(jnp.log(f), row, reverse)
    if not reverse:
        bref, blast = b[CHUNK // 2 - 1:CHUNK // 2, :], b[CHUNK - 1:CHUNK, :]
    else:
        bref, blast = b[CHUNK // 2:CHUNK // 2 + 1, :], b[0:1, :]
    return b, bref, blast


def _gla_fwd(U, lb, *, f_block, reverse, name, ride=None, post=None):
    T = U.shape[0]
    nb = T // GLA_TB
    n_g = 0 if ride is None else len(ride["arrays"])
    n_p = 0 if post is None else 3

    def body(uq_ref, uf_ref, ui_ref, lb_ref, *rest):
        post_in, rest = rest[:n_p], rest[n_p:]
        g_in, rest = rest[:n_g], rest[n_g:]
        o_ref, st_ref = rest[:2]
        mix_ref = rest[2] if n_p else None
        rest = rest[2 + (n_p > 0):]
        g_out, rest = rest[:n_g], rest[n_g:]
        s_ref = rest[0]
        if n_g:
            start, finish = ride["halves"](g_in, g_out, *rest[1:])
            pl.when(pl.program_id(0) == 0)(start)

        @pl.when(pl.program_id(0) == 0)
        def _():
            s_ref[...] = jnp.zeros_like(s_ref)

        row = lax.broadcasted_iota(jnp.int32, (CHUNK, HG_D), 0)
        ri = lax.broadcasted_iota(jnp.int32, (CHUNK, CHUNK), 0)
        ci = lax.broadcasted_iota(jnp.int32, (CHUNK, CHUNK), 1)
        mask = (ri <= ci) if reverse else (ri >= ci)

        def chunk(j, carry):
            c = (GLA_NC - 1 - j) if reverse else j
            rows = pl.ds(pl.multiple_of(c * CHUNK, CHUNK), CHUNK)
            for h in range(HG_HEADS):
                cols = pl.ds(h * HG_D, HG_D)
                v = ui_ref[rows, cols]
                q, _, _, f, k = _gla_gates(uq_ref[rows, cols], uf_ref[rows, cols], lb_ref[:, cols])
                b, bref, blast = _gla_decays(f, row, reverse)
                s = jnp.where(mask, _dot_nt(q * jnp.exp(b - bref), k * jnp.exp(bref - b)), 0.0)
                st = s_ref[h]
                st_ref[c, h] = st
                o = _dot(s, v) + _dot_nt(q * jnp.exp(b), st)
                if n_p:
                    other_ref, ug_ref, w_ref = post_in
                    o = o + other_ref[rows, cols]
                    r = lax.rsqrt(jnp.mean(o * o, axis=-1, keepdims=True) + EPS)
                    ug = ug_ref[rows, cols]
                    mix_ref[rows, cols] = (o * r * w_ref[...] * (ug * _sigmoid(ug))).astype(BF16)
                o_ref[rows, cols] = o
                s_ref[h] = st * jnp.exp(blast) + _dot_tn(v, k * jnp.exp(blast - b))
            return carry

        lax.fori_loop(0, GLA_NC, chunk, 0, unroll=GLA_NC)
        if n_g:
            pl.when(pl.program_id(0) == nb - 1)(finish)

    blk = (lambda i: nb - 1 - i) if reverse else (lambda i: i)
    ucol = lambda cb: pl.BlockSpec((GLA_TB, HG_W), lambda i: (blk(i), cb))
    tok = pl.BlockSpec((GLA_TB, HG_W), lambda i: (blk(i), 0))
    in_specs = [ucol(0), ucol(f_block), ucol(3), pl.BlockSpec((1, HG_W), lambda i: (0, 0))]
    args = [U, U, U, lb]
    out_specs = [tok, pl.BlockSpec((GLA_NC, HG_HEADS, HG_D, HG_D), lambda i: (blk(i), 0, 0, 0))]
    out_shape = [jax.ShapeDtypeStruct((T, HG_W), F32), jax.ShapeDtypeStruct((T // CHUNK, HG_HEADS, HG_D, HG_D), F32)]
    if n_p:
        in_specs += [tok, ucol(4), pl.BlockSpec((1, HG_D), lambda i: (0, 0))]
        args += [post[0], U, post[1]]
        out_specs.append(tok)
        out_shape.append(jax.ShapeDtypeStruct((T, HG_W), BF16))
    return pl.pallas_call(
        body, name=name, grid=(nb,), in_specs=in_specs + [ANY] * n_g, out_specs=out_specs + [ANY] * n_g,
        out_shape=out_shape + (ride["out_shape"] if n_g else []),
        scratch_shapes=[pltpu.VMEM((HG_HEADS, HG_D, HG_D), F32)] + (ride["scratch"] if n_g else []),
        compiler_params=pltpu.CompilerParams(dimension_semantics=("arbitrary",), vmem_limit_bytes=VMEM_LIMIT,
                                             has_side_effects=bool(n_g)),
    )(*args, *(ride["arrays"] if n_g else []))


def _gla_bwd(U, lb, do, states, *, f_block, reverse, name, prev=None):
    T = U.shape[0]
    nb = T // GLA_TB
    final = prev is not None

    def body(uq_ref, uf_ref, ui_ref, lb_ref, do_ref, st_ref, *rest):
        if final:
            dqp_ref, dzp_ref, dvp_ref, dug_ref, out_ref, dlb_ref, ds_ref = rest
        else:
            dq_ref, dz_ref, dv_ref, dlb_ref, ds_ref = rest

        @pl.when(pl.program_id(0) == 0)
        def _():
            ds_ref[...] = jnp.zeros_like(ds_ref)
            dlb_ref[...] = jnp.zeros_like(dlb_ref)

        row = lax.broadcasted_iota(jnp.int32, (CHUNK, HG_D), 0)
        ri = lax.broadcasted_iota(jnp.int32, (CHUNK, CHUNK), 0)
        ci = lax.broadcasted_iota(jnp.int32, (CHUNK, CHUNK), 1)
        mask = (ri <= ci) if reverse else (ri >= ci)

        def chunk(j, carry):
            c = j if reverse else (GLA_NC - 1 - j)
            rows = pl.ds(pl.multiple_of(c * CHUNK, CHUNK), CHUNK)
            for h in range(HG_HEADS):
                cols = pl.ds(h * HG_D, HG_D)
                v = ui_ref[rows, cols]
                lbv = lb_ref[:, cols]
                uq = uq_ref[rows, cols]
                q, sg, sgn, f, k = _gla_gates(uq, uf_ref[rows, cols], lbv)
                b, bref, blast = _gla_decays(f, row, reverse)
                eq, ek, eb, el, dec = (jnp.exp(b - bref), jnp.exp(bref - b), jnp.exp(b), jnp.exp(blast - b),
                                       jnp.exp(blast))
                qin, kin, qb, klast = q * eq, k * ek, q * eb, k * el
                dov = do_ref[rows, cols]
                st = st_ref[c, h]
                dst = ds_ref[h]
                p = jnp.where(mask, _dot_nt(qin, kin), 0.0)
                dp = jnp.where(mask, _dot_nt(dov, v), 0.0)
                dqin = _dot(dp, kin)
                dkin = _dot_tn(dp, qin)
                dv = _dot_tn(p, dov) + _dot_nt(klast, dst)
                dqb = _dot(dov, st)
                dklast = _dot(v, dst)
                ds_ref[h] = _dot_tn(dov, qb) + dst * dec
                db = dqin * qin - dkin * kin + dqb * qb - dklast * klast
                extra = (jnp.sum(dklast * klast, axis=0, keepdims=True)
                         + dec * jnp.sum(st * dst, axis=0, keepdims=True))
                dg = _cumsum_rows(db, row, not reverse) + extra
                dq = dqin * eq + dqb * eb
                dk = dkin * ek + dklast * el
                dfk = dg / f - dk
                dz = (dfk * (1.0 - lbv) * sg * sgn).astype(BF16)
                dlb_ref[:, cols] += jnp.sum(dfk * sgn, axis=0, keepdims=True)
                if final:
                    sq = _sigmoid(uq)
                    col = lambda blk: pl.ds(blk * HG_W + h * HG_D, HG_D)
                    out_ref[rows, col(0)] = ((dq + dqp_ref[rows, cols]) * (sq * (1.0 + uq * (1.0 - sq)))).astype(BF16)
                    out_ref[rows, col(1)] = dzp_ref[rows, cols]
                    out_ref[rows, col(2)] = dz
                    out_ref[rows, col(3)] = (dv + dvp_ref[rows, cols]).astype(BF16)
                    out_ref[rows, col(4)] = dug_ref[rows, cols]
                else:
                    dq_ref[rows, cols] = dq
                    dz_ref[rows, cols] = dz
                    dv_ref[rows, cols] = dv
            return carry

        lax.fori_loop(0, GLA_NC, chunk, 0, unroll=GLA_UNROLL)

    blk = (lambda i: i) if reverse else (lambda i: nb - 1 - i)
    ucol = lambda cb: pl.BlockSpec((GLA_TB, HG_W), lambda i: (blk(i), cb))
    tok = pl.BlockSpec((GLA_TB, HG_W), lambda i: (blk(i), 0))
    vec = pl.BlockSpec((1, HG_W), lambda i: (0, 0))
    in_specs = [ucol(0), ucol(f_block), ucol(3), vec, tok,
                pl.BlockSpec((GLA_NC, HG_HEADS, HG_D, HG_D), lambda i: (blk(i), 0, 0, 0))]
    vec_shape = jax.ShapeDtypeStruct((1, HG_W), F32)
    if final:
        in_specs += [tok] * 4
        out_specs = [pl.BlockSpec((GLA_TB, 5 * HG_W), lambda i: (blk(i), 0)), vec]
        out_shape = [jax.ShapeDtypeStruct((T, 5 * HG_W), BF16), vec_shape]
    else:
        out_specs = [tok, tok, tok, vec]
        out_shape = [jax.ShapeDtypeStruct((T, HG_W), F32), jax.ShapeDtypeStruct((T, HG_W), BF16),
                     jax.ShapeDtypeStruct((T, HG_W), F32), vec_shape]
    return pl.pallas_call(
        body, name=name, grid=(nb,), in_specs=in_specs, out_specs=out_specs, out_shape=out_shape,
        scratch_shapes=[pltpu.VMEM((HG_HEADS, HG_D, HG_D), F32)],
        compiler_params=_params(("arbitrary",)),
    )(U, U, U, lb, do, states, *(prev if final else ()))


def _hg_post_bwd(dmix, o_sum, U, w, *, name, tm=512):
    T = o_sum.shape[0]

    def body(dm_ref, o_ref, ug_ref, w_ref, do_ref, dug_ref, dw_ref):
        @pl.when(pl.program_id(0) == 0)
        def _():
            dw_ref[...] = jnp.zeros_like(dw_ref)

        wv = w_ref[...]
        for h in range(HG_HEADS):
            cols = pl.ds(h * HG_D, HG_D)
            o = o_ref[:, cols]
            r = lax.rsqrt(jnp.mean(o * o, axis=-1, keepdims=True) + EPS)
            xh = o * r
            ug = ug_ref[:, cols]
            sg = _sigmoid(ug)
            dm = dm_ref[:, cols]
            dn = dm * (ug * sg)
            dug_ref[:, cols] = (dm * (xh * wv) * (sg * (1.0 + ug * (1.0 - sg)))).astype(BF16)
            dxh = dn * wv
            t = jnp.mean(dxh * xh, axis=-1, keepdims=True)
            do_ref[:, cols] = r * (dxh - xh * t)
            dw_ref[:, cols] += jnp.sum(dn * xh, axis=0, keepdims=True)

    tok = pl.BlockSpec((tm, HG_W), lambda i: (i, 0))
    vec = pl.BlockSpec((1, HG_W), lambda i: (0, 0))
    return pl.pallas_call(
        body, name=name, grid=(T // tm,),
        in_specs=[tok, tok, pl.BlockSpec((tm, HG_W), lambda i: (i, 4)), pl.BlockSpec((1, HG_D), lambda i: (0, 0))],
        out_specs=[tok, tok, vec],
        out_shape=[jax.ShapeDtypeStruct((T, HG_W), F32), jax.ShapeDtypeStruct((T, HG_W), BF16),
                   jax.ShapeDtypeStruct((1, HG_W), F32)],
        compiler_params=_params(("arbitrary",)),
    )(dmix, o_sum, U, w)


def _rope_tables(T):
    rows = T // GRID_W
    row = np.repeat(np.arange(rows), GRID_W).astype(np.float32)
    col = np.tile(np.arange(GRID_W), rows).astype(np.float32)
    axis_dim = ATT_DH // 2
    freqs = (np.float32(ROPE_THETA) ** (-np.arange(0, axis_dim, 2, dtype=np.float32) / np.float32(axis_dim))
             ).astype(np.float32)
    ang = np.concatenate([row[:, None] * freqs, col[:, None] * freqs], axis=-1).astype(np.float32)
    cos, sin = np.cos(ang), np.sin(ang)
    c = np.repeat(cos, 2, axis=-1)
    s = np.stack([-sin, sin], axis=-1).reshape(T, ATT_DH)
    return jnp.asarray(np.tile(c, (1, 2)), F32), jnp.asarray(np.tile(s, (1, 2)), F32)


def _head_blockdiag(width):
    shift = ATT_DH.bit_length() - 1
    ri = jnp.right_shift(lax.broadcasted_iota(jnp.int32, (width, width), 0), shift)
    ci = jnp.right_shift(lax.broadcasted_iota(jnp.int32, (width, width), 1), shift)
    return jnp.where(ri == ci, 1.0, 0.0).astype(BF16)


def _head_sum(x, bd):
    hi = x.astype(BF16)
    lo = (x - hi.astype(F32)).astype(BF16)
    return jnp.dot(hi, bd, preferred_element_type=F32) + jnp.dot(lo, bd, preferred_element_type=F32)


def _pair_swap(x, even):
    n = x.shape[-1]
    return jnp.where(even, pltpu.roll(x, n - 1, 1), pltpu.roll(x, 1, 1))


FA_TQ = 512


FA_TK = 512


def _cols_from_tokens(x, kv):
    w = ATT_G * ATT_DH
    xt = x[:, kv * w:(kv + 1) * w].T
    return jnp.concatenate([xt[g * ATT_DH:(g + 1) * ATT_DH, :] for g in range(ATT_G)], axis=1)


def _tokens_from_cols(c):
    tq = c.shape[1] // ATT_G
    return jnp.concatenate([c[:, g * tq:(g + 1) * tq] for g in range(ATT_G)], axis=0).T


def _store_cols(ref, x, norm_ref=None):
    for kv in range(ATT_KV):
        cols = _cols_from_tokens(x, kv).astype(BF16)
        ref[kv, 0] = cols
        if norm_ref is not None:
            cf = cols.astype(F32)
            norm_ref[kv, 0] = jnp.sqrt(jnp.sum(cf * cf, axis=0, keepdims=True))


def _att_prep_fwd(U, cos, sin, qw, kw, *, name):
    T = U.shape[0]
    tm = min(FA_TQ, T)
    R = ATT_G * tm
    scale = ATT_DH ** -0.5

    def head_rows(ref, x):
        xt = x.astype(F32).T
        for kv in range(ATT_KV):
            ref[kv, 0] = xt[kv * ATT_DH:(kv + 1) * ATT_DH, :].astype(BF16)

    def body(aq_ref, ak_ref, av_ref, c_ref, s_ref, qw_ref, kw_ref, q_ref, qn_ref, kmax_ref, kc_ref, vc_ref):
        @pl.when(pl.program_id(0) == 0)
        def _():
            kmax_ref[...] = jnp.zeros_like(kmax_ref)

        bd = _head_blockdiag(ATT_QW)
        c2, s2 = c_ref[...], s_ref[...]
        c8, s8 = jnp.tile(c2, (1, 4)), jnp.tile(s2, (1, 4))

        def norm_rope(x, w, c, s, bdm):
            r = lax.rsqrt(_head_sum(x * x, bdm) * (1.0 / ATT_DH) + EPS)
            y = x * r * w
            even = (lax.broadcasted_iota(jnp.int32, y.shape, 1) & 1) == 0
            return y * c + _pair_swap(y, even) * s

        _store_cols(q_ref, norm_rope(aq_ref[...], qw_ref[...], c8, s8, bd) * (scale * LOG2E), qn_ref)
        kb = norm_rope(ak_ref[...], kw_ref[...], c2, s2, bd[:ATT_KW, :ATT_KW]).astype(BF16)
        kf = kb.astype(F32)
        ksq = _head_sum(kf * kf, bd[:ATT_KW, :ATT_KW])
        kmax_ref[...] = jnp.maximum(kmax_ref[...], jnp.max(ksq, axis=0, keepdims=True))
        head_rows(kc_ref, kb)
        head_rows(vc_ref, av_ref[...].astype(BF16))

    kv_spec = pl.BlockSpec((tm, ATT_KW), lambda i: (i, 0))
    tk = min(FA_TK, T)
    per = tk // tm
    c_spec = pl.BlockSpec((ATT_KV, 1, ATT_DH, tm), lambda i: (0, i // per, 0, i % per))
    c_shape = jax.ShapeDtypeStruct((ATT_KV, T // tk, ATT_DH, tk), BF16)
    return pl.pallas_call(
        body, name=name, grid=(T // tm,),
        in_specs=[pl.BlockSpec((tm, ATT_QW), lambda i: (i, 5)),
                  pl.BlockSpec((tm, ATT_KW), lambda i: (i, 24)), pl.BlockSpec((tm, ATT_KW), lambda i: (i, 25)),
                  kv_spec, kv_spec,
                  pl.BlockSpec((1, ATT_QW), lambda i: (0, 0)), pl.BlockSpec((1, ATT_KW), lambda i: (0, 0))],
        out_specs=[pl.BlockSpec((ATT_KV, 1, ATT_DH, R), lambda i: (0, i, 0, 0)),
                   pl.BlockSpec((ATT_KV, 1, 1, R), lambda i: (0, i, 0, 0)), pl.BlockSpec((1, ATT_KW), lambda i: (0, 0)),
                   c_spec, c_spec],
        out_shape=[jax.ShapeDtypeStruct((ATT_KV, T // tm, ATT_DH, R), BF16),
                   jax.ShapeDtypeStruct((ATT_KV, T // tm, 1, R), F32), jax.ShapeDtypeStruct((1, ATT_KW), F32),
                   c_shape, c_shape],
        compiler_params=_params(("arbitrary",)),
    )(U, U, U, cos, sin, qw, kw)


def _att_prep_bwd(U, dq_c, dk_c, dv_c, cos, sin, qw, kw, *, name):
    T = U.shape[0]
    tm = min(FA_TQ, T)
    R = ATT_G * tm
    scale = ATT_DH ** -0.5

    def body(aq_ref, ak_ref, dq_ref, dk_ref, dv_ref, c_ref, s_ref, qw_ref, kw_ref, out_ref, dqw_ref, dkw_ref):
        @pl.when(pl.program_id(0) == 0)
        def _():
            dqw_ref[...] = jnp.zeros_like(dqw_ref)
            dkw_ref[...] = jnp.zeros_like(dkw_ref)

        bd = _head_blockdiag(ATT_QW)
        c2, s2 = c_ref[...], s_ref[...]
        c8, s8 = jnp.tile(c2, (1, 4)), jnp.tile(s2, (1, 4))

        def bwd(x, dy, w, c, s, bdm):
            even = (lax.broadcasted_iota(jnp.int32, x.shape, 1) & 1) == 0
            dn = dy * c - _pair_swap(dy, even) * s
            r = lax.rsqrt(_head_sum(x * x, bdm) * (1.0 / ATT_DH) + EPS)
            xh = x * r
            dxh = dn * w
            t = _head_sum(dxh * xh, bdm) * (1.0 / ATT_DH)
            return r * (dxh - xh * t), jnp.sum(dn * xh, axis=0, keepdims=True)

        dq = jnp.concatenate([_tokens_from_cols(dq_ref[kv, 0]) for kv in range(ATT_KV)], axis=1)
        da, dw = bwd(aq_ref[...], dq * scale, qw_ref[...], c8, s8, bd)
        out_ref[:, 0:ATT_QW] = da.astype(BF16)
        dqw_ref[...] += dw
        tokens = lambda ref: jnp.concatenate([ref[kv, 0] for kv in range(ATT_KV)], axis=0).T
        da, dw = bwd(ak_ref[...], tokens(dk_ref) * (1.0 / LOG2E), kw_ref[...], c2, s2, bd[:ATT_KW, :ATT_KW])
        out_ref[:, ATT_QW:ATT_QW + ATT_KW] = da.astype(BF16)
        dkw_ref[...] += dw
        out_ref[:, ATT_QW + ATT_KW:ATT_QW + 2 * ATT_KW] = tokens(dv_ref).astype(BF16)

    kv_spec = pl.BlockSpec((tm, ATT_KW), lambda i: (i, 0))
    qv = pl.BlockSpec((1, ATT_QW), lambda i: (0, 0))
    kv = pl.BlockSpec((1, ATT_KW), lambda i: (0, 0))
    w_att = ATT_QW + 2 * ATT_KW
    per = dk_c.shape[3] // tm
    c_spec = pl.BlockSpec((ATT_KV, 1, ATT_DH, tm), lambda i: (0, i // per, 0, i % per))
    return pl.pallas_call(
        body, name=name, grid=(T // tm,),
        in_specs=[pl.BlockSpec((tm, ATT_QW), lambda i: (i, 5)), pl.BlockSpec((tm, ATT_KW), lambda i: (i, 24)),
                  pl.BlockSpec((ATT_KV, 1, ATT_DH, R), lambda i: (0, i, 0, 0)), c_spec, c_spec, kv_spec, kv_spec, qv, kv],
        out_specs=[pl.BlockSpec((tm, w_att), lambda i: (i, 0)), qv, kv],
        out_shape=[jax.ShapeDtypeStruct((T, w_att), BF16),
                   jax.ShapeDtypeStruct((1, ATT_QW), F32), jax.ShapeDtypeStruct((1, ATT_KW), F32)],
        compiler_params=_params(("arbitrary",)),
    )(U, U, dq_c, dk_c, dv_c, cos, sin, qw, kw)


def _scores(k_ref, j, qv):
    return lax.dot_general(k_ref[0, j], qv, (((0,), (0,)), ((), ())), preferred_element_type=F32)


def _flash_fwd(q_c, k_c, v_c, *, name):
    _, nq, _, R = q_c.shape
    _, n_k, _, tk = v_c.shape

    def body(q_ref, k_ref, v_ref, o_ref, lse_ref, acc_ref):
        qv = q_ref[0, 0]
        acc_ref[...] = jnp.zeros_like(acc_ref)

        def step(j, carry):
            m, l = carry
            s = _scores(k_ref, j, qv)
            m_new = jnp.maximum(m, jnp.max(s, axis=0, keepdims=True))
            alpha = jnp.exp2(m - m_new)
            p = jnp.exp2(s - m_new)
            l = alpha * l + jnp.sum(p, axis=0, keepdims=True)
            acc_ref[...] = alpha * acc_ref[...] + jnp.dot(v_ref[0, j], p.astype(BF16), preferred_element_type=F32)
            return m_new, l

        m, l = lax.fori_loop(0, n_k, step, (jnp.full((1, R), -jnp.inf, F32), jnp.zeros((1, R), F32)))
        o_ref[0, 0] = acc_ref[...] / l
        lse_ref[0, 0] = m + jnp.log2(l)

    cspec = pl.BlockSpec((1, 1, ATT_DH, R), lambda h, i: (h, i, 0, 0))
    kspec = pl.BlockSpec((1, n_k, ATT_DH, tk), lambda h, i: (h, 0, 0, 0))
    return pl.pallas_call(
        body, name=name, grid=(ATT_KV, nq),
        in_specs=[cspec, kspec, kspec],
        out_specs=[cspec, pl.BlockSpec((1, 1, 1, R), lambda h, i: (h, i, 0, 0))],
        out_shape=[jax.ShapeDtypeStruct((ATT_KV, nq, ATT_DH, R), F32), jax.ShapeDtypeStruct((ATT_KV, nq, 1, R), F32)],
        scratch_shapes=[pltpu.VMEM((ATT_DH, R), F32)],
        compiler_params=_params(("parallel", "parallel")),
    )(q_c, k_c, v_c)


FA_BOUND_MAX = 40.0 * LOG2E


def _flash_fwd_bounded(q_c, k_c, v_c, m_c, *, name):
    _, nq, _, R = q_c.shape
    _, n_k, _, tk = v_c.shape

    def body(q_ref, k_ref, v_ref, m_ref, o_ref, lse_ref, acc_ref):
        qv = q_ref[0, 0]
        m = m_ref[0, 0]
        acc_ref[...] = jnp.zeros_like(acc_ref)

        per = math.gcd(n_k, 8)

        def step(jj, l8):
            pv = None
            for u in range(per):
                j = per * jj + u
                p = jnp.exp2(_scores(k_ref, j, qv) - m)
                l8 = l8 + jnp.sum(p.reshape(tk // 8, 8, R), axis=0)
                d = jnp.dot(v_ref[0, j], p.astype(BF16), preferred_element_type=F32)
                pv = d if pv is None else pv + d
            acc_ref[...] += pv
            return l8

        l8 = lax.fori_loop(0, n_k // per, step, jnp.zeros((8, R), F32))
        l = jnp.sum(l8, axis=0, keepdims=True)
        o_ref[0, 0] = acc_ref[...] / l
        lse_ref[0, 0] = m + jnp.log2(l)

    cspec = pl.BlockSpec((1, 1, ATT_DH, R), lambda h, i: (h, i, 0, 0))
    kspec = pl.BlockSpec((1, n_k, ATT_DH, tk), lambda h, i: (h, 0, 0, 0))
    vspec = pl.BlockSpec((1, 1, 1, R), lambda h, i: (h, i, 0, 0))
    return pl.pallas_call(
        body, name=name, grid=(ATT_KV, nq),
        in_specs=[cspec, kspec, kspec, vspec],
        out_specs=[cspec, vspec],
        out_shape=[jax.ShapeDtypeStruct((ATT_KV, nq, ATT_DH, R), F32), jax.ShapeDtypeStruct((ATT_KV, nq, 1, R), F32)],
        scratch_shapes=[pltpu.VMEM((ATT_DH, R), F32)],
        compiler_params=_params(("parallel", "parallel")),
    )(q_c, k_c, v_c, m_c)


CHIP_MASKS = [(1, 0, 0), (0, 1, 0), (1, 1, 0)]


def _chip_slot(p):
    return 2 * p[0] + p[1]


def _flash_bwd(q_c, k_c, v_c, do_c, lse, delta, *, name, ride=None):
    _, nq, _, R = q_c.shape
    _, n_k, _, tk = k_c.shape
    n_ride = 0 if ride is None else len(ride["arrays"])

    def body(qc_ref, kc_ref, vc_ref, doc_ref, lse_ref, delta_ref, *rest):
        ride_in, rest = rest[:n_ride], rest[n_ride:]
        dq_ref, dk_ref, dv_ref = rest[:3]
        ride_out, rest = rest[3:3 + n_ride], rest[3 + n_ride:]
        acc_ref = rest[0]
        kv = pl.program_id(0)
        if n_ride:
            start, finish = ride["halves"](ride_in, ride_out, *rest[1:])
            pl.when((kv == 0) & (pl.program_id(1) == 0))(start)

        @pl.when(pl.program_id(1) == 0)
        def _():
            dk_ref[...] = jnp.zeros_like(dk_ref)
            dv_ref[...] = jnp.zeros_like(dv_ref)

        qc, doc = qc_ref[0, 0], doc_ref[0, 0]
        lsev, delta = lse_ref[0, 0], delta_ref[0, 0]
        acc_ref[...] = jnp.zeros_like(acc_ref)
        nt = (((1,), (1,)), ((), ()))

        def step(j, carry):
            p = jnp.exp2(_scores(kc_ref, j, qc) - lsev)
            dp = _scores(vc_ref, j, doc)
            ds = (p * (dp - delta)).astype(BF16)
            acc_ref[...] += jnp.dot(kc_ref[0, j], ds, preferred_element_type=F32)
            dk_ref[0, j] += lax.dot_general(qc, ds, nt, preferred_element_type=F32)
            dv_ref[0, j] += lax.dot_general(doc, p.astype(BF16), nt, preferred_element_type=F32)
            return carry

        lax.fori_loop(0, n_k, step, 0, unroll=4)
        dq_ref[0, 0] = acc_ref[...]

        if n_ride:
            pl.when((kv == ATT_KV - 1) & (pl.program_id(1) == nq - 1))(finish)

    cspec = pl.BlockSpec((1, 1, ATT_DH, R), lambda h, i: (h, i, 0, 0))
    vspec = pl.BlockSpec((1, 1, 1, R), lambda h, i: (h, i, 0, 0))
    kspec = pl.BlockSpec((1, n_k, ATT_DH, tk), lambda h, i: (h, 0, 0, 0))
    k_shape = jax.ShapeDtypeStruct(k_c.shape, F32)
    return pl.pallas_call(
        body, name=name, grid=(ATT_KV, nq),
        in_specs=[cspec, kspec, kspec, cspec, vspec, vspec] + [ANY] * n_ride,
        out_specs=[cspec, kspec, kspec] + [ANY] * n_ride,
        out_shape=[jax.ShapeDtypeStruct((ATT_KV, nq, ATT_DH, R), F32), k_shape, k_shape]
                  + (ride["out_shape"] if n_ride else []),
        scratch_shapes=[pltpu.VMEM((ATT_DH, R), F32)] + (ride["scratch"] if n_ride else []),
        compiler_params=pltpu.CompilerParams(dimension_semantics=("arbitrary", "arbitrary"),
                                             vmem_limit_bytes=VMEM_LIMIT, has_side_effects=bool(n_ride)),
    )(q_c, k_c, v_c, do_c, lse, delta, *(ride["arrays"] if n_ride else []))


def _att_post_fwd(o_c, w, *, name):
    _, nq, _, R = o_c.shape
    tm = R // ATT_G
    T = nq * tm

    def body(oc_ref, w_ref, o_ref, out_ref):
        ov = jnp.concatenate([_tokens_from_cols(oc_ref[kv, 0]) for kv in range(ATT_KV)], axis=1)
        r = lax.rsqrt(jnp.mean(ov * ov, axis=-1, keepdims=True) + EPS)
        o_ref[...] = ov
        out_ref[...] = (ov * r * w_ref[...]).astype(BF16)

    tok = pl.BlockSpec((tm, ATT_QW), lambda i: (i, 0))
    return pl.pallas_call(
        body, name=name, grid=(nq,),
        in_specs=[pl.BlockSpec((ATT_KV, 1, ATT_DH, R), lambda i: (0, i, 0, 0)), pl.BlockSpec((1, ATT_QW), lambda i: (0, 0))],
        out_specs=[tok, tok],
        out_shape=[jax.ShapeDtypeStruct((T, ATT_QW), F32), jax.ShapeDtypeStruct((T, ATT_QW), BF16)],
        compiler_params=_params(("parallel",)),
    )(o_c, w)


def _att_post_bwd(dmix, o, w, *, name):
    T = o.shape[0]
    tm = min(FA_TQ, T)
    R = ATT_G * tm

    def body(dm_ref, o_ref, w_ref, do_ref, delta_ref, dw_ref):
        @pl.when(pl.program_id(0) == 0)
        def _():
            dw_ref[...] = jnp.zeros_like(dw_ref)

        ov = o_ref[...]
        r = lax.rsqrt(jnp.mean(ov * ov, axis=-1, keepdims=True) + EPS)
        xh = ov * r
        dm = dm_ref[...]
        dxh = dm * w_ref[...]
        t = jnp.mean(dxh * xh, axis=-1, keepdims=True)
        do = r * (dxh - xh * t)
        _store_cols(do_ref, do)
        dob = do.astype(BF16).astype(F32)
        for kv in range(ATT_KV):
            delta_ref[kv, 0] = jnp.sum(_cols_from_tokens(dob * ov, kv), axis=0, keepdims=True)
        dw_ref[...] += jnp.sum(dm * xh, axis=0, keepdims=True)

    tok = pl.BlockSpec((tm, ATT_QW), lambda i: (i, 0))
    vec = pl.BlockSpec((1, ATT_QW), lambda i: (0, 0))
    return pl.pallas_call(
        body, name=name, grid=(T // tm,),
        in_specs=[pl.BlockSpec((tm, ATT_QW), lambda i: (i, 1)), tok, vec],
        out_specs=[pl.BlockSpec((ATT_KV, 1, ATT_DH, R), lambda i: (0, i, 0, 0)),
                   pl.BlockSpec((ATT_KV, 1, 1, R), lambda i: (0, i, 0, 0)), vec],
        out_shape=[jax.ShapeDtypeStruct((ATT_KV, T // tm, ATT_DH, R), BF16),
                   jax.ShapeDtypeStruct((ATT_KV, T // tm, 1, R), F32), jax.ShapeDtypeStruct((1, ATT_QW), F32)],
        compiler_params=_params(("arbitrary",)),
    )(dmix, o, w)


def _ffn_up(h2, wg_t, wu_t, *, name, tm=512):
    T = h2.shape[0]
    tn = _pick(D_FF, 1408)
    nt = (((1,), (1,)), ((), ()))

    def body(h_ref, wg_ref, wu_ref, g_ref, u_ref, a_ref):
        hv = h_ref[...]
        g = lax.dot_general(hv, wg_ref[...], nt, preferred_element_type=F32)
        u = lax.dot_general(hv, wu_ref[...], nt, preferred_element_type=F32)
        g_ref[...] = g.astype(BF16)
        u_ref[...] = u.astype(BF16)
        a_ref[...] = (g * _sigmoid(g) * u).astype(BF16)

    wspec = pl.BlockSpec((tn, D_MODEL), lambda i, j: (j, 0))
    ospec = pl.BlockSpec((tm, tn), lambda i, j: (i, j))
    return pl.pallas_call(
        body, name=name, grid=(T // tm, D_FF // tn),
        in_specs=[pl.BlockSpec((tm, D_MODEL), lambda i, j: (i, 0)), wspec, wspec],
        out_specs=[ospec] * 3, out_shape=[jax.ShapeDtypeStruct((T, D_FF), BF16)] * 3,
        compiler_params=_params(("parallel", "arbitrary")),
    )(h2, wg_t, wu_t)


def _ffn_act_bwd(dx2b, w_down, gate, up, *, name, tm=512):
    T = dx2b.shape[0]
    tn = _pick(D_FF, 1408)

    def body(dx_ref, w_ref, g_ref, u_ref, dg_ref, du_ref):
        da = lax.dot_general(dx_ref[...], w_ref[...], (((1,), (1,)), ((), ())), preferred_element_type=F32)
        g = g_ref[...].astype(F32)
        u = u_ref[...].astype(F32)
        sg = _sigmoid(g)
        dg_ref[...] = (da * u * (sg * (1.0 + g * (1.0 - sg)))).astype(BF16)
        du_ref[...] = (da * (g * sg)).astype(BF16)

    ospec = pl.BlockSpec((tm, tn), lambda i, j: (i, j))
    return pl.pallas_call(
        body, name=name, grid=(T // tm, D_FF // tn),
        in_specs=[pl.BlockSpec((tm, D_MODEL), lambda i, j: (i, 0)),
                  pl.BlockSpec((tn, D_MODEL), lambda i, j: (j, 0)), ospec, ospec],
        out_specs=[ospec] * 2, out_shape=[jax.ShapeDtypeStruct((T, D_FF), BF16)] * 2,
        compiler_params=_params(("parallel", "arbitrary")),
    )(dx2b, w_down, gate, up)


def _adam_math(w, g, m, v):
    m = ADAM_B1 * m + (1.0 - ADAM_B1) * g
    v = ADAM_B2 * v + (1.0 - ADAM_B2) * (g * g)
    m_hat = m / (1.0 - ADAM_B1 ** ADAM_STEP)
    v_hat = v / (1.0 - ADAM_B2 ** ADAM_STEP)
    delta = -ADAM_LR * (m_hat / (jnp.sqrt(v_hat) + ADAM_EPS) + ADAM_WD * w)
    return delta, m, v


def _adamw(parts, w, m, v, *, name, tr_cap=256):
    P, R, C = parts.shape
    tr = R
    for t in range(8, min(R, tr_cap) + 1, 8):
        if R % t == 0:
            tr = t

    def body(p_ref, w_ref, m_ref, v_ref, g_ref, d_ref, nm_ref, nv_ref):
        g = p_ref[0].astype(F32)
        for j in range(1, P):
            g = g + p_ref[j].astype(F32)
        d, nm, nv = _adam_math(w_ref[...], g, m_ref[...], v_ref[...])
        g_ref[...] = g
        d_ref[...] = d
        nm_ref[...] = nm
        nv_ref[...] = nv

    blk = pl.BlockSpec((tr, C), lambda i: (i, 0))
    return pl.pallas_call(
        body, name=name, grid=(R // tr,),
        in_specs=[pl.BlockSpec((P, tr, C), lambda i: (0, i, 0)), blk, blk, blk],
        out_specs=[blk] * 4, out_shape=[jax.ShapeDtypeStruct((R, C), F32)] * 4,
        compiler_params=_params(("parallel",)),
    )(parts, w, m, v)


def _gather_halves(ins, outs, send_sems, recv_sems, local_sems):
    n = len(ins)
    x, y, c = lax.axis_index("x"), lax.axis_index("y"), lax.axis_index("c")
    me, sibling = (x, y, c), (x, y, 1 - c)
    chips = [(1 - x, y), (x, 1 - y), (1 - x, 1 - y)]

    def slot(p):
        return 4 * p[0] + 2 * p[1] + p[2]

    def copy(a, k, block, to, src=None):
        dst = outs[a].at[slot(block)]
        return pltpu.make_async_remote_copy(
            src_ref=dst if src is None else src, dst_ref=dst,
            send_sem=send_sems.at[a * 7 + k], recv_sem=recv_sems.at[a * 7 + k],
            device_id=to, device_id_type=MESH)

    mine = [pltpu.make_async_copy(ins[a], outs[a].at[slot(me)], local_sems.at[a]) for a in range(n)]
    first = []
    for a in range(n):
        first.append(copy(a, 0, me, sibling, src=ins[a]))
        first += [copy(a, 1 + j, me, (*chip, c), src=ins[a]) for j, chip in enumerate(chips)]

    def start():
        for cp in mine + first:
            cp.start()

    def finish():
        passed = []
        for j, chip in enumerate(chips):
            for a in range(n):
                copy(a, 1 + j, (*chip, c), me).wait_recv()
                cp = copy(a, 4 + j, (*chip, c), sibling)
                cp.start()
                passed.append(cp)
        for a in range(n):
            copy(a, 0, sibling, me).wait_recv()
            for j, chip in enumerate(chips):
                copy(a, 4 + j, (*chip, 1 - c), me).wait_recv()
        for cp in first + passed:
            cp.wait_send()
        for cp in mine:
            cp.wait()

    return start, finish


def _gather_scratch(n):
    return [pltpu.SemaphoreType.DMA((7 * n,)), pltpu.SemaphoreType.DMA((7 * n,)), pltpu.SemaphoreType.DMA((n,))]


def _gathered_shapes(xs):
    return [jax.ShapeDtypeStruct((N_DEV,) + x.shape, x.dtype) for x in xs]


def _ride_gather(xs):
    xs = list(xs)
    return dict(arrays=xs, out_shape=_gathered_shapes(xs), scratch=_gather_scratch(len(xs)), halves=_gather_halves)


def _ride_chips(gs):
    gs = list(gs)
    n = len(gs)

    def halves(ins, outs, send_sems, recv_sems, local_sems):
        mine, copies = _exchange_copies(ins, outs, send_sems, recv_sems, local_sems, masks=CHIP_MASKS, slot=_chip_slot)

        def start():
            for cp in mine:
                cp.start()
            for send, _ in copies:
                send.start()

        def finish():
            for send, recv in copies:
                recv.wait_recv()
                send.wait_send()
            for cp in mine:
                cp.wait()

        return start, finish

    n_sem = len(CHIP_MASKS) * n
    return dict(arrays=gs, out_shape=[jax.ShapeDtypeStruct(g.shape, g.dtype) for g in gs], halves=halves,
                scratch=[pltpu.SemaphoreType.DMA((n_sem,)), pltpu.SemaphoreType.DMA((n_sem,)),
                         pltpu.SemaphoreType.DMA((n,))])


ALL_MASKS = [(mx, my, mc) for mx in (0, 1) for my in (0, 1) for mc in (0, 1)][1:]


def _flip(v, bit):
    return 1 - v if bit else v


def _exchange_copies(ins, outs, send_sems, recv_sems, local_sems, *, masks, slot):
    n, n_peers = len(ins), len(masks)
    x, y, c = lax.axis_index("x"), lax.axis_index("y"), lax.axis_index("c")
    my_slot = slot((x, y, c))
    mine = [pltpu.make_async_copy(ins[a].at[my_slot], outs[a].at[my_slot], local_sems.at[a]) for a in range(n)]
    copies = []
    for a in range(n):
        for k, (mx, my, mc) in enumerate(masks):
            peer = (_flip(x, mx), _flip(y, my), _flip(c, mc))
            peer_slot = slot(peer)
            sems = dict(send_sem=send_sems.at[a * n_peers + k], recv_sem=recv_sems.at[a * n_peers + k],
                        device_id=peer, device_id_type=MESH)
            copies.append((
                pltpu.make_async_remote_copy(src_ref=ins[a].at[peer_slot], dst_ref=outs[a].at[my_slot], **sems),
                pltpu.make_async_remote_copy(src_ref=ins[a].at[peer_slot], dst_ref=outs[a].at[peer_slot], **sems)))
    return mine, copies


def _send_to_all(v, *, name):
    def body(v_ref, out_ref, send_sems, recv_sems, local_sem):
        x, y, c = lax.axis_index("x"), lax.axis_index("y"), lax.axis_index("c")
        me = 4 * x + 2 * y + c
        mine = pltpu.make_async_copy(v_ref, out_ref.at[me], local_sem)
        mine.start()
        copies = []
        for k, (mx, my, mc) in enumerate(ALL_MASKS):
            peer = (_flip(x, mx), _flip(y, my), _flip(c, mc))
            peer_id = 4 * peer[0] + 2 * peer[1] + peer[2]
            sems = dict(send_sem=send_sems.at[k], recv_sem=recv_sems.at[k], device_id=peer, device_id_type=MESH)
            copies.append((pltpu.make_async_remote_copy(src_ref=v_ref, dst_ref=out_ref.at[me], **sems),
                           pltpu.make_async_remote_copy(src_ref=v_ref, dst_ref=out_ref.at[peer_id], **sems)))
        for send, _ in copies:
            send.start()
        for send, recv in copies:
            recv.wait_recv()
            send.wait_send()
        mine.wait()

    n_peers = len(ALL_MASKS)
    return pl.pallas_call(
        body, name=name, in_specs=[ANY], out_specs=ANY,
        out_shape=jax.ShapeDtypeStruct((N_DEV,) + v.shape, v.dtype),
        scratch_shapes=[pltpu.SemaphoreType.DMA((n_peers,)), pltpu.SemaphoreType.DMA((n_peers,)),
                        pltpu.SemaphoreType.DMA],
        compiler_params=pltpu.CompilerParams(has_side_effects=True),
    )(v)


SWAP_ROW_CHUNKS = 4


def _ride_swap(gs):
    gs = list(gs)
    n = len(gs)

    def halves(ins, outs, send_sems, recv_sems):
        x, y, c = lax.axis_index("x"), lax.axis_index("y"), lax.axis_index("c")
        sibling = dict(device_id=(x, y, 1 - c), device_id_type=MESH)

        def start():
            for a in range(n):
                Q, _, R, _ = ins[a].shape
                rows = R // SWAP_ROW_CHUNKS
                for q in range(Q):
                    for j in range(SWAP_ROW_CHUNKS):
                        part = pl.ds(j * rows, rows)
                        pltpu.make_async_remote_copy(src_ref=ins[a].at[q, 1 - c, part], dst_ref=outs[a].at[q, part],
                                                     send_sem=send_sems.at[a], recv_sem=recv_sems.at[a], **sibling).start()

        def finish():
            for a in range(n):
                pltpu.make_async_remote_copy(src_ref=outs[a], dst_ref=outs[a], send_sem=send_sems.at[a],
                                             recv_sem=recv_sems.at[a], **sibling).wait()

        return start, finish

    return dict(arrays=gs, out_shape=[jax.ShapeDtypeStruct(g.shape[:1] + g.shape[2:], g.dtype) for g in gs],
                scratch=[pltpu.SemaphoreType.DMA((n,)), pltpu.SemaphoreType.DMA((n,))], halves=halves)


def _core_swap(gs, *, name):
    ride = _ride_swap(gs)
    n = len(gs)

    def body(*refs):
        start, finish = ride["halves"](refs[:n], refs[n:2 * n], *refs[2 * n:])
        start()
        finish()

    return pl.pallas_call(
        body, name=name, in_specs=[ANY] * n, out_specs=[ANY] * n, out_shape=ride["out_shape"],
        scratch_shapes=ride["scratch"], compiler_params=pltpu.CompilerParams(has_side_effects=True),
    )(*gs)


def _pair_sum(g, other, core, *, name, tr_cap=256):
    Q, _, R, C = g.shape
    tr = max(t for t in range(16, min(R, tr_cap) + 1, 16) if R % t == 0)

    def body(core_ref, g_ref, o_ref, out_ref):
        out_ref[0] = (g_ref[0, 0] + o_ref[0]).astype(BF16)

    return pl.pallas_call(
        body, name=name,
        grid_spec=pltpu.PrefetchScalarGridSpec(
            num_scalar_prefetch=1, grid=(Q, R // tr),
            in_specs=[pl.BlockSpec((1, 1, tr, C), lambda q, i, core_ref: (q, core_ref[0], i, 0)),
                      pl.BlockSpec((1, tr, C), lambda q, i, core_ref: (q, i, 0))],
            out_specs=pl.BlockSpec((1, tr, C), lambda q, i, core_ref: (q, i, 0))),
        out_shape=jax.ShapeDtypeStruct((Q, R, C), BF16),
        compiler_params=_params(("parallel", "parallel")),
    )(core, g, other)


def _pack_small(norm1, norm2, final, att, hg, qn, kn, lb=None, loss=None):
    z = lambda n: jnp.zeros((n,), F32)
    rows = [norm1.reshape(-1), norm2.reshape(-1), final.reshape(-1),
            jnp.concatenate([att.reshape(-1), z(512)]),
            jnp.concatenate([hg.reshape(-1), qn.reshape(-1), kn.reshape(-1), z(1024 - 256)]),
            z(1024) if lb is None else lb.reshape(-1),
            z(1024) if loss is None else jnp.concatenate([loss.reshape(-1), z(1023)]), z(1024)]
    return jnp.stack(rows, axis=0)


def _unpack_small(p):
    return (p[0:1, :], p[1:2, :], p[2, :], p[3:4, 0:512], p[4:5, 0:128], p[4:5, 128:192], p[4:5, 192:256])


def _fold_heads(dhg, dqn, dkn, *, name):
    def body(hg_ref, q_ref, k_ref, ohg_ref, oq_ref, ok_ref):
        def fold128(v):
            acc = v[:, 0:LANES]
            for j in range(1, v.shape[1] // LANES):
                acc = acc + v[:, j * LANES:(j + 1) * LANES]
            return acc

        ohg_ref[...] = fold128(hg_ref[...])
        q = fold128(q_ref[...])
        oq_ref[...] = q + pltpu.roll(q, ATT_DH, 1)
        k = k_ref[...]
        ok_ref[...] = k + pltpu.roll(k, ATT_DH, 1)

    return pl.pallas_call(body, name=name, out_shape=[jax.ShapeDtypeStruct((1, LANES), F32)] * 3)(dhg, dqn, dkn)


def _lb_grad(dlb_sum, lb, *, name):
    def body(d_ref, lb_ref, o_ref):
        lbv = lb_ref[...]
        gl = d_ref[...] * lbv * (1.0 - lbv)
        o_ref[0:1, :] = gl[0:1, :]
        o_ref[1:2, :] = -gl[0:1, :]
        o_ref[2:3, :] = gl[1:2, :]
        o_ref[3:4, :] = -gl[1:2, :]

    return pl.pallas_call(body, name=name, out_shape=jax.ShapeDtypeStruct((4, HG_W), F32))(dlb_sum, lb)


def _lower_bounds(lb_logits_full, *, name):
    def body(l_ref, o_ref):
        for d in range(2):
            l0, l1 = l_ref[2 * d:2 * d + 1, :], l_ref[2 * d + 1:2 * d + 2, :]
            mx = jnp.maximum(l0, l1)
            e0, e1 = jnp.exp(l0 - mx), jnp.exp(l1 - mx)
            o_ref[d:d + 1, :] = e0 / (e0 + e1)

    return pl.pallas_call(body, name=name, out_shape=jax.ShapeDtypeStruct((2, HG_W), F32))(
        lb_logits_full.reshape(4, HG_W))


def _local_step(x, target, norm1_w, w_in_t, lb, hg_norm_w, q_norm_w, k_norm_w, att_norm_w, w_out, norm2_w,
                w_g_t, w_u_t, w_down, final_norm_w, reduce_early=None, reduce_late=None, shards=None):
    T = x.shape[0]
    cos, sin = _rope_tables(T)
    qw8 = jnp.tile(q_norm_w, (1, ATT_HEADS))
    kw2 = jnp.tile(k_norm_w, (1, ATT_KV))

    if shards is None:
        h, r1 = _rms_fwd(x, norm1_w, name="norm1_fwd")
        U = _mm_nn([(h, w_in_t)], trans_b=True, name="in_proj")
        o_f, st_f = _gla_fwd(U, lb[0:1], f_block=1, reverse=False, name="gla_fwd_f")
    else:
        h, r1, g_in, g_lb = _rms_fwd(x, norm1_w, ride=_ride_gather([shards["w_in_t"], shards["lb_logits"]]),
                                     name="norm1_fwd")
        w_in_t = g_in.reshape(-1, D_MODEL)
        lb = _lower_bounds(g_lb.transpose(1, 0, 2).reshape(2, 2, -1), name="lower_bounds")
        U, g_gu = _mm_nn([(h, w_in_t)], trans_b=True, ride=_ride_gather([shards["w_gu_t"]]), name="in_proj")
        o_f, st_f, g_out, g_dn = _gla_fwd(U, lb[0:1], f_block=1, reverse=False,
                                          ride=_ride_gather([shards["w_out"], shards["w_down"]]), name="gla_fwd_f")
        g_gu = g_gu.reshape(2, -1, D_MODEL)
        w_g_t, w_u_t = g_gu[0], g_gu[1]
        w_out, w_down = g_out.reshape(-1, D_MODEL), g_dn.reshape(-1, D_MODEL)
    o_sum, st_b, mix_hg = _gla_fwd(U, lb[1:2], f_block=2, reverse=True, post=(o_f, hg_norm_w), name="gla_fwd_b")
    q_c, qn_c, kmax2, k_c, v_c = _att_prep_fwd(U, cos, sin, qw8, kw2, name="att_prep_fwd")
    kmax = jnp.sqrt(jnp.max(kmax2.reshape(ATT_KV, ATT_DH), axis=1))
    m_c = qn_c * (kmax * 1.001).reshape(ATT_KV, 1, 1, 1)
    o_c, lse = lax.cond(jnp.max(m_c) <= FA_BOUND_MAX,
                        lambda: _flash_fwd_bounded(q_c, k_c, v_c, m_c, name="flash_fwd_bounded"),
                        lambda: _flash_fwd(q_c, k_c, v_c, name="flash_fwd"))
    o_att, mix_att = _att_post_fwd(o_c, att_norm_w, name="att_post_fwd")
    x1, h2, r2 = _mm_nn([(mix_hg, w_out[:HG_W]), (mix_att, w_out[HG_W:])], residual=x, tail=_tail_rms_fwd(norm2_w),
                        name="out_proj")
    gate, up, act = _ffn_up(h2, w_g_t, w_u_t, name="ffn_up")
    loss, dx2, dx2b, d_final = _mm_nn([(act, w_down)], residual=x1,
                                      tail=_tail_loss(target, final_norm_w.reshape(1, D_MODEL)), name="ffn_down")

    d_gate, d_up = _ffn_act_bwd(dx2b, w_down, gate, up, name="ffn_act_bwd")
    dw_down = _mm_tn(act, dx2b, tma_cap=1408, name="dw_down")
    dw_g = _mm_tn(d_gate, h2, tma_cap=1408, name="dw_gate")
    dw_u = _mm_tn(d_up, h2, tma_cap=1408, name="dw_up")
    mine = None if reduce_early is None else reduce_early["slabs"](dw_g, dw_u, dw_down)
    dx1, dx1b, d_norm2, *theirs = _mm_nn([(d_gate, w_g_t), (d_up, w_u_t)], tm=256,
                                         ride=None if mine is None else _ride_swap(mine),
                                         tail=_tail_rms_bwd(x1, r2, norm2_w, dx2, emit_bf16=True), name="ffn_up_bwd")
    dmix = _mm_nn([(dx1b, w_out)], trans_b=True, name="out_proj_bwd")
    dw_out = _mm_tn(mix_att, dx1b, rows=(HG_W, D_MODEL), name="dw_out_att",
                    into=_mm_tn(mix_hg, dx1b, rows=(0, D_MODEL), name="dw_out_hg"))
    do_c, delta, d_att = _att_post_bwd(dmix, o_att, att_norm_w, name="att_post_bwd")
    ride = None if reduce_early is None else _ride_chips(reduce_early["sums"](mine, theirs, dw_out))
    dq_c, dk_c, dv_c, *rode = _flash_bwd(q_c, k_c, v_c, do_c, lse, delta, ride=ride, name="flash_bwd")
    dU_att, d_qn, d_kn = _att_prep_bwd(U, dq_c, dk_c, dv_c, cos, sin, qw8, kw2, name="att_prep_bwd")
    do_hg, du_g, d_hg = _hg_post_bwd(dmix, o_sum, U, hg_norm_w, name="hg_post_bwd")
    dq_f, dz_f, dv_f, dlb_f = _gla_bwd(U, lb[0:1], do_hg, st_f, f_block=1, reverse=False, name="gla_bwd_f")
    dU_hg, dlb_b = _gla_bwd(U, lb[1:2], do_hg, st_b, f_block=2, reverse=True, prev=(dq_f, dz_f, dv_f, du_g),
                            name="gla_bwd_b")
    w_hg = 5 * HG_W
    n_in = w_hg + dU_att.shape[1]
    dw_in = _mm_tn(dU_att, h, tma_cap=256, rows=(w_hg, n_in), name="dw_in_att",
                   into=_mm_tn(dU_hg, h, tma_cap=1280, rows=(0, n_in), name="dw_in_hg"))
    late = None if reduce_late is None else _ride_chips(reduce_late(dw_in))
    grad_x, d_norm1, *rode_late = _mm_nn([(dU_hg, w_in_t[:w_hg]), (dU_att, w_in_t[w_hg:])], ride=late,
                                         tail=_tail_rms_bwd(x, r1, norm1_w, dx1, emit_bf16=False), name="in_proj_bwd")
    d_hg, d_qn, d_kn = _fold_heads(d_hg, d_qn, d_kn, name="fold_heads")

    big = dict(w_in=dw_in, w_out=dw_out, w_g=dw_g, w_u=dw_u, w_down=dw_down)
    small = dict(norm1=d_norm1, norm2=d_norm2, final=d_final, att=d_att, hg=d_hg,
                 qn=d_qn[:, :ATT_DH], kn=d_kn[:, :ATT_DH], lb=jnp.concatenate([dlb_f, dlb_b], axis=0))
    return loss, grad_x, big, small, rode + rode_late, lb


def kernel(x, norm1_w, w_in, lb_logits, hg_norm_w, q_norm_w, k_norm_w, att_norm_w, w_out, norm2_w, w_gate_up, w_down, final_norm_w, loss_target, m_norm1_w, m_w_in, m_lb_logits, m_hg_norm_w, m_q_norm_w, m_k_norm_w, m_att_norm_w, m_w_out, m_norm2_w, m_w_gate_up, m_w_down, m_final_norm_w, v_norm1_w, v_w_in, v_lb_logits, v_hg_norm_w, v_q_norm_w, v_k_norm_w, v_att_norm_w, v_w_out, v_norm2_w, v_w_gate_up, v_w_down, v_final_norm_w):
    T = x.shape[1]
    me = 4 * lax.axis_index("x") + 2 * lax.axis_index("y") + lax.axis_index("c")
    c_in, r_out, c_gu, r_dn = w_in.shape[2], w_out.shape[1], w_gate_up.shape[2], w_down.shape[1]
    lb_cols = lb_logits.shape[2]

    shards = dict(w_in_t=w_in[0].T.astype(BF16), lb_logits=lb_logits.reshape(4, lb_cols),
                  w_gu_t=w_gate_up[0].T.astype(BF16), w_out=w_out[0].astype(BF16), w_down=w_down[0].astype(BF16))

    chips = N_DEV // 2
    core = lax.axis_index("c").astype(jnp.int32).reshape(1)
    by_owner = lambda g, r: g.reshape(chips, 2, r, D_MODEL)

    def pair_sums(mine, theirs, names):
        return [_pair_sum(g, o, core, name="pair_sum_" + nm) for g, o, nm in zip(mine, theirs, names)]

    def early_slabs(dw_g_t, dw_u_t, dw_down):
        half = lambda g: g.reshape(chips // 2, 2, c_gu, D_MODEL)
        return [half(dw_g_t), half(dw_u_t), by_owner(dw_down, r_dn)]

    def early_sums(mine, theirs, dw_out):
        s_out = by_owner(dw_out, r_out)
        c_out, c_g, c_u, c_dn = pair_sums([s_out] + mine, list(_core_swap([s_out], name="exchange_cores_out"))
                                          + list(theirs), ("w_out", "w_gate", "w_up", "w_down"))
        return [c_out, jnp.concatenate([c_g, c_u], axis=0), c_dn]

    def reduce_late(dw_in_t):
        mine = [by_owner(dw_in_t, c_in)]
        return pair_sums(mine, _core_swap(mine, name="exchange_cores_in"), ("w_in",))

    loss, grad_x, big, small, (p_out, p_gu, p_dn, p_in), lb = _local_step(
        x[0], loss_target[0], norm1_w, None, None, hg_norm_w, q_norm_w, k_norm_w, att_norm_w, None, norm2_w,
        None, None, None, final_norm_w, reduce_early=dict(slabs=early_slabs, sums=early_sums),
        reduce_late=reduce_late, shards=shards)
    p_gu, p_in = p_gu.transpose(0, 2, 1), p_in.transpose(0, 2, 1)

    packed = _pack_small(small["norm1"], small["norm2"], small["final"], small["att"], small["hg"],
                         small["qn"], small["kn"], small["lb"], loss)
    all_small = _send_to_all(packed, name="exchange_small")

    g_w_in, d_w_in, nm_w_in, nv_w_in = _adamw(p_in, w_in[0], m_w_in[0], v_w_in[0], name="adamw_w_in")
    g_w_out, d_w_out, nm_w_out, nv_w_out = _adamw(p_out, w_out[0], m_w_out[0], v_w_out[0], name="adamw_w_out")
    g_w_gu, d_w_gu, nm_w_gu, nv_w_gu = _adamw(p_gu, w_gate_up[0], m_w_gate_up[0], v_w_gate_up[0], name="adamw_w_gu")
    g_w_dn, d_w_dn, nm_w_dn, nv_w_dn = _adamw(p_dn, w_down[0], m_w_down[0], v_w_down[0], name="adamw_w_down")

    pk = lambda vecs: _pack_small(*vecs)
    w_pk = pk([norm1_w, norm2_w, final_norm_w, att_norm_w, hg_norm_w, q_norm_w, k_norm_w])
    m_pk = pk([m_norm1_w, m_norm2_w, m_final_norm_w, m_att_norm_w, m_hg_norm_w, m_q_norm_w, m_k_norm_w])
    v_pk = pk([v_norm1_w, v_norm2_w, v_final_norm_w, v_att_norm_w, v_hg_norm_w, v_q_norm_w, v_k_norm_w])
    g_pk, d_pk, nm_pk, nv_pk = _adamw(all_small, w_pk, m_pk, v_pk, name="adamw_small")

    dlb_sum = g_pk[5:6, :].reshape(2, HG_W)
    g_lb_full = _lb_grad(dlb_sum, lb, name="lb_grad")
    g_lb_mine = lax.dynamic_slice_in_dim(g_lb_full, me * lb_cols, lb_cols, axis=1)
    g_lb_s, d_lb, nm_lb, nv_lb = _adamw(g_lb_mine[None], lb_logits.reshape(4, lb_cols),
                                        m_lb_logits.reshape(4, lb_cols), v_lb_logits.reshape(4, lb_cols),
                                        name="adamw_lb")

    loss_total = g_pk[6, 0]

    def outs(big4, lb_arr, pk_arr):
        n1, n2, fin, att, hg, qn, kn = _unpack_small(pk_arr)
        b_in, b_out, b_gu, b_dn = big4
        return [n1, b_in[None], lb_arr.reshape(2, 2, lb_cols), hg, qn, kn, att, b_out[None], n2, b_gu[None],
                b_dn[None], fin]

    return (loss_total, grad_x[None],
            *outs((g_w_in, g_w_out, g_w_gu, g_w_dn), g_lb_s, g_pk),
            *outs((d_w_in, d_w_out, d_w_gu, d_w_dn), d_lb, d_pk),
            *outs((nm_w_in, nm_w_out, nm_w_gu, nm_w_dn), nm_lb, nm_pk),
            *outs((nv_w_in, nv_w_out, nv_w_gu, nv_w_dn), nv_lb, nv_pk))
```
